```python
import math
import jax
import jax.numpy as jnp
from jax import lax
import numpy as np

D_MODEL = 1024
BATCH = 32
SEQ = 256
DEPTH = 2
DEC_BATCH = 2
DEC_SEQ = 4096
PAST_LEN = 512

GRID_W = 64
D_C = D_MODEL // 2
H_C = 4
DV_C = D_C // H_C
DQK_C = DV_C // 2
D_A = D_MODEL // 4
H_A = 4
DK_A = D_A // H_A
DV_A = D_A // H_A
D_B = D_MODEL // 4
H_B = 4
P_B = D_B // H_B
N_B = 64
G_B = 2
D_CONV = 3
CONV_CH = D_B + 2 * G_B * N_B
D_MIX = D_C + D_A + D_B
SPLIT_SIZES = (D_C, D_C, D_C, D_A, D_A, D_A, D_A, 2 * H_A, 2 * H_A,
               D_B, D_B, G_B * N_B, G_B * N_B, 2 * H_B)
D_IN = sum(SPLIT_SIZES)
D_FF = 2816
N_EXPERTS = 8
TOP_K = 2
N_DENSE = (DEPTH + 1) // 2
N_MOE = DEPTH // 2
ALPHA = (2.0 * DEPTH) ** 0.25
BETA = (8.0 * DEPTH) ** -0.25
CHUNK = 64
Q_BLOCK = 128
ROPE_BASE = 10000.0
EPS = 1e-5

kernel_name = "hybrid_mlstm_ssd_diffattn_dit_step"

F32 = jnp.float32


def _layernorm(x, g, b):
    xf = x.astype(F32)
    mu = xf.mean(-1, keepdims=True)
    var = jnp.square(xf - mu).mean(-1, keepdims=True)
    return ((xf - mu) * lax.rsqrt(var + EPS)).astype(x.dtype) * g + b


def _rmsnorm(x, w):
    xf = x.astype(F32)
    y = xf * lax.rsqrt(jnp.mean(xf * xf, -1, keepdims=True) + EPS)
    return y.astype(x.dtype) * w


def _axial_rope_tables(n_tokens, dtype):
    n_rows = n_tokens // GRID_W
    rows = jnp.repeat(jnp.arange(n_rows, dtype=F32), GRID_W)
    cols = jnp.tile(jnp.arange(GRID_W, dtype=F32), n_rows)
    half = DQK_C // 2
    inv = ROPE_BASE ** (-jnp.arange(0, half, 2, dtype=F32) / half)
    ang_r = rows[:, None] * inv
    ang_c = cols[:, None] * inv
    ang = jnp.concatenate([ang_r, ang_r, ang_c, ang_c], -1)
    return jnp.cos(ang).astype(dtype), jnp.sin(ang).astype(dtype)


def _apply_rope(x, cos, sin):
    x1, x2, x3, x4 = jnp.split(x, 4, axis=-1)
    rot = jnp.concatenate([-x2, x1, -x4, x3], -1)
    return x * cos + rot * sin


def _diff_attention(q, k, v, lam, lam_init, norm_w):
    b, h, _, t, d = q.shape
    nb = t // Q_BLOCK
    qb = jnp.moveaxis(q.reshape(b, h, 2, nb, Q_BLOCK, d), 3, 0)
    scale = d ** -0.5

    def block(qi):
        s = jnp.einsum('bhmqd,bhmkd->bhmqk', qi, k).astype(F32) * scale
        p = jax.nn.softmax(s, axis=-1)
        a = p[:, :, 0] - lam * p[:, :, 1]
        o = jnp.einsum('bhqk,bhkd->bhqd', a.astype(v.dtype), v)
        return _rmsnorm(o, norm_w) * (1.0 - lam_init)

    o = lax.map(block, qb)
    o = jnp.moveaxis(o, 0, 2).reshape(b, h, t, -1)
    return o.transpose(0, 2, 1, 3).reshape(b, t, -1)


def _mlstm_chunkwise(q, k, v, logi, logf, c0, n0, m0):
    b, h, t, dk = q.shape
    nc = t // CHUNK

    def chunks(a):
        return jnp.moveaxis(a.reshape(b, h, nc, CHUNK, *a.shape[3:]), 2, 0)

    mask = jnp.tril(jnp.ones((CHUNK, CHUNK), dtype=bool))
    scale = dk ** -0.5

    def step(carry, xs):
        c, n, m = carry
        qc, kc, vc, li, lf = xs
        qc = qc.astype(F32) * scale
        kc = kc.astype(F32)
        vc = vc.astype(F32)
        bcum = jnp.cumsum(lf, axis=-1)
        dmat = jnp.where(mask, bcum[..., :, None] - bcum[..., None, :] + li[..., None, :], -jnp.inf)
        inter = bcum + m[..., None]
        m_t = jnp.maximum(inter, dmat.max(-1))
        w_inter = jnp.exp(inter - m_t)
        s = jnp.einsum('bhtd,bhsd->bhts', qc, kc) * jnp.exp(dmat - m_t[..., None])
        num = w_inter[..., None] * jnp.einsum('bhtd,bhde->bhte', qc, c) + jnp.einsum('bhts,bhse->bhte', s, vc)
        den = w_inter * jnp.einsum('bhtd,bhd->bht', qc, n) + s.sum(-1)
        hc = num / jnp.maximum(jnp.abs(den), jnp.exp(-m_t))[..., None]
        b_last = bcum[..., -1]
        g = b_last[..., None] - bcum + li
        m_new = jnp.maximum(b_last + m, g.max(-1))
        w_c = jnp.exp(b_last + m - m_new)
        w_s = jnp.exp(g - m_new[..., None])
        c_new = w_c[..., None, None] * c + jnp.einsum('bhs,bhsd,bhse->bhde', w_s, kc, vc)
        n_new = w_c[..., None] * n + jnp.einsum('bhs,bhsd->bhd', w_s, kc)
        return (c_new, n_new, m_new), hc

    init = (c0.astype(F32), n0.astype(F32), m0.astype(F32))
    (c, n, m), hs = lax.scan(step, init, (chunks(q), chunks(k), chunks(v), chunks(logi), chunks(logf)))
    hs = jnp.moveaxis(hs, 0, 2).reshape(b, h, t, -1)
    return hs, c, n, m


def _ssd_chunkwise(x, dt, a_coef, bmat, cmat, s0):
    b, t, h, p = x.shape
    nc = t // CHUNK
    rep = h // G_B
    bh = jnp.repeat(bmat, rep, axis=2)
    ch = jnp.repeat(cmat, rep, axis=2)

    def chunks(a):
        return jnp.moveaxis(a.reshape(b, nc, CHUNK, *a.shape[2:]), 1, 0)

    mask = jnp.tril(jnp.ones((CHUNK, CHUNK), dtype=bool))[None, :, :, None]

    def step(s, xs):
        xc, dtc, bc, cc = xs
        xc = xc.astype(F32)
        bc = bc.astype(F32)
        cc = cc.astype(F32)
        acum = jnp.cumsum(dtc * a_coef, axis=1)
        decay = jnp.exp(jnp.where(mask, acum[:, :, None, :] - acum[:, None, :, :], -jnp.inf))
        scores = jnp.einsum('bthn,bshn->btsh', cc, bc) * decay
        y = jnp.einsum('btsh,bsh,bshp->bthp', scores, dtc, xc)
        y = y + jnp.einsum('bthn,bhpn->bthp', cc, s) * jnp.exp(acum)[..., None]
        a_last = acum[:, -1]
        w = jnp.exp(a_last[:, None, :] - acum) * dtc
        s_new = jnp.exp(a_last)[..., None, None] * s + jnp.einsum('bsh,bshp,bshn->bhpn', w, xc, bc)
        return s_new, y

    s, ys = lax.scan(step, s0.astype(F32), (chunks(x), chunks(dt), chunks(bh), chunks(ch)))
    return jnp.moveaxis(ys, 0, 1).reshape(b, t, h, p), s


def _conv_centred(x, w, bias):
    t = x.shape[1]
    pad = D_CONV // 2
    xp = jnp.pad(x, ((0, 0), (pad, pad), (0, 0)))
    return sum(xp[:, j:j + t] * w[j] for j in range(D_CONV)) + bias


def _token_mixer(h, P, l, ctx):
    bsz, t, _ = h.shape
    dtype = h.dtype
    u = h @ P['w_in'][l]
    idx = [int(i) for i in np.cumsum(SPLIT_SIZES)[:-1]]
    aq, ak, av, mq, mk, mv, mo, mi, mf, sz, sx, sb, sc, sdt = jnp.split(u, idx, axis=-1)

    q = aq.reshape(bsz, t, H_C, 2, DQK_C).transpose(0, 2, 3, 1, 4)
    k = ak.reshape(bsz, t, H_C, 2, DQK_C).transpose(0, 2, 3, 1, 4)
    v = av.reshape(bsz, t, H_C, DV_C).transpose(0, 2, 1, 3)
    lam_p = P['attn_lambda'][l].astype(F32)
    lam_init = 0.8 - 0.6 * math.exp(-0.3 * l)
    lam = jnp.exp(jnp.sum(lam_p[0] * lam_p[1])) - jnp.exp(jnp.sum(lam_p[2] * lam_p[3])) + lam_init
    if ctx is None:
        k_all, v_all = k, v
    else:
        ck, cv, cC, cn, cm, cS = ctx
        cos, sin = _axial_rope_tables(t, dtype)
        q = _apply_rope(q, cos, sin)
        k_all = jnp.concatenate([_apply_rope(k, cos, sin), ck.astype(dtype)], axis=3)
        v_all = jnp.concatenate([v, cv.astype(dtype)], axis=2)
    att = _diff_attention(q, k_all, v_all, lam, lam_init, P['attn_norm_w'][l])

    def heads(a, n_h):
        return a.reshape(bsz, t, n_h, -1).transpose(0, 2, 1, 3)
    mq_, mk_, mv_ = heads(mq, H_A), heads(mk, H_A), heads(mv, H_A)
    gi = jnp.transpose((mi.reshape(bsz, t, 2, H_A) + P['mlstm_gate_b'][l, 0]).astype(F32), (2, 0, 3, 1))
    gf = jax.nn.log_sigmoid(jnp.transpose((mf.reshape(bsz, t, 2, H_A) + P['mlstm_gate_b'][l, 1]).astype(F32), (2, 0, 3, 1)))
    if ctx is None:
        c0 = jnp.zeros((bsz, 2, H_A, DK_A, DV_A), F32)
        n0 = jnp.zeros((bsz, 2, H_A, DK_A), F32)
        m0 = jnp.zeros((bsz, 2, H_A), F32)
        s0 = jnp.zeros((bsz, 2, H_B, P_B, N_B), F32)
    else:
        c0, n0, m0, s0 = cC, cn, cm, cS
    hf, cf, nf, mf_ = _mlstm_chunkwise(mq_, mk_, mv_, gi[0], gf[0], c0[:, 0], n0[:, 0], m0[:, 0])
    hb, cb, nb_, mb_ = _mlstm_chunkwise(mq_[:, :, ::-1], mk_[:, :, ::-1], mv_[:, :, ::-1],
                                        gi[1][..., ::-1], gf[1][..., ::-1], c0[:, 1], n0[:, 1], m0[:, 1])
    htil = hf + hb[:, :, ::-1]
    mu = htil.mean(-1, keepdims=True)
    var = jnp.square(htil - mu).mean(-1, keepdims=True)
    hn = ((htil - mu) * lax.rsqrt(var + EPS)).transpose(0, 2, 1, 3).reshape(bsz, t, D_A)
    ml = (jax.nn.sigmoid(mo.astype(F32)) * hn.astype(dtype) * P['mlstm_norm_w'][l]).astype(dtype)

    xbc = jax.nn.silu(_conv_centred(jnp.concatenate([sx, sb, sc], -1), P['conv_w'][l], P['conv_b'][l]))
    sx_, sb_, sc_ = jnp.split(xbc, [D_B, D_B + G_B * N_B], axis=-1)
    xh = sx_.reshape(bsz, t, H_B, P_B)
    bg = sb_.reshape(bsz, t, G_B, N_B)
    cg = sc_.reshape(bsz, t, G_B, N_B)
    dt = jax.nn.softplus((sdt.reshape(bsz, t, 2, H_B) + P['ssm_dt_bias'][l]).astype(F32))
    a_coef = -jnp.exp(P['ssm_A_log'][l].astype(F32))
    yf, sf = _ssd_chunkwise(xh, dt[:, :, 0], a_coef[0], bg, cg, s0[:, 0])
    yb, sbk = _ssd_chunkwise(xh[:, ::-1], dt[:, ::-1, 1], a_coef[1], bg[:, ::-1], cg[:, ::-1], s0[:, 1])
    y = yf + yb[:, ::-1] + P['ssm_D'][l].astype(F32)[:, None] * xh.astype(F32)
    y = (y.reshape(bsz, t, D_B) * jax.nn.silu(sz.astype(F32))).reshape(bsz, t, G_B, D_B // G_B)
    y = y * lax.rsqrt(jnp.mean(y * y, -1, keepdims=True) + EPS)
    ssm = y.reshape(bsz, t, D_B).astype(dtype) * P['ssm_norm_w'][l]

    mixed = jnp.concatenate([att.astype(dtype), ml, ssm.astype(dtype)], -1) @ P['w_out'][l]
    if ctx is None:
        ctx_out = (k, v, jnp.stack([cf, cb], 1), jnp.stack([nf, nb_], 1),
                   jnp.stack([mf_, mb_], 1), jnp.stack([sf, sbk], 1))
        return mixed, ctx_out
    return mixed, None


def _swiglu(h, w1, w3, w2):
    return (jax.nn.silu(h @ w1) * (h @ w3)) @ w2


def _moe(h, router_w, w1, w3, w2):
    b, t, d = h.shape
    hf = h.reshape(b * t, d)
    probs = jax.nn.softmax((hf @ router_w).astype(F32), axis=-1)
    top_p, top_i = lax.top_k(probs, TOP_K)
    top_p = top_p / top_p.sum(-1, keepdims=True)
    gates = jnp.einsum('nk,nke->ne', top_p, jax.nn.one_hot(top_i, N_EXPERTS, dtype=F32))
    out = sum(gates[:, e:e + 1] * _swiglu(hf, w1[e], w3[e], w2[e]).astype(F32) for e in range(N_EXPERTS))
    return out.astype(h.dtype).reshape(b, t, d)


def _layer(x, mod, P, l, ctx):
    sh1, sc1, g1, sh2, sc2, g2 = jnp.split(mod.astype(x.dtype), 6, axis=-1)
    mixed, ctx_out = _token_mixer(x * (1 + sc1) + sh1, P, l, ctx)
    x = _layernorm(ALPHA * x + g1 * mixed, P['ln_g'][l, 0], P['ln_b'][l, 0])
    h = x * (1 + sc2) + sh2
    if l % 2 == 0:
        f = _swiglu(h, P['ffn_w1'][l // 2], P['ffn_w3'][l // 2], P['ffn_w2'][l // 2])
    else:
        f = _moe(h, P['router_w'][l // 2], P['moe_w1'][l // 2], P['moe_w3'][l // 2], P['moe_w2'][l // 2])
    x = _layernorm(ALPHA * x + g2 * f, P['ln_g'][l, 1], P['ln_b'][l, 1])
    return x, ctx_out


def setup_inputs(seed: int = 0) -> dict:
    key = jax.random.key(seed)
    ks = iter(jax.random.split(key, 48))

    def nrm(shape, s=1.0):
        return s * jax.random.normal(next(ks), shape, F32)

    def uni(shape, lo, hi):
        return jax.random.uniform(next(ks), shape, F32, minval=lo, maxval=hi)

    dt0 = jnp.exp(uni((DEPTH, 2, H_B), 0.0, 1.0) * (math.log(0.1) - math.log(1e-3)) + math.log(1e-3))
    return {
        'x_prompt': nrm((BATCH, SEQ, D_MODEL)),
        'x_sample': nrm((DEC_BATCH, DEC_SEQ, D_MODEL)),
        'c': nrm((DEC_BATCH, D_MODEL)),
        'cache_attn_k': nrm((DEC_BATCH, DEPTH, H_C, 2, PAST_LEN, DQK_C)),
        'cache_attn_v': nrm((DEC_BATCH, DEPTH, H_C, PAST_LEN, DV_C)),
        'state_mlstm_C': nrm((DEC_BATCH, DEPTH, 2, H_A, DK_A, DV_A), 0.5),
        'state_mlstm_n': nrm((DEC_BATCH, DEPTH, 2, H_A, DK_A)),
        'state_mlstm_m': nrm((DEC_BATCH, DEPTH, 2, H_A)),
        'state_ssm': nrm((DEC_BATCH, DEPTH, 2, H_B, P_B, N_B), 0.5),
        'c_ctx': nrm((D_MODEL,)),
        'w_ada': nrm((DEPTH, D_MODEL, 6 * D_MODEL), 0.5 * D_MODEL ** -0.5),
        'b_ada': nrm((DEPTH, 6 * D_MODEL), 0.02),
        'w_in': nrm((DEPTH, D_MODEL, D_IN), D_MODEL ** -0.5),
        'w_out': nrm((DEPTH, D_MIX, D_MODEL), BETA * D_MIX ** -0.5),
        'attn_lambda': nrm((DEPTH, 4, DQK_C), 0.1),
        'attn_norm_w': 1.0 + nrm((DEPTH, DV_C), 0.02),
        'mlstm_gate_b': jnp.stack([nrm((DEPTH, 2, H_A), 0.1), uni((DEPTH, 2, H_A), 3.0, 6.0)], axis=1),
        'mlstm_norm_w': 1.0 + nrm((DEPTH, D_A), 0.02),
        'conv_w': nrm((DEPTH, D_CONV, CONV_CH), D_CONV ** -0.5),
        'conv_b': nrm((DEPTH, CONV_CH), 0.02),
        'ssm_A_log': jnp.log(uni((DEPTH, 2, H_B), 1.0, 16.0)),
        'ssm_dt_bias': dt0 + jnp.log(-jnp.expm1(-dt0)),
        'ssm_D': 1.0 + nrm((DEPTH, H_B), 0.1),
        'ssm_norm_w': 1.0 + nrm((DEPTH, D_B), 0.02),
        'ln_g': 1.0 + nrm((DEPTH, 2, D_MODEL), 0.02),
        'ln_b': nrm((DEPTH, 2, D_MODEL), 0.02),
        'ffn_w1': nrm((N_DENSE, D_MODEL, D_FF), D_MODEL ** -0.5),
        'ffn_w3': nrm((N_DENSE, D_MODEL, D_FF), D_MODEL ** -0.5),
        'ffn_w2': nrm((N_DENSE, D_FF, D_MODEL), BETA * D_FF ** -0.5),
        'router_w': nrm((N_MOE, D_MODEL, N_EXPERTS), D_MODEL ** -0.5),
        'moe_w1': nrm((N_MOE, N_EXPERTS, D_MODEL, D_FF), D_MODEL ** -0.5),
        'moe_w3': nrm((N_MOE, N_EXPERTS, D_MODEL, D_FF), D_MODEL ** -0.5),
        'moe_w2': nrm((N_MOE, N_EXPERTS, D_FF, D_MODEL), BETA * D_FF ** -0.5),
    }


def reference(x_prompt, x_sample, c, cache_attn_k, cache_attn_v, state_mlstm_C, state_mlstm_n,
              state_mlstm_m, state_ssm, c_ctx, w_ada, b_ada, w_in, w_out, attn_lambda, attn_norm_w,
              mlstm_gate_b, mlstm_norm_w, conv_w, conv_b, ssm_A_log, ssm_dt_bias, ssm_D, ssm_norm_w,
              ln_g, ln_b, ffn_w1, ffn_w3, ffn_w2, router_w, moe_w1, moe_w3, moe_w2):
    P = dict(w_in=w_in, w_out=w_out, attn_lambda=attn_lambda, attn_norm_w=attn_norm_w,
             mlstm_gate_b=mlstm_gate_b, mlstm_norm_w=mlstm_norm_w, conv_w=conv_w, conv_b=conv_b,
             ssm_A_log=ssm_A_log, ssm_dt_bias=ssm_dt_bias, ssm_D=ssm_D, ssm_norm_w=ssm_norm_w,
             ln_g=ln_g, ln_b=ln_b, ffn_w1=ffn_w1, ffn_w3=ffn_w3, ffn_w2=ffn_w2,
             router_w=router_w, moe_w1=moe_w1, moe_w3=moe_w3, moe_w2=moe_w2)

    y_prompt = x_prompt
    ks, vs, cs, ns, ms, ss = [], [], [], [], [], []
    for l in range(DEPTH):
        mod = (jax.nn.silu(c_ctx) @ w_ada[l] + b_ada[l])[None, None, :]
        y_prompt, (k_l, v_l, c_l, n_l, m_l, s_l) = _layer(y_prompt, mod, P, l, None)
        ks.append(k_l)
        vs.append(v_l)
        cs.append(c_l)
        ns.append(n_l)
        ms.append(m_l)
        ss.append(s_l)

    y_sample = x_sample
    for l in range(DEPTH):
        mod = (jax.nn.silu(c) @ w_ada[l] + b_ada[l])[:, None, :]
        ctx = (cache_attn_k[:, l], cache_attn_v[:, l], state_mlstm_C[:, l], state_mlstm_n[:, l],
               state_mlstm_m[:, l], state_ssm[:, l])
        y_sample, _ = _layer(y_sample, mod, P, l, ctx)

    new_attn_k = jnp.stack(ks, axis=1)
    new_attn_v = jnp.stack(vs, axis=1)
    new_mlstm_C = jnp.stack(cs, axis=1)
    new_mlstm_n = jnp.stack(ns, axis=1)
    new_mlstm_m = jnp.stack(ms, axis=1)
    new_ssm = jnp.stack(ss, axis=1)
    return (y_prompt, y_sample, new_attn_k, new_attn_v, new_mlstm_C, new_mlstm_n, new_mlstm_m, new_ssm)
```

```python
import functools
import math

import jax
import jax.numpy as jnp
from jax import lax
from jax.experimental import pallas as pl
from jax.experimental.pallas import tpu as pltpu

F32 = jnp.float32
BF16 = jnp.bfloat16
HIGHEST = lax.Precision.HIGHEST

D_MODEL = 1024
DEPTH = 2
GRID_W = 64
N_HEADS = 4
D_ATT = 512
D_HEAD_V = 128
D_QK = 64
D_REC = 256
D_STATE = 64
N_GROUPS = 2
D_CONV = 3
D_FF = 2816
N_EXPERTS = 8
ALPHA = (2.0 * DEPTH) ** 0.25
CHUNK = 64
ROPE_BASE = 10000.0
EPS = 1e-5

COL_AQ, COL_AK, COL_AV = 0, 512, 1024
COL_MQ, COL_MK, COL_MV, COL_MO = 1536, 1792, 2048, 2304
COL_SX, COL_SBC, COL_SZ = 2560, 2816, 3072
COL_SMALL = 3328
U_COLS = 3584
ORIG_GATES, ORIG_SZ, ORIG_SX, ORIG_DT, ORIG_END = 2560, 2576, 2832, 3344, 3352
GATE_I, GATE_F, GATE_DT = 0, 8, 16

LANES = 128
ROW_TILE = 256
VMEM_LIMIT = 48 * 1024 * 1024

NT_DIMS = (((1,), (1,)), ((), ()))
TN_DIMS = (((0,), (0,)), ((), ()))


def _params(sem, vmem=VMEM_LIMIT):
    return pltpu.CompilerParams(dimension_semantics=sem, vmem_limit_bytes=vmem)


def _silu(x):
    return x * jax.nn.sigmoid(x)


def _bdot(a, b):
    return jnp.dot(a.astype(BF16), b.astype(BF16), preferred_element_type=F32)


def _bdot_nt(a, b):
    return lax.dot_general(a.astype(BF16), b.astype(BF16), NT_DIMS, preferred_element_type=F32)


def _bdot_tn(a, b):
    return lax.dot_general(a.astype(BF16), b.astype(BF16), TN_DIMS, preferred_element_type=F32)


def _layernorm_rows(y, g, b):
    mu = jnp.mean(y, -1, keepdims=True)
    d = y - mu
    var = jnp.mean(d * d, -1, keepdims=True)
    return d * lax.rsqrt(var + EPS) * g + b


def _mod_kernel(c_ref, w_ref, b_ref, o_ref):
    o_ref[0] = jnp.dot(_silu(c_ref[...]), w_ref[0], precision=HIGHEST,
                       preferred_element_type=F32) + b_ref[0]


def _modulation(cvec, w_ada, b_ada):
    tn = 1536
    return pl.pallas_call(
        _mod_kernel,
        grid=(DEPTH, 6 * D_MODEL // tn),
        in_specs=[pl.BlockSpec((8, D_MODEL), lambda l, j: (0, 0)),
                  pl.BlockSpec((1, D_MODEL, tn), lambda l, j: (l, 0, j)),
                  pl.BlockSpec((1, 1, tn), lambda l, j: (l, 0, j))],
        out_specs=pl.BlockSpec((1, 8, tn), lambda l, j: (l, 0, j)),
        out_shape=jax.ShapeDtypeStruct((DEPTH, 8, 6 * D_MODEL), F32),
        compiler_params=_params(("parallel", "parallel")),
        name="modulation",
    )(cvec, w_ada, b_ada.reshape(DEPTH, 1, 6 * D_MODEL))


def _inproj_kernel(x_ref, sc_ref, sh_ref, w_ref, o_ref):
    h = (x_ref[...] * (1.0 + sc_ref[0]) + sh_ref[0]).astype(BF16)
    for n0 in range(0, U_COLS, 512):
        o_ref[:, n0:n0 + 512] = jnp.dot(h, w_ref[:, n0:n0 + 512], preferred_element_type=F32)


def _in_proj(x, sc, sh, w, rows_per_mod):
    n = x.shape[0]
    tpb = rows_per_mod // ROW_TILE
    return pl.pallas_call(
        _inproj_kernel,
        grid=(n // ROW_TILE,),
        in_specs=[pl.BlockSpec((ROW_TILE, D_MODEL), lambda i: (i, 0)),
                  pl.BlockSpec((1, 1, D_MODEL), lambda i: (i // tpb, 0, 0)),
                  pl.BlockSpec((1, 1, D_MODEL), lambda i: (i // tpb, 0, 0)),
                  pl.BlockSpec((D_MODEL, U_COLS), lambda i: (0, 0))],
        out_specs=pl.BlockSpec((ROW_TILE, U_COLS), lambda i: (i, 0)),
        out_shape=jax.ShapeDtypeStruct((n, U_COLS), F32),
        compiler_params=_params(("parallel",)),
        name="in_proj",
    )(x, sc, sh, w)


def _lambda_scalar(lam_ref, lam_init):
    lp = lam_ref[...]
    s01 = jnp.sum(lp[0:1] * lp[1:2], axis=-1, keepdims=True)
    s23 = jnp.sum(lp[2:3] * lp[3:4], axis=-1, keepdims=True)
    return jnp.exp(s01) - jnp.exp(s23) + lam_init


def _softmax_rows(s):
    e = jnp.exp(s - jnp.max(s, -1, keepdims=True))
    return e / jnp.sum(e, -1, keepdims=True)


def _head_norm(o, nw, lam_init):
    return o * lax.rsqrt(jnp.mean(o * o, -1, keepdims=True) + EPS) * nw * (1.0 - lam_init)


def _attn_ctx_kernel(u_ref, lam_ref, nw_ref, att_ref, k_ref, v_ref, *, lam_init):
    lam = _lambda_scalar(lam_ref, lam_init)
    for h in range(N_HEADS):
        v = u_ref[:, COL_AV + h * D_HEAD_V:COL_AV + (h + 1) * D_HEAD_V]
        v_ref[0, h] = v
        ps = []
        for m in range(2):
            c0 = h * D_HEAD_V + m * D_QK
            q = u_ref[:, COL_AQ + c0:COL_AQ + c0 + D_QK] * (D_QK ** -0.5)
            k = u_ref[:, COL_AK + c0:COL_AK + c0 + D_QK]
            k_ref[0, h, m] = k
            ps.append(_softmax_rows(_bdot_nt(q, k)))
        o = _bdot(ps[0] - lam * ps[1], v)
        att_ref[:, h * D_HEAD_V:(h + 1) * D_HEAD_V] = _head_norm(o, nw_ref[...], lam_init).astype(BF16)


def _attention_ctx(u, lam_p, norm_w, lam_init, bsz, t):
    n = bsz * t
    return pl.pallas_call(
        functools.partial(_attn_ctx_kernel, lam_init=lam_init),
        grid=(bsz,),
        in_specs=[pl.BlockSpec((t, 3 * D_ATT), lambda b: (b, 0)),
                  pl.BlockSpec((4, D_QK), lambda b: (0, 0)),
                  pl.BlockSpec((1, D_HEAD_V), lambda b: (0, 0))],
        out_specs=[pl.BlockSpec((t, D_ATT), lambda b: (b, 0)),
                   pl.BlockSpec((1, N_HEADS, 2, t, D_QK), lambda b: (b, 0, 0, 0, 0)),
                   pl.BlockSpec((1, N_HEADS, t, D_HEAD_V), lambda b: (b, 0, 0, 0))],
        out_shape=[jax.ShapeDtypeStruct((n, D_ATT), BF16),
                   jax.ShapeDtypeStruct((bsz, N_HEADS, 2, t, D_QK), F32),
                   jax.ShapeDtypeStruct((bsz, N_HEADS, t, D_HEAD_V), F32)],
        compiler_params=_params(("parallel",)),
        name="attn_ctx",
    )(u, lam_p, norm_w.reshape(1, D_HEAD_V))


def _rope_kernel(u_ref, cos_ref, sa_ref, sb_ref, q_ref, k_ref, v_ref):
    cos, sa, sb = cos_ref[...], sa_ref[...], sb_ref[...]

    def rope(x):
        return x * cos + pltpu.roll(x, LANES - 16, 1) * sa + pltpu.roll(x, 16, 1) * sb

    for h in range(N_HEADS):
        q = rope(u_ref[:, COL_AQ + h * D_HEAD_V:COL_AQ + (h + 1) * D_HEAD_V]) * (D_QK ** -0.5)
        k = rope(u_ref[:, COL_AK + h * D_HEAD_V:COL_AK + (h + 1) * D_HEAD_V])
        for m in range(2):
            q_ref[0, h, m] = q[:, m * D_QK:(m + 1) * D_QK].astype(BF16)
            k_ref[0, h, m] = k[:, m * D_QK:(m + 1) * D_QK].astype(BF16)
        v_ref[0, h] = u_ref[:, COL_AV + h * D_HEAD_V:COL_AV + (h + 1) * D_HEAD_V].astype(BF16)


def _rope_tables(t):
    rows = jnp.repeat(jnp.arange(t // GRID_W, dtype=F32), GRID_W)
    cols = jnp.tile(jnp.arange(GRID_W, dtype=F32), t // GRID_W)
    half = D_QK // 2
    inv = ROPE_BASE ** (-jnp.arange(0, half, 2, dtype=F32) / half)
    ang_r = rows[:, None] * inv
    ang_c = cols[:, None] * inv
    ang = jnp.concatenate([ang_r, ang_r, ang_c, ang_c], -1)
    cos, sin = jnp.cos(ang), jnp.sin(ang)
    quarter = (jnp.arange(D_QK) // (D_QK // 4)) % 2
    sa = jnp.where(quarter == 0, -sin, 0.0)
    sb = jnp.where(quarter == 1, sin, 0.0)
    tile2 = lambda a: jnp.concatenate([a, a], -1)
    return tile2(cos), tile2(sa), tile2(sb)


def _rope_prep(u, bsz, t):
    tr = 512
    nb = t // tr
    cos, sa, sb = _rope_tables(t)
    tab = pl.BlockSpec((tr, LANES), lambda b, i: (i, 0))
    return pl.pallas_call(
        _rope_kernel,
        grid=(bsz, nb),
        in_specs=[pl.BlockSpec((tr, 3 * D_ATT), lambda b, i: (b * nb + i, 0)), tab, tab, tab],
        out_specs=[pl.BlockSpec((1, N_HEADS, 2, tr, D_QK), lambda b, i: (b, 0, 0, i, 0)),
                   pl.BlockSpec((1, N_HEADS, 2, tr, D_QK), lambda b, i: (b, 0, 0, i, 0)),
                   pl.BlockSpec((1, N_HEADS, tr, D_HEAD_V), lambda b, i: (b, 0, i, 0))],
        out_shape=[jax.ShapeDtypeStruct((bsz, N_HEADS, 2, t, D_QK), BF16),
                   jax.ShapeDtypeStruct((bsz, N_HEADS, 2, t, D_QK), BF16),
                   jax.ShapeDtypeStruct((bsz, N_HEADS, t, D_HEAD_V), BF16)],
        compiler_params=_params(("parallel", "parallel")),
        name="rope_prep",
    )(u, cos, sa, sb)


def _attn_lat_kernel(q_ref, k_ref, v_ref, lam_ref, nw_ref, o_ref, *, lam_init):
    lam = _lambda_scalar(lam_ref, lam_init)
    ps = [_softmax_rows(lax.dot_general(q_ref[0, 0, m], k_ref[0, 0, m], NT_DIMS,
                                        preferred_element_type=F32)) for m in range(2)]
    o = _bdot(ps[0] - lam * ps[1], v_ref[0, 0])
    o_ref[...] = _head_norm(o, nw_ref[...], lam_init).astype(BF16)


def _attention_lat(q, k_all, v_all, lam_p, norm_w, lam_init, bsz, t):
    tq = 128
    nq = t // tq
    s = k_all.shape[3]
    return pl.pallas_call(
        functools.partial(_attn_lat_kernel, lam_init=lam_init),
        grid=(bsz, N_HEADS, nq),
        in_specs=[pl.BlockSpec((1, 1, 2, tq, D_QK), lambda b, h, i: (b, h, 0, i, 0)),
                  pl.BlockSpec((1, 1, 2, s, D_QK), lambda b, h, i: (b, h, 0, 0, 0)),
                  pl.BlockSpec((1, 1, s, D_HEAD_V), lambda b, h, i: (b, h, 0, 0)),
                  pl.BlockSpec((4, D_QK), lambda b, h, i: (0, 0)),
                  pl.BlockSpec((1, D_HEAD_V), lambda b, h, i: (0, 0))],
        out_specs=pl.BlockSpec((tq, D_HEAD_V), lambda b, h, i: (b * nq + i, h)),
        out_shape=jax.ShapeDtypeStruct((bsz * t, D_ATT), BF16),
        compiler_params=_params(("parallel", "parallel", "parallel")),
        name="attn_lat",
    )(q, k_all, v_all, lam_p, norm_w.reshape(1, D_HEAD_V))


def _tri_masks():
    r = lax.broadcasted_iota(jnp.int32, (CHUNK, CHUNK), 0)
    c = lax.broadcasted_iota(jnp.int32, (CHUNK, CHUNK), 1)
    return (c <= r), (c >= r)


def _cumsums(x, mask):
    tri = mask.astype(F32)
    col = jnp.dot(tri, x, precision=HIGHEST, preferred_element_type=F32)
    row = lax.dot_general(x.T, tri, NT_DIMS, precision=HIGHEST, preferred_element_type=F32)
    return col, row


def _mlstm_kernel(*refs, nblk, rows, zero_init):
    if zero_init:
        (qf_ref, kf_ref, vf_ref, gf_ref, qb_ref, kb_ref, vb_ref, gb_ref, bias_ref,
         h_ref, cout_ref, mout_ref, c_scr, m_scr) = refs
    else:
        (qf_ref, kf_ref, vf_ref, gf_ref, qb_ref, kb_ref, vb_ref, gb_ref, bias_ref, c0_ref, m0_ref,
         h_ref, cout_ref, mout_ref, c_scr, m_scr) = refs
    j = pl.program_id(1)
    nchunk = rows // CHUNK

    @pl.when(j == 0)
    def _():
        h_ref[...] = jnp.zeros_like(h_ref)
        if zero_init:
            c_scr[...] = jnp.zeros_like(c_scr)
            m_scr[...] = jnp.zeros_like(m_scr)
        else:
            c_scr[...] = c0_ref[0]
            m_scr[...] = m0_ref[0]

    masks = _tri_masks()
    lane = lax.broadcasted_iota(jnp.int32, (CHUNK, LANES - D_STATE), 1)
    ones_pad = (lane == 0).astype(F32)

    def chunk(ci, carry):
        for d in range(2):
            q_ref, k_ref, v_ref, g_ref = ((qf_ref, kf_ref, vf_ref, gf_ref) if d == 0
                                          else (qb_ref, kb_ref, vb_ref, gb_ref))
            cj = ci if d == 0 else nchunk - 1 - ci
            blk = j if d == 0 else nblk - 1 - j
            r0 = pl.multiple_of(cj * CHUNK, CHUNK)
            out0 = pl.multiple_of(blk * rows + cj * CHUNK, CHUNK)
            g = g_ref[pl.ds(r0, CHUNK), :] + bias_ref[...]
            lf = jax.nn.log_sigmoid(g)
            bcol, brow = _cumsums(lf, masks[d])
            btot = jnp.sum(lf, axis=0, keepdims=True)
            g_t = g.T
            for h in range(N_HEADS):
                ch_i, ch_f = GATE_I + d * N_HEADS + h, GATE_F + d * N_HEADS + h
                hs = slice(h * D_STATE, (h + 1) * D_STATE)
                q = q_ref[pl.ds(r0, CHUNK), hs] * (D_STATE ** -0.5)
                k = k_ref[pl.ds(r0, CHUNK), hs]
                v = v_ref[pl.ds(r0, CHUNK), hs]
                vext = jnp.concatenate([v, ones_pad], axis=1)
                b_c = bcol[:, ch_f:ch_f + 1]
                b_r = brow[ch_f:ch_f + 1, :]
                li_c = g[:, ch_i:ch_i + 1]
                li_r = g_t[ch_i:ch_i + 1, :]
                m_prev = m_scr[d, h][:, 0:1]
                c_prev = c_scr[d, h]
                dm = jnp.where(masks[d], b_c - b_r + li_r, -jnp.inf)
                inter = b_c + m_prev
                m_t = jnp.maximum(inter, jnp.max(dm, -1, keepdims=True))
                w_inter = jnp.exp(inter - m_t)
                s = _bdot_nt(q, k) * jnp.exp(dm - m_t)
                nd = w_inter * _bdot(q, c_prev) + _bdot(s, vext)
                den = nd[:, D_STATE:D_STATE + 1]
                hc = nd[:, :D_STATE] / jnp.maximum(jnp.abs(den), jnp.exp(-m_t))
                h_ref[pl.ds(out0, CHUNK), hs] += hc
                b_last = btot[:, ch_f:ch_f + 1]
                gcol = b_last - b_c + li_c
                m_new = jnp.maximum(b_last + m_prev, jnp.max(gcol, 0, keepdims=True))
                w_c = jnp.exp(b_last + m_prev - m_new)
                w_s = jnp.exp(gcol - m_new)
                c_scr[d, h] = w_c * c_prev + _bdot_tn(k * w_s, vext)
                m_scr[d, h] = jnp.broadcast_to(m_new, (1, LANES))
        return carry

    lax.fori_loop(0, nchunk, chunk, 0)

    @pl.when(j == nblk - 1)
    def _():
        cout_ref[0] = c_scr[...]
        mout_ref[0] = m_scr[...]


def _mlstm(u, gate_bias, bsz, t, init):
    rows = min(t, ROW_TILE)
    nblk = t // rows
    zero_init = init is None

    def fwd(col):
        return lambda b, j: (b * nblk + j, col)

    def bwd(col):
        return lambda b, j: (b * nblk + nblk - 1 - j, col)

    cq, ck, cv, cg = COL_MQ // D_REC, COL_MK // D_REC, COL_MV // D_REC, COL_SMALL // LANES
    in_specs = []
    for mk in (fwd, bwd):
        in_specs += [pl.BlockSpec((rows, D_REC), mk(cq)), pl.BlockSpec((rows, D_REC), mk(ck)),
                     pl.BlockSpec((rows, D_REC), mk(cv)), pl.BlockSpec((rows, LANES), mk(cg))]
    in_specs.append(pl.BlockSpec((1, LANES), lambda b, j: (0, 0)))
    args = [u] * 8 + [gate_bias]
    state_c = pl.BlockSpec((1, 2, N_HEADS, D_STATE, LANES), lambda b, j: (b, 0, 0, 0, 0))
    state_m = pl.BlockSpec((1, 2, N_HEADS, 1, LANES), lambda b, j: (b, 0, 0, 0, 0))
    if not zero_init:
        in_specs += [state_c, state_m]
        args += list(init)
    return pl.pallas_call(
        functools.partial(_mlstm_kernel, nblk=nblk, rows=rows, zero_init=zero_init),
        grid=(bsz, nblk),
        in_specs=in_specs,
        out_specs=[pl.BlockSpec((t, D_REC), lambda b, j: (b, 0)), state_c, state_m],
        out_shape=[jax.ShapeDtypeStruct((bsz * t, D_REC), F32),
                   jax.ShapeDtypeStruct((bsz, 2, N_HEADS, D_STATE, LANES), F32),
                   jax.ShapeDtypeStruct((bsz, 2, N_HEADS, 1, LANES), F32)],
        scratch_shapes=[pltpu.VMEM((2, N_HEADS, D_STATE, LANES), F32),
                        pltpu.VMEM((2, N_HEADS, 1, LANES), F32)],
        compiler_params=_params(("parallel", "arbitrary")),
        name="mlstm_scan",
    )(*args)


def _conv_kernel(x_ref, prev_ref, next_ref, w_ref, b_ref, o_ref, *, nblk):
    i = pl.program_id(1)
    x = x_ref[...]
    rows = x.shape[0]
    r = lax.broadcasted_iota(jnp.int32, x.shape, 0)
    prev_row = prev_ref[7:8, :] * (i > 0).astype(F32)
    next_row = next_ref[0:1, :] * (i < nblk - 1).astype(F32)
    xm = jnp.where(r == 0, prev_row, pltpu.roll(x, 1, 0))
    xp = jnp.where(r == rows - 1, next_row, pltpu.roll(x, rows - 1, 0))
    w = w_ref[...]
    o_ref[...] = _silu(xm * w[0:1] + x * w[1:2] + xp * w[2:3] + b_ref[...])


def _ssd_conv(u, conv_w, conv_b, bsz, t):
    rows = min(t, 512)
    nblk = t // rows
    r8 = rows // 8
    width = 2 * D_REC
    c0 = COL_SX // width
    return pl.pallas_call(
        functools.partial(_conv_kernel, nblk=nblk),
        grid=(bsz, nblk),
        in_specs=[pl.BlockSpec((rows, width), lambda b, i: (b * nblk + i, c0)),
                  pl.BlockSpec((8, width), lambda b, i: (jnp.maximum((b * nblk + i) * r8 - 1, 0), c0)),
                  pl.BlockSpec((8, width), lambda b, i: (jnp.minimum((b * nblk + i + 1) * r8,
                                                                     bsz * nblk * r8 - 1), c0)),
                  pl.BlockSpec((D_CONV, width), lambda b, i: (0, 0)),
                  pl.BlockSpec((1, width), lambda b, i: (0, 0))],
        out_specs=pl.BlockSpec((rows, width), lambda b, i: (b * nblk + i, 0)),
        out_shape=jax.ShapeDtypeStruct((bsz * t, width), F32),
        compiler_params=_params(("parallel", "parallel")),
        name="ssd_conv",
    )(u, u, u, conv_w, conv_b.reshape(1, width))


def _ssd_kernel(*refs, nblk, rows, zero_init):
    if zero_init:
        (xf_ref, bcf_ref, gf_ref, xb_ref, bcb_ref, gb_ref, dtb_ref, alog_ref, dskip_ref,
         y_ref, sout_ref, s_scr) = refs
    else:
        (xf_ref, bcf_ref, gf_ref, xb_ref, bcb_ref, gb_ref, dtb_ref, alog_ref, dskip_ref, s0_ref,
         y_ref, sout_ref, s_scr) = refs
    j = pl.program_id(1)
    nchunk = rows // CHUNK

    @pl.when(j == 0)
    def _():
        y_ref[...] = jnp.zeros_like(y_ref)
        if zero_init:
            s_scr[...] = jnp.zeros_like(s_scr)
        else:
            s_scr[...] = s0_ref[0]

    masks = _tri_masks()
    a_coef = -jnp.exp(alog_ref[...])

    def chunk(ci, carry):
        for d in range(2):
            x_ref, bc_ref, g_ref = (xf_ref, bcf_ref, gf_ref) if d == 0 else (xb_ref, bcb_ref, gb_ref)
            cj = ci if d == 0 else nchunk - 1 - ci
            blk = j if d == 0 else nblk - 1 - j
            r0 = pl.multiple_of(cj * CHUNK, CHUNK)
            out0 = pl.multiple_of(blk * rows + cj * CHUNK, CHUNK)
            dt = jax.nn.softplus(g_ref[pl.ds(r0, CHUNK), :] + dtb_ref[...])
            da = dt * a_coef
            acol, arow = _cumsums(da, masks[d])
            atot = jnp.sum(da, axis=0, keepdims=True)
            for grp in range(N_GROUPS):
                bmat = bc_ref[pl.ds(r0, CHUNK), grp * D_STATE:(grp + 1) * D_STATE]
                cmat = bc_ref[pl.ds(r0, CHUNK), LANES + grp * D_STATE:LANES + (grp + 1) * D_STATE]
                cb = _bdot_nt(cmat, bmat)
                for h in range(grp * 2, grp * 2 + 2):
                    ch = GATE_DT + d * N_HEADS + h
                    hs = slice(h * D_STATE, (h + 1) * D_STATE)
                    x = x_ref[pl.ds(r0, CHUNK), hs]
                    a_c = acol[:, ch:ch + 1]
                    a_r = arow[ch:ch + 1, :]
                    dt_c = dt[:, ch:ch + 1]
                    s_prev = s_scr[d, h]
                    decay = jnp.exp(jnp.where(masks[d], a_c - a_r, -jnp.inf))
                    y = _bdot(cb * decay, x * dt_c) + _bdot_nt(cmat, s_prev) * jnp.exp(a_c)
                    if d == 0:
                        y = y + dskip_ref[:, hs] * x
                    y_ref[pl.ds(out0, CHUNK), hs] += y
                    a_last = atot[:, ch:ch + 1]
                    w = jnp.exp(a_last - a_c) * dt_c
                    s_scr[d, h] = jnp.exp(a_last) * s_prev + _bdot_tn(x * w, bmat)
        return carry

    lax.fori_loop(0, nchunk, chunk, 0)

    @pl.when(j == nblk - 1)
    def _():
        sout_ref[0] = s_scr[...]


def _ssd(xbc, u, dt_bias_row, alog_row, dskip_row, bsz, t, init):
    rows = min(t, ROW_TILE)
    nblk = t // rows
    zero_init = init is None

    def fwd(col):
        return lambda b, j: (b * nblk + j, col)

    def bwd(col):
        return lambda b, j: (b * nblk + nblk - 1 - j, col)

    in_specs = []
    for mk in (fwd, bwd):
        in_specs += [pl.BlockSpec((rows, D_REC), mk(0)), pl.BlockSpec((rows, D_REC), mk(1)),
                     pl.BlockSpec((rows, LANES), mk(COL_SMALL // LANES))]
    in_specs += [pl.BlockSpec((1, LANES), lambda b, j: (0, 0)),
                 pl.BlockSpec((1, LANES), lambda b, j: (0, 0)),
                 pl.BlockSpec((1, D_REC), lambda b, j: (0, 0))]
    args = [xbc, xbc, u, xbc, xbc, u, dt_bias_row, alog_row, dskip_row]
    state = pl.BlockSpec((1, 2, N_HEADS, D_STATE, D_STATE), lambda b, j: (b, 0, 0, 0, 0))
    if not zero_init:
        in_specs.append(state)
        args.append(init)
    return pl.pallas_call(
        functools.partial(_ssd_kernel, nblk=nblk, rows=rows, zero_init=zero_init),
        grid=(bsz, nblk),
        in_specs=in_specs,
        out_specs=[pl.BlockSpec((t, D_REC), lambda b, j: (b, 0)), state],
        out_shape=[jax.ShapeDtypeStruct((bsz * t, D_REC), F32),
                   jax.ShapeDtypeStruct((bsz, 2, N_HEADS, D_STATE, D_STATE), F32)],
        scratch_shapes=[pltpu.VMEM((2, N_HEADS, D_STATE, D_STATE), F32)],
        compiler_params=_params(("parallel", "arbitrary")),
        name="ssd_scan",
    )(*args)


def _outproj_kernel(att_ref, h_ref, mo_ref, y_ref, z_ref, w_ref, x_ref, gate_ref,
                    mnw_ref, snw_ref, lg_ref, lb_ref, o_ref):
    hh = h_ref[...]
    parts = []
    for h in range(N_HEADS):
        xh = hh[:, h * D_STATE:(h + 1) * D_STATE]
        mu = jnp.mean(xh, -1, keepdims=True)
        dlt = xh - mu
        var = jnp.mean(dlt * dlt, -1, keepdims=True)
        parts.append(dlt * lax.rsqrt(var + EPS))
    ml = jax.nn.sigmoid(mo_ref[...]) * jnp.concatenate(parts, axis=1) * mnw_ref[...]
    yz = y_ref[...] * _silu(z_ref[...])
    parts = []
    for grp in range(N_GROUPS):
        yg = yz[:, grp * LANES:(grp + 1) * LANES]
        parts.append(yg * lax.rsqrt(jnp.mean(yg * yg, -1, keepdims=True) + EPS))
    ssm = jnp.concatenate(parts, axis=1) * snw_ref[...]
    mixed = (jnp.dot(att_ref[...], w_ref[0:D_ATT], preferred_element_type=F32)
             + _bdot(ml, w_ref[D_ATT:D_ATT + D_REC])
             + _bdot(ssm, w_ref[D_ATT + D_REC:D_MODEL]))
    y = ALPHA * x_ref[...] + gate_ref[0] * mixed
    o_ref[...] = _layernorm_rows(y, lg_ref[...], lb_ref[...])


def _out_proj(att, hsum, ysum, u, w, x, gate, mnw, snw, lg, lb, rows_per_mod):
    n = x.shape[0]
    tpb = rows_per_mod // ROW_TILE
    row = lambda width, col: pl.BlockSpec((ROW_TILE, width), lambda i: (i, col))
    vec = lambda width: pl.BlockSpec((1, width), lambda i: (0, 0))
    return pl.pallas_call(
        _outproj_kernel,
        grid=(n // ROW_TILE,),
        in_specs=[row(D_ATT, 0), row(D_REC, 0), row(D_REC, COL_MO // D_REC), row(D_REC, 0),
                  row(D_REC, COL_SZ // D_REC),
                  pl.BlockSpec((D_MODEL, D_MODEL), lambda i: (0, 0)),
                  row(D_MODEL, 0),
                  pl.BlockSpec((1, 1, D_MODEL), lambda i: (i // tpb, 0, 0)),
                  vec(D_REC), vec(D_REC), vec(D_MODEL), vec(D_MODEL)],
        out_specs=row(D_MODEL, 0),
        out_shape=jax.ShapeDtypeStruct((n, D_MODEL), F32),
        compiler_params=_params(("parallel",)),
        name="out_proj",
    )(att, hsum, u, ysum, u, w, x, gate, mnw.reshape(1, D_REC), snw.reshape(1, D_REC),
      lg.reshape(1, D_MODEL), lb.reshape(1, D_MODEL))


FF_TILE = D_FF // 2
FFN_ROWS = 512


def _swiglu_partial(h, w1, w3, w2):
    a = jnp.dot(h, w1, preferred_element_type=F32)
    b = jnp.dot(h, w3, preferred_element_type=F32)
    return jnp.dot((_silu(a) * b).astype(BF16), w2, preferred_element_type=F32)


def _ffn_kernel(x_ref, sc_ref, sh_ref, gate_ref, w1_ref, w3_ref, w2_ref, lg_ref, lb_ref, o_ref,
                h_scr, acc_scr):
    j = pl.program_id(1)

    @pl.when(j == 0)
    def _():
        h_scr[...] = (x_ref[...] * (1.0 + sc_ref[0]) + sh_ref[0]).astype(BF16)
        acc_scr[...] = jnp.zeros_like(acc_scr)

    acc_scr[...] += _swiglu_partial(h_scr[...], w1_ref[...], w3_ref[...], w2_ref[...])

    @pl.when(j == pl.num_programs(1) - 1)
    def _():
        y = ALPHA * x_ref[...] + gate_ref[0] * acc_scr[...]
        o_ref[...] = _layernorm_rows(y, lg_ref[...], lb_ref[...])


def _ffn(x, sc, sh, gate, w1, w3, w2, lg, lb, rows_per_mod):
    n = x.shape[0]
    tpb = rows_per_mod // FFN_ROWS
    modspec = pl.BlockSpec((1, 1, D_MODEL), lambda i, j: (i // tpb, 0, 0))
    vec = pl.BlockSpec((1, D_MODEL), lambda i, j: (0, 0))
    return pl.pallas_call(
        _ffn_kernel,
        grid=(n // FFN_ROWS, D_FF // FF_TILE),
        in_specs=[pl.BlockSpec((FFN_ROWS, D_MODEL), lambda i, j: (i, 0)), modspec, modspec, modspec,
                  pl.BlockSpec((D_MODEL, FF_TILE), lambda i, j: (0, j)),
                  pl.BlockSpec((D_MODEL, FF_TILE), lambda i, j: (0, j)),
                  pl.BlockSpec((FF_TILE, D_MODEL), lambda i, j: (j, 0)), vec, vec],
        out_specs=pl.BlockSpec((FFN_ROWS, D_MODEL), lambda i, j: (i, 0)),
        out_shape=jax.ShapeDtypeStruct((n, D_MODEL), F32),
        scratch_shapes=[pltpu.VMEM((FFN_ROWS, D_MODEL), BF16), pltpu.VMEM((FFN_ROWS, D_MODEL), F32)],
        compiler_params=_params(("parallel", "arbitrary")),
        name="ffn_dense",
    )(x, sc, sh, gate, w1, w3, w2, lg.reshape(1, D_MODEL), lb.reshape(1, D_MODEL))


def _router_kernel(x_ref, sc_ref, sh_ref, rw_ref, gates_ref):
    h = x_ref[...] * (1.0 + sc_ref[0]) + sh_ref[0]
    logits = jnp.dot(h, rw_ref[...], precision=HIGHEST, preferred_element_type=F32)
    lane = lax.broadcasted_iota(jnp.int32, logits.shape, 1)
    valid = lane < N_EXPERTS
    p = jnp.where(valid, _softmax_rows(jnp.where(valid, logits, -jnp.inf)), -2.0)
    p1 = jnp.max(p, -1, keepdims=True)
    i1 = jnp.min(jnp.where(p == p1, lane, LANES), -1, keepdims=True)
    rest = jnp.where(lane == i1, -1.0, p)
    p2 = jnp.max(rest, -1, keepdims=True)
    i2 = jnp.min(jnp.where(rest == p2, lane, LANES), -1, keepdims=True)
    tot = p1 + p2
    gates_ref[...] = jnp.where(lane == i1, p1 / tot, jnp.where(lane == i2, p2 / tot, 0.0))


def _router(x, sc, sh, router_w, rows_per_mod):
    n = x.shape[0]
    tpb = rows_per_mod // ROW_TILE
    modspec = pl.BlockSpec((1, 1, D_MODEL), lambda i: (i // tpb, 0, 0))
    rw = jnp.pad(router_w, ((0, 0), (0, LANES - N_EXPERTS)))
    return pl.pallas_call(
        _router_kernel,
        grid=(n // ROW_TILE,),
        in_specs=[pl.BlockSpec((ROW_TILE, D_MODEL), lambda i: (i, 0)), modspec, modspec,
                  pl.BlockSpec((D_MODEL, LANES), lambda i: (0, 0))],
        out_specs=pl.BlockSpec((ROW_TILE, LANES), lambda i: (i, 0)),
        out_shape=jax.ShapeDtypeStruct((n, LANES), F32),
        compiler_params=_params(("parallel",)),
        name="router",
    )(x, sc, sh, rw)


def _moe_kernel(x_ref, sc_ref, sh_ref, gate_ref, gates_ref, w1_ref, w3_ref, w2_ref, lg_ref, lb_ref,
                o_ref, h_scr, acc_scr):
    e = pl.program_id(1)
    j = pl.program_id(2)
    first = jnp.logical_and(e == 0, j == 0)
    last = jnp.logical_and(e == pl.num_programs(1) - 1, j == pl.num_programs(2) - 1)

    @pl.when(first)
    def _():
        h_scr[...] = (x_ref[...] * (1.0 + sc_ref[0]) + sh_ref[0]).astype(BF16)
        acc_scr[...] = jnp.zeros_like(acc_scr)

    gates = gates_ref[...]
    lane = lax.broadcasted_iota(jnp.int32, gates.shape, 1)
    ge = jnp.sum(jnp.where(lane == e, gates, 0.0), -1, keepdims=True)
    acc_scr[...] += ge * _swiglu_partial(h_scr[...], w1_ref[0], w3_ref[0], w2_ref[0])

    @pl.when(last)
    def _():
        y = ALPHA * x_ref[...] + gate_ref[0] * acc_scr[...]
        o_ref[...] = _layernorm_rows(y, lg_ref[...], lb_ref[...])


def _moe(x, sc, sh, gate, gates, w1, w3, w2, lg, lb, rows_per_mod):
    n = x.shape[0]
    tpb = rows_per_mod // FFN_ROWS
    modspec = pl.BlockSpec((1, 1, D_MODEL), lambda i, e, j: (i // tpb, 0, 0))
    vec = pl.BlockSpec((1, D_MODEL), lambda i, e, j: (0, 0))
    return pl.pallas_call(
        _moe_kernel,
        grid=(n // FFN_ROWS, N_EXPERTS, D_FF // FF_TILE),
        in_specs=[pl.BlockSpec((FFN_ROWS, D_MODEL), lambda i, e, j: (i, 0)), modspec, modspec, modspec,
                  pl.BlockSpec((FFN_ROWS, LANES), lambda i, e, j: (i, 0)),
                  pl.BlockSpec((1, D_MODEL, FF_TILE), lambda i, e, j: (e, 0, j)),
                  pl.BlockSpec((1, D_MODEL, FF_TILE), lambda i, e, j: (e, 0, j)),
                  pl.BlockSpec((1, FF_TILE, D_MODEL), lambda i, e, j: (e, j, 0)), vec, vec],
        out_specs=pl.BlockSpec((FFN_ROWS, D_MODEL), lambda i, e, j: (i, 0)),
        out_shape=jax.ShapeDtypeStruct((n, D_MODEL), F32),
        scratch_shapes=[pltpu.VMEM((FFN_ROWS, D_MODEL), BF16), pltpu.VMEM((FFN_ROWS, D_MODEL), F32)],
        compiler_params=_params(("parallel", "arbitrary", "arbitrary")),
        name="moe",
    )(x, sc, sh, gate, gates, w1, w3, w2, lg.reshape(1, D_MODEL), lb.reshape(1, D_MODEL))


def _permute_w_in(w):
    pad = jnp.zeros((D_MODEL, U_COLS - ORIG_END), w.dtype)
    return jnp.concatenate([w[:, :ORIG_GATES], w[:, ORIG_SX:ORIG_DT], w[:, ORIG_SZ:ORIG_SX],
                            w[:, ORIG_GATES:ORIG_SZ], w[:, ORIG_DT:ORIG_END], pad], axis=1).astype(BF16)


def _small_row(vals, offset):
    v = vals.reshape(-1).astype(F32)
    return jnp.zeros((1, LANES), F32).at[0, offset:offset + v.shape[0]].set(v)


def _layer(x, mods, P, l, bsz, t, ctx):
    sh1, sc1, g1, sh2, sc2, g2 = mods
    rows_per_mod = x.shape[0] // sh1.shape[0]
    lam_init = 0.8 - 0.6 * math.exp(-0.3 * l)
    u = _in_proj(x, sc1, sh1, P['w_in'][l], rows_per_mod)

    gate_bias = (_small_row(P['mlstm_gate_b'][l, 0], GATE_I) + _small_row(P['mlstm_gate_b'][l, 1], GATE_F))
    dt_bias = _small_row(P['ssm_dt_bias'][l], GATE_DT)
    alog = _small_row(P['ssm_A_log'][l], GATE_DT)
    dskip = jnp.repeat(P['ssm_D'][l].astype(F32), D_STATE).reshape(1, D_REC)

    if ctx is None:
        att, k_new, v_new = _attention_ctx(u, P['attn_lambda'][l], P['attn_norm_w'][l], lam_init, bsz, t)
        m_init = s_init = None
    else:
        ck, cv, c_c, c_n, c_m, c_s = ctx
        q, k, v = _rope_prep(u, bsz, t)
        k_all = jnp.concatenate([k, ck.astype(BF16)], axis=3)
        v_all = jnp.concatenate([v, cv.astype(BF16)], axis=2)
        att = _attention_lat(q, k_all, v_all, P['attn_lambda'][l], P['attn_norm_w'][l], lam_init, bsz, t)
        cext = jnp.concatenate([c_c, c_n[..., None],
                                jnp.zeros(c_c.shape[:-1] + (LANES - D_STATE - 1,), F32)], axis=-1)
        m_init = (cext, jnp.broadcast_to(c_m[..., None, None], c_m.shape + (1, LANES)))
        s_init = c_s
    hsum, c_out, m_out = _mlstm(u, gate_bias, bsz, t, m_init)
    xbc = _ssd_conv(u, P['conv_w'][l], P['conv_b'][l], bsz, t)
    ysum, s_out = _ssd(xbc, u, dt_bias, alog, dskip, bsz, t, s_init)

    x = _out_proj(att, hsum, ysum, u, P['w_out'][l], x, g1, P['mlstm_norm_w'][l], P['ssm_norm_w'][l],
                  P['ln_g'][l, 0], P['ln_b'][l, 0], rows_per_mod)
    if l % 2 == 0:
        x = _ffn(x, sc2, sh2, g2, P['ffn_w1'][l // 2], P['ffn_w3'][l // 2], P['ffn_w2'][l // 2],
                 P['ln_g'][l, 1], P['ln_b'][l, 1], rows_per_mod)
    else:
        gates = _router(x, sc2, sh2, P['router_w'][l // 2], rows_per_mod)
        x = _moe(x, sc2, sh2, g2, gates, P['moe_w1'][l // 2], P['moe_w3'][l // 2], P['moe_w2'][l // 2],
                 P['ln_g'][l, 1], P['ln_b'][l, 1], rows_per_mod)
    if ctx is None:
        ctx_out = (k_new, v_new, c_out[..., :D_STATE], c_out[..., D_STATE], m_out[..., 0, 0], s_out)
        return x, ctx_out
    return x, None


def kernel(x_prompt, x_sample, c, cache_attn_k, cache_attn_v, state_mlstm_C, state_mlstm_n, state_mlstm_m, state_ssm, c_ctx, w_ada, b_ada, w_in, w_out, attn_lambda, attn_norm_w, mlstm_gate_b, mlstm_norm_w, conv_w, conv_b, ssm_A_log, ssm_dt_bias, ssm_D, ssm_norm_w, ln_g, ln_b, ffn_w1, ffn_w3, ffn_w2, router_w, moe_w1, moe_w3, moe_w2):
    bsz, seq, _ = x_prompt.shape
    dbsz, dseq, _ = x_sample.shape
    P = dict(w_in=[_permute_w_in(w_in[l]) for l in range(DEPTH)], w_out=w_out.astype(BF16),
             attn_lambda=attn_lambda, attn_norm_w=attn_norm_w, mlstm_gate_b=mlstm_gate_b,
             mlstm_norm_w=mlstm_norm_w, conv_w=conv_w, conv_b=conv_b, ssm_A_log=ssm_A_log,
             ssm_dt_bias=ssm_dt_bias, ssm_D=ssm_D, ssm_norm_w=ssm_norm_w, ln_g=ln_g, ln_b=ln_b,
             ffn_w1=ffn_w1.astype(BF16), ffn_w3=ffn_w3.astype(BF16), ffn_w2=ffn_w2.astype(BF16),
             router_w=router_w, moe_w1=moe_w1.astype(BF16), moe_w3=moe_w3.astype(BF16),
             moe_w2=moe_w2.astype(BF16))

    cvec = jnp.zeros((8, D_MODEL), F32).at[0].set(c_ctx).at[1:1 + dbsz].set(c)
    mod = _modulation(cvec, w_ada, b_ada)

    def mods_for(l, lo, hi):
        return [mod[l, lo:hi, i * D_MODEL:(i + 1) * D_MODEL][:, None, :] for i in range(6)]

    y_prompt = x_prompt.reshape(bsz * seq, D_MODEL)
    outs = []
    for l in range(DEPTH):
        y_prompt, ctx_out = _layer(y_prompt, mods_for(l, 0, 1), P, l, bsz, seq, None)
        outs.append(ctx_out)

    y_sample = x_sample.reshape(dbsz * dseq, D_MODEL)
    for l in range(DEPTH):
        ctx = (cache_attn_k[:, l], cache_attn_v[:, l], state_mlstm_C[:, l], state_mlstm_n[:, l],
               state_mlstm_m[:, l], state_ssm[:, l])
        y_sample, _ = _layer(y_sample, mods_for(l, 1, 1 + dbsz), P, l, dbsz, dseq, ctx)

    stacked = [jnp.stack([o[i] for o in outs], axis=1) for i in range(6)]
    return (y_prompt.reshape(bsz, seq, D_MODEL), y_sample.reshape(dbsz, dseq, D_MODEL), *stacked)
```

```python
import functools
import math

import jax
import jax.numpy as jnp
from jax import lax
from jax.experimental import pallas as pl
from jax.experimental.pallas import tpu as pltpu

F32 = jnp.float32
BF16 = jnp.bfloat16
HIGHEST = lax.Precision.HIGHEST

D_MODEL = 1024
DEPTH = 2
GRID_W = 64
N_HEADS = 4
D_ATT = 512
D_HEAD_V = 128
D_QK = 64
D_REC = 256
D_STATE = 64
N_GROUPS = 2
D_CONV = 3
D_FF = 2816
N_EXPERTS = 8
ALPHA = (2.0 * DEPTH) ** 0.25
CHUNK = 64
ROPE_BASE = 10000.0
EPS = 1e-5

COL_AQ, COL_AK, COL_AV = 0, 512, 1024
COL_MQ, COL_MK, COL_MV, COL_MO = 1536, 1792, 2048, 2304
COL_SX, COL_SBC, COL_SZ = 2560, 2816, 3072
COL_SMALL = 3328
U_COLS = 3584
ORIG_GATES, ORIG_SZ, ORIG_SX, ORIG_DT, ORIG_END = 2560, 2576, 2832, 3344, 3352
GATE_I, GATE_F, GATE_DT = 0, 8, 16

LANES = 128
ROW_TILE = 256
VMEM_LIMIT = 48 * 1024 * 1024

NT_DIMS = (((1,), (1,)), ((), ()))
TN_DIMS = (((0,), (0,)), ((), ()))


def _params(sem, vmem=VMEM_LIMIT):
    return pltpu.CompilerParams(dimension_semantics=sem, vmem_limit_bytes=vmem)


def _silu(x):
    return x * jax.nn.sigmoid(x)


def _bdot(a, b):
    return jnp.dot(a.astype(BF16), b.astype(BF16), preferred_element_type=F32)


def _bdot_nt(a, b):
    return lax.dot_general(a.astype(BF16), b.astype(BF16), NT_DIMS, preferred_element_type=F32)


def _bdot_tn(a, b):
    return lax.dot_general(a.astype(BF16), b.astype(BF16), TN_DIMS, preferred_element_type=F32)


def _layernorm_rows(y, g, b):
    mu = jnp.mean(y, -1, keepdims=True)
    d = y - mu
    var = jnp.mean(d * d, -1, keepdims=True)
    return d * lax.rsqrt(var + EPS) * g + b


def _mod_kernel(c_ref, w_ref, b_ref, o_ref):
    o_ref[0] = jnp.dot(_silu(c_ref[...]), w_ref[0], precision=HIGHEST,
                       preferred_element_type=F32) + b_ref[0]


def _modulation(cvec, w_ada, b_ada):
    tn = 1536
    return pl.pallas_call(
        _mod_kernel,
        grid=(DEPTH, 6 * D_MODEL // tn),
        in_specs=[pl.BlockSpec((8, D_MODEL), lambda l, j: (0, 0)),
                  pl.BlockSpec((1, D_MODEL, tn), lambda l, j: (l, 0, j)),
                  pl.BlockSpec((1, 1, tn), lambda l, j: (l, 0, j))],
        out_specs=pl.BlockSpec((1, 8, tn), lambda l, j: (l, 0, j)),
        out_shape=jax.ShapeDtypeStruct((DEPTH, 8, 6 * D_MODEL), F32),
        compiler_params=_params(("parallel", "parallel")),
        name="modulation",
    )(cvec, w_ada, b_ada.reshape(DEPTH, 1, 6 * D_MODEL))


def _inproj_kernel(x_ref, sc_ref, sh_ref, w_ref, o_ref):
    h = (x_ref[...] * (1.0 + sc_ref[0]) + sh_ref[0]).astype(BF16)
    for n0 in range(0, U_COLS, 512):
        o_ref[:, n0:n0 + 512] = jnp.dot(h, w_ref[:, n0:n0 + 512], preferred_element_type=F32)


def _in_proj(x, sc, sh, w, rows_per_mod):
    n = x.shape[0]
    tpb = rows_per_mod // ROW_TILE
    return pl.pallas_call(
        _inproj_kernel,
        grid=(n // ROW_TILE,),
        in_specs=[pl.BlockSpec((ROW_TILE, D_MODEL), lambda i: (i, 0)),
                  pl.BlockSpec((1, 1, D_MODEL), lambda i: (i // tpb, 0, 0)),
                  pl.BlockSpec((1, 1, D_MODEL), lambda i: (i // tpb, 0, 0)),
                  pl.BlockSpec((D_MODEL, U_COLS), lambda i: (0, 0))],
        out_specs=pl.BlockSpec((ROW_TILE, U_COLS), lambda i: (i, 0)),
        out_shape=jax.ShapeDtypeStruct((n, U_COLS), F32),
        compiler_params=_params(("parallel",)),
        name="in_proj",
    )(x, sc, sh, w)


def _lambda_scalar(lam_ref, lam_init):
    lp = lam_ref[...]
    s01 = jnp.sum(lp[0:1] * lp[1:2], axis=-1, keepdims=True)
    s23 = jnp.sum(lp[2:3] * lp[3:4], axis=-1, keepdims=True)
    return jnp.exp(s01) - jnp.exp(s23) + lam_init


def _softmax_rows(s):
    e = jnp.exp(s - jnp.max(s, -1, keepdims=True))
    return e / jnp.sum(e, -1, keepdims=True)


def _head_norm(o, nw, lam_init):
    return o * lax.rsqrt(jnp.mean(o * o, -1, keepdims=True) + EPS) * nw * (1.0 - lam_init)


def _attn_ctx_kernel(u_ref, lam_ref, nw_ref, att_ref, k_ref, v_ref, *, lam_init):
    lam = _lambda_scalar(lam_ref, lam_init)
    for h in range(N_HEADS):
        v = u_ref[:, COL_AV + h * D_HEAD_V:COL_AV + (h + 1) * D_HEAD_V]
        v_ref[0, h] = v
        ps = []
        for m in range(2):
            c0 = h * D_HEAD_V + m * D_QK
            q = u_ref[:, COL_AQ + c0:COL_AQ + c0 + D_QK] * (D_QK ** -0.5)
            k = u_ref[:, COL_AK + c0:COL_AK + c0 + D_QK]
            k_ref[0, h, m] = k
            ps.append(_softmax_rows(_bdot_nt(q, k)))
        o = _bdot(ps[0] - lam * ps[1], v)
        att_ref[:, h * D_HEAD_V:(h + 1) * D_HEAD_V] = _head_norm(o, nw_ref[...], lam_init).astype(BF16)


def _attention_ctx(u, lam_p, norm_w, lam_init, bsz, t):
    n = bsz * t
    return pl.pallas_call(
        functools.partial(_attn_ctx_kernel, lam_init=lam_init),
        grid=(bsz,),
        in_specs=[pl.BlockSpec((t, 3 * D_ATT), lambda b: (b, 0)),
                  pl.BlockSpec((4, D_QK), lambda b: (0, 0)),
                  pl.BlockSpec((1, D_HEAD_V), lambda b: (0, 0))],
        out_specs=[pl.BlockSpec((t, D_ATT), lambda b: (b, 0)),
                   pl.BlockSpec((1, N_HEADS, 2, t, D_QK), lambda b: (b, 0, 0, 0, 0)),
                   pl.BlockSpec((1, N_HEADS, t, D_HEAD_V), lambda b: (b, 0, 0, 0))],
        out_shape=[jax.ShapeDtypeStruct((n, D_ATT), BF16),
                   jax.ShapeDtypeStruct((bsz, N_HEADS, 2, t, D_QK), F32),
                   jax.ShapeDtypeStruct((bsz, N_HEADS, t, D_HEAD_V), F32)],
        compiler_params=_params(("parallel",)),
        name="attn_ctx",
    )(u, lam_p, norm_w.reshape(1, D_HEAD_V))


def _rope_kernel(u_ref, cos_ref, sa_ref, sb_ref, q_ref, k_ref, v_ref):
    cos, sa, sb = cos_ref[...], sa_ref[...], sb_ref[...]

    def rope(x):
        return x * cos + pltpu.roll(x, LANES - 16, 1) * sa + pltpu.roll(x, 16, 1) * sb

    for h in range(N_HEADS):
        q = rope(u_ref[:, COL_AQ + h * D_HEAD_V:COL_AQ + (h + 1) * D_HEAD_V]) * (D_QK ** -0.5)
        k = rope(u_ref[:, COL_AK + h * D_HEAD_V:COL_AK + (h + 1) * D_HEAD_V])
        for m in range(2):
            q_ref[0, h, m] = q[:, m * D_QK:(m + 1) * D_QK].astype(BF16)
            k_ref[0, h, m] = k[:, m * D_QK:(m + 1) * D_QK].astype(BF16)
        v_ref[0, h] = u_ref[:, COL_AV + h * D_HEAD_V:COL_AV + (h + 1) * D_HEAD_V].astype(BF16)


def _rope_tables(t):
    rows = jnp.repeat(jnp.arange(t // GRID_W, dtype=F32), GRID_W)
    cols = jnp.tile(jnp.arange(GRID_W, dtype=F32), t // GRID_W)
    half = D_QK // 2
    inv = ROPE_BASE ** (-jnp.arange(0, half, 2, dtype=F32) / half)
    ang_r = rows[:, None] * inv
    ang_c = cols[:, None] * inv
    ang = jnp.concatenate([ang_r, ang_r, ang_c, ang_c], -1)
    cos, sin = jnp.cos(ang), jnp.sin(ang)
    quarter = (jnp.arange(D_QK) // (D_QK // 4)) % 2
    sa = jnp.where(quarter == 0, -sin, 0.0)
    sb = jnp.where(quarter == 1, sin, 0.0)
    tile2 = lambda a: jnp.concatenate([a, a], -1)
    return tile2(cos), tile2(sa), tile2(sb)


def _rope_prep(u, bsz, t):
    tr = 512
    nb = t // tr
    cos, sa, sb = _rope_tables(t)
    tab = pl.BlockSpec((tr, LANES), lambda b, i: (i, 0))
    return pl.pallas_call(
        _rope_kernel,
        grid=(bsz, nb),
        in_specs=[pl.BlockSpec((tr, 3 * D_ATT), lambda b, i: (b * nb + i, 0)), tab, tab, tab],
        out_specs=[pl.BlockSpec((1, N_HEADS, 2, tr, D_QK), lambda b, i: (b, 0, 0, i, 0)),
                   pl.BlockSpec((1, N_HEADS, 2, tr, D_QK), lambda b, i: (b, 0, 0, i, 0)),
                   pl.BlockSpec((1, N_HEADS, tr, D_HEAD_V), lambda b, i: (b, 0, i, 0))],
        out_shape=[jax.ShapeDtypeStruct((bsz, N_HEADS, 2, t, D_QK), BF16),
                   jax.ShapeDtypeStruct((bsz, N_HEADS, 2, t, D_QK), BF16),
                   jax.ShapeDtypeStruct((bsz, N_HEADS, t, D_HEAD_V), BF16)],
        compiler_params=_params(("parallel", "parallel")),
        name="rope_prep",
    )(u, cos, sa, sb)


def _attn_lat_kernel(q_ref, k_ref, v_ref, lam_ref, nw_ref, o_ref, *, lam_init):
    lam = _lambda_scalar(lam_ref, lam_init)
    ps = [_softmax_rows(lax.dot_general(q_ref[0, 0, m], k_ref[0, 0, m], NT_DIMS,
                                        preferred_element_type=F32)) for m in range(2)]
    o = _bdot(ps[0] - lam * ps[1], v_ref[0, 0])
    o_ref[...] = _head_norm(o, nw_ref[...], lam_init).astype(BF16)


def _attention_lat(q, k_all, v_all, lam_p, norm_w, lam_init, bsz, t):
    tq = 128
    nq = t // tq
    s = k_all.shape[3]
    return pl.pallas_call(
        functools.partial(_attn_lat_kernel, lam_init=lam_init),
        grid=(bsz, N_HEADS, nq),
        in_specs=[pl.BlockSpec((1, 1, 2, tq, D_QK), lambda b, h, i: (b, h, 0, i, 0)),
                  pl.BlockSpec((1, 1, 2, s, D_QK), lambda b, h, i: (b, h, 0, 0, 0)),
                  pl.BlockSpec((1, 1, s, D_HEAD_V), lambda b, h, i: (b, h, 0, 0)),
                  pl.BlockSpec((4, D_QK), lambda b, h, i: (0, 0)),
                  pl.BlockSpec((1, D_HEAD_V), lambda b, h, i: (0, 0))],
        out_specs=pl.BlockSpec((tq, D_HEAD_V), lambda b, h, i: (b * nq + i, h)),
        out_shape=jax.ShapeDtypeStruct((bsz * t, D_ATT), BF16),
        compiler_params=_params(("parallel", "parallel", "parallel")),
        name="attn_lat",
    )(q, k_all, v_all, lam_p, norm_w.reshape(1, D_HEAD_V))


def _tri_masks():
    r = lax.broadcasted_iota(jnp.int32, (CHUNK, CHUNK), 0)
    c = lax.broadcasted_iota(jnp.int32, (CHUNK, CHUNK), 1)
    return (c <= r), (c >= r)


def _cumsums(x, mask):
    tri = mask.astype(F32)
    col = jnp.dot(tri, x, precision=HIGHEST, preferred_element_type=F32)
    row = lax.dot_general(x.T, tri, NT_DIMS, precision=HIGHEST, preferred_element_type=F32)
    return col, row


def _mlstm_kernel(*refs, nblk, rows, zero_init):
    if zero_init:
        (qf_ref, kf_ref, vf_ref, gf_ref, qb_ref, kb_ref, vb_ref, gb_ref, bias_ref,
         h_ref, cout_ref, mout_ref, c_scr, m_scr) = refs
    else:
        (qf_ref, kf_ref, vf_ref, gf_ref, qb_ref, kb_ref, vb_ref, gb_ref, bias_ref, c0_ref, m0_ref,
         h_ref, cout_ref, mout_ref, c_scr, m_scr) = refs
    j = pl.program_id(1)
    nchunk = rows // CHUNK

    @pl.when(j == 0)
    def _():
        h_ref[...] = jnp.zeros_like(h_ref)
        if zero_init:
            c_scr[...] = jnp.zeros_like(c_scr)
            m_scr[...] = jnp.zeros_like(m_scr)
        else:
            c_scr[...] = c0_ref[0]
            m_scr[...] = m0_ref[0]

    masks = _tri_masks()
    lane = lax.broadcasted_iota(jnp.int32, (CHUNK, LANES - D_STATE), 1)
    ones_pad = (lane == 0).astype(F32)

    def chunk(ci, carry):
        for d in range(2):
            q_ref, k_ref, v_ref, g_ref = ((qf_ref, kf_ref, vf_ref, gf_ref) if d == 0
                                          else (qb_ref, kb_ref, vb_ref, gb_ref))
            cj = ci if d == 0 else nchunk - 1 - ci
            blk = j if d == 0 else nblk - 1 - j
            r0 = pl.multiple_of(cj * CHUNK, CHUNK)
            out0 = pl.multiple_of(blk * rows + cj * CHUNK, CHUNK)
            g = g_ref[pl.ds(r0, CHUNK), :] + bias_ref[...]
            lf = jax.nn.log_sigmoid(g)
            bcol, brow = _cumsums(lf, masks[d])
            btot = jnp.sum(lf, axis=0, keepdims=True)
            g_t = g.T
            for h in range(N_HEADS):
                ch_i, ch_f = GATE_I + d * N_HEADS + h, GATE_F + d * N_HEADS + h
                hs = slice(h * D_STATE, (h + 1) * D_STATE)
                q = q_ref[pl.ds(r0, CHUNK), hs] * (D_STATE ** -0.5)
                k = k_ref[pl.ds(r0, CHUNK), hs]
                v = v_ref[pl.ds(r0, CHUNK), hs]
                vext = jnp.concatenate([v, ones_pad], axis=1)
                b_c = bcol[:, ch_f:ch_f + 1]
                b_r = brow[ch_f:ch_f + 1, :]
                li_c = g[:, ch_i:ch_i + 1]
                li_r = g_t[ch_i:ch_i + 1, :]
                m_prev = m_scr[d, h][:, 0:1]
                c_prev = c_scr[d, h]
                dm = jnp.where(masks[d], b_c - b_r + li_r, -jnp.inf)
                inter = b_c + m_prev
                m_t = jnp.maximum(inter, jnp.max(dm, -1, keepdims=True))
                w_inter = jnp.exp(inter - m_t)
                s = _bdot_nt(q, k) * jnp.exp(dm - m_t)
                nd = w_inter * _bdot(q, c_prev) + _bdot(s, vext)
                den = nd[:, D_STATE:D_STATE + 1]
                hc = nd[:, :D_STATE] / jnp.maximum(jnp.abs(den), jnp.exp(-m_t))
                h_ref[pl.ds(out0, CHUNK), hs] += hc
                b_last = btot[:, ch_f:ch_f + 1]
                gcol = b_last - b_c + li_c
                m_new = jnp.maximum(b_last + m_prev, jnp.max(gcol, 0, keepdims=True))
                w_c = jnp.exp(b_last + m_prev - m_new)
                w_s = jnp.exp(gcol - m_new)
                c_scr[d, h] = w_c * c_prev + _bdot_tn(k * w_s, vext)
                m_scr[d, h] = jnp.broadcast_to(m_new, (1, LANES))
        return carry

    lax.fori_loop(0, nchunk, chunk, 0)

    @pl.when(j == nblk - 1)
    def _():
        cout_ref[0] = c_scr[...]
        mout_ref[0] = m_scr[...]


def _mlstm(u, gate_bias, bsz, t, init):
    rows = min(t, ROW_TILE)
    nblk = t // rows
    zero_init = init is None

    def fwd(col):
        return lambda b, j: (b * nblk + j, col)

    def bwd(col):
        return lambda b, j: (b * nblk + nblk - 1 - j, col)

    cq, ck, cv, cg = COL_MQ // D_REC, COL_MK // D_REC, COL_MV // D_REC, COL_SMALL // LANES
    in_specs = []
    for mk in (fwd, bwd):
        in_specs += [pl.BlockSpec((rows, D_REC), mk(cq)), pl.BlockSpec((rows, D_REC), mk(ck)),
                     pl.BlockSpec((rows, D_REC), mk(cv)), pl.BlockSpec((rows, LANES), mk(cg))]
    in_specs.append(pl.BlockSpec((1, LANES), lambda b, j: (0, 0)))
    args = [u] * 8 + [gate_bias]
    state_c = pl.BlockSpec((1, 2, N_HEADS, D_STATE, LANES), lambda b, j: (b, 0, 0, 0, 0))
    state_m = pl.BlockSpec((1, 2, N_HEADS, 1, LANES), lambda b, j: (b, 0, 0, 0, 0))
    if not zero_init:
        in_specs += [state_c, state_m]
        args += list(init)
    return pl.pallas_call(
        functools.partial(_mlstm_kernel, nblk=nblk, rows=rows, zero_init=zero_init),
        grid=(bsz, nblk),
        in_specs=in_specs,
        out_specs=[pl.BlockSpec((t, D_REC), lambda b, j: (b, 0)), state_c, state_m],
        out_shape=[jax.ShapeDtypeStruct((bsz * t, D_REC), F32),
                   jax.ShapeDtypeStruct((bsz, 2, N_HEADS, D_STATE, LANES), F32),
                   jax.ShapeDtypeStruct((bsz, 2, N_HEADS, 1, LANES), F32)],
        scratch_shapes=[pltpu.VMEM((2, N_HEADS, D_STATE, LANES), F32),
                        pltpu.VMEM((2, N_HEADS, 1, LANES), F32)],
        compiler_params=_params(("parallel", "arbitrary")),
        name="mlstm_scan",
    )(*args)


def _conv_kernel(x_ref, prev_ref, next_ref, w_ref, b_ref, o_ref, *, nblk):
    i = pl.program_id(1)
    x = x_ref[...]
    rows = x.shape[0]
    r = lax.broadcasted_iota(jnp.int32, x.shape, 0)
    prev_row = prev_ref[7:8, :] * (i > 0).astype(F32)
    next_row = next_ref[0:1, :] * (i < nblk - 1).astype(F32)
    xm = jnp.where(r == 0, prev_row, pltpu.roll(x, 1, 0))
    xp = jnp.where(r == rows - 1, next_row, pltpu.roll(x, rows - 1, 0))
    w = w_ref[...]
    o_ref[...] = _silu(xm * w[0:1] + x * w[1:2] + xp * w[2:3] + b_ref[...])


def _ssd_conv(u, conv_w, conv_b, bsz, t):
    rows = min(t, 512)
    nblk = t // rows
    r8 = rows // 8
    width = 2 * D_REC
    c0 = COL_SX // width
    return pl.pallas_call(
        functools.partial(_conv_kernel, nblk=nblk),
        grid=(bsz, nblk),
        in_specs=[pl.BlockSpec((rows, width), lambda b, i: (b * nblk + i, c0)),
                  pl.BlockSpec((8, width), lambda b, i: (jnp.maximum((b * nblk + i) * r8 - 1, 0), c0)),
                  pl.BlockSpec((8, width), lambda b, i: (jnp.minimum((b * nblk + i + 1) * r8,
                                                                     bsz * nblk * r8 - 1), c0)),
                  pl.BlockSpec((D_CONV, width), lambda b, i: (0, 0)),
                  pl.BlockSpec((1, width), lambda b, i: (0, 0))],
        out_specs=pl.BlockSpec((rows, width), lambda b, i: (b * nblk + i, 0)),
        out_shape=jax.ShapeDtypeStruct((bsz * t, width), F32),
        compiler_params=_params(("parallel", "parallel")),
        name="ssd_conv",
    )(u, u, u, conv_w, conv_b.reshape(1, width))


def _ssd_kernel(*refs, nblk, rows, zero_init):
    if zero_init:
        (xf_ref, bcf_ref, gf_ref, xb_ref, bcb_ref, gb_ref, dtb_ref, alog_ref, dskip_ref,
         y_ref, sout_ref, s_scr) = refs
    else:
        (xf_ref, bcf_ref, gf_ref, xb_ref, bcb_ref, gb_ref, dtb_ref, alog_ref, dskip_ref, s0_ref,
         y_ref, sout_ref, s_scr) = refs
    j = pl.program_id(1)
    nchunk = rows // CHUNK

    @pl.when(j == 0)
    def _():
        y_ref[...] = jnp.zeros_like(y_ref)
        if zero_init:
            s_scr[...] = jnp.zeros_like(s_scr)
        else:
            s_scr[...] = s0_ref[0]

    masks = _tri_masks()
    a_coef = -jnp.exp(alog_ref[...])

    def chunk(ci, carry):
        for d in range(2):
            x_ref, bc_ref, g_ref = (xf_ref, bcf_ref, gf_ref) if d == 0 else (xb_ref, bcb_ref, gb_ref)
            cj = ci if d == 0 else nchunk - 1 - ci
            blk = j if d == 0 else nblk - 1 - j
            r0 = pl.multiple_of(cj * CHUNK, CHUNK)
            out0 = pl.multiple_of(blk * rows + cj * CHUNK, CHUNK)
            dt = jax.nn.softplus(g_ref[pl.ds(r0, CHUNK), :] + dtb_ref[...])
            da = dt * a_coef
            acol, arow = _cumsums(da, masks[d])
            atot = jnp.sum(da, axis=0, keepdims=True)
            for grp in range(N_GROUPS):
                bmat = bc_ref[pl.ds(r0, CHUNK), grp * D_STATE:(grp + 1) * D_STATE]
                cmat = bc_ref[pl.ds(r0, CHUNK), LANES + grp * D_STATE:LANES + (grp + 1) * D_STATE]
                cb = _bdot_nt(cmat, bmat)
                for h in range(grp * 2, grp * 2 + 2):
                    ch = GATE_DT + d * N_HEADS + h
                    hs = slice(h * D_STATE, (h + 1) * D_STATE)
                    x = x_ref[pl.ds(r0, CHUNK), hs]
                    a_c = acol[:, ch:ch + 1]
                    a_r = arow[ch:ch + 1, :]
                    dt_c = dt[:, ch:ch + 1]
                    s_prev = s_scr[d, h]
                    decay = jnp.exp(jnp.where(masks[d], a_c - a_r, -jnp.inf))
                    y = _bdot(cb * decay, x * dt_c) + _bdot_nt(cmat, s_prev) * jnp.exp(a_c)
                    if d == 0:
                        y = y + dskip_ref[:, hs] * x
                    y_ref[pl.ds(out0, CHUNK), hs] += y
                    a_last = atot[:, ch:ch + 1]
                    w = jnp.exp(a_last - a_c) * dt_c
                    s_scr[d, h] = jnp.exp(a_last) * s_prev + _bdot_tn(x * w, bmat)
        return carry

    lax.fori_loop(0, nchunk, chunk, 0)

    @pl.when(j == nblk - 1)
    def _():
        sout_ref[0] = s_scr[...]


def _ssd(xbc, u, dt_bias_row, alog_row, dskip_row, bsz, t, init):
    rows = min(t, ROW_TILE)
    nblk = t // rows
    zero_init = init is None

    def fwd(col):
        return lambda b, j: (b * nblk + j, col)

    def bwd(col):
        return lambda b, j: (b * nblk + nblk - 1 - j, col)

    in_specs = []
    for mk in (fwd, bwd):
        in_specs += [pl.BlockSpec((rows, D_REC), mk(0)), pl.BlockSpec((rows, D_REC), mk(1)),
                     pl.BlockSpec((rows, LANES), mk(COL_SMALL // LANES))]
    in_specs += [pl.BlockSpec((1, LANES), lambda b, j: (0, 0)),
                 pl.BlockSpec((1, LANES), lambda b, j: (0, 0)),
                 pl.BlockSpec((1, D_REC), lambda b, j: (0, 0))]
    args = [xbc, xbc, u, xbc, xbc, u, dt_bias_row, alog_row, dskip_row]
    state = pl.BlockSpec((1, 2, N_HEADS, D_STATE, D_STATE), lambda b, j: (b, 0, 0, 0, 0))
    if not zero_init:
        in_specs.append(state)
        args.append(init)
    return pl.pallas_call(
        functools.partial(_ssd_kernel, nblk=nblk, rows=rows, zero_init=zero_init),
        grid=(bsz, nblk),
        in_specs=in_specs,
        out_specs=[pl.BlockSpec((t, D_REC), lambda b, j: (b, 0)), state],
        out_shape=[jax.ShapeDtypeStruct((bsz * t, D_REC), F32),
                   jax.ShapeDtypeStruct((bsz, 2, N_HEADS, D_STATE, D_STATE), F32)],
        scratch_shapes=[pltpu.VMEM((2, N_HEADS, D_STATE, D_STATE), F32)],
        compiler_params=_params(("parallel", "arbitrary")),
        name="ssd_scan",
    )(*args)


def _outproj_kernel(att_ref, h_ref, mo_ref, y_ref, z_ref, w_ref, x_ref, gate_ref,
                    mnw_ref, snw_ref, lg_ref, lb_ref, o_ref):
    hh = h_ref[...]
    parts = []
    for h in range(N_HEADS):
        xh = hh[:, h * D_STATE:(h + 1) * D_STATE]
        mu = jnp.mean(xh, -1, keepdims=True)
        dlt = xh - mu
        var = jnp.mean(dlt * dlt, -1, keepdims=True)
        parts.append(dlt * lax.rsqrt(var + EPS))
    ml = jax.nn.sigmoid(mo_ref[...]) * jnp.concatenate(parts, axis=1) * mnw_ref[...]
    yz = y_ref[...] * _silu(z_ref[...])
    parts = []
    for grp in range(N_GROUPS):
        yg = yz[:, grp * LANES:(grp + 1) * LANES]
        parts.append(yg * lax.rsqrt(jnp.mean(yg * yg, -1, keepdims=True) + EPS))
    ssm = jnp.concatenate(parts, axis=1) * snw_ref[...]
    mixed = (jnp.dot(att_ref[...], w_ref[0:D_ATT], preferred_element_type=F32)
             + _bdot(ml, w_ref[D_ATT:D_ATT + D_REC])
             + _bdot(ssm, w_ref[D_ATT + D_REC:D_MODEL]))
    y = ALPHA * x_ref[...] + gate_ref[0] * mixed
    o_ref[...] = _layernorm_rows(y, lg_ref[...], lb_ref[...])


def _out_proj(att, hsum, ysum, u, w, x, gate, mnw, snw, lg, lb, rows_per_mod):
    n = x.shape[0]
    tpb = rows_per_mod // ROW_TILE
    row = lambda width, col: pl.BlockSpec((ROW_TILE, width), lambda i: (i, col))
    vec = lambda width: pl.BlockSpec((1, width), lambda i: (0, 0))
    return pl.pallas_call(
        _outproj_kernel,
        grid=(n // ROW_TILE,),
        in_specs=[row(D_ATT, 0), row(D_REC, 0), row(D_REC, COL_MO // D_REC), row(D_REC, 0),
                  row(D_REC, COL_SZ // D_REC),
                  pl.BlockSpec((D_MODEL, D_MODEL), lambda i: (0, 0)),
                  row(D_MODEL, 0),
                  pl.BlockSpec((1, 1, D_MODEL), lambda i: (i // tpb, 0, 0)),
                  vec(D_REC), vec(D_REC), vec(D_MODEL), vec(D_MODEL)],
        out_specs=row(D_MODEL, 0),
        out_shape=jax.ShapeDtypeStruct((n, D_MODEL), F32),
        compiler_params=_params(("parallel",)),
        name="out_proj",
    )(att, hsum, u, ysum, u, w, x, gate, mnw.reshape(1, D_REC), snw.reshape(1, D_REC),
      lg.reshape(1, D_MODEL), lb.reshape(1, D_MODEL))


FF_TILE = D_FF // 2
FFN_ROWS = 512


def _swiglu_partial(h, w1, w3, w2):
    a = jnp.dot(h, w1, preferred_element_type=F32)
    b = jnp.dot(h, w3, preferred_element_type=F32)
    return jnp.dot((_silu(a) * b).astype(BF16), w2, preferred_element_type=F32)


def _ffn_kernel(x_ref, sc_ref, sh_ref, gate_ref, w1_ref, w3_ref, w2_ref, lg_ref, lb_ref, o_ref,
                h_scr, acc_scr):
    j = pl.program_id(1)

    @pl.when(j == 0)
    def _():
        h_scr[...] = (x_ref[...] * (1.0 + sc_ref[0]) + sh_ref[0]).astype(BF16)
        acc_scr[...] = jnp.zeros_like(acc_scr)

    acc_scr[...] += _swiglu_partial(h_scr[...], w1_ref[...], w3_ref[...], w2_ref[...])

    @pl.when(j == pl.num_programs(1) - 1)
    def _():
        y = ALPHA * x_ref[...] + gate_ref[0] * acc_scr[...]
        o_ref[...] = _layernorm_rows(y, lg_ref[...], lb_ref[...])


def _ffn(x, sc, sh, gate, w1, w3, w2, lg, lb, rows_per_mod):
    n = x.shape[0]
    tpb = rows_per_mod // FFN_ROWS
    modspec = pl.BlockSpec((1, 1, D_MODEL), lambda i, j: (i // tpb, 0, 0))
    vec = pl.BlockSpec((1, D_MODEL), lambda i, j: (0, 0))
    return pl.pallas_call(
        _ffn_kernel,
        grid=(n // FFN_ROWS, D_FF // FF_TILE),
        in_specs=[pl.BlockSpec((FFN_ROWS, D_MODEL), lambda i, j: (i, 0)), modspec, modspec, modspec,
                  pl.BlockSpec((D_MODEL, FF_TILE), lambda i, j: (0, j)),
                  pl.BlockSpec((D_MODEL, FF_TILE), lambda i, j: (0, j)),
                  pl.BlockSpec((FF_TILE, D_MODEL), lambda i, j: (j, 0)), vec, vec],
        out_specs=pl.BlockSpec((FFN_ROWS, D_MODEL), lambda i, j: (i, 0)),
        out_shape=jax.ShapeDtypeStruct((n, D_MODEL), F32),
        scratch_shapes=[pltpu.VMEM((FFN_ROWS, D_MODEL), BF16), pltpu.VMEM((FFN_ROWS, D_MODEL), F32)],
        compiler_params=_params(("parallel", "arbitrary")),
        name="ffn_dense",
    )(x, sc, sh, gate, w1, w3, w2, lg.reshape(1, D_MODEL), lb.reshape(1, D_MODEL))


def _router_kernel(x_ref, sc_ref, sh_ref, rw_ref, gates_ref, h_ref):
    h = x_ref[...] * (1.0 + sc_ref[0]) + sh_ref[0]
    h_ref[...] = h.astype(BF16)
    logits = jnp.dot(h, rw_ref[...], precision=HIGHEST, preferred_element_type=F32)
    lane = lax.broadcasted_iota(jnp.int32, logits.shape, 1)
    valid = lane < N_EXPERTS
    p = jnp.where(valid, _softmax_rows(jnp.where(valid, logits, -jnp.inf)), -2.0)
    p1 = jnp.max(p, -1, keepdims=True)
    i1 = jnp.min(jnp.where(p == p1, lane, LANES), -1, keepdims=True)
    rest = jnp.where(lane == i1, -1.0, p)
    p2 = jnp.max(rest, -1, keepdims=True)
    i2 = jnp.min(jnp.where(rest == p2, lane, LANES), -1, keepdims=True)
    tot = p1 + p2
    gates_ref[...] = jnp.where(lane == i1, p1 / tot, jnp.where(lane == i2, p2 / tot, 0.0))


def _router(x, sc, sh, router_w, rows_per_mod):
    n = x.shape[0]
    tpb = rows_per_mod // ROW_TILE
    modspec = pl.BlockSpec((1, 1, D_MODEL), lambda i: (i // tpb, 0, 0))
    rw = jnp.pad(router_w, ((0, 0), (0, LANES - N_EXPERTS)))
    return pl.pallas_call(
        _router_kernel,
        grid=(n // ROW_TILE,),
        in_specs=[pl.BlockSpec((ROW_TILE, D_MODEL), lambda i: (i, 0)), modspec, modspec,
                  pl.BlockSpec((D_MODEL, LANES), lambda i: (0, 0))],
        out_specs=[pl.BlockSpec((ROW_TILE, LANES), lambda i: (i, 0)),
                   pl.BlockSpec((ROW_TILE, D_MODEL), lambda i: (i, 0))],
        out_shape=[jax.ShapeDtypeStruct((n, LANES), F32), jax.ShapeDtypeStruct((n, D_MODEL), BF16)],
        compiler_params=_params(("parallel",)),
        name="router",
    )(x, sc, sh, rw)


MOE_ROWS = 1024
MOE_TILE = 320
MOE_MAX_TILES = -(-MOE_ROWS // MOE_TILE)


def _moe_kernel(h_ref, gates_ref, w1_ref, w3_ref, w2_ref, o_ref, slot_scr, slott_scr, hs_scr, ys_scr):
    e = pl.program_id(1)
    j = pl.program_id(2)
    last_j = pl.num_programs(2) - 1

    @pl.when(jnp.logical_and(e == 0, j == 0))
    def _():
        mask = gates_ref[...] != 0.0
        r = lax.broadcasted_iota(jnp.int32, (MOE_ROWS, MOE_ROWS), 0)
        c = lax.broadcasted_iota(jnp.int32, (MOE_ROWS, MOE_ROWS), 1)
        before = jnp.where(c < r, 1.0, 0.0).astype(BF16)
        rank = jnp.dot(before, jnp.where(mask, 1.0, 0.0).astype(BF16), preferred_element_type=F32)
        slot = jnp.where(mask, rank, -1.0).astype(jnp.int32)
        slot_scr[...] = slot
        slott_scr[...] = slot.T
        o_ref[...] = jnp.zeros_like(o_ref)

    lane = lax.broadcasted_iota(jnp.int32, (MOE_ROWS, LANES), 1)
    slot_col = jnp.max(jnp.where(lane == e, slot_scr[...], -1), -1, keepdims=True)
    n_tiles = (jnp.max(slot_col) + MOE_TILE) // MOE_TILE

    for k in range(MOE_MAX_TILES):
        @pl.when(jnp.logical_and(j == 0, k < n_tiles))
        def _():
            slot_row = slott_scr[pl.ds(e, 1), :]
            rr = lax.broadcasted_iota(jnp.int32, (MOE_TILE, MOE_ROWS), 0) + k * MOE_TILE
            pick = jnp.where(rr == slot_row, 1.0, 0.0).astype(BF16)
            hs_scr[k] = jnp.dot(pick, h_ref[...], preferred_element_type=F32).astype(BF16)
            ys_scr[k] = jnp.zeros((MOE_TILE, D_MODEL), F32)

        @pl.when(k < n_tiles)
        def _():
            ys_scr[k] += _swiglu_partial(hs_scr[k], w1_ref[0], w3_ref[0], w2_ref[0])

        @pl.when(jnp.logical_and(j == last_j, k < n_tiles))
        def _():
            g_col = jnp.sum(jnp.where(lane == e, gates_ref[...], 0.0), -1, keepdims=True)
            cc = lax.broadcasted_iota(jnp.int32, (MOE_ROWS, MOE_TILE), 1) + k * MOE_TILE
            put = jnp.where(slot_col == cc, 1.0, 0.0).astype(BF16)
            y = ys_scr[k]
            y_hi = y.astype(BF16)
            y_lo = (y - y_hi.astype(F32)).astype(BF16)
            back = (jnp.dot(put, y_hi, preferred_element_type=F32)
                    + jnp.dot(put, y_lo, preferred_element_type=F32))
            o_ref[...] += g_col * back


def _moe(h, gates, w1, w3, w2):
    n = h.shape[0]
    return pl.pallas_call(
        _moe_kernel,
        grid=(n // MOE_ROWS, N_EXPERTS, D_FF // FF_TILE),
        in_specs=[pl.BlockSpec((MOE_ROWS, D_MODEL), lambda i, e, j: (i, 0)),
                  pl.BlockSpec((MOE_ROWS, LANES), lambda i, e, j: (i, 0)),
                  pl.BlockSpec((1, D_MODEL, FF_TILE), lambda i, e, j: (e, 0, j)),
                  pl.BlockSpec((1, D_MODEL, FF_TILE), lambda i, e, j: (e, 0, j)),
                  pl.BlockSpec((1, FF_TILE, D_MODEL), lambda i, e, j: (e, j, 0))],
        out_specs=pl.BlockSpec((MOE_ROWS, D_MODEL), lambda i, e, j: (i, 0)),
        out_shape=jax.ShapeDtypeStruct((n, D_MODEL), F32),
        scratch_shapes=[pltpu.VMEM((MOE_ROWS, LANES), jnp.int32), pltpu.VMEM((LANES, MOE_ROWS), jnp.int32),
                        pltpu.VMEM((MOE_MAX_TILES, MOE_TILE, D_MODEL), BF16),
                        pltpu.VMEM((MOE_MAX_TILES, MOE_TILE, D_MODEL), F32)],
        compiler_params=_params(("parallel", "arbitrary", "arbitrary")),
        name="moe",
    )(h, gates, w1, w3, w2)


def _residual_ln_kernel(x_ref, f_ref, gate_ref, lg_ref, lb_ref, o_ref):
    o_ref[...] = _layernorm_rows(ALPHA * x_ref[...] + gate_ref[0] * f_ref[...], lg_ref[...], lb_ref[...])


def _residual_ln(x, f, gate, lg, lb, rows_per_mod):
    n = x.shape[0]
    tpb = rows_per_mod // ROW_TILE
    row = pl.BlockSpec((ROW_TILE, D_MODEL), lambda i: (i, 0))
    vec = pl.BlockSpec((1, D_MODEL), lambda i: (0, 0))
    return pl.pallas_call(
        _residual_ln_kernel,
        grid=(n // ROW_TILE,),
        in_specs=[row, row, pl.BlockSpec((1, 1, D_MODEL), lambda i: (i // tpb, 0, 0)), vec, vec],
        out_specs=row,
        out_shape=jax.ShapeDtypeStruct((n, D_MODEL), F32),
        compiler_params=_params(("parallel",)),
        name="residual_ln",
    )(x, f, gate, lg.reshape(1, D_MODEL), lb.reshape(1, D_MODEL))


def _permute_w_in(w):
    pad = jnp.zeros((D_MODEL, U_COLS - ORIG_END), w.dtype)
    return jnp.concatenate([w[:, :ORIG_GATES], w[:, ORIG_SX:ORIG_DT], w[:, ORIG_SZ:ORIG_SX],
                            w[:, ORIG_GATES:ORIG_SZ], w[:, ORIG_DT:ORIG_END], pad], axis=1).astype(BF16)


def _small_row(vals, offset):
    v = vals.reshape(-1).astype(F32)
    return jnp.zeros((1, LANES), F32).at[0, offset:offset + v.shape[0]].set(v)


def _layer(x, mods, P, l, bsz, t, ctx):
    sh1, sc1, g1, sh2, sc2, g2 = mods
    rows_per_mod = x.shape[0] // sh1.shape[0]
    lam_init = 0.8 - 0.6 * math.exp(-0.3 * l)
    u = _in_proj(x, sc1, sh1, P['w_in'][l], rows_per_mod)

    gate_bias = (_small_row(P['mlstm_gate_b'][l, 0], GATE_I) + _small_row(P['mlstm_gate_b'][l, 1], GATE_F))
    dt_bias = _small_row(P['ssm_dt_bias'][l], GATE_DT)
    alog = _small_row(P['ssm_A_log'][l], GATE_DT)
    dskip = jnp.repeat(P['ssm_D'][l].astype(F32), D_STATE).reshape(1, D_REC)

    if ctx is None:
        att, k_new, v_new = _attention_ctx(u, P['attn_lambda'][l], P['attn_norm_w'][l], lam_init, bsz, t)
        m_init = s_init = None
    else:
        ck, cv, c_c, c_n, c_m, c_s = ctx
        q, k, v = _rope_prep(u, bsz, t)
        k_all = jnp.concatenate([k, ck.astype(BF16)], axis=3)
        v_all = jnp.concatenate([v, cv.astype(BF16)], axis=2)
        att = _attention_lat(q, k_all, v_all, P['attn_lambda'][l], P['attn_norm_w'][l], lam_init, bsz, t)
        cext = jnp.concatenate([c_c, c_n[..., None],
                                jnp.zeros(c_c.shape[:-1] + (LANES - D_STATE - 1,), F32)], axis=-1)
        m_init = (cext, jnp.broadcast_to(c_m[..., None, None], c_m.shape + (1, LANES)))
        s_init = c_s
    hsum, c_out, m_out = _mlstm(u, gate_bias, bsz, t, m_init)
    xbc = _ssd_conv(u, P['conv_w'][l], P['conv_b'][l], bsz, t)
    ysum, s_out = _ssd(xbc, u, dt_bias, alog, dskip, bsz, t, s_init)

    x = _out_proj(att, hsum, ysum, u, P['w_out'][l], x, g1, P['mlstm_norm_w'][l], P['ssm_norm_w'][l],
                  P['ln_g'][l, 0], P['ln_b'][l, 0], rows_per_mod)
    if l % 2 == 0:
        x = _ffn(x, sc2, sh2, g2, P['ffn_w1'][l // 2], P['ffn_w3'][l // 2], P['ffn_w2'][l // 2],
                 P['ln_g'][l, 1], P['ln_b'][l, 1], rows_per_mod)
    else:
        gates, h2 = _router(x, sc2, sh2, P['router_w'][l // 2], rows_per_mod)
        f = _moe(h2, gates, P['moe_w1'][l // 2], P['moe_w3'][l // 2], P['moe_w2'][l // 2])
        x = _residual_ln(x, f, g2, P['ln_g'][l, 1], P['ln_b'][l, 1], rows_per_mod)
    if ctx is None:
        ctx_out = (k_new, v_new, c_out[..., :D_STATE], c_out[..., D_STATE], m_out[..., 0, 0], s_out)
        return x, ctx_out
    return x, None


def kernel(x_prompt, x_sample, c, cache_attn_k, cache_attn_v, state_mlstm_C, state_mlstm_n, state_mlstm_m, state_ssm, c_ctx, w_ada, b_ada, w_in, w_out, attn_lambda, attn_norm_w, mlstm_gate_b, mlstm_norm_w, conv_w, conv_b, ssm_A_log, ssm_dt_bias, ssm_D, ssm_norm_w, ln_g, ln_b, ffn_w1, ffn_w3, ffn_w2, router_w, moe_w1, moe_w3, moe_w2):
    bsz, seq, _ = x_prompt.shape
    dbsz, dseq, _ = x_sample.shape
    P = dict(w_in=[_permute_w_in(w_in[l]) for l in range(DEPTH)], w_out=w_out.astype(BF16),
             attn_lambda=attn_lambda, attn_norm_w=attn_norm_w, mlstm_gate_b=mlstm_gate_b,
             mlstm_norm_w=mlstm_norm_w, conv_w=conv_w, conv_b=conv_b, ssm_A_log=ssm_A_log,
             ssm_dt_bias=ssm_dt_bias, ssm_D=ssm_D, ssm_norm_w=ssm_norm_w, ln_g=ln_g, ln_b=ln_b,
             ffn_w1=ffn_w1.astype(BF16), ffn_w3=ffn_w3.astype(BF16), ffn_w2=ffn_w2.astype(BF16),
             router_w=router_w, moe_w1=moe_w1.astype(BF16), moe_w3=moe_w3.astype(BF16),
             moe_w2=moe_w2.astype(BF16))

    cvec = jnp.zeros((8, D_MODEL), F32).at[0].set(c_ctx).at[1:1 + dbsz].set(c)
    mod = _modulation(cvec, w_ada, b_ada)

    def mods_for(l, lo, hi):
        return [mod[l, lo:hi, i * D_MODEL:(i + 1) * D_MODEL][:, None, :] for i in range(6)]

    y_prompt = x_prompt.reshape(bsz * seq, D_MODEL)
    outs = []
    for l in range(DEPTH):
        y_prompt, ctx_out = _layer(y_prompt, mods_for(l, 0, 1), P, l, bsz, seq, None)
        outs.append(ctx_out)

    y_sample = x_sample.reshape(dbsz * dseq, D_MODEL)
    for l in range(DEPTH):
        ctx = (cache_attn_k[:, l], cache_attn_v[:, l], state_mlstm_C[:, l], state_mlstm_n[:, l],
               state_mlstm_m[:, l], state_ssm[:, l])
        y_sample, _ = _layer(y_sample, mods_for(l, 1, 1 + dbsz), P, l, dbsz, dseq, ctx)

    stacked = [jnp.stack([o[i] for o in outs], axis=1) for i in range(6)]
    return (y_prompt.reshape(bsz, seq, D_MODEL), y_sample.reshape(dbsz, dseq, D_MODEL), *stacked)
```

```python
import functools
import math

import jax
import jax.numpy as jnp
from jax import lax
from jax.experimental import pallas as pl
from jax.experimental.pallas import tpu as pltpu

F32 = jnp.float32
BF16 = jnp.bfloat16
HIGHEST = lax.Precision.HIGHEST

D_MODEL = 1024
DEPTH = 2
GRID_W = 64
N_HEADS = 4
D_ATT = 512
D_HEAD_V = 128
D_QK = 64
D_REC = 256
D_STATE = 64
N_GROUPS = 2
D_CONV = 3
D_FF = 2816
N_EXPERTS = 8
ALPHA = (2.0 * DEPTH) ** 0.25
CHUNK = 64
ROPE_BASE = 10000.0
EPS = 1e-5

COL_AQ, COL_AK, COL_AV = 0, 512, 1024
COL_MQ, COL_MK, COL_MV, COL_MO = 1536, 1792, 2048, 2304
COL_SX, COL_SBC, COL_SZ = 2560, 2816, 3072
COL_SMALL = 3328
U_COLS = 3584
ORIG_GATES, ORIG_SZ, ORIG_SX, ORIG_DT, ORIG_END = 2560, 2576, 2832, 3344, 3352
GATE_I, GATE_F, GATE_DT = 0, 8, 16

LANES = 128
ROW_TILE = 256
VMEM_LIMIT = 48 * 1024 * 1024

NT_DIMS = (((1,), (1,)), ((), ()))
TN_DIMS = (((0,), (0,)), ((), ()))


def _params(sem, vmem=VMEM_LIMIT):
    return pltpu.CompilerParams(dimension_semantics=sem, vmem_limit_bytes=vmem)


def _silu(x):
    return x * jax.nn.sigmoid(x)


def _bdot(a, b):
    return jnp.dot(a.astype(BF16), b.astype(BF16), preferred_element_type=F32)


def _bdot_nt(a, b):
    return lax.dot_general(a.astype(BF16), b.astype(BF16), NT_DIMS, preferred_element_type=F32)


def _bdot_tn(a, b):
    return lax.dot_general(a.astype(BF16), b.astype(BF16), TN_DIMS, preferred_element_type=F32)


def _layernorm_rows(y, g, b):
    mu = jnp.mean(y, -1, keepdims=True)
    d = y - mu
    var = jnp.mean(d * d, -1, keepdims=True)
    return d * lax.rsqrt(var + EPS) * g + b


def _mod_kernel(c_ref, w_ref, b_ref, o_ref):
    o_ref[0] = jnp.dot(_silu(c_ref[...]), w_ref[0], precision=HIGHEST,
                       preferred_element_type=F32) + b_ref[0]


def _modulation(cvec, w_ada, b_ada):
    tn = 1536
    return pl.pallas_call(
        _mod_kernel,
        grid=(DEPTH, 6 * D_MODEL // tn),
        in_specs=[pl.BlockSpec((8, D_MODEL), lambda l, j: (0, 0)),
                  pl.BlockSpec((1, D_MODEL, tn), lambda l, j: (l, 0, j)),
                  pl.BlockSpec((1, 1, tn), lambda l, j: (l, 0, j))],
        out_specs=pl.BlockSpec((1, 8, tn), lambda l, j: (l, 0, j)),
        out_shape=jax.ShapeDtypeStruct((DEPTH, 8, 6 * D_MODEL), F32),
        compiler_params=_params(("parallel", "parallel")),
        name="modulation",
    )(cvec, w_ada, b_ada.reshape(DEPTH, 1, 6 * D_MODEL))


def _inproj_kernel(x_ref, sc_ref, sh_ref, w_ref, o_ref):
    h = (x_ref[...] * (1.0 + sc_ref[0]) + sh_ref[0]).astype(BF16)
    for n0 in range(0, U_COLS, 512):
        o_ref[:, n0:n0 + 512] = jnp.dot(h, w_ref[:, n0:n0 + 512], preferred_element_type=F32)


def _in_proj(x, sc, sh, w, rows_per_mod):
    n = x.shape[0]
    tpb = rows_per_mod // ROW_TILE
    return pl.pallas_call(
        _inproj_kernel,
        grid=(n // ROW_TILE,),
        in_specs=[pl.BlockSpec((ROW_TILE, D_MODEL), lambda i: (i, 0)),
                  pl.BlockSpec((1, 1, D_MODEL), lambda i: (i // tpb, 0, 0)),
                  pl.BlockSpec((1, 1, D_MODEL), lambda i: (i // tpb, 0, 0)),
                  pl.BlockSpec((D_MODEL, U_COLS), lambda i: (0, 0))],
        out_specs=pl.BlockSpec((ROW_TILE, U_COLS), lambda i: (i, 0)),
        out_shape=jax.ShapeDtypeStruct((n, U_COLS), F32),
        compiler_params=_params(("parallel",)),
        name="in_proj",
    )(x, sc, sh, w)


def _lambda_scalar(lam_ref, lam_init):
    lp = lam_ref[...]
    s01 = jnp.sum(lp[0:1] * lp[1:2], axis=-1, keepdims=True)
    s23 = jnp.sum(lp[2:3] * lp[3:4], axis=-1, keepdims=True)
    return jnp.exp(s01) - jnp.exp(s23) + lam_init


def _softmax_rows(s):
    e = jnp.exp(s - jnp.max(s, -1, keepdims=True))
    return e / jnp.sum(e, -1, keepdims=True)


def _head_norm(o, nw, lam_init):
    return o * lax.rsqrt(jnp.mean(o * o, -1, keepdims=True) + EPS) * nw * (1.0 - lam_init)


def _attn_ctx_kernel(u_ref, lam_ref, nw_ref, att_ref, k_ref, v_ref, *, lam_init):
    lam = _lambda_scalar(lam_ref, lam_init)
    for h in range(N_HEADS):
        v = u_ref[:, COL_AV + h * D_HEAD_V:COL_AV + (h + 1) * D_HEAD_V]
        v_ref[0, h] = v
        ps = []
        for m in range(2):
            c0 = h * D_HEAD_V + m * D_QK
            q = u_ref[:, COL_AQ + c0:COL_AQ + c0 + D_QK] * (D_QK ** -0.5)
            k = u_ref[:, COL_AK + c0:COL_AK + c0 + D_QK]
            k_ref[0, h, m] = k
            ps.append(_softmax_rows(_bdot_nt(q, k)))
        o = _bdot(ps[0] - lam * ps[1], v)
        att_ref[:, h * D_HEAD_V:(h + 1) * D_HEAD_V] = _head_norm(o, nw_ref[...], lam_init).astype(BF16)


def _attention_ctx(u, lam_p, norm_w, lam_init, bsz, t):
    n = bsz * t
    return pl.pallas_call(
        functools.partial(_attn_ctx_kernel, lam_init=lam_init),
        grid=(bsz,),
        in_specs=[pl.BlockSpec((t, 3 * D_ATT), lambda b: (b, 0)),
                  pl.BlockSpec((4, D_QK), lambda b: (0, 0)),
                  pl.BlockSpec((1, D_HEAD_V), lambda b: (0, 0))],
        out_specs=[pl.BlockSpec((t, D_ATT), lambda b: (b, 0)),
                   pl.BlockSpec((1, N_HEADS, 2, t, D_QK), lambda b: (b, 0, 0, 0, 0)),
                   pl.BlockSpec((1, N_HEADS, t, D_HEAD_V), lambda b: (b, 0, 0, 0))],
        out_shape=[jax.ShapeDtypeStruct((n, D_ATT), BF16),
                   jax.ShapeDtypeStruct((bsz, N_HEADS, 2, t, D_QK), F32),
                   jax.ShapeDtypeStruct((bsz, N_HEADS, t, D_HEAD_V), F32)],
        compiler_params=_params(("parallel",)),
        name="attn_ctx",
    )(u, lam_p, norm_w.reshape(1, D_HEAD_V))


def _rope_kernel(u_ref, cos_ref, sa_ref, sb_ref, q_ref, k_ref, v_ref):
    cos, sa, sb = cos_ref[...], sa_ref[...], sb_ref[...]

    def rope(x):
        return x * cos + pltpu.roll(x, LANES - 16, 1) * sa + pltpu.roll(x, 16, 1) * sb

    for h in range(N_HEADS):
        q = rope(u_ref[:, COL_AQ + h * D_HEAD_V:COL_AQ + (h + 1) * D_HEAD_V]) * (D_QK ** -0.5)
        k = rope(u_ref[:, COL_AK + h * D_HEAD_V:COL_AK + (h + 1) * D_HEAD_V])
        for m in range(2):
            q_ref[0, h, m] = q[:, m * D_QK:(m + 1) * D_QK].astype(BF16)
            k_ref[0, h, m] = k[:, m * D_QK:(m + 1) * D_QK].astype(BF16)
        v_ref[0, h] = u_ref[:, COL_AV + h * D_HEAD_V:COL_AV + (h + 1) * D_HEAD_V].astype(BF16)


def _rope_tables(t):
    rows = jnp.repeat(jnp.arange(t // GRID_W, dtype=F32), GRID_W)
    cols = jnp.tile(jnp.arange(GRID_W, dtype=F32), t // GRID_W)
    half = D_QK // 2
    inv = ROPE_BASE ** (-jnp.arange(0, half, 2, dtype=F32) / half)
    ang_r = rows[:, None] * inv
    ang_c = cols[:, None] * inv
    ang = jnp.concatenate([ang_r, ang_r, ang_c, ang_c], -1)
    cos, sin = jnp.cos(ang), jnp.sin(ang)
    quarter = (jnp.arange(D_QK) // (D_QK // 4)) % 2
    sa = jnp.where(quarter == 0, -sin, 0.0)
    sb = jnp.where(quarter == 1, sin, 0.0)
    tile2 = lambda a: jnp.concatenate([a, a], -1)
    return tile2(cos), tile2(sa), tile2(sb)


def _rope_prep(u, bsz, t):
    tr = 512
    nb = t // tr
    cos, sa, sb = _rope_tables(t)
    tab = pl.BlockSpec((tr, LANES), lambda b, i: (i, 0))
    return pl.pallas_call(
        _rope_kernel,
        grid=(bsz, nb),
        in_specs=[pl.BlockSpec((tr, 3 * D_ATT), lambda b, i: (b * nb + i, 0)), tab, tab, tab],
        out_specs=[pl.BlockSpec((1, N_HEADS, 2, tr, D_QK), lambda b, i: (b, 0, 0, i, 0)),
                   pl.BlockSpec((1, N_HEADS, 2, tr, D_QK), lambda b, i: (b, 0, 0, i, 0)),
                   pl.BlockSpec((1, N_HEADS, tr, D_HEAD_V), lambda b, i: (b, 0, i, 0))],
        out_shape=[jax.ShapeDtypeStruct((bsz, N_HEADS, 2, t, D_QK), BF16),
                   jax.ShapeDtypeStruct((bsz, N_HEADS, 2, t, D_QK), BF16),
                   jax.ShapeDtypeStruct((bsz, N_HEADS, t, D_HEAD_V), BF16)],
        compiler_params=_params(("parallel", "parallel")),
        name="rope_prep",
    )(u, cos, sa, sb)


def _attn_lat_kernel(q_ref, k_ref, v_ref, lam_ref, nw_ref, o_ref, *, lam_init):
    lam = _lambda_scalar(lam_ref, lam_init)
    ps = [_softmax_rows(lax.dot_general(q_ref[0, 0, m], k_ref[0, 0, m], NT_DIMS,
                                        preferred_element_type=F32)) for m in range(2)]
    o = _bdot(ps[0] - lam * ps[1], v_ref[0, 0])
    o_ref[...] = _head_norm(o, nw_ref[...], lam_init).astype(BF16)


def _attention_lat(q, k_all, v_all, lam_p, norm_w, lam_init, bsz, t):
    tq = 128
    nq = t // tq
    s = k_all.shape[3]
    return pl.pallas_call(
        functools.partial(_attn_lat_kernel, lam_init=lam_init),
        grid=(bsz, N_HEADS, nq),
        in_specs=[pl.BlockSpec((1, 1, 2, tq, D_QK), lambda b, h, i: (b, h, 0, i, 0)),
                  pl.BlockSpec((1, 1, 2, s, D_QK), lambda b, h, i: (b, h, 0, 0, 0)),
                  pl.BlockSpec((1, 1, s, D_HEAD_V), lambda b, h, i: (b, h, 0, 0)),
                  pl.BlockSpec((4, D_QK), lambda b, h, i: (0, 0)),
                  pl.BlockSpec((1, D_HEAD_V), lambda b, h, i: (0, 0))],
        out_specs=pl.BlockSpec((tq, D_HEAD_V), lambda b, h, i: (b * nq + i, h)),
        out_shape=jax.ShapeDtypeStruct((bsz * t, D_ATT), BF16),
        compiler_params=_params(("parallel", "parallel", "parallel")),
        name="attn_lat",
    )(q, k_all, v_all, lam_p, norm_w.reshape(1, D_HEAD_V))


HEAD_LANES = N_HEADS * D_STATE


def _scan_consts():
    t = lax.broadcasted_iota(jnp.int32, (CHUNK, HEAD_LANES), 0)
    s = lax.broadcasted_iota(jnp.int32, (CHUNK, HEAD_LANES), 1) & (CHUNK - 1)
    r = lax.broadcasted_iota(jnp.int32, (CHUNK, CHUNK), 0)
    c = lax.broadcasted_iota(jnp.int32, (CHUNK, CHUNK), 1)
    reads = (s <= t, s >= t)
    tri = (jnp.where(c <= r, 1.0, 0.0), jnp.where(c >= r, 1.0, 0.0))
    return reads, tri, s == t


def _group_mask(rows, cols, row_shift, col_shift):
    r = lax.broadcasted_iota(jnp.int32, (rows, cols), 0) >> 6
    c = (lax.broadcasted_iota(jnp.int32, (rows, cols), 1) >> 6) & (N_HEADS - 1)
    return (r >> row_shift) == (c >> col_shift)


def _spread(x, chans):
    return jnp.concatenate([jnp.broadcast_to(x[:, c:c + 1], (CHUNK, D_STATE)) for c in chans], axis=1)


def _stack_heads(x):
    return jnp.concatenate([x] * N_HEADS, axis=0)


def _mlstm_chunk(q4, k4, v4, g, d, cn_prev, m_prev, reads, tri, eye, bd):
    li = _spread(g, [GATE_I + d * N_HEADS + h for h in range(N_HEADS)])
    lf = _spread(jax.nn.log_sigmoid(g), [GATE_F + d * N_HEADS + h for h in range(N_HEADS)])
    bc = jnp.dot(tri[d], lf, precision=HIGHEST, preferred_element_type=F32)
    btot = jnp.sum(lf, 0, keepdims=True)
    b_row = jnp.sum(jnp.where(reads[1 - d], lf, 0.0), 0, keepdims=True)
    li_row = jnp.sum(jnp.where(eye, li, 0.0), 0, keepdims=True)
    dm = jnp.where(reads[d], bc - b_row + li_row, -jnp.inf)
    rmax = jnp.concatenate(
        [jnp.broadcast_to(jnp.max(dm[:, h * D_STATE:(h + 1) * D_STATE], -1, keepdims=True), (CHUNK, D_STATE))
         for h in range(N_HEADS)], axis=1)
    inter = bc + m_prev
    m_t = jnp.maximum(inter, rmax)
    w_inter = jnp.exp(inter - m_t)
    qs = (q4 * (D_STATE ** -0.5)).astype(BF16)
    kbd = jnp.where(bd[:, :HEAD_LANES], _stack_heads(k4.astype(BF16)), 0)
    s4 = lax.dot_general(qs, kbd, NT_DIMS, preferred_element_type=F32) * jnp.exp(dm - m_t)
    vo = jnp.concatenate([v4.astype(BF16), jnp.ones((CHUNK, HEAD_LANES), BF16)], axis=1)
    vbd = jnp.where(bd, _stack_heads(vo), 0)
    nd = (jnp.concatenate([w_inter, w_inter], axis=1)
          * jnp.dot(qs, cn_prev.astype(BF16), preferred_element_type=F32)
          + jnp.dot(s4.astype(BF16), vbd, preferred_element_type=F32))
    hc = nd[:, :HEAD_LANES] / jnp.maximum(jnp.abs(nd[:, HEAD_LANES:]), jnp.exp(-m_t))
    gcol = btot - bc + li
    m_new = jnp.maximum(btot + m_prev, jnp.max(gcol, 0, keepdims=True))
    w_c = jnp.exp(btot + m_prev - m_new)
    kw = (k4 * jnp.exp(gcol - m_new)).astype(BF16)
    dcn = lax.dot_general(kw, vo, TN_DIMS, preferred_element_type=F32)
    cn_new = jnp.concatenate([w_c, w_c], axis=1) * cn_prev + jnp.where(bd, dcn, 0.0)
    return hc, cn_new, m_new


def _mlstm_kernel(*refs, nblk, rows, zero_init):
    if zero_init:
        (qf_ref, kf_ref, vf_ref, gf_ref, qb_ref, kb_ref, vb_ref, gb_ref, bias_ref,
         hf_ref, hb_ref, cout_ref, mout_ref, cn_scr, m_scr) = refs
    else:
        (qf_ref, kf_ref, vf_ref, gf_ref, qb_ref, kb_ref, vb_ref, gb_ref, bias_ref, c0_ref, m0_ref,
         hf_ref, hb_ref, cout_ref, mout_ref, cn_scr, m_scr) = refs
    j = pl.program_id(1)
    nchunk = rows // CHUNK

    @pl.when(j == 0)
    def _():
        if zero_init:
            cn_scr[...] = jnp.zeros_like(cn_scr)
            m_scr[...] = jnp.zeros_like(m_scr)
        else:
            cn_scr[...] = c0_ref[0]
            m_scr[...] = m0_ref[0]

    reads, tri, eye = _scan_consts()
    bd = _group_mask(HEAD_LANES, 2 * HEAD_LANES, 0, 0)

    def chunk(ci, carry):
        for d in range(2):
            q_ref, k_ref, v_ref, g_ref, h_ref = ((qf_ref, kf_ref, vf_ref, gf_ref, hf_ref) if d == 0
                                                 else (qb_ref, kb_ref, vb_ref, gb_ref, hb_ref))
            cj = ci if d == 0 else nchunk - 1 - ci
            rs = pl.ds(pl.multiple_of(cj * CHUNK, CHUNK), CHUNK)
            g = g_ref[rs, :] + bias_ref[...]
            hc, cn_new, m_new = _mlstm_chunk(q_ref[rs, :], k_ref[rs, :], v_ref[rs, :], g, d,
                                             cn_scr[d], m_scr[d], reads, tri, eye, bd)
            h_ref[rs, :] = hc
            cn_scr[d] = cn_new
            m_scr[d] = m_new
        return carry

    lax.fori_loop(0, nchunk, chunk, 0)

    @pl.when(j == nblk - 1)
    def _():
        cout_ref[0] = cn_scr[...]
        mout_ref[0] = m_scr[...]


def _mlstm(u, gate_bias, bsz, t, init):
    rows = min(t, ROW_TILE)
    nblk = t // rows
    zero_init = init is None

    def fwd(col):
        return lambda b, j: (b * nblk + j, col)

    def bwd(col):
        return lambda b, j: (b * nblk + nblk - 1 - j, col)

    cq, ck, cv, cg = COL_MQ // D_REC, COL_MK // D_REC, COL_MV // D_REC, COL_SMALL // LANES
    in_specs = []
    for mk in (fwd, bwd):
        in_specs += [pl.BlockSpec((rows, D_REC), mk(cq)), pl.BlockSpec((rows, D_REC), mk(ck)),
                     pl.BlockSpec((rows, D_REC), mk(cv)), pl.BlockSpec((rows, LANES), mk(cg))]
    in_specs.append(pl.BlockSpec((1, LANES), lambda b, j: (0, 0)))
    args = [u] * 8 + [gate_bias]
    state_c = pl.BlockSpec((1, 2, HEAD_LANES, 2 * HEAD_LANES), lambda b, j: (b, 0, 0, 0))
    state_m = pl.BlockSpec((1, 2, 1, HEAD_LANES), lambda b, j: (b, 0, 0, 0))
    if not zero_init:
        in_specs += [state_c, state_m]
        args += list(init)
    return pl.pallas_call(
        functools.partial(_mlstm_kernel, nblk=nblk, rows=rows, zero_init=zero_init),
        grid=(bsz, nblk),
        in_specs=in_specs,
        out_specs=[pl.BlockSpec((rows, D_REC), fwd(0)), pl.BlockSpec((rows, D_REC), bwd(0)),
                   state_c, state_m],
        out_shape=[jax.ShapeDtypeStruct((bsz * t, D_REC), F32), jax.ShapeDtypeStruct((bsz * t, D_REC), F32),
                   jax.ShapeDtypeStruct((bsz, 2, HEAD_LANES, 2 * HEAD_LANES), F32),
                   jax.ShapeDtypeStruct((bsz, 2, 1, HEAD_LANES), F32)],
        scratch_shapes=[pltpu.VMEM((2, HEAD_LANES, 2 * HEAD_LANES), F32),
                        pltpu.VMEM((2, 1, HEAD_LANES), F32)],
        compiler_params=_params(("parallel", "arbitrary")),
        name="mlstm_scan",
    )(*args)


def _conv_kernel(x_ref, prev_ref, next_ref, w_ref, b_ref, o_ref, *, nblk):
    i = pl.program_id(1)
    x = x_ref[...]
    rows = x.shape[0]
    r = lax.broadcasted_iota(jnp.int32, x.shape, 0)
    prev_row = prev_ref[7:8, :] * (i > 0).astype(F32)
    next_row = next_ref[0:1, :] * (i < nblk - 1).astype(F32)
    xm = jnp.where(r == 0, prev_row, pltpu.roll(x, 1, 0))
    xp = jnp.where(r == rows - 1, next_row, pltpu.roll(x, rows - 1, 0))
    w = w_ref[...]
    o_ref[...] = _silu(xm * w[0:1] + x * w[1:2] + xp * w[2:3] + b_ref[...])


def _ssd_conv(u, conv_w, conv_b, bsz, t):
    rows = min(t, 512)
    nblk = t // rows
    r8 = rows // 8
    width = 2 * D_REC
    c0 = COL_SX // width
    return pl.pallas_call(
        functools.partial(_conv_kernel, nblk=nblk),
        grid=(bsz, nblk),
        in_specs=[pl.BlockSpec((rows, width), lambda b, i: (b * nblk + i, c0)),
                  pl.BlockSpec((8, width), lambda b, i: (jnp.maximum((b * nblk + i) * r8 - 1, 0), c0)),
                  pl.BlockSpec((8, width), lambda b, i: (jnp.minimum((b * nblk + i + 1) * r8,
                                                                     bsz * nblk * r8 - 1), c0)),
                  pl.BlockSpec((D_CONV, width), lambda b, i: (0, 0)),
                  pl.BlockSpec((1, width), lambda b, i: (0, 0))],
        out_specs=pl.BlockSpec((rows, width), lambda b, i: (b * nblk + i, 0)),
        out_shape=jax.ShapeDtypeStruct((bsz * t, width), F32),
        compiler_params=_params(("parallel", "parallel")),
        name="ssd_conv",
    )(u, u, u, conv_w, conv_b.reshape(1, width))


def _ssd_chunk(x4, bcm, dt128, da128, d, sg_prev, reads, tri, b_sel, s_sel, bd):
    chans = [GATE_DT + d * N_HEADS + h for h in range(N_HEADS)]
    dt = _spread(dt128, chans)
    da = _spread(da128, chans)
    ac = jnp.dot(tri[d], da, precision=HIGHEST, preferred_element_type=F32)
    atot = jnp.sum(da, 0, keepdims=True)
    a_row = jnp.sum(jnp.where(reads[1 - d], da, 0.0), 0, keepdims=True)
    decay = jnp.exp(jnp.where(reads[d], ac - a_row, -jnp.inf))
    bmat = bcm[:, :LANES].astype(BF16)
    cmat = bcm[:, LANES:].astype(BF16)
    bbd = jnp.where(b_sel, _stack_heads(bmat), 0)
    g4 = lax.dot_general(cmat, bbd, NT_DIMS, preferred_element_type=F32)
    xbd = jnp.where(bd, _stack_heads((x4 * dt).astype(BF16)), 0)
    y = (jnp.dot((g4 * decay).astype(BF16), xbd, preferred_element_type=F32)
         + jnp.dot(cmat, sg_prev.astype(BF16), preferred_element_type=F32) * jnp.exp(ac))
    w = jnp.exp(atot - ac) * dt
    dsg = lax.dot_general(bmat, (x4 * w).astype(BF16), TN_DIMS, preferred_element_type=F32)
    sg_new = jnp.exp(atot) * sg_prev + jnp.where(s_sel, dsg, 0.0)
    return y, sg_new


def _ssd_kernel(*refs, nblk, rows, zero_init):
    if zero_init:
        (xf_ref, bcf_ref, gf_ref, xb_ref, bcb_ref, gb_ref, dtb_ref, alog_ref, dskip_ref,
         yf_ref, yb_ref, sout_ref, s_scr) = refs
    else:
        (xf_ref, bcf_ref, gf_ref, xb_ref, bcb_ref, gb_ref, dtb_ref, alog_ref, dskip_ref, s0_ref,
         yf_ref, yb_ref, sout_ref, s_scr) = refs
    j = pl.program_id(1)
    nchunk = rows // CHUNK

    @pl.when(j == 0)
    def _():
        if zero_init:
            s_scr[...] = jnp.zeros_like(s_scr)
        else:
            s_scr[...] = s0_ref[0]

    reads, tri, _ = _scan_consts()
    bd = _group_mask(HEAD_LANES, HEAD_LANES, 0, 0)
    b_sel = _group_mask(HEAD_LANES, LANES, 1, 0)
    s_sel = _group_mask(LANES, HEAD_LANES, 0, 1)
    a_coef = -jnp.exp(alog_ref[...])

    def chunk(ci, carry):
        for d in range(2):
            x_ref, bc_ref, g_ref, y_ref = ((xf_ref, bcf_ref, gf_ref, yf_ref) if d == 0
                                           else (xb_ref, bcb_ref, gb_ref, yb_ref))
            cj = ci if d == 0 else nchunk - 1 - ci
            rs = pl.ds(pl.multiple_of(cj * CHUNK, CHUNK), CHUNK)
            dt128 = jax.nn.softplus(g_ref[rs, :] + dtb_ref[...])
            x4 = x_ref[rs, :]
            y, sg_new = _ssd_chunk(x4, bc_ref[rs, :], dt128, dt128 * a_coef, d, s_scr[d],
                                   reads, tri, b_sel, s_sel, bd)
            if d == 0:
                y = y + dskip_ref[...] * x4
            y_ref[rs, :] = y
            s_scr[d] = sg_new
        return carry

    lax.fori_loop(0, nchunk, chunk, 0)

    @pl.when(j == nblk - 1)
    def _():
        sout_ref[0] = s_scr[...]


def _ssd(xbc, u, dt_bias_row, alog_row, dskip_row, bsz, t, init):
    rows = min(t, ROW_TILE)
    nblk = t // rows
    zero_init = init is None

    def fwd(col):
        return lambda b, j: (b * nblk + j, col)

    def bwd(col):
        return lambda b, j: (b * nblk + nblk - 1 - j, col)

    in_specs = []
    for mk in (fwd, bwd):
        in_specs += [pl.BlockSpec((rows, D_REC), mk(0)), pl.BlockSpec((rows, D_REC), mk(1)),
                     pl.BlockSpec((rows, LANES), mk(COL_SMALL // LANES))]
    in_specs += [pl.BlockSpec((1, LANES), lambda b, j: (0, 0)),
                 pl.BlockSpec((1, LANES), lambda b, j: (0, 0)),
                 pl.BlockSpec((1, D_REC), lambda b, j: (0, 0))]
    args = [xbc, xbc, u, xbc, xbc, u, dt_bias_row, alog_row, dskip_row]
    state = pl.BlockSpec((1, 2, LANES, HEAD_LANES), lambda b, j: (b, 0, 0, 0))
    if not zero_init:
        in_specs.append(state)
        args.append(init)
    return pl.pallas_call(
        functools.partial(_ssd_kernel, nblk=nblk, rows=rows, zero_init=zero_init),
        grid=(bsz, nblk),
        in_specs=in_specs,
        out_specs=[pl.BlockSpec((rows, D_REC), fwd(0)), pl.BlockSpec((rows, D_REC), bwd(0)), state],
        out_shape=[jax.ShapeDtypeStruct((bsz * t, D_REC), F32), jax.ShapeDtypeStruct((bsz * t, D_REC), F32),
                   jax.ShapeDtypeStruct((bsz, 2, LANES, HEAD_LANES), F32)],
        scratch_shapes=[pltpu.VMEM((2, LANES, HEAD_LANES), F32)],
        compiler_params=_params(("parallel", "arbitrary")),
        name="ssd_scan",
    )(*args)


def _outproj_kernel(att_ref, hf_ref, hb_ref, mo_ref, yf_ref, yb_ref, z_ref, w_ref, x_ref, gate_ref,
                    mnw_ref, snw_ref, lg_ref, lb_ref, o_ref):
    hh = hf_ref[...] + hb_ref[...]
    parts = []
    for h in range(N_HEADS):
        xh = hh[:, h * D_STATE:(h + 1) * D_STATE]
        mu = jnp.mean(xh, -1, keepdims=True)
        dlt = xh - mu
        var = jnp.mean(dlt * dlt, -1, keepdims=True)
        parts.append(dlt * lax.rsqrt(var + EPS))
    ml = jax.nn.sigmoid(mo_ref[...]) * jnp.concatenate(parts, axis=1) * mnw_ref[...]
    yz = (yf_ref[...] + yb_ref[...]) * _silu(z_ref[...])
    parts = []
    for grp in range(N_GROUPS):
        yg = yz[:, grp * LANES:(grp + 1) * LANES]
        parts.append(yg * lax.rsqrt(jnp.mean(yg * yg, -1, keepdims=True) + EPS))
    ssm = jnp.concatenate(parts, axis=1) * snw_ref[...]
    mixed = (jnp.dot(att_ref[...], w_ref[0:D_ATT], preferred_element_type=F32)
             + _bdot(ml, w_ref[D_ATT:D_ATT + D_REC])
             + _bdot(ssm, w_ref[D_ATT + D_REC:D_MODEL]))
    y = ALPHA * x_ref[...] + gate_ref[0] * mixed
    o_ref[...] = _layernorm_rows(y, lg_ref[...], lb_ref[...])


def _out_proj(att, hf, hb, yf, yb, u, w, x, gate, mnw, snw, lg, lb, rows_per_mod):
    n = x.shape[0]
    tpb = rows_per_mod // ROW_TILE
    row = lambda width, col: pl.BlockSpec((ROW_TILE, width), lambda i: (i, col))
    vec = lambda width: pl.BlockSpec((1, width), lambda i: (0, 0))
    return pl.pallas_call(
        _outproj_kernel,
        grid=(n // ROW_TILE,),
        in_specs=[row(D_ATT, 0), row(D_REC, 0), row(D_REC, 0), row(D_REC, COL_MO // D_REC),
                  row(D_REC, 0), row(D_REC, 0), row(D_REC, COL_SZ // D_REC),
                  pl.BlockSpec((D_MODEL, D_MODEL), lambda i: (0, 0)),
                  row(D_MODEL, 0),
                  pl.BlockSpec((1, 1, D_MODEL), lambda i: (i // tpb, 0, 0)),
                  vec(D_REC), vec(D_REC), vec(D_MODEL), vec(D_MODEL)],
        out_specs=row(D_MODEL, 0),
        out_shape=jax.ShapeDtypeStruct((n, D_MODEL), F32),
        compiler_params=_params(("parallel",)),
        name="out_proj",
    )(att, hf, hb, u, yf, yb, u, w, x, gate, mnw.reshape(1, D_REC), snw.reshape(1, D_REC),
      lg.reshape(1, D_MODEL), lb.reshape(1, D_MODEL))


FF_TILE = D_FF // 2
FFN_ROWS = 512


def _swiglu_partial(h, w1, w3, w2):
    a = jnp.dot(h, w1, preferred_element_type=F32)
    b = jnp.dot(h, w3, preferred_element_type=F32)
    return jnp.dot((_silu(a) * b).astype(BF16), w2, preferred_element_type=F32)


def _ffn_kernel(x_ref, sc_ref, sh_ref, gate_ref, w1_ref, w3_ref, w2_ref, lg_ref, lb_ref, o_ref,
                h_scr, acc_scr):
    j = pl.program_id(1)

    @pl.when(j == 0)
    def _():
        h_scr[...] = (x_ref[...] * (1.0 + sc_ref[0]) + sh_ref[0]).astype(BF16)
        acc_scr[...] = jnp.zeros_like(acc_scr)

    acc_scr[...] += _swiglu_partial(h_scr[...], w1_ref[0], w3_ref[0], w2_ref[...])

    @pl.when(j == pl.num_programs(1) - 1)
    def _():
        y = ALPHA * x_ref[...] + gate_ref[0] * acc_scr[...]
        o_ref[...] = _layernorm_rows(y, lg_ref[...], lb_ref[...])


def _ffn(x, sc, sh, gate, w1, w3, w2, lg, lb, rows_per_mod):
    n = x.shape[0]
    tpb = rows_per_mod // FFN_ROWS
    modspec = pl.BlockSpec((1, 1, D_MODEL), lambda i, j: (i // tpb, 0, 0))
    vec = pl.BlockSpec((1, D_MODEL), lambda i, j: (0, 0))
    return pl.pallas_call(
        _ffn_kernel,
        grid=(n // FFN_ROWS, D_FF // FF_TILE),
        in_specs=[pl.BlockSpec((FFN_ROWS, D_MODEL), lambda i, j: (i, 0)), modspec, modspec, modspec,
                  pl.BlockSpec((1, D_MODEL, FF_TILE), lambda i, j: (j, 0, 0)),
                  pl.BlockSpec((1, D_MODEL, FF_TILE), lambda i, j: (j, 0, 0)),
                  pl.BlockSpec((FF_TILE, D_MODEL), lambda i, j: (j, 0)), vec, vec],
        out_specs=pl.BlockSpec((FFN_ROWS, D_MODEL), lambda i, j: (i, 0)),
        out_shape=jax.ShapeDtypeStruct((n, D_MODEL), F32),
        scratch_shapes=[pltpu.VMEM((FFN_ROWS, D_MODEL), BF16), pltpu.VMEM((FFN_ROWS, D_MODEL), F32)],
        compiler_params=_params(("parallel", "arbitrary")),
        name="ffn_dense",
    )(x, sc, sh, gate, w1, w3, w2, lg.reshape(1, D_MODEL), lb.reshape(1, D_MODEL))


def _router_kernel(x_ref, sc_ref, sh_ref, rw_ref, gates_ref, h_ref):
    h = x_ref[...] * (1.0 + sc_ref[0]) + sh_ref[0]
    h_ref[...] = h.astype(BF16)
    logits = jnp.dot(h, rw_ref[...], precision=HIGHEST, preferred_element_type=F32)
    lane = lax.broadcasted_iota(jnp.int32, logits.shape, 1)
    valid = lane < N_EXPERTS
    p = jnp.where(valid, _softmax_rows(jnp.where(valid, logits, -jnp.inf)), -2.0)
    p1 = jnp.max(p, -1, keepdims=True)
    i1 = jnp.min(jnp.where(p == p1, lane, LANES), -1, keepdims=True)
    rest = jnp.where(lane == i1, -1.0, p)
    p2 = jnp.max(rest, -1, keepdims=True)
    i2 = jnp.min(jnp.where(rest == p2, lane, LANES), -1, keepdims=True)
    tot = p1 + p2
    gates_ref[...] = jnp.where(lane == i1, p1 / tot, jnp.where(lane == i2, p2 / tot, 0.0))


def _router(x, sc, sh, router_w, rows_per_mod):
    n = x.shape[0]
    tpb = rows_per_mod // ROW_TILE
    modspec = pl.BlockSpec((1, 1, D_MODEL), lambda i: (i // tpb, 0, 0))
    rw = jnp.pad(router_w, ((0, 0), (0, LANES - N_EXPERTS)))
    return pl.pallas_call(
        _router_kernel,
        grid=(n // ROW_TILE,),
        in_specs=[pl.BlockSpec((ROW_TILE, D_MODEL), lambda i: (i, 0)), modspec, modspec,
                  pl.BlockSpec((D_MODEL, LANES), lambda i: (0, 0))],
        out_specs=[pl.BlockSpec((ROW_TILE, LANES), lambda i: (i, 0)),
                   pl.BlockSpec((ROW_TILE, D_MODEL), lambda i: (i, 0))],
        out_shape=[jax.ShapeDtypeStruct((n, LANES), F32), jax.ShapeDtypeStruct((n, D_MODEL), BF16)],
        compiler_params=_params(("parallel",)),
        name="router",
    )(x, sc, sh, rw)


MOE_ROWS = 1024
MOE_TILE = 320
MOE_MAX_TILES = -(-MOE_ROWS // MOE_TILE)


def _moe_kernel(h_ref, gates_ref, w1_ref, w3_ref, w2_ref, o_ref, slot_scr, slott_scr, hs_scr, ys_scr):
    e = pl.program_id(1)
    j = pl.program_id(2)
    last_j = pl.num_programs(2) - 1

    @pl.when(jnp.logical_and(e == 0, j == 0))
    def _():
        mask = gates_ref[...] != 0.0
        r = lax.broadcasted_iota(jnp.int32, (MOE_ROWS, MOE_ROWS), 0)
        c = lax.broadcasted_iota(jnp.int32, (MOE_ROWS, MOE_ROWS), 1)
        before = jnp.where(c < r, 1.0, 0.0).astype(BF16)
        rank = jnp.dot(before, jnp.where(mask, 1.0, 0.0).astype(BF16), preferred_element_type=F32)
        slot = jnp.where(mask, rank, -1.0).astype(jnp.int32)
        slot_scr[...] = slot
        slott_scr[...] = slot.T
        o_ref[...] = jnp.zeros_like(o_ref)

    lane = lax.broadcasted_iota(jnp.int32, (MOE_ROWS, LANES), 1)
    slot_col = jnp.max(jnp.where(lane == e, slot_scr[...], -1), -1, keepdims=True)
    n_tiles = (jnp.max(slot_col) + MOE_TILE) // MOE_TILE

    for k in range(MOE_MAX_TILES):
        @pl.when(jnp.logical_and(j == 0, k < n_tiles))
        def _():
            slot_row = slott_scr[pl.ds(e, 1), :]
            rr = lax.broadcasted_iota(jnp.int32, (MOE_TILE, MOE_ROWS), 0) + k * MOE_TILE
            pick = jnp.where(rr == slot_row, 1.0, 0.0).astype(BF16)
            hs_scr[k] = jnp.dot(pick, h_ref[...], preferred_element_type=F32).astype(BF16)
            ys_scr[k] = jnp.zeros((MOE_TILE, D_MODEL), F32)

        @pl.when(k < n_tiles)
        def _():
            ys_scr[k] += _swiglu_partial(hs_scr[k], w1_ref[0, 0], w3_ref[0, 0], w2_ref[0])

        @pl.when(jnp.logical_and(j == last_j, k < n_tiles))
        def _():
            g_col = jnp.sum(jnp.where(lane == e, gates_ref[...], 0.0), -1, keepdims=True)
            cc = lax.broadcasted_iota(jnp.int32, (MOE_ROWS, MOE_TILE), 1) + k * MOE_TILE
            put = jnp.where(slot_col == cc, 1.0, 0.0).astype(BF16)
            y = ys_scr[k]
            y_hi = y.astype(BF16)
            y_lo = (y - y_hi.astype(F32)).astype(BF16)
            back = (jnp.dot(put, y_hi, preferred_element_type=F32)
                    + jnp.dot(put, y_lo, preferred_element_type=F32))
            o_ref[...] += g_col * back


def _moe(h, gates, w1, w3, w2):
    n = h.shape[0]
    return pl.pallas_call(
        _moe_kernel,
        grid=(n // MOE_ROWS, N_EXPERTS, D_FF // FF_TILE),
        in_specs=[pl.BlockSpec((MOE_ROWS, D_MODEL), lambda i, e, j: (i, 0)),
                  pl.BlockSpec((MOE_ROWS, LANES), lambda i, e, j: (i, 0)),
                  pl.BlockSpec((1, 1, D_MODEL, FF_TILE), lambda i, e, j: (e, j, 0, 0)),
                  pl.BlockSpec((1, 1, D_MODEL, FF_TILE), lambda i, e, j: (e, j, 0, 0)),
                  pl.BlockSpec((1, FF_TILE, D_MODEL), lambda i, e, j: (e, j, 0))],
        out_specs=pl.BlockSpec((MOE_ROWS, D_MODEL), lambda i, e, j: (i, 0)),
        out_shape=jax.ShapeDtypeStruct((n, D_MODEL), F32),
        scratch_shapes=[pltpu.VMEM((MOE_ROWS, LANES), jnp.int32), pltpu.VMEM((LANES, MOE_ROWS), jnp.int32),
                        pltpu.VMEM((MOE_MAX_TILES, MOE_TILE, D_MODEL), BF16),
                        pltpu.VMEM((MOE_MAX_TILES, MOE_TILE, D_MODEL), F32)],
        compiler_params=_params(("parallel", "arbitrary", "arbitrary")),
        name="moe",
    )(h, gates, w1, w3, w2)


def _residual_ln_kernel(x_ref, f_ref, gate_ref, lg_ref, lb_ref, o_ref):
    o_ref[...] = _layernorm_rows(ALPHA * x_ref[...] + gate_ref[0] * f_ref[...], lg_ref[...], lb_ref[...])


def _residual_ln(x, f, gate, lg, lb, rows_per_mod):
    n = x.shape[0]
    tpb = rows_per_mod // ROW_TILE
    row = pl.BlockSpec((ROW_TILE, D_MODEL), lambda i: (i, 0))
    vec = pl.BlockSpec((1, D_MODEL), lambda i: (0, 0))
    return pl.pallas_call(
        _residual_ln_kernel,
        grid=(n // ROW_TILE,),
        in_specs=[row, row, pl.BlockSpec((1, 1, D_MODEL), lambda i: (i // tpb, 0, 0)), vec, vec],
        out_specs=row,
        out_shape=jax.ShapeDtypeStruct((n, D_MODEL), F32),
        compiler_params=_params(("parallel",)),
        name="residual_ln",
    )(x, f, gate, lg.reshape(1, D_MODEL), lb.reshape(1, D_MODEL))


def _permute_w_in(w):
    pad = jnp.zeros((D_MODEL, U_COLS - ORIG_END), w.dtype)
    return jnp.concatenate([w[:, :ORIG_GATES], w[:, ORIG_SX:ORIG_DT], w[:, ORIG_SZ:ORIG_SX],
                            w[:, ORIG_GATES:ORIG_SZ], w[:, ORIG_DT:ORIG_END], pad], axis=1).astype(BF16)


def _split_ff(w):
    lead = w.shape[:-2]
    w = w.astype(BF16).reshape(lead + (D_MODEL, D_FF // FF_TILE, FF_TILE))
    return jnp.swapaxes(w, -2, -3)


def _small_row(vals, offset):
    v = vals.reshape(-1).astype(F32)
    return jnp.zeros((1, LANES), F32).at[0, offset:offset + v.shape[0]].set(v)


def _pack_mlstm_state(c, n, m):
    eye = jnp.eye(N_HEADS, dtype=F32)
    shape = c.shape[:2] + (HEAD_LANES, HEAD_LANES)
    cbd = jnp.einsum('bdhke,hg->bdhkge', c, eye).reshape(shape)
    nbd = jnp.einsum('bdhk,hg,e->bdhkge', n, eye, jnp.ones((D_STATE,), F32)).reshape(shape)
    return jnp.concatenate([cbd, nbd], axis=-1), jnp.repeat(m, D_STATE, axis=-1)[:, :, None, :]


def _unpack_mlstm_state(cn, m):
    b = cn.shape[0]
    blocks = cn.reshape(b, 2, N_HEADS, D_STATE, 2, N_HEADS, D_STATE)
    c = jnp.stack([blocks[:, :, h, :, 0, h, :] for h in range(N_HEADS)], axis=2)
    n = jnp.stack([blocks[:, :, h, :, 1, h, 0] for h in range(N_HEADS)], axis=2)
    return c, n, m[:, :, 0, ::D_STATE]


def _pack_ssd_state(s):
    sel = (jnp.arange(N_GROUPS)[:, None] == jnp.arange(N_HEADS)[None, :] // 2).astype(F32)
    return jnp.einsum('bdhpn,gh->bdgnhp', s, sel).reshape(s.shape[:2] + (LANES, HEAD_LANES))


def _unpack_ssd_state(sg):
    b = sg.shape[0]
    blocks = sg.reshape(b, 2, N_GROUPS, D_STATE, N_HEADS, D_STATE)
    return jnp.stack([jnp.swapaxes(blocks[:, :, h // 2, :, h, :], -1, -2) for h in range(N_HEADS)], axis=2)


def _layer(x, mods, P, l, bsz, t, ctx):
    sh1, sc1, g1, sh2, sc2, g2 = mods
    rows_per_mod = x.shape[0] // sh1.shape[0]
    lam_init = 0.8 - 0.6 * math.exp(-0.3 * l)
    u = _in_proj(x, sc1, sh1, P['w_in'][l], rows_per_mod)

    gate_bias = (_small_row(P['mlstm_gate_b'][l, 0], GATE_I) + _small_row(P['mlstm_gate_b'][l, 1], GATE_F))
    dt_bias = _small_row(P['ssm_dt_bias'][l], GATE_DT)
    alog = _small_row(P['ssm_A_log'][l], GATE_DT)
    dskip = jnp.repeat(P['ssm_D'][l].astype(F32), D_STATE).reshape(1, D_REC)

    if ctx is None:
        att, k_new, v_new = _attention_ctx(u, P['attn_lambda'][l], P['attn_norm_w'][l], lam_init, bsz, t)
        m_init = s_init = None
    else:
        ck, cv, c_c, c_n, c_m, c_s = ctx
        q, k, v = _rope_prep(u, bsz, t)
        k_all = jnp.concatenate([k, ck.astype(BF16)], axis=3)
        v_all = jnp.concatenate([v, cv.astype(BF16)], axis=2)
        att = _attention_lat(q, k_all, v_all, P['attn_lambda'][l], P['attn_norm_w'][l], lam_init, bsz, t)
        m_init = _pack_mlstm_state(c_c, c_n, c_m)
        s_init = _pack_ssd_state(c_s)
    hf, hb, c_out, m_out = _mlstm(u, gate_bias, bsz, t, m_init)
    xbc = _ssd_conv(u, P['conv_w'][l], P['conv_b'][l], bsz, t)
    yf, yb, s_out = _ssd(xbc, u, dt_bias, alog, dskip, bsz, t, s_init)

    x = _out_proj(att, hf, hb, yf, yb, u, P['w_out'][l], x, g1, P['mlstm_norm_w'][l], P['ssm_norm_w'][l],
                  P['ln_g'][l, 0], P['ln_b'][l, 0], rows_per_mod)
    if l % 2 == 0:
        x = _ffn(x, sc2, sh2, g2, P['ffn_w1'][l // 2], P['ffn_w3'][l // 2], P['ffn_w2'][l // 2],
                 P['ln_g'][l, 1], P['ln_b'][l, 1], rows_per_mod)
    else:
        gates, h2 = _router(x, sc2, sh2, P['router_w'][l // 2], rows_per_mod)
        f = _moe(h2, gates, P['moe_w1'][l // 2], P['moe_w3'][l // 2], P['moe_w2'][l // 2])
        x = _residual_ln(x, f, g2, P['ln_g'][l, 1], P['ln_b'][l, 1], rows_per_mod)
    if ctx is None:
        return x, (k_new, v_new, *_unpack_mlstm_state(c_out, m_out), _unpack_ssd_state(s_out))
    return x, None


def kernel(x_prompt, x_sample, c, cache_attn_k, cache_attn_v, state_mlstm_C, state_mlstm_n, state_mlstm_m, state_ssm, c_ctx, w_ada, b_ada, w_in, w_out, attn_lambda, attn_norm_w, mlstm_gate_b, mlstm_norm_w, conv_w, conv_b, ssm_A_log, ssm_dt_bias, ssm_D, ssm_norm_w, ln_g, ln_b, ffn_w1, ffn_w3, ffn_w2, router_w, moe_w1, moe_w3, moe_w2):
    bsz, seq, _ = x_prompt.shape
    dbsz, dseq, _ = x_sample.shape
    P = dict(w_in=[_permute_w_in(w_in[l]) for l in range(DEPTH)], w_out=w_out.astype(BF16),
             attn_lambda=attn_lambda, attn_norm_w=attn_norm_w, mlstm_gate_b=mlstm_gate_b,
             mlstm_norm_w=mlstm_norm_w, conv_w=conv_w, conv_b=conv_b, ssm_A_log=ssm_A_log,
             ssm_dt_bias=ssm_dt_bias, ssm_D=ssm_D, ssm_norm_w=ssm_norm_w, ln_g=ln_g, ln_b=ln_b,
             ffn_w1=_split_ff(ffn_w1), ffn_w3=_split_ff(ffn_w3), ffn_w2=ffn_w2.astype(BF16),
             router_w=router_w, moe_w1=_split_ff(moe_w1), moe_w3=_split_ff(moe_w3),
             moe_w2=moe_w2.astype(BF16))

    cvec = jnp.zeros((8, D_MODEL), F32).at[0].set(c_ctx).at[1:1 + dbsz].set(c)
    mod = _modulation(cvec, w_ada, b_ada)

    def mods_for(l, lo, hi):
        return [mod[l, lo:hi, i * D_MODEL:(i + 1) * D_MODEL][:, None, :] for i in range(6)]

    y_prompt = x_prompt.reshape(bsz * seq, D_MODEL)
    outs = []
    for l in range(DEPTH):
        y_prompt, ctx_out = _layer(y_prompt, mods_for(l, 0, 1), P, l, bsz, seq, None)
        outs.append(ctx_out)

    y_sample = x_sample.reshape(dbsz * dseq, D_MODEL)
    for l in range(DEPTH):
        ctx = (cache_attn_k[:, l], cache_attn_v[:, l], state_mlstm_C[:, l], state_mlstm_n[:, l],
               state_mlstm_m[:, l], state_ssm[:, l])
        y_sample, _ = _layer(y_sample, mods_for(l, 1, 1 + dbsz), P, l, dbsz, dseq, ctx)

    stacked = [jnp.stack([o[i] for o in outs], axis=1) for i in range(6)]
    return (y_prompt.reshape(bsz, seq, D_MODEL), y_sample.reshape(dbsz, dseq, D_MODEL), *stacked)
```

```python
import functools
import math

import jax
import jax.numpy as jnp
from jax import lax
from jax.experimental import pallas as pl
from jax.experimental.pallas import tpu as pltpu

F32 = jnp.float32
BF16 = jnp.bfloat16
HIGHEST = lax.Precision.HIGHEST

D_MODEL = 1024
DEPTH = 2
GRID_W = 64
N_HEADS = 4
D_ATT = 512
D_HEAD_V = 128
D_QK = 64
D_REC = 256
D_STATE = 64
N_GROUPS = 2
D_CONV = 3
D_FF = 2816
N_EXPERTS = 8
ALPHA = (2.0 * DEPTH) ** 0.25
CHUNK = 64
ROPE_BASE = 10000.0
LOG2E = 1.4426950408889634
EPS = 1e-5

COL_AQ, COL_AK, COL_AV = 0, 512, 1024
COL_MQ, COL_MK, COL_MV, COL_MO = 1536, 1792, 2048, 2304
COL_SX, COL_SBC, COL_SZ = 2560, 2816, 3072
COL_SMALL = 3328
U_COLS = 3584
ORIG_GATES, ORIG_SZ, ORIG_SX, ORIG_DT, ORIG_END = 2560, 2576, 2832, 3344, 3352
GATE_I, GATE_F, GATE_DT = 0, 8, 16

LANES = 128
ROW_TILE = 256
VMEM_LIMIT = 48 * 1024 * 1024

NT_DIMS = (((1,), (1,)), ((), ()))
TN_DIMS = (((0,), (0,)), ((), ()))


def _params(sem, vmem=VMEM_LIMIT):
    return pltpu.CompilerParams(dimension_semantics=sem, vmem_limit_bytes=vmem)


def _silu(x):
    return x * jax.nn.sigmoid(x)


def _bdot(a, b):
    return jnp.dot(a.astype(BF16), b.astype(BF16), preferred_element_type=F32)


def _bdot_nt(a, b):
    return lax.dot_general(a.astype(BF16), b.astype(BF16), NT_DIMS, preferred_element_type=F32)


def _bdot_tn(a, b):
    return lax.dot_general(a.astype(BF16), b.astype(BF16), TN_DIMS, preferred_element_type=F32)


def _layernorm_rows(y, g, b):
    mu = jnp.mean(y, -1, keepdims=True)
    d = y - mu
    var = jnp.mean(d * d, -1, keepdims=True)
    return d * lax.rsqrt(var + EPS) * g + b


def _mod_kernel(c_ref, w_ref, b_ref, o_ref):
    o_ref[0] = jnp.dot(_silu(c_ref[...]), w_ref[0], precision=HIGHEST,
                       preferred_element_type=F32) + b_ref[0]


def _modulation(cvec, w_ada, b_ada):
    tn = 1536
    return pl.pallas_call(
        _mod_kernel,
        grid=(DEPTH, 6 * D_MODEL // tn),
        in_specs=[pl.BlockSpec((8, D_MODEL), lambda l, j: (0, 0)),
                  pl.BlockSpec((1, D_MODEL, tn), lambda l, j: (l, 0, j)),
                  pl.BlockSpec((1, 1, tn), lambda l, j: (l, 0, j))],
        out_specs=pl.BlockSpec((1, 8, tn), lambda l, j: (l, 0, j)),
        out_shape=jax.ShapeDtypeStruct((DEPTH, 8, 6 * D_MODEL), F32),
        compiler_params=_params(("parallel", "parallel")),
        name="modulation",
    )(cvec, w_ada, b_ada.reshape(DEPTH, 1, 6 * D_MODEL))


def _inproj_kernel(x_ref, sc_ref, sh_ref, w_ref, o_ref):
    h = (x_ref[...] * (1.0 + sc_ref[0]) + sh_ref[0]).astype(BF16)
    for n0 in range(0, U_COLS, 512):
        o_ref[:, n0:n0 + 512] = jnp.dot(h, w_ref[:, n0:n0 + 512], preferred_element_type=F32)


def _in_proj(x, sc, sh, w, rows_per_mod):
    n = x.shape[0]
    tpb = rows_per_mod // ROW_TILE
    return pl.pallas_call(
        _inproj_kernel,
        grid=(n // ROW_TILE,),
        in_specs=[pl.BlockSpec((ROW_TILE, D_MODEL), lambda i: (i, 0)),
                  pl.BlockSpec((1, 1, D_MODEL), lambda i: (i // tpb, 0, 0)),
                  pl.BlockSpec((1, 1, D_MODEL), lambda i: (i // tpb, 0, 0)),
                  pl.BlockSpec((D_MODEL, U_COLS), lambda i: (0, 0))],
        out_specs=pl.BlockSpec((ROW_TILE, U_COLS), lambda i: (i, 0)),
        out_shape=jax.ShapeDtypeStruct((n, U_COLS), F32),
        compiler_params=_params(("parallel",)),
        name="in_proj",
    )(x, sc, sh, w)


def _lambda_scalar(lam_ref, lam_init):
    lp = lam_ref[...]
    s01 = jnp.sum(lp[0:1] * lp[1:2], axis=-1, keepdims=True)
    s23 = jnp.sum(lp[2:3] * lp[3:4], axis=-1, keepdims=True)
    return jnp.exp(s01) - jnp.exp(s23) + lam_init


def _softmax_rows(s):
    e = jnp.exp(s - jnp.max(s, -1, keepdims=True))
    return e / jnp.sum(e, -1, keepdims=True)


def _head_norm(o, nw, lam_init):
    return o * lax.rsqrt(jnp.mean(o * o, -1, keepdims=True) + EPS) * nw * (1.0 - lam_init)


def _attn_ctx_kernel(u_ref, lam_ref, nw_ref, att_ref, k_ref, v_ref, *, lam_init):
    lam = _lambda_scalar(lam_ref, lam_init)
    for h in range(N_HEADS):
        v = u_ref[:, COL_AV + h * D_HEAD_V:COL_AV + (h + 1) * D_HEAD_V]
        v_ref[0, h] = v
        ps = []
        for m in range(2):
            c0 = h * D_HEAD_V + m * D_QK
            q = u_ref[:, COL_AQ + c0:COL_AQ + c0 + D_QK] * (D_QK ** -0.5)
            k = u_ref[:, COL_AK + c0:COL_AK + c0 + D_QK]
            k_ref[0, h, m] = k
            ps.append(_softmax_rows(_bdot_nt(q, k)))
        o = _bdot(ps[0] - lam * ps[1], v)
        att_ref[:, h * D_HEAD_V:(h + 1) * D_HEAD_V] = _head_norm(o, nw_ref[...], lam_init).astype(BF16)


def _attention_ctx(u, lam_p, norm_w, lam_init, bsz, t):
    n = bsz * t
    return pl.pallas_call(
        functools.partial(_attn_ctx_kernel, lam_init=lam_init),
        grid=(bsz,),
        in_specs=[pl.BlockSpec((t, 3 * D_ATT), lambda b: (b, 0)),
                  pl.BlockSpec((4, D_QK), lambda b: (0, 0)),
                  pl.BlockSpec((1, D_HEAD_V), lambda b: (0, 0))],
        out_specs=[pl.BlockSpec((t, D_ATT), lambda b: (b, 0)),
                   pl.BlockSpec((1, N_HEADS, 2, t, D_QK), lambda b: (b, 0, 0, 0, 0)),
                   pl.BlockSpec((1, N_HEADS, t, D_HEAD_V), lambda b: (b, 0, 0, 0))],
        out_shape=[jax.ShapeDtypeStruct((n, D_ATT), BF16),
                   jax.ShapeDtypeStruct((bsz, N_HEADS, 2, t, D_QK), F32),
                   jax.ShapeDtypeStruct((bsz, N_HEADS, t, D_HEAD_V), F32)],
        compiler_params=_params(("parallel",)),
        name="attn_ctx",
    )(u, lam_p, norm_w.reshape(1, D_HEAD_V))


def _rope_kernel(u_ref, cos_ref, sa_ref, sb_ref, q_ref, k_ref, v_ref):
    cos, sa, sb = cos_ref[...], sa_ref[...], sb_ref[...]

    def rope(x):
        return x * cos + pltpu.roll(x, LANES - 16, 1) * sa + pltpu.roll(x, 16, 1) * sb

    for h in range(N_HEADS):
        q = rope(u_ref[:, COL_AQ + h * D_HEAD_V:COL_AQ + (h + 1) * D_HEAD_V]) * (LOG2E * D_QK ** -0.5)
        k = rope(u_ref[:, COL_AK + h * D_HEAD_V:COL_AK + (h + 1) * D_HEAD_V])
        for m in range(2):
            q_ref[0, h, m] = q[:, m * D_QK:(m + 1) * D_QK].astype(BF16)
            k_ref[0, h, m] = k[:, m * D_QK:(m + 1) * D_QK].astype(BF16)
        v_ref[0, h] = u_ref[:, COL_AV + h * D_HEAD_V:COL_AV + (h + 1) * D_HEAD_V].astype(BF16)


def _rope_tables(t):
    rows = jnp.repeat(jnp.arange(t // GRID_W, dtype=F32), GRID_W)
    cols = jnp.tile(jnp.arange(GRID_W, dtype=F32), t // GRID_W)
    half = D_QK // 2
    inv = ROPE_BASE ** (-jnp.arange(0, half, 2, dtype=F32) / half)
    ang_r = rows[:, None] * inv
    ang_c = cols[:, None] * inv
    ang = jnp.concatenate([ang_r, ang_r, ang_c, ang_c], -1)
    cos, sin = jnp.cos(ang), jnp.sin(ang)
    quarter = (jnp.arange(D_QK) // (D_QK // 4)) % 2
    sa = jnp.where(quarter == 0, -sin, 0.0)
    sb = jnp.where(quarter == 1, sin, 0.0)
    tile2 = lambda a: jnp.concatenate([a, a], -1)
    return tile2(cos), tile2(sa), tile2(sb)


def _rope_prep(u, bsz, t):
    tr = 512
    nb = t // tr
    cos, sa, sb = _rope_tables(t)
    tab = pl.BlockSpec((tr, LANES), lambda b, i: (i, 0))
    return pl.pallas_call(
        _rope_kernel,
        grid=(bsz, nb),
        in_specs=[pl.BlockSpec((tr, 3 * D_ATT), lambda b, i: (b * nb + i, 0)), tab, tab, tab],
        out_specs=[pl.BlockSpec((1, N_HEADS, 2, tr, D_QK), lambda b, i: (b, 0, 0, i, 0)),
                   pl.BlockSpec((1, N_HEADS, 2, tr, D_QK), lambda b, i: (b, 0, 0, i, 0)),
                   pl.BlockSpec((1, N_HEADS, tr, D_HEAD_V), lambda b, i: (b, 0, i, 0))],
        out_shape=[jax.ShapeDtypeStruct((bsz, N_HEADS, 2, t, D_QK), BF16),
                   jax.ShapeDtypeStruct((bsz, N_HEADS, 2, t, D_QK), BF16),
                   jax.ShapeDtypeStruct((bsz, N_HEADS, t, D_HEAD_V), BF16)],
        compiler_params=_params(("parallel", "parallel")),
        name="rope_prep",
    )(u, cos, sa, sb)


def _attn_lat_kernel(q_ref, k_ref, v_ref, lam_ref, nw_ref, o_ref, *, lam_init):
    lam = _lambda_scalar(lam_ref, lam_init)
    es, sums = [], []
    for m in range(2):
        s = lax.dot_general(q_ref[0, 0, m], k_ref[0, 0, m], NT_DIMS, preferred_element_type=F32)
        e = jnp.exp2(s - jnp.max(s, -1, keepdims=True))
        es.append(e)
        sums.append(jnp.sum(e, -1, keepdims=True))
    a = es[0] - (lam * sums[0] / sums[1]) * es[1]
    o = _bdot(a, v_ref[0, 0]) / sums[0]
    o_ref[...] = _head_norm(o, nw_ref[...], lam_init).astype(BF16)


def _attention_lat(q, k_all, v_all, lam_p, norm_w, lam_init, bsz, t):
    tq = 256
    nq = t // tq
    s = k_all.shape[3]
    return pl.pallas_call(
        functools.partial(_attn_lat_kernel, lam_init=lam_init),
        grid=(bsz, N_HEADS, nq),
        in_specs=[pl.BlockSpec((1, 1, 2, tq, D_QK), lambda b, h, i: (b, h, 0, i, 0)),
                  pl.BlockSpec((1, 1, 2, s, D_QK), lambda b, h, i: (b, h, 0, 0, 0)),
                  pl.BlockSpec((1, 1, s, D_HEAD_V), lambda b, h, i: (b, h, 0, 0)),
                  pl.BlockSpec((4, D_QK), lambda b, h, i: (0, 0)),
                  pl.BlockSpec((1, D_HEAD_V), lambda b, h, i: (0, 0))],
        out_specs=pl.BlockSpec((tq, D_HEAD_V), lambda b, h, i: (b * nq + i, h)),
        out_shape=jax.ShapeDtypeStruct((bsz * t, D_ATT), BF16),
        compiler_params=_params(("parallel", "parallel", "parallel")),
        name="attn_lat",
    )(q, k_all, v_all, lam_p, norm_w.reshape(1, D_HEAD_V))


HEAD_LANES = N_HEADS * D_STATE


def _scan_consts():
    t = lax.broadcasted_iota(jnp.int32, (CHUNK, HEAD_LANES), 0)
    s = lax.broadcasted_iota(jnp.int32, (CHUNK, HEAD_LANES), 1) & (CHUNK - 1)
    r = lax.broadcasted_iota(jnp.int32, (CHUNK, CHUNK), 0)
    c = lax.broadcasted_iota(jnp.int32, (CHUNK, CHUNK), 1)
    reads = (s <= t, s >= t)
    tri = (jnp.where(c <= r, 1.0, 0.0), jnp.where(c >= r, 1.0, 0.0))
    return reads, tri, s == t


def _group_mask(rows, cols, row_shift, col_shift):
    r = lax.broadcasted_iota(jnp.int32, (rows, cols), 0) >> 6
    c = (lax.broadcasted_iota(jnp.int32, (rows, cols), 1) >> 6) & (N_HEADS - 1)
    return (r >> row_shift) == (c >> col_shift)


def _spread(x, chans):
    return jnp.concatenate([jnp.broadcast_to(x[:, c:c + 1], (CHUNK, D_STATE)) for c in chans], axis=1)


def _stack_heads(x):
    return jnp.concatenate([x] * N_HEADS, axis=0)


def _mlstm_chunk(q4, k4, v4, g, d, cn_prev, m_prev, reads, tri, eye, bd):
    li = _spread(g, [GATE_I + d * N_HEADS + h for h in range(N_HEADS)])
    lf = _spread(jax.nn.log_sigmoid(g), [GATE_F + d * N_HEADS + h for h in range(N_HEADS)])
    bc = jnp.dot(tri[d], lf, precision=HIGHEST, preferred_element_type=F32)
    btot = jnp.sum(lf, 0, keepdims=True)
    b_row = jnp.sum(jnp.where(reads[1 - d], lf, 0.0), 0, keepdims=True)
    li_row = jnp.sum(jnp.where(eye, li, 0.0), 0, keepdims=True)
    dm = jnp.where(reads[d], bc - b_row + li_row, -jnp.inf)
    rmax = jnp.concatenate(
        [jnp.broadcast_to(jnp.max(dm[:, h * D_STATE:(h + 1) * D_STATE], -1, keepdims=True), (CHUNK, D_STATE))
         for h in range(N_HEADS)], axis=1)
    inter = bc + m_prev
    m_t = jnp.maximum(inter, rmax)
    w_inter = jnp.exp(inter - m_t)
    qs = (q4 * (D_STATE ** -0.5)).astype(BF16)
    kbd = jnp.where(bd[:, :HEAD_LANES], _stack_heads(k4.astype(BF16)), 0)
    s4 = lax.dot_general(qs, kbd, NT_DIMS, preferred_element_type=F32) * jnp.exp(dm - m_t)
    vo = jnp.concatenate([v4.astype(BF16), jnp.ones((CHUNK, HEAD_LANES), BF16)], axis=1)
    vbd = jnp.where(bd, _stack_heads(vo), 0)
    nd = (jnp.concatenate([w_inter, w_inter], axis=1)
          * jnp.dot(qs, cn_prev.astype(BF16), preferred_element_type=F32)
          + jnp.dot(s4.astype(BF16), vbd, preferred_element_type=F32))
    hc = nd[:, :HEAD_LANES] / jnp.maximum(jnp.abs(nd[:, HEAD_LANES:]), jnp.exp(-m_t))
    gcol = btot - bc + li
    m_new = jnp.maximum(btot + m_prev, jnp.max(gcol, 0, keepdims=True))
    w_c = jnp.exp(btot + m_prev - m_new)
    kw = (k4 * jnp.exp(gcol - m_new)).astype(BF16)
    dcn = lax.dot_general(kw, vo, TN_DIMS, preferred_element_type=F32)
    cn_new = jnp.concatenate([w_c, w_c], axis=1) * cn_prev + jnp.where(bd, dcn, 0.0)
    return hc, cn_new, m_new


def _mlstm_kernel(*refs, nblk, rows, zero_init):
    if zero_init:
        (qf_ref, kf_ref, vf_ref, gf_ref, qb_ref, kb_ref, vb_ref, gb_ref, bias_ref,
         hf_ref, hb_ref, cout_ref, nout_ref, mout_ref, cn_scr, m_scr) = refs
    else:
        (qf_ref, kf_ref, vf_ref, gf_ref, qb_ref, kb_ref, vb_ref, gb_ref, bias_ref, c0_ref, m0_ref,
         hf_ref, hb_ref, cout_ref, nout_ref, mout_ref, cn_scr, m_scr) = refs
    j = pl.program_id(1)
    nchunk = rows // CHUNK

    @pl.when(j == 0)
    def _():
        if zero_init:
            cn_scr[...] = jnp.zeros_like(cn_scr)
            m_scr[...] = jnp.zeros_like(m_scr)
        else:
            cn_scr[...] = c0_ref[0]
            m_scr[...] = m0_ref[0]

    reads, tri, eye = _scan_consts()
    bd = _group_mask(HEAD_LANES, 2 * HEAD_LANES, 0, 0)

    def chunk(ci, carry):
        for d in range(2):
            q_ref, k_ref, v_ref, g_ref, h_ref = ((qf_ref, kf_ref, vf_ref, gf_ref, hf_ref) if d == 0
                                                 else (qb_ref, kb_ref, vb_ref, gb_ref, hb_ref))
            cj = ci if d == 0 else nchunk - 1 - ci
            rs = pl.ds(pl.multiple_of(cj * CHUNK, CHUNK), CHUNK)
            g = g_ref[rs, :] + bias_ref[...]
            hc, cn_new, m_new = _mlstm_chunk(q_ref[rs, :], k_ref[rs, :], v_ref[rs, :], g, d,
                                             cn_scr[d], m_scr[d], reads, tri, eye, bd)
            h_ref[rs, :] = hc
            cn_scr[d] = cn_new
            m_scr[d] = m_new
        return carry

    lax.fori_loop(0, nchunk, chunk, 0)

    @pl.when(j == nblk - 1)
    def _():
        for d in range(2):
            for h in range(N_HEADS):
                r0, r1 = h * D_STATE, (h + 1) * D_STATE
                cout_ref[0, d, h] = cn_scr[d, r0:r1, r0:r1]
                nout_ref[0, d, h] = cn_scr[d, r0:r1, HEAD_LANES + r0:HEAD_LANES + r1]
        mout_ref[0] = m_scr[...]


def _mlstm(u, gate_bias, bsz, t, init):
    rows = min(t, ROW_TILE)
    nblk = t // rows
    zero_init = init is None

    def fwd(col):
        return lambda b, j: (b * nblk + j, col)

    def bwd(col):
        return lambda b, j: (b * nblk + nblk - 1 - j, col)

    cq, ck, cv, cg = COL_MQ // D_REC, COL_MK // D_REC, COL_MV // D_REC, COL_SMALL // LANES
    in_specs = []
    for mk in (fwd, bwd):
        in_specs += [pl.BlockSpec((rows, D_REC), mk(cq)), pl.BlockSpec((rows, D_REC), mk(ck)),
                     pl.BlockSpec((rows, D_REC), mk(cv)), pl.BlockSpec((rows, LANES), mk(cg))]
    in_specs.append(pl.BlockSpec((1, LANES), lambda b, j: (0, 0)))
    args = [u] * 8 + [gate_bias]
    state_c = pl.BlockSpec((1, 2, HEAD_LANES, 2 * HEAD_LANES), lambda b, j: (b, 0, 0, 0))
    state_m = pl.BlockSpec((1, 2, 1, HEAD_LANES), lambda b, j: (b, 0, 0, 0))
    head_blocks = pl.BlockSpec((1, 2, N_HEADS, D_STATE, D_STATE), lambda b, j: (b, 0, 0, 0, 0))
    if not zero_init:
        in_specs += [state_c, state_m]
        args += list(init)
    return pl.pallas_call(
        functools.partial(_mlstm_kernel, nblk=nblk, rows=rows, zero_init=zero_init),
        grid=(bsz, nblk),
        in_specs=in_specs,
        out_specs=[pl.BlockSpec((rows, D_REC), fwd(0)), pl.BlockSpec((rows, D_REC), bwd(0)),
                   head_blocks, head_blocks, state_m],
        out_shape=[jax.ShapeDtypeStruct((bsz * t, D_REC), F32), jax.ShapeDtypeStruct((bsz * t, D_REC), F32),
                   jax.ShapeDtypeStruct((bsz, 2, N_HEADS, D_STATE, D_STATE), F32),
                   jax.ShapeDtypeStruct((bsz, 2, N_HEADS, D_STATE, D_STATE), F32),
                   jax.ShapeDtypeStruct((bsz, 2, 1, HEAD_LANES), F32)],
        scratch_shapes=[pltpu.VMEM((2, HEAD_LANES, 2 * HEAD_LANES), F32),
                        pltpu.VMEM((2, 1, HEAD_LANES), F32)],
        compiler_params=_params(("parallel", "arbitrary")),
        name="mlstm_scan",
    )(*args)


def _conv_kernel(x_ref, prev_ref, next_ref, w_ref, b_ref, o_ref, *, nblk):
    i = pl.program_id(1)
    x = x_ref[...]
    rows = x.shape[0]
    r = lax.broadcasted_iota(jnp.int32, x.shape, 0)
    prev_row = prev_ref[7:8, :] * (i > 0).astype(F32)
    next_row = next_ref[0:1, :] * (i < nblk - 1).astype(F32)
    xm = jnp.where(r == 0, prev_row, pltpu.roll(x, 1, 0))
    xp = jnp.where(r == rows - 1, next_row, pltpu.roll(x, rows - 1, 0))
    w = w_ref[...]
    o_ref[...] = _silu(xm * w[0:1] + x * w[1:2] + xp * w[2:3] + b_ref[...])


def _ssd_conv(u, conv_w, conv_b, bsz, t):
    rows = min(t, 512)
    nblk = t // rows
    r8 = rows // 8
    width = 2 * D_REC
    c0 = COL_SX // width
    return pl.pallas_call(
        functools.partial(_conv_kernel, nblk=nblk),
        grid=(bsz, nblk),
        in_specs=[pl.BlockSpec((rows, width), lambda b, i: (b * nblk + i, c0)),
                  pl.BlockSpec((8, width), lambda b, i: (jnp.maximum((b * nblk + i) * r8 - 1, 0), c0)),
                  pl.BlockSpec((8, width), lambda b, i: (jnp.minimum((b * nblk + i + 1) * r8,
                                                                     bsz * nblk * r8 - 1), c0)),
                  pl.BlockSpec((D_CONV, width), lambda b, i: (0, 0)),
                  pl.BlockSpec((1, width), lambda b, i: (0, 0))],
        out_specs=pl.BlockSpec((rows, width), lambda b, i: (b * nblk + i, 0)),
        out_shape=jax.ShapeDtypeStruct((bsz * t, width), F32),
        compiler_params=_params(("parallel", "parallel")),
        name="ssd_conv",
    )(u, u, u, conv_w, conv_b.reshape(1, width))


def _ssd_chunk(x4, bcm, dt128, da128, d, sg_prev, reads, tri, b_sel, s_sel, bd):
    chans = [GATE_DT + d * N_HEADS + h for h in range(N_HEADS)]
    dt = _spread(dt128, chans)
    da = _spread(da128, chans)
    ac = jnp.dot(tri[d], da, precision=HIGHEST, preferred_element_type=F32)
    atot = jnp.sum(da, 0, keepdims=True)
    a_row = jnp.sum(jnp.where(reads[1 - d], da, 0.0), 0, keepdims=True)
    decay = jnp.exp(jnp.where(reads[d], ac - a_row, -jnp.inf))
    bmat = bcm[:, :LANES].astype(BF16)
    cmat = bcm[:, LANES:].astype(BF16)
    bbd = jnp.where(b_sel, _stack_heads(bmat), 0)
    g4 = lax.dot_general(cmat, bbd, NT_DIMS, preferred_element_type=F32)
    xbd = jnp.where(bd, _stack_heads((x4 * dt).astype(BF16)), 0)
    y = (jnp.dot((g4 * decay).astype(BF16), xbd, preferred_element_type=F32)
         + jnp.dot(cmat, sg_prev.astype(BF16), preferred_element_type=F32) * jnp.exp(ac))
    w = jnp.exp(atot - ac) * dt
    dsg = lax.dot_general(bmat, (x4 * w).astype(BF16), TN_DIMS, preferred_element_type=F32)
    sg_new = jnp.exp(atot) * sg_prev + jnp.where(s_sel, dsg, 0.0)
    return y, sg_new


def _ssd_kernel(*refs, nblk, rows, zero_init):
    if zero_init:
        (xf_ref, bcf_ref, gf_ref, xb_ref, bcb_ref, gb_ref, dtb_ref, alog_ref, dskip_ref,
         yf_ref, yb_ref, sout_ref, s_scr) = refs
    else:
        (xf_ref, bcf_ref, gf_ref, xb_ref, bcb_ref, gb_ref, dtb_ref, alog_ref, dskip_ref, s0_ref,
         yf_ref, yb_ref, sout_ref, s_scr) = refs
    j = pl.program_id(1)
    nchunk = rows // CHUNK

    @pl.when(j == 0)
    def _():
        if zero_init:
            s_scr[...] = jnp.zeros_like(s_scr)
        else:
            s_scr[...] = s0_ref[0]

    reads, tri, _ = _scan_consts()
    bd = _group_mask(HEAD_LANES, HEAD_LANES, 0, 0)
    b_sel = _group_mask(HEAD_LANES, LANES, 1, 0)
    s_sel = _group_mask(LANES, HEAD_LANES, 0, 1)
    a_coef = -jnp.exp(alog_ref[...])

    def chunk(ci, carry):
        for d in range(2):
            x_ref, bc_ref, g_ref, y_ref = ((xf_ref, bcf_ref, gf_ref, yf_ref) if d == 0
                                           else (xb_ref, bcb_ref, gb_ref, yb_ref))
            cj = ci if d == 0 else nchunk - 1 - ci
            rs = pl.ds(pl.multiple_of(cj * CHUNK, CHUNK), CHUNK)
            dt128 = jax.nn.softplus(g_ref[rs, :] + dtb_ref[...])
            x4 = x_ref[rs, :]
            y, sg_new = _ssd_chunk(x4, bc_ref[rs, :], dt128, dt128 * a_coef, d, s_scr[d],
                                   reads, tri, b_sel, s_sel, bd)
            if d == 0:
                y = y + dskip_ref[...] * x4
            y_ref[rs, :] = y
            s_scr[d] = sg_new
        return carry

    lax.fori_loop(0, nchunk, chunk, 0)

    @pl.when(j == nblk - 1)
    def _():
        for d in range(2):
            for h in range(N_HEADS):
                g0 = (h // 2) * D_STATE
                sout_ref[0, d, h] = s_scr[d, g0:g0 + D_STATE, h * D_STATE:(h + 1) * D_STATE]


def _ssd(xbc, u, dt_bias_row, alog_row, dskip_row, bsz, t, init):
    rows = min(t, ROW_TILE)
    nblk = t // rows
    zero_init = init is None

    def fwd(col):
        return lambda b, j: (b * nblk + j, col)

    def bwd(col):
        return lambda b, j: (b * nblk + nblk - 1 - j, col)

    in_specs = []
    for mk in (fwd, bwd):
        in_specs += [pl.BlockSpec((rows, D_REC), mk(0)), pl.BlockSpec((rows, D_REC), mk(1)),
                     pl.BlockSpec((rows, LANES), mk(COL_SMALL // LANES))]
    in_specs += [pl.BlockSpec((1, LANES), lambda b, j: (0, 0)),
                 pl.BlockSpec((1, LANES), lambda b, j: (0, 0)),
                 pl.BlockSpec((1, D_REC), lambda b, j: (0, 0))]
    args = [xbc, xbc, u, xbc, xbc, u, dt_bias_row, alog_row, dskip_row]
    state = pl.BlockSpec((1, 2, LANES, HEAD_LANES), lambda b, j: (b, 0, 0, 0))
    if not zero_init:
        in_specs.append(state)
        args.append(init)
    return pl.pallas_call(
        functools.partial(_ssd_kernel, nblk=nblk, rows=rows, zero_init=zero_init),
        grid=(bsz, nblk),
        in_specs=in_specs,
        out_specs=[pl.BlockSpec((rows, D_REC), fwd(0)), pl.BlockSpec((rows, D_REC), bwd(0)),
                   pl.BlockSpec((1, 2, N_HEADS, D_STATE, D_STATE), lambda b, j: (b, 0, 0, 0, 0))],
        out_shape=[jax.ShapeDtypeStruct((bsz * t, D_REC), F32), jax.ShapeDtypeStruct((bsz * t, D_REC), F32),
                   jax.ShapeDtypeStruct((bsz, 2, N_HEADS, D_STATE, D_STATE), F32)],
        scratch_shapes=[pltpu.VMEM((2, LANES, HEAD_LANES), F32)],
        compiler_params=_params(("parallel", "arbitrary")),
        name="ssd_scan",
    )(*args)


def _outproj_kernel(att_ref, hf_ref, hb_ref, mo_ref, yf_ref, yb_ref, z_ref, w_ref, x_ref, gate_ref,
                    mnw_ref, snw_ref, lg_ref, lb_ref, o_ref):
    hh = hf_ref[...] + hb_ref[...]
    parts = []
    for h in range(N_HEADS):
        xh = hh[:, h * D_STATE:(h + 1) * D_STATE]
        mu = jnp.mean(xh, -1, keepdims=True)
        dlt = xh - mu
        var = jnp.mean(dlt * dlt, -1, keepdims=True)
        parts.append(dlt * lax.rsqrt(var + EPS))
    ml = jax.nn.sigmoid(mo_ref[...]) * jnp.concatenate(parts, axis=1) * mnw_ref[...]
    yz = (yf_ref[...] + yb_ref[...]) * _silu(z_ref[...])
    parts = []
    for grp in range(N_GROUPS):
        yg = yz[:, grp * LANES:(grp + 1) * LANES]
        parts.append(yg * lax.rsqrt(jnp.mean(yg * yg, -1, keepdims=True) + EPS))
    ssm = jnp.concatenate(parts, axis=1) * snw_ref[...]
    mixed = (jnp.dot(att_ref[...], w_ref[0:D_ATT], preferred_element_type=F32)
             + _bdot(ml, w_ref[D_ATT:D_ATT + D_REC])
             + _bdot(ssm, w_ref[D_ATT + D_REC:D_MODEL]))
    y = ALPHA * x_ref[...] + gate_ref[0] * mixed
    o_ref[...] = _layernorm_rows(y, lg_ref[...], lb_ref[...])


def _out_proj(att, hf, hb, yf, yb, u, w, x, gate, mnw, snw, lg, lb, rows_per_mod):
    n = x.shape[0]
    tpb = rows_per_mod // ROW_TILE
    row = lambda width, col: pl.BlockSpec((ROW_TILE, width), lambda i: (i, col))
    vec = lambda width: pl.BlockSpec((1, width), lambda i: (0, 0))
    return pl.pallas_call(
        _outproj_kernel,
        grid=(n // ROW_TILE,),
        in_specs=[row(D_ATT, 0), row(D_REC, 0), row(D_REC, 0), row(D_REC, COL_MO // D_REC),
                  row(D_REC, 0), row(D_REC, 0), row(D_REC, COL_SZ // D_REC),
                  pl.BlockSpec((D_MODEL, D_MODEL), lambda i: (0, 0)),
                  row(D_MODEL, 0),
                  pl.BlockSpec((1, 1, D_MODEL), lambda i: (i // tpb, 0, 0)),
                  vec(D_REC), vec(D_REC), vec(D_MODEL), vec(D_MODEL)],
        out_specs=row(D_MODEL, 0),
        out_shape=jax.ShapeDtypeStruct((n, D_MODEL), F32),
        compiler_params=_params(("parallel",)),
        name="out_proj",
    )(att, hf, hb, u, yf, yb, u, w, x, gate, mnw.reshape(1, D_REC), snw.reshape(1, D_REC),
      lg.reshape(1, D_MODEL), lb.reshape(1, D_MODEL))


FF_TILE = D_FF // 2
FFN_ROWS = 512


def _swiglu_partial(h, w1, w3, w2):
    a = jnp.dot(h, w1, preferred_element_type=F32)
    b = jnp.dot(h, w3, preferred_element_type=F32)
    return jnp.dot((_silu(a) * b).astype(BF16), w2, preferred_element_type=F32)


def _ffn_kernel(x_ref, sc_ref, sh_ref, gate_ref, w1_ref, w3_ref, w2_ref, lg_ref, lb_ref, o_ref,
                h_scr, acc_scr):
    j = pl.program_id(1)

    @pl.when(j == 0)
    def _():
        h_scr[...] = (x_ref[...] * (1.0 + sc_ref[0]) + sh_ref[0]).astype(BF16)
        acc_scr[...] = jnp.zeros_like(acc_scr)

    acc_scr[...] += _swiglu_partial(h_scr[...], w1_ref[...], w3_ref[...], w2_ref[...])

    @pl.when(j == pl.num_programs(1) - 1)
    def _():
        y = ALPHA * x_ref[...] + gate_ref[0] * acc_scr[...]
        o_ref[...] = _layernorm_rows(y, lg_ref[...], lb_ref[...])


def _ffn(x, sc, sh, gate, w1, w3, w2, lg, lb, rows_per_mod):
    n = x.shape[0]
    tpb = rows_per_mod // FFN_ROWS
    modspec = pl.BlockSpec((1, 1, D_MODEL), lambda i, j: (i // tpb, 0, 0))
    vec = pl.BlockSpec((1, D_MODEL), lambda i, j: (0, 0))
    return pl.pallas_call(
        _ffn_kernel,
        grid=(n // FFN_ROWS, D_FF // FF_TILE),
        in_specs=[pl.BlockSpec((FFN_ROWS, D_MODEL), lambda i, j: (i, 0)), modspec, modspec, modspec,
                  pl.BlockSpec((D_MODEL, FF_TILE), lambda i, j: (0, j)),
                  pl.BlockSpec((D_MODEL, FF_TILE), lambda i, j: (0, j)),
                  pl.BlockSpec((FF_TILE, D_MODEL), lambda i, j: (j, 0)), vec, vec],
        out_specs=pl.BlockSpec((FFN_ROWS, D_MODEL), lambda i, j: (i, 0)),
        out_shape=jax.ShapeDtypeStruct((n, D_MODEL), F32),
        scratch_shapes=[pltpu.VMEM((FFN_ROWS, D_MODEL), BF16), pltpu.VMEM((FFN_ROWS, D_MODEL), F32)],
        compiler_params=_params(("parallel", "arbitrary")),
        name="ffn_dense",
    )(x, sc, sh, gate, w1, w3, w2, lg.reshape(1, D_MODEL), lb.reshape(1, D_MODEL))


def _router_kernel(x_ref, sc_ref, sh_ref, rw_ref, gates_ref, h_ref):
    h = x_ref[...] * (1.0 + sc_ref[0]) + sh_ref[0]
    h_ref[...] = h.astype(BF16)
    logits = jnp.dot(h, rw_ref[...], precision=HIGHEST, preferred_element_type=F32)
    lane = lax.broadcasted_iota(jnp.int32, logits.shape, 1)
    valid = lane < N_EXPERTS
    p = jnp.where(valid, _softmax_rows(jnp.where(valid, logits, -jnp.inf)), -2.0)
    p1 = jnp.max(p, -1, keepdims=True)
    i1 = jnp.min(jnp.where(p == p1, lane, LANES), -1, keepdims=True)
    rest = jnp.where(lane == i1, -1.0, p)
    p2 = jnp.max(rest, -1, keepdims=True)
    i2 = jnp.min(jnp.where(rest == p2, lane, LANES), -1, keepdims=True)
    tot = p1 + p2
    gates_ref[...] = jnp.where(lane == i1, p1 / tot, jnp.where(lane == i2, p2 / tot, 0.0))


def _router(x, sc, sh, router_w, rows_per_mod):
    n = x.shape[0]
    tpb = rows_per_mod // ROW_TILE
    modspec = pl.BlockSpec((1, 1, D_MODEL), lambda i: (i // tpb, 0, 0))
    rw = jnp.pad(router_w, ((0, 0), (0, LANES - N_EXPERTS)))
    return pl.pallas_call(
        _router_kernel,
        grid=(n // ROW_TILE,),
        in_specs=[pl.BlockSpec((ROW_TILE, D_MODEL), lambda i: (i, 0)), modspec, modspec,
                  pl.BlockSpec((D_MODEL, LANES), lambda i: (0, 0))],
        out_specs=[pl.BlockSpec((ROW_TILE, LANES), lambda i: (i, 0)),
                   pl.BlockSpec((ROW_TILE, D_MODEL), lambda i: (i, 0))],
        out_shape=[jax.ShapeDtypeStruct((n, LANES), F32), jax.ShapeDtypeStruct((n, D_MODEL), BF16)],
        compiler_params=_params(("parallel",)),
        name="router",
    )(x, sc, sh, rw)


MOE_ROWS = 1024
MOE_TILE = 288
MOE_MAX_TILES = -(-MOE_ROWS // MOE_TILE)


def _moe_kernel(h_ref, gates_ref, w1_ref, w3_ref, w2_ref, o_ref, slot_scr, slott_scr, hs_scr, ys_scr):
    e = pl.program_id(1)
    j = pl.program_id(2)
    last_j = pl.num_programs(2) - 1

    @pl.when(jnp.logical_and(e == 0, j == 0))
    def _():
        mask = gates_ref[...] != 0.0
        r = lax.broadcasted_iota(jnp.int32, (MOE_ROWS, MOE_ROWS), 0)
        c = lax.broadcasted_iota(jnp.int32, (MOE_ROWS, MOE_ROWS), 1)
        before = jnp.where(c < r, 1.0, 0.0).astype(BF16)
        rank = jnp.dot(before, jnp.where(mask, 1.0, 0.0).astype(BF16), preferred_element_type=F32)
        slot = jnp.where(mask, rank, -1.0).astype(jnp.int32)
        slot_scr[...] = slot
        slott_scr[...] = slot.T
        o_ref[...] = jnp.zeros_like(o_ref)

    lane = lax.broadcasted_iota(jnp.int32, (MOE_ROWS, LANES), 1)
    slot_col = jnp.max(jnp.where(lane == e, slot_scr[...], -1), -1, keepdims=True)
    n_tiles = (jnp.max(slot_col) + MOE_TILE) // MOE_TILE

    for k in range(MOE_MAX_TILES):
        @pl.when(jnp.logical_and(j == 0, k < n_tiles))
        def _():
            slot_row = slott_scr[pl.ds(e, 1), :]
            rr = lax.broadcasted_iota(jnp.int32, (MOE_TILE, MOE_ROWS), 0) + k * MOE_TILE
            pick = jnp.where(rr == slot_row, 1.0, 0.0).astype(BF16)
            hs_scr[k] = jnp.dot(pick, h_ref[...], preferred_element_type=F32).astype(BF16)
            ys_scr[k] = jnp.zeros((MOE_TILE, D_MODEL), F32)

        @pl.when(k < n_tiles)
        def _():
            ys_scr[k] += _swiglu_partial(hs_scr[k], w1_ref[0], w3_ref[0], w2_ref[0])

        @pl.when(jnp.logical_and(j == last_j, k < n_tiles))
        def _():
            g_col = jnp.sum(jnp.where(lane == e, gates_ref[...], 0.0), -1, keepdims=True)
            cc = lax.broadcasted_iota(jnp.int32, (MOE_ROWS, MOE_TILE), 1) + k * MOE_TILE
            put = jnp.where(slot_col == cc, 1.0, 0.0).astype(BF16)
            y = ys_scr[k]
            y_hi = y.astype(BF16)
            y_lo = (y - y_hi.astype(F32)).astype(BF16)
            back = (jnp.dot(put, y_hi, preferred_element_type=F32)
                    + jnp.dot(put, y_lo, preferred_element_type=F32))
            o_ref[...] += g_col * back


def _moe(h, gates, w1, w3, w2):
    n = h.shape[0]
    return pl.pallas_call(
        _moe_kernel,
        grid=(n // MOE_ROWS, N_EXPERTS, D_FF // FF_TILE),
        in_specs=[pl.BlockSpec((MOE_ROWS, D_MODEL), lambda i, e, j: (i, 0)),
                  pl.BlockSpec((MOE_ROWS, LANES), lambda i, e, j: (i, 0)),
                  pl.BlockSpec((1, D_MODEL, FF_TILE), lambda i, e, j: (e, 0, j)),
                  pl.BlockSpec((1, D_MODEL, FF_TILE), lambda i, e, j: (e, 0, j)),
                  pl.BlockSpec((1, FF_TILE, D_MODEL), lambda i, e, j: (e, j, 0))],
        out_specs=pl.BlockSpec((MOE_ROWS, D_MODEL), lambda i, e, j: (i, 0)),
        out_shape=jax.ShapeDtypeStruct((n, D_MODEL), F32),
        scratch_shapes=[pltpu.VMEM((MOE_ROWS, LANES), jnp.int32), pltpu.VMEM((LANES, MOE_ROWS), jnp.int32),
                        pltpu.VMEM((MOE_MAX_TILES, MOE_TILE, D_MODEL), BF16),
                        pltpu.VMEM((MOE_MAX_TILES, MOE_TILE, D_MODEL), F32)],
        compiler_params=_params(("parallel", "arbitrary", "arbitrary")),
        name="moe",
    )(h, gates, w1, w3, w2)


def _residual_ln_kernel(x_ref, f_ref, gate_ref, lg_ref, lb_ref, o_ref):
    o_ref[...] = _layernorm_rows(ALPHA * x_ref[...] + gate_ref[0] * f_ref[...], lg_ref[...], lb_ref[...])


def _residual_ln(x, f, gate, lg, lb, rows_per_mod):
    n = x.shape[0]
    tpb = rows_per_mod // ROW_TILE
    row = pl.BlockSpec((ROW_TILE, D_MODEL), lambda i: (i, 0))
    vec = pl.BlockSpec((1, D_MODEL), lambda i: (0, 0))
    return pl.pallas_call(
        _residual_ln_kernel,
        grid=(n // ROW_TILE,),
        in_specs=[row, row, pl.BlockSpec((1, 1, D_MODEL), lambda i: (i // tpb, 0, 0)), vec, vec],
        out_specs=row,
        out_shape=jax.ShapeDtypeStruct((n, D_MODEL), F32),
        compiler_params=_params(("parallel",)),
        name="residual_ln",
    )(x, f, gate, lg.reshape(1, D_MODEL), lb.reshape(1, D_MODEL))


def _permute_w_in(w):
    pad = jnp.zeros((D_MODEL, U_COLS - ORIG_END), w.dtype)
    return jnp.concatenate([w[:, :ORIG_GATES], w[:, ORIG_SX:ORIG_DT], w[:, ORIG_SZ:ORIG_SX],
                            w[:, ORIG_GATES:ORIG_SZ], w[:, ORIG_DT:ORIG_END], pad], axis=1).astype(BF16)


def _small_row(vals, offset):
    v = vals.reshape(-1).astype(F32)
    return jnp.zeros((1, LANES), F32).at[0, offset:offset + v.shape[0]].set(v)


def _pack_mlstm_state(c, n, m):
    eye = jnp.eye(N_HEADS, dtype=F32)
    shape = c.shape[:2] + (HEAD_LANES, HEAD_LANES)
    cbd = jnp.einsum('bdhke,hg->bdhkge', c, eye).reshape(shape)
    nbd = jnp.einsum('bdhk,hg,e->bdhkge', n, eye, jnp.ones((D_STATE,), F32)).reshape(shape)
    return jnp.concatenate([cbd, nbd], axis=-1), jnp.repeat(m, D_STATE, axis=-1)[:, :, None, :]


def _pack_ssd_state(s):
    sel = (jnp.arange(N_GROUPS)[:, None] == jnp.arange(N_HEADS)[None, :] // 2).astype(F32)
    return jnp.einsum('bdhpn,gh->bdgnhp', s, sel).reshape(s.shape[:2] + (LANES, HEAD_LANES))


def _layer(x, mods, P, l, bsz, t, ctx):
    sh1, sc1, g1, sh2, sc2, g2 = mods
    rows_per_mod = x.shape[0] // sh1.shape[0]
    lam_init = 0.8 - 0.6 * math.exp(-0.3 * l)
    u = _in_proj(x, sc1, sh1, P['w_in'][l], rows_per_mod)

    gate_bias = (_small_row(P['mlstm_gate_b'][l, 0], GATE_I) + _small_row(P['mlstm_gate_b'][l, 1], GATE_F))
    dt_bias = _small_row(P['ssm_dt_bias'][l], GATE_DT)
    alog = _small_row(P['ssm_A_log'][l], GATE_DT)
    dskip = jnp.repeat(P['ssm_D'][l].astype(F32), D_STATE).reshape(1, D_REC)

    if ctx is None:
        att, k_new, v_new = _attention_ctx(u, P['attn_lambda'][l], P['attn_norm_w'][l], lam_init, bsz, t)
        m_init = s_init = None
    else:
        ck, cv, c_c, c_n, c_m, c_s = ctx
        q, k, v = _rope_prep(u, bsz, t)
        k_all = jnp.concatenate([k, ck.astype(BF16)], axis=3)
        v_all = jnp.concatenate([v, cv.astype(BF16)], axis=2)
        att = _attention_lat(q, k_all, v_all, P['attn_lambda'][l], P['attn_norm_w'][l], lam_init, bsz, t)
        m_init = _pack_mlstm_state(c_c, c_n, c_m)
        s_init = _pack_ssd_state(c_s)
    hf, hb, c_out, n_out, m_out = _mlstm(u, gate_bias, bsz, t, m_init)
    xbc = _ssd_conv(u, P['conv_w'][l], P['conv_b'][l], bsz, t)
    yf, yb, s_out = _ssd(xbc, u, dt_bias, alog, dskip, bsz, t, s_init)

    x = _out_proj(att, hf, hb, yf, yb, u, P['w_out'][l], x, g1, P['mlstm_norm_w'][l], P['ssm_norm_w'][l],
                  P['ln_g'][l, 0], P['ln_b'][l, 0], rows_per_mod)
    if l % 2 == 0:
        x = _ffn(x, sc2, sh2, g2, P['ffn_w1'][l // 2], P['ffn_w3'][l // 2], P['ffn_w2'][l // 2],
                 P['ln_g'][l, 1], P['ln_b'][l, 1], rows_per_mod)
    else:
        gates, h2 = _router(x, sc2, sh2, P['router_w'][l // 2], rows_per_mod)
        f = _moe(h2, gates, P['moe_w1'][l // 2], P['moe_w3'][l // 2], P['moe_w2'][l // 2])
        x = _residual_ln(x, f, g2, P['ln_g'][l, 1], P['ln_b'][l, 1], rows_per_mod)
    if ctx is None:
        return x, (k_new, v_new, c_out, n_out[..., 0], m_out[:, :, 0, ::D_STATE], jnp.swapaxes(s_out, -1, -2))
    return x, None


def kernel(x_prompt, x_sample, c, cache_attn_k, cache_attn_v, state_mlstm_C, state_mlstm_n, state_mlstm_m, state_ssm, c_ctx, w_ada, b_ada, w_in, w_out, attn_lambda, attn_norm_w, mlstm_gate_b, mlstm_norm_w, conv_w, conv_b, ssm_A_log, ssm_dt_bias, ssm_D, ssm_norm_w, ln_g, ln_b, ffn_w1, ffn_w3, ffn_w2, router_w, moe_w1, moe_w3, moe_w2):
    bsz, seq, _ = x_prompt.shape
    dbsz, dseq, _ = x_sample.shape
    P = dict(w_in=[_permute_w_in(w_in[l]) for l in range(DEPTH)], w_out=w_out.astype(BF16),
             attn_lambda=attn_lambda, attn_norm_w=attn_norm_w, mlstm_gate_b=mlstm_gate_b,
             mlstm_norm_w=mlstm_norm_w, conv_w=conv_w, conv_b=conv_b, ssm_A_log=ssm_A_log,
             ssm_dt_bias=ssm_dt_bias, ssm_D=ssm_D, ssm_norm_w=ssm_norm_w, ln_g=ln_g, ln_b=ln_b,
             ffn_w1=ffn_w1.astype(BF16), ffn_w3=ffn_w3.astype(BF16), ffn_w2=ffn_w2.astype(BF16),
             router_w=router_w, moe_w1=moe_w1.astype(BF16), moe_w3=moe_w3.astype(BF16),
             moe_w2=moe_w2.astype(BF16))

    cvec = jnp.zeros((8, D_MODEL), F32).at[0].set(c_ctx).at[1:1 + dbsz].set(c)
    mod = _modulation(cvec, w_ada, b_ada)

    def mods_for(l, lo, hi):
        return [mod[l, lo:hi, i * D_MODEL:(i + 1) * D_MODEL][:, None, :] for i in range(6)]

    y_prompt = x_prompt.reshape(bsz * seq, D_MODEL)
    outs = []
    for l in range(DEPTH):
        y_prompt, ctx_out = _layer(y_prompt, mods_for(l, 0, 1), P, l, bsz, seq, None)
        outs.append(ctx_out)

    y_sample = x_sample.reshape(dbsz * dseq, D_MODEL)
    for l in range(DEPTH):
        ctx = (cache_attn_k[:, l], cache_attn_v[:, l], state_mlstm_C[:, l], state_mlstm_n[:, l],
               state_mlstm_m[:, l], state_ssm[:, l])
        y_sample, _ = _layer(y_sample, mods_for(l, 1, 1 + dbsz), P, l, dbsz, dseq, ctx)

    stacked = [jnp.stack([o[i] for o in outs], axis=1) for i in range(6)]
    return (y_prompt.reshape(bsz, seq, D_MODEL), y_sample.reshape(dbsz, dseq, D_MODEL), *stacked)
```

```python
import functools
import math

import jax
import jax.numpy as jnp
from jax import lax
from jax.experimental import pallas as pl
from jax.experimental.pallas import tpu as pltpu

F32 = jnp.float32
BF16 = jnp.bfloat16
HIGHEST = lax.Precision.HIGHEST

D_MODEL = 1024
DEPTH = 2
GRID_W = 64
N_HEADS = 4
D_ATT = 512
D_HEAD_V = 128
D_QK = 64
D_REC = 256
D_STATE = 64
N_GROUPS = 2
D_CONV = 3
D_FF = 2816
N_EXPERTS = 8
ALPHA = (2.0 * DEPTH) ** 0.25
CHUNK = 64
ROPE_BASE = 10000.0
LOG2E = 1.4426950408889634
EPS = 1e-5

COL_AQ, COL_AK, COL_AV = 0, 512, 1024
COL_MQ, COL_MK, COL_MV, COL_MO = 1536, 1792, 2048, 2304
COL_SX, COL_SBC, COL_SZ = 2560, 2816, 3072
COL_SMALL = 3328
U_COLS = 3584
ORIG_GATES, ORIG_SZ, ORIG_SX, ORIG_DT, ORIG_END = 2560, 2576, 2832, 3344, 3352
GATE_I, GATE_F, GATE_DT = 0, 8, 16

LANES = 128
ROW_TILE = 256
VMEM_LIMIT = 48 * 1024 * 1024

NT_DIMS = (((1,), (1,)), ((), ()))
TN_DIMS = (((0,), (0,)), ((), ()))


def _params(sem, vmem=VMEM_LIMIT):
    return pltpu.CompilerParams(dimension_semantics=sem, vmem_limit_bytes=vmem)


def _silu(x):
    return x * jax.nn.sigmoid(x)


def _bdot(a, b):
    return jnp.dot(a.astype(BF16), b.astype(BF16), preferred_element_type=F32)


def _bdot_nt(a, b):
    return lax.dot_general(a.astype(BF16), b.astype(BF16), NT_DIMS, preferred_element_type=F32)


def _bdot_tn(a, b):
    return lax.dot_general(a.astype(BF16), b.astype(BF16), TN_DIMS, preferred_element_type=F32)


def _layernorm_rows(y, g, b):
    mu = jnp.mean(y, -1, keepdims=True)
    d = y - mu
    var = jnp.mean(d * d, -1, keepdims=True)
    return d * lax.rsqrt(var + EPS) * g + b


def _mod_kernel(c_ref, w_ref, b_ref, o_ref):
    o_ref[0] = jnp.dot(_silu(c_ref[...]), w_ref[0], precision=HIGHEST,
                       preferred_element_type=F32) + b_ref[0]


def _modulation(cvec, w_ada, b_ada):
    tn = 1536
    return pl.pallas_call(
        _mod_kernel,
        grid=(DEPTH, 6 * D_MODEL // tn),
        in_specs=[pl.BlockSpec((8, D_MODEL), lambda l, j: (0, 0)),
                  pl.BlockSpec((1, D_MODEL, tn), lambda l, j: (l, 0, j)),
                  pl.BlockSpec((1, 1, tn), lambda l, j: (l, 0, j))],
        out_specs=pl.BlockSpec((1, 8, tn), lambda l, j: (l, 0, j)),
        out_shape=jax.ShapeDtypeStruct((DEPTH, 8, 6 * D_MODEL), F32),
        compiler_params=_params(("parallel", "parallel")),
        name="modulation",
    )(cvec, w_ada, b_ada.reshape(DEPTH, 1, 6 * D_MODEL))


def _inproj_kernel(x_ref, sc_ref, sh_ref, w_ref, o_ref):
    h = (x_ref[...] * (1.0 + sc_ref[0]) + sh_ref[0]).astype(BF16)
    for n0 in range(0, U_COLS, 512):
        o_ref[:, n0:n0 + 512] = jnp.dot(h, w_ref[:, n0:n0 + 512], preferred_element_type=F32)


def _in_proj(x, sc, sh, w, rows_per_mod):
    n = x.shape[0]
    tpb = rows_per_mod // ROW_TILE
    return pl.pallas_call(
        _inproj_kernel,
        grid=(n // ROW_TILE,),
        in_specs=[pl.BlockSpec((ROW_TILE, D_MODEL), lambda i: (i, 0)),
                  pl.BlockSpec((1, 1, D_MODEL), lambda i: (i // tpb, 0, 0)),
                  pl.BlockSpec((1, 1, D_MODEL), lambda i: (i // tpb, 0, 0)),
                  pl.BlockSpec((D_MODEL, U_COLS), lambda i: (0, 0))],
        out_specs=pl.BlockSpec((ROW_TILE, U_COLS), lambda i: (i, 0)),
        out_shape=jax.ShapeDtypeStruct((n, U_COLS), F32),
        compiler_params=_params(("parallel",)),
        name="in_proj",
    )(x, sc, sh, w)


def _lambda_scalar(lam_ref, lam_init):
    lp = lam_ref[...]
    s01 = jnp.sum(lp[0:1] * lp[1:2], axis=-1, keepdims=True)
    s23 = jnp.sum(lp[2:3] * lp[3:4], axis=-1, keepdims=True)
    return jnp.exp(s01) - jnp.exp(s23) + lam_init


def _softmax_rows(s):
    e = jnp.exp(s - jnp.max(s, -1, keepdims=True))
    return e / jnp.sum(e, -1, keepdims=True)


def _head_norm(o, nw, lam_init):
    return o * lax.rsqrt(jnp.mean(o * o, -1, keepdims=True) + EPS) * nw * (1.0 - lam_init)


def _attn_ctx_kernel(u_ref, lam_ref, nw_ref, att_ref, k_ref, v_ref, *, lam_init):
    lam = _lambda_scalar(lam_ref, lam_init)
    for h in range(N_HEADS):
        v = u_ref[:, COL_AV + h * D_HEAD_V:COL_AV + (h + 1) * D_HEAD_V]
        v_ref[0, h] = v
        ps = []
        for m in range(2):
            c0 = h * D_HEAD_V + m * D_QK
            q = u_ref[:, COL_AQ + c0:COL_AQ + c0 + D_QK] * (D_QK ** -0.5)
            k = u_ref[:, COL_AK + c0:COL_AK + c0 + D_QK]
            k_ref[0, h, m] = k
            ps.append(_softmax_rows(_bdot_nt(q, k)))
        o = _bdot(ps[0] - lam * ps[1], v)
        att_ref[:, h * D_HEAD_V:(h + 1) * D_HEAD_V] = _head_norm(o, nw_ref[...], lam_init).astype(BF16)


def _attention_ctx(u, lam_p, norm_w, lam_init, bsz, t):
    n = bsz * t
    return pl.pallas_call(
        functools.partial(_attn_ctx_kernel, lam_init=lam_init),
        grid=(bsz,),
        in_specs=[pl.BlockSpec((t, 3 * D_ATT), lambda b: (b, 0)),
                  pl.BlockSpec((4, D_QK), lambda b: (0, 0)),
                  pl.BlockSpec((1, D_HEAD_V), lambda b: (0, 0))],
        out_specs=[pl.BlockSpec((t, D_ATT), lambda b: (b, 0)),
                   pl.BlockSpec((1, N_HEADS, 2, t, D_QK), lambda b: (b, 0, 0, 0, 0)),
                   pl.BlockSpec((1, N_HEADS, t, D_HEAD_V), lambda b: (b, 0, 0, 0))],
        out_shape=[jax.ShapeDtypeStruct((n, D_ATT), BF16),
                   jax.ShapeDtypeStruct((bsz, N_HEADS, 2, t, D_QK), F32),
                   jax.ShapeDtypeStruct((bsz, N_HEADS, t, D_HEAD_V), F32)],
        compiler_params=_params(("parallel",)),
        name="attn_ctx",
    )(u, lam_p, norm_w.reshape(1, D_HEAD_V))


def _rope_kernel(u_ref, cos_ref, sa_ref, sb_ref, q_ref, k_ref, v_ref):
    cos, sa, sb = cos_ref[...], sa_ref[...], sb_ref[...]

    def rope(x):
        return x * cos + pltpu.roll(x, LANES - 16, 1) * sa + pltpu.roll(x, 16, 1) * sb

    for h in range(N_HEADS):
        q = rope(u_ref[:, COL_AQ + h * D_HEAD_V:COL_AQ + (h + 1) * D_HEAD_V]) * (LOG2E * D_QK ** -0.5)
        k = rope(u_ref[:, COL_AK + h * D_HEAD_V:COL_AK + (h + 1) * D_HEAD_V])
        for m in range(2):
            q_ref[0, h, m] = q[:, m * D_QK:(m + 1) * D_QK].astype(BF16)
            k_ref[0, h, m] = k[:, m * D_QK:(m + 1) * D_QK].astype(BF16)
        v_ref[0, h] = u_ref[:, COL_AV + h * D_HEAD_V:COL_AV + (h + 1) * D_HEAD_V].astype(BF16)


def _rope_tables(t):
    rows = jnp.repeat(jnp.arange(t // GRID_W, dtype=F32), GRID_W)
    cols = jnp.tile(jnp.arange(GRID_W, dtype=F32), t // GRID_W)
    half = D_QK // 2
    inv = ROPE_BASE ** (-jnp.arange(0, half, 2, dtype=F32) / half)
    ang_r = rows[:, None] * inv
    ang_c = cols[:, None] * inv
    ang = jnp.concatenate([ang_r, ang_r, ang_c, ang_c], -1)
    cos, sin = jnp.cos(ang), jnp.sin(ang)
    quarter = (jnp.arange(D_QK) // (D_QK // 4)) % 2
    sa = jnp.where(quarter == 0, -sin, 0.0)
    sb = jnp.where(quarter == 1, sin, 0.0)
    tile2 = lambda a: jnp.concatenate([a, a], -1)
    return tile2(cos), tile2(sa), tile2(sb)


def _rope_prep(u, bsz, t):
    tr = 512
    nb = t // tr
    cos, sa, sb = _rope_tables(t)
    tab = pl.BlockSpec((tr, LANES), lambda b, i: (i, 0))
    return pl.pallas_call(
        _rope_kernel,
        grid=(bsz, nb),
        in_specs=[pl.BlockSpec((tr, 3 * D_ATT), lambda b, i: (b * nb + i, 0)), tab, tab, tab],
        out_specs=[pl.BlockSpec((1, N_HEADS, 2, tr, D_QK), lambda b, i: (b, 0, 0, i, 0)),
                   pl.BlockSpec((1, N_HEADS, 2, tr, D_QK), lambda b, i: (b, 0, 0, i, 0)),
                   pl.BlockSpec((1, N_HEADS, tr, D_HEAD_V), lambda b, i: (b, 0, i, 0))],
        out_shape=[jax.ShapeDtypeStruct((bsz, N_HEADS, 2, t, D_QK), BF16),
                   jax.ShapeDtypeStruct((bsz, N_HEADS, 2, t, D_QK), BF16),
                   jax.ShapeDtypeStruct((bsz, N_HEADS, t, D_HEAD_V), BF16)],
        compiler_params=_params(("parallel", "parallel")),
        name="rope_prep",
    )(u, cos, sa, sb)


def _attn_lat_kernel(q_ref, k_ref, v_ref, lam_ref, nw_ref, o_ref, *, lam_init):
    lam = _lambda_scalar(lam_ref, lam_init)
    es, sums = [], []
    for m in range(2):
        s = lax.dot_general(q_ref[0, 0, m], k_ref[0, 0, m], NT_DIMS, preferred_element_type=F32)
        e = jnp.exp2(s - jnp.max(s, -1, keepdims=True))
        es.append(e)
        sums.append(jnp.sum(e, -1, keepdims=True))
    a = es[0] - (lam * sums[0] / sums[1]) * es[1]
    o = _bdot(a, v_ref[0, 0]) / sums[0]
    o_ref[...] = _head_norm(o, nw_ref[...], lam_init).astype(BF16)


def _attention_lat(q, k_all, v_all, lam_p, norm_w, lam_init, bsz, t):
    tq = 256
    nq = t // tq
    s = k_all.shape[3]
    return pl.pallas_call(
        functools.partial(_attn_lat_kernel, lam_init=lam_init),
        grid=(bsz, N_HEADS, nq),
        in_specs=[pl.BlockSpec((1, 1, 2, tq, D_QK), lambda b, h, i: (b, h, 0, i, 0)),
                  pl.BlockSpec((1, 1, 2, s, D_QK), lambda b, h, i: (b, h, 0, 0, 0)),
                  pl.BlockSpec((1, 1, s, D_HEAD_V), lambda b, h, i: (b, h, 0, 0)),
                  pl.BlockSpec((4, D_QK), lambda b, h, i: (0, 0)),
                  pl.BlockSpec((1, D_HEAD_V), lambda b, h, i: (0, 0))],
        out_specs=pl.BlockSpec((tq, D_HEAD_V), lambda b, h, i: (b * nq + i, h)),
        out_shape=jax.ShapeDtypeStruct((bsz * t, D_ATT), BF16),
        compiler_params=_params(("parallel", "parallel", "parallel")),
        name="attn_lat",
    )(q, k_all, v_all, lam_p, norm_w.reshape(1, D_HEAD_V))


HEAD_LANES = N_HEADS * D_STATE


def _scan_consts():
    t = lax.broadcasted_iota(jnp.int32, (CHUNK, HEAD_LANES), 0)
    s = lax.broadcasted_iota(jnp.int32, (CHUNK, HEAD_LANES), 1) & (CHUNK - 1)
    r = lax.broadcasted_iota(jnp.int32, (CHUNK, CHUNK), 0)
    c = lax.broadcasted_iota(jnp.int32, (CHUNK, CHUNK), 1)
    reads = (s <= t, s >= t)
    tri = (jnp.where(c <= r, 1.0, 0.0), jnp.where(c >= r, 1.0, 0.0))
    return reads, tri, s == t


def _group_mask(rows, cols, row_shift, col_shift):
    r = lax.broadcasted_iota(jnp.int32, (rows, cols), 0) >> 6
    c = (lax.broadcasted_iota(jnp.int32, (rows, cols), 1) >> 6) & (N_HEADS - 1)
    return (r >> row_shift) == (c >> col_shift)


def _spread(x, chans):
    return jnp.concatenate([jnp.broadcast_to(x[:, c:c + 1], (CHUNK, D_STATE)) for c in chans], axis=1)


def _stack_heads(x):
    return jnp.concatenate([x] * N_HEADS, axis=0)


def _mlstm_chunk(q4, k4, v4, g, d, cn_prev, m_prev, reads, tri, eye, bd):
    li = _spread(g, [GATE_I + d * N_HEADS + h for h in range(N_HEADS)])
    lf = _spread(jax.nn.log_sigmoid(g), [GATE_F + d * N_HEADS + h for h in range(N_HEADS)])
    bc = jnp.dot(tri[d], lf, precision=HIGHEST, preferred_element_type=F32)
    btot = jnp.sum(lf, 0, keepdims=True)
    b_row = jnp.sum(jnp.where(reads[1 - d], lf, 0.0), 0, keepdims=True)
    li_row = jnp.sum(jnp.where(eye, li, 0.0), 0, keepdims=True)
    dm = jnp.where(reads[d], bc - b_row + li_row, -jnp.inf)
    rmax = jnp.concatenate(
        [jnp.broadcast_to(jnp.max(dm[:, h * D_STATE:(h + 1) * D_STATE], -1, keepdims=True), (CHUNK, D_STATE))
         for h in range(N_HEADS)], axis=1)
    inter = bc + m_prev
    m_t = jnp.maximum(inter, rmax)
    w_inter = jnp.exp(inter - m_t)
    qs = (q4 * (D_STATE ** -0.5)).astype(BF16)
    kbd = jnp.where(bd[:, :HEAD_LANES], _stack_heads(k4.astype(BF16)), 0)
    s4 = lax.dot_general(qs, kbd, NT_DIMS, preferred_element_type=F32) * jnp.exp(dm - m_t)
    vo = jnp.concatenate([v4.astype(BF16), jnp.ones((CHUNK, HEAD_LANES), BF16)], axis=1)
    vbd = jnp.where(bd, _stack_heads(vo), 0)
    nd = (jnp.concatenate([w_inter, w_inter], axis=1)
          * jnp.dot(qs, cn_prev.astype(BF16), preferred_element_type=F32)
          + jnp.dot(s4.astype(BF16), vbd, preferred_element_type=F32))
    hc = nd[:, :HEAD_LANES] / jnp.maximum(jnp.abs(nd[:, HEAD_LANES:]), jnp.exp(-m_t))
    gcol = btot - bc + li
    m_new = jnp.maximum(btot + m_prev, jnp.max(gcol, 0, keepdims=True))
    w_c = jnp.exp(btot + m_prev - m_new)
    kw = (k4 * jnp.exp(gcol - m_new)).astype(BF16)
    dcn = lax.dot_general(kw, vo, TN_DIMS, preferred_element_type=F32)
    cn_new = jnp.concatenate([w_c, w_c], axis=1) * cn_prev + jnp.where(bd, dcn, 0.0)
    return hc, cn_new, m_new


def _mlstm_kernel(*refs, nblk, rows, zero_init):
    if zero_init:
        (qf_ref, kf_ref, vf_ref, gf_ref, qb_ref, kb_ref, vb_ref, gb_ref, bias_ref,
         hf_ref, hb_ref, cout_ref, nout_ref, mout_ref, cn_scr, m_scr) = refs
    else:
        (qf_ref, kf_ref, vf_ref, gf_ref, qb_ref, kb_ref, vb_ref, gb_ref, bias_ref, c0_ref, m0_ref,
         hf_ref, hb_ref, cout_ref, nout_ref, mout_ref, cn_scr, m_scr) = refs
    j = pl.program_id(1)
    nchunk = rows // CHUNK

    @pl.when(j == 0)
    def _():
        if zero_init:
            cn_scr[...] = jnp.zeros_like(cn_scr)
            m_scr[...] = jnp.zeros_like(m_scr)
        else:
            cn_scr[...] = c0_ref[0]
            m_scr[...] = m0_ref[0]

    reads, tri, eye = _scan_consts()
    bd = _group_mask(HEAD_LANES, 2 * HEAD_LANES, 0, 0)

    def chunk(ci, carry):
        for d in range(2):
            q_ref, k_ref, v_ref, g_ref, h_ref = ((qf_ref, kf_ref, vf_ref, gf_ref, hf_ref) if d == 0
                                                 else (qb_ref, kb_ref, vb_ref, gb_ref, hb_ref))
            cj = ci if d == 0 else nchunk - 1 - ci
            rs = pl.ds(pl.multiple_of(cj * CHUNK, CHUNK), CHUNK)
            g = g_ref[rs, :] + bias_ref[...]
            hc, cn_new, m_new = _mlstm_chunk(q_ref[rs, :], k_ref[rs, :], v_ref[rs, :], g, d,
                                             cn_scr[d], m_scr[d], reads, tri, eye, bd)
            h_ref[rs, :] = hc
            cn_scr[d] = cn_new
            m_scr[d] = m_new
        return carry

    lax.fori_loop(0, nchunk, chunk, 0)

    @pl.when(j == nblk - 1)
    def _():
        for d in range(2):
            for h in range(N_HEADS):
                r0, r1 = h * D_STATE, (h + 1) * D_STATE
                cout_ref[0, d, h] = cn_scr[d, r0:r1, r0:r1]
                nout_ref[0, d, h] = cn_scr[d, r0:r1, HEAD_LANES + r0:HEAD_LANES + r1]
        mout_ref[0] = m_scr[...]


def _mlstm(u, gate_bias, bsz, t, init):
    rows = min(t, ROW_TILE)
    nblk = t // rows
    zero_init = init is None

    def fwd(col):
        return lambda b, j: (b * nblk + j, col)

    def bwd(col):
        return lambda b, j: (b * nblk + nblk - 1 - j, col)

    cq, ck, cv, cg = COL_MQ // D_REC, COL_MK // D_REC, COL_MV // D_REC, COL_SMALL // LANES
    in_specs = []
    for mk in (fwd, bwd):
        in_specs += [pl.BlockSpec((rows, D_REC), mk(cq)), pl.BlockSpec((rows, D_REC), mk(ck)),
                     pl.BlockSpec((rows, D_REC), mk(cv)), pl.BlockSpec((rows, LANES), mk(cg))]
    in_specs.append(pl.BlockSpec((1, LANES), lambda b, j: (0, 0)))
    args = [u] * 8 + [gate_bias]
    state_c = pl.BlockSpec((1, 2, HEAD_LANES, 2 * HEAD_LANES), lambda b, j: (b, 0, 0, 0))
    state_m = pl.BlockSpec((1, 2, 1, HEAD_LANES), lambda b, j: (b, 0, 0, 0))
    head_blocks = pl.BlockSpec((1, 2, N_HEADS, D_STATE, D_STATE), lambda b, j: (b, 0, 0, 0, 0))
    if not zero_init:
        in_specs += [state_c, state_m]
        args += list(init)
    return pl.pallas_call(
        functools.partial(_mlstm_kernel, nblk=nblk, rows=rows, zero_init=zero_init),
        grid=(bsz, nblk),
        in_specs=in_specs,
        out_specs=[pl.BlockSpec((rows, D_REC), fwd(0)), pl.BlockSpec((rows, D_REC), bwd(0)),
                   head_blocks, head_blocks, state_m],
        out_shape=[jax.ShapeDtypeStruct((bsz * t, D_REC), F32), jax.ShapeDtypeStruct((bsz * t, D_REC), F32),
                   jax.ShapeDtypeStruct((bsz, 2, N_HEADS, D_STATE, D_STATE), F32),
                   jax.ShapeDtypeStruct((bsz, 2, N_HEADS, D_STATE, D_STATE), F32),
                   jax.ShapeDtypeStruct((bsz, 2, 1, HEAD_LANES), F32)],
        scratch_shapes=[pltpu.VMEM((2, HEAD_LANES, 2 * HEAD_LANES), F32),
                        pltpu.VMEM((2, 1, HEAD_LANES), F32)],
        compiler_params=_params(("parallel", "arbitrary")),
        name="mlstm_scan",
    )(*args)


def _conv_kernel(x_ref, prev_ref, next_ref, w_ref, b_ref, o_ref, *, nblk):
    i = pl.program_id(1)
    x = x_ref[...]
    rows = x.shape[0]
    r = lax.broadcasted_iota(jnp.int32, x.shape, 0)
    prev_row = prev_ref[7:8, :] * (i > 0).astype(F32)
    next_row = next_ref[0:1, :] * (i < nblk - 1).astype(F32)
    xm = jnp.where(r == 0, prev_row, pltpu.roll(x, 1, 0))
    xp = jnp.where(r == rows - 1, next_row, pltpu.roll(x, rows - 1, 0))
    w = w_ref[...]
    o_ref[...] = _silu(xm * w[0:1] + x * w[1:2] + xp * w[2:3] + b_ref[...])


def _ssd_conv(u, conv_w, conv_b, bsz, t):
    rows = min(t, 512)
    nblk = t // rows
    r8 = rows // 8
    width = 2 * D_REC
    c0 = COL_SX // width
    return pl.pallas_call(
        functools.partial(_conv_kernel, nblk=nblk),
        grid=(bsz, nblk),
        in_specs=[pl.BlockSpec((rows, width), lambda b, i: (b * nblk + i, c0)),
                  pl.BlockSpec((8, width), lambda b, i: (jnp.maximum((b * nblk + i) * r8 - 1, 0), c0)),
                  pl.BlockSpec((8, width), lambda b, i: (jnp.minimum((b * nblk + i + 1) * r8,
                                                                     bsz * nblk * r8 - 1), c0)),
                  pl.BlockSpec((D_CONV, width), lambda b, i: (0, 0)),
                  pl.BlockSpec((1, width), lambda b, i: (0, 0))],
        out_specs=pl.BlockSpec((rows, width), lambda b, i: (b * nblk + i, 0)),
        out_shape=jax.ShapeDtypeStruct((bsz * t, width), F32),
        compiler_params=_params(("parallel", "parallel")),
        name="ssd_conv",
    )(u, u, u, conv_w, conv_b.reshape(1, width))


def _ssd_chunk(x4, bcm, dt128, da128, d, sg_prev, reads, tri, b_sel, s_sel, bd):
    chans = [GATE_DT + d * N_HEADS + h for h in range(N_HEADS)]
    dt = _spread(dt128, chans)
    da = _spread(da128, chans)
    ac = jnp.dot(tri[d], da, precision=HIGHEST, preferred_element_type=F32)
    atot = jnp.sum(da, 0, keepdims=True)
    a_row = jnp.sum(jnp.where(reads[1 - d], da, 0.0), 0, keepdims=True)
    decay = jnp.exp(jnp.where(reads[d], ac - a_row, -jnp.inf))
    bmat = bcm[:, :LANES].astype(BF16)
    cmat = bcm[:, LANES:].astype(BF16)
    bbd = jnp.where(b_sel, _stack_heads(bmat), 0)
    g4 = lax.dot_general(cmat, bbd, NT_DIMS, preferred_element_type=F32)
    xbd = jnp.where(bd, _stack_heads((x4 * dt).astype(BF16)), 0)
    y = (jnp.dot((g4 * decay).astype(BF16), xbd, preferred_element_type=F32)
         + jnp.dot(cmat, sg_prev.astype(BF16), preferred_element_type=F32) * jnp.exp(ac))
    w = jnp.exp(atot - ac) * dt
    dsg = lax.dot_general(bmat, (x4 * w).astype(BF16), TN_DIMS, preferred_element_type=F32)
    sg_new = jnp.exp(atot) * sg_prev + jnp.where(s_sel, dsg, 0.0)
    return y, sg_new


def _ssd_kernel(*refs, nblk, rows, zero_init):
    if zero_init:
        (xf_ref, bcf_ref, gf_ref, xb_ref, bcb_ref, gb_ref, dtb_ref, alog_ref, dskip_ref,
         yf_ref, yb_ref, sout_ref, s_scr) = refs
    else:
        (xf_ref, bcf_ref, gf_ref, xb_ref, bcb_ref, gb_ref, dtb_ref, alog_ref, dskip_ref, s0_ref,
         yf_ref, yb_ref, sout_ref, s_scr) = refs
    j = pl.program_id(1)
    nchunk = rows // CHUNK

    @pl.when(j == 0)
    def _():
        if zero_init:
            s_scr[...] = jnp.zeros_like(s_scr)
        else:
            s_scr[...] = s0_ref[0]

    reads, tri, _ = _scan_consts()
    bd = _group_mask(HEAD_LANES, HEAD_LANES, 0, 0)
    b_sel = _group_mask(HEAD_LANES, LANES, 1, 0)
    s_sel = _group_mask(LANES, HEAD_LANES, 0, 1)
    a_coef = -jnp.exp(alog_ref[...])

    def chunk(ci, carry):
        for d in range(2):
            x_ref, bc_ref, g_ref, y_ref = ((xf_ref, bcf_ref, gf_ref, yf_ref) if d == 0
                                           else (xb_ref, bcb_ref, gb_ref, yb_ref))
            cj = ci if d == 0 else nchunk - 1 - ci
            rs = pl.ds(pl.multiple_of(cj * CHUNK, CHUNK), CHUNK)
            dt128 = jax.nn.softplus(g_ref[rs, :] + dtb_ref[...])
            x4 = x_ref[rs, :]
            y, sg_new = _ssd_chunk(x4, bc_ref[rs, :], dt128, dt128 * a_coef, d, s_scr[d],
                                   reads, tri, b_sel, s_sel, bd)
            if d == 0:
                y = y + dskip_ref[...] * x4
            y_ref[rs, :] = y
            s_scr[d] = sg_new
        return carry

    lax.fori_loop(0, nchunk, chunk, 0)

    @pl.when(j == nblk - 1)
    def _():
        for d in range(2):
            for h in range(N_HEADS):
                g0 = (h // 2) * D_STATE
                sout_ref[0, d, h] = s_scr[d, g0:g0 + D_STATE, h * D_STATE:(h + 1) * D_STATE]


def _ssd(xbc, u, dt_bias_row, alog_row, dskip_row, bsz, t, init):
    rows = min(t, ROW_TILE)
    nblk = t // rows
    zero_init = init is None

    def fwd(col):
        return lambda b, j: (b * nblk + j, col)

    def bwd(col):
        return lambda b, j: (b * nblk + nblk - 1 - j, col)

    in_specs = []
    for mk in (fwd, bwd):
        in_specs += [pl.BlockSpec((rows, D_REC), mk(0)), pl.BlockSpec((rows, D_REC), mk(1)),
                     pl.BlockSpec((rows, LANES), mk(COL_SMALL // LANES))]
    in_specs += [pl.BlockSpec((1, LANES), lambda b, j: (0, 0)),
                 pl.BlockSpec((1, LANES), lambda b, j: (0, 0)),
                 pl.BlockSpec((1, D_REC), lambda b, j: (0, 0))]
    args = [xbc, xbc, u, xbc, xbc, u, dt_bias_row, alog_row, dskip_row]
    state = pl.BlockSpec((1, 2, LANES, HEAD_LANES), lambda b, j: (b, 0, 0, 0))
    if not zero_init:
        in_specs.append(state)
        args.append(init)
    return pl.pallas_call(
        functools.partial(_ssd_kernel, nblk=nblk, rows=rows, zero_init=zero_init),
        grid=(bsz, nblk),
        in_specs=in_specs,
        out_specs=[pl.BlockSpec((rows, D_REC), fwd(0)), pl.BlockSpec((rows, D_REC), bwd(0)),
                   pl.BlockSpec((1, 2, N_HEADS, D_STATE, D_STATE), lambda b, j: (b, 0, 0, 0, 0))],
        out_shape=[jax.ShapeDtypeStruct((bsz * t, D_REC), F32), jax.ShapeDtypeStruct((bsz * t, D_REC), F32),
                   jax.ShapeDtypeStruct((bsz, 2, N_HEADS, D_STATE, D_STATE), F32)],
        scratch_shapes=[pltpu.VMEM((2, LANES, HEAD_LANES), F32)],
        compiler_params=_params(("parallel", "arbitrary")),
        name="ssd_scan",
    )(*args)


def _outproj_kernel(att_ref, hf_ref, hb_ref, mo_ref, yf_ref, yb_ref, z_ref, w_ref, x_ref, gate_ref,
                    mnw_ref, snw_ref, lg_ref, lb_ref, o_ref):
    hh = hf_ref[...] + hb_ref[...]
    parts = []
    for h in range(N_HEADS):
        xh = hh[:, h * D_STATE:(h + 1) * D_STATE]
        mu = jnp.mean(xh, -1, keepdims=True)
        dlt = xh - mu
        var = jnp.mean(dlt * dlt, -1, keepdims=True)
        parts.append(dlt * lax.rsqrt(var + EPS))
    ml = jax.nn.sigmoid(mo_ref[...]) * jnp.concatenate(parts, axis=1) * mnw_ref[...]
    yz = (yf_ref[...] + yb_ref[...]) * _silu(z_ref[...])
    parts = []
    for grp in range(N_GROUPS):
        yg = yz[:, grp * LANES:(grp + 1) * LANES]
        parts.append(yg * lax.rsqrt(jnp.mean(yg * yg, -1, keepdims=True) + EPS))
    ssm = jnp.concatenate(parts, axis=1) * snw_ref[...]
    mixed = (jnp.dot(att_ref[...], w_ref[0:D_ATT], preferred_element_type=F32)
             + _bdot(ml, w_ref[D_ATT:D_ATT + D_REC])
             + _bdot(ssm, w_ref[D_ATT + D_REC:D_MODEL]))
    y = ALPHA * x_ref[...] + gate_ref[0] * mixed
    o_ref[...] = _layernorm_rows(y, lg_ref[...], lb_ref[...])


def _out_proj(att, hf, hb, yf, yb, u, w, x, gate, mnw, snw, lg, lb, rows_per_mod):
    n = x.shape[0]
    tpb = rows_per_mod // ROW_TILE
    row = lambda width, col: pl.BlockSpec((ROW_TILE, width), lambda i: (i, col))
    vec = lambda width: pl.BlockSpec((1, width), lambda i: (0, 0))
    return pl.pallas_call(
        _outproj_kernel,
        grid=(n // ROW_TILE,),
        in_specs=[row(D_ATT, 0), row(D_REC, 0), row(D_REC, 0), row(D_REC, COL_MO // D_REC),
                  row(D_REC, 0), row(D_REC, 0), row(D_REC, COL_SZ // D_REC),
                  pl.BlockSpec((D_MODEL, D_MODEL), lambda i: (0, 0)),
                  row(D_MODEL, 0),
                  pl.BlockSpec((1, 1, D_MODEL), lambda i: (i // tpb, 0, 0)),
                  vec(D_REC), vec(D_REC), vec(D_MODEL), vec(D_MODEL)],
        out_specs=row(D_MODEL, 0),
        out_shape=jax.ShapeDtypeStruct((n, D_MODEL), F32),
        compiler_params=_params(("parallel",)),
        name="out_proj",
    )(att, hf, hb, u, yf, yb, u, w, x, gate, mnw.reshape(1, D_REC), snw.reshape(1, D_REC),
      lg.reshape(1, D_MODEL), lb.reshape(1, D_MODEL))


FF_TILE = D_FF // 2
FFN_ROWS = 1024
FFN_SUB = 512
FFN_VMEM_LIMIT = 56 * 1024 * 1024


def _swiglu_partial(h, w1, w3, w2):
    a = jnp.dot(h, w1, preferred_element_type=F32)
    b = jnp.dot(h, w3, preferred_element_type=F32)
    return jnp.dot((_silu(a) * b).astype(BF16), w2, preferred_element_type=F32)


def _ffn_kernel(x_ref, sc_ref, sh_ref, gate_ref, w1_ref, w3_ref, w2_ref, lg_ref, lb_ref, o_ref,
                h_scr, acc_scr):
    j = pl.program_id(1)

    @pl.when(j == 0)
    def _():
        h_scr[...] = (x_ref[...] * (1.0 + sc_ref[0]) + sh_ref[0]).astype(BF16)
        acc_scr[...] = jnp.zeros_like(acc_scr)

    for r0 in range(0, FFN_ROWS, FFN_SUB):
        rows = slice(r0, r0 + FFN_SUB)
        acc_scr[rows, :] += _swiglu_partial(h_scr[rows, :], w1_ref[...], w3_ref[...], w2_ref[...])

    @pl.when(j == pl.num_programs(1) - 1)
    def _():
        y = ALPHA * x_ref[...] + gate_ref[0] * acc_scr[...]
        o_ref[...] = _layernorm_rows(y, lg_ref[...], lb_ref[...])


def _ffn(x, sc, sh, gate, w1, w3, w2, lg, lb, rows_per_mod):
    n = x.shape[0]
    tpb = rows_per_mod // FFN_ROWS
    modspec = pl.BlockSpec((1, 1, D_MODEL), lambda i, j: (i // tpb, 0, 0))
    vec = pl.BlockSpec((1, D_MODEL), lambda i, j: (0, 0))
    return pl.pallas_call(
        _ffn_kernel,
        grid=(n // FFN_ROWS, D_FF // FF_TILE),
        in_specs=[pl.BlockSpec((FFN_ROWS, D_MODEL), lambda i, j: (i, 0)), modspec, modspec, modspec,
                  pl.BlockSpec((D_MODEL, FF_TILE), lambda i, j: (0, j)),
                  pl.BlockSpec((D_MODEL, FF_TILE), lambda i, j: (0, j)),
                  pl.BlockSpec((FF_TILE, D_MODEL), lambda i, j: (j, 0)), vec, vec],
        out_specs=pl.BlockSpec((FFN_ROWS, D_MODEL), lambda i, j: (i, 0)),
        out_shape=jax.ShapeDtypeStruct((n, D_MODEL), F32),
        scratch_shapes=[pltpu.VMEM((FFN_ROWS, D_MODEL), BF16), pltpu.VMEM((FFN_ROWS, D_MODEL), F32)],
        compiler_params=_params(("parallel", "arbitrary"), FFN_VMEM_LIMIT),
        name="ffn_dense",
    )(x, sc, sh, gate, w1, w3, w2, lg.reshape(1, D_MODEL), lb.reshape(1, D_MODEL))


def _router_kernel(x_ref, sc_ref, sh_ref, rw_ref, gates_ref, h_ref):
    h = x_ref[...] * (1.0 + sc_ref[0]) + sh_ref[0]
    h_ref[...] = h.astype(BF16)
    logits = jnp.dot(h, rw_ref[...], precision=HIGHEST, preferred_element_type=F32)
    lane = lax.broadcasted_iota(jnp.int32, logits.shape, 1)
    valid = lane < N_EXPERTS
    p = jnp.where(valid, _softmax_rows(jnp.where(valid, logits, -jnp.inf)), -2.0)
    p1 = jnp.max(p, -1, keepdims=True)
    i1 = jnp.min(jnp.where(p == p1, lane, LANES), -1, keepdims=True)
    rest = jnp.where(lane == i1, -1.0, p)
    p2 = jnp.max(rest, -1, keepdims=True)
    i2 = jnp.min(jnp.where(rest == p2, lane, LANES), -1, keepdims=True)
    tot = p1 + p2
    gates_ref[...] = jnp.where(lane == i1, p1 / tot, jnp.where(lane == i2, p2 / tot, 0.0))


def _router(x, sc, sh, router_w, rows_per_mod):
    n = x.shape[0]
    tpb = rows_per_mod // ROW_TILE
    modspec = pl.BlockSpec((1, 1, D_MODEL), lambda i: (i // tpb, 0, 0))
    rw = jnp.pad(router_w, ((0, 0), (0, LANES - N_EXPERTS)))
    return pl.pallas_call(
        _router_kernel,
        grid=(n // ROW_TILE,),
        in_specs=[pl.BlockSpec((ROW_TILE, D_MODEL), lambda i: (i, 0)), modspec, modspec,
                  pl.BlockSpec((D_MODEL, LANES), lambda i: (0, 0))],
        out_specs=[pl.BlockSpec((ROW_TILE, LANES), lambda i: (i, 0)),
                   pl.BlockSpec((ROW_TILE, D_MODEL), lambda i: (i, 0))],
        out_shape=[jax.ShapeDtypeStruct((n, LANES), F32), jax.ShapeDtypeStruct((n, D_MODEL), BF16)],
        compiler_params=_params(("parallel",)),
        name="router",
    )(x, sc, sh, rw)


MOE_SUB = 1024
MOE_NSUB = 2
MOE_ROWS = MOE_SUB * MOE_NSUB
MOE_TILE = 288
MOE_VMEM_LIMIT = 56 * 1024 * 1024


def _moe_kernel(h_ref, gates_ref, w1_ref, w3_ref, w2_ref, o_ref, slot_scr, slott_scr, hs_scr, ys_scr):
    e = pl.program_id(1)
    j = pl.program_id(2)
    last_j = pl.num_programs(2) - 1

    def swiglu(rows_bf16):
        return _swiglu_partial(rows_bf16, w1_ref[0], w3_ref[0], w2_ref[0])

    @pl.when(jnp.logical_and(e == 0, j == 0))
    def _():
        r = lax.broadcasted_iota(jnp.int32, (MOE_SUB, MOE_SUB), 0)
        c = lax.broadcasted_iota(jnp.int32, (MOE_SUB, MOE_SUB), 1)
        before = jnp.where(c < r, 1.0, 0.0).astype(BF16)
        for sb in range(MOE_NSUB):
            mask = gates_ref[sb * MOE_SUB:(sb + 1) * MOE_SUB, :] != 0.0
            rank = jnp.dot(before, jnp.where(mask, 1.0, 0.0).astype(BF16), preferred_element_type=F32)
            slot = jnp.where(mask, rank, -1.0).astype(jnp.int32)
            slot_scr[sb] = slot
            slott_scr[sb] = slot.T
        o_ref[...] = jnp.zeros_like(o_ref)

    lane = lax.broadcasted_iota(jnp.int32, (MOE_SUB, LANES), 1)
    for sb in range(MOE_NSUB):
        rows = slice(sb * MOE_SUB, (sb + 1) * MOE_SUB)
        slot_col = jnp.max(jnp.where(lane == e, slot_scr[sb], -1), -1, keepdims=True)
        n_tiles = (jnp.max(slot_col) + MOE_TILE) // MOE_TILE

        def gather(k, sb=sb, rows=rows):
            slot_row = slott_scr[sb, pl.ds(e, 1), :]
            rr = lax.broadcasted_iota(jnp.int32, (MOE_TILE, MOE_SUB), 0) + k * MOE_TILE
            pick = jnp.where(rr == slot_row, 1.0, 0.0).astype(BF16)
            return jnp.dot(pick, h_ref[rows, :], preferred_element_type=F32).astype(BF16)

        def add_back(k, y, rows=rows, slot_col=slot_col):
            g_col = jnp.sum(jnp.where(lane == e, gates_ref[rows, :], 0.0), -1, keepdims=True)
            cc = lax.broadcasted_iota(jnp.int32, (MOE_SUB, MOE_TILE), 1) + k * MOE_TILE
            put = jnp.where(slot_col == cc, 1.0, 0.0).astype(BF16)
            y_hi = y.astype(BF16)
            y_lo = (y - y_hi.astype(F32)).astype(BF16)
            back = (jnp.dot(put, y_hi, preferred_element_type=F32)
                    + jnp.dot(put, y_lo, preferred_element_type=F32))
            o_ref[rows, :] += g_col * back

        @pl.when(jnp.logical_and(j == 0, n_tiles > 0))
        def _(sb=sb, gather=gather):
            hs_scr[sb] = gather(0)
            ys_scr[sb] = jnp.zeros((MOE_TILE, D_MODEL), F32)

        @pl.when(n_tiles > 0)
        def _(sb=sb):
            ys_scr[sb] += swiglu(hs_scr[sb])

        @pl.when(jnp.logical_and(j == last_j, n_tiles > 0))
        def _(sb=sb, add_back=add_back):
            add_back(0, ys_scr[sb])

        def extra(k, carry, gather=gather, add_back=add_back):
            add_back(k, swiglu(gather(k)))
            return carry

        lax.fori_loop(1, n_tiles, extra, 0)


def _moe(h, gates, w1, w3, w2):
    n = h.shape[0]
    once = pl.Buffered(1)
    return pl.pallas_call(
        _moe_kernel,
        grid=(n // MOE_ROWS, N_EXPERTS, D_FF // FF_TILE),
        in_specs=[pl.BlockSpec((MOE_ROWS, D_MODEL), lambda i, e, j: (i, 0), pipeline_mode=once),
                  pl.BlockSpec((MOE_ROWS, LANES), lambda i, e, j: (i, 0), pipeline_mode=once),
                  pl.BlockSpec((1, D_MODEL, FF_TILE), lambda i, e, j: (e, 0, j)),
                  pl.BlockSpec((1, D_MODEL, FF_TILE), lambda i, e, j: (e, 0, j)),
                  pl.BlockSpec((1, FF_TILE, D_MODEL), lambda i, e, j: (e, j, 0))],
        out_specs=pl.BlockSpec((MOE_ROWS, D_MODEL), lambda i, e, j: (i, 0), pipeline_mode=once),
        out_shape=jax.ShapeDtypeStruct((n, D_MODEL), F32),
        scratch_shapes=[pltpu.VMEM((MOE_NSUB, MOE_SUB, LANES), jnp.int32),
                        pltpu.VMEM((MOE_NSUB, LANES, MOE_SUB), jnp.int32),
                        pltpu.VMEM((MOE_NSUB, MOE_TILE, D_MODEL), BF16),
                        pltpu.VMEM((MOE_NSUB, MOE_TILE, D_MODEL), F32)],
        compiler_params=_params(("parallel", "arbitrary", "arbitrary"), MOE_VMEM_LIMIT),
        name="moe",
    )(h, gates, w1, w3, w2)


def _residual_ln_kernel(x_ref, f_ref, gate_ref, lg_ref, lb_ref, o_ref):
    o_ref[...] = _layernorm_rows(ALPHA * x_ref[...] + gate_ref[0] * f_ref[...], lg_ref[...], lb_ref[...])


def _residual_ln(x, f, gate, lg, lb, rows_per_mod):
    n = x.shape[0]
    tpb = rows_per_mod // ROW_TILE
    row = pl.BlockSpec((ROW_TILE, D_MODEL), lambda i: (i, 0))
    vec = pl.BlockSpec((1, D_MODEL), lambda i: (0, 0))
    return pl.pallas_call(
        _residual_ln_kernel,
        grid=(n // ROW_TILE,),
        in_specs=[row, row, pl.BlockSpec((1, 1, D_MODEL), lambda i: (i // tpb, 0, 0)), vec, vec],
        out_specs=row,
        out_shape=jax.ShapeDtypeStruct((n, D_MODEL), F32),
        compiler_params=_params(("parallel",)),
        name="residual_ln",
    )(x, f, gate, lg.reshape(1, D_MODEL), lb.reshape(1, D_MODEL))


def _permute_w_in(w):
    pad = jnp.zeros((D_MODEL, U_COLS - ORIG_END), w.dtype)
    return jnp.concatenate([w[:, :ORIG_GATES], w[:, ORIG_SX:ORIG_DT], w[:, ORIG_SZ:ORIG_SX],
                            w[:, ORIG_GATES:ORIG_SZ], w[:, ORIG_DT:ORIG_END], pad], axis=1).astype(BF16)


def _small_row(vals, offset):
    v = vals.reshape(-1).astype(F32)
    return jnp.zeros((1, LANES), F32).at[0, offset:offset + v.shape[0]].set(v)


def _pack_mlstm_state(c, n, m):
    eye = jnp.eye(N_HEADS, dtype=F32)
    shape = c.shape[:2] + (HEAD_LANES, HEAD_LANES)
    cbd = jnp.einsum('bdhke,hg->bdhkge', c, eye).reshape(shape)
    nbd = jnp.einsum('bdhk,hg,e->bdhkge', n, eye, jnp.ones((D_STATE,), F32)).reshape(shape)
    return jnp.concatenate([cbd, nbd], axis=-1), jnp.repeat(m, D_STATE, axis=-1)[:, :, None, :]


def _pack_ssd_state(s):
    sel = (jnp.arange(N_GROUPS)[:, None] == jnp.arange(N_HEADS)[None, :] // 2).astype(F32)
    return jnp.einsum('bdhpn,gh->bdgnhp', s, sel).reshape(s.shape[:2] + (LANES, HEAD_LANES))


def _layer(x, mods, P, l, bsz, t, ctx):
    sh1, sc1, g1, sh2, sc2, g2 = mods
    rows_per_mod = x.shape[0] // sh1.shape[0]
    lam_init = 0.8 - 0.6 * math.exp(-0.3 * l)
    u = _in_proj(x, sc1, sh1, P['w_in'][l], rows_per_mod)

    gate_bias = (_small_row(P['mlstm_gate_b'][l, 0], GATE_I) + _small_row(P['mlstm_gate_b'][l, 1], GATE_F))
    dt_bias = _small_row(P['ssm_dt_bias'][l], GATE_DT)
    alog = _small_row(P['ssm_A_log'][l], GATE_DT)
    dskip = jnp.repeat(P['ssm_D'][l].astype(F32), D_STATE).reshape(1, D_REC)

    if ctx is None:
        att, k_new, v_new = _attention_ctx(u, P['attn_lambda'][l], P['attn_norm_w'][l], lam_init, bsz, t)
        m_init = s_init = None
    else:
        ck, cv, c_c, c_n, c_m, c_s = ctx
        q, k, v = _rope_prep(u, bsz, t)
        k_all = jnp.concatenate([k, ck.astype(BF16)], axis=3)
        v_all = jnp.concatenate([v, cv.astype(BF16)], axis=2)
        att = _attention_lat(q, k_all, v_all, P['attn_lambda'][l], P['attn_norm_w'][l], lam_init, bsz, t)
        m_init = _pack_mlstm_state(c_c, c_n, c_m)
        s_init = _pack_ssd_state(c_s)
    hf, hb, c_out, n_out, m_out = _mlstm(u, gate_bias, bsz, t, m_init)
    xbc = _ssd_conv(u, P['conv_w'][l], P['conv_b'][l], bsz, t)
    yf, yb, s_out = _ssd(xbc, u, dt_bias, alog, dskip, bsz, t, s_init)

    x = _out_proj(att, hf, hb, yf, yb, u, P['w_out'][l], x, g1, P['mlstm_norm_w'][l], P['ssm_norm_w'][l],
                  P['ln_g'][l, 0], P['ln_b'][l, 0], rows_per_mod)
    if l % 2 == 0:
        x = _ffn(x, sc2, sh2, g2, P['ffn_w1'][l // 2], P['ffn_w3'][l // 2], P['ffn_w2'][l // 2],
                 P['ln_g'][l, 1], P['ln_b'][l, 1], rows_per_mod)
    else:
        gates, h2 = _router(x, sc2, sh2, P['router_w'][l // 2], rows_per_mod)
        f = _moe(h2, gates, P['moe_w1'][l // 2], P['moe_w3'][l // 2], P['moe_w2'][l // 2])
        x = _residual_ln(x, f, g2, P['ln_g'][l, 1], P['ln_b'][l, 1], rows_per_mod)
    if ctx is None:
        return x, (k_new, v_new, c_out, n_out[..., 0], m_out[:, :, 0, ::D_STATE], jnp.swapaxes(s_out, -1, -2))
    return x, None


def kernel(x_prompt, x_sample, c, cache_attn_k, cache_attn_v, state_mlstm_C, state_mlstm_n, state_mlstm_m, state_ssm, c_ctx, w_ada, b_ada, w_in, w_out, attn_lambda, attn_norm_w, mlstm_gate_b, mlstm_norm_w, conv_w, conv_b, ssm_A_log, ssm_dt_bias, ssm_D, ssm_norm_w, ln_g, ln_b, ffn_w1, ffn_w3, ffn_w2, router_w, moe_w1, moe_w3, moe_w2):
    bsz, seq, _ = x_prompt.shape
    dbsz, dseq, _ = x_sample.shape
    P = dict(w_in=[_permute_w_in(w_in[l]) for l in range(DEPTH)], w_out=w_out.astype(BF16),
             attn_lambda=attn_lambda, attn_norm_w=attn_norm_w, mlstm_gate_b=mlstm_gate_b,
             mlstm_norm_w=mlstm_norm_w, conv_w=conv_w, conv_b=conv_b, ssm_A_log=ssm_A_log,
             ssm_dt_bias=ssm_dt_bias, ssm_D=ssm_D, ssm_norm_w=ssm_norm_w, ln_g=ln_g, ln_b=ln_b,
             ffn_w1=ffn_w1.astype(BF16), ffn_w3=ffn_w3.astype(BF16), ffn_w2=ffn_w2.astype(BF16),
             router_w=router_w, moe_w1=moe_w1.astype(BF16), moe_w3=moe_w3.astype(BF16),
             moe_w2=moe_w2.astype(BF16))

    cvec = jnp.zeros((8, D_MODEL), F32).at[0].set(c_ctx).at[1:1 + dbsz].set(c)
    mod = _modulation(cvec, w_ada, b_ada)

    def mods_for(l, lo, hi):
        return [mod[l, lo:hi, i * D_MODEL:(i + 1) * D_MODEL][:, None, :] for i in range(6)]

    y_prompt = x_prompt.reshape(bsz * seq, D_MODEL)
    outs = []
    for l in range(DEPTH):
        y_prompt, ctx_out = _layer(y_prompt, mods_for(l, 0, 1), P, l, bsz, seq, None)
        outs.append(ctx_out)

    y_sample = x_sample.reshape(dbsz * dseq, D_MODEL)
    for l in range(DEPTH):
        ctx = (cache_attn_k[:, l], cache_attn_v[:, l], state_mlstm_C[:, l], state_mlstm_n[:, l],
               state_mlstm_m[:, l], state_ssm[:, l])
        y_sample, _ = _layer(y_sample, mods_for(l, 1, 1 + dbsz), P, l, dbsz, dseq, ctx)

    stacked = [jnp.stack([o[i] for o in outs], axis=1) for i in range(6)]
    return (y_prompt.reshape(bsz, seq, D_MODEL), y_sample.reshape(dbsz, dseq, D_MODEL), *stacked)
```

```python
import functools
import math

import jax
import jax.numpy as jnp
from jax import lax
from jax.experimental import pallas as pl
from jax.experimental.pallas import tpu as pltpu

F32 = jnp.float32
BF16 = jnp.bfloat16
HIGHEST = lax.Precision.HIGHEST

D_MODEL = 1024
DEPTH = 2
GRID_W = 64
N_HEADS = 4
D_ATT = 512
D_HEAD_V = 128
D_QK = 64
D_REC = 256
D_STATE = 64
N_GROUPS = 2
D_CONV = 3
D_FF = 2816
N_EXPERTS = 8
ALPHA = (2.0 * DEPTH) ** 0.25
CHUNK = 64
ROPE_BASE = 10000.0
LOG2E = 1.4426950408889634
EPS = 1e-5

COL_AQ, COL_AK, COL_AV = 0, 512, 1024
COL_MQ, COL_MK, COL_MV, COL_MO = 1536, 1792, 2048, 2304
COL_SX, COL_SBC, COL_SZ = 2560, 2816, 3072
COL_SMALL = 3328
U_COLS = 3584
ORIG_GATES, ORIG_SZ, ORIG_SX, ORIG_DT, ORIG_END = 2560, 2576, 2832, 3344, 3352
GATE_I, GATE_F, GATE_DT = 0, 8, 16

LANES = 128
ROW_TILE = 512
SCAN_ROWS = 256
VMEM_LIMIT = 48 * 1024 * 1024

NT_DIMS = (((1,), (1,)), ((), ()))
TN_DIMS = (((0,), (0,)), ((), ()))


def _params(sem, vmem=VMEM_LIMIT):
    return pltpu.CompilerParams(dimension_semantics=sem, vmem_limit_bytes=vmem)


def _silu(x):
    return x * jax.nn.sigmoid(x)


def _bdot(a, b):
    return jnp.dot(a.astype(BF16), b.astype(BF16), preferred_element_type=F32)


def _bdot_nt(a, b):
    return lax.dot_general(a.astype(BF16), b.astype(BF16), NT_DIMS, preferred_element_type=F32)


def _bdot_tn(a, b):
    return lax.dot_general(a.astype(BF16), b.astype(BF16), TN_DIMS, preferred_element_type=F32)


def _layernorm_rows(y, g, b):
    mu = jnp.mean(y, -1, keepdims=True)
    d = y - mu
    var = jnp.mean(d * d, -1, keepdims=True)
    return d * lax.rsqrt(var + EPS) * g + b


def _mod_kernel(c_ref, w_ref, b_ref, o_ref):
    o_ref[0] = jnp.dot(_silu(c_ref[...]), w_ref[0], precision=HIGHEST,
                       preferred_element_type=F32) + b_ref[0]


def _modulation(cvec, w_ada, b_ada):
    tn = 1536
    return pl.pallas_call(
        _mod_kernel,
        grid=(DEPTH, 6 * D_MODEL // tn),
        in_specs=[pl.BlockSpec((8, D_MODEL), lambda l, j: (0, 0)),
                  pl.BlockSpec((1, D_MODEL, tn), lambda l, j: (l, 0, j)),
                  pl.BlockSpec((1, 1, tn), lambda l, j: (l, 0, j))],
        out_specs=pl.BlockSpec((1, 8, tn), lambda l, j: (l, 0, j)),
        out_shape=jax.ShapeDtypeStruct((DEPTH, 8, 6 * D_MODEL), F32),
        compiler_params=_params(("parallel", "parallel")),
        name="modulation",
    )(cvec, w_ada, b_ada.reshape(DEPTH, 1, 6 * D_MODEL))


def _inproj_kernel(x_ref, sc_ref, sh_ref, w_ref, o_ref):
    h = (x_ref[...] * (1.0 + sc_ref[0]) + sh_ref[0]).astype(BF16)
    for n0 in range(0, U_COLS, 512):
        o_ref[:, n0:n0 + 512] = jnp.dot(h, w_ref[:, n0:n0 + 512], preferred_element_type=F32)


def _in_proj(x, sc, sh, w, rows_per_mod):
    n = x.shape[0]
    tpb = rows_per_mod // ROW_TILE
    return pl.pallas_call(
        _inproj_kernel,
        grid=(n // ROW_TILE,),
        in_specs=[pl.BlockSpec((ROW_TILE, D_MODEL), lambda i: (i, 0)),
                  pl.BlockSpec((1, 1, D_MODEL), lambda i: (i // tpb, 0, 0)),
                  pl.BlockSpec((1, 1, D_MODEL), lambda i: (i // tpb, 0, 0)),
                  pl.BlockSpec((D_MODEL, U_COLS), lambda i: (0, 0))],
        out_specs=pl.BlockSpec((ROW_TILE, U_COLS), lambda i: (i, 0)),
        out_shape=jax.ShapeDtypeStruct((n, U_COLS), F32),
        compiler_params=_params(("parallel",)),
        name="in_proj",
    )(x, sc, sh, w)


def _lambda_scalar(lam_ref, lam_init):
    lp = lam_ref[...]
    s01 = jnp.sum(lp[0:1] * lp[1:2], axis=-1, keepdims=True)
    s23 = jnp.sum(lp[2:3] * lp[3:4], axis=-1, keepdims=True)
    return jnp.exp(s01) - jnp.exp(s23) + lam_init


def _softmax_rows(s):
    e = jnp.exp(s - jnp.max(s, -1, keepdims=True))
    return e / jnp.sum(e, -1, keepdims=True)


def _head_norm(o, nw, lam_init):
    return o * lax.rsqrt(jnp.mean(o * o, -1, keepdims=True) + EPS) * nw * (1.0 - lam_init)


def _attn_ctx_kernel(u_ref, lam_ref, nw_ref, att_ref, k_ref, v_ref, *, lam_init):
    lam = _lambda_scalar(lam_ref, lam_init)
    for h in range(N_HEADS):
        v = u_ref[:, COL_AV + h * D_HEAD_V:COL_AV + (h + 1) * D_HEAD_V]
        v_ref[0, h] = v
        ps = []
        for m in range(2):
            c0 = h * D_HEAD_V + m * D_QK
            q = u_ref[:, COL_AQ + c0:COL_AQ + c0 + D_QK] * (D_QK ** -0.5)
            k = u_ref[:, COL_AK + c0:COL_AK + c0 + D_QK]
            k_ref[0, h, m] = k
            ps.append(_softmax_rows(_bdot_nt(q, k)))
        o = _bdot(ps[0] - lam * ps[1], v)
        att_ref[:, h * D_HEAD_V:(h + 1) * D_HEAD_V] = _head_norm(o, nw_ref[...], lam_init).astype(BF16)


def _attention_ctx(u, lam_p, norm_w, lam_init, bsz, t):
    n = bsz * t
    return pl.pallas_call(
        functools.partial(_attn_ctx_kernel, lam_init=lam_init),
        grid=(bsz,),
        in_specs=[pl.BlockSpec((t, 3 * D_ATT), lambda b: (b, 0)),
                  pl.BlockSpec((4, D_QK), lambda b: (0, 0)),
                  pl.BlockSpec((1, D_HEAD_V), lambda b: (0, 0))],
        out_specs=[pl.BlockSpec((t, D_ATT), lambda b: (b, 0)),
                   pl.BlockSpec((1, N_HEADS, 2, t, D_QK), lambda b: (b, 0, 0, 0, 0)),
                   pl.BlockSpec((1, N_HEADS, t, D_HEAD_V), lambda b: (b, 0, 0, 0))],
        out_shape=[jax.ShapeDtypeStruct((n, D_ATT), BF16),
                   jax.ShapeDtypeStruct((bsz, N_HEADS, 2, t, D_QK), F32),
                   jax.ShapeDtypeStruct((bsz, N_HEADS, t, D_HEAD_V), F32)],
        compiler_params=_params(("parallel",)),
        name="attn_ctx",
    )(u, lam_p, norm_w.reshape(1, D_HEAD_V))


def _rope_kernel(u_ref, cos_ref, sa_ref, sb_ref, q_ref, k_ref, v_ref):
    cos, sa, sb = cos_ref[...], sa_ref[...], sb_ref[...]

    def rope(x):
        return x * cos + pltpu.roll(x, LANES - 16, 1) * sa + pltpu.roll(x, 16, 1) * sb

    for h in range(N_HEADS):
        q = rope(u_ref[:, COL_AQ + h * D_HEAD_V:COL_AQ + (h + 1) * D_HEAD_V]) * (LOG2E * D_QK ** -0.5)
        k = rope(u_ref[:, COL_AK + h * D_HEAD_V:COL_AK + (h + 1) * D_HEAD_V])
        for m in range(2):
            q_ref[0, h, m] = q[:, m * D_QK:(m + 1) * D_QK].astype(BF16)
            k_ref[0, h, m] = k[:, m * D_QK:(m + 1) * D_QK].astype(BF16)
        v_ref[0, h] = u_ref[:, COL_AV + h * D_HEAD_V:COL_AV + (h + 1) * D_HEAD_V].astype(BF16)


def _rope_tables(t):
    rows = jnp.repeat(jnp.arange(t // GRID_W, dtype=F32), GRID_W)
    cols = jnp.tile(jnp.arange(GRID_W, dtype=F32), t // GRID_W)
    half = D_QK // 2
    inv = ROPE_BASE ** (-jnp.arange(0, half, 2, dtype=F32) / half)
    ang_r = rows[:, None] * inv
    ang_c = cols[:, None] * inv
    ang = jnp.concatenate([ang_r, ang_r, ang_c, ang_c], -1)
    cos, sin = jnp.cos(ang), jnp.sin(ang)
    quarter = (jnp.arange(D_QK) // (D_QK // 4)) % 2
    sa = jnp.where(quarter == 0, -sin, 0.0)
    sb = jnp.where(quarter == 1, sin, 0.0)
    tile2 = lambda a: jnp.concatenate([a, a], -1)
    return tile2(cos), tile2(sa), tile2(sb)


def _rope_prep(u, bsz, t):
    tr = 512
    nb = t // tr
    cos, sa, sb = _rope_tables(t)
    tab = pl.BlockSpec((tr, LANES), lambda b, i: (i, 0))
    return pl.pallas_call(
        _rope_kernel,
        grid=(bsz, nb),
        in_specs=[pl.BlockSpec((tr, 3 * D_ATT), lambda b, i: (b * nb + i, 0)), tab, tab, tab],
        out_specs=[pl.BlockSpec((1, N_HEADS, 2, tr, D_QK), lambda b, i: (b, 0, 0, i, 0)),
                   pl.BlockSpec((1, N_HEADS, 2, tr, D_QK), lambda b, i: (b, 0, 0, i, 0)),
                   pl.BlockSpec((1, N_HEADS, tr, D_HEAD_V), lambda b, i: (b, 0, i, 0))],
        out_shape=[jax.ShapeDtypeStruct((bsz, N_HEADS, 2, t, D_QK), BF16),
                   jax.ShapeDtypeStruct((bsz, N_HEADS, 2, t, D_QK), BF16),
                   jax.ShapeDtypeStruct((bsz, N_HEADS, t, D_HEAD_V), BF16)],
        compiler_params=_params(("parallel", "parallel")),
        name="rope_prep",
    )(u, cos, sa, sb)


def _attn_lat_kernel(q_ref, k_ref, v_ref, lam_ref, nw_ref, o_ref, *, lam_init):
    lam = _lambda_scalar(lam_ref, lam_init)
    es, sums = [], []
    for m in range(2):
        s = lax.dot_general(q_ref[0, 0, m], k_ref[0, 0, m], NT_DIMS, preferred_element_type=F32)
        e = jnp.exp2(s - jnp.max(s, -1, keepdims=True))
        es.append(e)
        sums.append(jnp.sum(e, -1, keepdims=True))
    a = es[0] - (lam * sums[0] / sums[1]) * es[1]
    o = _bdot(a, v_ref[0, 0]) / sums[0]
    o_ref[...] = _head_norm(o, nw_ref[...], lam_init).astype(BF16)


def _attention_lat(q, k_all, v_all, lam_p, norm_w, lam_init, bsz, t):
    tq = 256
    nq = t // tq
    s = k_all.shape[3]
    return pl.pallas_call(
        functools.partial(_attn_lat_kernel, lam_init=lam_init),
        grid=(bsz, N_HEADS, nq),
        in_specs=[pl.BlockSpec((1, 1, 2, tq, D_QK), lambda b, h, i: (b, h, 0, i, 0)),
                  pl.BlockSpec((1, 1, 2, s, D_QK), lambda b, h, i: (b, h, 0, 0, 0)),
                  pl.BlockSpec((1, 1, s, D_HEAD_V), lambda b, h, i: (b, h, 0, 0)),
                  pl.BlockSpec((4, D_QK), lambda b, h, i: (0, 0)),
                  pl.BlockSpec((1, D_HEAD_V), lambda b, h, i: (0, 0))],
        out_specs=pl.BlockSpec((tq, D_HEAD_V), lambda b, h, i: (b * nq + i, h)),
        out_shape=jax.ShapeDtypeStruct((bsz * t, D_ATT), BF16),
        compiler_params=_params(("parallel", "parallel", "parallel")),
        name="attn_lat",
    )(q, k_all, v_all, lam_p, norm_w.reshape(1, D_HEAD_V))


HEAD_LANES = N_HEADS * D_STATE


def _scan_consts():
    t = lax.broadcasted_iota(jnp.int32, (CHUNK, HEAD_LANES), 0)
    s = lax.broadcasted_iota(jnp.int32, (CHUNK, HEAD_LANES), 1) & (CHUNK - 1)
    r = lax.broadcasted_iota(jnp.int32, (CHUNK, CHUNK), 0)
    c = lax.broadcasted_iota(jnp.int32, (CHUNK, CHUNK), 1)
    reads = (s <= t, s >= t)
    tri = (jnp.where(c <= r, 1.0, 0.0), jnp.where(c >= r, 1.0, 0.0))
    return reads, tri, s == t


def _group_mask(rows, cols, row_shift, col_shift):
    r = lax.broadcasted_iota(jnp.int32, (rows, cols), 0) >> 6
    c = (lax.broadcasted_iota(jnp.int32, (rows, cols), 1) >> 6) & (N_HEADS - 1)
    return (r >> row_shift) == (c >> col_shift)


def _spread(x, chans):
    return jnp.concatenate([jnp.broadcast_to(x[:, c:c + 1], (CHUNK, D_STATE)) for c in chans], axis=1)


def _stack_heads(x):
    return jnp.concatenate([x] * N_HEADS, axis=0)


def _mlstm_chunk(q4, k4, v4, g, d, cn_prev, m_prev, reads, tri, eye, bd):
    li = _spread(g, [GATE_I + d * N_HEADS + h for h in range(N_HEADS)])
    lf = _spread(jax.nn.log_sigmoid(g), [GATE_F + d * N_HEADS + h for h in range(N_HEADS)])
    bc = jnp.dot(tri[d], lf, precision=HIGHEST, preferred_element_type=F32)
    btot = jnp.sum(lf, 0, keepdims=True)
    b_row = jnp.sum(jnp.where(reads[1 - d], lf, 0.0), 0, keepdims=True)
    li_row = jnp.sum(jnp.where(eye, li, 0.0), 0, keepdims=True)
    dm = jnp.where(reads[d], bc - b_row + li_row, -jnp.inf)
    rmax = jnp.concatenate(
        [jnp.broadcast_to(jnp.max(dm[:, h * D_STATE:(h + 1) * D_STATE], -1, keepdims=True), (CHUNK, D_STATE))
         for h in range(N_HEADS)], axis=1)
    inter = bc + m_prev
    m_t = jnp.maximum(inter, rmax)
    w_inter = jnp.exp(inter - m_t)
    qs = (q4 * (D_STATE ** -0.5)).astype(BF16)
    kbd = jnp.where(bd[:, :HEAD_LANES], _stack_heads(k4.astype(BF16)), 0)
    s4 = lax.dot_general(qs, kbd, NT_DIMS, preferred_element_type=F32) * jnp.exp(dm - m_t)
    vo = jnp.concatenate([v4.astype(BF16), jnp.ones((CHUNK, HEAD_LANES), BF16)], axis=1)
    vbd = jnp.where(bd, _stack_heads(vo), 0)
    nd = (jnp.concatenate([w_inter, w_inter], axis=1)
          * jnp.dot(qs, cn_prev.astype(BF16), preferred_element_type=F32)
          + jnp.dot(s4.astype(BF16), vbd, preferred_element_type=F32))
    hc = nd[:, :HEAD_LANES] / jnp.maximum(jnp.abs(nd[:, HEAD_LANES:]), jnp.exp(-m_t))
    gcol = btot - bc + li
    m_new = jnp.maximum(btot + m_prev, jnp.max(gcol, 0, keepdims=True))
    w_c = jnp.exp(btot + m_prev - m_new)
    kw = (k4 * jnp.exp(gcol - m_new)).astype(BF16)
    dcn = lax.dot_general(kw, vo, TN_DIMS, preferred_element_type=F32)
    cn_new = jnp.concatenate([w_c, w_c], axis=1) * cn_prev + jnp.where(bd, dcn, 0.0)
    return hc, cn_new, m_new


def _mlstm_kernel(*refs, nblk, rows, zero_init):
    if zero_init:
        (qf_ref, kf_ref, vf_ref, gf_ref, qb_ref, kb_ref, vb_ref, gb_ref, bias_ref,
         hf_ref, hb_ref, cout_ref, nout_ref, mout_ref, cn_scr, m_scr) = refs
    else:
        (qf_ref, kf_ref, vf_ref, gf_ref, qb_ref, kb_ref, vb_ref, gb_ref, bias_ref, c0_ref, m0_ref,
         hf_ref, hb_ref, cout_ref, nout_ref, mout_ref, cn_scr, m_scr) = refs
    j = pl.program_id(1)
    nchunk = rows // CHUNK

    @pl.when(j == 0)
    def _():
        if zero_init:
            cn_scr[...] = jnp.zeros_like(cn_scr)
            m_scr[...] = jnp.zeros_like(m_scr)
        else:
            cn_scr[...] = c0_ref[0]
            m_scr[...] = m0_ref[0]

    reads, tri, eye = _scan_consts()
    bd = _group_mask(HEAD_LANES, 2 * HEAD_LANES, 0, 0)

    def chunk(ci, carry):
        for d in range(2):
            q_ref, k_ref, v_ref, g_ref, h_ref = ((qf_ref, kf_ref, vf_ref, gf_ref, hf_ref) if d == 0
                                                 else (qb_ref, kb_ref, vb_ref, gb_ref, hb_ref))
            cj = ci if d == 0 else nchunk - 1 - ci
            rs = pl.ds(pl.multiple_of(cj * CHUNK, CHUNK), CHUNK)
            g = g_ref[rs, :] + bias_ref[...]
            hc, cn_new, m_new = _mlstm_chunk(q_ref[rs, :], k_ref[rs, :], v_ref[rs, :], g, d,
                                             cn_scr[d], m_scr[d], reads, tri, eye, bd)
            h_ref[rs, :] = hc
            cn_scr[d] = cn_new
            m_scr[d] = m_new
        return carry

    lax.fori_loop(0, nchunk, chunk, 0, unroll=True)

    @pl.when(j == nblk - 1)
    def _():
        for d in range(2):
            for h in range(N_HEADS):
                r0, r1 = h * D_STATE, (h + 1) * D_STATE
                cout_ref[0, d, h] = cn_scr[d, r0:r1, r0:r1]
                nout_ref[0, d, h] = cn_scr[d, r0:r1, HEAD_LANES + r0:HEAD_LANES + r1]
        mout_ref[0] = m_scr[...]


def _mlstm(u, gate_bias, bsz, t, init):
    rows = min(t, SCAN_ROWS)
    nblk = t // rows
    zero_init = init is None

    def fwd(col):
        return lambda b, j: (b * nblk + j, col)

    def bwd(col):
        return lambda b, j: (b * nblk + nblk - 1 - j, col)

    cq, ck, cv, cg = COL_MQ // D_REC, COL_MK // D_REC, COL_MV // D_REC, COL_SMALL // LANES
    in_specs = []
    for mk in (fwd, bwd):
        in_specs += [pl.BlockSpec((rows, D_REC), mk(cq)), pl.BlockSpec((rows, D_REC), mk(ck)),
                     pl.BlockSpec((rows, D_REC), mk(cv)), pl.BlockSpec((rows, LANES), mk(cg))]
    in_specs.append(pl.BlockSpec((1, LANES), lambda b, j: (0, 0)))
    args = [u] * 8 + [gate_bias]
    state_c = pl.BlockSpec((1, 2, HEAD_LANES, 2 * HEAD_LANES), lambda b, j: (b, 0, 0, 0))
    state_m = pl.BlockSpec((1, 2, 1, HEAD_LANES), lambda b, j: (b, 0, 0, 0))
    head_blocks = pl.BlockSpec((1, 2, N_HEADS, D_STATE, D_STATE), lambda b, j: (b, 0, 0, 0, 0))
    if not zero_init:
        in_specs += [state_c, state_m]
        args += list(init)
    return pl.pallas_call(
        functools.partial(_mlstm_kernel, nblk=nblk, rows=rows, zero_init=zero_init),
        grid=(bsz, nblk),
        in_specs=in_specs,
        out_specs=[pl.BlockSpec((rows, D_REC), fwd(0)), pl.BlockSpec((rows, D_REC), bwd(0)),
                   head_blocks, head_blocks, state_m],
        out_shape=[jax.ShapeDtypeStruct((bsz * t, D_REC), F32), jax.ShapeDtypeStruct((bsz * t, D_REC), F32),
                   jax.ShapeDtypeStruct((bsz, 2, N_HEADS, D_STATE, D_STATE), F32),
                   jax.ShapeDtypeStruct((bsz, 2, N_HEADS, D_STATE, D_STATE), F32),
                   jax.ShapeDtypeStruct((bsz, 2, 1, HEAD_LANES), F32)],
        scratch_shapes=[pltpu.VMEM((2, HEAD_LANES, 2 * HEAD_LANES), F32),
                        pltpu.VMEM((2, 1, HEAD_LANES), F32)],
        compiler_params=_params(("parallel", "arbitrary")),
        name="mlstm_scan",
    )(*args)


def _conv_kernel(x_ref, prev_ref, next_ref, w_ref, b_ref, o_ref, *, nblk):
    i = pl.program_id(1)
    x = x_ref[...]
    rows = x.shape[0]
    r = lax.broadcasted_iota(jnp.int32, x.shape, 0)
    prev_row = prev_ref[7:8, :] * (i > 0).astype(F32)
    next_row = next_ref[0:1, :] * (i < nblk - 1).astype(F32)
    xm = jnp.where(r == 0, prev_row, pltpu.roll(x, 1, 0))
    xp = jnp.where(r == rows - 1, next_row, pltpu.roll(x, rows - 1, 0))
    w = w_ref[...]
    o_ref[...] = _silu(xm * w[0:1] + x * w[1:2] + xp * w[2:3] + b_ref[...])


def _ssd_conv(u, conv_w, conv_b, bsz, t):
    rows = min(t, 512)
    nblk = t // rows
    r8 = rows // 8
    width = 2 * D_REC
    c0 = COL_SX // width
    return pl.pallas_call(
        functools.partial(_conv_kernel, nblk=nblk),
        grid=(bsz, nblk),
        in_specs=[pl.BlockSpec((rows, width), lambda b, i: (b * nblk + i, c0)),
                  pl.BlockSpec((8, width), lambda b, i: (jnp.maximum((b * nblk + i) * r8 - 1, 0), c0)),
                  pl.BlockSpec((8, width), lambda b, i: (jnp.minimum((b * nblk + i + 1) * r8,
                                                                     bsz * nblk * r8 - 1), c0)),
                  pl.BlockSpec((D_CONV, width), lambda b, i: (0, 0)),
                  pl.BlockSpec((1, width), lambda b, i: (0, 0))],
        out_specs=pl.BlockSpec((rows, width), lambda b, i: (b * nblk + i, 0)),
        out_shape=jax.ShapeDtypeStruct((bsz * t, width), F32),
        compiler_params=_params(("parallel", "parallel")),
        name="ssd_conv",
    )(u, u, u, conv_w, conv_b.reshape(1, width))


def _ssd_chunk(x4, bcm, dt128, da128, d, sg_prev, reads, tri, b_sel, s_sel, bd):
    chans = [GATE_DT + d * N_HEADS + h for h in range(N_HEADS)]
    dt = _spread(dt128, chans)
    da = _spread(da128, chans)
    ac = jnp.dot(tri[d], da, precision=HIGHEST, preferred_element_type=F32)
    atot = jnp.sum(da, 0, keepdims=True)
    a_row = jnp.sum(jnp.where(reads[1 - d], da, 0.0), 0, keepdims=True)
    decay = jnp.exp(jnp.where(reads[d], ac - a_row, -jnp.inf))
    bmat = bcm[:, :LANES].astype(BF16)
    cmat = bcm[:, LANES:].astype(BF16)
    bbd = jnp.where(b_sel, _stack_heads(bmat), 0)
    g4 = lax.dot_general(cmat, bbd, NT_DIMS, preferred_element_type=F32)
    xbd = jnp.where(bd, _stack_heads((x4 * dt).astype(BF16)), 0)
    y = (jnp.dot((g4 * decay).astype(BF16), xbd, preferred_element_type=F32)
         + jnp.dot(cmat, sg_prev.astype(BF16), preferred_element_type=F32) * jnp.exp(ac))
    w = jnp.exp(atot - ac) * dt
    dsg = lax.dot_general(bmat, (x4 * w).astype(BF16), TN_DIMS, preferred_element_type=F32)
    sg_new = jnp.exp(atot) * sg_prev + jnp.where(s_sel, dsg, 0.0)
    return y, sg_new


def _ssd_kernel(*refs, nblk, rows, zero_init):
    if zero_init:
        (xf_ref, bcf_ref, gf_ref, xb_ref, bcb_ref, gb_ref, dtb_ref, alog_ref, dskip_ref,
         yf_ref, yb_ref, sout_ref, s_scr) = refs
    else:
        (xf_ref, bcf_ref, gf_ref, xb_ref, bcb_ref, gb_ref, dtb_ref, alog_ref, dskip_ref, s0_ref,
         yf_ref, yb_ref, sout_ref, s_scr) = refs
    j = pl.program_id(1)
    nchunk = rows // CHUNK

    @pl.when(j == 0)
    def _():
        if zero_init:
            s_scr[...] = jnp.zeros_like(s_scr)
        else:
            s_scr[...] = s0_ref[0]

    reads, tri, _ = _scan_consts()
    bd = _group_mask(HEAD_LANES, HEAD_LANES, 0, 0)
    b_sel = _group_mask(HEAD_LANES, LANES, 1, 0)
    s_sel = _group_mask(LANES, HEAD_LANES, 0, 1)
    a_coef = -jnp.exp(alog_ref[...])

    def chunk(ci, carry):
        for d in range(2):
            x_ref, bc_ref, g_ref, y_ref = ((xf_ref, bcf_ref, gf_ref, yf_ref) if d == 0
                                           else (xb_ref, bcb_ref, gb_ref, yb_ref))
            cj = ci if d == 0 else nchunk - 1 - ci
            rs = pl.ds(pl.multiple_of(cj * CHUNK, CHUNK), CHUNK)
            dt128 = jax.nn.softplus(g_ref[rs, :] + dtb_ref[...])
            x4 = x_ref[rs, :]
            y, sg_new = _ssd_chunk(x4, bc_ref[rs, :], dt128, dt128 * a_coef, d, s_scr[d],
                                   reads, tri, b_sel, s_sel, bd)
            if d == 0:
                y = y + dskip_ref[...] * x4
            y_ref[rs, :] = y
            s_scr[d] = sg_new
        return carry

    lax.fori_loop(0, nchunk, chunk, 0, unroll=True)

    @pl.when(j == nblk - 1)
    def _():
        for d in range(2):
            for h in range(N_HEADS):
                g0 = (h // 2) * D_STATE
                sout_ref[0, d, h] = s_scr[d, g0:g0 + D_STATE, h * D_STATE:(h + 1) * D_STATE]


def _ssd(xbc, u, dt_bias_row, alog_row, dskip_row, bsz, t, init):
    rows = min(t, SCAN_ROWS)
    nblk = t // rows
    zero_init = init is None

    def fwd(col):
        return lambda b, j: (b * nblk + j, col)

    def bwd(col):
        return lambda b, j: (b * nblk + nblk - 1 - j, col)

    in_specs = []
    for mk in (fwd, bwd):
        in_specs += [pl.BlockSpec((rows, D_REC), mk(0)), pl.BlockSpec((rows, D_REC), mk(1)),
                     pl.BlockSpec((rows, LANES), mk(COL_SMALL // LANES))]
    in_specs += [pl.BlockSpec((1, LANES), lambda b, j: (0, 0)),
                 pl.BlockSpec((1, LANES), lambda b, j: (0, 0)),
                 pl.BlockSpec((1, D_REC), lambda b, j: (0, 0))]
    args = [xbc, xbc, u, xbc, xbc, u, dt_bias_row, alog_row, dskip_row]
    state = pl.BlockSpec((1, 2, LANES, HEAD_LANES), lambda b, j: (b, 0, 0, 0))
    if not zero_init:
        in_specs.append(state)
        args.append(init)
    return pl.pallas_call(
        functools.partial(_ssd_kernel, nblk=nblk, rows=rows, zero_init=zero_init),
        grid=(bsz, nblk),
        in_specs=in_specs,
        out_specs=[pl.BlockSpec((rows, D_REC), fwd(0)), pl.BlockSpec((rows, D_REC), bwd(0)),
                   pl.BlockSpec((1, 2, N_HEADS, D_STATE, D_STATE), lambda b, j: (b, 0, 0, 0, 0))],
        out_shape=[jax.ShapeDtypeStruct((bsz * t, D_REC), F32), jax.ShapeDtypeStruct((bsz * t, D_REC), F32),
                   jax.ShapeDtypeStruct((bsz, 2, N_HEADS, D_STATE, D_STATE), F32)],
        scratch_shapes=[pltpu.VMEM((2, LANES, HEAD_LANES), F32)],
        compiler_params=_params(("parallel", "arbitrary")),
        name="ssd_scan",
    )(*args)


def _outproj_kernel(att_ref, hf_ref, hb_ref, mo_ref, yf_ref, yb_ref, z_ref, w_ref, x_ref, gate_ref,
                    mnw_ref, snw_ref, lg_ref, lb_ref, o_ref):
    hh = hf_ref[...] + hb_ref[...]
    parts = []
    for h in range(N_HEADS):
        xh = hh[:, h * D_STATE:(h + 1) * D_STATE]
        mu = jnp.mean(xh, -1, keepdims=True)
        dlt = xh - mu
        var = jnp.mean(dlt * dlt, -1, keepdims=True)
        parts.append(dlt * lax.rsqrt(var + EPS))
    ml = jax.nn.sigmoid(mo_ref[...]) * jnp.concatenate(parts, axis=1) * mnw_ref[...]
    yz = (yf_ref[...] + yb_ref[...]) * _silu(z_ref[...])
    parts = []
    for grp in range(N_GROUPS):
        yg = yz[:, grp * LANES:(grp + 1) * LANES]
        parts.append(yg * lax.rsqrt(jnp.mean(yg * yg, -1, keepdims=True) + EPS))
    ssm = jnp.concatenate(parts, axis=1) * snw_ref[...]
    mixed = (jnp.dot(att_ref[...], w_ref[0:D_ATT], preferred_element_type=F32)
             + _bdot(ml, w_ref[D_ATT:D_ATT + D_REC])
             + _bdot(ssm, w_ref[D_ATT + D_REC:D_MODEL]))
    y = ALPHA * x_ref[...] + gate_ref[0] * mixed
    o_ref[...] = _layernorm_rows(y, lg_ref[...], lb_ref[...])


def _out_proj(att, hf, hb, yf, yb, u, w, x, gate, mnw, snw, lg, lb, rows_per_mod):
    n = x.shape[0]
    tpb = rows_per_mod // ROW_TILE
    row = lambda width, col: pl.BlockSpec((ROW_TILE, width), lambda i: (i, col))
    vec = lambda width: pl.BlockSpec((1, width), lambda i: (0, 0))
    return pl.pallas_call(
        _outproj_kernel,
        grid=(n // ROW_TILE,),
        in_specs=[row(D_ATT, 0), row(D_REC, 0), row(D_REC, 0), row(D_REC, COL_MO // D_REC),
                  row(D_REC, 0), row(D_REC, 0), row(D_REC, COL_SZ // D_REC),
                  pl.BlockSpec((D_MODEL, D_MODEL), lambda i: (0, 0)),
                  row(D_MODEL, 0),
                  pl.BlockSpec((1, 1, D_MODEL), lambda i: (i // tpb, 0, 0)),
                  vec(D_REC), vec(D_REC), vec(D_MODEL), vec(D_MODEL)],
        out_specs=row(D_MODEL, 0),
        out_shape=jax.ShapeDtypeStruct((n, D_MODEL), F32),
        compiler_params=_params(("parallel",)),
        name="out_proj",
    )(att, hf, hb, u, yf, yb, u, w, x, gate, mnw.reshape(1, D_REC), snw.reshape(1, D_REC),
      lg.reshape(1, D_MODEL), lb.reshape(1, D_MODEL))


FF_TILE = D_FF // 2
FFN_ROWS = 1024
FFN_SUB = 512
FFN_VMEM_LIMIT = 56 * 1024 * 1024


def _swiglu_partial(h, w1, w3, w2):
    a = jnp.dot(h, w1, preferred_element_type=F32)
    b = jnp.dot(h, w3, preferred_element_type=F32)
    return jnp.dot((_silu(a) * b).astype(BF16), w2, preferred_element_type=F32)


def _ffn_kernel(x_ref, sc_ref, sh_ref, gate_ref, w1_ref, w3_ref, w2_ref, lg_ref, lb_ref, o_ref,
                h_scr, acc_scr):
    j = pl.program_id(1)

    @pl.when(j == 0)
    def _():
        h_scr[...] = (x_ref[...] * (1.0 + sc_ref[0]) + sh_ref[0]).astype(BF16)
        acc_scr[...] = jnp.zeros_like(acc_scr)

    for r0 in range(0, FFN_ROWS, FFN_SUB):
        rows = slice(r0, r0 + FFN_SUB)
        acc_scr[rows, :] += _swiglu_partial(h_scr[rows, :], w1_ref[...], w3_ref[...], w2_ref[...])

    @pl.when(j == pl.num_programs(1) - 1)
    def _():
        y = ALPHA * x_ref[...] + gate_ref[0] * acc_scr[...]
        o_ref[...] = _layernorm_rows(y, lg_ref[...], lb_ref[...])


def _ffn(x, sc, sh, gate, w1, w3, w2, lg, lb, rows_per_mod):
    n = x.shape[0]
    tpb = rows_per_mod // FFN_ROWS
    modspec = pl.BlockSpec((1, 1, D_MODEL), lambda i, j: (i // tpb, 0, 0))
    vec = pl.BlockSpec((1, D_MODEL), lambda i, j: (0, 0))
    return pl.pallas_call(
        _ffn_kernel,
        grid=(n // FFN_ROWS, D_FF // FF_TILE),
        in_specs=[pl.BlockSpec((FFN_ROWS, D_MODEL), lambda i, j: (i, 0)), modspec, modspec, modspec,
                  pl.BlockSpec((D_MODEL, FF_TILE), lambda i, j: (0, j)),
                  pl.BlockSpec((D_MODEL, FF_TILE), lambda i, j: (0, j)),
                  pl.BlockSpec((FF_TILE, D_MODEL), lambda i, j: (j, 0)), vec, vec],
        out_specs=pl.BlockSpec((FFN_ROWS, D_MODEL), lambda i, j: (i, 0)),
        out_shape=jax.ShapeDtypeStruct((n, D_MODEL), F32),
        scratch_shapes=[pltpu.VMEM((FFN_ROWS, D_MODEL), BF16), pltpu.VMEM((FFN_ROWS, D_MODEL), F32)],
        compiler_params=_params(("parallel", "arbitrary"), FFN_VMEM_LIMIT),
        name="ffn_dense",
    )(x, sc, sh, gate, w1, w3, w2, lg.reshape(1, D_MODEL), lb.reshape(1, D_MODEL))


def _router_kernel(x_ref, sc_ref, sh_ref, rw_ref, gates_ref, h_ref):
    h = x_ref[...] * (1.0 + sc_ref[0]) + sh_ref[0]
    h_ref[...] = h.astype(BF16)
    logits = jnp.dot(h, rw_ref[...], precision=HIGHEST, preferred_element_type=F32)
    lane = lax.broadcasted_iota(jnp.int32, logits.shape, 1)
    valid = lane < N_EXPERTS
    p = jnp.where(valid, _softmax_rows(jnp.where(valid, logits, -jnp.inf)), -2.0)
    p1 = jnp.max(p, -1, keepdims=True)
    i1 = jnp.min(jnp.where(p == p1, lane, LANES), -1, keepdims=True)
    rest = jnp.where(lane == i1, -1.0, p)
    p2 = jnp.max(rest, -1, keepdims=True)
    i2 = jnp.min(jnp.where(rest == p2, lane, LANES), -1, keepdims=True)
    tot = p1 + p2
    gates_ref[...] = jnp.where(lane == i1, p1 / tot, jnp.where(lane == i2, p2 / tot, 0.0))


def _router(x, sc, sh, router_w, rows_per_mod):
    n = x.shape[0]
    tpb = rows_per_mod // ROW_TILE
    modspec = pl.BlockSpec((1, 1, D_MODEL), lambda i: (i // tpb, 0, 0))
    rw = jnp.pad(router_w, ((0, 0), (0, LANES - N_EXPERTS)))
    return pl.pallas_call(
        _router_kernel,
        grid=(n // ROW_TILE,),
        in_specs=[pl.BlockSpec((ROW_TILE, D_MODEL), lambda i: (i, 0)), modspec, modspec,
                  pl.BlockSpec((D_MODEL, LANES), lambda i: (0, 0))],
        out_specs=[pl.BlockSpec((ROW_TILE, LANES), lambda i: (i, 0)),
                   pl.BlockSpec((ROW_TILE, D_MODEL), lambda i: (i, 0))],
        out_shape=[jax.ShapeDtypeStruct((n, LANES), F32), jax.ShapeDtypeStruct((n, D_MODEL), BF16)],
        compiler_params=_params(("parallel",)),
        name="router",
    )(x, sc, sh, rw)


MOE_SUB = 1024
MOE_NSUB = 1
MOE_ROWS = MOE_SUB * MOE_NSUB
MOE_TILE = 288
MOE_VMEM_LIMIT = 56 * 1024 * 1024


def _moe_kernel(h_ref, gates_ref, w1_ref, w3_ref, w2_ref, o_ref, slot_scr, slott_scr, hs_scr, ys_scr):
    e = pl.program_id(1)
    j = pl.program_id(2)
    last_j = pl.num_programs(2) - 1

    def swiglu(rows_bf16):
        return _swiglu_partial(rows_bf16, w1_ref[0], w3_ref[0], w2_ref[0])

    @pl.when(jnp.logical_and(e == 0, j == 0))
    def _():
        r = lax.broadcasted_iota(jnp.int32, (MOE_SUB, MOE_SUB), 0)
        c = lax.broadcasted_iota(jnp.int32, (MOE_SUB, MOE_SUB), 1)
        before = jnp.where(c < r, 1.0, 0.0).astype(BF16)
        for sb in range(MOE_NSUB):
            mask = gates_ref[sb * MOE_SUB:(sb + 1) * MOE_SUB, :] != 0.0
            rank = jnp.dot(before, jnp.where(mask, 1.0, 0.0).astype(BF16), preferred_element_type=F32)
            slot = jnp.where(mask, rank, -1.0).astype(jnp.int32)
            slot_scr[sb] = slot
            slott_scr[sb] = slot.T
        o_ref[...] = jnp.zeros_like(o_ref)

    lane = lax.broadcasted_iota(jnp.int32, (MOE_SUB, LANES), 1)
    for sb in range(MOE_NSUB):
        rows = slice(sb * MOE_SUB, (sb + 1) * MOE_SUB)
        slot_col = jnp.max(jnp.where(lane == e, slot_scr[sb], -1), -1, keepdims=True)
        n_tiles = (jnp.max(slot_col) + MOE_TILE) // MOE_TILE

        def gather(k, sb=sb, rows=rows):
            slot_row = slott_scr[sb, pl.ds(e, 1), :]
            rr = lax.broadcasted_iota(jnp.int32, (MOE_TILE, MOE_SUB), 0) + k * MOE_TILE
            pick = jnp.where(rr == slot_row, 1.0, 0.0).astype(BF16)
            return jnp.dot(pick, h_ref[rows, :], preferred_element_type=F32).astype(BF16)

        def add_back(k, y, rows=rows, slot_col=slot_col):
            g_col = jnp.sum(jnp.where(lane == e, gates_ref[rows, :], 0.0), -1, keepdims=True)
            cc = lax.broadcasted_iota(jnp.int32, (MOE_SUB, 2 * MOE_TILE), 1)
            cc = jnp.where(cc >= MOE_TILE, cc - MOE_TILE, cc) + k * MOE_TILE
            put = jnp.where(slot_col == cc, 1.0, 0.0).astype(BF16)
            y_hi = y.astype(BF16)
            y_lo = (y - y_hi.astype(F32)).astype(BF16)
            back = jnp.dot(put, jnp.concatenate([y_hi, y_lo], axis=0), preferred_element_type=F32)
            o_ref[rows, :] += g_col * back

        @pl.when(jnp.logical_and(j == 0, n_tiles > 0))
        def _(sb=sb, gather=gather):
            hs_scr[sb] = gather(0)
            ys_scr[sb] = jnp.zeros((MOE_TILE, D_MODEL), F32)

        @pl.when(n_tiles > 0)
        def _(sb=sb):
            ys_scr[sb] += swiglu(hs_scr[sb])

        @pl.when(jnp.logical_and(j == last_j, n_tiles > 0))
        def _(sb=sb, add_back=add_back):
            add_back(0, ys_scr[sb])

        def extra(k, carry, gather=gather, add_back=add_back):
            add_back(k, swiglu(gather(k)))
            return carry

        lax.fori_loop(1, n_tiles, extra, 0)


def _moe(h, gates, w1, w3, w2):
    n = h.shape[0]
    return pl.pallas_call(
        _moe_kernel,
        grid=(n // MOE_ROWS, N_EXPERTS, D_FF // FF_TILE),
        in_specs=[pl.BlockSpec((MOE_ROWS, D_MODEL), lambda i, e, j: (i, 0)),
                  pl.BlockSpec((MOE_ROWS, LANES), lambda i, e, j: (i, 0)),
                  pl.BlockSpec((1, D_MODEL, FF_TILE), lambda i, e, j: (e, 0, j)),
                  pl.BlockSpec((1, D_MODEL, FF_TILE), lambda i, e, j: (e, 0, j)),
                  pl.BlockSpec((1, FF_TILE, D_MODEL), lambda i, e, j: (e, j, 0))],
        out_specs=pl.BlockSpec((MOE_ROWS, D_MODEL), lambda i, e, j: (i, 0)),
        out_shape=jax.ShapeDtypeStruct((n, D_MODEL), F32),
        scratch_shapes=[pltpu.VMEM((MOE_NSUB, MOE_SUB, LANES), jnp.int32),
                        pltpu.VMEM((MOE_NSUB, LANES, MOE_SUB), jnp.int32),
                        pltpu.VMEM((MOE_NSUB, MOE_TILE, D_MODEL), BF16),
                        pltpu.VMEM((MOE_NSUB, MOE_TILE, D_MODEL), F32)],
        compiler_params=_params(("parallel", "arbitrary", "arbitrary"), MOE_VMEM_LIMIT),
        name="moe",
    )(h, gates, w1, w3, w2)


def _residual_ln_kernel(x_ref, f_ref, gate_ref, lg_ref, lb_ref, o_ref):
    o_ref[...] = _layernorm_rows(ALPHA * x_ref[...] + gate_ref[0] * f_ref[...], lg_ref[...], lb_ref[...])


def _residual_ln(x, f, gate, lg, lb, rows_per_mod):
    n = x.shape[0]
    tpb = rows_per_mod // ROW_TILE
    row = pl.BlockSpec((ROW_TILE, D_MODEL), lambda i: (i, 0))
    vec = pl.BlockSpec((1, D_MODEL), lambda i: (0, 0))
    return pl.pallas_call(
        _residual_ln_kernel,
        grid=(n // ROW_TILE,),
        in_specs=[row, row, pl.BlockSpec((1, 1, D_MODEL), lambda i: (i // tpb, 0, 0)), vec, vec],
        out_specs=row,
        out_shape=jax.ShapeDtypeStruct((n, D_MODEL), F32),
        compiler_params=_params(("parallel",)),
        name="residual_ln",
    )(x, f, gate, lg.reshape(1, D_MODEL), lb.reshape(1, D_MODEL))


def _permute_w_in(w):
    pad = jnp.zeros((D_MODEL, U_COLS - ORIG_END), w.dtype)
    return jnp.concatenate([w[:, :ORIG_GATES], w[:, ORIG_SX:ORIG_DT], w[:, ORIG_SZ:ORIG_SX],
                            w[:, ORIG_GATES:ORIG_SZ], w[:, ORIG_DT:ORIG_END], pad], axis=1).astype(BF16)


def _small_row(vals, offset):
    v = vals.reshape(-1).astype(F32)
    return jnp.zeros((1, LANES), F32).at[0, offset:offset + v.shape[0]].set(v)


def _pack_mlstm_state(c, n, m):
    eye = jnp.eye(N_HEADS, dtype=F32)
    shape = c.shape[:2] + (HEAD_LANES, HEAD_LANES)
    cbd = jnp.einsum('bdhke,hg->bdhkge', c, eye).reshape(shape)
    nbd = jnp.einsum('bdhk,hg,e->bdhkge', n, eye, jnp.ones((D_STATE,), F32)).reshape(shape)
    return jnp.concatenate([cbd, nbd], axis=-1), jnp.repeat(m, D_STATE, axis=-1)[:, :, None, :]


def _pack_ssd_state(s):
    sel = (jnp.arange(N_GROUPS)[:, None] == jnp.arange(N_HEADS)[None, :] // 2).astype(F32)
    return jnp.einsum('bdhpn,gh->bdgnhp', s, sel).reshape(s.shape[:2] + (LANES, HEAD_LANES))


def _layer(x, mods, P, l, bsz, t, ctx):
    sh1, sc1, g1, sh2, sc2, g2 = mods
    rows_per_mod = x.shape[0] // sh1.shape[0]
    lam_init = 0.8 - 0.6 * math.exp(-0.3 * l)
    u = _in_proj(x, sc1, sh1, P['w_in'][l], rows_per_mod)

    gate_bias = (_small_row(P['mlstm_gate_b'][l, 0], GATE_I) + _small_row(P['mlstm_gate_b'][l, 1], GATE_F))
    dt_bias = _small_row(P['ssm_dt_bias'][l], GATE_DT)
    alog = _small_row(P['ssm_A_log'][l], GATE_DT)
    dskip = jnp.repeat(P['ssm_D'][l].astype(F32), D_STATE).reshape(1, D_REC)

    if ctx is None:
        att, k_new, v_new = _attention_ctx(u, P['attn_lambda'][l], P['attn_norm_w'][l], lam_init, bsz, t)
        m_init = s_init = None
    else:
        ck, cv, c_c, c_n, c_m, c_s = ctx
        q, k, v = _rope_prep(u, bsz, t)
        k_all = jnp.concatenate([k, ck.astype(BF16)], axis=3)
        v_all = jnp.concatenate([v, cv.astype(BF16)], axis=2)
        att = _attention_lat(q, k_all, v_all, P['attn_lambda'][l], P['attn_norm_w'][l], lam_init, bsz, t)
        m_init = _pack_mlstm_state(c_c, c_n, c_m)
        s_init = _pack_ssd_state(c_s)
    hf, hb, c_out, n_out, m_out = _mlstm(u, gate_bias, bsz, t, m_init)
    xbc = _ssd_conv(u, P['conv_w'][l], P['conv_b'][l], bsz, t)
    yf, yb, s_out = _ssd(xbc, u, dt_bias, alog, dskip, bsz, t, s_init)

    x = _out_proj(att, hf, hb, yf, yb, u, P['w_out'][l], x, g1, P['mlstm_norm_w'][l], P['ssm_norm_w'][l],
                  P['ln_g'][l, 0], P['ln_b'][l, 0], rows_per_mod)
    if l % 2 == 0:
        x = _ffn(x, sc2, sh2, g2, P['ffn_w1'][l // 2], P['ffn_w3'][l // 2], P['ffn_w2'][l // 2],
                 P['ln_g'][l, 1], P['ln_b'][l, 1], rows_per_mod)
    else:
        gates, h2 = _router(x, sc2, sh2, P['router_w'][l // 2], rows_per_mod)
        f = _moe(h2, gates, P['moe_w1'][l // 2], P['moe_w3'][l // 2], P['moe_w2'][l // 2])
        x = _residual_ln(x, f, g2, P['ln_g'][l, 1], P['ln_b'][l, 1], rows_per_mod)
    if ctx is None:
        return x, (k_new, v_new, c_out, n_out[..., 0], m_out[:, :, 0, ::D_STATE], jnp.swapaxes(s_out, -1, -2))
    return x, None


def kernel(x_prompt, x_sample, c, cache_attn_k, cache_attn_v, state_mlstm_C, state_mlstm_n, state_mlstm_m, state_ssm, c_ctx, w_ada, b_ada, w_in, w_out, attn_lambda, attn_norm_w, mlstm_gate_b, mlstm_norm_w, conv_w, conv_b, ssm_A_log, ssm_dt_bias, ssm_D, ssm_norm_w, ln_g, ln_b, ffn_w1, ffn_w3, ffn_w2, router_w, moe_w1, moe_w3, moe_w2):
    bsz, seq, _ = x_prompt.shape
    dbsz, dseq, _ = x_sample.shape
    P = dict(w_in=[_permute_w_in(w_in[l]) for l in range(DEPTH)], w_out=w_out.astype(BF16),
             attn_lambda=attn_lambda, attn_norm_w=attn_norm_w, mlstm_gate_b=mlstm_gate_b,
             mlstm_norm_w=mlstm_norm_w, conv_w=conv_w, conv_b=conv_b, ssm_A_log=ssm_A_log,
             ssm_dt_bias=ssm_dt_bias, ssm_D=ssm_D, ssm_norm_w=ssm_norm_w, ln_g=ln_g, ln_b=ln_b,
             ffn_w1=ffn_w1.astype(BF16), ffn_w3=ffn_w3.astype(BF16), ffn_w2=ffn_w2.astype(BF16),
             router_w=router_w, moe_w1=moe_w1.astype(BF16), moe_w3=moe_w3.astype(BF16),
             moe_w2=moe_w2.astype(BF16))

    cvec = jnp.zeros((8, D_MODEL), F32).at[0].set(c_ctx).at[1:1 + dbsz].set(c)
    mod = _modulation(cvec, w_ada, b_ada)

    def mods_for(l, lo, hi):
        return [mod[l, lo:hi, i * D_MODEL:(i + 1) * D_MODEL][:, None, :] for i in range(6)]

    y_prompt = x_prompt.reshape(bsz * seq, D_MODEL)
    outs = []
    for l in range(DEPTH):
        y_prompt, ctx_out = _layer(y_prompt, mods_for(l, 0, 1), P, l, bsz, seq, None)
        outs.append(ctx_out)

    y_sample = x_sample.reshape(dbsz * dseq, D_MODEL)
    for l in range(DEPTH):
        ctx = (cache_attn_k[:, l], cache_attn_v[:, l], state_mlstm_C[:, l], state_mlstm_n[:, l],
               state_mlstm_m[:, l], state_ssm[:, l])
        y_sample, _ = _layer(y_sample, mods_for(l, 1, 1 + dbsz), P, l, dbsz, dseq, ctx)

    stacked = [jnp.stack([o[i] for o in outs], axis=1) for i in range(6)]
    return (y_prompt.reshape(bsz, seq, D_MODEL), y_sample.reshape(dbsz, dseq, D_MODEL), *stacked)
```

```python
import functools
import math

import jax
import jax.numpy as jnp
from jax import lax
from jax.experimental import pallas as pl
from jax.experimental.pallas import tpu as pltpu

F32 = jnp.float32
BF16 = jnp.bfloat16
HIGHEST = lax.Precision.HIGHEST

D_MODEL = 1024
DEPTH = 2
GRID_W = 64
N_HEADS = 4
D_ATT = 512
D_HEAD_V = 128
D_QK = 64
D_REC = 256
D_STATE = 64
N_GROUPS = 2
D_CONV = 3
D_FF = 2816
N_EXPERTS = 8
ALPHA = (2.0 * DEPTH) ** 0.25
CHUNK = 64
ROPE_BASE = 10000.0
LOG2E = 1.4426950408889634
EPS = 1e-5

COL_AQ, COL_AK, COL_AV = 0, 512, 1024
COL_MQ, COL_MK, COL_MV, COL_MO = 1536, 1792, 2048, 2304
COL_SX, COL_SBC, COL_SZ = 2560, 2816, 3072
COL_SMALL = 3328
U_COLS = 3584
ORIG_GATES, ORIG_SZ, ORIG_SX, ORIG_DT, ORIG_END = 2560, 2576, 2832, 3344, 3352
GATE_I, GATE_F, GATE_DT = 0, 8, 16

LANES = 128
ROW_TILE = 512
SCAN_ROWS = 256
VMEM_LIMIT = 48 * 1024 * 1024

NT_DIMS = (((1,), (1,)), ((), ()))
TN_DIMS = (((0,), (0,)), ((), ()))


def _params(sem, vmem=VMEM_LIMIT):
    return pltpu.CompilerParams(dimension_semantics=sem, vmem_limit_bytes=vmem)


def _silu(x):
    return x * jax.nn.sigmoid(x)


def _bdot(a, b):
    return jnp.dot(a.astype(BF16), b.astype(BF16), preferred_element_type=F32)


def _bdot_nt(a, b):
    return lax.dot_general(a.astype(BF16), b.astype(BF16), NT_DIMS, preferred_element_type=F32)


def _bdot_tn(a, b):
    return lax.dot_general(a.astype(BF16), b.astype(BF16), TN_DIMS, preferred_element_type=F32)


def _layernorm_rows(y, g, b):
    mu = jnp.mean(y, -1, keepdims=True)
    d = y - mu
    var = jnp.mean(d * d, -1, keepdims=True)
    return d * lax.rsqrt(var + EPS) * g + b


def _mod_kernel(c_ref, w_ref, b_ref, o_ref):
    o_ref[0] = jnp.dot(_silu(c_ref[...]), w_ref[0], precision=HIGHEST,
                       preferred_element_type=F32) + b_ref[0]


def _modulation(cvec, w_ada, b_ada):
    tn = 1536
    return pl.pallas_call(
        _mod_kernel,
        grid=(DEPTH, 6 * D_MODEL // tn),
        in_specs=[pl.BlockSpec((8, D_MODEL), lambda l, j: (0, 0)),
                  pl.BlockSpec((1, D_MODEL, tn), lambda l, j: (l, 0, j)),
                  pl.BlockSpec((1, 1, tn), lambda l, j: (l, 0, j))],
        out_specs=pl.BlockSpec((1, 8, tn), lambda l, j: (l, 0, j)),
        out_shape=jax.ShapeDtypeStruct((DEPTH, 8, 6 * D_MODEL), F32),
        compiler_params=_params(("parallel", "parallel")),
        name="modulation",
    )(cvec, w_ada, b_ada.reshape(DEPTH, 1, 6 * D_MODEL))


def _inproj_kernel(x_ref, sc_ref, sh_ref, w_ref, o_ref):
    h = (x_ref[...] * (1.0 + sc_ref[0]) + sh_ref[0]).astype(BF16)
    for n0 in range(0, U_COLS, 512):
        o_ref[:, n0:n0 + 512] = jnp.dot(h, w_ref[:, n0:n0 + 512], preferred_element_type=F32)


def _in_proj(x, sc, sh, w, rows_per_mod):
    n = x.shape[0]
    tpb = rows_per_mod // ROW_TILE
    return pl.pallas_call(
        _inproj_kernel,
        grid=(n // ROW_TILE,),
        in_specs=[pl.BlockSpec((ROW_TILE, D_MODEL), lambda i: (i, 0)),
                  pl.BlockSpec((1, 1, D_MODEL), lambda i: (i // tpb, 0, 0)),
                  pl.BlockSpec((1, 1, D_MODEL), lambda i: (i // tpb, 0, 0)),
                  pl.BlockSpec((D_MODEL, U_COLS), lambda i: (0, 0))],
        out_specs=pl.BlockSpec((ROW_TILE, U_COLS), lambda i: (i, 0)),
        out_shape=jax.ShapeDtypeStruct((n, U_COLS), F32),
        compiler_params=_params(("parallel",)),
        name="in_proj",
    )(x, sc, sh, w)


def _lambda_scalar(lam_ref, lam_init):
    lp = lam_ref[...]
    s01 = jnp.sum(lp[0:1] * lp[1:2], axis=-1, keepdims=True)
    s23 = jnp.sum(lp[2:3] * lp[3:4], axis=-1, keepdims=True)
    return jnp.exp(s01) - jnp.exp(s23) + lam_init


def _softmax_rows(s):
    e = jnp.exp(s - jnp.max(s, -1, keepdims=True))
    return e / jnp.sum(e, -1, keepdims=True)


def _head_norm(o, nw, lam_init):
    return o * lax.rsqrt(jnp.mean(o * o, -1, keepdims=True) + EPS) * nw * (1.0 - lam_init)


def _attn_ctx_kernel(u_ref, lam_ref, nw_ref, att_ref, k_ref, v_ref, *, lam_init):
    lam = _lambda_scalar(lam_ref, lam_init)
    for h in range(N_HEADS):
        v = u_ref[:, COL_AV + h * D_HEAD_V:COL_AV + (h + 1) * D_HEAD_V]
        v_ref[0, h] = v
        ps = []
        for m in range(2):
            c0 = h * D_HEAD_V + m * D_QK
            q = u_ref[:, COL_AQ + c0:COL_AQ + c0 + D_QK] * (D_QK ** -0.5)
            k = u_ref[:, COL_AK + c0:COL_AK + c0 + D_QK]
            k_ref[0, h, m] = k
            ps.append(_softmax_rows(_bdot_nt(q, k)))
        o = _bdot(ps[0] - lam * ps[1], v)
        att_ref[:, h * D_HEAD_V:(h + 1) * D_HEAD_V] = _head_norm(o, nw_ref[...], lam_init).astype(BF16)


def _attention_ctx(u, lam_p, norm_w, lam_init, bsz, t):
    n = bsz * t
    return pl.pallas_call(
        functools.partial(_attn_ctx_kernel, lam_init=lam_init),
        grid=(bsz,),
        in_specs=[pl.BlockSpec((t, 3 * D_ATT), lambda b: (b, 0)),
                  pl.BlockSpec((4, D_QK), lambda b: (0, 0)),
                  pl.BlockSpec((1, D_HEAD_V), lambda b: (0, 0))],
        out_specs=[pl.BlockSpec((t, D_ATT), lambda b: (b, 0)),
                   pl.BlockSpec((1, N_HEADS, 2, t, D_QK), lambda b: (b, 0, 0, 0, 0)),
                   pl.BlockSpec((1, N_HEADS, t, D_HEAD_V), lambda b: (b, 0, 0, 0))],
        out_shape=[jax.ShapeDtypeStruct((n, D_ATT), BF16),
                   jax.ShapeDtypeStruct((bsz, N_HEADS, 2, t, D_QK), F32),
                   jax.ShapeDtypeStruct((bsz, N_HEADS, t, D_HEAD_V), F32)],
        compiler_params=_params(("parallel",)),
        name="attn_ctx",
    )(u, lam_p, norm_w.reshape(1, D_HEAD_V))


def _rope_kernel(u_ref, cos_ref, sa_ref, sb_ref, q_ref, k_ref, v_ref):
    cos, sa, sb = cos_ref[...], sa_ref[...], sb_ref[...]

    def rope(x):
        return x * cos + pltpu.roll(x, LANES - 16, 1) * sa + pltpu.roll(x, 16, 1) * sb

    for h in range(N_HEADS):
        q = rope(u_ref[:, COL_AQ + h * D_HEAD_V:COL_AQ + (h + 1) * D_HEAD_V]) * (LOG2E * D_QK ** -0.5)
        k = rope(u_ref[:, COL_AK + h * D_HEAD_V:COL_AK + (h + 1) * D_HEAD_V])
        for m in range(2):
            q_ref[0, h, m] = q[:, m * D_QK:(m + 1) * D_QK].astype(BF16)
            k_ref[0, h, m] = k[:, m * D_QK:(m + 1) * D_QK].astype(BF16)
        v_ref[0, h] = u_ref[:, COL_AV + h * D_HEAD_V:COL_AV + (h + 1) * D_HEAD_V].astype(BF16)


def _rope_tables(t):
    rows = jnp.repeat(jnp.arange(t // GRID_W, dtype=F32), GRID_W)
    cols = jnp.tile(jnp.arange(GRID_W, dtype=F32), t // GRID_W)
    half = D_QK // 2
    inv = ROPE_BASE ** (-jnp.arange(0, half, 2, dtype=F32) / half)
    ang_r = rows[:, None] * inv
    ang_c = cols[:, None] * inv
    ang = jnp.concatenate([ang_r, ang_r, ang_c, ang_c], -1)
    cos, sin = jnp.cos(ang), jnp.sin(ang)
    quarter = (jnp.arange(D_QK) // (D_QK // 4)) % 2
    sa = jnp.where(quarter == 0, -sin, 0.0)
    sb = jnp.where(quarter == 1, sin, 0.0)
    tile2 = lambda a: jnp.concatenate([a, a], -1)
    return tile2(cos), tile2(sa), tile2(sb)


def _rope_prep(u, bsz, t):
    tr = 512
    nb = t // tr
    cos, sa, sb = _rope_tables(t)
    tab = pl.BlockSpec((tr, LANES), lambda b, i: (i, 0))
    return pl.pallas_call(
        _rope_kernel,
        grid=(bsz, nb),
        in_specs=[pl.BlockSpec((tr, 3 * D_ATT), lambda b, i: (b * nb + i, 0)), tab, tab, tab],
        out_specs=[pl.BlockSpec((1, N_HEADS, 2, tr, D_QK), lambda b, i: (b, 0, 0, i, 0)),
                   pl.BlockSpec((1, N_HEADS, 2, tr, D_QK), lambda b, i: (b, 0, 0, i, 0)),
                   pl.BlockSpec((1, N_HEADS, tr, D_HEAD_V), lambda b, i: (b, 0, i, 0))],
        out_shape=[jax.ShapeDtypeStruct((bsz, N_HEADS, 2, t, D_QK), BF16),
                   jax.ShapeDtypeStruct((bsz, N_HEADS, 2, t, D_QK), BF16),
                   jax.ShapeDtypeStruct((bsz, N_HEADS, t, D_HEAD_V), BF16)],
        compiler_params=_params(("parallel", "parallel")),
        name="rope_prep",
    )(u, cos, sa, sb)


def _attn_lat_kernel(q_ref, k_ref, v_ref, lam_ref, nw_ref, o_ref, *, lam_init):
    lam = _lambda_scalar(lam_ref, lam_init)
    es, sums = [], []
    for m in range(2):
        s = lax.dot_general(q_ref[0, 0, m], k_ref[0, 0, m], NT_DIMS, preferred_element_type=F32)
        e = jnp.exp2(s - jnp.max(s, -1, keepdims=True))
        es.append(e)
        sums.append(jnp.sum(e, -1, keepdims=True))
    a = es[0] - (lam * sums[0] / sums[1]) * es[1]
    o = _bdot(a, v_ref[0, 0]) / sums[0]
    o_ref[...] = _head_norm(o, nw_ref[...], lam_init).astype(BF16)


def _attention_lat(q, k_all, v_all, lam_p, norm_w, lam_init, bsz, t):
    tq = 256
    nq = t // tq
    s = k_all.shape[3]
    return pl.pallas_call(
        functools.partial(_attn_lat_kernel, lam_init=lam_init),
        grid=(bsz, N_HEADS, nq),
        in_specs=[pl.BlockSpec((1, 1, 2, tq, D_QK), lambda b, h, i: (b, h, 0, i, 0)),
                  pl.BlockSpec((1, 1, 2, s, D_QK), lambda b, h, i: (b, h, 0, 0, 0)),
                  pl.BlockSpec((1, 1, s, D_HEAD_V), lambda b, h, i: (b, h, 0, 0)),
                  pl.BlockSpec((4, D_QK), lambda b, h, i: (0, 0)),
                  pl.BlockSpec((1, D_HEAD_V), lambda b, h, i: (0, 0))],
        out_specs=pl.BlockSpec((tq, D_HEAD_V), lambda b, h, i: (b * nq + i, h)),
        out_shape=jax.ShapeDtypeStruct((bsz * t, D_ATT), BF16),
        compiler_params=_params(("parallel", "parallel", "parallel")),
        name="attn_lat",
    )(q, k_all, v_all, lam_p, norm_w.reshape(1, D_HEAD_V))


HEAD_LANES = N_HEADS * D_STATE


def _scan_consts():
    t = lax.broadcasted_iota(jnp.int32, (CHUNK, HEAD_LANES), 0)
    s = lax.broadcasted_iota(jnp.int32, (CHUNK, HEAD_LANES), 1) & (CHUNK - 1)
    r = lax.broadcasted_iota(jnp.int32, (CHUNK, CHUNK), 0)
    c = lax.broadcasted_iota(jnp.int32, (CHUNK, CHUNK), 1)
    reads = (s <= t, s >= t)
    tri = (jnp.where(c <= r, 1.0, 0.0), jnp.where(c >= r, 1.0, 0.0))
    return reads, tri, s == t


def _group_mask(rows, cols, row_shift, col_shift):
    r = lax.broadcasted_iota(jnp.int32, (rows, cols), 0) >> 6
    c = (lax.broadcasted_iota(jnp.int32, (rows, cols), 1) >> 6) & (N_HEADS - 1)
    return (r >> row_shift) == (c >> col_shift)


def _spread(x, chans):
    return jnp.concatenate([jnp.broadcast_to(x[:, c:c + 1], (CHUNK, D_STATE)) for c in chans], axis=1)


def _stack_heads(x):
    return jnp.concatenate([x] * N_HEADS, axis=0)


def _mlstm_chunk(q4, k4, v4, g, d, cn_prev, m_prev, reads, tri, eye, bd):
    li = _spread(g, [GATE_I + d * N_HEADS + h for h in range(N_HEADS)])
    lf = _spread(jax.nn.log_sigmoid(g), [GATE_F + d * N_HEADS + h for h in range(N_HEADS)])
    bc = jnp.dot(tri[d], lf, precision=HIGHEST, preferred_element_type=F32)
    btot = jnp.sum(lf, 0, keepdims=True)
    b_row = jnp.sum(jnp.where(reads[1 - d], lf, 0.0), 0, keepdims=True)
    li_row = jnp.sum(jnp.where(eye, li, 0.0), 0, keepdims=True)
    dm = jnp.where(reads[d], bc - b_row + li_row, -jnp.inf)
    rmax = jnp.concatenate(
        [jnp.broadcast_to(jnp.max(dm[:, h * D_STATE:(h + 1) * D_STATE], -1, keepdims=True), (CHUNK, D_STATE))
         for h in range(N_HEADS)], axis=1)
    inter = bc + m_prev
    m_t = jnp.maximum(inter, rmax)
    w_inter = jnp.exp(inter - m_t)
    qs = (q4 * (D_STATE ** -0.5)).astype(BF16)
    kbd = jnp.where(bd[:, :HEAD_LANES], _stack_heads(k4.astype(BF16)), 0)
    s4 = lax.dot_general(qs, kbd, NT_DIMS, preferred_element_type=F32) * jnp.exp(dm - m_t)
    vo = jnp.concatenate([v4.astype(BF16), jnp.ones((CHUNK, HEAD_LANES), BF16)], axis=1)
    vbd = jnp.where(bd, _stack_heads(vo), 0)
    nd = (jnp.concatenate([w_inter, w_inter], axis=1)
          * jnp.dot(qs, cn_prev.astype(BF16), preferred_element_type=F32)
          + jnp.dot(s4.astype(BF16), vbd, preferred_element_type=F32))
    hc = nd[:, :HEAD_LANES] / jnp.maximum(jnp.abs(nd[:, HEAD_LANES:]), jnp.exp(-m_t))
    gcol = btot - bc + li
    m_new = jnp.maximum(btot + m_prev, jnp.max(gcol, 0, keepdims=True))
    w_c = jnp.exp(btot + m_prev - m_new)
    kw = (k4 * jnp.exp(gcol - m_new)).astype(BF16)
    dcn = lax.dot_general(kw, vo, TN_DIMS, preferred_element_type=F32)
    cn_new = jnp.concatenate([w_c, w_c], axis=1) * cn_prev + jnp.where(bd, dcn, 0.0)
    return hc, cn_new, m_new


def _mlstm_kernel(*refs, nblk, rows, zero_init):
    if zero_init:
        (qf_ref, kf_ref, vf_ref, gf_ref, qb_ref, kb_ref, vb_ref, gb_ref, bias_ref,
         hf_ref, hb_ref, cout_ref, nout_ref, mout_ref, cn_scr, m_scr) = refs
    else:
        (qf_ref, kf_ref, vf_ref, gf_ref, qb_ref, kb_ref, vb_ref, gb_ref, bias_ref, c0_ref, m0_ref,
         hf_ref, hb_ref, cout_ref, nout_ref, mout_ref, cn_scr, m_scr) = refs
    j = pl.program_id(1)
    nchunk = rows // CHUNK

    @pl.when(j == 0)
    def _():
        if zero_init:
            cn_scr[...] = jnp.zeros_like(cn_scr)
            m_scr[...] = jnp.zeros_like(m_scr)
        else:
            cn_scr[...] = c0_ref[0]
            m_scr[...] = m0_ref[0]

    reads, tri, eye = _scan_consts()
    bd = _group_mask(HEAD_LANES, 2 * HEAD_LANES, 0, 0)

    def chunk(ci, carry):
        for d in range(2):
            q_ref, k_ref, v_ref, g_ref, h_ref = ((qf_ref, kf_ref, vf_ref, gf_ref, hf_ref) if d == 0
                                                 else (qb_ref, kb_ref, vb_ref, gb_ref, hb_ref))
            cj = ci if d == 0 else nchunk - 1 - ci
            rs = pl.ds(pl.multiple_of(cj * CHUNK, CHUNK), CHUNK)
            g = g_ref[rs, :] + bias_ref[...]
            hc, cn_new, m_new = _mlstm_chunk(q_ref[rs, :], k_ref[rs, :], v_ref[rs, :], g, d,
                                             cn_scr[d], m_scr[d], reads, tri, eye, bd)
            h_ref[rs, :] = hc
            cn_scr[d] = cn_new
            m_scr[d] = m_new
        return carry

    lax.fori_loop(0, nchunk, chunk, 0, unroll=True)

    @pl.when(j == nblk - 1)
    def _():
        for d in range(2):
            for h in range(N_HEADS):
                r0, r1 = h * D_STATE, (h + 1) * D_STATE
                cout_ref[0, d, h] = cn_scr[d, r0:r1, r0:r1]
                nout_ref[0, d, h] = cn_scr[d, r0:r1, HEAD_LANES + r0:HEAD_LANES + r1]
        mout_ref[0] = m_scr[...]


def _mlstm(u, gate_bias, bsz, t, init):
    rows = min(t, SCAN_ROWS)
    nblk = t // rows
    zero_init = init is None

    def fwd(col):
        return lambda b, j: (b * nblk + j, col)

    def bwd(col):
        return lambda b, j: (b * nblk + nblk - 1 - j, col)

    cq, ck, cv, cg = COL_MQ // D_REC, COL_MK // D_REC, COL_MV // D_REC, COL_SMALL // LANES
    in_specs = []
    for mk in (fwd, bwd):
        in_specs += [pl.BlockSpec((rows, D_REC), mk(cq)), pl.BlockSpec((rows, D_REC), mk(ck)),
                     pl.BlockSpec((rows, D_REC), mk(cv)), pl.BlockSpec((rows, LANES), mk(cg))]
    in_specs.append(pl.BlockSpec((1, LANES), lambda b, j: (0, 0)))
    args = [u] * 8 + [gate_bias]
    state_c = pl.BlockSpec((1, 2, HEAD_LANES, 2 * HEAD_LANES), lambda b, j: (b, 0, 0, 0))
    state_m = pl.BlockSpec((1, 2, 1, HEAD_LANES), lambda b, j: (b, 0, 0, 0))
    head_blocks = pl.BlockSpec((1, 2, N_HEADS, D_STATE, D_STATE), lambda b, j: (b, 0, 0, 0, 0))
    if not zero_init:
        in_specs += [state_c, state_m]
        args += list(init)
    return pl.pallas_call(
        functools.partial(_mlstm_kernel, nblk=nblk, rows=rows, zero_init=zero_init),
        grid=(bsz, nblk),
        in_specs=in_specs,
        out_specs=[pl.BlockSpec((rows, D_REC), fwd(0)), pl.BlockSpec((rows, D_REC), bwd(0)),
                   head_blocks, head_blocks, state_m],
        out_shape=[jax.ShapeDtypeStruct((bsz * t, D_REC), F32), jax.ShapeDtypeStruct((bsz * t, D_REC), F32),
                   jax.ShapeDtypeStruct((bsz, 2, N_HEADS, D_STATE, D_STATE), F32),
                   jax.ShapeDtypeStruct((bsz, 2, N_HEADS, D_STATE, D_STATE), F32),
                   jax.ShapeDtypeStruct((bsz, 2, 1, HEAD_LANES), F32)],
        scratch_shapes=[pltpu.VMEM((2, HEAD_LANES, 2 * HEAD_LANES), F32),
                        pltpu.VMEM((2, 1, HEAD_LANES), F32)],
        compiler_params=_params(("parallel", "arbitrary")),
        name="mlstm_scan",
    )(*args)


def _conv_kernel(x_ref, prev_ref, next_ref, w_ref, b_ref, o_ref, *, nblk):
    i = pl.program_id(1)
    x = x_ref[...]
    rows = x.shape[0]
    r = lax.broadcasted_iota(jnp.int32, x.shape, 0)
    prev_row = prev_ref[7:8, :] * (i > 0).astype(F32)
    next_row = next_ref[0:1, :] * (i < nblk - 1).astype(F32)
    xm = jnp.where(r == 0, prev_row, pltpu.roll(x, 1, 0))
    xp = jnp.where(r == rows - 1, next_row, pltpu.roll(x, rows - 1, 0))
    w = w_ref[...]
    o_ref[...] = _silu(xm * w[0:1] + x * w[1:2] + xp * w[2:3] + b_ref[...])


def _ssd_conv(u, conv_w, conv_b, bsz, t):
    rows = min(t, 512)
    nblk = t // rows
    r8 = rows // 8
    width = 2 * D_REC
    c0 = COL_SX // width
    return pl.pallas_call(
        functools.partial(_conv_kernel, nblk=nblk),
        grid=(bsz, nblk),
        in_specs=[pl.BlockSpec((rows, width), lambda b, i: (b * nblk + i, c0)),
                  pl.BlockSpec((8, width), lambda b, i: (jnp.maximum((b * nblk + i) * r8 - 1, 0), c0)),
                  pl.BlockSpec((8, width), lambda b, i: (jnp.minimum((b * nblk + i + 1) * r8,
                                                                     bsz * nblk * r8 - 1), c0)),
                  pl.BlockSpec((D_CONV, width), lambda b, i: (0, 0)),
                  pl.BlockSpec((1, width), lambda b, i: (0, 0))],
        out_specs=pl.BlockSpec((rows, width), lambda b, i: (b * nblk + i, 0)),
        out_shape=jax.ShapeDtypeStruct((bsz * t, width), F32),
        compiler_params=_params(("parallel", "parallel")),
        name="ssd_conv",
    )(u, u, u, conv_w, conv_b.reshape(1, width))


def _ssd_chunk(x4, bcm, dt128, da128, d, sg_prev, reads, tri, b_sel, s_sel, bd):
    chans = [GATE_DT + d * N_HEADS + h for h in range(N_HEADS)]
    dt = _spread(dt128, chans)
    da = _spread(da128, chans)
    ac = jnp.dot(tri[d], da, precision=HIGHEST, preferred_element_type=F32)
    atot = jnp.sum(da, 0, keepdims=True)
    a_row = jnp.sum(jnp.where(reads[1 - d], da, 0.0), 0, keepdims=True)
    decay = jnp.exp(jnp.where(reads[d], ac - a_row, -jnp.inf))
    bmat = bcm[:, :LANES].astype(BF16)
    cmat = bcm[:, LANES:].astype(BF16)
    bbd = jnp.where(b_sel, _stack_heads(bmat), 0)
    g4 = lax.dot_general(cmat, bbd, NT_DIMS, preferred_element_type=F32)
    xbd = jnp.where(bd, _stack_heads((x4 * dt).astype(BF16)), 0)
    y = (jnp.dot((g4 * decay).astype(BF16), xbd, preferred_element_type=F32)
         + jnp.dot(cmat, sg_prev.astype(BF16), preferred_element_type=F32) * jnp.exp(ac))
    w = jnp.exp(atot - ac) * dt
    dsg = lax.dot_general(bmat, (x4 * w).astype(BF16), TN_DIMS, preferred_element_type=F32)
    sg_new = jnp.exp(atot) * sg_prev + jnp.where(s_sel, dsg, 0.0)
    return y, sg_new


def _ssd_kernel(*refs, nblk, rows, zero_init):
    if zero_init:
        (xf_ref, bcf_ref, gf_ref, xb_ref, bcb_ref, gb_ref, dtb_ref, alog_ref, dskip_ref,
         yf_ref, yb_ref, sout_ref, s_scr) = refs
    else:
        (xf_ref, bcf_ref, gf_ref, xb_ref, bcb_ref, gb_ref, dtb_ref, alog_ref, dskip_ref, s0_ref,
         yf_ref, yb_ref, sout_ref, s_scr) = refs
    j = pl.program_id(1)
    nchunk = rows // CHUNK

    @pl.when(j == 0)
    def _():
        if zero_init:
            s_scr[...] = jnp.zeros_like(s_scr)
        else:
            s_scr[...] = s0_ref[0]

    reads, tri, _ = _scan_consts()
    bd = _group_mask(HEAD_LANES, HEAD_LANES, 0, 0)
    b_sel = _group_mask(HEAD_LANES, LANES, 1, 0)
    s_sel = _group_mask(LANES, HEAD_LANES, 0, 1)
    a_coef = -jnp.exp(alog_ref[...])

    def chunk(ci, carry):
        for d in range(2):
            x_ref, bc_ref, g_ref, y_ref = ((xf_ref, bcf_ref, gf_ref, yf_ref) if d == 0
                                           else (xb_ref, bcb_ref, gb_ref, yb_ref))
            cj = ci if d == 0 else nchunk - 1 - ci
            rs = pl.ds(pl.multiple_of(cj * CHUNK, CHUNK), CHUNK)
            dt128 = jax.nn.softplus(g_ref[rs, :] + dtb_ref[...])
            x4 = x_ref[rs, :]
            y, sg_new = _ssd_chunk(x4, bc_ref[rs, :], dt128, dt128 * a_coef, d, s_scr[d],
                                   reads, tri, b_sel, s_sel, bd)
            if d == 0:
                y = y + dskip_ref[...] * x4
            y_ref[rs, :] = y
            s_scr[d] = sg_new
        return carry

    lax.fori_loop(0, nchunk, chunk, 0, unroll=True)

    @pl.when(j == nblk - 1)
    def _():
        for d in range(2):
            for h in range(N_HEADS):
                g0 = (h // 2) * D_STATE
                sout_ref[0, d, h] = s_scr[d, g0:g0 + D_STATE, h * D_STATE:(h + 1) * D_STATE]


def _ssd(xbc, u, dt_bias_row, alog_row, dskip_row, bsz, t, init):
    rows = min(t, SCAN_ROWS)
    nblk = t // rows
    zero_init = init is None

    def fwd(col):
        return lambda b, j: (b * nblk + j, col)

    def bwd(col):
        return lambda b, j: (b * nblk + nblk - 1 - j, col)

    in_specs = []
    for mk in (fwd, bwd):
        in_specs += [pl.BlockSpec((rows, D_REC), mk(0)), pl.BlockSpec((rows, D_REC), mk(1)),
                     pl.BlockSpec((rows, LANES), mk(COL_SMALL // LANES))]
    in_specs += [pl.BlockSpec((1, LANES), lambda b, j: (0, 0)),
                 pl.BlockSpec((1, LANES), lambda b, j: (0, 0)),
                 pl.BlockSpec((1, D_REC), lambda b, j: (0, 0))]
    args = [xbc, xbc, u, xbc, xbc, u, dt_bias_row, alog_row, dskip_row]
    state = pl.BlockSpec((1, 2, LANES, HEAD_LANES), lambda b, j: (b, 0, 0, 0))
    if not zero_init:
        in_specs.append(state)
        args.append(init)
    return pl.pallas_call(
        functools.partial(_ssd_kernel, nblk=nblk, rows=rows, zero_init=zero_init),
        grid=(bsz, nblk),
        in_specs=in_specs,
        out_specs=[pl.BlockSpec((rows, D_REC), fwd(0)), pl.BlockSpec((rows, D_REC), bwd(0)),
                   pl.BlockSpec((1, 2, N_HEADS, D_STATE, D_STATE), lambda b, j: (b, 0, 0, 0, 0))],
        out_shape=[jax.ShapeDtypeStruct((bsz * t, D_REC), F32), jax.ShapeDtypeStruct((bsz * t, D_REC), F32),
                   jax.ShapeDtypeStruct((bsz, 2, N_HEADS, D_STATE, D_STATE), F32)],
        scratch_shapes=[pltpu.VMEM((2, LANES, HEAD_LANES), F32)],
        compiler_params=_params(("parallel", "arbitrary")),
        name="ssd_scan",
    )(*args)


def _outproj_kernel(att_ref, hf_ref, hb_ref, mo_ref, yf_ref, yb_ref, z_ref, w_ref, x_ref, gate_ref,
                    mnw_ref, snw_ref, lg_ref, lb_ref, o_ref):
    hh = hf_ref[...] + hb_ref[...]
    parts = []
    for h in range(N_HEADS):
        xh = hh[:, h * D_STATE:(h + 1) * D_STATE]
        mu = jnp.mean(xh, -1, keepdims=True)
        dlt = xh - mu
        var = jnp.mean(dlt * dlt, -1, keepdims=True)
        parts.append(dlt * lax.rsqrt(var + EPS))
    ml = jax.nn.sigmoid(mo_ref[...]) * jnp.concatenate(parts, axis=1) * mnw_ref[...]
    yz = (yf_ref[...] + yb_ref[...]) * _silu(z_ref[...])
    parts = []
    for grp in range(N_GROUPS):
        yg = yz[:, grp * LANES:(grp + 1) * LANES]
        parts.append(yg * lax.rsqrt(jnp.mean(yg * yg, -1, keepdims=True) + EPS))
    ssm = jnp.concatenate(parts, axis=1) * snw_ref[...]
    mixed = (jnp.dot(att_ref[...], w_ref[0:D_ATT], preferred_element_type=F32)
             + _bdot(ml, w_ref[D_ATT:D_ATT + D_REC])
             + _bdot(ssm, w_ref[D_ATT + D_REC:D_MODEL]))
    y = ALPHA * x_ref[...] + gate_ref[0] * mixed
    o_ref[...] = _layernorm_rows(y, lg_ref[...], lb_ref[...])


def _out_proj(att, hf, hb, yf, yb, u, w, x, gate, mnw, snw, lg, lb, rows_per_mod):
    n = x.shape[0]
    tpb = rows_per_mod // ROW_TILE
    row = lambda width, col: pl.BlockSpec((ROW_TILE, width), lambda i: (i, col))
    vec = lambda width: pl.BlockSpec((1, width), lambda i: (0, 0))
    return pl.pallas_call(
        _outproj_kernel,
        grid=(n // ROW_TILE,),
        in_specs=[row(D_ATT, 0), row(D_REC, 0), row(D_REC, 0), row(D_REC, COL_MO // D_REC),
                  row(D_REC, 0), row(D_REC, 0), row(D_REC, COL_SZ // D_REC),
                  pl.BlockSpec((D_MODEL, D_MODEL), lambda i: (0, 0)),
                  row(D_MODEL, 0),
                  pl.BlockSpec((1, 1, D_MODEL), lambda i: (i // tpb, 0, 0)),
                  vec(D_REC), vec(D_REC), vec(D_MODEL), vec(D_MODEL)],
        out_specs=row(D_MODEL, 0),
        out_shape=jax.ShapeDtypeStruct((n, D_MODEL), F32),
        compiler_params=_params(("parallel",)),
        name="out_proj",
    )(att, hf, hb, u, yf, yb, u, w, x, gate, mnw.reshape(1, D_REC), snw.reshape(1, D_REC),
      lg.reshape(1, D_MODEL), lb.reshape(1, D_MODEL))


FF_TILE = D_FF // 2
FFN_ROWS = 1024
FFN_SUB = 512
FFN_VMEM_LIMIT = 56 * 1024 * 1024


def _swiglu_partial(h, w1, w3, w2):
    a = jnp.dot(h, w1, preferred_element_type=F32)
    b = jnp.dot(h, w3, preferred_element_type=F32)
    return jnp.dot((_silu(a) * b).astype(BF16), w2, preferred_element_type=F32)


def _ffn_kernel(x_ref, sc_ref, sh_ref, gate_ref, w1_ref, w3_ref, w2_ref, lg_ref, lb_ref, o_ref,
                h_scr, acc_scr):
    j = pl.program_id(1)

    @pl.when(j == 0)
    def _():
        h_scr[...] = (x_ref[...] * (1.0 + sc_ref[0]) + sh_ref[0]).astype(BF16)
        acc_scr[...] = jnp.zeros_like(acc_scr)

    for r0 in range(0, FFN_ROWS, FFN_SUB):
        rows = slice(r0, r0 + FFN_SUB)
        acc_scr[rows, :] += _swiglu_partial(h_scr[rows, :], w1_ref[...], w3_ref[...], w2_ref[...])

    @pl.when(j == pl.num_programs(1) - 1)
    def _():
        y = ALPHA * x_ref[...] + gate_ref[0] * acc_scr[...]
        o_ref[...] = _layernorm_rows(y, lg_ref[...], lb_ref[...])


def _ffn(x, sc, sh, gate, w1, w3, w2, lg, lb, rows_per_mod):
    n = x.shape[0]
    tpb = rows_per_mod // FFN_ROWS
    modspec = pl.BlockSpec((1, 1, D_MODEL), lambda i, j: (i // tpb, 0, 0))
    vec = pl.BlockSpec((1, D_MODEL), lambda i, j: (0, 0))
    return pl.pallas_call(
        _ffn_kernel,
        grid=(n // FFN_ROWS, D_FF // FF_TILE),
        in_specs=[pl.BlockSpec((FFN_ROWS, D_MODEL), lambda i, j: (i, 0)), modspec, modspec, modspec,
                  pl.BlockSpec((D_MODEL, FF_TILE), lambda i, j: (0, j)),
                  pl.BlockSpec((D_MODEL, FF_TILE), lambda i, j: (0, j)),
                  pl.BlockSpec((FF_TILE, D_MODEL), lambda i, j: (j, 0)), vec, vec],
        out_specs=pl.BlockSpec((FFN_ROWS, D_MODEL), lambda i, j: (i, 0)),
        out_shape=jax.ShapeDtypeStruct((n, D_MODEL), F32),
        scratch_shapes=[pltpu.VMEM((FFN_ROWS, D_MODEL), BF16), pltpu.VMEM((FFN_ROWS, D_MODEL), F32)],
        compiler_params=_params(("parallel", "arbitrary"), FFN_VMEM_LIMIT),
        name="ffn_dense",
    )(x, sc, sh, gate, w1, w3, w2, lg.reshape(1, D_MODEL), lb.reshape(1, D_MODEL))


def _router_kernel(x_ref, sc_ref, sh_ref, rw_ref, gates_ref, h_ref):
    h = x_ref[...] * (1.0 + sc_ref[0]) + sh_ref[0]
    h_ref[...] = h.astype(BF16)
    logits = jnp.dot(h, rw_ref[...], precision=HIGHEST, preferred_element_type=F32)
    lane = lax.broadcasted_iota(jnp.int32, logits.shape, 1)
    valid = lane < N_EXPERTS
    p = jnp.where(valid, _softmax_rows(jnp.where(valid, logits, -jnp.inf)), -2.0)
    p1 = jnp.max(p, -1, keepdims=True)
    i1 = jnp.min(jnp.where(p == p1, lane, LANES), -1, keepdims=True)
    rest = jnp.where(lane == i1, -1.0, p)
    p2 = jnp.max(rest, -1, keepdims=True)
    i2 = jnp.min(jnp.where(rest == p2, lane, LANES), -1, keepdims=True)
    tot = p1 + p2
    gates_ref[...] = jnp.where(lane == i1, p1 / tot, jnp.where(lane == i2, p2 / tot, 0.0))


def _router(x, sc, sh, router_w, rows_per_mod):
    n = x.shape[0]
    tpb = rows_per_mod // ROW_TILE
    modspec = pl.BlockSpec((1, 1, D_MODEL), lambda i: (i // tpb, 0, 0))
    rw = jnp.pad(router_w, ((0, 0), (0, LANES - N_EXPERTS)))
    return pl.pallas_call(
        _router_kernel,
        grid=(n // ROW_TILE,),
        in_specs=[pl.BlockSpec((ROW_TILE, D_MODEL), lambda i: (i, 0)), modspec, modspec,
                  pl.BlockSpec((D_MODEL, LANES), lambda i: (0, 0))],
        out_specs=[pl.BlockSpec((ROW_TILE, LANES), lambda i: (i, 0)),
                   pl.BlockSpec((ROW_TILE, D_MODEL), lambda i: (i, 0))],
        out_shape=[jax.ShapeDtypeStruct((n, LANES), F32), jax.ShapeDtypeStruct((n, D_MODEL), BF16)],
        compiler_params=_params(("parallel",)),
        name="router",
    )(x, sc, sh, rw)


MOE_SUB = 1024
MOE_NSUB = 2
MOE_ROWS = MOE_SUB * MOE_NSUB
MOE_TILE = 288
MOE_GATHER_GROUP = 4
MOE_VMEM_LIMIT = 60 * 1024 * 1024


def _moe_kernel(h_ref, gates_ref, w1_ref, w3_ref, w2_ref, o_ref, slot_scr, slott_scr, hs_scr, ys_scr):
    e = pl.program_id(1)
    j = pl.program_id(2)
    last_j = pl.num_programs(2) - 1

    def swiglu(rows_bf16):
        return _swiglu_partial(rows_bf16, w1_ref[0], w3_ref[0], w2_ref[0])

    def pick(sb, expert, k):
        slot_row = slott_scr[sb, pl.ds(expert, 1), :]
        rr = lax.broadcasted_iota(jnp.int32, (MOE_TILE, MOE_SUB), 0) + k * MOE_TILE
        return jnp.where(rr == slot_row, 1.0, 0.0).astype(BF16)

    @pl.when(jnp.logical_and(e == 0, j == 0))
    def _():
        r = lax.broadcasted_iota(jnp.int32, (MOE_SUB, MOE_SUB), 0)
        c = lax.broadcasted_iota(jnp.int32, (MOE_SUB, MOE_SUB), 1)
        before = jnp.where(c < r, 1.0, 0.0).astype(BF16)
        for sb in range(MOE_NSUB):
            rows = slice(sb * MOE_SUB, (sb + 1) * MOE_SUB)
            mask = gates_ref[rows, :] != 0.0
            rank = jnp.dot(before, jnp.where(mask, 1.0, 0.0).astype(BF16), preferred_element_type=F32)
            slot = jnp.where(mask, rank, -1.0).astype(jnp.int32)
            slot_scr[sb] = slot
            slott_scr[sb] = slot.T
            for e0 in range(0, N_EXPERTS, MOE_GATHER_GROUP):
                picks = jnp.concatenate([pick(sb, ex, 0) for ex in range(e0, e0 + MOE_GATHER_GROUP)], axis=0)
                got = jnp.dot(picks, h_ref[rows, :], preferred_element_type=F32).astype(BF16)
                for g in range(MOE_GATHER_GROUP):
                    hs_scr[e0 + g, sb * MOE_TILE:(sb + 1) * MOE_TILE, :] = got[g * MOE_TILE:(g + 1) * MOE_TILE]
        o_ref[...] = jnp.zeros_like(o_ref)

    part = swiglu(hs_scr[e])

    @pl.when(j == 0)
    def _():
        ys_scr[...] = part

    @pl.when(j > 0)
    def _():
        ys_scr[...] += part

    lane = lax.broadcasted_iota(jnp.int32, (MOE_SUB, LANES), 1)
    for sb in range(MOE_NSUB):
        rows = slice(sb * MOE_SUB, (sb + 1) * MOE_SUB)
        slot_col = jnp.max(jnp.where(lane == e, slot_scr[sb], -1), -1, keepdims=True)
        n_tiles = (jnp.max(slot_col) + MOE_TILE) // MOE_TILE

        def gather(k, sb=sb, rows=rows):
            return jnp.dot(pick(sb, e, k), h_ref[rows, :], preferred_element_type=F32).astype(BF16)

        def add_back(k, y, rows=rows, slot_col=slot_col):
            g_col = jnp.sum(jnp.where(lane == e, gates_ref[rows, :], 0.0), -1, keepdims=True)
            cc = lax.broadcasted_iota(jnp.int32, (MOE_SUB, 2 * MOE_TILE), 1)
            cc = jnp.where(cc >= MOE_TILE, cc - MOE_TILE, cc) + k * MOE_TILE
            put = jnp.where(slot_col == cc, 1.0, 0.0).astype(BF16)
            y_hi = y.astype(BF16)
            y_lo = (y - y_hi.astype(F32)).astype(BF16)
            back = jnp.dot(put, jnp.concatenate([y_hi, y_lo], axis=0), preferred_element_type=F32)
            o_ref[rows, :] += g_col * back

        @pl.when(jnp.logical_and(j == last_j, n_tiles > 0))
        def _(sb=sb, add_back=add_back):
            add_back(0, ys_scr[sb * MOE_TILE:(sb + 1) * MOE_TILE, :])

        def extra(k, carry, gather=gather, add_back=add_back):
            add_back(k, swiglu(gather(k)))
            return carry

        lax.fori_loop(1, n_tiles, extra, 0)


def _moe(h, gates, w1, w3, w2):
    n = h.shape[0]
    once = pl.Buffered(1)
    return pl.pallas_call(
        _moe_kernel,
        grid=(n // MOE_ROWS, N_EXPERTS, D_FF // FF_TILE),
        in_specs=[pl.BlockSpec((MOE_ROWS, D_MODEL), lambda i, e, j: (i, 0), pipeline_mode=once),
                  pl.BlockSpec((MOE_ROWS, LANES), lambda i, e, j: (i, 0), pipeline_mode=once),
                  pl.BlockSpec((1, D_MODEL, FF_TILE), lambda i, e, j: (e, 0, j)),
                  pl.BlockSpec((1, D_MODEL, FF_TILE), lambda i, e, j: (e, 0, j)),
                  pl.BlockSpec((1, FF_TILE, D_MODEL), lambda i, e, j: (e, j, 0))],
        out_specs=pl.BlockSpec((MOE_ROWS, D_MODEL), lambda i, e, j: (i, 0), pipeline_mode=once),
        out_shape=jax.ShapeDtypeStruct((n, D_MODEL), F32),
        scratch_shapes=[pltpu.VMEM((MOE_NSUB, MOE_SUB, LANES), jnp.int32),
                        pltpu.VMEM((MOE_NSUB, LANES, MOE_SUB), jnp.int32),
                        pltpu.VMEM((N_EXPERTS, MOE_NSUB * MOE_TILE, D_MODEL), BF16),
                        pltpu.VMEM((MOE_NSUB * MOE_TILE, D_MODEL), F32)],
        compiler_params=_params(("parallel", "arbitrary", "arbitrary"), MOE_VMEM_LIMIT),
        name="moe",
    )(h, gates, w1, w3, w2)


def _residual_ln_kernel(x_ref, f_ref, gate_ref, lg_ref, lb_ref, o_ref):
    o_ref[...] = _layernorm_rows(ALPHA * x_ref[...] + gate_ref[0] * f_ref[...], lg_ref[...], lb_ref[...])


def _residual_ln(x, f, gate, lg, lb, rows_per_mod):
    n = x.shape[0]
    tpb = rows_per_mod // ROW_TILE
    row = pl.BlockSpec((ROW_TILE, D_MODEL), lambda i: (i, 0))
    vec = pl.BlockSpec((1, D_MODEL), lambda i: (0, 0))
    return pl.pallas_call(
        _residual_ln_kernel,
        grid=(n // ROW_TILE,),
        in_specs=[row, row, pl.BlockSpec((1, 1, D_MODEL), lambda i: (i // tpb, 0, 0)), vec, vec],
        out_specs=row,
        out_shape=jax.ShapeDtypeStruct((n, D_MODEL), F32),
        compiler_params=_params(("parallel",)),
        name="residual_ln",
    )(x, f, gate, lg.reshape(1, D_MODEL), lb.reshape(1, D_MODEL))


def _permute_w_in(w):
    pad = jnp.zeros((D_MODEL, U_COLS - ORIG_END), w.dtype)
    return jnp.concatenate([w[:, :ORIG_GATES], w[:, ORIG_SX:ORIG_DT], w[:, ORIG_SZ:ORIG_SX],
                            w[:, ORIG_GATES:ORIG_SZ], w[:, ORIG_DT:ORIG_END], pad], axis=1).astype(BF16)


def _small_row(vals, offset):
    v = vals.reshape(-1).astype(F32)
    return jnp.zeros((1, LANES), F32).at[0, offset:offset + v.shape[0]].set(v)


def _pack_mlstm_state(c, n, m):
    eye = jnp.eye(N_HEADS, dtype=F32)
    shape = c.shape[:2] + (HEAD_LANES, HEAD_LANES)
    cbd = jnp.einsum('bdhke,hg->bdhkge', c, eye).reshape(shape)
    nbd = jnp.einsum('bdhk,hg,e->bdhkge', n, eye, jnp.ones((D_STATE,), F32)).reshape(shape)
    return jnp.concatenate([cbd, nbd], axis=-1), jnp.repeat(m, D_STATE, axis=-1)[:, :, None, :]


def _pack_ssd_state(s):
    sel = (jnp.arange(N_GROUPS)[:, None] == jnp.arange(N_HEADS)[None, :] // 2).astype(F32)
    return jnp.einsum('bdhpn,gh->bdgnhp', s, sel).reshape(s.shape[:2] + (LANES, HEAD_LANES))


def _layer(x, mods, P, l, bsz, t, ctx):
    sh1, sc1, g1, sh2, sc2, g2 = mods
    rows_per_mod = x.shape[0] // sh1.shape[0]
    lam_init = 0.8 - 0.6 * math.exp(-0.3 * l)
    u = _in_proj(x, sc1, sh1, P['w_in'][l], rows_per_mod)

    gate_bias = (_small_row(P['mlstm_gate_b'][l, 0], GATE_I) + _small_row(P['mlstm_gate_b'][l, 1], GATE_F))
    dt_bias = _small_row(P['ssm_dt_bias'][l], GATE_DT)
    alog = _small_row(P['ssm_A_log'][l], GATE_DT)
    dskip = jnp.repeat(P['ssm_D'][l].astype(F32), D_STATE).reshape(1, D_REC)

    if ctx is None:
        att, k_new, v_new = _attention_ctx(u, P['attn_lambda'][l], P['attn_norm_w'][l], lam_init, bsz, t)
        m_init = s_init = None
    else:
        ck, cv, c_c, c_n, c_m, c_s = ctx
        q, k, v = _rope_prep(u, bsz, t)
        k_all = jnp.concatenate([k, ck.astype(BF16)], axis=3)
        v_all = jnp.concatenate([v, cv.astype(BF16)], axis=2)
        att = _attention_lat(q, k_all, v_all, P['attn_lambda'][l], P['attn_norm_w'][l], lam_init, bsz, t)
        m_init = _pack_mlstm_state(c_c, c_n, c_m)
        s_init = _pack_ssd_state(c_s)
    hf, hb, c_out, n_out, m_out = _mlstm(u, gate_bias, bsz, t, m_init)
    xbc = _ssd_conv(u, P['conv_w'][l], P['conv_b'][l], bsz, t)
    yf, yb, s_out = _ssd(xbc, u, dt_bias, alog, dskip, bsz, t, s_init)

    x = _out_proj(att, hf, hb, yf, yb, u, P['w_out'][l], x, g1, P['mlstm_norm_w'][l], P['ssm_norm_w'][l],
                  P['ln_g'][l, 0], P['ln_b'][l, 0], rows_per_mod)
    if l % 2 == 0:
        x = _ffn(x, sc2, sh2, g2, P['ffn_w1'][l // 2], P['ffn_w3'][l // 2], P['ffn_w2'][l // 2],
                 P['ln_g'][l, 1], P['ln_b'][l, 1], rows_per_mod)
    else:
        gates, h2 = _router(x, sc2, sh2, P['router_w'][l // 2], rows_per_mod)
        f = _moe(h2, gates, P['moe_w1'][l // 2], P['moe_w3'][l // 2], P['moe_w2'][l // 2])
        x = _residual_ln(x, f, g2, P['ln_g'][l, 1], P['ln_b'][l, 1], rows_per_mod)
    if ctx is None:
        return x, (k_new, v_new, c_out, n_out[..., 0], m_out[:, :, 0, ::D_STATE], jnp.swapaxes(s_out, -1, -2))
    return x, None


def kernel(x_prompt, x_sample, c, cache_attn_k, cache_attn_v, state_mlstm_C, state_mlstm_n, state_mlstm_m, state_ssm, c_ctx, w_ada, b_ada, w_in, w_out, attn_lambda, attn_norm_w, mlstm_gate_b, mlstm_norm_w, conv_w, conv_b, ssm_A_log, ssm_dt_bias, ssm_D, ssm_norm_w, ln_g, ln_b, ffn_w1, ffn_w3, ffn_w2, router_w, moe_w1, moe_w3, moe_w2):
    bsz, seq, _ = x_prompt.shape
    dbsz, dseq, _ = x_sample.shape
    P = dict(w_in=[_permute_w_in(w_in[l]) for l in range(DEPTH)], w_out=w_out.astype(BF16),
             attn_lambda=attn_lambda, attn_norm_w=attn_norm_w, mlstm_gate_b=mlstm_gate_b,
             mlstm_norm_w=mlstm_norm_w, conv_w=conv_w, conv_b=conv_b, ssm_A_log=ssm_A_log,
             ssm_dt_bias=ssm_dt_bias, ssm_D=ssm_D, ssm_norm_w=ssm_norm_w, ln_g=ln_g, ln_b=ln_b,
             ffn_w1=ffn_w1.astype(BF16), ffn_w3=ffn_w3.astype(BF16), ffn_w2=ffn_w2.astype(BF16),
             router_w=router_w, moe_w1=moe_w1.astype(BF16), moe_w3=moe_w3.astype(BF16),
             moe_w2=moe_w2.astype(BF16))

    cvec = jnp.zeros((8, D_MODEL), F32).at[0].set(c_ctx).at[1:1 + dbsz].set(c)
    mod = _modulation(cvec, w_ada, b_ada)

    def mods_for(l, lo, hi):
        return [mod[l, lo:hi, i * D_MODEL:(i + 1) * D_MODEL][:, None, :] for i in range(6)]

    y_prompt = x_prompt.reshape(bsz * seq, D_MODEL)
    outs = []
    for l in range(DEPTH):
        y_prompt, ctx_out = _layer(y_prompt, mods_for(l, 0, 1), P, l, bsz, seq, None)
        outs.append(ctx_out)

    y_sample = x_sample.reshape(dbsz * dseq, D_MODEL)
    for l in range(DEPTH):
        ctx = (cache_attn_k[:, l], cache_attn_v[:, l], state_mlstm_C[:, l], state_mlstm_n[:, l],
               state_mlstm_m[:, l], state_ssm[:, l])
        y_sample, _ = _layer(y_sample, mods_for(l, 1, 1 + dbsz), P, l, dbsz, dseq, ctx)

    stacked = [jnp.stack([o[i] for o in outs], axis=1) for i in range(6)]
    return (y_prompt.reshape(bsz, seq, D_MODEL), y_sample.reshape(dbsz, dseq, D_MODEL), *stacked)
```

```python
import functools
import math

import jax
import jax.numpy as jnp
from jax import lax
from jax.experimental import pallas as pl
from jax.experimental.pallas import tpu as pltpu

F32 = jnp.float32
BF16 = jnp.bfloat16
HIGHEST = lax.Precision.HIGHEST

D_MODEL = 1024
DEPTH = 2
GRID_W = 64
N_HEADS = 4
D_ATT = 512
D_HEAD_V = 128
D_QK = 64
D_REC = 256
D_STATE = 64
N_GROUPS = 2
D_CONV = 3
D_FF = 2816
N_EXPERTS = 8
ALPHA = (2.0 * DEPTH) ** 0.25
CHUNK = 64
ROPE_BASE = 10000.0
LOG2E = 1.4426950408889634
EPS = 1e-5

COL_AQ, COL_AK, COL_AV = 0, 512, 1024
COL_MQ, COL_MK, COL_MV, COL_MO = 1536, 1792, 2048, 2304
COL_SX, COL_SBC, COL_SZ = 2560, 2816, 3072
COL_SMALL = 3328
U_COLS = 3584
ORIG_GATES, ORIG_SZ, ORIG_SX, ORIG_DT, ORIG_END = 2560, 2576, 2832, 3344, 3352
GATE_I, GATE_F, GATE_DT = 0, 8, 16

LANES = 128
ROW_TILE = 512
SCAN_ROWS = 256
VMEM_LIMIT = 48 * 1024 * 1024

NT_DIMS = (((1,), (1,)), ((), ()))
TN_DIMS = (((0,), (0,)), ((), ()))


def _params(sem, vmem=VMEM_LIMIT):
    return pltpu.CompilerParams(dimension_semantics=sem, vmem_limit_bytes=vmem)


def _silu(x):
    return x * jax.nn.sigmoid(x)


def _bdot(a, b):
    return jnp.dot(a.astype(BF16), b.astype(BF16), preferred_element_type=F32)


def _bdot_nt(a, b):
    return lax.dot_general(a.astype(BF16), b.astype(BF16), NT_DIMS, preferred_element_type=F32)


def _bdot_tn(a, b):
    return lax.dot_general(a.astype(BF16), b.astype(BF16), TN_DIMS, preferred_element_type=F32)


def _layernorm_rows(y, g, b):
    mu = jnp.mean(y, -1, keepdims=True)
    d = y - mu
    var = jnp.mean(d * d, -1, keepdims=True)
    return d * lax.rsqrt(var + EPS) * g + b


def _mod_kernel(c_ref, w_ref, b_ref, o_ref):
    o_ref[0] = jnp.dot(_silu(c_ref[...]), w_ref[0], precision=HIGHEST,
                       preferred_element_type=F32) + b_ref[0]


def _modulation(cvec, w_ada, b_ada):
    tn = 1536
    return pl.pallas_call(
        _mod_kernel,
        grid=(DEPTH, 6 * D_MODEL // tn),
        in_specs=[pl.BlockSpec((8, D_MODEL), lambda l, j: (0, 0)),
                  pl.BlockSpec((1, D_MODEL, tn), lambda l, j: (l, 0, j)),
                  pl.BlockSpec((1, 1, tn), lambda l, j: (l, 0, j))],
        out_specs=pl.BlockSpec((1, 8, tn), lambda l, j: (l, 0, j)),
        out_shape=jax.ShapeDtypeStruct((DEPTH, 8, 6 * D_MODEL), F32),
        compiler_params=_params(("parallel", "parallel")),
        name="modulation",
    )(cvec, w_ada, b_ada.reshape(DEPTH, 1, 6 * D_MODEL))


def _inproj_kernel(x_ref, sc_ref, sh_ref, w_ref, o_ref):
    h = (x_ref[...] * (1.0 + sc_ref[0]) + sh_ref[0]).astype(BF16)
    for n0 in range(0, U_COLS, 512):
        o_ref[:, n0:n0 + 512] = jnp.dot(h, w_ref[:, n0:n0 + 512], preferred_element_type=F32)


def _in_proj(x, sc, sh, w, rows_per_mod):
    n = x.shape[0]
    tpb = rows_per_mod // ROW_TILE
    return pl.pallas_call(
        _inproj_kernel,
        grid=(n // ROW_TILE,),
        in_specs=[pl.BlockSpec((ROW_TILE, D_MODEL), lambda i: (i, 0)),
                  pl.BlockSpec((1, 1, D_MODEL), lambda i: (i // tpb, 0, 0)),
                  pl.BlockSpec((1, 1, D_MODEL), lambda i: (i // tpb, 0, 0)),
                  pl.BlockSpec((D_MODEL, U_COLS), lambda i: (0, 0))],
        out_specs=pl.BlockSpec((ROW_TILE, U_COLS), lambda i: (i, 0)),
        out_shape=jax.ShapeDtypeStruct((n, U_COLS), F32),
        compiler_params=_params(("parallel",)),
        name="in_proj",
    )(x, sc, sh, w)


def _lambda_scalar(lam_ref, lam_init):
    lp = lam_ref[...]
    s01 = jnp.sum(lp[0:1] * lp[1:2], axis=-1, keepdims=True)
    s23 = jnp.sum(lp[2:3] * lp[3:4], axis=-1, keepdims=True)
    return jnp.exp(s01) - jnp.exp(s23) + lam_init


def _softmax_rows(s):
    e = jnp.exp(s - jnp.max(s, -1, keepdims=True))
    return e / jnp.sum(e, -1, keepdims=True)


def _head_norm(o, nw, lam_init):
    return o * lax.rsqrt(jnp.mean(o * o, -1, keepdims=True) + EPS) * nw * (1.0 - lam_init)


def _attn_ctx_kernel(u_ref, lam_ref, nw_ref, att_ref, k_ref, v_ref, *, lam_init):
    lam = _lambda_scalar(lam_ref, lam_init)
    for h in range(N_HEADS):
        v = u_ref[:, COL_AV + h * D_HEAD_V:COL_AV + (h + 1) * D_HEAD_V]
        v_ref[0, h] = v
        ps = []
        for m in range(2):
            c0 = h * D_HEAD_V + m * D_QK
            q = u_ref[:, COL_AQ + c0:COL_AQ + c0 + D_QK] * (D_QK ** -0.5)
            k = u_ref[:, COL_AK + c0:COL_AK + c0 + D_QK]
            k_ref[0, h, m] = k
            ps.append(_softmax_rows(_bdot_nt(q, k)))
        o = _bdot(ps[0] - lam * ps[1], v)
        att_ref[:, h * D_HEAD_V:(h + 1) * D_HEAD_V] = _head_norm(o, nw_ref[...], lam_init).astype(BF16)


def _attention_ctx(u, lam_p, norm_w, lam_init, bsz, t):
    n = bsz * t
    return pl.pallas_call(
        functools.partial(_attn_ctx_kernel, lam_init=lam_init),
        grid=(bsz,),
        in_specs=[pl.BlockSpec((t, 3 * D_ATT), lambda b: (b, 0)),
                  pl.BlockSpec((4, D_QK), lambda b: (0, 0)),
                  pl.BlockSpec((1, D_HEAD_V), lambda b: (0, 0))],
        out_specs=[pl.BlockSpec((t, D_ATT), lambda b: (b, 0)),
                   pl.BlockSpec((1, N_HEADS, 2, t, D_QK), lambda b: (b, 0, 0, 0, 0)),
                   pl.BlockSpec((1, N_HEADS, t, D_HEAD_V), lambda b: (b, 0, 0, 0))],
        out_shape=[jax.ShapeDtypeStruct((n, D_ATT), BF16),
                   jax.ShapeDtypeStruct((bsz, N_HEADS, 2, t, D_QK), F32),
                   jax.ShapeDtypeStruct((bsz, N_HEADS, t, D_HEAD_V), F32)],
        compiler_params=_params(("parallel",)),
        name="attn_ctx",
    )(u, lam_p, norm_w.reshape(1, D_HEAD_V))


def _rope_kernel(u_ref, cos_ref, sa_ref, sb_ref, q_ref, k_ref, v_ref):
    cos, sa, sb = cos_ref[...], sa_ref[...], sb_ref[...]

    def rope(x):
        return x * cos + pltpu.roll(x, LANES - 16, 1) * sa + pltpu.roll(x, 16, 1) * sb

    for h in range(N_HEADS):
        q = rope(u_ref[:, COL_AQ + h * D_HEAD_V:COL_AQ + (h + 1) * D_HEAD_V]) * (LOG2E * D_QK ** -0.5)
        k = rope(u_ref[:, COL_AK + h * D_HEAD_V:COL_AK + (h + 1) * D_HEAD_V])
        for m in range(2):
            q_ref[0, h, m] = q[:, m * D_QK:(m + 1) * D_QK].astype(BF16)
            k_ref[0, h, m] = k[:, m * D_QK:(m + 1) * D_QK].astype(BF16)
        v_ref[0, h] = u_ref[:, COL_AV + h * D_HEAD_V:COL_AV + (h + 1) * D_HEAD_V].astype(BF16)


def _rope_tables(t):
    rows = jnp.repeat(jnp.arange(t // GRID_W, dtype=F32), GRID_W)
    cols = jnp.tile(jnp.arange(GRID_W, dtype=F32), t // GRID_W)
    half = D_QK // 2
    inv = ROPE_BASE ** (-jnp.arange(0, half, 2, dtype=F32) / half)
    ang_r = rows[:, None] * inv
    ang_c = cols[:, None] * inv
    ang = jnp.concatenate([ang_r, ang_r, ang_c, ang_c], -1)
    cos, sin = jnp.cos(ang), jnp.sin(ang)
    quarter = (jnp.arange(D_QK) // (D_QK // 4)) % 2
    sa = jnp.where(quarter == 0, -sin, 0.0)
    sb = jnp.where(quarter == 1, sin, 0.0)
    tile2 = lambda a: jnp.concatenate([a, a], -1)
    return tile2(cos), tile2(sa), tile2(sb)


def _rope_prep(u, bsz, t):
    tr = 512
    nb = t // tr
    cos, sa, sb = _rope_tables(t)
    tab = pl.BlockSpec((tr, LANES), lambda b, i: (i, 0))
    return pl.pallas_call(
        _rope_kernel,
        grid=(bsz, nb),
        in_specs=[pl.BlockSpec((tr, 3 * D_ATT), lambda b, i: (b * nb + i, 0)), tab, tab, tab],
        out_specs=[pl.BlockSpec((1, N_HEADS, 2, tr, D_QK), lambda b, i: (b, 0, 0, i, 0)),
                   pl.BlockSpec((1, N_HEADS, 2, tr, D_QK), lambda b, i: (b, 0, 0, i, 0)),
                   pl.BlockSpec((1, N_HEADS, tr, D_HEAD_V), lambda b, i: (b, 0, i, 0))],
        out_shape=[jax.ShapeDtypeStruct((bsz, N_HEADS, 2, t, D_QK), BF16),
                   jax.ShapeDtypeStruct((bsz, N_HEADS, 2, t, D_QK), BF16),
                   jax.ShapeDtypeStruct((bsz, N_HEADS, t, D_HEAD_V), BF16)],
        compiler_params=_params(("parallel", "parallel")),
        name="rope_prep",
    )(u, cos, sa, sb)


def _attn_lat_kernel(q_ref, k_ref, v_ref, lam_ref, nw_ref, o_ref, *, lam_init):
    lam = _lambda_scalar(lam_ref, lam_init)
    es, sums = [], []
    for m in range(2):
        s = lax.dot_general(q_ref[0, 0, m], k_ref[0, 0, m], NT_DIMS, preferred_element_type=F32)
        e = jnp.exp2(s - jnp.max(s, -1, keepdims=True))
        es.append(e)
        sums.append(jnp.sum(e, -1, keepdims=True))
    a = es[0] - (lam * sums[0] / sums[1]) * es[1]
    o = _bdot(a, v_ref[0, 0]) / sums[0]
    o_ref[...] = _head_norm(o, nw_ref[...], lam_init).astype(BF16)


def _attention_lat(q, k_all, v_all, lam_p, norm_w, lam_init, bsz, t):
    tq = 256
    nq = t // tq
    s = k_all.shape[3]
    return pl.pallas_call(
        functools.partial(_attn_lat_kernel, lam_init=lam_init),
        grid=(bsz, N_HEADS, nq),
        in_specs=[pl.BlockSpec((1, 1, 2, tq, D_QK), lambda b, h, i: (b, h, 0, i, 0)),
                  pl.BlockSpec((1, 1, 2, s, D_QK), lambda b, h, i: (b, h, 0, 0, 0)),
                  pl.BlockSpec((1, 1, s, D_HEAD_V), lambda b, h, i: (b, h, 0, 0)),
                  pl.BlockSpec((4, D_QK), lambda b, h, i: (0, 0)),
                  pl.BlockSpec((1, D_HEAD_V), lambda b, h, i: (0, 0))],
        out_specs=pl.BlockSpec((tq, D_HEAD_V), lambda b, h, i: (b * nq + i, h)),
        out_shape=jax.ShapeDtypeStruct((bsz * t, D_ATT), BF16),
        compiler_params=_params(("parallel", "parallel", "parallel")),
        name="attn_lat",
    )(q, k_all, v_all, lam_p, norm_w.reshape(1, D_HEAD_V))


HEAD_LANES = N_HEADS * D_STATE


def _scan_consts():
    t = lax.broadcasted_iota(jnp.int32, (CHUNK, HEAD_LANES), 0)
    s = lax.broadcasted_iota(jnp.int32, (CHUNK, HEAD_LANES), 1) & (CHUNK - 1)
    r = lax.broadcasted_iota(jnp.int32, (CHUNK, CHUNK), 0)
    c = lax.broadcasted_iota(jnp.int32, (CHUNK, CHUNK), 1)
    reads = (s <= t, s >= t)
    tri = (jnp.where(c <= r, 1.0, 0.0), jnp.where(c >= r, 1.0, 0.0))
    return reads, tri, s == t


def _group_mask(rows, cols, row_shift, col_shift):
    r = lax.broadcasted_iota(jnp.int32, (rows, cols), 0) >> 6
    c = (lax.broadcasted_iota(jnp.int32, (rows, cols), 1) >> 6) & (N_HEADS - 1)
    return (r >> row_shift) == (c >> col_shift)


def _spread(x, chans):
    return jnp.concatenate([jnp.broadcast_to(x[:, c:c + 1], (CHUNK, D_STATE)) for c in chans], axis=1)


def _stack_heads(x):
    return jnp.concatenate([x] * N_HEADS, axis=0)


def _mlstm_chunk(q4, k4, v4, g, d, cn_prev, m_prev, reads, tri, eye, bd):
    li = _spread(g, [GATE_I + d * N_HEADS + h for h in range(N_HEADS)])
    lf = _spread(jax.nn.log_sigmoid(g), [GATE_F + d * N_HEADS + h for h in range(N_HEADS)])
    bc = jnp.dot(tri[d], lf, precision=HIGHEST, preferred_element_type=F32)
    btot = jnp.sum(lf, 0, keepdims=True)
    b_row = jnp.sum(jnp.where(reads[1 - d], lf, 0.0), 0, keepdims=True)
    li_row = jnp.sum(jnp.where(eye, li, 0.0), 0, keepdims=True)
    dm = jnp.where(reads[d], bc - b_row + li_row, -jnp.inf)
    rmax = jnp.concatenate(
        [jnp.broadcast_to(jnp.max(dm[:, h * D_STATE:(h + 1) * D_STATE], -1, keepdims=True), (CHUNK, D_STATE))
         for h in range(N_HEADS)], axis=1)
    inter = bc + m_prev
    m_t = jnp.maximum(inter, rmax)
    w_inter = jnp.exp(inter - m_t)
    qs = (q4 * (D_STATE ** -0.5)).astype(BF16)
    kbd = jnp.where(bd[:, :HEAD_LANES], _stack_heads(k4.astype(BF16)), 0)
    s4 = lax.dot_general(qs, kbd, NT_DIMS, preferred_element_type=F32) * jnp.exp(dm - m_t)
    vo = jnp.concatenate([v4.astype(BF16), jnp.ones((CHUNK, HEAD_LANES), BF16)], axis=1)
    vbd = jnp.where(bd, _stack_heads(vo), 0)
    nd = (jnp.concatenate([w_inter, w_inter], axis=1)
          * jnp.dot(qs, cn_prev.astype(BF16), preferred_element_type=F32)
          + jnp.dot(s4.astype(BF16), vbd, preferred_element_type=F32))
    hc = nd[:, :HEAD_LANES] / jnp.maximum(jnp.abs(nd[:, HEAD_LANES:]), jnp.exp(-m_t))
    gcol = btot - bc + li
    m_new = jnp.maximum(btot + m_prev, jnp.max(gcol, 0, keepdims=True))
    w_c = jnp.exp(btot + m_prev - m_new)
    kw = (k4 * jnp.exp(gcol - m_new)).astype(BF16)
    dcn = lax.dot_general(kw, vo, TN_DIMS, preferred_element_type=F32)
    cn_new = jnp.concatenate([w_c, w_c], axis=1) * cn_prev + jnp.where(bd, dcn, 0.0)
    return hc, cn_new, m_new


def _mlstm_kernel(*refs, nblk, rows, zero_init):
    if zero_init:
        (qf_ref, kf_ref, vf_ref, gf_ref, qb_ref, kb_ref, vb_ref, gb_ref, bias_ref,
         hf_ref, hb_ref, cout_ref, nout_ref, mout_ref, cn_scr, m_scr) = refs
    else:
        (qf_ref, kf_ref, vf_ref, gf_ref, qb_ref, kb_ref, vb_ref, gb_ref, bias_ref, c0_ref, m0_ref,
         hf_ref, hb_ref, cout_ref, nout_ref, mout_ref, cn_scr, m_scr) = refs
    j = pl.program_id(1)
    nchunk = rows // CHUNK

    @pl.when(j == 0)
    def _():
        if zero_init:
            cn_scr[...] = jnp.zeros_like(cn_scr)
            m_scr[...] = jnp.zeros_like(m_scr)
        else:
            cn_scr[...] = c0_ref[0]
            m_scr[...] = m0_ref[0]

    reads, tri, eye = _scan_consts()
    bd = _group_mask(HEAD_LANES, 2 * HEAD_LANES, 0, 0)

    def chunk(ci, carry):
        for d in range(2):
            q_ref, k_ref, v_ref, g_ref, h_ref = ((qf_ref, kf_ref, vf_ref, gf_ref, hf_ref) if d == 0
                                                 else (qb_ref, kb_ref, vb_ref, gb_ref, hb_ref))
            cj = ci if d == 0 else nchunk - 1 - ci
            rs = pl.ds(pl.multiple_of(cj * CHUNK, CHUNK), CHUNK)
            g = g_ref[rs, :] + bias_ref[...]
            hc, cn_new, m_new = _mlstm_chunk(q_ref[rs, :], k_ref[rs, :], v_ref[rs, :], g, d,
                                             cn_scr[d], m_scr[d], reads, tri, eye, bd)
            h_ref[rs, :] = hc
            cn_scr[d] = cn_new
            m_scr[d] = m_new
        return carry

    lax.fori_loop(0, nchunk, chunk, 0, unroll=True)

    @pl.when(j == nblk - 1)
    def _():
        for d in range(2):
            for h in range(N_HEADS):
                r0, r1 = h * D_STATE, (h + 1) * D_STATE
                cout_ref[0, d, h] = cn_scr[d, r0:r1, r0:r1]
                nout_ref[0, d, h] = cn_scr[d, r0:r1, HEAD_LANES + r0:HEAD_LANES + r1]
        mout_ref[0] = m_scr[...]


def _mlstm(u, gate_bias, bsz, t, init):
    rows = min(t, SCAN_ROWS)
    nblk = t // rows
    zero_init = init is None

    def fwd(col):
        return lambda b, j: (b * nblk + j, col)

    def bwd(col):
        return lambda b, j: (b * nblk + nblk - 1 - j, col)

    cq, ck, cv, cg = COL_MQ // D_REC, COL_MK // D_REC, COL_MV // D_REC, COL_SMALL // LANES
    in_specs = []
    for mk in (fwd, bwd):
        in_specs += [pl.BlockSpec((rows, D_REC), mk(cq)), pl.BlockSpec((rows, D_REC), mk(ck)),
                     pl.BlockSpec((rows, D_REC), mk(cv)), pl.BlockSpec((rows, LANES), mk(cg))]
    in_specs.append(pl.BlockSpec((1, LANES), lambda b, j: (0, 0)))
    args = [u] * 8 + [gate_bias]
    state_c = pl.BlockSpec((1, 2, HEAD_LANES, 2 * HEAD_LANES), lambda b, j: (b, 0, 0, 0))
    state_m = pl.BlockSpec((1, 2, 1, HEAD_LANES), lambda b, j: (b, 0, 0, 0))
    head_blocks = pl.BlockSpec((1, 2, N_HEADS, D_STATE, D_STATE), lambda b, j: (b, 0, 0, 0, 0))
    if not zero_init:
        in_specs += [state_c, state_m]
        args += list(init)
    return pl.pallas_call(
        functools.partial(_mlstm_kernel, nblk=nblk, rows=rows, zero_init=zero_init),
        grid=(bsz, nblk),
        in_specs=in_specs,
        out_specs=[pl.BlockSpec((rows, D_REC), fwd(0)), pl.BlockSpec((rows, D_REC), bwd(0)),
                   head_blocks, head_blocks, state_m],
        out_shape=[jax.ShapeDtypeStruct((bsz * t, D_REC), F32), jax.ShapeDtypeStruct((bsz * t, D_REC), F32),
                   jax.ShapeDtypeStruct((bsz, 2, N_HEADS, D_STATE, D_STATE), F32),
                   jax.ShapeDtypeStruct((bsz, 2, N_HEADS, D_STATE, D_STATE), F32),
                   jax.ShapeDtypeStruct((bsz, 2, 1, HEAD_LANES), F32)],
        scratch_shapes=[pltpu.VMEM((2, HEAD_LANES, 2 * HEAD_LANES), F32),
                        pltpu.VMEM((2, 1, HEAD_LANES), F32)],
        compiler_params=_params(("parallel", "arbitrary")),
        name="mlstm_scan",
    )(*args)


def _conv_kernel(x_ref, prev_ref, next_ref, w_ref, b_ref, o_ref, *, nblk):
    i = pl.program_id(1)
    x = x_ref[...]
    rows = x.shape[0]
    r = lax.broadcasted_iota(jnp.int32, x.shape, 0)
    prev_row = prev_ref[7:8, :] * (i > 0).astype(F32)
    next_row = next_ref[0:1, :] * (i < nblk - 1).astype(F32)
    xm = jnp.where(r == 0, prev_row, pltpu.roll(x, 1, 0))
    xp = jnp.where(r == rows - 1, next_row, pltpu.roll(x, rows - 1, 0))
    w = w_ref[...]
    o_ref[...] = _silu(xm * w[0:1] + x * w[1:2] + xp * w[2:3] + b_ref[...])


def _ssd_conv(u, conv_w, conv_b, bsz, t):
    rows = min(t, 512)
    nblk = t // rows
    r8 = rows // 8
    width = 2 * D_REC
    c0 = COL_SX // width
    return pl.pallas_call(
        functools.partial(_conv_kernel, nblk=nblk),
        grid=(bsz, nblk),
        in_specs=[pl.BlockSpec((rows, width), lambda b, i: (b * nblk + i, c0)),
                  pl.BlockSpec((8, width), lambda b, i: (jnp.maximum((b * nblk + i) * r8 - 1, 0), c0)),
                  pl.BlockSpec((8, width), lambda b, i: (jnp.minimum((b * nblk + i + 1) * r8,
                                                                     bsz * nblk * r8 - 1), c0)),
                  pl.BlockSpec((D_CONV, width), lambda b, i: (0, 0)),
                  pl.BlockSpec((1, width), lambda b, i: (0, 0))],
        out_specs=pl.BlockSpec((rows, width), lambda b, i: (b * nblk + i, 0)),
        out_shape=jax.ShapeDtypeStruct((bsz * t, width), F32),
        compiler_params=_params(("parallel", "parallel")),
        name="ssd_conv",
    )(u, u, u, conv_w, conv_b.reshape(1, width))


def _ssd_chunk(x4, bcm, dt128, da128, d, sg_prev, reads, tri, b_sel, s_sel, bd):
    chans = [GATE_DT + d * N_HEADS + h for h in range(N_HEADS)]
    dt = _spread(dt128, chans)
    da = _spread(da128, chans)
    ac = jnp.dot(tri[d], da, precision=HIGHEST, preferred_element_type=F32)
    atot = jnp.sum(da, 0, keepdims=True)
    a_row = jnp.sum(jnp.where(reads[1 - d], da, 0.0), 0, keepdims=True)
    decay = jnp.exp(jnp.where(reads[d], ac - a_row, -jnp.inf))
    bmat = bcm[:, :LANES].astype(BF16)
    cmat = bcm[:, LANES:].astype(BF16)
    bbd = jnp.where(b_sel, _stack_heads(bmat), 0)
    g4 = lax.dot_general(cmat, bbd, NT_DIMS, preferred_element_type=F32)
    xbd = jnp.where(bd, _stack_heads((x4 * dt).astype(BF16)), 0)
    y = (jnp.dot((g4 * decay).astype(BF16), xbd, preferred_element_type=F32)
         + jnp.dot(cmat, sg_prev.astype(BF16), preferred_element_type=F32) * jnp.exp(ac))
    w = jnp.exp(atot - ac) * dt
    dsg = lax.dot_general(bmat, (x4 * w).astype(BF16), TN_DIMS, preferred_element_type=F32)
    sg_new = jnp.exp(atot) * sg_prev + jnp.where(s_sel, dsg, 0.0)
    return y, sg_new


def _ssd_kernel(*refs, nblk, rows, zero_init):
    if zero_init:
        (xf_ref, bcf_ref, gf_ref, xb_ref, bcb_ref, gb_ref, dtb_ref, alog_ref, dskip_ref,
         yf_ref, yb_ref, sout_ref, s_scr) = refs
    else:
        (xf_ref, bcf_ref, gf_ref, xb_ref, bcb_ref, gb_ref, dtb_ref, alog_ref, dskip_ref, s0_ref,
         yf_ref, yb_ref, sout_ref, s_scr) = refs
    j = pl.program_id(1)
    nchunk = rows // CHUNK

    @pl.when(j == 0)
    def _():
        if zero_init:
            s_scr[...] = jnp.zeros_like(s_scr)
        else:
            s_scr[...] = s0_ref[0]

    reads, tri, _ = _scan_consts()
    bd = _group_mask(HEAD_LANES, HEAD_LANES, 0, 0)
    b_sel = _group_mask(HEAD_LANES, LANES, 1, 0)
    s_sel = _group_mask(LANES, HEAD_LANES, 0, 1)
    a_coef = -jnp.exp(alog_ref[...])

    def chunk(ci, carry):
        for d in range(2):
            x_ref, bc_ref, g_ref, y_ref = ((xf_ref, bcf_ref, gf_ref, yf_ref) if d == 0
                                           else (xb_ref, bcb_ref, gb_ref, yb_ref))
            cj = ci if d == 0 else nchunk - 1 - ci
            rs = pl.ds(pl.multiple_of(cj * CHUNK, CHUNK), CHUNK)
            dt128 = jax.nn.softplus(g_ref[rs, :] + dtb_ref[...])
            x4 = x_ref[rs, :]
            y, sg_new = _ssd_chunk(x4, bc_ref[rs, :], dt128, dt128 * a_coef, d, s_scr[d],
                                   reads, tri, b_sel, s_sel, bd)
            if d == 0:
                y = y + dskip_ref[...] * x4
            y_ref[rs, :] = y
            s_scr[d] = sg_new
        return carry

    lax.fori_loop(0, nchunk, chunk, 0, unroll=True)

    @pl.when(j == nblk - 1)
    def _():
        for d in range(2):
            for h in range(N_HEADS):
                g0 = (h // 2) * D_STATE
                sout_ref[0, d, h] = s_scr[d, g0:g0 + D_STATE, h * D_STATE:(h + 1) * D_STATE]


def _ssd(xbc, u, dt_bias_row, alog_row, dskip_row, bsz, t, init):
    rows = min(t, SCAN_ROWS)
    nblk = t // rows
    zero_init = init is None

    def fwd(col):
        return lambda b, j: (b * nblk + j, col)

    def bwd(col):
        return lambda b, j: (b * nblk + nblk - 1 - j, col)

    in_specs = []
    for mk in (fwd, bwd):
        in_specs += [pl.BlockSpec((rows, D_REC), mk(0)), pl.BlockSpec((rows, D_REC), mk(1)),
                     pl.BlockSpec((rows, LANES), mk(COL_SMALL // LANES))]
    in_specs += [pl.BlockSpec((1, LANES), lambda b, j: (0, 0)),
                 pl.BlockSpec((1, LANES), lambda b, j: (0, 0)),
                 pl.BlockSpec((1, D_REC), lambda b, j: (0, 0))]
    args = [xbc, xbc, u, xbc, xbc, u, dt_bias_row, alog_row, dskip_row]
    state = pl.BlockSpec((1, 2, LANES, HEAD_LANES), lambda b, j: (b, 0, 0, 0))
    if not zero_init:
        in_specs.append(state)
        args.append(init)
    return pl.pallas_call(
        functools.partial(_ssd_kernel, nblk=nblk, rows=rows, zero_init=zero_init),
        grid=(bsz, nblk),
        in_specs=in_specs,
        out_specs=[pl.BlockSpec((rows, D_REC), fwd(0)), pl.BlockSpec((rows, D_REC), bwd(0)),
                   pl.BlockSpec((1, 2, N_HEADS, D_STATE, D_STATE), lambda b, j: (b, 0, 0, 0, 0))],
        out_shape=[jax.ShapeDtypeStruct((bsz * t, D_REC), F32), jax.ShapeDtypeStruct((bsz * t, D_REC), F32),
                   jax.ShapeDtypeStruct((bsz, 2, N_HEADS, D_STATE, D_STATE), F32)],
        scratch_shapes=[pltpu.VMEM((2, LANES, HEAD_LANES), F32)],
        compiler_params=_params(("parallel", "arbitrary")),
        name="ssd_scan",
    )(*args)


def _outproj_kernel(att_ref, hf_ref, hb_ref, mo_ref, yf_ref, yb_ref, z_ref, w_ref, x_ref, gate_ref,
                    mnw_ref, snw_ref, lg_ref, lb_ref, o_ref):
    hh = hf_ref[...] + hb_ref[...]
    parts = []
    for h in range(N_HEADS):
        xh = hh[:, h * D_STATE:(h + 1) * D_STATE]
        mu = jnp.mean(xh, -1, keepdims=True)
        dlt = xh - mu
        var = jnp.mean(dlt * dlt, -1, keepdims=True)
        parts.append(dlt * lax.rsqrt(var + EPS))
    ml = jax.nn.sigmoid(mo_ref[...]) * jnp.concatenate(parts, axis=1) * mnw_ref[...]
    yz = (yf_ref[...] + yb_ref[...]) * _silu(z_ref[...])
    parts = []
    for grp in range(N_GROUPS):
        yg = yz[:, grp * LANES:(grp + 1) * LANES]
        parts.append(yg * lax.rsqrt(jnp.mean(yg * yg, -1, keepdims=True) + EPS))
    ssm = jnp.concatenate(parts, axis=1) * snw_ref[...]
    mixed = (jnp.dot(att_ref[...], w_ref[0:D_ATT], preferred_element_type=F32)
             + _bdot(ml, w_ref[D_ATT:D_ATT + D_REC])
             + _bdot(ssm, w_ref[D_ATT + D_REC:D_MODEL]))
    y = ALPHA * x_ref[...] + gate_ref[0] * mixed
    o_ref[...] = _layernorm_rows(y, lg_ref[...], lb_ref[...])


def _out_proj(att, hf, hb, yf, yb, u, w, x, gate, mnw, snw, lg, lb, rows_per_mod):
    n = x.shape[0]
    tpb = rows_per_mod // ROW_TILE
    row = lambda width, col: pl.BlockSpec((ROW_TILE, width), lambda i: (i, col))
    vec = lambda width: pl.BlockSpec((1, width), lambda i: (0, 0))
    return pl.pallas_call(
        _outproj_kernel,
        grid=(n // ROW_TILE,),
        in_specs=[row(D_ATT, 0), row(D_REC, 0), row(D_REC, 0), row(D_REC, COL_MO // D_REC),
                  row(D_REC, 0), row(D_REC, 0), row(D_REC, COL_SZ // D_REC),
                  pl.BlockSpec((D_MODEL, D_MODEL), lambda i: (0, 0)),
                  row(D_MODEL, 0),
                  pl.BlockSpec((1, 1, D_MODEL), lambda i: (i // tpb, 0, 0)),
                  vec(D_REC), vec(D_REC), vec(D_MODEL), vec(D_MODEL)],
        out_specs=row(D_MODEL, 0),
        out_shape=jax.ShapeDtypeStruct((n, D_MODEL), F32),
        compiler_params=_params(("parallel",)),
        name="out_proj",
    )(att, hf, hb, u, yf, yb, u, w, x, gate, mnw.reshape(1, D_REC), snw.reshape(1, D_REC),
      lg.reshape(1, D_MODEL), lb.reshape(1, D_MODEL))


FF_TILE = D_FF // 2
FFN_ROWS = 1024
FFN_SUB = 512
FFN_VMEM_LIMIT = 56 * 1024 * 1024


def _swiglu_partial(h, w1, w3, w2):
    a = jnp.dot(h, w1, preferred_element_type=F32)
    b = jnp.dot(h, w3, preferred_element_type=F32)
    return jnp.dot((_silu(a) * b).astype(BF16), w2, preferred_element_type=F32)


def _ffn_kernel(x_ref, sc_ref, sh_ref, gate_ref, w1_ref, w3_ref, w2_ref, lg_ref, lb_ref, o_ref,
                h_scr, acc_scr):
    j = pl.program_id(1)

    @pl.when(j == 0)
    def _():
        h_scr[...] = (x_ref[...] * (1.0 + sc_ref[0]) + sh_ref[0]).astype(BF16)
        acc_scr[...] = jnp.zeros_like(acc_scr)

    for r0 in range(0, FFN_ROWS, FFN_SUB):
        rows = slice(r0, r0 + FFN_SUB)
        acc_scr[rows, :] += _swiglu_partial(h_scr[rows, :], w1_ref[...], w3_ref[...], w2_ref[...])

    @pl.when(j == pl.num_programs(1) - 1)
    def _():
        y = ALPHA * x_ref[...] + gate_ref[0] * acc_scr[...]
        o_ref[...] = _layernorm_rows(y, lg_ref[...], lb_ref[...])


def _ffn(x, sc, sh, gate, w1, w3, w2, lg, lb, rows_per_mod):
    n = x.shape[0]
    tpb = rows_per_mod // FFN_ROWS
    modspec = pl.BlockSpec((1, 1, D_MODEL), lambda i, j: (i // tpb, 0, 0))
    vec = pl.BlockSpec((1, D_MODEL), lambda i, j: (0, 0))
    return pl.pallas_call(
        _ffn_kernel,
        grid=(n // FFN_ROWS, D_FF // FF_TILE),
        in_specs=[pl.BlockSpec((FFN_ROWS, D_MODEL), lambda i, j: (i, 0)), modspec, modspec, modspec,
                  pl.BlockSpec((D_MODEL, FF_TILE), lambda i, j: (0, j)),
                  pl.BlockSpec((D_MODEL, FF_TILE), lambda i, j: (0, j)),
                  pl.BlockSpec((FF_TILE, D_MODEL), lambda i, j: (j, 0)), vec, vec],
        out_specs=pl.BlockSpec((FFN_ROWS, D_MODEL), lambda i, j: (i, 0)),
        out_shape=jax.ShapeDtypeStruct((n, D_MODEL), F32),
        scratch_shapes=[pltpu.VMEM((FFN_ROWS, D_MODEL), BF16), pltpu.VMEM((FFN_ROWS, D_MODEL), F32)],
        compiler_params=_params(("parallel", "arbitrary"), FFN_VMEM_LIMIT),
        name="ffn_dense",
    )(x, sc, sh, gate, w1, w3, w2, lg.reshape(1, D_MODEL), lb.reshape(1, D_MODEL))


def _router_kernel(x_ref, sc_ref, sh_ref, rw_ref, gates_ref, h_ref):
    h = x_ref[...] * (1.0 + sc_ref[0]) + sh_ref[0]
    h_ref[...] = h.astype(BF16)
    logits = jnp.dot(h, rw_ref[...], precision=HIGHEST, preferred_element_type=F32)
    lane = lax.broadcasted_iota(jnp.int32, logits.shape, 1)
    valid = lane < N_EXPERTS
    p = jnp.where(valid, _softmax_rows(jnp.where(valid, logits, -jnp.inf)), -2.0)
    p1 = jnp.max(p, -1, keepdims=True)
    i1 = jnp.min(jnp.where(p == p1, lane, LANES), -1, keepdims=True)
    rest = jnp.where(lane == i1, -1.0, p)
    p2 = jnp.max(rest, -1, keepdims=True)
    i2 = jnp.min(jnp.where(rest == p2, lane, LANES), -1, keepdims=True)
    tot = p1 + p2
    gates_ref[...] = jnp.where(lane == i1, p1 / tot, jnp.where(lane == i2, p2 / tot, 0.0))


def _router(x, sc, sh, router_w, rows_per_mod):
    n = x.shape[0]
    tpb = rows_per_mod // ROW_TILE
    modspec = pl.BlockSpec((1, 1, D_MODEL), lambda i: (i // tpb, 0, 0))
    rw = jnp.pad(router_w, ((0, 0), (0, LANES - N_EXPERTS)))
    return pl.pallas_call(
        _router_kernel,
        grid=(n // ROW_TILE,),
        in_specs=[pl.BlockSpec((ROW_TILE, D_MODEL), lambda i: (i, 0)), modspec, modspec,
                  pl.BlockSpec((D_MODEL, LANES), lambda i: (0, 0))],
        out_specs=[pl.BlockSpec((ROW_TILE, LANES), lambda i: (i, 0)),
                   pl.BlockSpec((ROW_TILE, D_MODEL), lambda i: (i, 0))],
        out_shape=[jax.ShapeDtypeStruct((n, LANES), F32), jax.ShapeDtypeStruct((n, D_MODEL), BF16)],
        compiler_params=_params(("parallel",)),
        name="router",
    )(x, sc, sh, rw)


MOE_ROWS = 1024
MOE_TILE = 128
MOE_MAX_TILES = MOE_ROWS // MOE_TILE
MOE_VMEM_LIMIT = 56 * 1024 * 1024


def _moe_kernel(h_ref, gates_ref, w1_ref, w3_ref, w2_ref, o_ref, slot_scr, slott_scr, hs_scr, ys_scr):
    e = pl.program_id(1)
    j = pl.program_id(2)
    last_j = pl.num_programs(2) - 1

    @pl.when(jnp.logical_and(e == 0, j == 0))
    def _():
        r = lax.broadcasted_iota(jnp.int32, (MOE_ROWS, MOE_ROWS), 0)
        c = lax.broadcasted_iota(jnp.int32, (MOE_ROWS, MOE_ROWS), 1)
        before = jnp.where(c < r, 1.0, 0.0).astype(BF16)
        mask = gates_ref[...] != 0.0
        rank = jnp.dot(before, jnp.where(mask, 1.0, 0.0).astype(BF16), preferred_element_type=F32)
        slot = jnp.where(mask, rank, -1.0).astype(jnp.int32)
        slot_scr[...] = slot
        slott_scr[...] = slot.T
        o_ref[...] = jnp.zeros_like(o_ref)

    lane = lax.broadcasted_iota(jnp.int32, (MOE_ROWS, LANES), 1)
    slot_col = jnp.max(jnp.where(lane == e, slot_scr[...], -1), -1, keepdims=True)
    n_tiles = (jnp.max(slot_col) + MOE_TILE) // MOE_TILE

    def tile(k, carry):
        @pl.when(j == 0)
        def _():
            slot_row = slott_scr[pl.ds(e, 1), :]
            rr = lax.broadcasted_iota(jnp.int32, (MOE_TILE, MOE_ROWS), 0) + k * MOE_TILE
            pick = jnp.where(rr == slot_row, 1.0, 0.0).astype(BF16)
            hs_scr[k] = jnp.dot(pick, h_ref[...], preferred_element_type=F32).astype(BF16)
            ys_scr[k] = jnp.zeros((MOE_TILE, D_MODEL), F32)

        ys_scr[k] += _swiglu_partial(hs_scr[k], w1_ref[0], w3_ref[0], w2_ref[0])

        @pl.when(j == last_j)
        def _():
            g_col = jnp.sum(jnp.where(lane == e, gates_ref[...], 0.0), -1, keepdims=True)
            cc = lax.broadcasted_iota(jnp.int32, (MOE_ROWS, 2 * MOE_TILE), 1)
            cc = jnp.where(cc >= MOE_TILE, cc - MOE_TILE, cc) + k * MOE_TILE
            put = jnp.where(slot_col == cc, 1.0, 0.0).astype(BF16)
            y = ys_scr[k]
            y_hi = y.astype(BF16)
            y_lo = (y - y_hi.astype(F32)).astype(BF16)
            back = jnp.dot(put, jnp.concatenate([y_hi, y_lo], axis=0), preferred_element_type=F32)
            o_ref[...] += g_col * back

        return carry

    lax.fori_loop(0, n_tiles, tile, 0)


def _moe(h, gates, w1, w3, w2):
    n = h.shape[0]
    return pl.pallas_call(
        _moe_kernel,
        grid=(n // MOE_ROWS, N_EXPERTS, D_FF // FF_TILE),
        in_specs=[pl.BlockSpec((MOE_ROWS, D_MODEL), lambda i, e, j: (i, 0)),
                  pl.BlockSpec((MOE_ROWS, LANES), lambda i, e, j: (i, 0)),
                  pl.BlockSpec((1, D_MODEL, FF_TILE), lambda i, e, j: (e, 0, j)),
                  pl.BlockSpec((1, D_MODEL, FF_TILE), lambda i, e, j: (e, 0, j)),
                  pl.BlockSpec((1, FF_TILE, D_MODEL), lambda i, e, j: (e, j, 0))],
        out_specs=pl.BlockSpec((MOE_ROWS, D_MODEL), lambda i, e, j: (i, 0)),
        out_shape=jax.ShapeDtypeStruct((n, D_MODEL), F32),
        scratch_shapes=[pltpu.VMEM((MOE_ROWS, LANES), jnp.int32), pltpu.VMEM((LANES, MOE_ROWS), jnp.int32),
                        pltpu.VMEM((MOE_MAX_TILES, MOE_TILE, D_MODEL), BF16),
                        pltpu.VMEM((MOE_MAX_TILES, MOE_TILE, D_MODEL), F32)],
        compiler_params=_params(("parallel", "arbitrary", "arbitrary"), MOE_VMEM_LIMIT),
        name="moe",
    )(h, gates, w1, w3, w2)


def _residual_ln_kernel(x_ref, f_ref, gate_ref, lg_ref, lb_ref, o_ref):
    o_ref[...] = _layernorm_rows(ALPHA * x_ref[...] + gate_ref[0] * f_ref[...], lg_ref[...], lb_ref[...])


def _residual_ln(x, f, gate, lg, lb, rows_per_mod):
    n = x.shape[0]
    tpb = rows_per_mod // ROW_TILE
    row = pl.BlockSpec((ROW_TILE, D_MODEL), lambda i: (i, 0))
    vec = pl.BlockSpec((1, D_MODEL), lambda i: (0, 0))
    return pl.pallas_call(
        _residual_ln_kernel,
        grid=(n // ROW_TILE,),
        in_specs=[row, row, pl.BlockSpec((1, 1, D_MODEL), lambda i: (i // tpb, 0, 0)), vec, vec],
        out_specs=row,
        out_shape=jax.ShapeDtypeStruct((n, D_MODEL), F32),
        compiler_params=_params(("parallel",)),
        name="residual_ln",
    )(x, f, gate, lg.reshape(1, D_MODEL), lb.reshape(1, D_MODEL))


def _permute_w_in(w):
    pad = jnp.zeros((D_MODEL, U_COLS - ORIG_END), w.dtype)
    return jnp.concatenate([w[:, :ORIG_GATES], w[:, ORIG_SX:ORIG_DT], w[:, ORIG_SZ:ORIG_SX],
                            w[:, ORIG_GATES:ORIG_SZ], w[:, ORIG_DT:ORIG_END], pad], axis=1).astype(BF16)


def _small_row(vals, offset):
    v = vals.reshape(-1).astype(F32)
    return jnp.zeros((1, LANES), F32).at[0, offset:offset + v.shape[0]].set(v)


def _pack_mlstm_state(c, n, m):
    eye = jnp.eye(N_HEADS, dtype=F32)
    shape = c.shape[:2] + (HEAD_LANES, HEAD_LANES)
    cbd = jnp.einsum('bdhke,hg->bdhkge', c, eye).reshape(shape)
    nbd = jnp.einsum('bdhk,hg,e->bdhkge', n, eye, jnp.ones((D_STATE,), F32)).reshape(shape)
    return jnp.concatenate([cbd, nbd], axis=-1), jnp.repeat(m, D_STATE, axis=-1)[:, :, None, :]


def _pack_ssd_state(s):
    sel = (jnp.arange(N_GROUPS)[:, None] == jnp.arange(N_HEADS)[None, :] // 2).astype(F32)
    return jnp.einsum('bdhpn,gh->bdgnhp', s, sel).reshape(s.shape[:2] + (LANES, HEAD_LANES))


def _layer(x, mods, P, l, bsz, t, ctx):
    sh1, sc1, g1, sh2, sc2, g2 = mods
    rows_per_mod = x.shape[0] // sh1.shape[0]
    lam_init = 0.8 - 0.6 * math.exp(-0.3 * l)
    u = _in_proj(x, sc1, sh1, P['w_in'][l], rows_per_mod)

    gate_bias = (_small_row(P['mlstm_gate_b'][l, 0], GATE_I) + _small_row(P['mlstm_gate_b'][l, 1], GATE_F))
    dt_bias = _small_row(P['ssm_dt_bias'][l], GATE_DT)
    alog = _small_row(P['ssm_A_log'][l], GATE_DT)
    dskip = jnp.repeat(P['ssm_D'][l].astype(F32), D_STATE).reshape(1, D_REC)

    if ctx is None:
        att, k_new, v_new = _attention_ctx(u, P['attn_lambda'][l], P['attn_norm_w'][l], lam_init, bsz, t)
        m_init = s_init = None
    else:
        ck, cv, c_c, c_n, c_m, c_s = ctx
        q, k, v = _rope_prep(u, bsz, t)
        k_all = jnp.concatenate([k, ck.astype(BF16)], axis=3)
        v_all = jnp.concatenate([v, cv.astype(BF16)], axis=2)
        att = _attention_lat(q, k_all, v_all, P['attn_lambda'][l], P['attn_norm_w'][l], lam_init, bsz, t)
        m_init = _pack_mlstm_state(c_c, c_n, c_m)
        s_init = _pack_ssd_state(c_s)
    hf, hb, c_out, n_out, m_out = _mlstm(u, gate_bias, bsz, t, m_init)
    xbc = _ssd_conv(u, P['conv_w'][l], P['conv_b'][l], bsz, t)
    yf, yb, s_out = _ssd(xbc, u, dt_bias, alog, dskip, bsz, t, s_init)

    x = _out_proj(att, hf, hb, yf, yb, u, P['w_out'][l], x, g1, P['mlstm_norm_w'][l], P['ssm_norm_w'][l],
                  P['ln_g'][l, 0], P['ln_b'][l, 0], rows_per_mod)
    if l % 2 == 0:
        x = _ffn(x, sc2, sh2, g2, P['ffn_w1'][l // 2], P['ffn_w3'][l // 2], P['ffn_w2'][l // 2],
                 P['ln_g'][l, 1], P['ln_b'][l, 1], rows_per_mod)
    else:
        gates, h2 = _router(x, sc2, sh2, P['router_w'][l // 2], rows_per_mod)
        f = _moe(h2, gates, P['moe_w1'][l // 2], P['moe_w3'][l // 2], P['moe_w2'][l // 2])
        x = _residual_ln(x, f, g2, P['ln_g'][l, 1], P['ln_b'][l, 1], rows_per_mod)
    if ctx is None:
        return x, (k_new, v_new, c_out, n_out[..., 0], m_out[:, :, 0, ::D_STATE], jnp.swapaxes(s_out, -1, -2))
    return x, None


def kernel(x_prompt, x_sample, c, cache_attn_k, cache_attn_v, state_mlstm_C, state_mlstm_n, state_mlstm_m, state_ssm, c_ctx, w_ada, b_ada, w_in, w_out, attn_lambda, attn_norm_w, mlstm_gate_b, mlstm_norm_w, conv_w, conv_b, ssm_A_log, ssm_dt_bias, ssm_D, ssm_norm_w, ln_g, ln_b, ffn_w1, ffn_w3, ffn_w2, router_w, moe_w1, moe_w3, moe_w2):
    bsz, seq, _ = x_prompt.shape
    dbsz, dseq, _ = x_sample.shape
    P = dict(w_in=[_permute_w_in(w_in[l]) for l in range(DEPTH)], w_out=w_out.astype(BF16),
             attn_lambda=attn_lambda, attn_norm_w=attn_norm_w, mlstm_gate_b=mlstm_gate_b,
             mlstm_norm_w=mlstm_norm_w, conv_w=conv_w, conv_b=conv_b, ssm_A_log=ssm_A_log,
             ssm_dt_bias=ssm_dt_bias, ssm_D=ssm_D, ssm_norm_w=ssm_norm_w, ln_g=ln_g, ln_b=ln_b,
             ffn_w1=ffn_w1.astype(BF16), ffn_w3=ffn_w3.astype(BF16), ffn_w2=ffn_w2.astype(BF16),
             router_w=router_w, moe_w1=moe_w1.astype(BF16), moe_w3=moe_w3.astype(BF16),
             moe_w2=moe_w2.astype(BF16))

    cvec = jnp.zeros((8, D_MODEL), F32).at[0].set(c_ctx).at[1:1 + dbsz].set(c)
    mod = _modulation(cvec, w_ada, b_ada)

    def mods_for(l, lo, hi):
        return [mod[l, lo:hi, i * D_MODEL:(i + 1) * D_MODEL][:, None, :] for i in range(6)]

    y_prompt = x_prompt.reshape(bsz * seq, D_MODEL)
    outs = []
    for l in range(DEPTH):
        y_prompt, ctx_out = _layer(y_prompt, mods_for(l, 0, 1), P, l, bsz, seq, None)
        outs.append(ctx_out)

    y_sample = x_sample.reshape(dbsz * dseq, D_MODEL)
    for l in range(DEPTH):
        ctx = (cache_attn_k[:, l], cache_attn_v[:, l], state_mlstm_C[:, l], state_mlstm_n[:, l],
               state_mlstm_m[:, l], state_ssm[:, l])
        y_sample, _ = _layer(y_sample, mods_for(l, 1, 1 + dbsz), P, l, dbsz, dseq, ctx)

    stacked = [jnp.stack([o[i] for o in outs], axis=1) for i in range(6)]
    return (y_prompt.reshape(bsz, seq, D_MODEL), y_sample.reshape(dbsz, dseq, D_MODEL), *stacked)
```

```python
import functools
import math

import jax
import jax.numpy as jnp
from jax import lax
from jax.experimental import pallas as pl
from jax.experimental.pallas import tpu as pltpu

F32 = jnp.float32
BF16 = jnp.bfloat16
HIGHEST = lax.Precision.HIGHEST

D_MODEL = 1024
DEPTH = 2
GRID_W = 64
N_HEADS = 4
D_ATT = 512
D_HEAD_V = 128
D_QK = 64
D_REC = 256
D_STATE = 64
N_GROUPS = 2
D_CONV = 3
D_FF = 2816
N_EXPERTS = 8
ALPHA = (2.0 * DEPTH) ** 0.25
CHUNK = 64
ROPE_BASE = 10000.0
LOG2E = 1.4426950408889634
EPS = 1e-5

COL_AQ, COL_AK, COL_AV = 0, 512, 1024
COL_MQ, COL_MK, COL_MV, COL_MO = 1536, 1792, 2048, 2304
COL_SX, COL_SBC, COL_SZ = 2560, 2816, 3072
COL_SMALL = 3328
U_COLS = 3584
ORIG_GATES, ORIG_SZ, ORIG_SX, ORIG_DT, ORIG_END = 2560, 2576, 2832, 3344, 3352
GATE_I, GATE_F, GATE_DT = 0, 8, 16

LANES = 128
ROW_TILE = 512
SCAN_ROWS = 256
VMEM_LIMIT = 48 * 1024 * 1024

NT_DIMS = (((1,), (1,)), ((), ()))
TN_DIMS = (((0,), (0,)), ((), ()))


def _params(sem, vmem=VMEM_LIMIT):
    return pltpu.CompilerParams(dimension_semantics=sem, vmem_limit_bytes=vmem)


def _silu(x):
    return x * jax.nn.sigmoid(x)


def _bdot(a, b):
    return jnp.dot(a.astype(BF16), b.astype(BF16), preferred_element_type=F32)


def _bdot_nt(a, b):
    return lax.dot_general(a.astype(BF16), b.astype(BF16), NT_DIMS, preferred_element_type=F32)


def _bdot_tn(a, b):
    return lax.dot_general(a.astype(BF16), b.astype(BF16), TN_DIMS, preferred_element_type=F32)


def _layernorm_rows(y, g, b):
    mu = jnp.mean(y, -1, keepdims=True)
    d = y - mu
    var = jnp.mean(d * d, -1, keepdims=True)
    return d * lax.rsqrt(var + EPS) * g + b


def _mod_kernel(c_ref, w_ref, b_ref, o_ref):
    o_ref[0] = jnp.dot(_silu(c_ref[...]), w_ref[0], precision=HIGHEST,
                       preferred_element_type=F32) + b_ref[0]


def _modulation(cvec, w_ada, b_ada):
    tn = 1536
    return pl.pallas_call(
        _mod_kernel,
        grid=(DEPTH, 6 * D_MODEL // tn),
        in_specs=[pl.BlockSpec((8, D_MODEL), lambda l, j: (0, 0)),
                  pl.BlockSpec((1, D_MODEL, tn), lambda l, j: (l, 0, j)),
                  pl.BlockSpec((1, 1, tn), lambda l, j: (l, 0, j))],
        out_specs=pl.BlockSpec((1, 8, tn), lambda l, j: (l, 0, j)),
        out_shape=jax.ShapeDtypeStruct((DEPTH, 8, 6 * D_MODEL), F32),
        compiler_params=_params(("parallel", "parallel")),
        name="modulation",
    )(cvec, w_ada, b_ada.reshape(DEPTH, 1, 6 * D_MODEL))


def _inproj_kernel(x_ref, sc_ref, sh_ref, w_ref, o_ref):
    h = (x_ref[...] * (1.0 + sc_ref[0]) + sh_ref[0]).astype(BF16)
    for n0 in range(0, U_COLS, 512):
        o_ref[:, n0:n0 + 512] = jnp.dot(h, w_ref[:, n0:n0 + 512], preferred_element_type=F32)


def _in_proj(x, sc, sh, w, rows_per_mod):
    n = x.shape[0]
    tpb = rows_per_mod // ROW_TILE
    return pl.pallas_call(
        _inproj_kernel,
        grid=(n // ROW_TILE,),
        in_specs=[pl.BlockSpec((ROW_TILE, D_MODEL), lambda i: (i, 0)),
                  pl.BlockSpec((1, 1, D_MODEL), lambda i: (i // tpb, 0, 0)),
                  pl.BlockSpec((1, 1, D_MODEL), lambda i: (i // tpb, 0, 0)),
                  pl.BlockSpec((D_MODEL, U_COLS), lambda i: (0, 0))],
        out_specs=pl.BlockSpec((ROW_TILE, U_COLS), lambda i: (i, 0)),
        out_shape=jax.ShapeDtypeStruct((n, U_COLS), F32),
        compiler_params=_params(("parallel",)),
        name="in_proj",
    )(x, sc, sh, w)


def _lambda_scalar(lam_ref, lam_init):
    lp = lam_ref[...]
    s01 = jnp.sum(lp[0:1] * lp[1:2], axis=-1, keepdims=True)
    s23 = jnp.sum(lp[2:3] * lp[3:4], axis=-1, keepdims=True)
    return jnp.exp(s01) - jnp.exp(s23) + lam_init


def _softmax_rows(s):
    e = jnp.exp(s - jnp.max(s, -1, keepdims=True))
    return e / jnp.sum(e, -1, keepdims=True)


def _head_norm(o, nw, lam_init):
    return o * lax.rsqrt(jnp.mean(o * o, -1, keepdims=True) + EPS) * nw * (1.0 - lam_init)


def _layer_cache(tail, l, bsz):
    zeros = (0,) * len(tail)
    owned = DEPTH if l == 0 else 1
    spec = pl.BlockSpec((1, owned) + tuple(tail), lambda b, *_: (b, l) + zeros)
    return jax.ShapeDtypeStruct((bsz, DEPTH) + tuple(tail), F32), spec, owned


def _zero_later_layers(ref):
    for later in range(1, ref.shape[1]):
        ref[0, later] = jnp.zeros(ref.shape[2:], ref.dtype)


def _attn_ctx_kernel(u_ref, lam_ref, nw_ref, *rest, lam_init):
    att_ref, k_ref, v_ref = rest[-3:]
    lam = _lambda_scalar(lam_ref, lam_init)
    for h in range(N_HEADS):
        v = u_ref[:, COL_AV + h * D_HEAD_V:COL_AV + (h + 1) * D_HEAD_V]
        v_ref[0, 0, h] = v
        ps = []
        for m in range(2):
            c0 = h * D_HEAD_V + m * D_QK
            q = u_ref[:, COL_AQ + c0:COL_AQ + c0 + D_QK] * (D_QK ** -0.5)
            k = u_ref[:, COL_AK + c0:COL_AK + c0 + D_QK]
            k_ref[0, 0, h, m] = k
            ps.append(_softmax_rows(_bdot_nt(q, k)))
        o = _bdot(ps[0] - lam * ps[1], v)
        att_ref[:, h * D_HEAD_V:(h + 1) * D_HEAD_V] = _head_norm(o, nw_ref[...], lam_init).astype(BF16)
    _zero_later_layers(k_ref)
    _zero_later_layers(v_ref)


def _attention_ctx(u, lam_p, norm_w, lam_init, bsz, t, l, caches):
    n = bsz * t
    k_shape, k_spec, _ = _layer_cache((N_HEADS, 2, t, D_QK), l, bsz)
    v_shape, v_spec, _ = _layer_cache((N_HEADS, t, D_HEAD_V), l, bsz)
    in_specs = [pl.BlockSpec((t, 3 * D_ATT), lambda b: (b, 0)),
                pl.BlockSpec((4, D_QK), lambda b: (0, 0)),
                pl.BlockSpec((1, D_HEAD_V), lambda b: (0, 0))]
    args = [u, lam_p, norm_w.reshape(1, D_HEAD_V)]
    aliases = {}
    if caches is not None:
        in_specs += [pl.BlockSpec(memory_space=pl.ANY)] * 2
        aliases = {len(args): 1, len(args) + 1: 2}
        args += list(caches)
    return pl.pallas_call(
        functools.partial(_attn_ctx_kernel, lam_init=lam_init),
        grid=(bsz,),
        in_specs=in_specs,
        out_specs=[pl.BlockSpec((t, D_ATT), lambda b: (b, 0)), k_spec, v_spec],
        out_shape=[jax.ShapeDtypeStruct((n, D_ATT), BF16), k_shape, v_shape],
        input_output_aliases=aliases,
        compiler_params=_params(("parallel",)),
        name="attn_ctx",
    )(*args)


def _rope_kernel(u_ref, cos_ref, sa_ref, sb_ref, q_ref, k_ref, v_ref):
    cos, sa, sb = cos_ref[...], sa_ref[...], sb_ref[...]

    def rope(x):
        return x * cos + pltpu.roll(x, LANES - 16, 1) * sa + pltpu.roll(x, 16, 1) * sb

    for h in range(N_HEADS):
        q = rope(u_ref[:, COL_AQ + h * D_HEAD_V:COL_AQ + (h + 1) * D_HEAD_V]) * (LOG2E * D_QK ** -0.5)
        k = rope(u_ref[:, COL_AK + h * D_HEAD_V:COL_AK + (h + 1) * D_HEAD_V])
        for m in range(2):
            q_ref[0, h, m] = q[:, m * D_QK:(m + 1) * D_QK].astype(BF16)
            k_ref[0, h, m] = k[:, m * D_QK:(m + 1) * D_QK].astype(BF16)
        v_ref[0, h] = u_ref[:, COL_AV + h * D_HEAD_V:COL_AV + (h + 1) * D_HEAD_V].astype(BF16)


def _rope_tables(t):
    rows = jnp.repeat(jnp.arange(t // GRID_W, dtype=F32), GRID_W)
    cols = jnp.tile(jnp.arange(GRID_W, dtype=F32), t // GRID_W)
    half = D_QK // 2
    inv = ROPE_BASE ** (-jnp.arange(0, half, 2, dtype=F32) / half)
    ang_r = rows[:, None] * inv
    ang_c = cols[:, None] * inv
    ang = jnp.concatenate([ang_r, ang_r, ang_c, ang_c], -1)
    cos, sin = jnp.cos(ang), jnp.sin(ang)
    quarter = (jnp.arange(D_QK) // (D_QK // 4)) % 2
    sa = jnp.where(quarter == 0, -sin, 0.0)
    sb = jnp.where(quarter == 1, sin, 0.0)
    tile2 = lambda a: jnp.concatenate([a, a], -1)
    return tile2(cos), tile2(sa), tile2(sb)


def _rope_prep(u, bsz, t):
    tr = 512
    nb = t // tr
    cos, sa, sb = _rope_tables(t)
    tab = pl.BlockSpec((tr, LANES), lambda b, i: (i, 0))
    return pl.pallas_call(
        _rope_kernel,
        grid=(bsz, nb),
        in_specs=[pl.BlockSpec((tr, 3 * D_ATT), lambda b, i: (b * nb + i, 0)), tab, tab, tab],
        out_specs=[pl.BlockSpec((1, N_HEADS, 2, tr, D_QK), lambda b, i: (b, 0, 0, i, 0)),
                   pl.BlockSpec((1, N_HEADS, 2, tr, D_QK), lambda b, i: (b, 0, 0, i, 0)),
                   pl.BlockSpec((1, N_HEADS, tr, D_HEAD_V), lambda b, i: (b, 0, i, 0))],
        out_shape=[jax.ShapeDtypeStruct((bsz, N_HEADS, 2, t, D_QK), BF16),
                   jax.ShapeDtypeStruct((bsz, N_HEADS, 2, t, D_QK), BF16),
                   jax.ShapeDtypeStruct((bsz, N_HEADS, t, D_HEAD_V), BF16)],
        compiler_params=_params(("parallel", "parallel")),
        name="rope_prep",
    )(u, cos, sa, sb)


def _attn_lat_kernel(q_ref, k_ref, v_ref, lam_ref, nw_ref, o_ref, *, lam_init):
    lam = _lambda_scalar(lam_ref, lam_init)
    es, sums = [], []
    for m in range(2):
        s = lax.dot_general(q_ref[0, 0, m], k_ref[0, 0, m], NT_DIMS, preferred_element_type=F32)
        e = jnp.exp2(s - jnp.max(s, -1, keepdims=True))
        es.append(e)
        sums.append(jnp.sum(e, -1, keepdims=True))
    a = es[0] - (lam * sums[0] / sums[1]) * es[1]
    o = _bdot(a, v_ref[0, 0]) / sums[0]
    o_ref[...] = _head_norm(o, nw_ref[...], lam_init).astype(BF16)


def _attention_lat(q, k_all, v_all, lam_p, norm_w, lam_init, bsz, t):
    tq = 256
    nq = t // tq
    s = k_all.shape[3]
    return pl.pallas_call(
        functools.partial(_attn_lat_kernel, lam_init=lam_init),
        grid=(bsz, N_HEADS, nq),
        in_specs=[pl.BlockSpec((1, 1, 2, tq, D_QK), lambda b, h, i: (b, h, 0, i, 0)),
                  pl.BlockSpec((1, 1, 2, s, D_QK), lambda b, h, i: (b, h, 0, 0, 0)),
                  pl.BlockSpec((1, 1, s, D_HEAD_V), lambda b, h, i: (b, h, 0, 0)),
                  pl.BlockSpec((4, D_QK), lambda b, h, i: (0, 0)),
                  pl.BlockSpec((1, D_HEAD_V), lambda b, h, i: (0, 0))],
        out_specs=pl.BlockSpec((tq, D_HEAD_V), lambda b, h, i: (b * nq + i, h)),
        out_shape=jax.ShapeDtypeStruct((bsz * t, D_ATT), BF16),
        compiler_params=_params(("parallel", "parallel", "parallel")),
        name="attn_lat",
    )(q, k_all, v_all, lam_p, norm_w.reshape(1, D_HEAD_V))


HEAD_LANES = N_HEADS * D_STATE


def _scan_consts():
    t = lax.broadcasted_iota(jnp.int32, (CHUNK, HEAD_LANES), 0)
    s = lax.broadcasted_iota(jnp.int32, (CHUNK, HEAD_LANES), 1) & (CHUNK - 1)
    r = lax.broadcasted_iota(jnp.int32, (CHUNK, CHUNK), 0)
    c = lax.broadcasted_iota(jnp.int32, (CHUNK, CHUNK), 1)
    reads = (s <= t, s >= t)
    tri = (jnp.where(c <= r, 1.0, 0.0), jnp.where(c >= r, 1.0, 0.0))
    return reads, tri, s == t


def _group_mask(rows, cols, row_shift, col_shift):
    r = lax.broadcasted_iota(jnp.int32, (rows, cols), 0) >> 6
    c = (lax.broadcasted_iota(jnp.int32, (rows, cols), 1) >> 6) & (N_HEADS - 1)
    return (r >> row_shift) == (c >> col_shift)


def _spread(x, chans):
    return jnp.concatenate([jnp.broadcast_to(x[:, c:c + 1], (CHUNK, D_STATE)) for c in chans], axis=1)


def _stack_heads(x):
    return jnp.concatenate([x] * N_HEADS, axis=0)


def _mlstm_chunk(q4, k4, v4, g, d, cn_prev, m_prev, reads, tri, eye, bd):
    li = _spread(g, [GATE_I + d * N_HEADS + h for h in range(N_HEADS)])
    lf = _spread(jax.nn.log_sigmoid(g), [GATE_F + d * N_HEADS + h for h in range(N_HEADS)])
    bc = jnp.dot(tri[d], lf, precision=HIGHEST, preferred_element_type=F32)
    btot = jnp.sum(lf, 0, keepdims=True)
    b_row = jnp.sum(jnp.where(reads[1 - d], lf, 0.0), 0, keepdims=True)
    li_row = jnp.sum(jnp.where(eye, li, 0.0), 0, keepdims=True)
    dm = jnp.where(reads[d], bc - b_row + li_row, -jnp.inf)
    rmax = jnp.concatenate(
        [jnp.broadcast_to(jnp.max(dm[:, h * D_STATE:(h + 1) * D_STATE], -1, keepdims=True), (CHUNK, D_STATE))
         for h in range(N_HEADS)], axis=1)
    inter = bc + m_prev
    m_t = jnp.maximum(inter, rmax)
    w_inter = jnp.exp(inter - m_t)
    qs = (q4 * (D_STATE ** -0.5)).astype(BF16)
    kbd = jnp.where(bd[:, :HEAD_LANES], _stack_heads(k4.astype(BF16)), 0)
    s4 = lax.dot_general(qs, kbd, NT_DIMS, preferred_element_type=F32) * jnp.exp(dm - m_t)
    vo = jnp.concatenate([v4.astype(BF16), jnp.ones((CHUNK, HEAD_LANES), BF16)], axis=1)
    vbd = jnp.where(bd, _stack_heads(vo), 0)
    nd = (jnp.concatenate([w_inter, w_inter], axis=1)
          * jnp.dot(qs, cn_prev.astype(BF16), preferred_element_type=F32)
          + jnp.dot(s4.astype(BF16), vbd, preferred_element_type=F32))
    hc = nd[:, :HEAD_LANES] / jnp.maximum(jnp.abs(nd[:, HEAD_LANES:]), jnp.exp(-m_t))
    gcol = btot - bc + li
    m_new = jnp.maximum(btot + m_prev, jnp.max(gcol, 0, keepdims=True))
    w_c = jnp.exp(btot + m_prev - m_new)
    kw = (k4 * jnp.exp(gcol - m_new)).astype(BF16)
    dcn = lax.dot_general(kw, vo, TN_DIMS, preferred_element_type=F32)
    cn_new = jnp.concatenate([w_c, w_c], axis=1) * cn_prev + jnp.where(bd, dcn, 0.0)
    return hc, cn_new, m_new


def _mlstm_kernel(*refs, nblk, rows, zero_init, n_alias):
    qf_ref, kf_ref, vf_ref, gf_ref, qb_ref, kb_ref, vb_ref, gb_ref, bias_ref = refs[:9]
    if not zero_init:
        c0_ref, m0_ref = refs[9:11]
    outs = refs[9 + (0 if zero_init else 2) + n_alias:-2]
    hf_ref, hb_ref = outs[:2]
    cn_scr, m_scr = refs[-2:]
    j = pl.program_id(1)
    nchunk = rows // CHUNK

    @pl.when(j == 0)
    def _():
        if zero_init:
            cn_scr[...] = jnp.zeros_like(cn_scr)
            m_scr[...] = jnp.zeros_like(m_scr)
        else:
            cn_scr[...] = c0_ref[0]
            m_scr[...] = m0_ref[0]

    reads, tri, eye = _scan_consts()
    bd = _group_mask(HEAD_LANES, 2 * HEAD_LANES, 0, 0)

    def chunk(ci, carry):
        for d in range(2):
            q_ref, k_ref, v_ref, g_ref, h_ref = ((qf_ref, kf_ref, vf_ref, gf_ref, hf_ref) if d == 0
                                                 else (qb_ref, kb_ref, vb_ref, gb_ref, hb_ref))
            cj = ci if d == 0 else nchunk - 1 - ci
            rs = pl.ds(pl.multiple_of(cj * CHUNK, CHUNK), CHUNK)
            g = g_ref[rs, :] + bias_ref[...]
            hc, cn_new, m_new = _mlstm_chunk(q_ref[rs, :], k_ref[rs, :], v_ref[rs, :], g, d,
                                             cn_scr[d], m_scr[d], reads, tri, eye, bd)
            h_ref[rs, :] = hc
            cn_scr[d] = cn_new
            m_scr[d] = m_new
        return carry

    lax.fori_loop(0, nchunk, chunk, 0, unroll=True)

    if zero_init:
        cout_ref, nout_ref, mout_ref = outs[2:]

        @pl.when(j == nblk - 1)
        def _():
            r = lax.broadcasted_iota(jnp.int32, (D_STATE, D_STATE), 0)
            c = lax.broadcasted_iota(jnp.int32, (D_STATE, D_STATE), 1)
            for d in range(2):
                for h in range(N_HEADS):
                    r0, r1 = h * D_STATE, (h + 1) * D_STATE
                    cout_ref[0, 0, d, h] = cn_scr[d, r0:r1, r0:r1]
                    n_spread = cn_scr[d, r0:r1, HEAD_LANES + r0:HEAD_LANES + r1]
                    nout_ref[0, 0, d, h:h + 1, :] = jnp.sum(jnp.where(r == c, n_spread, 0.0), 0, keepdims=True)
            mout_ref[0, 0] = m_scr[...]
            _zero_later_layers(cout_ref)
            _zero_later_layers(nout_ref)
            _zero_later_layers(mout_ref)


def _mlstm(u, gate_bias, bsz, t, init, l=0, caches=None):
    rows = min(t, SCAN_ROWS)
    nblk = t // rows
    zero_init = init is None

    def fwd(col):
        return lambda b, j: (b * nblk + j, col)

    def bwd(col):
        return lambda b, j: (b * nblk + nblk - 1 - j, col)

    cq, ck, cv, cg = COL_MQ // D_REC, COL_MK // D_REC, COL_MV // D_REC, COL_SMALL // LANES
    in_specs = []
    for mk in (fwd, bwd):
        in_specs += [pl.BlockSpec((rows, D_REC), mk(cq)), pl.BlockSpec((rows, D_REC), mk(ck)),
                     pl.BlockSpec((rows, D_REC), mk(cv)), pl.BlockSpec((rows, LANES), mk(cg))]
    in_specs.append(pl.BlockSpec((1, LANES), lambda b, j: (0, 0)))
    args = [u] * 8 + [gate_bias]
    state_c = pl.BlockSpec((1, 2, HEAD_LANES, 2 * HEAD_LANES), lambda b, j: (b, 0, 0, 0))
    state_m = pl.BlockSpec((1, 2, 1, HEAD_LANES), lambda b, j: (b, 0, 0, 0))
    out_specs = [pl.BlockSpec((rows, D_REC), fwd(0)), pl.BlockSpec((rows, D_REC), bwd(0))]
    out_shape = [jax.ShapeDtypeStruct((bsz * t, D_REC), F32), jax.ShapeDtypeStruct((bsz * t, D_REC), F32)]
    aliases = {}
    if zero_init:
        for tail in ((2, N_HEADS, D_STATE, D_STATE), (2, N_HEADS, D_STATE), (2, 1, HEAD_LANES)):
            shape, spec, _ = _layer_cache(tail, l, bsz)
            out_shape.append(shape)
            out_specs.append(spec)
        if caches is not None:
            in_specs += [pl.BlockSpec(memory_space=pl.ANY)] * len(caches)
            aliases = {len(args) + i: 2 + i for i in range(len(caches))}
            args += list(caches)
    else:
        in_specs += [state_c, state_m]
        args += list(init)
    return pl.pallas_call(
        functools.partial(_mlstm_kernel, nblk=nblk, rows=rows, zero_init=zero_init, n_alias=len(aliases)),
        grid=(bsz, nblk),
        in_specs=in_specs,
        out_specs=out_specs,
        out_shape=out_shape,
        input_output_aliases=aliases,
        scratch_shapes=[pltpu.VMEM((2, HEAD_LANES, 2 * HEAD_LANES), F32),
                        pltpu.VMEM((2, 1, HEAD_LANES), F32)],
        compiler_params=_params(("parallel", "arbitrary")),
        name="mlstm_scan",
    )(*args)


def _conv_kernel(x_ref, prev_ref, next_ref, w_ref, b_ref, o_ref, *, nblk):
    i = pl.program_id(1)
    x = x_ref[...]
    rows = x.shape[0]
    r = lax.broadcasted_iota(jnp.int32, x.shape, 0)
    prev_row = prev_ref[7:8, :] * (i > 0).astype(F32)
    next_row = next_ref[0:1, :] * (i < nblk - 1).astype(F32)
    xm = jnp.where(r == 0, prev_row, pltpu.roll(x, 1, 0))
    xp = jnp.where(r == rows - 1, next_row, pltpu.roll(x, rows - 1, 0))
    w = w_ref[...]
    o_ref[...] = _silu(xm * w[0:1] + x * w[1:2] + xp * w[2:3] + b_ref[...])


def _ssd_conv(u, conv_w, conv_b, bsz, t):
    rows = min(t, 512)
    nblk = t // rows
    r8 = rows // 8
    width = 2 * D_REC
    c0 = COL_SX // width
    return pl.pallas_call(
        functools.partial(_conv_kernel, nblk=nblk),
        grid=(bsz, nblk),
        in_specs=[pl.BlockSpec((rows, width), lambda b, i: (b * nblk + i, c0)),
                  pl.BlockSpec((8, width), lambda b, i: (jnp.maximum((b * nblk + i) * r8 - 1, 0), c0)),
                  pl.BlockSpec((8, width), lambda b, i: (jnp.minimum((b * nblk + i + 1) * r8,
                                                                     bsz * nblk * r8 - 1), c0)),
                  pl.BlockSpec((D_CONV, width), lambda b, i: (0, 0)),
                  pl.BlockSpec((1, width), lambda b, i: (0, 0))],
        out_specs=pl.BlockSpec((rows, width), lambda b, i: (b * nblk + i, 0)),
        out_shape=jax.ShapeDtypeStruct((bsz * t, width), F32),
        compiler_params=_params(("parallel", "parallel")),
        name="ssd_conv",
    )(u, u, u, conv_w, conv_b.reshape(1, width))


def _ssd_chunk(x4, bcm, dt128, da128, d, sg_prev, reads, tri, b_sel, s_sel, bd):
    chans = [GATE_DT + d * N_HEADS + h for h in range(N_HEADS)]
    dt = _spread(dt128, chans)
    da = _spread(da128, chans)
    ac = jnp.dot(tri[d], da, precision=HIGHEST, preferred_element_type=F32)
    atot = jnp.sum(da, 0, keepdims=True)
    a_row = jnp.sum(jnp.where(reads[1 - d], da, 0.0), 0, keepdims=True)
    decay = jnp.exp(jnp.where(reads[d], ac - a_row, -jnp.inf))
    bmat = bcm[:, :LANES].astype(BF16)
    cmat = bcm[:, LANES:].astype(BF16)
    bbd = jnp.where(b_sel, _stack_heads(bmat), 0)
    g4 = lax.dot_general(cmat, bbd, NT_DIMS, preferred_element_type=F32)
    xbd = jnp.where(bd, _stack_heads((x4 * dt).astype(BF16)), 0)
    y = (jnp.dot((g4 * decay).astype(BF16), xbd, preferred_element_type=F32)
         + jnp.dot(cmat, sg_prev.astype(BF16), preferred_element_type=F32) * jnp.exp(ac))
    w = jnp.exp(atot - ac) * dt
    dsg = lax.dot_general(bmat, (x4 * w).astype(BF16), TN_DIMS, preferred_element_type=F32)
    sg_new = jnp.exp(atot) * sg_prev + jnp.where(s_sel, dsg, 0.0)
    return y, sg_new


def _ssd_kernel(*refs, nblk, rows, zero_init, n_alias):
    xf_ref, bcf_ref, gf_ref, xb_ref, bcb_ref, gb_ref, dtb_ref, alog_ref, dskip_ref = refs[:9]
    if not zero_init:
        s0_ref = refs[9]
    outs = refs[9 + (0 if zero_init else 1) + n_alias:-1]
    yf_ref, yb_ref = outs[:2]
    s_scr = refs[-1]
    j = pl.program_id(1)
    nchunk = rows // CHUNK

    @pl.when(j == 0)
    def _():
        if zero_init:
            s_scr[...] = jnp.zeros_like(s_scr)
        else:
            s_scr[...] = s0_ref[0]

    reads, tri, _ = _scan_consts()
    bd = _group_mask(HEAD_LANES, HEAD_LANES, 0, 0)
    b_sel = _group_mask(HEAD_LANES, LANES, 1, 0)
    s_sel = _group_mask(LANES, HEAD_LANES, 0, 1)
    a_coef = -jnp.exp(alog_ref[...])

    def chunk(ci, carry):
        for d in range(2):
            x_ref, bc_ref, g_ref, y_ref = ((xf_ref, bcf_ref, gf_ref, yf_ref) if d == 0
                                           else (xb_ref, bcb_ref, gb_ref, yb_ref))
            cj = ci if d == 0 else nchunk - 1 - ci
            rs = pl.ds(pl.multiple_of(cj * CHUNK, CHUNK), CHUNK)
            dt128 = jax.nn.softplus(g_ref[rs, :] + dtb_ref[...])
            x4 = x_ref[rs, :]
            y, sg_new = _ssd_chunk(x4, bc_ref[rs, :], dt128, dt128 * a_coef, d, s_scr[d],
                                   reads, tri, b_sel, s_sel, bd)
            if d == 0:
                y = y + dskip_ref[...] * x4
            y_ref[rs, :] = y
            s_scr[d] = sg_new
        return carry

    lax.fori_loop(0, nchunk, chunk, 0, unroll=True)

    if zero_init:
        sout_ref = outs[2]

        @pl.when(j == nblk - 1)
        def _():
            for d in range(2):
                s_t = s_scr[d].T
                for h in range(N_HEADS):
                    g0 = (h // 2) * D_STATE
                    sout_ref[0, 0, d, h] = s_t[h * D_STATE:(h + 1) * D_STATE, g0:g0 + D_STATE]
            _zero_later_layers(sout_ref)


def _ssd(xbc, u, dt_bias_row, alog_row, dskip_row, bsz, t, init, l=0, cache=None):
    rows = min(t, SCAN_ROWS)
    nblk = t // rows
    zero_init = init is None

    def fwd(col):
        return lambda b, j: (b * nblk + j, col)

    def bwd(col):
        return lambda b, j: (b * nblk + nblk - 1 - j, col)

    in_specs = []
    for mk in (fwd, bwd):
        in_specs += [pl.BlockSpec((rows, D_REC), mk(0)), pl.BlockSpec((rows, D_REC), mk(1)),
                     pl.BlockSpec((rows, LANES), mk(COL_SMALL // LANES))]
    in_specs += [pl.BlockSpec((1, LANES), lambda b, j: (0, 0)),
                 pl.BlockSpec((1, LANES), lambda b, j: (0, 0)),
                 pl.BlockSpec((1, D_REC), lambda b, j: (0, 0))]
    args = [xbc, xbc, u, xbc, xbc, u, dt_bias_row, alog_row, dskip_row]
    out_specs = [pl.BlockSpec((rows, D_REC), fwd(0)), pl.BlockSpec((rows, D_REC), bwd(0))]
    out_shape = [jax.ShapeDtypeStruct((bsz * t, D_REC), F32), jax.ShapeDtypeStruct((bsz * t, D_REC), F32)]
    aliases = {}
    if zero_init:
        shape, spec, _ = _layer_cache((2, N_HEADS, D_STATE, D_STATE), l, bsz)
        out_shape.append(shape)
        out_specs.append(spec)
        if cache is not None:
            in_specs.append(pl.BlockSpec(memory_space=pl.ANY))
            aliases = {len(args): 2}
            args.append(cache)
    else:
        in_specs.append(pl.BlockSpec((1, 2, LANES, HEAD_LANES), lambda b, j: (b, 0, 0, 0)))
        args.append(init)
    return pl.pallas_call(
        functools.partial(_ssd_kernel, nblk=nblk, rows=rows, zero_init=zero_init, n_alias=len(aliases)),
        grid=(bsz, nblk),
        in_specs=in_specs,
        out_specs=out_specs,
        out_shape=out_shape,
        input_output_aliases=aliases,
        scratch_shapes=[pltpu.VMEM((2, LANES, HEAD_LANES), F32)],
        compiler_params=_params(("parallel", "arbitrary")),
        name="ssd_scan",
    )(*args)


def _outproj_kernel(att_ref, hf_ref, hb_ref, mo_ref, yf_ref, yb_ref, z_ref, w_ref, x_ref, gate_ref,
                    mnw_ref, snw_ref, lg_ref, lb_ref, o_ref):
    hh = hf_ref[...] + hb_ref[...]
    parts = []
    for h in range(N_HEADS):
        xh = hh[:, h * D_STATE:(h + 1) * D_STATE]
        mu = jnp.mean(xh, -1, keepdims=True)
        dlt = xh - mu
        var = jnp.mean(dlt * dlt, -1, keepdims=True)
        parts.append(dlt * lax.rsqrt(var + EPS))
    ml = jax.nn.sigmoid(mo_ref[...]) * jnp.concatenate(parts, axis=1) * mnw_ref[...]
    yz = (yf_ref[...] + yb_ref[...]) * _silu(z_ref[...])
    parts = []
    for grp in range(N_GROUPS):
        yg = yz[:, grp * LANES:(grp + 1) * LANES]
        parts.append(yg * lax.rsqrt(jnp.mean(yg * yg, -1, keepdims=True) + EPS))
    ssm = jnp.concatenate(parts, axis=1) * snw_ref[...]
    mixed = (jnp.dot(att_ref[...], w_ref[0:D_ATT], preferred_element_type=F32)
             + _bdot(ml, w_ref[D_ATT:D_ATT + D_REC])
             + _bdot(ssm, w_ref[D_ATT + D_REC:D_MODEL]))
    y = ALPHA * x_ref[...] + gate_ref[0] * mixed
    o_ref[...] = _layernorm_rows(y, lg_ref[...], lb_ref[...])


def _out_proj(att, hf, hb, yf, yb, u, w, x, gate, mnw, snw, lg, lb, rows_per_mod):
    n = x.shape[0]
    tpb = rows_per_mod // ROW_TILE
    row = lambda width, col: pl.BlockSpec((ROW_TILE, width), lambda i: (i, col))
    vec = lambda width: pl.BlockSpec((1, width), lambda i: (0, 0))
    return pl.pallas_call(
        _outproj_kernel,
        grid=(n // ROW_TILE,),
        in_specs=[row(D_ATT, 0), row(D_REC, 0), row(D_REC, 0), row(D_REC, COL_MO // D_REC),
                  row(D_REC, 0), row(D_REC, 0), row(D_REC, COL_SZ // D_REC),
                  pl.BlockSpec((D_MODEL, D_MODEL), lambda i: (0, 0)),
                  row(D_MODEL, 0),
                  pl.BlockSpec((1, 1, D_MODEL), lambda i: (i // tpb, 0, 0)),
                  vec(D_REC), vec(D_REC), vec(D_MODEL), vec(D_MODEL)],
        out_specs=row(D_MODEL, 0),
        out_shape=jax.ShapeDtypeStruct((n, D_MODEL), F32),
        compiler_params=_params(("parallel",)),
        name="out_proj",
    )(att, hf, hb, u, yf, yb, u, w, x, gate, mnw.reshape(1, D_REC), snw.reshape(1, D_REC),
      lg.reshape(1, D_MODEL), lb.reshape(1, D_MODEL))


FF_TILE = D_FF // 2
FFN_ROWS = 1024
FFN_SUB = 512
FFN_VMEM_LIMIT = 56 * 1024 * 1024


def _swiglu_partial(h, w1, w3, w2):
    a = jnp.dot(h, w1, preferred_element_type=F32)
    b = jnp.dot(h, w3, preferred_element_type=F32)
    return jnp.dot((_silu(a) * b).astype(BF16), w2, preferred_element_type=F32)


def _ffn_kernel(x_ref, sc_ref, sh_ref, gate_ref, w1_ref, w3_ref, w2_ref, lg_ref, lb_ref, o_ref,
                h_scr, acc_scr):
    j = pl.program_id(1)

    @pl.when(j == 0)
    def _():
        h_scr[...] = (x_ref[...] * (1.0 + sc_ref[0]) + sh_ref[0]).astype(BF16)
        acc_scr[...] = jnp.zeros_like(acc_scr)

    for r0 in range(0, FFN_ROWS, FFN_SUB):
        rows = slice(r0, r0 + FFN_SUB)
        acc_scr[rows, :] += _swiglu_partial(h_scr[rows, :], w1_ref[...], w3_ref[...], w2_ref[...])

    @pl.when(j == pl.num_programs(1) - 1)
    def _():
        y = ALPHA * x_ref[...] + gate_ref[0] * acc_scr[...]
        o_ref[...] = _layernorm_rows(y, lg_ref[...], lb_ref[...])


def _ffn(x, sc, sh, gate, w1, w3, w2, lg, lb, rows_per_mod):
    n = x.shape[0]
    tpb = rows_per_mod // FFN_ROWS
    modspec = pl.BlockSpec((1, 1, D_MODEL), lambda i, j: (i // tpb, 0, 0))
    vec = pl.BlockSpec((1, D_MODEL), lambda i, j: (0, 0))
    return pl.pallas_call(
        _ffn_kernel,
        grid=(n // FFN_ROWS, D_FF // FF_TILE),
        in_specs=[pl.BlockSpec((FFN_ROWS, D_MODEL), lambda i, j: (i, 0)), modspec, modspec, modspec,
                  pl.BlockSpec((D_MODEL, FF_TILE), lambda i, j: (0, j)),
                  pl.BlockSpec((D_MODEL, FF_TILE), lambda i, j: (0, j)),
                  pl.BlockSpec((FF_TILE, D_MODEL), lambda i, j: (j, 0)), vec, vec],
        out_specs=pl.BlockSpec((FFN_ROWS, D_MODEL), lambda i, j: (i, 0)),
        out_shape=jax.ShapeDtypeStruct((n, D_MODEL), F32),
        scratch_shapes=[pltpu.VMEM((FFN_ROWS, D_MODEL), BF16), pltpu.VMEM((FFN_ROWS, D_MODEL), F32)],
        compiler_params=_params(("parallel", "arbitrary"), FFN_VMEM_LIMIT),
        name="ffn_dense",
    )(x, sc, sh, gate, w1, w3, w2, lg.reshape(1, D_MODEL), lb.reshape(1, D_MODEL))


def _router_kernel(x_ref, sc_ref, sh_ref, rw_ref, gates_ref, h_ref):
    h = x_ref[...] * (1.0 + sc_ref[0]) + sh_ref[0]
    h_ref[...] = h.astype(BF16)
    logits = jnp.dot(h, rw_ref[...], precision=HIGHEST, preferred_element_type=F32)
    lane = lax.broadcasted_iota(jnp.int32, logits.shape, 1)
    valid = lane < N_EXPERTS
    p = jnp.where(valid, _softmax_rows(jnp.where(valid, logits, -jnp.inf)), -2.0)
    p1 = jnp.max(p, -1, keepdims=True)
    i1 = jnp.min(jnp.where(p == p1, lane, LANES), -1, keepdims=True)
    rest = jnp.where(lane == i1, -1.0, p)
    p2 = jnp.max(rest, -1, keepdims=True)
    i2 = jnp.min(jnp.where(rest == p2, lane, LANES), -1, keepdims=True)
    tot = p1 + p2
    gates_ref[...] = jnp.where(lane == i1, p1 / tot, jnp.where(lane == i2, p2 / tot, 0.0))


def _router(x, sc, sh, router_w, rows_per_mod):
    n = x.shape[0]
    tpb = rows_per_mod // ROW_TILE
    modspec = pl.BlockSpec((1, 1, D_MODEL), lambda i: (i // tpb, 0, 0))
    rw = jnp.pad(router_w, ((0, 0), (0, LANES - N_EXPERTS)))
    return pl.pallas_call(
        _router_kernel,
        grid=(n // ROW_TILE,),
        in_specs=[pl.BlockSpec((ROW_TILE, D_MODEL), lambda i: (i, 0)), modspec, modspec,
                  pl.BlockSpec((D_MODEL, LANES), lambda i: (0, 0))],
        out_specs=[pl.BlockSpec((ROW_TILE, LANES), lambda i: (i, 0)),
                   pl.BlockSpec((ROW_TILE, D_MODEL), lambda i: (i, 0))],
        out_shape=[jax.ShapeDtypeStruct((n, LANES), F32), jax.ShapeDtypeStruct((n, D_MODEL), BF16)],
        compiler_params=_params(("parallel",)),
        name="router",
    )(x, sc, sh, rw)


MOE_ROWS = 1024
MOE_TILE = 128
MOE_MAX_TILES = MOE_ROWS // MOE_TILE
MOE_VMEM_LIMIT = 56 * 1024 * 1024


def _moe_kernel(h_ref, gates_ref, w1_ref, w3_ref, w2_ref, x_ref, gate_ref, lg_ref, lb_ref, o_ref,
                slot_scr, slott_scr, hs_scr, ys_scr):
    e = pl.program_id(1)
    j = pl.program_id(2)
    last_j = pl.num_programs(2) - 1

    @pl.when(jnp.logical_and(e == 0, j == 0))
    def _():
        r = lax.broadcasted_iota(jnp.int32, (MOE_ROWS, MOE_ROWS), 0)
        c = lax.broadcasted_iota(jnp.int32, (MOE_ROWS, MOE_ROWS), 1)
        before = jnp.where(c < r, 1.0, 0.0).astype(BF16)
        mask = gates_ref[...] != 0.0
        rank = jnp.dot(before, jnp.where(mask, 1.0, 0.0).astype(BF16), preferred_element_type=F32)
        slot = jnp.where(mask, rank, -1.0).astype(jnp.int32)
        slot_scr[...] = slot
        slott_scr[...] = slot.T
        o_ref[...] = jnp.zeros_like(o_ref)

    lane = lax.broadcasted_iota(jnp.int32, (MOE_ROWS, LANES), 1)
    slot_col = jnp.max(jnp.where(lane == e, slot_scr[...], -1), -1, keepdims=True)
    n_tiles = (jnp.max(slot_col) + MOE_TILE) // MOE_TILE

    def tile(k, carry):
        @pl.when(j == 0)
        def _():
            slot_row = slott_scr[pl.ds(e, 1), :]
            rr = lax.broadcasted_iota(jnp.int32, (MOE_TILE, MOE_ROWS), 0) + k * MOE_TILE
            pick = jnp.where(rr == slot_row, 1.0, 0.0).astype(BF16)
            hs_scr[k] = jnp.dot(pick, h_ref[...], preferred_element_type=F32).astype(BF16)
            ys_scr[k] = jnp.zeros((MOE_TILE, D_MODEL), F32)

        ys_scr[k] += _swiglu_partial(hs_scr[k], w1_ref[0], w3_ref[0], w2_ref[0])

        @pl.when(j == last_j)
        def _():
            g_col = jnp.sum(jnp.where(lane == e, gates_ref[...], 0.0), -1, keepdims=True)
            cc = lax.broadcasted_iota(jnp.int32, (MOE_ROWS, 2 * MOE_TILE), 1)
            cc = jnp.where(cc >= MOE_TILE, cc - MOE_TILE, cc) + k * MOE_TILE
            put = jnp.where(slot_col == cc, 1.0, 0.0).astype(BF16)
            y = ys_scr[k]
            y_hi = y.astype(BF16)
            y_lo = (y - y_hi.astype(F32)).astype(BF16)
            back = jnp.dot(put, jnp.concatenate([y_hi, y_lo], axis=0), preferred_element_type=F32)
            o_ref[...] += g_col * back

        return carry

    lax.fori_loop(0, n_tiles, tile, 0)

    @pl.when(jnp.logical_and(e == pl.num_programs(1) - 1, j == last_j))
    def _():
        y = ALPHA * x_ref[...] + gate_ref[0] * o_ref[...]
        o_ref[...] = _layernorm_rows(y, lg_ref[...], lb_ref[...])


def _moe(h, gates, w1, w3, w2, x, gate, lg, lb, rows_per_mod):
    n = h.shape[0]
    tpb = rows_per_mod // MOE_ROWS
    vec = pl.BlockSpec((1, D_MODEL), lambda i, e, j: (0, 0))
    return pl.pallas_call(
        _moe_kernel,
        grid=(n // MOE_ROWS, N_EXPERTS, D_FF // FF_TILE),
        in_specs=[pl.BlockSpec((MOE_ROWS, D_MODEL), lambda i, e, j: (i, 0)),
                  pl.BlockSpec((MOE_ROWS, LANES), lambda i, e, j: (i, 0)),
                  pl.BlockSpec((1, D_MODEL, FF_TILE), lambda i, e, j: (e, 0, j)),
                  pl.BlockSpec((1, D_MODEL, FF_TILE), lambda i, e, j: (e, 0, j)),
                  pl.BlockSpec((1, FF_TILE, D_MODEL), lambda i, e, j: (e, j, 0)),
                  pl.BlockSpec((MOE_ROWS, D_MODEL), lambda i, e, j: (i, 0)),
                  pl.BlockSpec((1, 1, D_MODEL), lambda i, e, j: (i // tpb, 0, 0)), vec, vec],
        out_specs=pl.BlockSpec((MOE_ROWS, D_MODEL), lambda i, e, j: (i, 0)),
        out_shape=jax.ShapeDtypeStruct((n, D_MODEL), F32),
        scratch_shapes=[pltpu.VMEM((MOE_ROWS, LANES), jnp.int32), pltpu.VMEM((LANES, MOE_ROWS), jnp.int32),
                        pltpu.VMEM((MOE_MAX_TILES, MOE_TILE, D_MODEL), BF16),
                        pltpu.VMEM((MOE_MAX_TILES, MOE_TILE, D_MODEL), F32)],
        compiler_params=_params(("parallel", "arbitrary", "arbitrary"), MOE_VMEM_LIMIT),
        name="moe",
    )(h, gates, w1, w3, w2, x, gate, lg.reshape(1, D_MODEL), lb.reshape(1, D_MODEL))


def _permute_w_in(w):
    pad = jnp.zeros((D_MODEL, U_COLS - ORIG_END), w.dtype)
    return jnp.concatenate([w[:, :ORIG_GATES], w[:, ORIG_SX:ORIG_DT], w[:, ORIG_SZ:ORIG_SX],
                            w[:, ORIG_GATES:ORIG_SZ], w[:, ORIG_DT:ORIG_END], pad], axis=1).astype(BF16)


def _small_row(vals, offset):
    v = vals.reshape(-1).astype(F32)
    return jnp.zeros((1, LANES), F32).at[0, offset:offset + v.shape[0]].set(v)


def _pack_mlstm_state(c, n, m):
    eye = jnp.eye(N_HEADS, dtype=F32)
    shape = c.shape[:2] + (HEAD_LANES, HEAD_LANES)
    cbd = jnp.einsum('bdhke,hg->bdhkge', c, eye).reshape(shape)
    nbd = jnp.einsum('bdhk,hg,e->bdhkge', n, eye, jnp.ones((D_STATE,), F32)).reshape(shape)
    return jnp.concatenate([cbd, nbd], axis=-1), jnp.repeat(m, D_STATE, axis=-1)[:, :, None, :]


def _pack_ssd_state(s):
    sel = (jnp.arange(N_GROUPS)[:, None] == jnp.arange(N_HEADS)[None, :] // 2).astype(F32)
    return jnp.einsum('bdhpn,gh->bdgnhp', s, sel).reshape(s.shape[:2] + (LANES, HEAD_LANES))


def _layer(x, mods, P, l, bsz, t, ctx, caches=None):
    sh1, sc1, g1, sh2, sc2, g2 = mods
    rows_per_mod = x.shape[0] // sh1.shape[0]
    lam_init = 0.8 - 0.6 * math.exp(-0.3 * l)
    u = _in_proj(x, sc1, sh1, P['w_in'][l], rows_per_mod)

    gate_bias = (_small_row(P['mlstm_gate_b'][l, 0], GATE_I) + _small_row(P['mlstm_gate_b'][l, 1], GATE_F))
    dt_bias = _small_row(P['ssm_dt_bias'][l], GATE_DT)
    alog = _small_row(P['ssm_A_log'][l], GATE_DT)
    dskip = jnp.repeat(P['ssm_D'][l].astype(F32), D_STATE).reshape(1, D_REC)

    if ctx is None:
        att, k_new, v_new = _attention_ctx(u, P['attn_lambda'][l], P['attn_norm_w'][l], lam_init, bsz, t, l,
                                           None if caches is None else caches[0:2])
        m_init = s_init = None
    else:
        ck, cv, c_c, c_n, c_m, c_s = ctx
        q, k, v = _rope_prep(u, bsz, t)
        k_all = jnp.concatenate([k, ck.astype(BF16)], axis=3)
        v_all = jnp.concatenate([v, cv.astype(BF16)], axis=2)
        att = _attention_lat(q, k_all, v_all, P['attn_lambda'][l], P['attn_norm_w'][l], lam_init, bsz, t)
        m_init = _pack_mlstm_state(c_c, c_n, c_m)
        s_init = _pack_ssd_state(c_s)
    hf, hb, *mlstm_caches = _mlstm(u, gate_bias, bsz, t, m_init, l, None if caches is None else caches[2:5])
    xbc = _ssd_conv(u, P['conv_w'][l], P['conv_b'][l], bsz, t)
    yf, yb, *ssd_caches = _ssd(xbc, u, dt_bias, alog, dskip, bsz, t, s_init, l,
                               None if caches is None else caches[5])

    x = _out_proj(att, hf, hb, yf, yb, u, P['w_out'][l], x, g1, P['mlstm_norm_w'][l], P['ssm_norm_w'][l],
                  P['ln_g'][l, 0], P['ln_b'][l, 0], rows_per_mod)
    if l % 2 == 0:
        x = _ffn(x, sc2, sh2, g2, P['ffn_w1'][l // 2], P['ffn_w3'][l // 2], P['ffn_w2'][l // 2],
                 P['ln_g'][l, 1], P['ln_b'][l, 1], rows_per_mod)
    else:
        gates, h2 = _router(x, sc2, sh2, P['router_w'][l // 2], rows_per_mod)
        x = _moe(h2, gates, P['moe_w1'][l // 2], P['moe_w3'][l // 2], P['moe_w2'][l // 2],
                 x, g2, P['ln_g'][l, 1], P['ln_b'][l, 1], rows_per_mod)
    if ctx is None:
        return x, (k_new, v_new, *mlstm_caches, *ssd_caches)
    return x, None


def kernel(x_prompt, x_sample, c, cache_attn_k, cache_attn_v, state_mlstm_C, state_mlstm_n, state_mlstm_m, state_ssm, c_ctx, w_ada, b_ada, w_in, w_out, attn_lambda, attn_norm_w, mlstm_gate_b, mlstm_norm_w, conv_w, conv_b, ssm_A_log, ssm_dt_bias, ssm_D, ssm_norm_w, ln_g, ln_b, ffn_w1, ffn_w3, ffn_w2, router_w, moe_w1, moe_w3, moe_w2):
    bsz, seq, _ = x_prompt.shape
    dbsz, dseq, _ = x_sample.shape
    P = dict(w_in=[_permute_w_in(w_in[l]) for l in range(DEPTH)], w_out=w_out.astype(BF16),
             attn_lambda=attn_lambda, attn_norm_w=attn_norm_w, mlstm_gate_b=mlstm_gate_b,
             mlstm_norm_w=mlstm_norm_w, conv_w=conv_w, conv_b=conv_b, ssm_A_log=ssm_A_log,
             ssm_dt_bias=ssm_dt_bias, ssm_D=ssm_D, ssm_norm_w=ssm_norm_w, ln_g=ln_g, ln_b=ln_b,
             ffn_w1=ffn_w1.astype(BF16), ffn_w3=ffn_w3.astype(BF16), ffn_w2=ffn_w2.astype(BF16),
             router_w=router_w, moe_w1=moe_w1.astype(BF16), moe_w3=moe_w3.astype(BF16),
             moe_w2=moe_w2.astype(BF16))

    cvec = jnp.zeros((8, D_MODEL), F32).at[0].set(c_ctx).at[1:1 + dbsz].set(c)
    mod = _modulation(cvec, w_ada, b_ada)

    def mods_for(l, lo, hi):
        return [mod[l, lo:hi, i * D_MODEL:(i + 1) * D_MODEL][:, None, :] for i in range(6)]

    y_prompt = x_prompt.reshape(bsz * seq, D_MODEL)
    caches = None
    for l in range(DEPTH):
        y_prompt, caches = _layer(y_prompt, mods_for(l, 0, 1), P, l, bsz, seq, None, caches)
    new_k, new_v, new_c, new_n, m_spread, new_s = caches

    y_sample = x_sample.reshape(dbsz * dseq, D_MODEL)
    for l in range(DEPTH):
        ctx = (cache_attn_k[:, l], cache_attn_v[:, l], state_mlstm_C[:, l], state_mlstm_n[:, l],
               state_mlstm_m[:, l], state_ssm[:, l])
        y_sample, _ = _layer(y_sample, mods_for(l, 1, 1 + dbsz), P, l, dbsz, dseq, ctx)

    return (y_prompt.reshape(bsz, seq, D_MODEL), y_sample.reshape(dbsz, dseq, D_MODEL),
            new_k, new_v, new_c, new_n, m_spread[:, :, :, 0, ::D_STATE], new_s)
```

```python
import functools
import math

import jax
import jax.numpy as jnp
from jax import lax
from jax.experimental import pallas as pl
from jax.experimental.pallas import tpu as pltpu

F32 = jnp.float32
BF16 = jnp.bfloat16
HIGHEST = lax.Precision.HIGHEST

D_MODEL = 1024
DEPTH = 2
GRID_W = 64
N_HEADS = 4
D_ATT = 512
D_HEAD_V = 128
D_QK = 64
D_REC = 256
D_STATE = 64
N_GROUPS = 2
D_CONV = 3
D_FF = 2816
N_EXPERTS = 8
ALPHA = (2.0 * DEPTH) ** 0.25
CHUNK = 64
ROPE_BASE = 10000.0
LOG2E = 1.4426950408889634
EPS = 1e-5

COL_AQ, COL_AK, COL_AV = 0, 512, 1024
COL_MQ, COL_MK, COL_MV, COL_MO = 1536, 1792, 2048, 2304
COL_SX, COL_SBC, COL_SZ = 2560, 2816, 3072
COL_SMALL = 3328
U_COLS = 3584
ORIG_GATES, ORIG_SZ, ORIG_SX, ORIG_DT, ORIG_END = 2560, 2576, 2832, 3344, 3352
GATE_I, GATE_F, GATE_DT = 0, 8, 16

LANES = 128
ROW_TILE = 512
SCAN_ROWS = 256
VMEM_LIMIT = 48 * 1024 * 1024

NT_DIMS = (((1,), (1,)), ((), ()))
TN_DIMS = (((0,), (0,)), ((), ()))


def _params(sem, vmem=VMEM_LIMIT):
    return pltpu.CompilerParams(dimension_semantics=sem, vmem_limit_bytes=vmem)


def _silu(x):
    return x * jax.nn.sigmoid(x)


def _bdot(a, b):
    return jnp.dot(a.astype(BF16), b.astype(BF16), preferred_element_type=F32)


def _bdot_nt(a, b):
    return lax.dot_general(a.astype(BF16), b.astype(BF16), NT_DIMS, preferred_element_type=F32)


def _bdot_tn(a, b):
    return lax.dot_general(a.astype(BF16), b.astype(BF16), TN_DIMS, preferred_element_type=F32)


def _layernorm_rows(y, g, b):
    mu = jnp.mean(y, -1, keepdims=True)
    d = y - mu
    var = jnp.mean(d * d, -1, keepdims=True)
    return d * lax.rsqrt(var + EPS) * g + b


def _mod_kernel(c_ref, w_ref, b_ref, o_ref):
    o_ref[0] = jnp.dot(_silu(c_ref[...]), w_ref[0], precision=HIGHEST,
                       preferred_element_type=F32) + b_ref[0]


def _modulation(cvec, w_ada, b_ada):
    tn = 1536
    return pl.pallas_call(
        _mod_kernel,
        grid=(DEPTH, 6 * D_MODEL // tn),
        in_specs=[pl.BlockSpec((8, D_MODEL), lambda l, j: (0, 0)),
                  pl.BlockSpec((1, D_MODEL, tn), lambda l, j: (l, 0, j)),
                  pl.BlockSpec((1, 1, tn), lambda l, j: (l, 0, j))],
        out_specs=pl.BlockSpec((1, 8, tn), lambda l, j: (l, 0, j)),
        out_shape=jax.ShapeDtypeStruct((DEPTH, 8, 6 * D_MODEL), F32),
        compiler_params=_params(("parallel", "parallel")),
        name="modulation",
    )(cvec, w_ada, b_ada.reshape(DEPTH, 1, 6 * D_MODEL))


def _inproj_kernel(x_ref, sc_ref, sh_ref, w_ref, o_ref):
    h = (x_ref[...] * (1.0 + sc_ref[0]) + sh_ref[0]).astype(BF16)
    for n0 in range(0, U_COLS, 512):
        o_ref[:, n0:n0 + 512] = jnp.dot(h, w_ref[:, n0:n0 + 512], preferred_element_type=F32)


def _in_proj(x, sc, sh, w, rows_per_mod):
    n = x.shape[0]
    tpb = rows_per_mod // ROW_TILE
    return pl.pallas_call(
        _inproj_kernel,
        grid=(n // ROW_TILE,),
        in_specs=[pl.BlockSpec((ROW_TILE, D_MODEL), lambda i: (i, 0)),
                  pl.BlockSpec((1, 1, D_MODEL), lambda i: (i // tpb, 0, 0)),
                  pl.BlockSpec((1, 1, D_MODEL), lambda i: (i // tpb, 0, 0)),
                  pl.BlockSpec((D_MODEL, U_COLS), lambda i: (0, 0))],
        out_specs=pl.BlockSpec((ROW_TILE, U_COLS), lambda i: (i, 0)),
        out_shape=jax.ShapeDtypeStruct((n, U_COLS), F32),
        compiler_params=_params(("parallel",)),
        name="in_proj",
    )(x, sc, sh, w)


def _lambda_scalar(lam_ref, lam_init):
    lp = lam_ref[...]
    s01 = jnp.sum(lp[0:1] * lp[1:2], axis=-1, keepdims=True)
    s23 = jnp.sum(lp[2:3] * lp[3:4], axis=-1, keepdims=True)
    return jnp.exp(s01) - jnp.exp(s23) + lam_init


def _softmax_rows(s):
    e = jnp.exp(s - jnp.max(s, -1, keepdims=True))
    return e / jnp.sum(e, -1, keepdims=True)


def _head_norm(o, nw, lam_init):
    return o * lax.rsqrt(jnp.mean(o * o, -1, keepdims=True) + EPS) * nw * (1.0 - lam_init)


def _layer_cache(tail, l, bsz):
    zeros = (0,) * len(tail)
    owned = DEPTH if l == 0 else 1
    spec = pl.BlockSpec((1, owned) + tuple(tail), lambda b, *_: (b, l) + zeros)
    return jax.ShapeDtypeStruct((bsz, DEPTH) + tuple(tail), F32), spec, owned


def _zero_later_layers(ref):
    for later in range(1, ref.shape[1]):
        ref[0, later] = jnp.zeros(ref.shape[2:], ref.dtype)


def _attn_ctx_kernel(u_ref, lam_ref, nw_ref, *rest, lam_init):
    att_ref, k_ref, v_ref = rest[-3:]
    lam = _lambda_scalar(lam_ref, lam_init)
    for h in range(N_HEADS):
        v = u_ref[:, COL_AV + h * D_HEAD_V:COL_AV + (h + 1) * D_HEAD_V]
        v_ref[0, 0, h] = v
        ps = []
        for m in range(2):
            c0 = h * D_HEAD_V + m * D_QK
            q = u_ref[:, COL_AQ + c0:COL_AQ + c0 + D_QK] * (D_QK ** -0.5)
            k = u_ref[:, COL_AK + c0:COL_AK + c0 + D_QK]
            k_ref[0, 0, h, m] = k
            ps.append(_softmax_rows(_bdot_nt(q, k)))
        o = _bdot(ps[0] - lam * ps[1], v)
        att_ref[:, h * D_HEAD_V:(h + 1) * D_HEAD_V] = _head_norm(o, nw_ref[...], lam_init).astype(BF16)
    _zero_later_layers(k_ref)
    _zero_later_layers(v_ref)


def _attention_ctx(u, lam_p, norm_w, lam_init, bsz, t, l, caches):
    n = bsz * t
    k_shape, k_spec, _ = _layer_cache((N_HEADS, 2, t, D_QK), l, bsz)
    v_shape, v_spec, _ = _layer_cache((N_HEADS, t, D_HEAD_V), l, bsz)
    in_specs = [pl.BlockSpec((t, 3 * D_ATT), lambda b: (b, 0)),
                pl.BlockSpec((4, D_QK), lambda b: (0, 0)),
                pl.BlockSpec((1, D_HEAD_V), lambda b: (0, 0))]
    args = [u, lam_p, norm_w.reshape(1, D_HEAD_V)]
    aliases = {}
    if caches is not None:
        in_specs += [pl.BlockSpec(memory_space=pl.ANY)] * 2
        aliases = {len(args): 1, len(args) + 1: 2}
        args += list(caches)
    return pl.pallas_call(
        functools.partial(_attn_ctx_kernel, lam_init=lam_init),
        grid=(bsz,),
        in_specs=in_specs,
        out_specs=[pl.BlockSpec((t, D_ATT), lambda b: (b, 0)), k_spec, v_spec],
        out_shape=[jax.ShapeDtypeStruct((n, D_ATT), BF16), k_shape, v_shape],
        input_output_aliases=aliases,
        compiler_params=_params(("parallel",)),
        name="attn_ctx",
    )(*args)


def _rope_kernel(u_ref, cos_ref, sa_ref, sb_ref, q_ref, k_ref, v_ref):
    cos, sa, sb = cos_ref[...], sa_ref[...], sb_ref[...]

    def rope(x):
        return x * cos + pltpu.roll(x, LANES - 16, 1) * sa + pltpu.roll(x, 16, 1) * sb

    for h in range(N_HEADS):
        q = rope(u_ref[:, COL_AQ + h * D_HEAD_V:COL_AQ + (h + 1) * D_HEAD_V]) * (LOG2E * D_QK ** -0.5)
        k = rope(u_ref[:, COL_AK + h * D_HEAD_V:COL_AK + (h + 1) * D_HEAD_V])
        for m in range(2):
            q_ref[0, h, m] = q[:, m * D_QK:(m + 1) * D_QK].astype(BF16)
            k_ref[0, h, m] = k[:, m * D_QK:(m + 1) * D_QK].astype(BF16)
        v_ref[0, h] = u_ref[:, COL_AV + h * D_HEAD_V:COL_AV + (h + 1) * D_HEAD_V].astype(BF16)


def _rope_tables(t):
    rows = jnp.repeat(jnp.arange(t // GRID_W, dtype=F32), GRID_W)
    cols = jnp.tile(jnp.arange(GRID_W, dtype=F32), t // GRID_W)
    half = D_QK // 2
    inv = ROPE_BASE ** (-jnp.arange(0, half, 2, dtype=F32) / half)
    ang_r = rows[:, None] * inv
    ang_c = cols[:, None] * inv
    ang = jnp.concatenate([ang_r, ang_r, ang_c, ang_c], -1)
    cos, sin = jnp.cos(ang), jnp.sin(ang)
    quarter = (jnp.arange(D_QK) // (D_QK // 4)) % 2
    sa = jnp.where(quarter == 0, -sin, 0.0)
    sb = jnp.where(quarter == 1, sin, 0.0)
    tile2 = lambda a: jnp.concatenate([a, a], -1)
    return tile2(cos), tile2(sa), tile2(sb)


def _rope_prep(u, bsz, t):
    tr = 512
    nb = t // tr
    cos, sa, sb = _rope_tables(t)
    tab = pl.BlockSpec((tr, LANES), lambda b, i: (i, 0))
    return pl.pallas_call(
        _rope_kernel,
        grid=(bsz, nb),
        in_specs=[pl.BlockSpec((tr, 3 * D_ATT), lambda b, i: (b * nb + i, 0)), tab, tab, tab],
        out_specs=[pl.BlockSpec((1, N_HEADS, 2, tr, D_QK), lambda b, i: (b, 0, 0, i, 0)),
                   pl.BlockSpec((1, N_HEADS, 2, tr, D_QK), lambda b, i: (b, 0, 0, i, 0)),
                   pl.BlockSpec((1, N_HEADS, tr, D_HEAD_V), lambda b, i: (b, 0, i, 0))],
        out_shape=[jax.ShapeDtypeStruct((bsz, N_HEADS, 2, t, D_QK), BF16),
                   jax.ShapeDtypeStruct((bsz, N_HEADS, 2, t, D_QK), BF16),
                   jax.ShapeDtypeStruct((bsz, N_HEADS, t, D_HEAD_V), BF16)],
        compiler_params=_params(("parallel", "parallel")),
        name="rope_prep",
    )(u, cos, sa, sb)


def _attn_lat_kernel(q_ref, k_ref, v_ref, lam_ref, nw_ref, o_ref, *, lam_init):
    lam = _lambda_scalar(lam_ref, lam_init)
    es, sums = [], []
    for m in range(2):
        s = lax.dot_general(q_ref[0, 0, m], k_ref[0, 0, m], NT_DIMS, preferred_element_type=F32)
        e = jnp.exp2(s - jnp.max(s, -1, keepdims=True))
        es.append(e)
        sums.append(jnp.sum(e, -1, keepdims=True))
    a = es[0] - (lam * sums[0] / sums[1]) * es[1]
    o = _bdot(a, v_ref[0, 0]) / sums[0]
    o_ref[...] = _head_norm(o, nw_ref[...], lam_init).astype(BF16)


def _attention_lat(q, k_all, v_all, lam_p, norm_w, lam_init, bsz, t):
    tq = 256
    nq = t // tq
    s = k_all.shape[3]
    return pl.pallas_call(
        functools.partial(_attn_lat_kernel, lam_init=lam_init),
        grid=(bsz, N_HEADS, nq),
        in_specs=[pl.BlockSpec((1, 1, 2, tq, D_QK), lambda b, h, i: (b, h, 0, i, 0)),
                  pl.BlockSpec((1, 1, 2, s, D_QK), lambda b, h, i: (b, h, 0, 0, 0)),
                  pl.BlockSpec((1, 1, s, D_HEAD_V), lambda b, h, i: (b, h, 0, 0)),
                  pl.BlockSpec((4, D_QK), lambda b, h, i: (0, 0)),
                  pl.BlockSpec((1, D_HEAD_V), lambda b, h, i: (0, 0))],
        out_specs=pl.BlockSpec((tq, D_HEAD_V), lambda b, h, i: (b * nq + i, h)),
        out_shape=jax.ShapeDtypeStruct((bsz * t, D_ATT), BF16),
        compiler_params=_params(("parallel", "parallel", "parallel")),
        name="attn_lat",
    )(q, k_all, v_all, lam_p, norm_w.reshape(1, D_HEAD_V))


HEAD_LANES = N_HEADS * D_STATE


def _scan_consts():
    t = lax.broadcasted_iota(jnp.int32, (CHUNK, HEAD_LANES), 0)
    s = lax.broadcasted_iota(jnp.int32, (CHUNK, HEAD_LANES), 1) & (CHUNK - 1)
    r = lax.broadcasted_iota(jnp.int32, (CHUNK, CHUNK), 0)
    c = lax.broadcasted_iota(jnp.int32, (CHUNK, CHUNK), 1)
    reads = (jnp.where(s <= t, 1.0, 0.0), jnp.where(s >= t, 1.0, 0.0))
    block = (jnp.where(s <= t, 0.0, -jnp.inf), jnp.where(s >= t, 0.0, -jnp.inf))
    tri = (jnp.where(c <= r, 1.0, 0.0), jnp.where(c >= r, 1.0, 0.0))
    return reads, block, tri, jnp.where(s == t, 1.0, 0.0)


def _group_mask(rows, cols, row_shift, col_shift, dtype):
    r = lax.broadcasted_iota(jnp.int32, (rows, cols), 0) >> 6
    c = (lax.broadcasted_iota(jnp.int32, (rows, cols), 1) >> 6) & (N_HEADS - 1)
    return jnp.where((r >> row_shift) == (c >> col_shift), 1.0, 0.0).astype(dtype)


def _spread(x, chans):
    return jnp.concatenate([jnp.broadcast_to(x[:, c:c + 1], (CHUNK, D_STATE)) for c in chans], axis=1)


def _stack_heads(x):
    return jnp.concatenate([x] * N_HEADS, axis=0)


def _mlstm_chunk(q4, k4, v4, g, d, cn_prev, m_prev, reads, block, tri, eye, bd, bd_f32):
    li = _spread(g, [GATE_I + d * N_HEADS + h for h in range(N_HEADS)])
    lf = _spread(jax.nn.log_sigmoid(g), [GATE_F + d * N_HEADS + h for h in range(N_HEADS)])
    bc = jnp.dot(tri[d], lf, precision=HIGHEST, preferred_element_type=F32)
    btot = jnp.sum(lf, 0, keepdims=True)
    b_row = jnp.sum(reads[1 - d] * lf, 0, keepdims=True)
    li_row = jnp.sum(eye * li, 0, keepdims=True)
    dm = bc - b_row + li_row + block[d]
    rmax = jnp.concatenate(
        [jnp.broadcast_to(jnp.max(dm[:, h * D_STATE:(h + 1) * D_STATE], -1, keepdims=True), (CHUNK, D_STATE))
         for h in range(N_HEADS)], axis=1)
    inter = bc + m_prev
    m_t = jnp.maximum(inter, rmax)
    w_inter = jnp.exp(inter - m_t)
    qs = (q4 * (D_STATE ** -0.5)).astype(BF16)
    kbd = bd[:, :HEAD_LANES] * _stack_heads(k4.astype(BF16))
    s4 = lax.dot_general(qs, kbd, NT_DIMS, preferred_element_type=F32) * jnp.exp(dm - m_t)
    vo = jnp.concatenate([v4.astype(BF16), jnp.ones((CHUNK, HEAD_LANES), BF16)], axis=1)
    vbd = bd * _stack_heads(vo)
    nd = (jnp.concatenate([w_inter, w_inter], axis=1)
          * jnp.dot(qs, bd * _stack_heads(cn_prev.astype(BF16)), preferred_element_type=F32)
          + jnp.dot(s4.astype(BF16), vbd, preferred_element_type=F32))
    hc = nd[:, :HEAD_LANES] / jnp.maximum(jnp.abs(nd[:, HEAD_LANES:]), jnp.exp(-m_t))
    gcol = btot - bc + li
    m_new = jnp.maximum(btot + m_prev, jnp.max(gcol, 0, keepdims=True))
    w_c = jnp.exp(btot + m_prev - m_new)
    kw = (k4 * jnp.exp(gcol - m_new)).astype(BF16)
    dcn = lax.dot_general(kw, vo, TN_DIMS, preferred_element_type=F32)
    own = sum(bd_f32[h * D_STATE:(h + 1) * D_STATE] * dcn[h * D_STATE:(h + 1) * D_STATE]
              for h in range(N_HEADS))
    cn_new = jnp.concatenate([w_c, w_c], axis=1) * cn_prev + own
    return hc, cn_new, m_new


def _mlstm_kernel(*refs, nblk, rows, zero_init, n_alias):
    qf_ref, kf_ref, vf_ref, gf_ref, qb_ref, kb_ref, vb_ref, gb_ref, bias_ref = refs[:9]
    if not zero_init:
        c0_ref, m0_ref = refs[9:11]
    outs = refs[9 + (0 if zero_init else 2) + n_alias:-2]
    hf_ref, hb_ref = outs[:2]
    cn_scr, m_scr = refs[-2:]
    j = pl.program_id(1)
    nchunk = rows // CHUNK

    @pl.when(j == 0)
    def _():
        if zero_init:
            cn_scr[...] = jnp.zeros_like(cn_scr)
            m_scr[...] = jnp.zeros_like(m_scr)
        else:
            cn_scr[...] = c0_ref[0]
            m_scr[...] = m0_ref[0]

    reads, block, tri, eye = _scan_consts()
    bd = _group_mask(HEAD_LANES, 2 * HEAD_LANES, 0, 0, BF16)
    bd_f32 = _group_mask(HEAD_LANES, 2 * HEAD_LANES, 0, 0, F32)

    def chunk(ci, carry):
        for d in range(2):
            q_ref, k_ref, v_ref, g_ref, h_ref = ((qf_ref, kf_ref, vf_ref, gf_ref, hf_ref) if d == 0
                                                 else (qb_ref, kb_ref, vb_ref, gb_ref, hb_ref))
            cj = ci if d == 0 else nchunk - 1 - ci
            rs = pl.ds(pl.multiple_of(cj * CHUNK, CHUNK), CHUNK)
            g = g_ref[rs, :] + bias_ref[...]
            hc, cn_new, m_new = _mlstm_chunk(q_ref[rs, :], k_ref[rs, :], v_ref[rs, :], g, d,
                                             cn_scr[d], m_scr[d], reads, block, tri, eye, bd, bd_f32)
            h_ref[rs, :] = hc
            cn_scr[d] = cn_new
            m_scr[d] = m_new
        return carry

    lax.fori_loop(0, nchunk, chunk, 0, unroll=True)

    if zero_init:
        cout_ref, nout_ref, mout_ref = outs[2:]

        @pl.when(j == nblk - 1)
        def _():
            r = lax.broadcasted_iota(jnp.int32, (D_STATE, D_STATE), 0)
            c = lax.broadcasted_iota(jnp.int32, (D_STATE, D_STATE), 1)
            for d in range(2):
                for h in range(N_HEADS):
                    r0, r1 = h * D_STATE, (h + 1) * D_STATE
                    cout_ref[0, 0, d, h] = cn_scr[d, :, r0:r1]
                    n_spread = cn_scr[d, :, HEAD_LANES + r0:HEAD_LANES + r1]
                    nout_ref[0, 0, d, h:h + 1, :] = jnp.sum(jnp.where(r == c, n_spread, 0.0), 0, keepdims=True)
            mout_ref[0, 0] = m_scr[...]
            _zero_later_layers(cout_ref)
            _zero_later_layers(nout_ref)
            _zero_later_layers(mout_ref)


def _mlstm(u, gate_bias, bsz, t, init, l=0, caches=None):
    rows = min(t, SCAN_ROWS)
    nblk = t // rows
    zero_init = init is None

    def fwd(col):
        return lambda b, j: (b * nblk + j, col)

    def bwd(col):
        return lambda b, j: (b * nblk + nblk - 1 - j, col)

    cq, ck, cv, cg = COL_MQ // D_REC, COL_MK // D_REC, COL_MV // D_REC, COL_SMALL // LANES
    in_specs = []
    for mk in (fwd, bwd):
        in_specs += [pl.BlockSpec((rows, D_REC), mk(cq)), pl.BlockSpec((rows, D_REC), mk(ck)),
                     pl.BlockSpec((rows, D_REC), mk(cv)), pl.BlockSpec((rows, LANES), mk(cg))]
    in_specs.append(pl.BlockSpec((1, LANES), lambda b, j: (0, 0)))
    args = [u] * 8 + [gate_bias]
    state_c = pl.BlockSpec((1, 2, D_STATE, 2 * HEAD_LANES), lambda b, j: (b, 0, 0, 0))
    state_m = pl.BlockSpec((1, 2, 1, HEAD_LANES), lambda b, j: (b, 0, 0, 0))
    out_specs = [pl.BlockSpec((rows, D_REC), fwd(0)), pl.BlockSpec((rows, D_REC), bwd(0))]
    out_shape = [jax.ShapeDtypeStruct((bsz * t, D_REC), F32), jax.ShapeDtypeStruct((bsz * t, D_REC), F32)]
    aliases = {}
    if zero_init:
        for tail in ((2, N_HEADS, D_STATE, D_STATE), (2, N_HEADS, D_STATE), (2, 1, HEAD_LANES)):
            shape, spec, _ = _layer_cache(tail, l, bsz)
            out_shape.append(shape)
            out_specs.append(spec)
        if caches is not None:
            in_specs += [pl.BlockSpec(memory_space=pl.ANY)] * len(caches)
            aliases = {len(args) + i: 2 + i for i in range(len(caches))}
            args += list(caches)
    else:
        in_specs += [state_c, state_m]
        args += list(init)
    return pl.pallas_call(
        functools.partial(_mlstm_kernel, nblk=nblk, rows=rows, zero_init=zero_init, n_alias=len(aliases)),
        grid=(bsz, nblk),
        in_specs=in_specs,
        out_specs=out_specs,
        out_shape=out_shape,
        input_output_aliases=aliases,
        scratch_shapes=[pltpu.VMEM((2, D_STATE, 2 * HEAD_LANES), F32),
                        pltpu.VMEM((2, 1, HEAD_LANES), F32)],
        compiler_params=_params(("parallel", "arbitrary")),
        name="mlstm_scan",
    )(*args)


def _conv_kernel(x_ref, prev_ref, next_ref, w_ref, b_ref, o_ref, *, nblk):
    i = pl.program_id(1)
    x = x_ref[...]
    rows = x.shape[0]
    r = lax.broadcasted_iota(jnp.int32, x.shape, 0)
    prev_row = prev_ref[7:8, :] * (i > 0).astype(F32)
    next_row = next_ref[0:1, :] * (i < nblk - 1).astype(F32)
    xm = jnp.where(r == 0, prev_row, pltpu.roll(x, 1, 0))
    xp = jnp.where(r == rows - 1, next_row, pltpu.roll(x, rows - 1, 0))
    w = w_ref[...]
    o_ref[...] = _silu(xm * w[0:1] + x * w[1:2] + xp * w[2:3] + b_ref[...])


def _ssd_conv(u, conv_w, conv_b, bsz, t):
    rows = min(t, 512)
    nblk = t // rows
    r8 = rows // 8
    width = 2 * D_REC
    c0 = COL_SX // width
    return pl.pallas_call(
        functools.partial(_conv_kernel, nblk=nblk),
        grid=(bsz, nblk),
        in_specs=[pl.BlockSpec((rows, width), lambda b, i: (b * nblk + i, c0)),
                  pl.BlockSpec((8, width), lambda b, i: (jnp.maximum((b * nblk + i) * r8 - 1, 0), c0)),
                  pl.BlockSpec((8, width), lambda b, i: (jnp.minimum((b * nblk + i + 1) * r8,
                                                                     bsz * nblk * r8 - 1), c0)),
                  pl.BlockSpec((D_CONV, width), lambda b, i: (0, 0)),
                  pl.BlockSpec((1, width), lambda b, i: (0, 0))],
        out_specs=pl.BlockSpec((rows, width), lambda b, i: (b * nblk + i, 0)),
        out_shape=jax.ShapeDtypeStruct((bsz * t, width), F32),
        compiler_params=_params(("parallel", "parallel")),
        name="ssd_conv",
    )(u, u, u, conv_w, conv_b.reshape(1, width))


def _ssd_chunk(x4, bcm, dt128, da128, d, sg_prev, reads, block, tri, b_sel, s_sel, bd):
    chans = [GATE_DT + d * N_HEADS + h for h in range(N_HEADS)]
    dt = _spread(dt128, chans)
    da = _spread(da128, chans)
    ac = jnp.dot(tri[d], da, precision=HIGHEST, preferred_element_type=F32)
    atot = jnp.sum(da, 0, keepdims=True)
    a_row = jnp.sum(reads[1 - d] * da, 0, keepdims=True)
    decay = jnp.exp(ac - a_row + block[d])
    bmat = bcm[:, :LANES].astype(BF16)
    cmat = bcm[:, LANES:].astype(BF16)
    bbd = b_sel * _stack_heads(bmat)
    g4 = lax.dot_general(cmat, bbd, NT_DIMS, preferred_element_type=F32)
    xbd = bd * _stack_heads((x4 * dt).astype(BF16))
    y = (jnp.dot((g4 * decay).astype(BF16), xbd, preferred_element_type=F32)
         + jnp.dot(cmat, sg_prev.astype(BF16), preferred_element_type=F32) * jnp.exp(ac))
    w = jnp.exp(atot - ac) * dt
    dsg = lax.dot_general(bmat, (x4 * w).astype(BF16), TN_DIMS, preferred_element_type=F32)
    sg_new = jnp.exp(atot) * sg_prev + s_sel * dsg
    return y, sg_new


def _ssd_kernel(*refs, nblk, rows, zero_init, n_alias):
    xf_ref, bcf_ref, gf_ref, xb_ref, bcb_ref, gb_ref, dtb_ref, alog_ref, dskip_ref = refs[:9]
    if not zero_init:
        s0_ref = refs[9]
    outs = refs[9 + (0 if zero_init else 1) + n_alias:-1]
    yf_ref, yb_ref = outs[:2]
    s_scr = refs[-1]
    j = pl.program_id(1)
    nchunk = rows // CHUNK

    @pl.when(j == 0)
    def _():
        if zero_init:
            s_scr[...] = jnp.zeros_like(s_scr)
        else:
            s_scr[...] = s0_ref[0]

    reads, block, tri, _ = _scan_consts()
    bd = _group_mask(HEAD_LANES, HEAD_LANES, 0, 0, BF16)
    b_sel = _group_mask(HEAD_LANES, LANES, 1, 0, BF16)
    s_sel = _group_mask(LANES, HEAD_LANES, 0, 1, F32)
    a_coef = -jnp.exp(alog_ref[...])

    def chunk(ci, carry):
        for d in range(2):
            x_ref, bc_ref, g_ref, y_ref = ((xf_ref, bcf_ref, gf_ref, yf_ref) if d == 0
                                           else (xb_ref, bcb_ref, gb_ref, yb_ref))
            cj = ci if d == 0 else nchunk - 1 - ci
            rs = pl.ds(pl.multiple_of(cj * CHUNK, CHUNK), CHUNK)
            dt128 = jax.nn.softplus(g_ref[rs, :] + dtb_ref[...])
            x4 = x_ref[rs, :]
            y, sg_new = _ssd_chunk(x4, bc_ref[rs, :], dt128, dt128 * a_coef, d, s_scr[d],
                                   reads, block, tri, b_sel, s_sel, bd)
            if d == 0:
                y = y + dskip_ref[...] * x4
            y_ref[rs, :] = y
            s_scr[d] = sg_new
        return carry

    lax.fori_loop(0, nchunk, chunk, 0, unroll=True)

    if zero_init:
        sout_ref = outs[2]

        @pl.when(j == nblk - 1)
        def _():
            for d in range(2):
                s_t = s_scr[d].T
                for h in range(N_HEADS):
                    g0 = (h // 2) * D_STATE
                    sout_ref[0, 0, d, h] = s_t[h * D_STATE:(h + 1) * D_STATE, g0:g0 + D_STATE]
            _zero_later_layers(sout_ref)


def _ssd(xbc, u, dt_bias_row, alog_row, dskip_row, bsz, t, init, l=0, cache=None):
    rows = min(t, SCAN_ROWS)
    nblk = t // rows
    zero_init = init is None

    def fwd(col):
        return lambda b, j: (b * nblk + j, col)

    def bwd(col):
        return lambda b, j: (b * nblk + nblk - 1 - j, col)

    in_specs = []
    for mk in (fwd, bwd):
        in_specs += [pl.BlockSpec((rows, D_REC), mk(0)), pl.BlockSpec((rows, D_REC), mk(1)),
                     pl.BlockSpec((rows, LANES), mk(COL_SMALL // LANES))]
    in_specs += [pl.BlockSpec((1, LANES), lambda b, j: (0, 0)),
                 pl.BlockSpec((1, LANES), lambda b, j: (0, 0)),
                 pl.BlockSpec((1, D_REC), lambda b, j: (0, 0))]
    args = [xbc, xbc, u, xbc, xbc, u, dt_bias_row, alog_row, dskip_row]
    out_specs = [pl.BlockSpec((rows, D_REC), fwd(0)), pl.BlockSpec((rows, D_REC), bwd(0))]
    out_shape = [jax.ShapeDtypeStruct((bsz * t, D_REC), F32), jax.ShapeDtypeStruct((bsz * t, D_REC), F32)]
    aliases = {}
    if zero_init:
        shape, spec, _ = _layer_cache((2, N_HEADS, D_STATE, D_STATE), l, bsz)
        out_shape.append(shape)
        out_specs.append(spec)
        if cache is not None:
            in_specs.append(pl.BlockSpec(memory_space=pl.ANY))
            aliases = {len(args): 2}
            args.append(cache)
    else:
        in_specs.append(pl.BlockSpec((1, 2, LANES, HEAD_LANES), lambda b, j: (b, 0, 0, 0)))
        args.append(init)
    return pl.pallas_call(
        functools.partial(_ssd_kernel, nblk=nblk, rows=rows, zero_init=zero_init, n_alias=len(aliases)),
        grid=(bsz, nblk),
        in_specs=in_specs,
        out_specs=out_specs,
        out_shape=out_shape,
        input_output_aliases=aliases,
        scratch_shapes=[pltpu.VMEM((2, LANES, HEAD_LANES), F32)],
        compiler_params=_params(("parallel", "arbitrary")),
        name="ssd_scan",
    )(*args)


def _outproj_kernel(att_ref, hf_ref, hb_ref, mo_ref, yf_ref, yb_ref, z_ref, w_ref, x_ref, gate_ref,
                    mnw_ref, snw_ref, lg_ref, lb_ref, o_ref):
    hh = hf_ref[...] + hb_ref[...]
    parts = []
    for h in range(N_HEADS):
        xh = hh[:, h * D_STATE:(h + 1) * D_STATE]
        mu = jnp.mean(xh, -1, keepdims=True)
        dlt = xh - mu
        var = jnp.mean(dlt * dlt, -1, keepdims=True)
        parts.append(dlt * lax.rsqrt(var + EPS))
    ml = jax.nn.sigmoid(mo_ref[...]) * jnp.concatenate(parts, axis=1) * mnw_ref[...]
    yz = (yf_ref[...] + yb_ref[...]) * _silu(z_ref[...])
    parts = []
    for grp in range(N_GROUPS):
        yg = yz[:, grp * LANES:(grp + 1) * LANES]
        parts.append(yg * lax.rsqrt(jnp.mean(yg * yg, -1, keepdims=True) + EPS))
    ssm = jnp.concatenate(parts, axis=1) * snw_ref[...]
    mixed = (jnp.dot(att_ref[...], w_ref[0:D_ATT], preferred_element_type=F32)
             + _bdot(ml, w_ref[D_ATT:D_ATT + D_REC])
             + _bdot(ssm, w_ref[D_ATT + D_REC:D_MODEL]))
    y = ALPHA * x_ref[...] + gate_ref[0] * mixed
    o_ref[...] = _layernorm_rows(y, lg_ref[...], lb_ref[...])


def _out_proj(att, hf, hb, yf, yb, u, w, x, gate, mnw, snw, lg, lb, rows_per_mod):
    n = x.shape[0]
    tpb = rows_per_mod // ROW_TILE
    row = lambda width, col: pl.BlockSpec((ROW_TILE, width), lambda i: (i, col))
    vec = lambda width: pl.BlockSpec((1, width), lambda i: (0, 0))
    return pl.pallas_call(
        _outproj_kernel,
        grid=(n // ROW_TILE,),
        in_specs=[row(D_ATT, 0), row(D_REC, 0), row(D_REC, 0), row(D_REC, COL_MO // D_REC),
                  row(D_REC, 0), row(D_REC, 0), row(D_REC, COL_SZ // D_REC),
                  pl.BlockSpec((D_MODEL, D_MODEL), lambda i: (0, 0)),
                  row(D_MODEL, 0),
                  pl.BlockSpec((1, 1, D_MODEL), lambda i: (i // tpb, 0, 0)),
                  vec(D_REC), vec(D_REC), vec(D_MODEL), vec(D_MODEL)],
        out_specs=row(D_MODEL, 0),
        out_shape=jax.ShapeDtypeStruct((n, D_MODEL), F32),
        compiler_params=_params(("parallel",)),
        name="out_proj",
    )(att, hf, hb, u, yf, yb, u, w, x, gate, mnw.reshape(1, D_REC), snw.reshape(1, D_REC),
      lg.reshape(1, D_MODEL), lb.reshape(1, D_MODEL))


FF_TILE = D_FF // 2
FFN_ROWS = 1024
FFN_SUB = 512
FFN_VMEM_LIMIT = 56 * 1024 * 1024


def _swiglu_partial(h, w1, w3, w2):
    a = jnp.dot(h, w1, preferred_element_type=F32)
    b = jnp.dot(h, w3, preferred_element_type=F32)
    return jnp.dot((_silu(a) * b).astype(BF16), w2, preferred_element_type=F32)


def _ffn_kernel(x_ref, sc_ref, sh_ref, gate_ref, w1_ref, w3_ref, w2_ref, lg_ref, lb_ref, o_ref,
                h_scr, acc_scr):
    j = pl.program_id(1)

    @pl.when(j == 0)
    def _():
        h_scr[...] = (x_ref[...] * (1.0 + sc_ref[0]) + sh_ref[0]).astype(BF16)
        acc_scr[...] = jnp.zeros_like(acc_scr)

    for r0 in range(0, FFN_ROWS, FFN_SUB):
        rows = slice(r0, r0 + FFN_SUB)
        acc_scr[rows, :] += _swiglu_partial(h_scr[rows, :], w1_ref[...], w3_ref[...], w2_ref[...])

    @pl.when(j == pl.num_programs(1) - 1)
    def _():
        y = ALPHA * x_ref[...] + gate_ref[0] * acc_scr[...]
        o_ref[...] = _layernorm_rows(y, lg_ref[...], lb_ref[...])


def _ffn(x, sc, sh, gate, w1, w3, w2, lg, lb, rows_per_mod):
    n = x.shape[0]
    tpb = rows_per_mod // FFN_ROWS
    modspec = pl.BlockSpec((1, 1, D_MODEL), lambda i, j: (i // tpb, 0, 0))
    vec = pl.BlockSpec((1, D_MODEL), lambda i, j: (0, 0))
    return pl.pallas_call(
        _ffn_kernel,
        grid=(n // FFN_ROWS, D_FF // FF_TILE),
        in_specs=[pl.BlockSpec((FFN_ROWS, D_MODEL), lambda i, j: (i, 0)), modspec, modspec, modspec,
                  pl.BlockSpec((D_MODEL, FF_TILE), lambda i, j: (0, j)),
                  pl.BlockSpec((D_MODEL, FF_TILE), lambda i, j: (0, j)),
                  pl.BlockSpec((FF_TILE, D_MODEL), lambda i, j: (j, 0)), vec, vec],
        out_specs=pl.BlockSpec((FFN_ROWS, D_MODEL), lambda i, j: (i, 0)),
        out_shape=jax.ShapeDtypeStruct((n, D_MODEL), F32),
        scratch_shapes=[pltpu.VMEM((FFN_ROWS, D_MODEL), BF16), pltpu.VMEM((FFN_ROWS, D_MODEL), F32)],
        compiler_params=_params(("parallel", "arbitrary"), FFN_VMEM_LIMIT),
        name="ffn_dense",
    )(x, sc, sh, gate, w1, w3, w2, lg.reshape(1, D_MODEL), lb.reshape(1, D_MODEL))


def _router_kernel(x_ref, sc_ref, sh_ref, rw_ref, gates_ref, h_ref):
    h = x_ref[...] * (1.0 + sc_ref[0]) + sh_ref[0]
    h_ref[...] = h.astype(BF16)
    logits = jnp.dot(h, rw_ref[...], precision=HIGHEST, preferred_element_type=F32)
    lane = lax.broadcasted_iota(jnp.int32, logits.shape, 1)
    valid = lane < N_EXPERTS
    p = jnp.where(valid, _softmax_rows(jnp.where(valid, logits, -jnp.inf)), -2.0)
    p1 = jnp.max(p, -1, keepdims=True)
    i1 = jnp.min(jnp.where(p == p1, lane, LANES), -1, keepdims=True)
    rest = jnp.where(lane == i1, -1.0, p)
    p2 = jnp.max(rest, -1, keepdims=True)
    i2 = jnp.min(jnp.where(rest == p2, lane, LANES), -1, keepdims=True)
    tot = p1 + p2
    gates_ref[...] = jnp.where(lane == i1, p1 / tot, jnp.where(lane == i2, p2 / tot, 0.0))


def _router(x, sc, sh, router_w, rows_per_mod):
    n = x.shape[0]
    tpb = rows_per_mod // ROW_TILE
    modspec = pl.BlockSpec((1, 1, D_MODEL), lambda i: (i // tpb, 0, 0))
    rw = jnp.pad(router_w, ((0, 0), (0, LANES - N_EXPERTS)))
    return pl.pallas_call(
        _router_kernel,
        grid=(n // ROW_TILE,),
        in_specs=[pl.BlockSpec((ROW_TILE, D_MODEL), lambda i: (i, 0)), modspec, modspec,
                  pl.BlockSpec((D_MODEL, LANES), lambda i: (0, 0))],
        out_specs=[pl.BlockSpec((ROW_TILE, LANES), lambda i: (i, 0)),
                   pl.BlockSpec((ROW_TILE, D_MODEL), lambda i: (i, 0))],
        out_shape=[jax.ShapeDtypeStruct((n, LANES), F32), jax.ShapeDtypeStruct((n, D_MODEL), BF16)],
        compiler_params=_params(("parallel",)),
        name="router",
    )(x, sc, sh, rw)


MOE_ROWS = 1024
MOE_TILE = 128
MOE_MAX_TILES = MOE_ROWS // MOE_TILE
MOE_VMEM_LIMIT = 56 * 1024 * 1024


def _moe_kernel(h_ref, gates_ref, w1_ref, w3_ref, w2_ref, x_ref, gate_ref, lg_ref, lb_ref, o_ref,
                slot_scr, slott_scr, hs_scr, ys_scr):
    e = pl.program_id(1)
    j = pl.program_id(2)
    last_j = pl.num_programs(2) - 1

    @pl.when(jnp.logical_and(e == 0, j == 0))
    def _():
        r = lax.broadcasted_iota(jnp.int32, (MOE_ROWS, MOE_ROWS), 0)
        c = lax.broadcasted_iota(jnp.int32, (MOE_ROWS, MOE_ROWS), 1)
        before = jnp.where(c < r, 1.0, 0.0).astype(BF16)
        mask = gates_ref[...] != 0.0
        rank = jnp.dot(before, jnp.where(mask, 1.0, 0.0).astype(BF16), preferred_element_type=F32)
        slot = jnp.where(mask, rank, -1.0).astype(jnp.int32)
        slot_scr[...] = slot
        slott_scr[...] = slot.T
        o_ref[...] = jnp.zeros_like(o_ref)

    lane = lax.broadcasted_iota(jnp.int32, (MOE_ROWS, LANES), 1)
    slot_col = jnp.max(jnp.where(lane == e, slot_scr[...], -1), -1, keepdims=True)
    n_tiles = (jnp.max(slot_col) + MOE_TILE) // MOE_TILE

    def tile(k, carry):
        @pl.when(j == 0)
        def _():
            slot_row = slott_scr[pl.ds(e, 1), :]
            rr = lax.broadcasted_iota(jnp.int32, (MOE_TILE, MOE_ROWS), 0) + k * MOE_TILE
            pick = jnp.where(rr == slot_row, 1.0, 0.0).astype(BF16)
            hs_scr[k] = jnp.dot(pick, h_ref[...], preferred_element_type=F32).astype(BF16)
            ys_scr[k] = jnp.zeros((MOE_TILE, D_MODEL), F32)

        ys_scr[k] += _swiglu_partial(hs_scr[k], w1_ref[0], w3_ref[0], w2_ref[0])

        @pl.when(j == last_j)
        def _():
            g_col = jnp.sum(jnp.where(lane == e, gates_ref[...], 0.0), -1, keepdims=True)
            cc = lax.broadcasted_iota(jnp.int32, (MOE_ROWS, 2 * MOE_TILE), 1)
            cc = jnp.where(cc >= MOE_TILE, cc - MOE_TILE, cc) + k * MOE_TILE
            put = jnp.where(slot_col == cc, 1.0, 0.0).astype(BF16)
            y = ys_scr[k]
            y_hi = y.astype(BF16)
            y_lo = (y - y_hi.astype(F32)).astype(BF16)
            back = jnp.dot(put, jnp.concatenate([y_hi, y_lo], axis=0), preferred_element_type=F32)
            o_ref[...] += g_col * back

        return carry

    lax.fori_loop(0, n_tiles, tile, 0)

    @pl.when(jnp.logical_and(e == pl.num_programs(1) - 1, j == last_j))
    def _():
        y = ALPHA * x_ref[...] + gate_ref[0] * o_ref[...]
        o_ref[...] = _layernorm_rows(y, lg_ref[...], lb_ref[...])


def _moe(h, gates, w1, w3, w2, x, gate, lg, lb, rows_per_mod):
    n = h.shape[0]
    tpb = rows_per_mod // MOE_ROWS
    vec = pl.BlockSpec((1, D_MODEL), lambda i, e, j: (0, 0))
    return pl.pallas_call(
        _moe_kernel,
        grid=(n // MOE_ROWS, N_EXPERTS, D_FF // FF_TILE),
        in_specs=[pl.BlockSpec((MOE_ROWS, D_MODEL), lambda i, e, j: (i, 0)),
                  pl.BlockSpec((MOE_ROWS, LANES), lambda i, e, j: (i, 0)),
                  pl.BlockSpec((1, D_MODEL, FF_TILE), lambda i, e, j: (e, 0, j)),
                  pl.BlockSpec((1, D_MODEL, FF_TILE), lambda i, e, j: (e, 0, j)),
                  pl.BlockSpec((1, FF_TILE, D_MODEL), lambda i, e, j: (e, j, 0)),
                  pl.BlockSpec((MOE_ROWS, D_MODEL), lambda i, e, j: (i, 0)),
                  pl.BlockSpec((1, 1, D_MODEL), lambda i, e, j: (i // tpb, 0, 0)), vec, vec],
        out_specs=pl.BlockSpec((MOE_ROWS, D_MODEL), lambda i, e, j: (i, 0)),
        out_shape=jax.ShapeDtypeStruct((n, D_MODEL), F32),
        scratch_shapes=[pltpu.VMEM((MOE_ROWS, LANES), jnp.int32), pltpu.VMEM((LANES, MOE_ROWS), jnp.int32),
                        pltpu.VMEM((MOE_MAX_TILES, MOE_TILE, D_MODEL), BF16),
                        pltpu.VMEM((MOE_MAX_TILES, MOE_TILE, D_MODEL), F32)],
        compiler_params=_params(("parallel", "arbitrary", "arbitrary"), MOE_VMEM_LIMIT),
        name="moe",
    )(h, gates, w1, w3, w2, x, gate, lg.reshape(1, D_MODEL), lb.reshape(1, D_MODEL))


def _permute_w_in(w):
    pad = jnp.zeros((D_MODEL, U_COLS - ORIG_END), w.dtype)
    return jnp.concatenate([w[:, :ORIG_GATES], w[:, ORIG_SX:ORIG_DT], w[:, ORIG_SZ:ORIG_SX],
                            w[:, ORIG_GATES:ORIG_SZ], w[:, ORIG_DT:ORIG_END], pad], axis=1).astype(BF16)


def _small_row(vals, offset):
    v = vals.reshape(-1).astype(F32)
    return jnp.zeros((1, LANES), F32).at[0, offset:offset + v.shape[0]].set(v)


def _pack_mlstm_state(c, n, m):
    shape = c.shape[:2] + (D_STATE, HEAD_LANES)
    c_rows = jnp.swapaxes(c, 2, 3).reshape(shape)
    n_rows = jnp.broadcast_to(jnp.swapaxes(n, 2, 3)[..., None], c.shape[:2] + (D_STATE, N_HEADS, D_STATE))
    return (jnp.concatenate([c_rows, n_rows.reshape(shape)], axis=-1),
            jnp.repeat(m, D_STATE, axis=-1)[:, :, None, :])


def _pack_ssd_state(s):
    sel = (jnp.arange(N_GROUPS)[:, None] == jnp.arange(N_HEADS)[None, :] // 2).astype(F32)
    return jnp.einsum('bdhpn,gh->bdgnhp', s, sel).reshape(s.shape[:2] + (LANES, HEAD_LANES))


def _layer(x, mods, P, l, bsz, t, ctx, caches=None):
    sh1, sc1, g1, sh2, sc2, g2 = mods
    rows_per_mod = x.shape[0] // sh1.shape[0]
    lam_init = 0.8 - 0.6 * math.exp(-0.3 * l)
    u = _in_proj(x, sc1, sh1, P['w_in'][l], rows_per_mod)

    gate_bias = (_small_row(P['mlstm_gate_b'][l, 0], GATE_I) + _small_row(P['mlstm_gate_b'][l, 1], GATE_F))
    dt_bias = _small_row(P['ssm_dt_bias'][l], GATE_DT)
    alog = _small_row(P['ssm_A_log'][l], GATE_DT)
    dskip = jnp.repeat(P['ssm_D'][l].astype(F32), D_STATE).reshape(1, D_REC)

    if ctx is None:
        att, k_new, v_new = _attention_ctx(u, P['attn_lambda'][l], P['attn_norm_w'][l], lam_init, bsz, t, l,
                                           None if caches is None else caches[0:2])
        m_init = s_init = None
    else:
        ck, cv, c_c, c_n, c_m, c_s = ctx
        q, k, v = _rope_prep(u, bsz, t)
        k_all = jnp.concatenate([k, ck.astype(BF16)], axis=3)
        v_all = jnp.concatenate([v, cv.astype(BF16)], axis=2)
        att = _attention_lat(q, k_all, v_all, P['attn_lambda'][l], P['attn_norm_w'][l], lam_init, bsz, t)
        m_init = _pack_mlstm_state(c_c, c_n, c_m)
        s_init = _pack_ssd_state(c_s)
    hf, hb, *mlstm_caches = _mlstm(u, gate_bias, bsz, t, m_init, l, None if caches is None else caches[2:5])
    xbc = _ssd_conv(u, P['conv_w'][l], P['conv_b'][l], bsz, t)
    yf, yb, *ssd_caches = _ssd(xbc, u, dt_bias, alog, dskip, bsz, t, s_init, l,
                               None if caches is None else caches[5])

    x = _out_proj(att, hf, hb, yf, yb, u, P['w_out'][l], x, g1, P['mlstm_norm_w'][l], P['ssm_norm_w'][l],
                  P['ln_g'][l, 0], P['ln_b'][l, 0], rows_per_mod)
    if l % 2 == 0:
        x = _ffn(x, sc2, sh2, g2, P['ffn_w1'][l // 2], P['ffn_w3'][l // 2], P['ffn_w2'][l // 2],
                 P['ln_g'][l, 1], P['ln_b'][l, 1], rows_per_mod)
    else:
        gates, h2 = _router(x, sc2, sh2, P['router_w'][l // 2], rows_per_mod)
        x = _moe(h2, gates, P['moe_w1'][l // 2], P['moe_w3'][l // 2], P['moe_w2'][l // 2],
                 x, g2, P['ln_g'][l, 1], P['ln_b'][l, 1], rows_per_mod)
    if ctx is None:
        return x, (k_new, v_new, *mlstm_caches, *ssd_caches)
    return x, None


def kernel(x_prompt, x_sample, c, cache_attn_k, cache_attn_v, state_mlstm_C, state_mlstm_n, state_mlstm_m, state_ssm, c_ctx, w_ada, b_ada, w_in, w_out, attn_lambda, attn_norm_w, mlstm_gate_b, mlstm_norm_w, conv_w, conv_b, ssm_A_log, ssm_dt_bias, ssm_D, ssm_norm_w, ln_g, ln_b, ffn_w1, ffn_w3, ffn_w2, router_w, moe_w1, moe_w3, moe_w2):
    bsz, seq, _ = x_prompt.shape
    dbsz, dseq, _ = x_sample.shape
    P = dict(w_in=[_permute_w_in(w_in[l]) for l in range(DEPTH)], w_out=w_out.astype(BF16),
             attn_lambda=attn_lambda, attn_norm_w=attn_norm_w, mlstm_gate_b=mlstm_gate_b,
             mlstm_norm_w=mlstm_norm_w, conv_w=conv_w, conv_b=conv_b, ssm_A_log=ssm_A_log,
             ssm_dt_bias=ssm_dt_bias, ssm_D=ssm_D, ssm_norm_w=ssm_norm_w, ln_g=ln_g, ln_b=ln_b,
             ffn_w1=ffn_w1.astype(BF16), ffn_w3=ffn_w3.astype(BF16), ffn_w2=ffn_w2.astype(BF16),
             router_w=router_w, moe_w1=moe_w1.astype(BF16), moe_w3=moe_w3.astype(BF16),
             moe_w2=moe_w2.astype(BF16))

    cvec = jnp.zeros((8, D_MODEL), F32).at[0].set(c_ctx).at[1:1 + dbsz].set(c)
    mod = _modulation(cvec, w_ada, b_ada)

    def mods_for(l, lo, hi):
        return [mod[l, lo:hi, i * D_MODEL:(i + 1) * D_MODEL][:, None, :] for i in range(6)]

    y_prompt = x_prompt.reshape(bsz * seq, D_MODEL)
    caches = None
    for l in range(DEPTH):
        y_prompt, caches = _layer(y_prompt, mods_for(l, 0, 1), P, l, bsz, seq, None, caches)
    new_k, new_v, new_c, new_n, m_spread, new_s = caches

    y_sample = x_sample.reshape(dbsz * dseq, D_MODEL)
    for l in range(DEPTH):
        ctx = (cache_attn_k[:, l], cache_attn_v[:, l], state_mlstm_C[:, l], state_mlstm_n[:, l],
               state_mlstm_m[:, l], state_ssm[:, l])
        y_sample, _ = _layer(y_sample, mods_for(l, 1, 1 + dbsz), P, l, dbsz, dseq, ctx)

    return (y_prompt.reshape(bsz, seq, D_MODEL), y_sample.reshape(dbsz, dseq, D_MODEL),
            new_k, new_v, new_c, new_n, m_spread[:, :, :, 0, ::D_STATE], new_s)
```

```python
import functools
import math

import jax
import jax.numpy as jnp
from jax import lax
from jax.experimental import pallas as pl
from jax.experimental.pallas import tpu as pltpu

F32 = jnp.float32
BF16 = jnp.bfloat16
HIGHEST = lax.Precision.HIGHEST

D_MODEL = 1024
DEPTH = 2
GRID_W = 64
N_HEADS = 4
D_ATT = 512
D_HEAD_V = 128
D_QK = 64
D_REC = 256
D_STATE = 64
N_GROUPS = 2
D_CONV = 3
D_FF = 2816
N_EXPERTS = 8
ALPHA = (2.0 * DEPTH) ** 0.25
CHUNK = 64
ROPE_BASE = 10000.0
LOG2E = 1.4426950408889634
EPS = 1e-5

COL_AQ, COL_AK, COL_AV = 0, 512, 1024
COL_MQ, COL_MK, COL_MV, COL_MO = 1536, 1792, 2048, 2304
COL_SX, COL_SBC, COL_SZ = 2560, 2816, 3072
COL_SMALL = 3328
U_COLS = 3584
ORIG_GATES, ORIG_SZ, ORIG_SX, ORIG_DT, ORIG_END = 2560, 2576, 2832, 3344, 3352
GATE_I, GATE_F, GATE_DT = 0, 8, 16

LANES = 128
ROW_TILE = 512
SCAN_ROWS = 256
VMEM_LIMIT = 48 * 1024 * 1024

NT_DIMS = (((1,), (1,)), ((), ()))
TN_DIMS = (((0,), (0,)), ((), ()))


def _params(sem, vmem=VMEM_LIMIT):
    return pltpu.CompilerParams(dimension_semantics=sem, vmem_limit_bytes=vmem)


def _silu(x):
    return x * jax.nn.sigmoid(x)


def _bdot(a, b):
    return jnp.dot(a.astype(BF16), b.astype(BF16), preferred_element_type=F32)


def _bdot_nt(a, b):
    return lax.dot_general(a.astype(BF16), b.astype(BF16), NT_DIMS, preferred_element_type=F32)


def _bdot_tn(a, b):
    return lax.dot_general(a.astype(BF16), b.astype(BF16), TN_DIMS, preferred_element_type=F32)


def _layernorm_rows(y, g, b):
    mu = jnp.mean(y, -1, keepdims=True)
    d = y - mu
    var = jnp.mean(d * d, -1, keepdims=True)
    return d * lax.rsqrt(var + EPS) * g + b


def _mod_kernel(c_ref, w_ref, b_ref, o_ref):
    o_ref[0] = jnp.dot(_silu(c_ref[...]), w_ref[0], precision=HIGHEST,
                       preferred_element_type=F32) + b_ref[0]


def _modulation(cvec, w_ada, b_ada):
    tn = 1536
    return pl.pallas_call(
        _mod_kernel,
        grid=(DEPTH, 6 * D_MODEL // tn),
        in_specs=[pl.BlockSpec((8, D_MODEL), lambda l, j: (0, 0)),
                  pl.BlockSpec((1, D_MODEL, tn), lambda l, j: (l, 0, j)),
                  pl.BlockSpec((1, 1, tn), lambda l, j: (l, 0, j))],
        out_specs=pl.BlockSpec((1, 8, tn), lambda l, j: (l, 0, j)),
        out_shape=jax.ShapeDtypeStruct((DEPTH, 8, 6 * D_MODEL), F32),
        compiler_params=_params(("parallel", "parallel")),
        name="modulation",
    )(cvec, w_ada, b_ada.reshape(DEPTH, 1, 6 * D_MODEL))


def _inproj_kernel(x_ref, sc_ref, sh_ref, w_ref, o_ref):
    h = (x_ref[...] * (1.0 + sc_ref[0]) + sh_ref[0]).astype(BF16)
    for n0 in range(0, U_COLS, 512):
        o_ref[:, n0:n0 + 512] = jnp.dot(h, w_ref[:, n0:n0 + 512], preferred_element_type=F32)


def _in_proj(x, sc, sh, w, rows_per_mod):
    n = x.shape[0]
    tpb = rows_per_mod // ROW_TILE
    return pl.pallas_call(
        _inproj_kernel,
        grid=(n // ROW_TILE,),
        in_specs=[pl.BlockSpec((ROW_TILE, D_MODEL), lambda i: (i, 0)),
                  pl.BlockSpec((1, 1, D_MODEL), lambda i: (i // tpb, 0, 0)),
                  pl.BlockSpec((1, 1, D_MODEL), lambda i: (i // tpb, 0, 0)),
                  pl.BlockSpec((D_MODEL, U_COLS), lambda i: (0, 0))],
        out_specs=pl.BlockSpec((ROW_TILE, U_COLS), lambda i: (i, 0)),
        out_shape=jax.ShapeDtypeStruct((n, U_COLS), F32),
        compiler_params=_params(("parallel",)),
        name="in_proj",
    )(x, sc, sh, w)


def _lambda_scalar(lam_ref, lam_init):
    lp = lam_ref[...]
    s01 = jnp.sum(lp[0:1] * lp[1:2], axis=-1, keepdims=True)
    s23 = jnp.sum(lp[2:3] * lp[3:4], axis=-1, keepdims=True)
    return jnp.exp(s01) - jnp.exp(s23) + lam_init


def _softmax_rows(s):
    e = jnp.exp(s - jnp.max(s, -1, keepdims=True))
    return e / jnp.sum(e, -1, keepdims=True)


def _head_norm(o, nw, lam_init):
    return o * lax.rsqrt(jnp.mean(o * o, -1, keepdims=True) + EPS) * nw * (1.0 - lam_init)


def _layer_cache(tail, l, bsz):
    zeros = (0,) * len(tail)
    owned = DEPTH if l == 0 else 1
    spec = pl.BlockSpec((1, owned) + tuple(tail), lambda b, *_: (b, l) + zeros)
    return jax.ShapeDtypeStruct((bsz, DEPTH) + tuple(tail), F32), spec, owned


def _zero_later_layers(ref):
    for later in range(1, ref.shape[1]):
        ref[0, later] = jnp.zeros(ref.shape[2:], ref.dtype)


def _attn_ctx_kernel(u_ref, lam_ref, nw_ref, *rest, lam_init):
    att_ref, k_ref, v_ref = rest[-3:]
    lam = _lambda_scalar(lam_ref, lam_init)
    for h in range(N_HEADS):
        v = u_ref[:, COL_AV + h * D_HEAD_V:COL_AV + (h + 1) * D_HEAD_V]
        v_ref[0, 0, h] = v
        ps = []
        for m in range(2):
            c0 = h * D_HEAD_V + m * D_QK
            q = u_ref[:, COL_AQ + c0:COL_AQ + c0 + D_QK] * (D_QK ** -0.5)
            k = u_ref[:, COL_AK + c0:COL_AK + c0 + D_QK]
            k_ref[0, 0, h, m] = k
            ps.append(_softmax_rows(_bdot_nt(q, k)))
        o = _bdot(ps[0] - lam * ps[1], v)
        att_ref[:, h * D_HEAD_V:(h + 1) * D_HEAD_V] = _head_norm(o, nw_ref[...], lam_init).astype(BF16)
    _zero_later_layers(k_ref)
    _zero_later_layers(v_ref)


def _attention_ctx(u, lam_p, norm_w, lam_init, bsz, t, l, caches):
    n = bsz * t
    k_shape, k_spec, _ = _layer_cache((N_HEADS, 2, t, D_QK), l, bsz)
    v_shape, v_spec, _ = _layer_cache((N_HEADS, t, D_HEAD_V), l, bsz)
    in_specs = [pl.BlockSpec((t, 3 * D_ATT), lambda b: (b, 0)),
                pl.BlockSpec((4, D_QK), lambda b: (0, 0)),
                pl.BlockSpec((1, D_HEAD_V), lambda b: (0, 0))]
    args = [u, lam_p, norm_w.reshape(1, D_HEAD_V)]
    aliases = {}
    if caches is not None:
        in_specs += [pl.BlockSpec(memory_space=pl.ANY)] * 2
        aliases = {len(args): 1, len(args) + 1: 2}
        args += list(caches)
    return pl.pallas_call(
        functools.partial(_attn_ctx_kernel, lam_init=lam_init),
        grid=(bsz,),
        in_specs=in_specs,
        out_specs=[pl.BlockSpec((t, D_ATT), lambda b: (b, 0)), k_spec, v_spec],
        out_shape=[jax.ShapeDtypeStruct((n, D_ATT), BF16), k_shape, v_shape],
        input_output_aliases=aliases,
        compiler_params=_params(("parallel",)),
        name="attn_ctx",
    )(*args)


def _rope_kernel(u_ref, cos_ref, sa_ref, sb_ref, q_ref, k_ref, v_ref):
    cos, sa, sb = cos_ref[...], sa_ref[...], sb_ref[...]

    def rope(x):
        return x * cos + pltpu.roll(x, LANES - 16, 1) * sa + pltpu.roll(x, 16, 1) * sb

    for h in range(N_HEADS):
        q = rope(u_ref[:, COL_AQ + h * D_HEAD_V:COL_AQ + (h + 1) * D_HEAD_V]) * (LOG2E * D_QK ** -0.5)
        k = rope(u_ref[:, COL_AK + h * D_HEAD_V:COL_AK + (h + 1) * D_HEAD_V])
        for m in range(2):
            q_ref[0, h, m] = q[:, m * D_QK:(m + 1) * D_QK].astype(BF16)
            k_ref[0, h, m] = k[:, m * D_QK:(m + 1) * D_QK].astype(BF16)
        v_ref[0, h] = u_ref[:, COL_AV + h * D_HEAD_V:COL_AV + (h + 1) * D_HEAD_V].astype(BF16)


def _rope_tables(t):
    rows = jnp.repeat(jnp.arange(t // GRID_W, dtype=F32), GRID_W)
    cols = jnp.tile(jnp.arange(GRID_W, dtype=F32), t // GRID_W)
    half = D_QK // 2
    inv = ROPE_BASE ** (-jnp.arange(0, half, 2, dtype=F32) / half)
    ang_r = rows[:, None] * inv
    ang_c = cols[:, None] * inv
    ang = jnp.concatenate([ang_r, ang_r, ang_c, ang_c], -1)
    cos, sin = jnp.cos(ang), jnp.sin(ang)
    quarter = (jnp.arange(D_QK) // (D_QK // 4)) % 2
    sa = jnp.where(quarter == 0, -sin, 0.0)
    sb = jnp.where(quarter == 1, sin, 0.0)
    tile2 = lambda a: jnp.concatenate([a, a], -1)
    return tile2(cos), tile2(sa), tile2(sb)


def _rope_prep(u, bsz, t):
    tr = 512
    nb = t // tr
    cos, sa, sb = _rope_tables(t)
    tab = pl.BlockSpec((tr, LANES), lambda b, i: (i, 0))
    return pl.pallas_call(
        _rope_kernel,
        grid=(bsz, nb),
        in_specs=[pl.BlockSpec((tr, 3 * D_ATT), lambda b, i: (b * nb + i, 0)), tab, tab, tab],
        out_specs=[pl.BlockSpec((1, N_HEADS, 2, tr, D_QK), lambda b, i: (b, 0, 0, i, 0)),
                   pl.BlockSpec((1, N_HEADS, 2, tr, D_QK), lambda b, i: (b, 0, 0, i, 0)),
                   pl.BlockSpec((1, N_HEADS, tr, D_HEAD_V), lambda b, i: (b, 0, i, 0))],
        out_shape=[jax.ShapeDtypeStruct((bsz, N_HEADS, 2, t, D_QK), BF16),
                   jax.ShapeDtypeStruct((bsz, N_HEADS, 2, t, D_QK), BF16),
                   jax.ShapeDtypeStruct((bsz, N_HEADS, t, D_HEAD_V), BF16)],
        compiler_params=_params(("parallel", "parallel")),
        name="rope_prep",
    )(u, cos, sa, sb)


def _attn_lat_kernel(q_ref, k_ref, v_ref, lam_ref, nw_ref, o_ref, *, lam_init):
    lam = _lambda_scalar(lam_ref, lam_init)
    es, sums = [], []
    for m in range(2):
        s = lax.dot_general(q_ref[0, 0, m], k_ref[0, 0, m], NT_DIMS, preferred_element_type=F32)
        e = jnp.exp2(s - jnp.max(s, -1, keepdims=True))
        es.append(e)
        sums.append(jnp.sum(e, -1, keepdims=True))
    a = es[0] - (lam * sums[0] / sums[1]) * es[1]
    o = _bdot(a, v_ref[0, 0]) / sums[0]
    o_ref[...] = _head_norm(o, nw_ref[...], lam_init).astype(BF16)


def _attention_lat(q, k_all, v_all, lam_p, norm_w, lam_init, bsz, t):
    tq = 256
    nq = t // tq
    s = k_all.shape[3]
    return pl.pallas_call(
        functools.partial(_attn_lat_kernel, lam_init=lam_init),
        grid=(bsz, N_HEADS, nq),
        in_specs=[pl.BlockSpec((1, 1, 2, tq, D_QK), lambda b, h, i: (b, h, 0, i, 0)),
                  pl.BlockSpec((1, 1, 2, s, D_QK), lambda b, h, i: (b, h, 0, 0, 0)),
                  pl.BlockSpec((1, 1, s, D_HEAD_V), lambda b, h, i: (b, h, 0, 0)),
                  pl.BlockSpec((4, D_QK), lambda b, h, i: (0, 0)),
                  pl.BlockSpec((1, D_HEAD_V), lambda b, h, i: (0, 0))],
        out_specs=pl.BlockSpec((tq, D_HEAD_V), lambda b, h, i: (b * nq + i, h)),
        out_shape=jax.ShapeDtypeStruct((bsz * t, D_ATT), BF16),
        compiler_params=_params(("parallel", "parallel", "parallel")),
        name="attn_lat",
    )(q, k_all, v_all, lam_p, norm_w.reshape(1, D_HEAD_V))


HEAD_LANES = N_HEADS * D_STATE


def _scan_consts():
    t = lax.broadcasted_iota(jnp.int32, (CHUNK, HEAD_LANES), 0)
    s = lax.broadcasted_iota(jnp.int32, (CHUNK, HEAD_LANES), 1) & (CHUNK - 1)
    r = lax.broadcasted_iota(jnp.int32, (CHUNK, CHUNK), 0)
    c = lax.broadcasted_iota(jnp.int32, (CHUNK, CHUNK), 1)
    reads = (jnp.where(s <= t, 1.0, 0.0), jnp.where(s >= t, 1.0, 0.0))
    block = (jnp.where(s <= t, 0.0, -jnp.inf), jnp.where(s >= t, 0.0, -jnp.inf))
    tri = (jnp.where(c <= r, 1.0, 0.0).astype(BF16), jnp.where(c >= r, 1.0, 0.0).astype(BF16))
    return reads, block, tri, jnp.where(s == t, 1.0, 0.0)


def _cumsum_rows(tri, x):
    hi = x.astype(BF16)
    rest = x - hi.astype(F32)
    mid = rest.astype(BF16)
    lo = (rest - mid.astype(F32)).astype(BF16)
    parts = jnp.dot(tri, jnp.concatenate([hi, mid, lo], axis=1), preferred_element_type=F32)
    return parts[:, :HEAD_LANES] + parts[:, HEAD_LANES:2 * HEAD_LANES] + parts[:, 2 * HEAD_LANES:]


def _group_mask(rows, cols, row_shift, col_shift, dtype):
    r = lax.broadcasted_iota(jnp.int32, (rows, cols), 0) >> 6
    c = (lax.broadcasted_iota(jnp.int32, (rows, cols), 1) >> 6) & (N_HEADS - 1)
    return jnp.where((r >> row_shift) == (c >> col_shift), 1.0, 0.0).astype(dtype)


def _spread(x, chans):
    return jnp.concatenate([jnp.broadcast_to(x[:, c:c + 1], (CHUNK, D_STATE)) for c in chans], axis=1)


def _stack_heads(x):
    return jnp.concatenate([x] * N_HEADS, axis=0)


def _mlstm_chunk(q4, k4, v4, g, d, cn_prev, m_prev, reads, block, tri, eye, bd, bd_f32):
    li = _spread(g, [GATE_I + d * N_HEADS + h for h in range(N_HEADS)])
    lf = _spread(jax.nn.log_sigmoid(g), [GATE_F + d * N_HEADS + h for h in range(N_HEADS)])
    bc = _cumsum_rows(tri[d], lf)
    btot = jnp.sum(lf, 0, keepdims=True)
    b_row = jnp.sum(reads[1 - d] * lf, 0, keepdims=True)
    li_row = jnp.sum(eye * li, 0, keepdims=True)
    dm = bc - b_row + li_row + block[d]
    rmax = jnp.concatenate(
        [jnp.broadcast_to(jnp.max(dm[:, h * D_STATE:(h + 1) * D_STATE], -1, keepdims=True), (CHUNK, D_STATE))
         for h in range(N_HEADS)], axis=1)
    inter = bc + m_prev
    m_t = jnp.maximum(inter, rmax)
    w_inter = jnp.exp(inter - m_t)
    qs = (q4 * (D_STATE ** -0.5)).astype(BF16)
    kbd = bd[:, :HEAD_LANES] * _stack_heads(k4.astype(BF16))
    s4 = lax.dot_general(qs, kbd, NT_DIMS, preferred_element_type=F32) * jnp.exp(dm - m_t)
    vo = jnp.concatenate([v4.astype(BF16), jnp.ones((CHUNK, HEAD_LANES), BF16)], axis=1)
    vbd = bd * _stack_heads(vo)
    nd = (jnp.concatenate([w_inter, w_inter], axis=1)
          * jnp.dot(qs, bd * _stack_heads(cn_prev.astype(BF16)), preferred_element_type=F32)
          + jnp.dot(s4.astype(BF16), vbd, preferred_element_type=F32))
    hc = nd[:, :HEAD_LANES] / jnp.maximum(jnp.abs(nd[:, HEAD_LANES:]), jnp.exp(-m_t))
    gcol = btot - bc + li
    m_new = jnp.maximum(btot + m_prev, jnp.max(gcol, 0, keepdims=True))
    w_c = jnp.exp(btot + m_prev - m_new)
    kw = (k4 * jnp.exp(gcol - m_new)).astype(BF16)
    dcn = lax.dot_general(kw, vo, TN_DIMS, preferred_element_type=F32)
    own = sum(bd_f32[h * D_STATE:(h + 1) * D_STATE] * dcn[h * D_STATE:(h + 1) * D_STATE]
              for h in range(N_HEADS))
    cn_new = jnp.concatenate([w_c, w_c], axis=1) * cn_prev + own
    return hc, cn_new, m_new


def _mlstm_kernel(*refs, nblk, rows, zero_init, n_alias):
    qf_ref, kf_ref, vf_ref, gf_ref, qb_ref, kb_ref, vb_ref, gb_ref, bias_ref = refs[:9]
    if not zero_init:
        c0_ref, m0_ref = refs[9:11]
    outs = refs[9 + (0 if zero_init else 2) + n_alias:-2]
    hf_ref, hb_ref = outs[:2]
    cn_scr, m_scr = refs[-2:]
    j = pl.program_id(1)
    nchunk = rows // CHUNK

    @pl.when(j == 0)
    def _():
        if zero_init:
            cn_scr[...] = jnp.zeros_like(cn_scr)
            m_scr[...] = jnp.zeros_like(m_scr)
        else:
            cn_scr[...] = c0_ref[0]
            m_scr[...] = m0_ref[0]

    reads, block, tri, eye = _scan_consts()
    bd = _group_mask(HEAD_LANES, 2 * HEAD_LANES, 0, 0, BF16)
    bd_f32 = _group_mask(HEAD_LANES, 2 * HEAD_LANES, 0, 0, F32)

    state = [(cn_scr[d], m_scr[d]) for d in range(2)]
    for ci in range(nchunk):
        for d in range(2):
            q_ref, k_ref, v_ref, g_ref, h_ref = ((qf_ref, kf_ref, vf_ref, gf_ref, hf_ref) if d == 0
                                                 else (qb_ref, kb_ref, vb_ref, gb_ref, hb_ref))
            cj = ci if d == 0 else nchunk - 1 - ci
            rs = slice(cj * CHUNK, (cj + 1) * CHUNK)
            g = g_ref[rs, :] + bias_ref[...]
            hc, cn_new, m_new = _mlstm_chunk(q_ref[rs, :], k_ref[rs, :], v_ref[rs, :], g, d,
                                             *state[d], reads, block, tri, eye, bd, bd_f32)
            h_ref[rs, :] = hc
            state[d] = (cn_new, m_new)
    for d in range(2):
        cn_scr[d], m_scr[d] = state[d]

    if zero_init:
        cout_ref, nout_ref, mout_ref = outs[2:]

        @pl.when(j == nblk - 1)
        def _():
            r = lax.broadcasted_iota(jnp.int32, (D_STATE, D_STATE), 0)
            c = lax.broadcasted_iota(jnp.int32, (D_STATE, D_STATE), 1)
            for d in range(2):
                for h in range(N_HEADS):
                    r0, r1 = h * D_STATE, (h + 1) * D_STATE
                    cout_ref[0, 0, d, h] = cn_scr[d, :, r0:r1]
                    n_spread = cn_scr[d, :, HEAD_LANES + r0:HEAD_LANES + r1]
                    nout_ref[0, 0, d, h:h + 1, :] = jnp.sum(jnp.where(r == c, n_spread, 0.0), 0, keepdims=True)
            mout_ref[0, 0] = m_scr[...]
            _zero_later_layers(cout_ref)
            _zero_later_layers(nout_ref)
            _zero_later_layers(mout_ref)


def _mlstm(u, gate_bias, bsz, t, init, l=0, caches=None):
    rows = min(t, SCAN_ROWS)
    nblk = t // rows
    zero_init = init is None

    def fwd(col):
        return lambda b, j: (b * nblk + j, col)

    def bwd(col):
        return lambda b, j: (b * nblk + nblk - 1 - j, col)

    cq, ck, cv, cg = COL_MQ // D_REC, COL_MK // D_REC, COL_MV // D_REC, COL_SMALL // LANES
    in_specs = []
    for mk in (fwd, bwd):
        in_specs += [pl.BlockSpec((rows, D_REC), mk(cq)), pl.BlockSpec((rows, D_REC), mk(ck)),
                     pl.BlockSpec((rows, D_REC), mk(cv)), pl.BlockSpec((rows, LANES), mk(cg))]
    in_specs.append(pl.BlockSpec((1, LANES), lambda b, j: (0, 0)))
    args = [u] * 8 + [gate_bias]
    state_c = pl.BlockSpec((1, 2, D_STATE, 2 * HEAD_LANES), lambda b, j: (b, 0, 0, 0))
    state_m = pl.BlockSpec((1, 2, 1, HEAD_LANES), lambda b, j: (b, 0, 0, 0))
    out_specs = [pl.BlockSpec((rows, D_REC), fwd(0)), pl.BlockSpec((rows, D_REC), bwd(0))]
    out_shape = [jax.ShapeDtypeStruct((bsz * t, D_REC), F32), jax.ShapeDtypeStruct((bsz * t, D_REC), F32)]
    aliases = {}
    if zero_init:
        for tail in ((2, N_HEADS, D_STATE, D_STATE), (2, N_HEADS, D_STATE), (2, 1, HEAD_LANES)):
            shape, spec, _ = _layer_cache(tail, l, bsz)
            out_shape.append(shape)
            out_specs.append(spec)
        if caches is not None:
            in_specs += [pl.BlockSpec(memory_space=pl.ANY)] * len(caches)
            aliases = {len(args) + i: 2 + i for i in range(len(caches))}
            args += list(caches)
    else:
        in_specs += [state_c, state_m]
        args += list(init)
    return pl.pallas_call(
        functools.partial(_mlstm_kernel, nblk=nblk, rows=rows, zero_init=zero_init, n_alias=len(aliases)),
        grid=(bsz, nblk),
        in_specs=in_specs,
        out_specs=out_specs,
        out_shape=out_shape,
        input_output_aliases=aliases,
        scratch_shapes=[pltpu.VMEM((2, D_STATE, 2 * HEAD_LANES), F32),
                        pltpu.VMEM((2, 1, HEAD_LANES), F32)],
        compiler_params=_params(("parallel", "arbitrary")),
        name="mlstm_scan",
    )(*args)


def _conv_kernel(x_ref, prev_ref, next_ref, w_ref, b_ref, o_ref, *, nblk):
    i = pl.program_id(1)
    x = x_ref[...]
    rows = x.shape[0]
    r = lax.broadcasted_iota(jnp.int32, x.shape, 0)
    prev_row = prev_ref[7:8, :] * (i > 0).astype(F32)
    next_row = next_ref[0:1, :] * (i < nblk - 1).astype(F32)
    xm = jnp.where(r == 0, prev_row, pltpu.roll(x, 1, 0))
    xp = jnp.where(r == rows - 1, next_row, pltpu.roll(x, rows - 1, 0))
    w = w_ref[...]
    o_ref[...] = _silu(xm * w[0:1] + x * w[1:2] + xp * w[2:3] + b_ref[...])


def _ssd_conv(u, conv_w, conv_b, bsz, t):
    rows = min(t, 512)
    nblk = t // rows
    r8 = rows // 8
    width = 2 * D_REC
    c0 = COL_SX // width
    return pl.pallas_call(
        functools.partial(_conv_kernel, nblk=nblk),
        grid=(bsz, nblk),
        in_specs=[pl.BlockSpec((rows, width), lambda b, i: (b * nblk + i, c0)),
                  pl.BlockSpec((8, width), lambda b, i: (jnp.maximum((b * nblk + i) * r8 - 1, 0), c0)),
                  pl.BlockSpec((8, width), lambda b, i: (jnp.minimum((b * nblk + i + 1) * r8,
                                                                     bsz * nblk * r8 - 1), c0)),
                  pl.BlockSpec((D_CONV, width), lambda b, i: (0, 0)),
                  pl.BlockSpec((1, width), lambda b, i: (0, 0))],
        out_specs=pl.BlockSpec((rows, width), lambda b, i: (b * nblk + i, 0)),
        out_shape=jax.ShapeDtypeStruct((bsz * t, width), F32),
        compiler_params=_params(("parallel", "parallel")),
        name="ssd_conv",
    )(u, u, u, conv_w, conv_b.reshape(1, width))


def _ssd_chunk(x4, bcm, dt128, da128, d, sg_prev, reads, block, tri, b_sel, s_sel, bd):
    chans = [GATE_DT + d * N_HEADS + h for h in range(N_HEADS)]
    dt = _spread(dt128, chans)
    da = _spread(da128, chans)
    ac = _cumsum_rows(tri[d], da)
    atot = jnp.sum(da, 0, keepdims=True)
    a_row = jnp.sum(reads[1 - d] * da, 0, keepdims=True)
    decay = jnp.exp(ac - a_row + block[d])
    bmat = bcm[:, :LANES].astype(BF16)
    cmat = bcm[:, LANES:].astype(BF16)
    bbd = b_sel * _stack_heads(bmat)
    g4 = lax.dot_general(cmat, bbd, NT_DIMS, preferred_element_type=F32)
    xbd = bd * _stack_heads((x4 * dt).astype(BF16))
    y = (jnp.dot((g4 * decay).astype(BF16), xbd, preferred_element_type=F32)
         + jnp.dot(cmat, sg_prev.astype(BF16), preferred_element_type=F32) * jnp.exp(ac))
    w = jnp.exp(atot - ac) * dt
    dsg = lax.dot_general(bmat, (x4 * w).astype(BF16), TN_DIMS, preferred_element_type=F32)
    sg_new = jnp.exp(atot) * sg_prev + s_sel * dsg
    return y, sg_new


def _ssd_kernel(*refs, nblk, rows, zero_init, n_alias):
    xf_ref, bcf_ref, gf_ref, xb_ref, bcb_ref, gb_ref, dtb_ref, alog_ref, dskip_ref = refs[:9]
    if not zero_init:
        s0_ref = refs[9]
    outs = refs[9 + (0 if zero_init else 1) + n_alias:-1]
    yf_ref, yb_ref = outs[:2]
    s_scr = refs[-1]
    j = pl.program_id(1)
    nchunk = rows // CHUNK

    @pl.when(j == 0)
    def _():
        if zero_init:
            s_scr[...] = jnp.zeros_like(s_scr)
        else:
            s_scr[...] = s0_ref[0]

    reads, block, tri, _ = _scan_consts()
    bd = _group_mask(HEAD_LANES, HEAD_LANES, 0, 0, BF16)
    b_sel = _group_mask(HEAD_LANES, LANES, 1, 0, BF16)
    s_sel = _group_mask(LANES, HEAD_LANES, 0, 1, F32)
    a_coef = -jnp.exp(alog_ref[...])

    state = [s_scr[d] for d in range(2)]
    for ci in range(nchunk):
        for d in range(2):
            x_ref, bc_ref, g_ref, y_ref = ((xf_ref, bcf_ref, gf_ref, yf_ref) if d == 0
                                           else (xb_ref, bcb_ref, gb_ref, yb_ref))
            cj = ci if d == 0 else nchunk - 1 - ci
            rs = slice(cj * CHUNK, (cj + 1) * CHUNK)
            dt128 = jax.nn.softplus(g_ref[rs, :] + dtb_ref[...])
            x4 = x_ref[rs, :]
            y, state[d] = _ssd_chunk(x4, bc_ref[rs, :], dt128, dt128 * a_coef, d, state[d],
                                     reads, block, tri, b_sel, s_sel, bd)
            if d == 0:
                y = y + dskip_ref[...] * x4
            y_ref[rs, :] = y
    for d in range(2):
        s_scr[d] = state[d]

    if zero_init:
        sout_ref = outs[2]

        @pl.when(j == nblk - 1)
        def _():
            for d in range(2):
                s_t = s_scr[d].T
                for h in range(N_HEADS):
                    g0 = (h // 2) * D_STATE
                    sout_ref[0, 0, d, h] = s_t[h * D_STATE:(h + 1) * D_STATE, g0:g0 + D_STATE]
            _zero_later_layers(sout_ref)


def _ssd(xbc, u, dt_bias_row, alog_row, dskip_row, bsz, t, init, l=0, cache=None):
    rows = min(t, SCAN_ROWS)
    nblk = t // rows
    zero_init = init is None

    def fwd(col):
        return lambda b, j: (b * nblk + j, col)

    def bwd(col):
        return lambda b, j: (b * nblk + nblk - 1 - j, col)

    in_specs = []
    for mk in (fwd, bwd):
        in_specs += [pl.BlockSpec((rows, D_REC), mk(0)), pl.BlockSpec((rows, D_REC), mk(1)),
                     pl.BlockSpec((rows, LANES), mk(COL_SMALL // LANES))]
    in_specs += [pl.BlockSpec((1, LANES), lambda b, j: (0, 0)),
                 pl.BlockSpec((1, LANES), lambda b, j: (0, 0)),
                 pl.BlockSpec((1, D_REC), lambda b, j: (0, 0))]
    args = [xbc, xbc, u, xbc, xbc, u, dt_bias_row, alog_row, dskip_row]
    out_specs = [pl.BlockSpec((rows, D_REC), fwd(0)), pl.BlockSpec((rows, D_REC), bwd(0))]
    out_shape = [jax.ShapeDtypeStruct((bsz * t, D_REC), F32), jax.ShapeDtypeStruct((bsz * t, D_REC), F32)]
    aliases = {}
    if zero_init:
        shape, spec, _ = _layer_cache((2, N_HEADS, D_STATE, D_STATE), l, bsz)
        out_shape.append(shape)
        out_specs.append(spec)
        if cache is not None:
            in_specs.append(pl.BlockSpec(memory_space=pl.ANY))
            aliases = {len(args): 2}
            args.append(cache)
    else:
        in_specs.append(pl.BlockSpec((1, 2, LANES, HEAD_LANES), lambda b, j: (b, 0, 0, 0)))
        args.append(init)
    return pl.pallas_call(
        functools.partial(_ssd_kernel, nblk=nblk, rows=rows, zero_init=zero_init, n_alias=len(aliases)),
        grid=(bsz, nblk),
        in_specs=in_specs,
        out_specs=out_specs,
        out_shape=out_shape,
        input_output_aliases=aliases,
        scratch_shapes=[pltpu.VMEM((2, LANES, HEAD_LANES), F32)],
        compiler_params=_params(("parallel", "arbitrary")),
        name="ssd_scan",
    )(*args)


def _outproj_kernel(att_ref, hf_ref, hb_ref, mo_ref, yf_ref, yb_ref, z_ref, w_ref, x_ref, gate_ref,
                    mnw_ref, snw_ref, lg_ref, lb_ref, o_ref):
    hh = hf_ref[...] + hb_ref[...]
    parts = []
    for h in range(N_HEADS):
        xh = hh[:, h * D_STATE:(h + 1) * D_STATE]
        mu = jnp.mean(xh, -1, keepdims=True)
        dlt = xh - mu
        var = jnp.mean(dlt * dlt, -1, keepdims=True)
        parts.append(dlt * lax.rsqrt(var + EPS))
    ml = jax.nn.sigmoid(mo_ref[...]) * jnp.concatenate(parts, axis=1) * mnw_ref[...]
    yz = (yf_ref[...] + yb_ref[...]) * _silu(z_ref[...])
    parts = []
    for grp in range(N_GROUPS):
        yg = yz[:, grp * LANES:(grp + 1) * LANES]
        parts.append(yg * lax.rsqrt(jnp.mean(yg * yg, -1, keepdims=True) + EPS))
    ssm = jnp.concatenate(parts, axis=1) * snw_ref[...]
    mixed = (jnp.dot(att_ref[...], w_ref[0:D_ATT], preferred_element_type=F32)
             + _bdot(ml, w_ref[D_ATT:D_ATT + D_REC])
             + _bdot(ssm, w_ref[D_ATT + D_REC:D_MODEL]))
    y = ALPHA * x_ref[...] + gate_ref[0] * mixed
    o_ref[...] = _layernorm_rows(y, lg_ref[...], lb_ref[...])


def _out_proj(att, hf, hb, yf, yb, u, w, x, gate, mnw, snw, lg, lb, rows_per_mod):
    n = x.shape[0]
    tpb = rows_per_mod // ROW_TILE
    row = lambda width, col: pl.BlockSpec((ROW_TILE, width), lambda i: (i, col))
    vec = lambda width: pl.BlockSpec((1, width), lambda i: (0, 0))
    return pl.pallas_call(
        _outproj_kernel,
        grid=(n // ROW_TILE,),
        in_specs=[row(D_ATT, 0), row(D_REC, 0), row(D_REC, 0), row(D_REC, COL_MO // D_REC),
                  row(D_REC, 0), row(D_REC, 0), row(D_REC, COL_SZ // D_REC),
                  pl.BlockSpec((D_MODEL, D_MODEL), lambda i: (0, 0)),
                  row(D_MODEL, 0),
                  pl.BlockSpec((1, 1, D_MODEL), lambda i: (i // tpb, 0, 0)),
                  vec(D_REC), vec(D_REC), vec(D_MODEL), vec(D_MODEL)],
        out_specs=row(D_MODEL, 0),
        out_shape=jax.ShapeDtypeStruct((n, D_MODEL), F32),
        compiler_params=_params(("parallel",)),
        name="out_proj",
    )(att, hf, hb, u, yf, yb, u, w, x, gate, mnw.reshape(1, D_REC), snw.reshape(1, D_REC),
      lg.reshape(1, D_MODEL), lb.reshape(1, D_MODEL))


FF_TILE = D_FF // 2
FFN_ROWS = 1024
FFN_SUB = 512
FFN_VMEM_LIMIT = 56 * 1024 * 1024


def _swiglu_partial(h, w1, w3, w2):
    a = jnp.dot(h, w1, preferred_element_type=F32)
    b = jnp.dot(h, w3, preferred_element_type=F32)
    return jnp.dot((_silu(a) * b).astype(BF16), w2, preferred_element_type=F32)


def _ffn_kernel(x_ref, sc_ref, sh_ref, gate_ref, w1_ref, w3_ref, w2_ref, lg_ref, lb_ref, o_ref):
    for r0 in range(0, FFN_ROWS, FFN_SUB):
        rows = slice(r0, r0 + FFN_SUB)
        x = x_ref[rows, :]
        h = (x * (1.0 + sc_ref[0]) + sh_ref[0]).astype(BF16)
        y = ALPHA * x + gate_ref[0] * _swiglu_partial(h, w1_ref[...], w3_ref[...], w2_ref[...])
        o_ref[rows, :] = _layernorm_rows(y, lg_ref[...], lb_ref[...])


def _ffn(x, sc, sh, gate, w1, w3, w2, lg, lb, rows_per_mod):
    n = x.shape[0]
    tpb = rows_per_mod // FFN_ROWS
    modspec = pl.BlockSpec((1, 1, D_MODEL), lambda i: (i // tpb, 0, 0))
    vec = pl.BlockSpec((1, D_MODEL), lambda i: (0, 0))
    once = pl.Buffered(1)
    return pl.pallas_call(
        _ffn_kernel,
        grid=(n // FFN_ROWS,),
        in_specs=[pl.BlockSpec((FFN_ROWS, D_MODEL), lambda i: (i, 0)), modspec, modspec, modspec,
                  pl.BlockSpec((D_MODEL, D_FF), lambda i: (0, 0), pipeline_mode=once),
                  pl.BlockSpec((D_MODEL, D_FF), lambda i: (0, 0), pipeline_mode=once),
                  pl.BlockSpec((D_FF, D_MODEL), lambda i: (0, 0), pipeline_mode=once), vec, vec],
        out_specs=pl.BlockSpec((FFN_ROWS, D_MODEL), lambda i: (i, 0)),
        out_shape=jax.ShapeDtypeStruct((n, D_MODEL), F32),
        compiler_params=_params(("parallel",), FFN_VMEM_LIMIT),
        name="ffn_dense",
    )(x, sc, sh, gate, w1, w3, w2, lg.reshape(1, D_MODEL), lb.reshape(1, D_MODEL))


def _router_kernel(x_ref, sc_ref, sh_ref, rw_ref, gates_ref, h_ref):
    h = x_ref[...] * (1.0 + sc_ref[0]) + sh_ref[0]
    h_ref[...] = h.astype(BF16)
    logits = jnp.dot(h, rw_ref[...], precision=HIGHEST, preferred_element_type=F32)
    lane = lax.broadcasted_iota(jnp.int32, logits.shape, 1)
    valid = lane < N_EXPERTS
    p = jnp.where(valid, _softmax_rows(jnp.where(valid, logits, -jnp.inf)), -2.0)
    p1 = jnp.max(p, -1, keepdims=True)
    i1 = jnp.min(jnp.where(p == p1, lane, LANES), -1, keepdims=True)
    rest = jnp.where(lane == i1, -1.0, p)
    p2 = jnp.max(rest, -1, keepdims=True)
    i2 = jnp.min(jnp.where(rest == p2, lane, LANES), -1, keepdims=True)
    tot = p1 + p2
    gates_ref[...] = jnp.where(lane == i1, p1 / tot, jnp.where(lane == i2, p2 / tot, 0.0))


def _router(x, sc, sh, router_w, rows_per_mod):
    n = x.shape[0]
    tpb = rows_per_mod // ROW_TILE
    modspec = pl.BlockSpec((1, 1, D_MODEL), lambda i: (i // tpb, 0, 0))
    rw = jnp.pad(router_w, ((0, 0), (0, LANES - N_EXPERTS)))
    return pl.pallas_call(
        _router_kernel,
        grid=(n // ROW_TILE,),
        in_specs=[pl.BlockSpec((ROW_TILE, D_MODEL), lambda i: (i, 0)), modspec, modspec,
                  pl.BlockSpec((D_MODEL, LANES), lambda i: (0, 0))],
        out_specs=[pl.BlockSpec((ROW_TILE, LANES), lambda i: (i, 0)),
                   pl.BlockSpec((ROW_TILE, D_MODEL), lambda i: (i, 0))],
        out_shape=[jax.ShapeDtypeStruct((n, LANES), F32), jax.ShapeDtypeStruct((n, D_MODEL), BF16)],
        compiler_params=_params(("parallel",)),
        name="router",
    )(x, sc, sh, rw)


MOE_ROWS = 1024
MOE_TILE = 128
MOE_MAX_TILES = MOE_ROWS // MOE_TILE
MOE_VMEM_LIMIT = 56 * 1024 * 1024


def _moe_kernel(h_ref, gates_ref, w1_ref, w3_ref, w2_ref, x_ref, gate_ref, lg_ref, lb_ref, o_ref,
                slot_scr, slott_scr, hs_scr, ys_scr):
    e = pl.program_id(1)
    j = pl.program_id(2)
    last_j = pl.num_programs(2) - 1

    @pl.when(jnp.logical_and(e == 0, j == 0))
    def _():
        r = lax.broadcasted_iota(jnp.int32, (MOE_ROWS, MOE_ROWS), 0)
        c = lax.broadcasted_iota(jnp.int32, (MOE_ROWS, MOE_ROWS), 1)
        before = jnp.where(c < r, 1.0, 0.0).astype(BF16)
        mask = gates_ref[...] != 0.0
        rank = jnp.dot(before, jnp.where(mask, 1.0, 0.0).astype(BF16), preferred_element_type=F32)
        slot = jnp.where(mask, rank, -1.0).astype(jnp.int32)
        slot_scr[...] = slot
        slott_scr[...] = slot.T
        o_ref[...] = jnp.zeros_like(o_ref)

    lane = lax.broadcasted_iota(jnp.int32, (MOE_ROWS, LANES), 1)
    slot_col = jnp.max(jnp.where(lane == e, slot_scr[...], -1), -1, keepdims=True)
    n_tiles = (jnp.max(slot_col) + MOE_TILE) // MOE_TILE

    def tile(k, carry):
        @pl.when(j == 0)
        def _():
            slot_row = slott_scr[pl.ds(e, 1), :]
            rr = lax.broadcasted_iota(jnp.int32, (MOE_TILE, MOE_ROWS), 0) + k * MOE_TILE
            pick = jnp.where(rr == slot_row, 1.0, 0.0).astype(BF16)
            hs_scr[k] = jnp.dot(pick, h_ref[...], preferred_element_type=F32).astype(BF16)
            ys_scr[k] = jnp.zeros((MOE_TILE, D_MODEL), F32)

        ys_scr[k] += _swiglu_partial(hs_scr[k], w1_ref[0], w3_ref[0], w2_ref[0])

        @pl.when(j == last_j)
        def _():
            g_col = jnp.sum(jnp.where(lane == e, gates_ref[...], 0.0), -1, keepdims=True)
            cc = lax.broadcasted_iota(jnp.int32, (MOE_ROWS, 2 * MOE_TILE), 1)
            cc = jnp.where(cc >= MOE_TILE, cc - MOE_TILE, cc) + k * MOE_TILE
            put = jnp.where(slot_col == cc, 1.0, 0.0).astype(BF16)
            y = ys_scr[k]
            y_hi = y.astype(BF16)
            y_lo = (y - y_hi.astype(F32)).astype(BF16)
            back = jnp.dot(put, jnp.concatenate([y_hi, y_lo], axis=0), preferred_element_type=F32)
            o_ref[...] += g_col * back

        return carry

    lax.fori_loop(0, n_tiles, tile, 0)

    @pl.when(jnp.logical_and(e == pl.num_programs(1) - 1, j == last_j))
    def _():
        y = ALPHA * x_ref[...] + gate_ref[0] * o_ref[...]
        o_ref[...] = _layernorm_rows(y, lg_ref[...], lb_ref[...])


def _moe(h, gates, w1, w3, w2, x, gate, lg, lb, rows_per_mod):
    n = h.shape[0]
    tpb = rows_per_mod // MOE_ROWS
    vec = pl.BlockSpec((1, D_MODEL), lambda i, e, j: (0, 0))
    return pl.pallas_call(
        _moe_kernel,
        grid=(n // MOE_ROWS, N_EXPERTS, D_FF // FF_TILE),
        in_specs=[pl.BlockSpec((MOE_ROWS, D_MODEL), lambda i, e, j: (i, 0)),
                  pl.BlockSpec((MOE_ROWS, LANES), lambda i, e, j: (i, 0)),
                  pl.BlockSpec((1, D_MODEL, FF_TILE), lambda i, e, j: (e, 0, j)),
                  pl.BlockSpec((1, D_MODEL, FF_TILE), lambda i, e, j: (e, 0, j)),
                  pl.BlockSpec((1, FF_TILE, D_MODEL), lambda i, e, j: (e, j, 0)),
                  pl.BlockSpec((MOE_ROWS, D_MODEL), lambda i, e, j: (i, 0)),
                  pl.BlockSpec((1, 1, D_MODEL), lambda i, e, j: (i // tpb, 0, 0)), vec, vec],
        out_specs=pl.BlockSpec((MOE_ROWS, D_MODEL), lambda i, e, j: (i, 0)),
        out_shape=jax.ShapeDtypeStruct((n, D_MODEL), F32),
        scratch_shapes=[pltpu.VMEM((MOE_ROWS, LANES), jnp.int32), pltpu.VMEM((LANES, MOE_ROWS), jnp.int32),
                        pltpu.VMEM((MOE_MAX_TILES, MOE_TILE, D_MODEL), BF16),
                        pltpu.VMEM((MOE_MAX_TILES, MOE_TILE, D_MODEL), F32)],
        compiler_params=_params(("parallel", "arbitrary", "arbitrary"), MOE_VMEM_LIMIT),
        name="moe",
    )(h, gates, w1, w3, w2, x, gate, lg.reshape(1, D_MODEL), lb.reshape(1, D_MODEL))


def _permute_w_in(w):
    pad = jnp.zeros((D_MODEL, U_COLS - ORIG_END), w.dtype)
    return jnp.concatenate([w[:, :ORIG_GATES], w[:, ORIG_SX:ORIG_DT], w[:, ORIG_SZ:ORIG_SX],
                            w[:, ORIG_GATES:ORIG_SZ], w[:, ORIG_DT:ORIG_END], pad], axis=1).astype(BF16)


def _small_row(vals, offset):
    v = vals.reshape(-1).astype(F32)
    return jnp.zeros((1, LANES), F32).at[0, offset:offset + v.shape[0]].set(v)


def _pack_mlstm_state(c, n, m):
    shape = c.shape[:2] + (D_STATE, HEAD_LANES)
    c_rows = jnp.swapaxes(c, 2, 3).reshape(shape)
    n_rows = jnp.broadcast_to(jnp.swapaxes(n, 2, 3)[..., None], c.shape[:2] + (D_STATE, N_HEADS, D_STATE))
    return (jnp.concatenate([c_rows, n_rows.reshape(shape)], axis=-1),
            jnp.repeat(m, D_STATE, axis=-1)[:, :, None, :])


def _pack_ssd_state(s):
    sel = (jnp.arange(N_GROUPS)[:, None] == jnp.arange(N_HEADS)[None, :] // 2).astype(F32)
    return jnp.einsum('bdhpn,gh->bdgnhp', s, sel).reshape(s.shape[:2] + (LANES, HEAD_LANES))


def _layer(x, mods, P, l, bsz, t, ctx, caches=None):
    sh1, sc1, g1, sh2, sc2, g2 = mods
    rows_per_mod = x.shape[0] // sh1.shape[0]
    lam_init = 0.8 - 0.6 * math.exp(-0.3 * l)
    u = _in_proj(x, sc1, sh1, P['w_in'][l], rows_per_mod)

    gate_bias = (_small_row(P['mlstm_gate_b'][l, 0], GATE_I) + _small_row(P['mlstm_gate_b'][l, 1], GATE_F))
    dt_bias = _small_row(P['ssm_dt_bias'][l], GATE_DT)
    alog = _small_row(P['ssm_A_log'][l], GATE_DT)
    dskip = jnp.repeat(P['ssm_D'][l].astype(F32), D_STATE).reshape(1, D_REC)

    if ctx is None:
        att, k_new, v_new = _attention_ctx(u, P['attn_lambda'][l], P['attn_norm_w'][l], lam_init, bsz, t, l,
                                           None if caches is None else caches[0:2])
        m_init = s_init = None
    else:
        ck, cv, c_c, c_n, c_m, c_s = ctx
        q, k, v = _rope_prep(u, bsz, t)
        k_all = jnp.concatenate([k, ck.astype(BF16)], axis=3)
        v_all = jnp.concatenate([v, cv.astype(BF16)], axis=2)
        att = _attention_lat(q, k_all, v_all, P['attn_lambda'][l], P['attn_norm_w'][l], lam_init, bsz, t)
        m_init = _pack_mlstm_state(c_c, c_n, c_m)
        s_init = _pack_ssd_state(c_s)
    hf, hb, *mlstm_caches = _mlstm(u, gate_bias, bsz, t, m_init, l, None if caches is None else caches[2:5])
    xbc = _ssd_conv(u, P['conv_w'][l], P['conv_b'][l], bsz, t)
    yf, yb, *ssd_caches = _ssd(xbc, u, dt_bias, alog, dskip, bsz, t, s_init, l,
                               None if caches is None else caches[5])

    x = _out_proj(att, hf, hb, yf, yb, u, P['w_out'][l], x, g1, P['mlstm_norm_w'][l], P['ssm_norm_w'][l],
                  P['ln_g'][l, 0], P['ln_b'][l, 0], rows_per_mod)
    if l % 2 == 0:
        x = _ffn(x, sc2, sh2, g2, P['ffn_w1'][l // 2], P['ffn_w3'][l // 2], P['ffn_w2'][l // 2],
                 P['ln_g'][l, 1], P['ln_b'][l, 1], rows_per_mod)
    else:
        gates, h2 = _router(x, sc2, sh2, P['router_w'][l // 2], rows_per_mod)
        x = _moe(h2, gates, P['moe_w1'][l // 2], P['moe_w3'][l // 2], P['moe_w2'][l // 2],
                 x, g2, P['ln_g'][l, 1], P['ln_b'][l, 1], rows_per_mod)
    if ctx is None:
        return x, (k_new, v_new, *mlstm_caches, *ssd_caches)
    return x, None


def kernel(x_prompt, x_sample, c, cache_attn_k, cache_attn_v, state_mlstm_C, state_mlstm_n, state_mlstm_m, state_ssm, c_ctx, w_ada, b_ada, w_in, w_out, attn_lambda, attn_norm_w, mlstm_gate_b, mlstm_norm_w, conv_w, conv_b, ssm_A_log, ssm_dt_bias, ssm_D, ssm_norm_w, ln_g, ln_b, ffn_w1, ffn_w3, ffn_w2, router_w, moe_w1, moe_w3, moe_w2):
    bsz, seq, _ = x_prompt.shape
    dbsz, dseq, _ = x_sample.shape
    P = dict(w_in=[_permute_w_in(w_in[l]) for l in range(DEPTH)], w_out=w_out.astype(BF16),
             attn_lambda=attn_lambda, attn_norm_w=attn_norm_w, mlstm_gate_b=mlstm_gate_b,
             mlstm_norm_w=mlstm_norm_w, conv_w=conv_w, conv_b=conv_b, ssm_A_log=ssm_A_log,
             ssm_dt_bias=ssm_dt_bias, ssm_D=ssm_D, ssm_norm_w=ssm_norm_w, ln_g=ln_g, ln_b=ln_b,
             ffn_w1=ffn_w1.astype(BF16), ffn_w3=ffn_w3.astype(BF16), ffn_w2=ffn_w2.astype(BF16),
             router_w=router_w, moe_w1=moe_w1.astype(BF16), moe_w3=moe_w3.astype(BF16),
             moe_w2=moe_w2.astype(BF16))

    cvec = jnp.zeros((8, D_MODEL), F32).at[0].set(c_ctx).at[1:1 + dbsz].set(c)
    mod = _modulation(cvec, w_ada, b_ada)

    def mods_for(l, lo, hi):
        return [mod[l, lo:hi, i * D_MODEL:(i + 1) * D_MODEL][:, None, :] for i in range(6)]

    y_prompt = x_prompt.reshape(bsz * seq, D_MODEL)
    caches = None
    for l in range(DEPTH):
        y_prompt, caches = _layer(y_prompt, mods_for(l, 0, 1), P, l, bsz, seq, None, caches)
    new_k, new_v, new_c, new_n, m_spread, new_s = caches

    y_sample = x_sample.reshape(dbsz * dseq, D_MODEL)
    for l in range(DEPTH):
        ctx = (cache_attn_k[:, l], cache_attn_v[:, l], state_mlstm_C[:, l], state_mlstm_n[:, l],
               state_mlstm_m[:, l], state_ssm[:, l])
        y_sample, _ = _layer(y_sample, mods_for(l, 1, 1 + dbsz), P, l, dbsz, dseq, ctx)

    return (y_prompt.reshape(bsz, seq, D_MODEL), y_sample.reshape(dbsz, dseq, D_MODEL),
            new_k, new_v, new_c, new_n, m_spread[:, :, :, 0, ::D_STATE], new_s)
```

```python
import functools
import math

import jax
import jax.numpy as jnp
from jax import lax
from jax.experimental import pallas as pl
from jax.experimental.pallas import tpu as pltpu

F32 = jnp.float32
BF16 = jnp.bfloat16
HIGHEST = lax.Precision.HIGHEST

D_MODEL = 1024
DEPTH = 2
GRID_W = 64
N_HEADS = 4
D_ATT = 512
D_HEAD_V = 128
D_QK = 64
D_REC = 256
D_STATE = 64
N_GROUPS = 2
D_CONV = 3
D_FF = 2816
N_EXPERTS = 8
ALPHA = (2.0 * DEPTH) ** 0.25
CHUNK = 64
ROPE_BASE = 10000.0
LOG2E = 1.4426950408889634
EPS = 1e-5

COL_AQ, COL_AK, COL_AV = 0, 512, 1024
COL_MQ, COL_MK, COL_MV, COL_MO = 1536, 1792, 2048, 2304
COL_SX, COL_SBC, COL_SZ = 2560, 2816, 3072
COL_SMALL = 3328
U_COLS = 3584
ORIG_GATES, ORIG_SZ, ORIG_SX, ORIG_DT, ORIG_END = 2560, 2576, 2832, 3344, 3352
GATE_I, GATE_F, GATE_DT = 0, 8, 16

LANES = 128
ROW_TILE = 512
SCAN_ROWS = 256
SCAN_BATCH = 2
VMEM_LIMIT = 48 * 1024 * 1024

NT_DIMS = (((1,), (1,)), ((), ()))
TN_DIMS = (((0,), (0,)), ((), ()))


def _params(sem, vmem=VMEM_LIMIT):
    return pltpu.CompilerParams(dimension_semantics=sem, vmem_limit_bytes=vmem)


def _silu(x):
    return x * jax.nn.sigmoid(x)


def _bdot(a, b):
    return jnp.dot(a.astype(BF16), b.astype(BF16), preferred_element_type=F32)


def _bdot_nt(a, b):
    return lax.dot_general(a.astype(BF16), b.astype(BF16), NT_DIMS, preferred_element_type=F32)


def _bdot_tn(a, b):
    return lax.dot_general(a.astype(BF16), b.astype(BF16), TN_DIMS, preferred_element_type=F32)


def _layernorm_rows(y, g, b):
    mu = jnp.mean(y, -1, keepdims=True)
    d = y - mu
    var = jnp.mean(d * d, -1, keepdims=True)
    return d * lax.rsqrt(var + EPS) * g + b


def _mod_kernel(c_ref, w_ref, b_ref, o_ref):
    o_ref[0] = jnp.dot(_silu(c_ref[...]), w_ref[0], precision=HIGHEST,
                       preferred_element_type=F32) + b_ref[0]


def _modulation(cvec, w_ada, b_ada):
    tn = 1536
    return pl.pallas_call(
        _mod_kernel,
        grid=(DEPTH, 6 * D_MODEL // tn),
        in_specs=[pl.BlockSpec((8, D_MODEL), lambda l, j: (0, 0)),
                  pl.BlockSpec((1, D_MODEL, tn), lambda l, j: (l, 0, j)),
                  pl.BlockSpec((1, 1, tn), lambda l, j: (l, 0, j))],
        out_specs=pl.BlockSpec((1, 8, tn), lambda l, j: (l, 0, j)),
        out_shape=jax.ShapeDtypeStruct((DEPTH, 8, 6 * D_MODEL), F32),
        compiler_params=_params(("parallel", "parallel")),
        name="modulation",
    )(cvec, w_ada, b_ada.reshape(DEPTH, 1, 6 * D_MODEL))


def _inproj_kernel(x_ref, sc_ref, sh_ref, w_ref, o_ref):
    h = (x_ref[...] * (1.0 + sc_ref[0]) + sh_ref[0]).astype(BF16)
    for n0 in range(0, U_COLS, 512):
        o_ref[:, n0:n0 + 512] = jnp.dot(h, w_ref[:, n0:n0 + 512], preferred_element_type=F32)


def _in_proj(x, sc, sh, w, rows_per_mod):
    n = x.shape[0]
    tpb = rows_per_mod // ROW_TILE
    return pl.pallas_call(
        _inproj_kernel,
        grid=(n // ROW_TILE,),
        in_specs=[pl.BlockSpec((ROW_TILE, D_MODEL), lambda i: (i, 0)),
                  pl.BlockSpec((1, 1, D_MODEL), lambda i: (i // tpb, 0, 0)),
                  pl.BlockSpec((1, 1, D_MODEL), lambda i: (i // tpb, 0, 0)),
                  pl.BlockSpec((D_MODEL, U_COLS), lambda i: (0, 0))],
        out_specs=pl.BlockSpec((ROW_TILE, U_COLS), lambda i: (i, 0)),
        out_shape=jax.ShapeDtypeStruct((n, U_COLS), F32),
        compiler_params=_params(("parallel",)),
        name="in_proj",
    )(x, sc, sh, w)


def _lambda_scalar(lam_ref, lam_init):
    lp = lam_ref[...]
    s01 = jnp.sum(lp[0:1] * lp[1:2], axis=-1, keepdims=True)
    s23 = jnp.sum(lp[2:3] * lp[3:4], axis=-1, keepdims=True)
    return jnp.exp(s01) - jnp.exp(s23) + lam_init


def _softmax_rows(s):
    e = jnp.exp(s - jnp.max(s, -1, keepdims=True))
    return e / jnp.sum(e, -1, keepdims=True)


def _head_norm(o, nw, lam_init):
    return o * lax.rsqrt(jnp.mean(o * o, -1, keepdims=True) + EPS) * nw * (1.0 - lam_init)


def _layer_cache(tail, l, bsz, nb=1):
    zeros = (0,) * len(tail)
    owned = DEPTH if l == 0 else 1
    spec = pl.BlockSpec((nb, owned) + tuple(tail), lambda b, *_: (b, l) + zeros)
    return jax.ShapeDtypeStruct((bsz, DEPTH) + tuple(tail), F32), spec, owned


def _zero_later_layers(ref):
    for b in range(ref.shape[0]):
        for later in range(1, ref.shape[1]):
            ref[b, later] = jnp.zeros(ref.shape[2:], ref.dtype)


def _attn_ctx_kernel(u_ref, lam_ref, nw_ref, *rest, lam_init):
    att_ref, k_ref, v_ref = rest[-3:]
    lam = _lambda_scalar(lam_ref, lam_init)
    for h in range(N_HEADS):
        v = u_ref[:, COL_AV + h * D_HEAD_V:COL_AV + (h + 1) * D_HEAD_V]
        v_ref[0, 0, h] = v
        ps = []
        for m in range(2):
            c0 = h * D_HEAD_V + m * D_QK
            q = u_ref[:, COL_AQ + c0:COL_AQ + c0 + D_QK] * (D_QK ** -0.5)
            k = u_ref[:, COL_AK + c0:COL_AK + c0 + D_QK]
            k_ref[0, 0, h, m] = k
            ps.append(_softmax_rows(_bdot_nt(q, k)))
        o = _bdot(ps[0] - lam * ps[1], v)
        att_ref[:, h * D_HEAD_V:(h + 1) * D_HEAD_V] = _head_norm(o, nw_ref[...], lam_init).astype(BF16)
    _zero_later_layers(k_ref)
    _zero_later_layers(v_ref)


def _attention_ctx(u, lam_p, norm_w, lam_init, bsz, t, l, caches):
    n = bsz * t
    k_shape, k_spec, _ = _layer_cache((N_HEADS, 2, t, D_QK), l, bsz)
    v_shape, v_spec, _ = _layer_cache((N_HEADS, t, D_HEAD_V), l, bsz)
    in_specs = [pl.BlockSpec((t, 3 * D_ATT), lambda b: (b, 0)),
                pl.BlockSpec((4, D_QK), lambda b: (0, 0)),
                pl.BlockSpec((1, D_HEAD_V), lambda b: (0, 0))]
    args = [u, lam_p, norm_w.reshape(1, D_HEAD_V)]
    aliases = {}
    if caches is not None:
        in_specs += [pl.BlockSpec(memory_space=pl.ANY)] * 2
        aliases = {len(args): 1, len(args) + 1: 2}
        args += list(caches)
    return pl.pallas_call(
        functools.partial(_attn_ctx_kernel, lam_init=lam_init),
        grid=(bsz,),
        in_specs=in_specs,
        out_specs=[pl.BlockSpec((t, D_ATT), lambda b: (b, 0)), k_spec, v_spec],
        out_shape=[jax.ShapeDtypeStruct((n, D_ATT), BF16), k_shape, v_shape],
        input_output_aliases=aliases,
        compiler_params=_params(("parallel",)),
        name="attn_ctx",
    )(*args)


def _rope_kernel(u_ref, cos_ref, sa_ref, sb_ref, q_ref, k_ref, v_ref):
    cos, sa, sb = cos_ref[...], sa_ref[...], sb_ref[...]

    def rope(x):
        return x * cos + pltpu.roll(x, LANES - 16, 1) * sa + pltpu.roll(x, 16, 1) * sb

    for h in range(N_HEADS):
        q = rope(u_ref[:, COL_AQ + h * D_HEAD_V:COL_AQ + (h + 1) * D_HEAD_V]) * (LOG2E * D_QK ** -0.5)
        k = rope(u_ref[:, COL_AK + h * D_HEAD_V:COL_AK + (h + 1) * D_HEAD_V])
        for m in range(2):
            q_ref[0, h, m] = q[:, m * D_QK:(m + 1) * D_QK].astype(BF16)
            k_ref[0, h, m] = k[:, m * D_QK:(m + 1) * D_QK].astype(BF16)
        v_ref[0, h] = u_ref[:, COL_AV + h * D_HEAD_V:COL_AV + (h + 1) * D_HEAD_V].astype(BF16)


def _rope_tables(t):
    rows = jnp.repeat(jnp.arange(t // GRID_W, dtype=F32), GRID_W)
    cols = jnp.tile(jnp.arange(GRID_W, dtype=F32), t // GRID_W)
    half = D_QK // 2
    inv = ROPE_BASE ** (-jnp.arange(0, half, 2, dtype=F32) / half)
    ang_r = rows[:, None] * inv
    ang_c = cols[:, None] * inv
    ang = jnp.concatenate([ang_r, ang_r, ang_c, ang_c], -1)
    cos, sin = jnp.cos(ang), jnp.sin(ang)
    quarter = (jnp.arange(D_QK) // (D_QK // 4)) % 2
    sa = jnp.where(quarter == 0, -sin, 0.0)
    sb = jnp.where(quarter == 1, sin, 0.0)
    tile2 = lambda a: jnp.concatenate([a, a], -1)
    return tile2(cos), tile2(sa), tile2(sb)


def _rope_prep(u, bsz, t):
    tr = 512
    nb = t // tr
    cos, sa, sb = _rope_tables(t)
    tab = pl.BlockSpec((tr, LANES), lambda b, i: (i, 0))
    return pl.pallas_call(
        _rope_kernel,
        grid=(bsz, nb),
        in_specs=[pl.BlockSpec((tr, 3 * D_ATT), lambda b, i: (b * nb + i, 0)), tab, tab, tab],
        out_specs=[pl.BlockSpec((1, N_HEADS, 2, tr, D_QK), lambda b, i: (b, 0, 0, i, 0)),
                   pl.BlockSpec((1, N_HEADS, 2, tr, D_QK), lambda b, i: (b, 0, 0, i, 0)),
                   pl.BlockSpec((1, N_HEADS, tr, D_HEAD_V), lambda b, i: (b, 0, i, 0))],
        out_shape=[jax.ShapeDtypeStruct((bsz, N_HEADS, 2, t, D_QK), BF16),
                   jax.ShapeDtypeStruct((bsz, N_HEADS, 2, t, D_QK), BF16),
                   jax.ShapeDtypeStruct((bsz, N_HEADS, t, D_HEAD_V), BF16)],
        compiler_params=_params(("parallel", "parallel")),
        name="rope_prep",
    )(u, cos, sa, sb)


def _attn_lat_kernel(q_ref, k_ref, v_ref, lam_ref, nw_ref, o_ref, *, lam_init):
    lam = _lambda_scalar(lam_ref, lam_init)
    es, sums = [], []
    for m in range(2):
        s = lax.dot_general(q_ref[0, 0, m], k_ref[0, 0, m], NT_DIMS, preferred_element_type=F32)
        e = jnp.exp2(s - jnp.max(s, -1, keepdims=True))
        es.append(e)
        sums.append(jnp.sum(e, -1, keepdims=True))
    a = es[0] - (lam * sums[0] / sums[1]) * es[1]
    o = _bdot(a, v_ref[0, 0]) / sums[0]
    o_ref[...] = _head_norm(o, nw_ref[...], lam_init).astype(BF16)


def _attention_lat(q, k_all, v_all, lam_p, norm_w, lam_init, bsz, t):
    tq = 256
    nq = t // tq
    s = k_all.shape[3]
    return pl.pallas_call(
        functools.partial(_attn_lat_kernel, lam_init=lam_init),
        grid=(bsz, N_HEADS, nq),
        in_specs=[pl.BlockSpec((1, 1, 2, tq, D_QK), lambda b, h, i: (b, h, 0, i, 0)),
                  pl.BlockSpec((1, 1, 2, s, D_QK), lambda b, h, i: (b, h, 0, 0, 0)),
                  pl.BlockSpec((1, 1, s, D_HEAD_V), lambda b, h, i: (b, h, 0, 0)),
                  pl.BlockSpec((4, D_QK), lambda b, h, i: (0, 0)),
                  pl.BlockSpec((1, D_HEAD_V), lambda b, h, i: (0, 0))],
        out_specs=pl.BlockSpec((tq, D_HEAD_V), lambda b, h, i: (b * nq + i, h)),
        out_shape=jax.ShapeDtypeStruct((bsz * t, D_ATT), BF16),
        compiler_params=_params(("parallel", "parallel", "parallel")),
        name="attn_lat",
    )(q, k_all, v_all, lam_p, norm_w.reshape(1, D_HEAD_V))


HEAD_LANES = N_HEADS * D_STATE


def _scan_consts():
    t = lax.broadcasted_iota(jnp.int32, (CHUNK, HEAD_LANES), 0)
    s = lax.broadcasted_iota(jnp.int32, (CHUNK, HEAD_LANES), 1) & (CHUNK - 1)
    r = lax.broadcasted_iota(jnp.int32, (CHUNK, CHUNK), 0)
    c = lax.broadcasted_iota(jnp.int32, (CHUNK, CHUNK), 1)
    reads = (jnp.where(s <= t, 1.0, 0.0), jnp.where(s >= t, 1.0, 0.0))
    block = (jnp.where(s <= t, 0.0, -jnp.inf), jnp.where(s >= t, 0.0, -jnp.inf))
    tri = (jnp.where(c <= r, 1.0, 0.0).astype(BF16), jnp.where(c >= r, 1.0, 0.0).astype(BF16))
    return reads, block, tri, jnp.where(s == t, 1.0, 0.0)


def _cumsum_rows(tri, x):
    hi = x.astype(BF16)
    rest = x - hi.astype(F32)
    mid = rest.astype(BF16)
    lo = (rest - mid.astype(F32)).astype(BF16)
    parts = jnp.dot(tri, jnp.concatenate([hi, mid, lo], axis=1), preferred_element_type=F32)
    return parts[:, :HEAD_LANES] + parts[:, HEAD_LANES:2 * HEAD_LANES] + parts[:, 2 * HEAD_LANES:]


def _group_mask(rows, cols, row_shift, col_shift, dtype):
    r = lax.broadcasted_iota(jnp.int32, (rows, cols), 0) >> 6
    c = (lax.broadcasted_iota(jnp.int32, (rows, cols), 1) >> 6) & (N_HEADS - 1)
    return jnp.where((r >> row_shift) == (c >> col_shift), 1.0, 0.0).astype(dtype)


def _spread(x, chans):
    return jnp.concatenate([jnp.broadcast_to(x[:, c:c + 1], (CHUNK, D_STATE)) for c in chans], axis=1)


def _stack_heads(x):
    return jnp.concatenate([x] * N_HEADS, axis=0)


def _mlstm_chunk(q4, k4, v4, g, d, cn_prev, m_prev, reads, block, tri, eye, bd, bd_f32):
    li = _spread(g, [GATE_I + d * N_HEADS + h for h in range(N_HEADS)])
    lf = _spread(jax.nn.log_sigmoid(g), [GATE_F + d * N_HEADS + h for h in range(N_HEADS)])
    bc = _cumsum_rows(tri[d], lf)
    btot = jnp.sum(lf, 0, keepdims=True)
    b_row = jnp.sum(reads[1 - d] * lf, 0, keepdims=True)
    li_row = jnp.sum(eye * li, 0, keepdims=True)
    dm = bc - b_row + li_row + block[d]
    rmax = jnp.concatenate(
        [jnp.broadcast_to(jnp.max(dm[:, h * D_STATE:(h + 1) * D_STATE], -1, keepdims=True), (CHUNK, D_STATE))
         for h in range(N_HEADS)], axis=1)
    inter = bc + m_prev
    m_t = jnp.maximum(inter, rmax)
    w_inter = jnp.exp(inter - m_t)
    qs = (q4 * (D_STATE ** -0.5)).astype(BF16)
    kbd = bd[:, :HEAD_LANES] * _stack_heads(k4.astype(BF16))
    s4 = lax.dot_general(qs, kbd, NT_DIMS, preferred_element_type=F32) * jnp.exp(dm - m_t)
    vo = jnp.concatenate([v4.astype(BF16), jnp.ones((CHUNK, HEAD_LANES), BF16)], axis=1)
    vbd = bd * _stack_heads(vo)
    nd = (jnp.concatenate([w_inter, w_inter], axis=1)
          * jnp.dot(qs, bd * _stack_heads(cn_prev.astype(BF16)), preferred_element_type=F32)
          + jnp.dot(s4.astype(BF16), vbd, preferred_element_type=F32))
    hc = nd[:, :HEAD_LANES] / jnp.maximum(jnp.abs(nd[:, HEAD_LANES:]), jnp.exp(-m_t))
    gcol = btot - bc + li
    m_new = jnp.maximum(btot + m_prev, jnp.max(gcol, 0, keepdims=True))
    w_c = jnp.exp(btot + m_prev - m_new)
    kw = (k4 * jnp.exp(gcol - m_new)).astype(BF16)
    dcn = lax.dot_general(kw, vo, TN_DIMS, preferred_element_type=F32)
    own = sum(bd_f32[h * D_STATE:(h + 1) * D_STATE] * dcn[h * D_STATE:(h + 1) * D_STATE]
              for h in range(N_HEADS))
    cn_new = jnp.concatenate([w_c, w_c], axis=1) * cn_prev + own
    return hc, cn_new, m_new


def _mlstm_kernel(*refs, nblk, rows, zero_init, n_alias):
    qf_ref, kf_ref, vf_ref, gf_ref, qb_ref, kb_ref, vb_ref, gb_ref, bias_ref = refs[:9]
    if not zero_init:
        c0_ref, m0_ref = refs[9:11]
    outs = refs[9 + (0 if zero_init else 2) + n_alias:-2]
    hf_ref, hb_ref = outs[:2]
    cn_scr, m_scr = refs[-2:]
    j = pl.program_id(1)
    nchunk = rows // CHUNK

    @pl.when(j == 0)
    def _():
        if zero_init:
            cn_scr[...] = jnp.zeros_like(cn_scr)
            m_scr[...] = jnp.zeros_like(m_scr)
        else:
            cn_scr[...] = c0_ref[...]
            m_scr[...] = m0_ref[...]

    reads, block, tri, eye = _scan_consts()
    bd = _group_mask(HEAD_LANES, 2 * HEAD_LANES, 0, 0, BF16)
    bd_f32 = _group_mask(HEAD_LANES, 2 * HEAD_LANES, 0, 0, F32)

    chains = [(b, d) for b in range(SCAN_BATCH) for d in range(2)]
    state = {bd_: (cn_scr[bd_], m_scr[bd_]) for bd_ in chains}
    for ci in range(nchunk):
        for b, d in chains:
            q_ref, k_ref, v_ref, g_ref, h_ref = ((qf_ref, kf_ref, vf_ref, gf_ref, hf_ref) if d == 0
                                                 else (qb_ref, kb_ref, vb_ref, gb_ref, hb_ref))
            cj = ci if d == 0 else nchunk - 1 - ci
            rs = slice(cj * CHUNK, (cj + 1) * CHUNK)
            g = g_ref[b, rs, :] + bias_ref[...]
            hc, cn_new, m_new = _mlstm_chunk(q_ref[b, rs, :], k_ref[b, rs, :], v_ref[b, rs, :], g, d,
                                             *state[b, d], reads, block, tri, eye, bd, bd_f32)
            h_ref[b, rs, :] = hc
            state[b, d] = (cn_new, m_new)
    for bd_ in chains:
        cn_scr[bd_], m_scr[bd_] = state[bd_]

    if zero_init:
        cout_ref, nout_ref, mout_ref = outs[2:]

        @pl.when(j == nblk - 1)
        def _():
            r = lax.broadcasted_iota(jnp.int32, (D_STATE, D_STATE), 0)
            c = lax.broadcasted_iota(jnp.int32, (D_STATE, D_STATE), 1)
            for b, d in chains:
                for h in range(N_HEADS):
                    r0, r1 = h * D_STATE, (h + 1) * D_STATE
                    cout_ref[b, 0, d, h] = cn_scr[b, d, :, r0:r1]
                    n_spread = cn_scr[b, d, :, HEAD_LANES + r0:HEAD_LANES + r1]
                    nout_ref[b, 0, d, h:h + 1, :] = jnp.sum(jnp.where(r == c, n_spread, 0.0), 0, keepdims=True)
                mout_ref[b, 0, d] = m_scr[b, d]
            _zero_later_layers(cout_ref)
            _zero_later_layers(nout_ref)
            _zero_later_layers(mout_ref)


def _mlstm(u, gate_bias, bsz, t, init, l=0, caches=None):
    rows = min(t, SCAN_ROWS)
    nblk = t // rows
    zero_init = init is None

    nb = SCAN_BATCH
    fwd = lambda col: (lambda p, j: (p, j, col))
    bwd = lambda col: (lambda p, j: (p, nblk - 1 - j, col))
    cq, ck, cv, cg = COL_MQ // D_REC, COL_MK // D_REC, COL_MV // D_REC, COL_SMALL // LANES
    in_specs = []
    for mk in (fwd, bwd):
        in_specs += [pl.BlockSpec((nb, rows, D_REC), mk(cq)), pl.BlockSpec((nb, rows, D_REC), mk(ck)),
                     pl.BlockSpec((nb, rows, D_REC), mk(cv)), pl.BlockSpec((nb, rows, LANES), mk(cg))]
    in_specs.append(pl.BlockSpec((1, LANES), lambda p, j: (0, 0)))
    args = [u.reshape(bsz, t, U_COLS)] * 8 + [gate_bias]
    state_c = pl.BlockSpec((nb, 2, D_STATE, 2 * HEAD_LANES), lambda p, j: (p, 0, 0, 0))
    state_m = pl.BlockSpec((nb, 2, 1, HEAD_LANES), lambda p, j: (p, 0, 0, 0))
    out_specs = [pl.BlockSpec((nb, rows, D_REC), fwd(0)), pl.BlockSpec((nb, rows, D_REC), bwd(0))]
    out_shape = [jax.ShapeDtypeStruct((bsz, t, D_REC), F32), jax.ShapeDtypeStruct((bsz, t, D_REC), F32)]
    aliases = {}
    if zero_init:
        for tail in ((2, N_HEADS, D_STATE, D_STATE), (2, N_HEADS, D_STATE), (2, 1, HEAD_LANES)):
            shape, spec, _ = _layer_cache(tail, l, bsz, nb)
            out_shape.append(shape)
            out_specs.append(spec)
        if caches is not None:
            in_specs += [pl.BlockSpec(memory_space=pl.ANY)] * len(caches)
            aliases = {len(args) + i: 2 + i for i in range(len(caches))}
            args += list(caches)
    else:
        in_specs += [state_c, state_m]
        args += list(init)
    hf, hb, *state_out = pl.pallas_call(
        functools.partial(_mlstm_kernel, nblk=nblk, rows=rows, zero_init=zero_init, n_alias=len(aliases)),
        grid=(bsz // nb, nblk),
        in_specs=in_specs,
        out_specs=out_specs,
        out_shape=out_shape,
        input_output_aliases=aliases,
        scratch_shapes=[pltpu.VMEM((nb, 2, D_STATE, 2 * HEAD_LANES), F32),
                        pltpu.VMEM((nb, 2, 1, HEAD_LANES), F32)],
        compiler_params=_params(("parallel", "arbitrary")),
        name="mlstm_scan",
    )(*args)
    return (hf.reshape(bsz * t, D_REC), hb.reshape(bsz * t, D_REC), *state_out)


def _conv_kernel(x_ref, prev_ref, next_ref, w_ref, b_ref, o_ref, *, nblk):
    i = pl.program_id(1)
    x = x_ref[...]
    rows = x.shape[0]
    r = lax.broadcasted_iota(jnp.int32, x.shape, 0)
    prev_row = prev_ref[7:8, :] * (i > 0).astype(F32)
    next_row = next_ref[0:1, :] * (i < nblk - 1).astype(F32)
    xm = jnp.where(r == 0, prev_row, pltpu.roll(x, 1, 0))
    xp = jnp.where(r == rows - 1, next_row, pltpu.roll(x, rows - 1, 0))
    w = w_ref[...]
    o_ref[...] = _silu(xm * w[0:1] + x * w[1:2] + xp * w[2:3] + b_ref[...])


def _ssd_conv(u, conv_w, conv_b, bsz, t):
    rows = min(t, 512)
    nblk = t // rows
    r8 = rows // 8
    width = 2 * D_REC
    c0 = COL_SX // width
    return pl.pallas_call(
        functools.partial(_conv_kernel, nblk=nblk),
        grid=(bsz, nblk),
        in_specs=[pl.BlockSpec((rows, width), lambda b, i: (b * nblk + i, c0)),
                  pl.BlockSpec((8, width), lambda b, i: (jnp.maximum((b * nblk + i) * r8 - 1, 0), c0)),
                  pl.BlockSpec((8, width), lambda b, i: (jnp.minimum((b * nblk + i + 1) * r8,
                                                                     bsz * nblk * r8 - 1), c0)),
                  pl.BlockSpec((D_CONV, width), lambda b, i: (0, 0)),
                  pl.BlockSpec((1, width), lambda b, i: (0, 0))],
        out_specs=pl.BlockSpec((rows, width), lambda b, i: (b * nblk + i, 0)),
        out_shape=jax.ShapeDtypeStruct((bsz * t, width), F32),
        compiler_params=_params(("parallel", "parallel")),
        name="ssd_conv",
    )(u, u, u, conv_w, conv_b.reshape(1, width))


def _ssd_chunk(x4, bcm, dt128, da128, d, sg_prev, reads, block, tri, b_sel, s_sel, bd):
    chans = [GATE_DT + d * N_HEADS + h for h in range(N_HEADS)]
    dt = _spread(dt128, chans)
    da = _spread(da128, chans)
    ac = _cumsum_rows(tri[d], da)
    atot = jnp.sum(da, 0, keepdims=True)
    a_row = jnp.sum(reads[1 - d] * da, 0, keepdims=True)
    decay = jnp.exp(ac - a_row + block[d])
    bmat = bcm[:, :LANES].astype(BF16)
    cmat = bcm[:, LANES:].astype(BF16)
    bbd = b_sel * _stack_heads(bmat)
    g4 = lax.dot_general(cmat, bbd, NT_DIMS, preferred_element_type=F32)
    xbd = bd * _stack_heads((x4 * dt).astype(BF16))
    y = (jnp.dot((g4 * decay).astype(BF16), xbd, preferred_element_type=F32)
         + jnp.dot(cmat, sg_prev.astype(BF16), preferred_element_type=F32) * jnp.exp(ac))
    w = jnp.exp(atot - ac) * dt
    dsg = lax.dot_general(bmat, (x4 * w).astype(BF16), TN_DIMS, preferred_element_type=F32)
    sg_new = jnp.exp(atot) * sg_prev + s_sel * dsg
    return y, sg_new


def _ssd_kernel(*refs, nblk, rows, zero_init, n_alias):
    xf_ref, bcf_ref, gf_ref, xb_ref, bcb_ref, gb_ref, dtb_ref, alog_ref, dskip_ref = refs[:9]
    if not zero_init:
        s0_ref = refs[9]
    outs = refs[9 + (0 if zero_init else 1) + n_alias:-1]
    yf_ref, yb_ref = outs[:2]
    s_scr = refs[-1]
    j = pl.program_id(1)
    nchunk = rows // CHUNK

    @pl.when(j == 0)
    def _():
        if zero_init:
            s_scr[...] = jnp.zeros_like(s_scr)
        else:
            s_scr[...] = s0_ref[0]

    reads, block, tri, _ = _scan_consts()
    bd = _group_mask(HEAD_LANES, HEAD_LANES, 0, 0, BF16)
    b_sel = _group_mask(HEAD_LANES, LANES, 1, 0, BF16)
    s_sel = _group_mask(LANES, HEAD_LANES, 0, 1, F32)
    a_coef = -jnp.exp(alog_ref[...])

    state = [s_scr[d] for d in range(2)]
    for ci in range(nchunk):
        for d in range(2):
            x_ref, bc_ref, g_ref, y_ref = ((xf_ref, bcf_ref, gf_ref, yf_ref) if d == 0
                                           else (xb_ref, bcb_ref, gb_ref, yb_ref))
            cj = ci if d == 0 else nchunk - 1 - ci
            rs = slice(cj * CHUNK, (cj + 1) * CHUNK)
            dt128 = jax.nn.softplus(g_ref[rs, :] + dtb_ref[...])
            x4 = x_ref[rs, :]
            y, state[d] = _ssd_chunk(x4, bc_ref[rs, :], dt128, dt128 * a_coef, d, state[d],
                                     reads, block, tri, b_sel, s_sel, bd)
            if d == 0:
                y = y + dskip_ref[...] * x4
            y_ref[rs, :] = y
    for d in range(2):
        s_scr[d] = state[d]

    if zero_init:
        sout_ref = outs[2]

        @pl.when(j == nblk - 1)
        def _():
            for d in range(2):
                s_t = s_scr[d].T
                for h in range(N_HEADS):
                    g0 = (h // 2) * D_STATE
                    sout_ref[0, 0, d, h] = s_t[h * D_STATE:(h + 1) * D_STATE, g0:g0 + D_STATE]
            _zero_later_layers(sout_ref)


def _ssd(xbc, u, dt_bias_row, alog_row, dskip_row, bsz, t, init, l=0, cache=None):
    rows = min(t, SCAN_ROWS)
    nblk = t // rows
    zero_init = init is None

    def fwd(col):
        return lambda b, j: (b * nblk + j, col)

    def bwd(col):
        return lambda b, j: (b * nblk + nblk - 1 - j, col)

    in_specs = []
    for mk in (fwd, bwd):
        in_specs += [pl.BlockSpec((rows, D_REC), mk(0)), pl.BlockSpec((rows, D_REC), mk(1)),
                     pl.BlockSpec((rows, LANES), mk(COL_SMALL // LANES))]
    in_specs += [pl.BlockSpec((1, LANES), lambda b, j: (0, 0)),
                 pl.BlockSpec((1, LANES), lambda b, j: (0, 0)),
                 pl.BlockSpec((1, D_REC), lambda b, j: (0, 0))]
    args = [xbc, xbc, u, xbc, xbc, u, dt_bias_row, alog_row, dskip_row]
    out_specs = [pl.BlockSpec((rows, D_REC), fwd(0)), pl.BlockSpec((rows, D_REC), bwd(0))]
    out_shape = [jax.ShapeDtypeStruct((bsz * t, D_REC), F32), jax.ShapeDtypeStruct((bsz * t, D_REC), F32)]
    aliases = {}
    if zero_init:
        shape, spec, _ = _layer_cache((2, N_HEADS, D_STATE, D_STATE), l, bsz)
        out_shape.append(shape)
        out_specs.append(spec)
        if cache is not None:
            in_specs.append(pl.BlockSpec(memory_space=pl.ANY))
            aliases = {len(args): 2}
            args.append(cache)
    else:
        in_specs.append(pl.BlockSpec((1, 2, LANES, HEAD_LANES), lambda b, j: (b, 0, 0, 0)))
        args.append(init)
    return pl.pallas_call(
        functools.partial(_ssd_kernel, nblk=nblk, rows=rows, zero_init=zero_init, n_alias=len(aliases)),
        grid=(bsz, nblk),
        in_specs=in_specs,
        out_specs=out_specs,
        out_shape=out_shape,
        input_output_aliases=aliases,
        scratch_shapes=[pltpu.VMEM((2, LANES, HEAD_LANES), F32)],
        compiler_params=_params(("parallel", "arbitrary")),
        name="ssd_scan",
    )(*args)


def _outproj_kernel(att_ref, hf_ref, hb_ref, mo_ref, yf_ref, yb_ref, z_ref, w_ref, x_ref, gate_ref,
                    mnw_ref, snw_ref, lg_ref, lb_ref, o_ref):
    hh = hf_ref[...] + hb_ref[...]
    parts = []
    for h in range(N_HEADS):
        xh = hh[:, h * D_STATE:(h + 1) * D_STATE]
        mu = jnp.mean(xh, -1, keepdims=True)
        dlt = xh - mu
        var = jnp.mean(dlt * dlt, -1, keepdims=True)
        parts.append(dlt * lax.rsqrt(var + EPS))
    ml = jax.nn.sigmoid(mo_ref[...]) * jnp.concatenate(parts, axis=1) * mnw_ref[...]
    yz = (yf_ref[...] + yb_ref[...]) * _silu(z_ref[...])
    parts = []
    for grp in range(N_GROUPS):
        yg = yz[:, grp * LANES:(grp + 1) * LANES]
        parts.append(yg * lax.rsqrt(jnp.mean(yg * yg, -1, keepdims=True) + EPS))
    ssm = jnp.concatenate(parts, axis=1) * snw_ref[...]
    mixed = (jnp.dot(att_ref[...], w_ref[0:D_ATT], preferred_element_type=F32)
             + _bdot(ml, w_ref[D_ATT:D_ATT + D_REC])
             + _bdot(ssm, w_ref[D_ATT + D_REC:D_MODEL]))
    y = ALPHA * x_ref[...] + gate_ref[0] * mixed
    o_ref[...] = _layernorm_rows(y, lg_ref[...], lb_ref[...])


def _out_proj(att, hf, hb, yf, yb, u, w, x, gate, mnw, snw, lg, lb, rows_per_mod):
    n = x.shape[0]
    tpb = rows_per_mod // ROW_TILE
    row = lambda width, col: pl.BlockSpec((ROW_TILE, width), lambda i: (i, col))
    vec = lambda width: pl.BlockSpec((1, width), lambda i: (0, 0))
    return pl.pallas_call(
        _outproj_kernel,
        grid=(n // ROW_TILE,),
        in_specs=[row(D_ATT, 0), row(D_REC, 0), row(D_REC, 0), row(D_REC, COL_MO // D_REC),
                  row(D_REC, 0), row(D_REC, 0), row(D_REC, COL_SZ // D_REC),
                  pl.BlockSpec((D_MODEL, D_MODEL), lambda i: (0, 0)),
                  row(D_MODEL, 0),
                  pl.BlockSpec((1, 1, D_MODEL), lambda i: (i // tpb, 0, 0)),
                  vec(D_REC), vec(D_REC), vec(D_MODEL), vec(D_MODEL)],
        out_specs=row(D_MODEL, 0),
        out_shape=jax.ShapeDtypeStruct((n, D_MODEL), F32),
        compiler_params=_params(("parallel",)),
        name="out_proj",
    )(att, hf, hb, u, yf, yb, u, w, x, gate, mnw.reshape(1, D_REC), snw.reshape(1, D_REC),
      lg.reshape(1, D_MODEL), lb.reshape(1, D_MODEL))


FF_TILE = D_FF // 2
FFN_ROWS = 1024
FFN_SUB = 512
FFN_VMEM_LIMIT = 56 * 1024 * 1024


def _swiglu_partial(h, w1, w3, w2):
    a = jnp.dot(h, w1, preferred_element_type=F32)
    b = jnp.dot(h, w3, preferred_element_type=F32)
    return jnp.dot((_silu(a) * b).astype(BF16), w2, preferred_element_type=F32)


def _ffn_kernel(x_ref, sc_ref, sh_ref, gate_ref, w1_ref, w3_ref, w2_ref, lg_ref, lb_ref, o_ref):
    for r0 in range(0, FFN_ROWS, FFN_SUB):
        rows = slice(r0, r0 + FFN_SUB)
        x = x_ref[rows, :]
        h = (x * (1.0 + sc_ref[0]) + sh_ref[0]).astype(BF16)
        y = ALPHA * x + gate_ref[0] * _swiglu_partial(h, w1_ref[...], w3_ref[...], w2_ref[...])
        o_ref[rows, :] = _layernorm_rows(y, lg_ref[...], lb_ref[...])


def _ffn(x, sc, sh, gate, w1, w3, w2, lg, lb, rows_per_mod):
    n = x.shape[0]
    tpb = rows_per_mod // FFN_ROWS
    modspec = pl.BlockSpec((1, 1, D_MODEL), lambda i: (i // tpb, 0, 0))
    vec = pl.BlockSpec((1, D_MODEL), lambda i: (0, 0))
    once = pl.Buffered(1)
    return pl.pallas_call(
        _ffn_kernel,
        grid=(n // FFN_ROWS,),
        in_specs=[pl.BlockSpec((FFN_ROWS, D_MODEL), lambda i: (i, 0)), modspec, modspec, modspec,
                  pl.BlockSpec((D_MODEL, D_FF), lambda i: (0, 0), pipeline_mode=once),
                  pl.BlockSpec((D_MODEL, D_FF), lambda i: (0, 0), pipeline_mode=once),
                  pl.BlockSpec((D_FF, D_MODEL), lambda i: (0, 0), pipeline_mode=once), vec, vec],
        out_specs=pl.BlockSpec((FFN_ROWS, D_MODEL), lambda i: (i, 0)),
        out_shape=jax.ShapeDtypeStruct((n, D_MODEL), F32),
        compiler_params=_params(("parallel",), FFN_VMEM_LIMIT),
        name="ffn_dense",
    )(x, sc, sh, gate, w1, w3, w2, lg.reshape(1, D_MODEL), lb.reshape(1, D_MODEL))


def _router_kernel(x_ref, sc_ref, sh_ref, rw_ref, gates_ref, h_ref):
    h = x_ref[...] * (1.0 + sc_ref[0]) + sh_ref[0]
    h_ref[...] = h.astype(BF16)
    logits = jnp.dot(h, rw_ref[...], precision=HIGHEST, preferred_element_type=F32)
    lane = lax.broadcasted_iota(jnp.int32, logits.shape, 1)
    valid = lane < N_EXPERTS
    p = jnp.where(valid, _softmax_rows(jnp.where(valid, logits, -jnp.inf)), -2.0)
    p1 = jnp.max(p, -1, keepdims=True)
    i1 = jnp.min(jnp.where(p == p1, lane, LANES), -1, keepdims=True)
    rest = jnp.where(lane == i1, -1.0, p)
    p2 = jnp.max(rest, -1, keepdims=True)
    i2 = jnp.min(jnp.where(rest == p2, lane, LANES), -1, keepdims=True)
    tot = p1 + p2
    gates_ref[...] = jnp.where(lane == i1, p1 / tot, jnp.where(lane == i2, p2 / tot, 0.0))


def _router(x, sc, sh, router_w, rows_per_mod):
    n = x.shape[0]
    tpb = rows_per_mod // ROW_TILE
    modspec = pl.BlockSpec((1, 1, D_MODEL), lambda i: (i // tpb, 0, 0))
    rw = jnp.pad(router_w, ((0, 0), (0, LANES - N_EXPERTS)))
    return pl.pallas_call(
        _router_kernel,
        grid=(n // ROW_TILE,),
        in_specs=[pl.BlockSpec((ROW_TILE, D_MODEL), lambda i: (i, 0)), modspec, modspec,
                  pl.BlockSpec((D_MODEL, LANES), lambda i: (0, 0))],
        out_specs=[pl.BlockSpec((ROW_TILE, LANES), lambda i: (i, 0)),
                   pl.BlockSpec((ROW_TILE, D_MODEL), lambda i: (i, 0))],
        out_shape=[jax.ShapeDtypeStruct((n, LANES), F32), jax.ShapeDtypeStruct((n, D_MODEL), BF16)],
        compiler_params=_params(("parallel",)),
        name="router",
    )(x, sc, sh, rw)


MOE_ROWS = 1024
MOE_TILE = 128
MOE_MAX_TILES = MOE_ROWS // MOE_TILE
MOE_VMEM_LIMIT = 56 * 1024 * 1024


def _moe_kernel(h_ref, gates_ref, w1_ref, w3_ref, w2_ref, x_ref, gate_ref, lg_ref, lb_ref, o_ref,
                slot_scr, slott_scr, slotc_scr, gatec_scr, hs_scr, ys_scr):
    e = pl.program_id(1)
    j = pl.program_id(2)
    last_j = pl.num_programs(2) - 1

    @pl.when(jnp.logical_and(e == 0, j == 0))
    def _():
        r = lax.broadcasted_iota(jnp.int32, (MOE_ROWS, MOE_ROWS), 0)
        c = lax.broadcasted_iota(jnp.int32, (MOE_ROWS, MOE_ROWS), 1)
        before = jnp.where(c < r, 1.0, 0.0).astype(BF16)
        mask = gates_ref[...] != 0.0
        rank = jnp.dot(before, jnp.where(mask, 1.0, 0.0).astype(BF16), preferred_element_type=F32)
        slot = jnp.where(mask, rank, -1.0).astype(jnp.int32)
        slot_scr[...] = slot
        slott_scr[...] = slot.T
        o_ref[...] = jnp.zeros_like(o_ref)

    @pl.when(j == 0)
    def _():
        lane = lax.broadcasted_iota(jnp.int32, (MOE_ROWS, LANES), 1)
        slotc_scr[...] = jnp.max(jnp.where(lane == e, slot_scr[...], -1), -1, keepdims=True)
        gatec_scr[...] = jnp.sum(jnp.where(lane == e, gates_ref[...], 0.0), -1, keepdims=True)

    slot_col = slotc_scr[...]
    n_tiles = (jnp.max(slot_col) + MOE_TILE) // MOE_TILE

    def tile(k, carry):
        @pl.when(j == 0)
        def _():
            slot_row = slott_scr[pl.ds(e, 1), :]
            rr = lax.broadcasted_iota(jnp.int32, (MOE_TILE, MOE_ROWS), 0) + k * MOE_TILE
            pick = jnp.where(rr == slot_row, 1.0, 0.0).astype(BF16)
            hs_scr[k] = jnp.dot(pick, h_ref[...], preferred_element_type=F32).astype(BF16)
            ys_scr[k] = jnp.zeros((MOE_TILE, D_MODEL), F32)

        ys_scr[k] += _swiglu_partial(hs_scr[k], w1_ref[0], w3_ref[0], w2_ref[0])

        @pl.when(j == last_j)
        def _():
            cc = lax.broadcasted_iota(jnp.int32, (MOE_ROWS, 2 * MOE_TILE), 1)
            cc = jnp.where(cc >= MOE_TILE, cc - MOE_TILE, cc) + k * MOE_TILE
            put = jnp.where(slot_col == cc, 1.0, 0.0).astype(BF16)
            y = ys_scr[k]
            y_hi = y.astype(BF16)
            y_lo = (y - y_hi.astype(F32)).astype(BF16)
            back = jnp.dot(put, jnp.concatenate([y_hi, y_lo], axis=0), preferred_element_type=F32)
            o_ref[...] += gatec_scr[...] * back

        return carry

    lax.fori_loop(0, n_tiles, tile, 0)

    @pl.when(jnp.logical_and(e == pl.num_programs(1) - 1, j == last_j))
    def _():
        y = ALPHA * x_ref[...] + gate_ref[0] * o_ref[...]
        o_ref[...] = _layernorm_rows(y, lg_ref[...], lb_ref[...])


def _moe(h, gates, w1, w3, w2, x, gate, lg, lb, rows_per_mod):
    n = h.shape[0]
    tpb = rows_per_mod // MOE_ROWS
    vec = pl.BlockSpec((1, D_MODEL), lambda i, e, j: (0, 0))
    return pl.pallas_call(
        _moe_kernel,
        grid=(n // MOE_ROWS, N_EXPERTS, D_FF // FF_TILE),
        in_specs=[pl.BlockSpec((MOE_ROWS, D_MODEL), lambda i, e, j: (i, 0)),
                  pl.BlockSpec((MOE_ROWS, LANES), lambda i, e, j: (i, 0)),
                  pl.BlockSpec((1, D_MODEL, FF_TILE), lambda i, e, j: (e, 0, j)),
                  pl.BlockSpec((1, D_MODEL, FF_TILE), lambda i, e, j: (e, 0, j)),
                  pl.BlockSpec((1, FF_TILE, D_MODEL), lambda i, e, j: (e, j, 0)),
                  pl.BlockSpec((MOE_ROWS, D_MODEL), lambda i, e, j: (i, 0)),
                  pl.BlockSpec((1, 1, D_MODEL), lambda i, e, j: (i // tpb, 0, 0)), vec, vec],
        out_specs=pl.BlockSpec((MOE_ROWS, D_MODEL), lambda i, e, j: (i, 0)),
        out_shape=jax.ShapeDtypeStruct((n, D_MODEL), F32),
        scratch_shapes=[pltpu.VMEM((MOE_ROWS, LANES), jnp.int32), pltpu.VMEM((LANES, MOE_ROWS), jnp.int32),
                        pltpu.VMEM((MOE_ROWS, 1), jnp.int32), pltpu.VMEM((MOE_ROWS, 1), F32),
                        pltpu.VMEM((MOE_MAX_TILES, MOE_TILE, D_MODEL), BF16),
                        pltpu.VMEM((MOE_MAX_TILES, MOE_TILE, D_MODEL), F32)],
        compiler_params=_params(("parallel", "arbitrary", "arbitrary"), MOE_VMEM_LIMIT),
        name="moe",
    )(h, gates, w1, w3, w2, x, gate, lg.reshape(1, D_MODEL), lb.reshape(1, D_MODEL))


def _permute_w_in(w):
    pad = jnp.zeros((D_MODEL, U_COLS - ORIG_END), w.dtype)
    return jnp.concatenate([w[:, :ORIG_GATES], w[:, ORIG_SX:ORIG_DT], w[:, ORIG_SZ:ORIG_SX],
                            w[:, ORIG_GATES:ORIG_SZ], w[:, ORIG_DT:ORIG_END], pad], axis=1).astype(BF16)


def _small_row(vals, offset):
    v = vals.reshape(-1).astype(F32)
    return jnp.zeros((1, LANES), F32).at[0, offset:offset + v.shape[0]].set(v)


def _pack_mlstm_state(c, n, m):
    shape = c.shape[:2] + (D_STATE, HEAD_LANES)
    c_rows = jnp.swapaxes(c, 2, 3).reshape(shape)
    n_rows = jnp.broadcast_to(jnp.swapaxes(n, 2, 3)[..., None], c.shape[:2] + (D_STATE, N_HEADS, D_STATE))
    return (jnp.concatenate([c_rows, n_rows.reshape(shape)], axis=-1),
            jnp.repeat(m, D_STATE, axis=-1)[:, :, None, :])


def _pack_ssd_state(s):
    sel = (jnp.arange(N_GROUPS)[:, None] == jnp.arange(N_HEADS)[None, :] // 2).astype(F32)
    return jnp.einsum('bdhpn,gh->bdgnhp', s, sel).reshape(s.shape[:2] + (LANES, HEAD_LANES))


def _layer(x, mods, P, l, bsz, t, ctx, caches=None):
    sh1, sc1, g1, sh2, sc2, g2 = mods
    rows_per_mod = x.shape[0] // sh1.shape[0]
    lam_init = 0.8 - 0.6 * math.exp(-0.3 * l)
    u = _in_proj(x, sc1, sh1, P['w_in'][l], rows_per_mod)

    gate_bias = (_small_row(P['mlstm_gate_b'][l, 0], GATE_I) + _small_row(P['mlstm_gate_b'][l, 1], GATE_F))
    dt_bias = _small_row(P['ssm_dt_bias'][l], GATE_DT)
    alog = _small_row(P['ssm_A_log'][l], GATE_DT)
    dskip = jnp.repeat(P['ssm_D'][l].astype(F32), D_STATE).reshape(1, D_REC)

    if ctx is None:
        att, k_new, v_new = _attention_ctx(u, P['attn_lambda'][l], P['attn_norm_w'][l], lam_init, bsz, t, l,
                                           None if caches is None else caches[0:2])
        m_init = s_init = None
    else:
        ck, cv, c_c, c_n, c_m, c_s = ctx
        q, k, v = _rope_prep(u, bsz, t)
        k_all = jnp.concatenate([k, ck.astype(BF16)], axis=3)
        v_all = jnp.concatenate([v, cv.astype(BF16)], axis=2)
        att = _attention_lat(q, k_all, v_all, P['attn_lambda'][l], P['attn_norm_w'][l], lam_init, bsz, t)
        m_init = _pack_mlstm_state(c_c, c_n, c_m)
        s_init = _pack_ssd_state(c_s)
    hf, hb, *mlstm_caches = _mlstm(u, gate_bias, bsz, t, m_init, l, None if caches is None else caches[2:5])
    xbc = _ssd_conv(u, P['conv_w'][l], P['conv_b'][l], bsz, t)
    yf, yb, *ssd_caches = _ssd(xbc, u, dt_bias, alog, dskip, bsz, t, s_init, l,
                               None if caches is None else caches[5])

    x = _out_proj(att, hf, hb, yf, yb, u, P['w_out'][l], x, g1, P['mlstm_norm_w'][l], P['ssm_norm_w'][l],
                  P['ln_g'][l, 0], P['ln_b'][l, 0], rows_per_mod)
    if l % 2 == 0:
        x = _ffn(x, sc2, sh2, g2, P['ffn_w1'][l // 2], P['ffn_w3'][l // 2], P['ffn_w2'][l // 2],
                 P['ln_g'][l, 1], P['ln_b'][l, 1], rows_per_mod)
    else:
        gates, h2 = _router(x, sc2, sh2, P['router_w'][l // 2], rows_per_mod)
        x = _moe(h2, gates, P['moe_w1'][l // 2], P['moe_w3'][l // 2], P['moe_w2'][l // 2],
                 x, g2, P['ln_g'][l, 1], P['ln_b'][l, 1], rows_per_mod)
    if ctx is None:
        return x, (k_new, v_new, *mlstm_caches, *ssd_caches)
    return x, None


def kernel(x_prompt, x_sample, c, cache_attn_k, cache_attn_v, state_mlstm_C, state_mlstm_n, state_mlstm_m, state_ssm, c_ctx, w_ada, b_ada, w_in, w_out, attn_lambda, attn_norm_w, mlstm_gate_b, mlstm_norm_w, conv_w, conv_b, ssm_A_log, ssm_dt_bias, ssm_D, ssm_norm_w, ln_g, ln_b, ffn_w1, ffn_w3, ffn_w2, router_w, moe_w1, moe_w3, moe_w2):
    bsz, seq, _ = x_prompt.shape
    dbsz, dseq, _ = x_sample.shape
    P = dict(w_in=[_permute_w_in(w_in[l]) for l in range(DEPTH)], w_out=w_out.astype(BF16),
             attn_lambda=attn_lambda, attn_norm_w=attn_norm_w, mlstm_gate_b=mlstm_gate_b,
             mlstm_norm_w=mlstm_norm_w, conv_w=conv_w, conv_b=conv_b, ssm_A_log=ssm_A_log,
             ssm_dt_bias=ssm_dt_bias, ssm_D=ssm_D, ssm_norm_w=ssm_norm_w, ln_g=ln_g, ln_b=ln_b,
             ffn_w1=ffn_w1.astype(BF16), ffn_w3=ffn_w3.astype(BF16), ffn_w2=ffn_w2.astype(BF16),
             router_w=router_w, moe_w1=moe_w1.astype(BF16), moe_w3=moe_w3.astype(BF16),
             moe_w2=moe_w2.astype(BF16))

    cvec = jnp.zeros((8, D_MODEL), F32).at[0].set(c_ctx).at[1:1 + dbsz].set(c)
    mod = _modulation(cvec, w_ada, b_ada)

    def mods_for(l, lo, hi):
        return [mod[l, lo:hi, i * D_MODEL:(i + 1) * D_MODEL][:, None, :] for i in range(6)]

    y_prompt = x_prompt.reshape(bsz * seq, D_MODEL)
    caches = None
    for l in range(DEPTH):
        y_prompt, caches = _layer(y_prompt, mods_for(l, 0, 1), P, l, bsz, seq, None, caches)
    new_k, new_v, new_c, new_n, m_spread, new_s = caches

    y_sample = x_sample.reshape(dbsz * dseq, D_MODEL)
    for l in range(DEPTH):
        ctx = (cache_attn_k[:, l], cache_attn_v[:, l], state_mlstm_C[:, l], state_mlstm_n[:, l],
               state_mlstm_m[:, l], state_ssm[:, l])
        y_sample, _ = _layer(y_sample, mods_for(l, 1, 1 + dbsz), P, l, dbsz, dseq, ctx)

    return (y_prompt.reshape(bsz, seq, D_MODEL), y_sample.reshape(dbsz, dseq, D_MODEL),
            new_k, new_v, new_c, new_n, m_spread[:, :, :, 0, ::D_STATE], new_s)
```

```python
import functools
import math

import jax
import jax.numpy as jnp
from jax import lax
from jax.experimental import pallas as pl
from jax.experimental.pallas import tpu as pltpu

F32 = jnp.float32
BF16 = jnp.bfloat16
HIGHEST = lax.Precision.HIGHEST

D_MODEL = 1024
DEPTH = 2
GRID_W = 64
N_HEADS = 4
D_ATT = 512
D_HEAD_V = 128
D_QK = 64
D_REC = 256
D_STATE = 64
N_GROUPS = 2
D_CONV = 3
D_FF = 2816
N_EXPERTS = 8
ALPHA = (2.0 * DEPTH) ** 0.25
CHUNK = 64
ROPE_BASE = 10000.0
LOG2E = 1.4426950408889634
EPS = 1e-5

COL_AQ, COL_AK, COL_AV = 0, 512, 1024
COL_MQ, COL_MK, COL_MV, COL_MO = 1536, 1792, 2048, 2304
COL_SX, COL_SBC, COL_SZ = 2560, 2816, 3072
COL_SMALL = 3328
U_COLS = 3584
ORIG_GATES, ORIG_SZ, ORIG_SX, ORIG_DT, ORIG_END = 2560, 2576, 2832, 3344, 3352
GATE_I, GATE_F, GATE_DT = 0, 8, 16

LANES = 128
ROW_TILE = 512
SCAN_ROWS = 256
SCAN_BATCH = 2
VMEM_LIMIT = 48 * 1024 * 1024

NT_DIMS = (((1,), (1,)), ((), ()))
TN_DIMS = (((0,), (0,)), ((), ()))


def _params(sem, vmem=VMEM_LIMIT):
    return pltpu.CompilerParams(dimension_semantics=sem, vmem_limit_bytes=vmem)


def _silu(x):
    return x * jax.nn.sigmoid(x)


def _bdot(a, b):
    return jnp.dot(a.astype(BF16), b.astype(BF16), preferred_element_type=F32)


def _bdot_nt(a, b):
    return lax.dot_general(a.astype(BF16), b.astype(BF16), NT_DIMS, preferred_element_type=F32)


def _layernorm_rows(y, g, b):
    mu = jnp.mean(y, -1, keepdims=True)
    d = y - mu
    var = jnp.mean(d * d, -1, keepdims=True)
    return d * lax.rsqrt(var + EPS) * g + b


def _mod_kernel(c_ref, w_ref, b_ref, o_ref):
    o_ref[0] = jnp.dot(_silu(c_ref[...]), w_ref[0], precision=HIGHEST,
                       preferred_element_type=F32) + b_ref[0]


def _modulation(cvec, w_ada, b_ada):
    tn = 1536
    return pl.pallas_call(
        _mod_kernel,
        grid=(DEPTH, 6 * D_MODEL // tn),
        in_specs=[pl.BlockSpec((8, D_MODEL), lambda l, j: (0, 0)),
                  pl.BlockSpec((1, D_MODEL, tn), lambda l, j: (l, 0, j)),
                  pl.BlockSpec((1, 1, tn), lambda l, j: (l, 0, j))],
        out_specs=pl.BlockSpec((1, 8, tn), lambda l, j: (l, 0, j)),
        out_shape=jax.ShapeDtypeStruct((DEPTH, 8, 6 * D_MODEL), F32),
        compiler_params=_params(("parallel", "parallel")),
        name="modulation",
    )(cvec, w_ada, b_ada.reshape(DEPTH, 1, 6 * D_MODEL))


def _inproj_kernel(x_ref, sc_ref, sh_ref, w_ref, o_ref):
    h = (x_ref[...] * (1.0 + sc_ref[0]) + sh_ref[0]).astype(BF16)
    for n0 in range(0, U_COLS, 512):
        o_ref[:, n0:n0 + 512] = jnp.dot(h, w_ref[:, n0:n0 + 512], preferred_element_type=F32)


def _in_proj(x, sc, sh, w, rows_per_mod):
    n = x.shape[0]
    tpb = rows_per_mod // ROW_TILE
    return pl.pallas_call(
        _inproj_kernel,
        grid=(n // ROW_TILE,),
        in_specs=[pl.BlockSpec((ROW_TILE, D_MODEL), lambda i: (i, 0)),
                  pl.BlockSpec((1, 1, D_MODEL), lambda i: (i // tpb, 0, 0)),
                  pl.BlockSpec((1, 1, D_MODEL), lambda i: (i // tpb, 0, 0)),
                  pl.BlockSpec((D_MODEL, U_COLS), lambda i: (0, 0))],
        out_specs=pl.BlockSpec((ROW_TILE, U_COLS), lambda i: (i, 0)),
        out_shape=jax.ShapeDtypeStruct((n, U_COLS), F32),
        compiler_params=_params(("parallel",)),
        name="in_proj",
    )(x, sc, sh, w)


def _lambda_scalar(lam_ref, lam_init):
    lp = lam_ref[...]
    s01 = jnp.sum(lp[0:1] * lp[1:2], axis=-1, keepdims=True)
    s23 = jnp.sum(lp[2:3] * lp[3:4], axis=-1, keepdims=True)
    return jnp.exp(s01) - jnp.exp(s23) + lam_init


def _softmax_rows(s):
    e = jnp.exp(s - jnp.max(s, -1, keepdims=True))
    return e / jnp.sum(e, -1, keepdims=True)


def _head_norm(o, nw, lam_init):
    return o * lax.rsqrt(jnp.mean(o * o, -1, keepdims=True) + EPS) * nw * (1.0 - lam_init)


def _layer_cache(tail, l, bsz, nb=1):
    zeros = (0,) * len(tail)
    owned = DEPTH if l == 0 else 1
    spec = pl.BlockSpec((nb, owned) + tuple(tail), lambda b, *_: (b, l) + zeros)
    return jax.ShapeDtypeStruct((bsz, DEPTH) + tuple(tail), F32), spec, owned


def _zero_later_layers(ref):
    for b in range(ref.shape[0]):
        for later in range(1, ref.shape[1]):
            ref[b, later] = jnp.zeros(ref.shape[2:], ref.dtype)


def _attn_ctx_kernel(u_ref, lam_ref, nw_ref, *rest, lam_init):
    att_ref, k_ref, v_ref = rest[-3:]
    lam = _lambda_scalar(lam_ref, lam_init)
    for h in range(N_HEADS):
        v = u_ref[:, COL_AV + h * D_HEAD_V:COL_AV + (h + 1) * D_HEAD_V]
        v_ref[0, 0, h] = v
        ps = []
        for m in range(2):
            c0 = h * D_HEAD_V + m * D_QK
            q = u_ref[:, COL_AQ + c0:COL_AQ + c0 + D_QK] * (D_QK ** -0.5)
            k = u_ref[:, COL_AK + c0:COL_AK + c0 + D_QK]
            k_ref[0, 0, h, m] = k
            ps.append(_softmax_rows(_bdot_nt(q, k)))
        o = _bdot(ps[0] - lam * ps[1], v)
        att_ref[:, h * D_HEAD_V:(h + 1) * D_HEAD_V] = _head_norm(o, nw_ref[...], lam_init).astype(BF16)
    _zero_later_layers(k_ref)
    _zero_later_layers(v_ref)


def _attention_ctx(u, lam_p, norm_w, lam_init, bsz, t, l, caches):
    n = bsz * t
    k_shape, k_spec, _ = _layer_cache((N_HEADS, 2, t, D_QK), l, bsz)
    v_shape, v_spec, _ = _layer_cache((N_HEADS, t, D_HEAD_V), l, bsz)
    in_specs = [pl.BlockSpec((t, 3 * D_ATT), lambda b: (b, 0)),
                pl.BlockSpec((4, D_QK), lambda b: (0, 0)),
                pl.BlockSpec((1, D_HEAD_V), lambda b: (0, 0))]
    args = [u, lam_p, norm_w.reshape(1, D_HEAD_V)]
    aliases = {}
    if caches is not None:
        in_specs += [pl.BlockSpec(memory_space=pl.ANY)] * 2
        aliases = {len(args): 1, len(args) + 1: 2}
        args += list(caches)
    return pl.pallas_call(
        functools.partial(_attn_ctx_kernel, lam_init=lam_init),
        grid=(bsz,),
        in_specs=in_specs,
        out_specs=[pl.BlockSpec((t, D_ATT), lambda b: (b, 0)), k_spec, v_spec],
        out_shape=[jax.ShapeDtypeStruct((n, D_ATT), BF16), k_shape, v_shape],
        input_output_aliases=aliases,
        compiler_params=_params(("parallel",)),
        name="attn_ctx",
    )(*args)


def _rope_kernel(u_ref, cos_ref, sa_ref, sb_ref, q_ref, k_ref, v_ref):
    cos, sa, sb = cos_ref[...], sa_ref[...], sb_ref[...]

    def rope(x):
        return x * cos + pltpu.roll(x, LANES - 16, 1) * sa + pltpu.roll(x, 16, 1) * sb

    for h in range(N_HEADS):
        q = rope(u_ref[:, COL_AQ + h * D_HEAD_V:COL_AQ + (h + 1) * D_HEAD_V]) * (LOG2E * D_QK ** -0.5)
        k = rope(u_ref[:, COL_AK + h * D_HEAD_V:COL_AK + (h + 1) * D_HEAD_V])
        for m in range(2):
            q_ref[0, h, m] = q[:, m * D_QK:(m + 1) * D_QK].astype(BF16)
            k_ref[0, h, m] = k[:, m * D_QK:(m + 1) * D_QK].astype(BF16)
        v_ref[0, h] = u_ref[:, COL_AV + h * D_HEAD_V:COL_AV + (h + 1) * D_HEAD_V].astype(BF16)


def _rope_tables(t):
    rows = jnp.repeat(jnp.arange(t // GRID_W, dtype=F32), GRID_W)
    cols = jnp.tile(jnp.arange(GRID_W, dtype=F32), t // GRID_W)
    half = D_QK // 2
    inv = ROPE_BASE ** (-jnp.arange(0, half, 2, dtype=F32) / half)
    ang_r = rows[:, None] * inv
    ang_c = cols[:, None] * inv
    ang = jnp.concatenate([ang_r, ang_r, ang_c, ang_c], -1)
    cos, sin = jnp.cos(ang), jnp.sin(ang)
    quarter = (jnp.arange(D_QK) // (D_QK // 4)) % 2
    sa = jnp.where(quarter == 0, -sin, 0.0)
    sb = jnp.where(quarter == 1, sin, 0.0)
    tile2 = lambda a: jnp.concatenate([a, a], -1)
    return tile2(cos), tile2(sa), tile2(sb)


def _rope_prep(u, bsz, t):
    tr = 512
    nb = t // tr
    cos, sa, sb = _rope_tables(t)
    tab = pl.BlockSpec((tr, LANES), lambda b, i: (i, 0))
    return pl.pallas_call(
        _rope_kernel,
        grid=(bsz, nb),
        in_specs=[pl.BlockSpec((tr, 3 * D_ATT), lambda b, i: (b * nb + i, 0)), tab, tab, tab],
        out_specs=[pl.BlockSpec((1, N_HEADS, 2, tr, D_QK), lambda b, i: (b, 0, 0, i, 0)),
                   pl.BlockSpec((1, N_HEADS, 2, tr, D_QK), lambda b, i: (b, 0, 0, i, 0)),
                   pl.BlockSpec((1, N_HEADS, tr, D_HEAD_V), lambda b, i: (b, 0, i, 0))],
        out_shape=[jax.ShapeDtypeStruct((bsz, N_HEADS, 2, t, D_QK), BF16),
                   jax.ShapeDtypeStruct((bsz, N_HEADS, 2, t, D_QK), BF16),
                   jax.ShapeDtypeStruct((bsz, N_HEADS, t, D_HEAD_V), BF16)],
        compiler_params=_params(("parallel", "parallel")),
        name="rope_prep",
    )(u, cos, sa, sb)


def _attn_lat_kernel(q_ref, k_ref, v_ref, lam_ref, nw_ref, o_ref, *, lam_init):
    lam = _lambda_scalar(lam_ref, lam_init)
    es, sums = [], []
    for m in range(2):
        s = lax.dot_general(q_ref[0, 0, m], k_ref[0, 0, m], NT_DIMS, preferred_element_type=F32)
        e = jnp.exp2(s - jnp.max(s, -1, keepdims=True))
        es.append(e)
        sums.append(jnp.sum(e, -1, keepdims=True))
    a = es[0] - (lam * sums[0] / sums[1]) * es[1]
    o = _bdot(a, v_ref[0, 0]) / sums[0]
    o_ref[...] = _head_norm(o, nw_ref[...], lam_init).astype(BF16)


def _attention_lat(q, k_all, v_all, lam_p, norm_w, lam_init, bsz, t):
    tq = 256
    nq = t // tq
    s = k_all.shape[3]
    return pl.pallas_call(
        functools.partial(_attn_lat_kernel, lam_init=lam_init),
        grid=(bsz, N_HEADS, nq),
        in_specs=[pl.BlockSpec((1, 1, 2, tq, D_QK), lambda b, h, i: (b, h, 0, i, 0)),
                  pl.BlockSpec((1, 1, 2, s, D_QK), lambda b, h, i: (b, h, 0, 0, 0)),
                  pl.BlockSpec((1, 1, s, D_HEAD_V), lambda b, h, i: (b, h, 0, 0)),
                  pl.BlockSpec((4, D_QK), lambda b, h, i: (0, 0)),
                  pl.BlockSpec((1, D_HEAD_V), lambda b, h, i: (0, 0))],
        out_specs=pl.BlockSpec((tq, D_HEAD_V), lambda b, h, i: (b * nq + i, h)),
        out_shape=jax.ShapeDtypeStruct((bsz * t, D_ATT), BF16),
        compiler_params=_params(("parallel", "parallel", "parallel")),
        name="attn_lat",
    )(q, k_all, v_all, lam_p, norm_w.reshape(1, D_HEAD_V))


HEAD_LANES = N_HEADS * D_STATE


def _scan_consts():
    t = lax.broadcasted_iota(jnp.int32, (CHUNK, HEAD_LANES), 0)
    s = lax.broadcasted_iota(jnp.int32, (CHUNK, HEAD_LANES), 1) & (CHUNK - 1)
    r = lax.broadcasted_iota(jnp.int32, (CHUNK, CHUNK), 0)
    c = lax.broadcasted_iota(jnp.int32, (CHUNK, CHUNK), 1)
    reads = (jnp.where(s <= t, 1.0, 0.0), jnp.where(s >= t, 1.0, 0.0))
    block = (jnp.where(s <= t, 0.0, -jnp.inf), jnp.where(s >= t, 0.0, -jnp.inf))
    tri = (jnp.where(c <= r, 1.0, 0.0).astype(BF16), jnp.where(c >= r, 1.0, 0.0).astype(BF16))
    return reads, block, tri, jnp.where(s == t, 1.0, 0.0)


def _cumsum_rows(tri, x):
    hi = x.astype(BF16)
    rest = x - hi.astype(F32)
    mid = rest.astype(BF16)
    lo = (rest - mid.astype(F32)).astype(BF16)
    parts = jnp.dot(tri, jnp.concatenate([hi, mid, lo], axis=1), preferred_element_type=F32)
    return parts[:, :HEAD_LANES] + parts[:, HEAD_LANES:2 * HEAD_LANES] + parts[:, 2 * HEAD_LANES:]


def _group_mask(rows, cols, row_shift, col_shift, dtype):
    r = lax.broadcasted_iota(jnp.int32, (rows, cols), 0) >> 6
    c = (lax.broadcasted_iota(jnp.int32, (rows, cols), 1) >> 6) & (N_HEADS - 1)
    return jnp.where((r >> row_shift) == (c >> col_shift), 1.0, 0.0).astype(dtype)


def _spread(x, chans):
    return jnp.concatenate([jnp.broadcast_to(x[:, c:c + 1], (CHUNK, D_STATE)) for c in chans], axis=1)


def _stack_heads(x):
    return jnp.concatenate([x] * N_HEADS, axis=0)


def _mlstm_chunk(q4, k4, v4, g, d, cn_prev, m_prev, reads, block, tri, eye, bd, bd_f32):
    li = _spread(g, [GATE_I + d * N_HEADS + h for h in range(N_HEADS)])
    lf = _spread(jax.nn.log_sigmoid(g), [GATE_F + d * N_HEADS + h for h in range(N_HEADS)])
    bc = _cumsum_rows(tri[d], lf)
    btot = jnp.sum(lf, 0, keepdims=True)
    b_row = jnp.sum(reads[1 - d] * lf, 0, keepdims=True)
    li_row = jnp.sum(eye * li, 0, keepdims=True)
    dm = bc - b_row + li_row + block[d]
    rmax = jnp.concatenate(
        [jnp.broadcast_to(jnp.max(dm[:, h * D_STATE:(h + 1) * D_STATE], -1, keepdims=True), (CHUNK, D_STATE))
         for h in range(N_HEADS)], axis=1)
    inter = bc + m_prev
    m_t = jnp.maximum(inter, rmax)
    w_inter = jnp.exp(inter - m_t)
    qs = (q4 * (D_STATE ** -0.5)).astype(BF16)
    kbd = bd[:, :HEAD_LANES] * _stack_heads(k4.astype(BF16))
    s4 = lax.dot_general(qs, kbd, NT_DIMS, preferred_element_type=F32) * jnp.exp(dm - m_t)
    vo = jnp.concatenate([v4.astype(BF16), jnp.ones((CHUNK, HEAD_LANES), BF16)], axis=1)
    vbd = bd * _stack_heads(vo)
    nd = (jnp.concatenate([w_inter, w_inter], axis=1)
          * jnp.dot(qs, bd * _stack_heads(cn_prev.astype(BF16)), preferred_element_type=F32)
          + jnp.dot(s4.astype(BF16), vbd, preferred_element_type=F32))
    hc = nd[:, :HEAD_LANES] / jnp.maximum(jnp.abs(nd[:, HEAD_LANES:]), jnp.exp(-m_t))
    gcol = btot - bc + li
    m_new = jnp.maximum(btot + m_prev, jnp.max(gcol, 0, keepdims=True))
    w_c = jnp.exp(btot + m_prev - m_new)
    kw = (k4 * jnp.exp(gcol - m_new)).astype(BF16)
    dcn = lax.dot_general(kw, vo, TN_DIMS, preferred_element_type=F32)
    own = sum(bd_f32[h * D_STATE:(h + 1) * D_STATE] * dcn[h * D_STATE:(h + 1) * D_STATE]
              for h in range(N_HEADS))
    cn_new = jnp.concatenate([w_c, w_c], axis=1) * cn_prev + own
    return hc, cn_new, m_new


def _mlstm_kernel(*refs, nblk, rows, zero_init, n_alias):
    qf_ref, kf_ref, vf_ref, gf_ref, qb_ref, kb_ref, vb_ref, gb_ref, bias_ref = refs[:9]
    if not zero_init:
        c0_ref, m0_ref = refs[9:11]
    outs = refs[9 + (0 if zero_init else 2) + n_alias:-2]
    hf_ref, hb_ref = outs[:2]
    cn_scr, m_scr = refs[-2:]
    j = pl.program_id(1)
    nchunk = rows // CHUNK

    @pl.when(j == 0)
    def _():
        if zero_init:
            cn_scr[...] = jnp.zeros_like(cn_scr)
            m_scr[...] = jnp.zeros_like(m_scr)
        else:
            cn_scr[...] = c0_ref[...]
            m_scr[...] = m0_ref[...]

    reads, block, tri, eye = _scan_consts()
    bd = _group_mask(HEAD_LANES, 2 * HEAD_LANES, 0, 0, BF16)
    bd_f32 = _group_mask(HEAD_LANES, 2 * HEAD_LANES, 0, 0, F32)

    chains = [(b, d) for b in range(SCAN_BATCH) for d in range(2)]
    state = {bd_: (cn_scr[bd_], m_scr[bd_]) for bd_ in chains}
    for ci in range(nchunk):
        for b, d in chains:
            q_ref, k_ref, v_ref, g_ref, h_ref = ((qf_ref, kf_ref, vf_ref, gf_ref, hf_ref) if d == 0
                                                 else (qb_ref, kb_ref, vb_ref, gb_ref, hb_ref))
            cj = ci if d == 0 else nchunk - 1 - ci
            rs = slice(cj * CHUNK, (cj + 1) * CHUNK)
            g = g_ref[b, rs, :] + bias_ref[...]
            hc, cn_new, m_new = _mlstm_chunk(q_ref[b, rs, :], k_ref[b, rs, :], v_ref[b, rs, :], g, d,
                                             *state[b, d], reads, block, tri, eye, bd, bd_f32)
            h_ref[b, rs, :] = hc
            state[b, d] = (cn_new, m_new)
    for bd_ in chains:
        cn_scr[bd_], m_scr[bd_] = state[bd_]

    if zero_init:
        cout_ref, nout_ref, mout_ref = outs[2:]

        @pl.when(j == nblk - 1)
        def _():
            r = lax.broadcasted_iota(jnp.int32, (D_STATE, D_STATE), 0)
            c = lax.broadcasted_iota(jnp.int32, (D_STATE, D_STATE), 1)
            for b, d in chains:
                for h in range(N_HEADS):
                    r0, r1 = h * D_STATE, (h + 1) * D_STATE
                    cout_ref[b, 0, d, h] = cn_scr[b, d, :, r0:r1]
                    n_spread = cn_scr[b, d, :, HEAD_LANES + r0:HEAD_LANES + r1]
                    nout_ref[b, 0, d, h:h + 1, :] = jnp.sum(jnp.where(r == c, n_spread, 0.0), 0, keepdims=True)
                mout_ref[b, 0, d] = m_scr[b, d]
            _zero_later_layers(cout_ref)
            _zero_later_layers(nout_ref)
            _zero_later_layers(mout_ref)


def _mlstm(u, gate_bias, bsz, t, init, l=0, caches=None):
    rows = min(t, SCAN_ROWS)
    nblk = t // rows
    zero_init = init is None

    nb = SCAN_BATCH
    fwd = lambda col: (lambda p, j: (p, j, col))
    bwd = lambda col: (lambda p, j: (p, nblk - 1 - j, col))
    cq, ck, cv, cg = COL_MQ // D_REC, COL_MK // D_REC, COL_MV // D_REC, COL_SMALL // LANES
    in_specs = []
    for mk in (fwd, bwd):
        in_specs += [pl.BlockSpec((nb, rows, D_REC), mk(cq)), pl.BlockSpec((nb, rows, D_REC), mk(ck)),
                     pl.BlockSpec((nb, rows, D_REC), mk(cv)), pl.BlockSpec((nb, rows, LANES), mk(cg))]
    in_specs.append(pl.BlockSpec((1, LANES), lambda p, j: (0, 0)))
    args = [u.reshape(bsz, t, U_COLS)] * 8 + [gate_bias]
    state_c = pl.BlockSpec((nb, 2, D_STATE, 2 * HEAD_LANES), lambda p, j: (p, 0, 0, 0))
    state_m = pl.BlockSpec((nb, 2, 1, HEAD_LANES), lambda p, j: (p, 0, 0, 0))
    out_specs = [pl.BlockSpec((nb, rows, D_REC), fwd(0)), pl.BlockSpec((nb, rows, D_REC), bwd(0))]
    out_shape = [jax.ShapeDtypeStruct((bsz, t, D_REC), F32), jax.ShapeDtypeStruct((bsz, t, D_REC), F32)]
    aliases = {}
    if zero_init:
        for tail in ((2, N_HEADS, D_STATE, D_STATE), (2, N_HEADS, D_STATE), (2, 1, HEAD_LANES)):
            shape, spec, _ = _layer_cache(tail, l, bsz, nb)
            out_shape.append(shape)
            out_specs.append(spec)
        if caches is not None:
            in_specs += [pl.BlockSpec(memory_space=pl.ANY)] * len(caches)
            aliases = {len(args) + i: 2 + i for i in range(len(caches))}
            args += list(caches)
    else:
        in_specs += [state_c, state_m]
        args += list(init)
    hf, hb, *state_out = pl.pallas_call(
        functools.partial(_mlstm_kernel, nblk=nblk, rows=rows, zero_init=zero_init, n_alias=len(aliases)),
        grid=(bsz // nb, nblk),
        in_specs=in_specs,
        out_specs=out_specs,
        out_shape=out_shape,
        input_output_aliases=aliases,
        scratch_shapes=[pltpu.VMEM((nb, 2, D_STATE, 2 * HEAD_LANES), F32),
                        pltpu.VMEM((nb, 2, 1, HEAD_LANES), F32)],
        compiler_params=_params(("parallel", "arbitrary")),
        name="mlstm_scan",
    )(*args)
    return (hf.reshape(bsz * t, D_REC), hb.reshape(bsz * t, D_REC), *state_out)


def _conv_kernel(x_ref, prev_ref, next_ref, w_ref, b_ref, o_ref, *, nblk):
    i = pl.program_id(1)
    x = x_ref[...]
    rows = x.shape[0]
    r = lax.broadcasted_iota(jnp.int32, x.shape, 0)
    prev_row = prev_ref[7:8, :] * (i > 0).astype(F32)
    next_row = next_ref[0:1, :] * (i < nblk - 1).astype(F32)
    xm = jnp.where(r == 0, prev_row, pltpu.roll(x, 1, 0))
    xp = jnp.where(r == rows - 1, next_row, pltpu.roll(x, rows - 1, 0))
    w = w_ref[...]
    o_ref[...] = _silu(xm * w[0:1] + x * w[1:2] + xp * w[2:3] + b_ref[...])


def _ssd_conv(u, conv_w, conv_b, bsz, t):
    rows = min(t, 512)
    nblk = t // rows
    r8 = rows // 8
    width = 2 * D_REC
    c0 = COL_SX // width
    return pl.pallas_call(
        functools.partial(_conv_kernel, nblk=nblk),
        grid=(bsz, nblk),
        in_specs=[pl.BlockSpec((rows, width), lambda b, i: (b * nblk + i, c0)),
                  pl.BlockSpec((8, width), lambda b, i: (jnp.maximum((b * nblk + i) * r8 - 1, 0), c0)),
                  pl.BlockSpec((8, width), lambda b, i: (jnp.minimum((b * nblk + i + 1) * r8,
                                                                     bsz * nblk * r8 - 1), c0)),
                  pl.BlockSpec((D_CONV, width), lambda b, i: (0, 0)),
                  pl.BlockSpec((1, width), lambda b, i: (0, 0))],
        out_specs=pl.BlockSpec((rows, width), lambda b, i: (b * nblk + i, 0)),
        out_shape=jax.ShapeDtypeStruct((bsz * t, width), F32),
        compiler_params=_params(("parallel", "parallel")),
        name="ssd_conv",
    )(u, u, u, conv_w, conv_b.reshape(1, width))


def _ssd_chunk(x4, bcm, dt128, da128, d, sg_prev, reads, block, tri, b_sel, s_sel, bd):
    chans = [GATE_DT + d * N_HEADS + h for h in range(N_HEADS)]
    dt = _spread(dt128, chans)
    da = _spread(da128, chans)
    ac = _cumsum_rows(tri[d], da)
    atot = jnp.sum(da, 0, keepdims=True)
    a_row = jnp.sum(reads[1 - d] * da, 0, keepdims=True)
    decay = jnp.exp(ac - a_row + block[d])
    bmat = bcm[:, :LANES].astype(BF16)
    cmat = bcm[:, LANES:].astype(BF16)
    bbd = b_sel * _stack_heads(bmat)
    g4 = lax.dot_general(cmat, bbd, NT_DIMS, preferred_element_type=F32)
    xbd = bd * _stack_heads((x4 * dt).astype(BF16))
    y = (jnp.dot((g4 * decay).astype(BF16), xbd, preferred_element_type=F32)
         + jnp.dot(cmat, sg_prev.astype(BF16), preferred_element_type=F32) * jnp.exp(ac))
    w = jnp.exp(atot - ac) * dt
    dsg = lax.dot_general(bmat, (x4 * w).astype(BF16), TN_DIMS, preferred_element_type=F32)
    sg_new = jnp.exp(atot) * sg_prev + s_sel * dsg
    return y, sg_new


def _ssd_kernel(*refs, nblk, rows, zero_init, n_alias):
    xf_ref, bcf_ref, gf_ref, xb_ref, bcb_ref, gb_ref, dtb_ref, alog_ref, dskip_ref = refs[:9]
    if not zero_init:
        s0_ref = refs[9]
    outs = refs[9 + (0 if zero_init else 1) + n_alias:-1]
    yf_ref, yb_ref = outs[:2]
    s_scr = refs[-1]
    j = pl.program_id(1)
    nchunk = rows // CHUNK

    @pl.when(j == 0)
    def _():
        if zero_init:
            s_scr[...] = jnp.zeros_like(s_scr)
        else:
            s_scr[...] = s0_ref[0]

    reads, block, tri, _ = _scan_consts()
    bd = _group_mask(HEAD_LANES, HEAD_LANES, 0, 0, BF16)
    b_sel = _group_mask(HEAD_LANES, LANES, 1, 0, BF16)
    s_sel = _group_mask(LANES, HEAD_LANES, 0, 1, F32)
    a_coef = -jnp.exp(alog_ref[...])

    state = [s_scr[d] for d in range(2)]
    for ci in range(nchunk):
        for d in range(2):
            x_ref, bc_ref, g_ref, y_ref = ((xf_ref, bcf_ref, gf_ref, yf_ref) if d == 0
                                           else (xb_ref, bcb_ref, gb_ref, yb_ref))
            cj = ci if d == 0 else nchunk - 1 - ci
            rs = slice(cj * CHUNK, (cj + 1) * CHUNK)
            dt128 = jax.nn.softplus(g_ref[rs, :] + dtb_ref[...])
            x4 = x_ref[rs, :]
            y, state[d] = _ssd_chunk(x4, bc_ref[rs, :], dt128, dt128 * a_coef, d, state[d],
                                     reads, block, tri, b_sel, s_sel, bd)
            if d == 0:
                y = y + dskip_ref[...] * x4
            y_ref[rs, :] = y
    for d in range(2):
        s_scr[d] = state[d]

    if zero_init:
        sout_ref = outs[2]

        @pl.when(j == nblk - 1)
        def _():
            for d in range(2):
                s_t = s_scr[d].T
                for h in range(N_HEADS):
                    g0 = (h // 2) * D_STATE
                    sout_ref[0, 0, d, h] = s_t[h * D_STATE:(h + 1) * D_STATE, g0:g0 + D_STATE]
            _zero_later_layers(sout_ref)


def _ssd(xbc, u, dt_bias_row, alog_row, dskip_row, bsz, t, init, l=0, cache=None):
    rows = min(t, SCAN_ROWS)
    nblk = t // rows
    zero_init = init is None

    def fwd(col):
        return lambda b, j: (b * nblk + j, col)

    def bwd(col):
        return lambda b, j: (b * nblk + nblk - 1 - j, col)

    in_specs = []
    for mk in (fwd, bwd):
        in_specs += [pl.BlockSpec((rows, D_REC), mk(0)), pl.BlockSpec((rows, D_REC), mk(1)),
                     pl.BlockSpec((rows, LANES), mk(COL_SMALL // LANES))]
    in_specs += [pl.BlockSpec((1, LANES), lambda b, j: (0, 0)),
                 pl.BlockSpec((1, LANES), lambda b, j: (0, 0)),
                 pl.BlockSpec((1, D_REC), lambda b, j: (0, 0))]
    args = [xbc, xbc, u, xbc, xbc, u, dt_bias_row, alog_row, dskip_row]
    out_specs = [pl.BlockSpec((rows, D_REC), fwd(0)), pl.BlockSpec((rows, D_REC), bwd(0))]
    out_shape = [jax.ShapeDtypeStruct((bsz * t, D_REC), F32), jax.ShapeDtypeStruct((bsz * t, D_REC), F32)]
    aliases = {}
    if zero_init:
        shape, spec, _ = _layer_cache((2, N_HEADS, D_STATE, D_STATE), l, bsz)
        out_shape.append(shape)
        out_specs.append(spec)
        if cache is not None:
            in_specs.append(pl.BlockSpec(memory_space=pl.ANY))
            aliases = {len(args): 2}
            args.append(cache)
    else:
        in_specs.append(pl.BlockSpec((1, 2, LANES, HEAD_LANES), lambda b, j: (b, 0, 0, 0)))
        args.append(init)
    return pl.pallas_call(
        functools.partial(_ssd_kernel, nblk=nblk, rows=rows, zero_init=zero_init, n_alias=len(aliases)),
        grid=(bsz, nblk),
        in_specs=in_specs,
        out_specs=out_specs,
        out_shape=out_shape,
        input_output_aliases=aliases,
        scratch_shapes=[pltpu.VMEM((2, LANES, HEAD_LANES), F32)],
        compiler_params=_params(("parallel", "arbitrary")),
        name="ssd_scan",
    )(*args)


def _outproj_kernel(att_ref, hf_ref, hb_ref, mo_ref, yf_ref, yb_ref, z_ref, w_ref, x_ref, gate_ref,
                    mnw_ref, snw_ref, lg_ref, lb_ref, o_ref):
    hh = hf_ref[...] + hb_ref[...]
    parts = []
    for h in range(N_HEADS):
        xh = hh[:, h * D_STATE:(h + 1) * D_STATE]
        mu = jnp.mean(xh, -1, keepdims=True)
        dlt = xh - mu
        var = jnp.mean(dlt * dlt, -1, keepdims=True)
        parts.append(dlt * lax.rsqrt(var + EPS))
    ml = jax.nn.sigmoid(mo_ref[...]) * jnp.concatenate(parts, axis=1) * mnw_ref[...]
    yz = (yf_ref[...] + yb_ref[...]) * _silu(z_ref[...])
    parts = []
    for grp in range(N_GROUPS):
        yg = yz[:, grp * LANES:(grp + 1) * LANES]
        parts.append(yg * lax.rsqrt(jnp.mean(yg * yg, -1, keepdims=True) + EPS))
    ssm = jnp.concatenate(parts, axis=1) * snw_ref[...]
    mixed = (jnp.dot(att_ref[...], w_ref[0:D_ATT], preferred_element_type=F32)
             + _bdot(ml, w_ref[D_ATT:D_ATT + D_REC])
             + _bdot(ssm, w_ref[D_ATT + D_REC:D_MODEL]))
    y = ALPHA * x_ref[...] + gate_ref[0] * mixed
    o_ref[...] = _layernorm_rows(y, lg_ref[...], lb_ref[...])


def _out_proj(att, hf, hb, yf, yb, u, w, x, gate, mnw, snw, lg, lb, rows_per_mod):
    n = x.shape[0]
    tpb = rows_per_mod // ROW_TILE
    row = lambda width, col: pl.BlockSpec((ROW_TILE, width), lambda i: (i, col))
    vec = lambda width: pl.BlockSpec((1, width), lambda i: (0, 0))
    return pl.pallas_call(
        _outproj_kernel,
        grid=(n // ROW_TILE,),
        in_specs=[row(D_ATT, 0), row(D_REC, 0), row(D_REC, 0), row(D_REC, COL_MO // D_REC),
                  row(D_REC, 0), row(D_REC, 0), row(D_REC, COL_SZ // D_REC),
                  pl.BlockSpec((D_MODEL, D_MODEL), lambda i: (0, 0)),
                  row(D_MODEL, 0),
                  pl.BlockSpec((1, 1, D_MODEL), lambda i: (i // tpb, 0, 0)),
                  vec(D_REC), vec(D_REC), vec(D_MODEL), vec(D_MODEL)],
        out_specs=row(D_MODEL, 0),
        out_shape=jax.ShapeDtypeStruct((n, D_MODEL), F32),
        compiler_params=_params(("parallel",)),
        name="out_proj",
    )(att, hf, hb, u, yf, yb, u, w, x, gate, mnw.reshape(1, D_REC), snw.reshape(1, D_REC),
      lg.reshape(1, D_MODEL), lb.reshape(1, D_MODEL))


FF_TILE = D_FF // 2
FFN_ROWS = 1024
FFN_SUB = 512
FFN_VMEM_LIMIT = 56 * 1024 * 1024


def _swiglu_partial(h, w1, w3, w2):
    a = jnp.dot(h, w1, preferred_element_type=F32)
    b = jnp.dot(h, w3, preferred_element_type=F32)
    return jnp.dot((_silu(a) * b).astype(BF16), w2, preferred_element_type=F32)


def _ffn_kernel(x_ref, sc_ref, sh_ref, gate_ref, w1_ref, w3_ref, w2_ref, lg_ref, lb_ref, o_ref):
    for r0 in range(0, FFN_ROWS, FFN_SUB):
        rows = slice(r0, r0 + FFN_SUB)
        x = x_ref[rows, :]
        h = (x * (1.0 + sc_ref[0]) + sh_ref[0]).astype(BF16)
        y = ALPHA * x + gate_ref[0] * _swiglu_partial(h, w1_ref[...], w3_ref[...], w2_ref[...])
        o_ref[rows, :] = _layernorm_rows(y, lg_ref[...], lb_ref[...])


def _ffn(x, sc, sh, gate, w1, w3, w2, lg, lb, rows_per_mod):
    n = x.shape[0]
    tpb = rows_per_mod // FFN_ROWS
    modspec = pl.BlockSpec((1, 1, D_MODEL), lambda i: (i // tpb, 0, 0))
    vec = pl.BlockSpec((1, D_MODEL), lambda i: (0, 0))
    once = pl.Buffered(1)
    return pl.pallas_call(
        _ffn_kernel,
        grid=(n // FFN_ROWS,),
        in_specs=[pl.BlockSpec((FFN_ROWS, D_MODEL), lambda i: (i, 0)), modspec, modspec, modspec,
                  pl.BlockSpec((D_MODEL, D_FF), lambda i: (0, 0), pipeline_mode=once),
                  pl.BlockSpec((D_MODEL, D_FF), lambda i: (0, 0), pipeline_mode=once),
                  pl.BlockSpec((D_FF, D_MODEL), lambda i: (0, 0), pipeline_mode=once), vec, vec],
        out_specs=pl.BlockSpec((FFN_ROWS, D_MODEL), lambda i: (i, 0)),
        out_shape=jax.ShapeDtypeStruct((n, D_MODEL), F32),
        compiler_params=_params(("parallel",), FFN_VMEM_LIMIT),
        name="ffn_dense",
    )(x, sc, sh, gate, w1, w3, w2, lg.reshape(1, D_MODEL), lb.reshape(1, D_MODEL))


def _router_kernel(x_ref, sc_ref, sh_ref, rw_ref, gates_ref, h_ref):
    h = x_ref[...] * (1.0 + sc_ref[0]) + sh_ref[0]
    h_ref[...] = h.astype(BF16)
    logits = jnp.dot(h, rw_ref[...], precision=HIGHEST, preferred_element_type=F32)
    lane = lax.broadcasted_iota(jnp.int32, logits.shape, 1)
    valid = lane < N_EXPERTS
    p = jnp.where(valid, _softmax_rows(jnp.where(valid, logits, -jnp.inf)), -2.0)
    p1 = jnp.max(p, -1, keepdims=True)
    i1 = jnp.min(jnp.where(p == p1, lane, LANES), -1, keepdims=True)
    rest = jnp.where(lane == i1, -1.0, p)
    p2 = jnp.max(rest, -1, keepdims=True)
    i2 = jnp.min(jnp.where(rest == p2, lane, LANES), -1, keepdims=True)
    tot = p1 + p2
    gates_ref[...] = jnp.where(lane == i1, p1 / tot, jnp.where(lane == i2, p2 / tot, 0.0))


def _router(x, sc, sh, router_w, rows_per_mod):
    n = x.shape[0]
    tpb = rows_per_mod // ROW_TILE
    modspec = pl.BlockSpec((1, 1, D_MODEL), lambda i: (i // tpb, 0, 0))
    rw = jnp.pad(router_w, ((0, 0), (0, LANES - N_EXPERTS)))
    return pl.pallas_call(
        _router_kernel,
        grid=(n // ROW_TILE,),
        in_specs=[pl.BlockSpec((ROW_TILE, D_MODEL), lambda i: (i, 0)), modspec, modspec,
                  pl.BlockSpec((D_MODEL, LANES), lambda i: (0, 0))],
        out_specs=[pl.BlockSpec((ROW_TILE, LANES), lambda i: (i, 0)),
                   pl.BlockSpec((ROW_TILE, D_MODEL), lambda i: (i, 0))],
        out_shape=[jax.ShapeDtypeStruct((n, LANES), F32), jax.ShapeDtypeStruct((n, D_MODEL), BF16)],
        compiler_params=_params(("parallel",)),
        name="router",
    )(x, sc, sh, rw)


MOE_ROWS = 1024
MOE_TILE = 128
MOE_MAX_TILES = MOE_ROWS // MOE_TILE
MOE_VMEM_LIMIT = 56 * 1024 * 1024


def _moe_kernel(h_ref, gates_ref, w1_ref, w3_ref, w2_ref, x_ref, gate_ref, lg_ref, lb_ref, o_ref,
                slot_scr, slott_scr, hs_scr, ys_scr):
    e = pl.program_id(1)
    j = pl.program_id(2)
    last_j = pl.num_programs(2) - 1

    @pl.when(jnp.logical_and(e == 0, j == 0))
    def _():
        r = lax.broadcasted_iota(jnp.int32, (MOE_ROWS, MOE_ROWS), 0)
        c = lax.broadcasted_iota(jnp.int32, (MOE_ROWS, MOE_ROWS), 1)
        before = jnp.where(c < r, 1.0, 0.0).astype(BF16)
        mask = gates_ref[...] != 0.0
        rank = jnp.dot(before, jnp.where(mask, 1.0, 0.0).astype(BF16), preferred_element_type=F32)
        slot = jnp.where(mask, rank, -1.0).astype(jnp.int32)
        slot_scr[...] = slot
        slott_scr[...] = slot.T
        o_ref[...] = jnp.zeros_like(o_ref)

    lane = lax.broadcasted_iota(jnp.int32, (MOE_ROWS, LANES), 1)
    slot_col = jnp.max(jnp.where(lane == e, slot_scr[...], -1), -1, keepdims=True)
    n_tiles = (jnp.max(slot_col) + MOE_TILE) // MOE_TILE

    def tile(k, carry):
        @pl.when(j == 0)
        def _():
            slot_row = slott_scr[pl.ds(e, 1), :]
            rr = lax.broadcasted_iota(jnp.int32, (MOE_TILE, MOE_ROWS), 0) + k * MOE_TILE
            pick = jnp.where(rr == slot_row, 1.0, 0.0).astype(BF16)
            hs_scr[k] = jnp.dot(pick, h_ref[...], preferred_element_type=F32).astype(BF16)
            ys_scr[k] = jnp.zeros((MOE_TILE, D_MODEL), F32)

        ys_scr[k] += _swiglu_partial(hs_scr[k], w1_ref[0], w3_ref[0], w2_ref[0])

        @pl.when(j == last_j)
        def _():
            g_col = jnp.sum(jnp.where(lane == e, gates_ref[...], 0.0), -1, keepdims=True)
            cc = lax.broadcasted_iota(jnp.int32, (MOE_ROWS, 2 * MOE_TILE), 1)
            cc = jnp.where(cc >= MOE_TILE, cc - MOE_TILE, cc) + k * MOE_TILE
            put = jnp.where(slot_col == cc, 1.0, 0.0).astype(BF16)
            y = ys_scr[k]
            y_hi = y.astype(BF16)
            y_lo = (y - y_hi.astype(F32)).astype(BF16)
            back = jnp.dot(put, jnp.concatenate([y_hi, y_lo], axis=0), preferred_element_type=F32)
            o_ref[...] += g_col * back

        return carry

    lax.fori_loop(0, n_tiles, tile, 0)

    @pl.when(jnp.logical_and(e == pl.num_programs(1) - 1, j == last_j))
    def _():
        y = ALPHA * x_ref[...] + gate_ref[0] * o_ref[...]
        o_ref[...] = _layernorm_rows(y, lg_ref[...], lb_ref[...])


def _moe(h, gates, w1, w3, w2, x, gate, lg, lb, rows_per_mod):
    n = h.shape[0]
    tpb = rows_per_mod // MOE_ROWS
    vec = pl.BlockSpec((1, D_MODEL), lambda i, e, j: (0, 0))
    return pl.pallas_call(
        _moe_kernel,
        grid=(n // MOE_ROWS, N_EXPERTS, D_FF // FF_TILE),
        in_specs=[pl.BlockSpec((MOE_ROWS, D_MODEL), lambda i, e, j: (i, 0)),
                  pl.BlockSpec((MOE_ROWS, LANES), lambda i, e, j: (i, 0)),
                  pl.BlockSpec((1, D_MODEL, FF_TILE), lambda i, e, j: (e, 0, j)),
                  pl.BlockSpec((1, D_MODEL, FF_TILE), lambda i, e, j: (e, 0, j)),
                  pl.BlockSpec((1, FF_TILE, D_MODEL), lambda i, e, j: (e, j, 0)),
                  pl.BlockSpec((MOE_ROWS, D_MODEL), lambda i, e, j: (i, 0)),
                  pl.BlockSpec((1, 1, D_MODEL), lambda i, e, j: (i // tpb, 0, 0)), vec, vec],
        out_specs=pl.BlockSpec((MOE_ROWS, D_MODEL), lambda i, e, j: (i, 0)),
        out_shape=jax.ShapeDtypeStruct((n, D_MODEL), F32),
        scratch_shapes=[pltpu.VMEM((MOE_ROWS, LANES), jnp.int32), pltpu.VMEM((LANES, MOE_ROWS), jnp.int32),
                        pltpu.VMEM((MOE_MAX_TILES, MOE_TILE, D_MODEL), BF16),
                        pltpu.VMEM((MOE_MAX_TILES, MOE_TILE, D_MODEL), F32)],
        compiler_params=_params(("parallel", "arbitrary", "arbitrary"), MOE_VMEM_LIMIT),
        name="moe",
    )(h, gates, w1, w3, w2, x, gate, lg.reshape(1, D_MODEL), lb.reshape(1, D_MODEL))


def _permute_w_in(w):
    pad = jnp.zeros((D_MODEL, U_COLS - ORIG_END), w.dtype)
    return jnp.concatenate([w[:, :ORIG_GATES], w[:, ORIG_SX:ORIG_DT], w[:, ORIG_SZ:ORIG_SX],
                            w[:, ORIG_GATES:ORIG_SZ], w[:, ORIG_DT:ORIG_END], pad], axis=1).astype(BF16)


def _small_row(vals, offset):
    v = vals.reshape(-1).astype(F32)
    return jnp.zeros((1, LANES), F32).at[0, offset:offset + v.shape[0]].set(v)


def _pack_mlstm_state(c, n, m):
    shape = c.shape[:2] + (D_STATE, HEAD_LANES)
    c_rows = jnp.swapaxes(c, 2, 3).reshape(shape)
    n_rows = jnp.broadcast_to(jnp.swapaxes(n, 2, 3)[..., None], c.shape[:2] + (D_STATE, N_HEADS, D_STATE))
    return (jnp.concatenate([c_rows, n_rows.reshape(shape)], axis=-1),
            jnp.repeat(m, D_STATE, axis=-1)[:, :, None, :])


def _pack_ssd_state(s):
    sel = (jnp.arange(N_GROUPS)[:, None] == jnp.arange(N_HEADS)[None, :] // 2).astype(F32)
    return jnp.einsum('bdhpn,gh->bdgnhp', s, sel).reshape(s.shape[:2] + (LANES, HEAD_LANES))


def _layer(x, mods, P, l, bsz, t, ctx, caches=None):
    sh1, sc1, g1, sh2, sc2, g2 = mods
    rows_per_mod = x.shape[0] // sh1.shape[0]
    lam_init = 0.8 - 0.6 * math.exp(-0.3 * l)
    u = _in_proj(x, sc1, sh1, P['w_in'][l], rows_per_mod)

    gate_bias = (_small_row(P['mlstm_gate_b'][l, 0], GATE_I) + _small_row(P['mlstm_gate_b'][l, 1], GATE_F))
    dt_bias = _small_row(P['ssm_dt_bias'][l], GATE_DT)
    alog = _small_row(P['ssm_A_log'][l], GATE_DT)
    dskip = jnp.repeat(P['ssm_D'][l].astype(F32), D_STATE).reshape(1, D_REC)

    if ctx is None:
        att, k_new, v_new = _attention_ctx(u, P['attn_lambda'][l], P['attn_norm_w'][l], lam_init, bsz, t, l,
                                           None if caches is None else caches[0:2])
        m_init = s_init = None
    else:
        ck, cv, c_c, c_n, c_m, c_s = ctx
        q, k, v = _rope_prep(u, bsz, t)
        k_all = jnp.concatenate([k, ck.astype(BF16)], axis=3)
        v_all = jnp.concatenate([v, cv.astype(BF16)], axis=2)
        att = _attention_lat(q, k_all, v_all, P['attn_lambda'][l], P['attn_norm_w'][l], lam_init, bsz, t)
        m_init = _pack_mlstm_state(c_c, c_n, c_m)
        s_init = _pack_ssd_state(c_s)
    hf, hb, *mlstm_caches = _mlstm(u, gate_bias, bsz, t, m_init, l, None if caches is None else caches[2:5])
    xbc = _ssd_conv(u, P['conv_w'][l], P['conv_b'][l], bsz, t)
    yf, yb, *ssd_caches = _ssd(xbc, u, dt_bias, alog, dskip, bsz, t, s_init, l,
                               None if caches is None else caches[5])

    x = _out_proj(att, hf, hb, yf, yb, u, P['w_out'][l], x, g1, P['mlstm_norm_w'][l], P['ssm_norm_w'][l],
                  P['ln_g'][l, 0], P['ln_b'][l, 0], rows_per_mod)
    if l % 2 == 0:
        x = _ffn(x, sc2, sh2, g2, P['ffn_w1'][l // 2], P['ffn_w3'][l // 2], P['ffn_w2'][l // 2],
                 P['ln_g'][l, 1], P['ln_b'][l, 1], rows_per_mod)
    else:
        gates, h2 = _router(x, sc2, sh2, P['router_w'][l // 2], rows_per_mod)
        x = _moe(h2, gates, P['moe_w1'][l // 2], P['moe_w3'][l // 2], P['moe_w2'][l // 2],
                 x, g2, P['ln_g'][l, 1], P['ln_b'][l, 1], rows_per_mod)
    if ctx is None:
        return x, (k_new, v_new, *mlstm_caches, *ssd_caches)
    return x, None


def kernel(x_prompt, x_sample, c, cache_attn_k, cache_attn_v, state_mlstm_C, state_mlstm_n, state_mlstm_m, state_ssm, c_ctx, w_ada, b_ada, w_in, w_out, attn_lambda, attn_norm_w, mlstm_gate_b, mlstm_norm_w, conv_w, conv_b, ssm_A_log, ssm_dt_bias, ssm_D, ssm_norm_w, ln_g, ln_b, ffn_w1, ffn_w3, ffn_w2, router_w, moe_w1, moe_w3, moe_w2):
    bsz, seq, _ = x_prompt.shape
    dbsz, dseq, _ = x_sample.shape
    P = dict(w_in=[_permute_w_in(w_in[l]) for l in range(DEPTH)], w_out=w_out.astype(BF16),
             attn_lambda=attn_lambda, attn_norm_w=attn_norm_w, mlstm_gate_b=mlstm_gate_b,
             mlstm_norm_w=mlstm_norm_w, conv_w=conv_w, conv_b=conv_b, ssm_A_log=ssm_A_log,
             ssm_dt_bias=ssm_dt_bias, ssm_D=ssm_D, ssm_norm_w=ssm_norm_w, ln_g=ln_g, ln_b=ln_b,
             ffn_w1=ffn_w1.astype(BF16), ffn_w3=ffn_w3.astype(BF16), ffn_w2=ffn_w2.astype(BF16),
             router_w=router_w, moe_w1=moe_w1.astype(BF16), moe_w3=moe_w3.astype(BF16),
             moe_w2=moe_w2.astype(BF16))

    cvec = jnp.zeros((8, D_MODEL), F32).at[0].set(c_ctx).at[1:1 + dbsz].set(c)
    mod = _modulation(cvec, w_ada, b_ada)

    def mods_for(l, lo, hi):
        return [mod[l, lo:hi, i * D_MODEL:(i + 1) * D_MODEL][:, None, :] for i in range(6)]

    y_prompt = x_prompt.reshape(bsz * seq, D_MODEL)
    caches = None
    for l in range(DEPTH):
        y_prompt, caches = _layer(y_prompt, mods_for(l, 0, 1), P, l, bsz, seq, None, caches)
    new_k, new_v, new_c, new_n, m_spread, new_s = caches

    y_sample = x_sample.reshape(dbsz * dseq, D_MODEL)
    for l in range(DEPTH):
        ctx = (cache_attn_k[:, l], cache_attn_v[:, l], state_mlstm_C[:, l], state_mlstm_n[:, l],
               state_mlstm_m[:, l], state_ssm[:, l])
        y_sample, _ = _layer(y_sample, mods_for(l, 1, 1 + dbsz), P, l, dbsz, dseq, ctx)

    return (y_prompt.reshape(bsz, seq, D_MODEL), y_sample.reshape(dbsz, dseq, D_MODEL),
            new_k, new_v, new_c, new_n, m_spread[:, :, :, 0, ::D_STATE], new_s)
```

```python
import functools
import math

import jax
import jax.numpy as jnp
from jax import lax
from jax.experimental import pallas as pl
from jax.experimental.pallas import tpu as pltpu

F32 = jnp.float32
BF16 = jnp.bfloat16
HIGHEST = lax.Precision.HIGHEST

D_MODEL = 1024
DEPTH = 2
GRID_W = 64
N_HEADS = 4
D_ATT = 512
D_HEAD_V = 128
D_QK = 64
D_REC = 256
D_STATE = 64
N_GROUPS = 2
D_CONV = 3
D_FF = 2816
N_EXPERTS = 8
ALPHA = (2.0 * DEPTH) ** 0.25
CHUNK = 64
ROPE_BASE = 10000.0
LOG2E = 1.4426950408889634
EPS = 1e-5

COL_AQ, COL_AK, COL_AV = 0, 512, 1024
COL_MQ, COL_MK, COL_MV, COL_MO = 1536, 1792, 2048, 2304
COL_SX, COL_SBC, COL_SZ = 2560, 2816, 3072
COL_SMALL = 3328
U_COLS = 3584
ORIG_GATES, ORIG_SZ, ORIG_SX, ORIG_DT, ORIG_END = 2560, 2576, 2832, 3344, 3352
GATE_I, GATE_F, GATE_DT = 0, 8, 16

LANES = 128
ROW_TILE = 512
SCAN_ROWS = 256
SCAN_BATCH = 2
VMEM_LIMIT = 48 * 1024 * 1024

NT_DIMS = (((1,), (1,)), ((), ()))
TN_DIMS = (((0,), (0,)), ((), ()))


def _params(sem, vmem=VMEM_LIMIT):
    return pltpu.CompilerParams(dimension_semantics=sem, vmem_limit_bytes=vmem)


def _silu(x):
    return x * jax.nn.sigmoid(x)


def _bdot(a, b):
    return jnp.dot(a.astype(BF16), b.astype(BF16), preferred_element_type=F32)


def _bdot_nt(a, b):
    return lax.dot_general(a.astype(BF16), b.astype(BF16), NT_DIMS, preferred_element_type=F32)


def _layernorm_rows(y, g, b):
    mu = jnp.mean(y, -1, keepdims=True)
    d = y - mu
    var = jnp.mean(d * d, -1, keepdims=True)
    return d * lax.rsqrt(var + EPS) * g + b


def _mod_kernel(c_ref, w_ref, b_ref, o_ref):
    o_ref[0] = jnp.dot(_silu(c_ref[...]), w_ref[0], precision=HIGHEST,
                       preferred_element_type=F32) + b_ref[0]


def _modulation(cvec, w_ada, b_ada):
    tn = 1536
    return pl.pallas_call(
        _mod_kernel,
        grid=(DEPTH, 6 * D_MODEL // tn),
        in_specs=[pl.BlockSpec((8, D_MODEL), lambda l, j: (0, 0)),
                  pl.BlockSpec((1, D_MODEL, tn), lambda l, j: (l, 0, j)),
                  pl.BlockSpec((1, 1, tn), lambda l, j: (l, 0, j))],
        out_specs=pl.BlockSpec((1, 8, tn), lambda l, j: (l, 0, j)),
        out_shape=jax.ShapeDtypeStruct((DEPTH, 8, 6 * D_MODEL), F32),
        compiler_params=_params(("parallel", "parallel")),
        name="modulation",
    )(cvec, w_ada, b_ada.reshape(DEPTH, 1, 6 * D_MODEL))


CONV_COLS = 2 * D_REC
HALO = 16


def _inproj_kernel(x_ref, prev_ref, next_ref, sc_ref, sh_ref, w_ref, cw_ref, cb_ref, o_ref, *, seq):
    def modulate(x):
        return (x * (1.0 + sc_ref[0]) + sh_ref[0]).astype(BF16)

    h = modulate(x_ref[...])
    for n0 in range(0, U_COLS, CONV_COLS):
        if n0 != COL_SX:
            o_ref[:, n0:n0 + CONV_COLS] = jnp.dot(h, w_ref[:, n0:n0 + CONV_COLS], preferred_element_type=F32)
            continue
        h_ext = jnp.concatenate([h, modulate(prev_ref[...]), modulate(next_ref[...])], axis=0)
        pre_ext = jnp.dot(h_ext, w_ref[:, n0:n0 + CONV_COLS], preferred_element_type=F32)
        pre = pre_ext[:ROW_TILE]
        before = pre_ext[ROW_TILE + HALO - 1:ROW_TILE + HALO]
        after = pre_ext[ROW_TILE + HALO:ROW_TILE + HALO + 1]
        r = lax.broadcasted_iota(jnp.int32, (ROW_TILE, 1), 0)
        pos = (r + pl.program_id(0) * ROW_TILE) % seq
        up = jnp.where(r == 0, before, pltpu.roll(pre, 1, 0))
        down = jnp.where(r == ROW_TILE - 1, after, pltpu.roll(pre, ROW_TILE - 1, 0))
        up = jnp.where(pos == 0, 0.0, up)
        down = jnp.where(pos == seq - 1, 0.0, down)
        cw = cw_ref[...]
        o_ref[:, n0:n0 + CONV_COLS] = _silu(up * cw[0:1] + pre * cw[1:2] + down * cw[2:3] + cb_ref[...])


def _in_proj(x, sc, sh, w, conv_w, conv_b, rows_per_mod, seq):
    n = x.shape[0]
    tpb = rows_per_mod // ROW_TILE
    r8 = ROW_TILE // HALO
    last8 = n // HALO - 1
    assert COL_SX % CONV_COLS == 0 and U_COLS % CONV_COLS == 0
    return pl.pallas_call(
        functools.partial(_inproj_kernel, seq=seq),
        grid=(n // ROW_TILE,),
        in_specs=[pl.BlockSpec((ROW_TILE, D_MODEL), lambda i: (i, 0)),
                  pl.BlockSpec((HALO, D_MODEL), lambda i: (jnp.maximum(i * r8 - 1, 0), 0)),
                  pl.BlockSpec((HALO, D_MODEL), lambda i: (jnp.minimum((i + 1) * r8, last8), 0)),
                  pl.BlockSpec((1, 1, D_MODEL), lambda i: (i // tpb, 0, 0)),
                  pl.BlockSpec((1, 1, D_MODEL), lambda i: (i // tpb, 0, 0)),
                  pl.BlockSpec((D_MODEL, U_COLS), lambda i: (0, 0)),
                  pl.BlockSpec((D_CONV, CONV_COLS), lambda i: (0, 0)),
                  pl.BlockSpec((1, CONV_COLS), lambda i: (0, 0))],
        out_specs=pl.BlockSpec((ROW_TILE, U_COLS), lambda i: (i, 0)),
        out_shape=jax.ShapeDtypeStruct((n, U_COLS), F32),
        compiler_params=_params(("parallel",)),
        name="in_proj",
    )(x, x, x, sc, sh, w, conv_w, conv_b.reshape(1, CONV_COLS))


def _lambda_scalar(lam_ref, lam_init):
    lp = lam_ref[...]
    s01 = jnp.sum(lp[0:1] * lp[1:2], axis=-1, keepdims=True)
    s23 = jnp.sum(lp[2:3] * lp[3:4], axis=-1, keepdims=True)
    return jnp.exp(s01) - jnp.exp(s23) + lam_init


def _softmax_rows(s):
    e = jnp.exp(s - jnp.max(s, -1, keepdims=True))
    return e / jnp.sum(e, -1, keepdims=True)


def _head_norm(o, nw, lam_init):
    return o * lax.rsqrt(jnp.mean(o * o, -1, keepdims=True) + EPS) * nw * (1.0 - lam_init)


def _layer_cache(tail, l, bsz, nb=1):
    zeros = (0,) * len(tail)
    owned = DEPTH if l == 0 else 1
    spec = pl.BlockSpec((nb, owned) + tuple(tail), lambda b, *_: (b, l) + zeros)
    return jax.ShapeDtypeStruct((bsz, DEPTH) + tuple(tail), F32), spec, owned


def _zero_later_layers(ref):
    for b in range(ref.shape[0]):
        for later in range(1, ref.shape[1]):
            ref[b, later] = jnp.zeros(ref.shape[2:], ref.dtype)


def _attn_ctx_kernel(u_ref, lam_ref, nw_ref, *rest, lam_init):
    att_ref, k_ref, v_ref = rest[-3:]
    lam = _lambda_scalar(lam_ref, lam_init)
    for h in range(N_HEADS):
        v = u_ref[:, COL_AV + h * D_HEAD_V:COL_AV + (h + 1) * D_HEAD_V]
        v_ref[0, 0, h] = v
        ps = []
        for m in range(2):
            c0 = h * D_HEAD_V + m * D_QK
            q = u_ref[:, COL_AQ + c0:COL_AQ + c0 + D_QK] * (D_QK ** -0.5)
            k = u_ref[:, COL_AK + c0:COL_AK + c0 + D_QK]
            k_ref[0, 0, h, m] = k
            ps.append(_softmax_rows(_bdot_nt(q, k)))
        o = _bdot(ps[0] - lam * ps[1], v)
        att_ref[:, h * D_HEAD_V:(h + 1) * D_HEAD_V] = _head_norm(o, nw_ref[...], lam_init).astype(BF16)
    _zero_later_layers(k_ref)
    _zero_later_layers(v_ref)


def _attention_ctx(u, lam_p, norm_w, lam_init, bsz, t, l, caches):
    n = bsz * t
    k_shape, k_spec, _ = _layer_cache((N_HEADS, 2, t, D_QK), l, bsz)
    v_shape, v_spec, _ = _layer_cache((N_HEADS, t, D_HEAD_V), l, bsz)
    in_specs = [pl.BlockSpec((t, 3 * D_ATT), lambda b: (b, 0)),
                pl.BlockSpec((4, D_QK), lambda b: (0, 0)),
                pl.BlockSpec((1, D_HEAD_V), lambda b: (0, 0))]
    args = [u, lam_p, norm_w.reshape(1, D_HEAD_V)]
    aliases = {}
    if caches is not None:
        in_specs += [pl.BlockSpec(memory_space=pl.ANY)] * 2
        aliases = {len(args): 1, len(args) + 1: 2}
        args += list(caches)
    return pl.pallas_call(
        functools.partial(_attn_ctx_kernel, lam_init=lam_init),
        grid=(bsz,),
        in_specs=in_specs,
        out_specs=[pl.BlockSpec((t, D_ATT), lambda b: (b, 0)), k_spec, v_spec],
        out_shape=[jax.ShapeDtypeStruct((n, D_ATT), BF16), k_shape, v_shape],
        input_output_aliases=aliases,
        compiler_params=_params(("parallel",)),
        name="attn_ctx",
    )(*args)


def _rope_kernel(u_ref, cos_ref, sa_ref, sb_ref, q_ref, k_ref, v_ref):
    cos, sa, sb = cos_ref[...], sa_ref[...], sb_ref[...]

    def rope(x):
        return x * cos + pltpu.roll(x, LANES - 16, 1) * sa + pltpu.roll(x, 16, 1) * sb

    for h in range(N_HEADS):
        q = rope(u_ref[:, COL_AQ + h * D_HEAD_V:COL_AQ + (h + 1) * D_HEAD_V]) * (LOG2E * D_QK ** -0.5)
        k = rope(u_ref[:, COL_AK + h * D_HEAD_V:COL_AK + (h + 1) * D_HEAD_V])
        for m in range(2):
            q_ref[0, h, m] = q[:, m * D_QK:(m + 1) * D_QK].astype(BF16)
            k_ref[0, h, m] = k[:, m * D_QK:(m + 1) * D_QK].astype(BF16)
        v_ref[0, h] = u_ref[:, COL_AV + h * D_HEAD_V:COL_AV + (h + 1) * D_HEAD_V].astype(BF16)


def _rope_tables(t):
    rows = jnp.repeat(jnp.arange(t // GRID_W, dtype=F32), GRID_W)
    cols = jnp.tile(jnp.arange(GRID_W, dtype=F32), t // GRID_W)
    half = D_QK // 2
    inv = ROPE_BASE ** (-jnp.arange(0, half, 2, dtype=F32) / half)
    ang_r = rows[:, None] * inv
    ang_c = cols[:, None] * inv
    ang = jnp.concatenate([ang_r, ang_r, ang_c, ang_c], -1)
    cos, sin = jnp.cos(ang), jnp.sin(ang)
    quarter = (jnp.arange(D_QK) // (D_QK // 4)) % 2
    sa = jnp.where(quarter == 0, -sin, 0.0)
    sb = jnp.where(quarter == 1, sin, 0.0)
    tile2 = lambda a: jnp.concatenate([a, a], -1)
    return tile2(cos), tile2(sa), tile2(sb)


def _rope_prep(u, bsz, t):
    tr = 512
    nb = t // tr
    cos, sa, sb = _rope_tables(t)
    tab = pl.BlockSpec((tr, LANES), lambda b, i: (i, 0))
    return pl.pallas_call(
        _rope_kernel,
        grid=(bsz, nb),
        in_specs=[pl.BlockSpec((tr, 3 * D_ATT), lambda b, i: (b * nb + i, 0)), tab, tab, tab],
        out_specs=[pl.BlockSpec((1, N_HEADS, 2, tr, D_QK), lambda b, i: (b, 0, 0, i, 0)),
                   pl.BlockSpec((1, N_HEADS, 2, tr, D_QK), lambda b, i: (b, 0, 0, i, 0)),
                   pl.BlockSpec((1, N_HEADS, tr, D_HEAD_V), lambda b, i: (b, 0, i, 0))],
        out_shape=[jax.ShapeDtypeStruct((bsz, N_HEADS, 2, t, D_QK), BF16),
                   jax.ShapeDtypeStruct((bsz, N_HEADS, 2, t, D_QK), BF16),
                   jax.ShapeDtypeStruct((bsz, N_HEADS, t, D_HEAD_V), BF16)],
        compiler_params=_params(("parallel", "parallel")),
        name="rope_prep",
    )(u, cos, sa, sb)


def _attn_lat_kernel(q_ref, k_ref, v_ref, lam_ref, nw_ref, o_ref, *, lam_init):
    lam = _lambda_scalar(lam_ref, lam_init)
    es, sums = [], []
    for m in range(2):
        s = lax.dot_general(q_ref[0, 0, m], k_ref[0, 0, m], NT_DIMS, preferred_element_type=F32)
        e = jnp.exp2(s - jnp.max(s, -1, keepdims=True))
        es.append(e)
        sums.append(jnp.sum(e, -1, keepdims=True))
    a = es[0] - (lam * sums[0] / sums[1]) * es[1]
    o = _bdot(a, v_ref[0, 0]) / sums[0]
    o_ref[...] = _head_norm(o, nw_ref[...], lam_init).astype(BF16)


def _attention_lat(q, k_all, v_all, lam_p, norm_w, lam_init, bsz, t):
    tq = 256
    nq = t // tq
    s = k_all.shape[3]
    return pl.pallas_call(
        functools.partial(_attn_lat_kernel, lam_init=lam_init),
        grid=(bsz, N_HEADS, nq),
        in_specs=[pl.BlockSpec((1, 1, 2, tq, D_QK), lambda b, h, i: (b, h, 0, i, 0)),
                  pl.BlockSpec((1, 1, 2, s, D_QK), lambda b, h, i: (b, h, 0, 0, 0)),
                  pl.BlockSpec((1, 1, s, D_HEAD_V), lambda b, h, i: (b, h, 0, 0)),
                  pl.BlockSpec((4, D_QK), lambda b, h, i: (0, 0)),
                  pl.BlockSpec((1, D_HEAD_V), lambda b, h, i: (0, 0))],
        out_specs=pl.BlockSpec((tq, D_HEAD_V), lambda b, h, i: (b * nq + i, h)),
        out_shape=jax.ShapeDtypeStruct((bsz * t, D_ATT), BF16),
        compiler_params=_params(("parallel", "parallel", "parallel")),
        name="attn_lat",
    )(q, k_all, v_all, lam_p, norm_w.reshape(1, D_HEAD_V))


HEAD_LANES = N_HEADS * D_STATE


def _scan_consts():
    t = lax.broadcasted_iota(jnp.int32, (CHUNK, HEAD_LANES), 0)
    s = lax.broadcasted_iota(jnp.int32, (CHUNK, HEAD_LANES), 1) & (CHUNK - 1)
    r = lax.broadcasted_iota(jnp.int32, (CHUNK, CHUNK), 0)
    c = lax.broadcasted_iota(jnp.int32, (CHUNK, CHUNK), 1)
    reads = (jnp.where(s <= t, 1.0, 0.0), jnp.where(s >= t, 1.0, 0.0))
    block = (jnp.where(s <= t, 0.0, -jnp.inf), jnp.where(s >= t, 0.0, -jnp.inf))
    tri = (jnp.where(c <= r, 1.0, 0.0).astype(BF16), jnp.where(c >= r, 1.0, 0.0).astype(BF16))
    return reads, block, tri, jnp.where(s == t, 1.0, 0.0)


def _cumsum_rows(tri, x):
    hi = x.astype(BF16)
    rest = x - hi.astype(F32)
    mid = rest.astype(BF16)
    lo = (rest - mid.astype(F32)).astype(BF16)
    parts = jnp.dot(tri, jnp.concatenate([hi, mid, lo], axis=1), preferred_element_type=F32)
    return parts[:, :HEAD_LANES] + parts[:, HEAD_LANES:2 * HEAD_LANES] + parts[:, 2 * HEAD_LANES:]


def _group_mask(rows, cols, row_shift, col_shift, dtype):
    r = lax.broadcasted_iota(jnp.int32, (rows, cols), 0) >> 6
    c = (lax.broadcasted_iota(jnp.int32, (rows, cols), 1) >> 6) & (N_HEADS - 1)
    return jnp.where((r >> row_shift) == (c >> col_shift), 1.0, 0.0).astype(dtype)


def _spread(x, chans):
    return jnp.concatenate([jnp.broadcast_to(x[:, c:c + 1], (CHUNK, D_STATE)) for c in chans], axis=1)


def _stack_heads(x):
    return jnp.concatenate([x] * N_HEADS, axis=0)


def _mlstm_chunk(q4, k4, v4, g, d, cn_prev, m_prev, reads, block, tri, eye, bd, bd_f32):
    li = _spread(g, [GATE_I + d * N_HEADS + h for h in range(N_HEADS)])
    lf = _spread(jax.nn.log_sigmoid(g), [GATE_F + d * N_HEADS + h for h in range(N_HEADS)])
    bc = _cumsum_rows(tri[d], lf)
    btot = jnp.sum(lf, 0, keepdims=True)
    b_row = jnp.sum(reads[1 - d] * lf, 0, keepdims=True)
    li_row = jnp.sum(eye * li, 0, keepdims=True)
    dm = bc - b_row + li_row + block[d]
    rmax = jnp.concatenate(
        [jnp.broadcast_to(jnp.max(dm[:, h * D_STATE:(h + 1) * D_STATE], -1, keepdims=True), (CHUNK, D_STATE))
         for h in range(N_HEADS)], axis=1)
    inter = bc + m_prev
    m_t = jnp.maximum(inter, rmax)
    w_inter = jnp.exp(inter - m_t)
    qs = (q4 * (D_STATE ** -0.5)).astype(BF16)
    kbd = bd[:, :HEAD_LANES] * _stack_heads(k4.astype(BF16))
    s4 = lax.dot_general(qs, kbd, NT_DIMS, preferred_element_type=F32) * jnp.exp(dm - m_t)
    vo = jnp.concatenate([v4.astype(BF16), jnp.ones((CHUNK, HEAD_LANES), BF16)], axis=1)
    vbd = bd * _stack_heads(vo)
    nd = (jnp.concatenate([w_inter, w_inter], axis=1)
          * jnp.dot(qs, bd * _stack_heads(cn_prev.astype(BF16)), preferred_element_type=F32)
          + jnp.dot(s4.astype(BF16), vbd, preferred_element_type=F32))
    hc = nd[:, :HEAD_LANES] / jnp.maximum(jnp.abs(nd[:, HEAD_LANES:]), jnp.exp(-m_t))
    gcol = btot - bc + li
    m_new = jnp.maximum(btot + m_prev, jnp.max(gcol, 0, keepdims=True))
    w_c = jnp.exp(btot + m_prev - m_new)
    kw = (k4 * jnp.exp(gcol - m_new)).astype(BF16)
    dcn = lax.dot_general(kw, vo, TN_DIMS, preferred_element_type=F32)
    own = sum(bd_f32[h * D_STATE:(h + 1) * D_STATE] * dcn[h * D_STATE:(h + 1) * D_STATE]
              for h in range(N_HEADS))
    cn_new = jnp.concatenate([w_c, w_c], axis=1) * cn_prev + own
    return hc, cn_new, m_new


def _mlstm_kernel(*refs, nblk, rows, zero_init, n_alias):
    qf_ref, kf_ref, vf_ref, gf_ref, qb_ref, kb_ref, vb_ref, gb_ref, bias_ref = refs[:9]
    if not zero_init:
        c0_ref, m0_ref = refs[9:11]
    outs = refs[9 + (0 if zero_init else 2) + n_alias:-2]
    hf_ref, hb_ref = outs[:2]
    cn_scr, m_scr = refs[-2:]
    j = pl.program_id(1)
    nchunk = rows // CHUNK

    @pl.when(j == 0)
    def _():
        if zero_init:
            cn_scr[...] = jnp.zeros_like(cn_scr)
            m_scr[...] = jnp.zeros_like(m_scr)
        else:
            cn_scr[...] = c0_ref[...]
            m_scr[...] = m0_ref[...]

    reads, block, tri, eye = _scan_consts()
    bd = _group_mask(HEAD_LANES, 2 * HEAD_LANES, 0, 0, BF16)
    bd_f32 = _group_mask(HEAD_LANES, 2 * HEAD_LANES, 0, 0, F32)

    chains = [(b, d) for b in range(SCAN_BATCH) for d in range(2)]
    state = {bd_: (cn_scr[bd_], m_scr[bd_]) for bd_ in chains}
    for ci in range(nchunk):
        for b, d in chains:
            q_ref, k_ref, v_ref, g_ref, h_ref = ((qf_ref, kf_ref, vf_ref, gf_ref, hf_ref) if d == 0
                                                 else (qb_ref, kb_ref, vb_ref, gb_ref, hb_ref))
            cj = ci if d == 0 else nchunk - 1 - ci
            rs = slice(cj * CHUNK, (cj + 1) * CHUNK)
            g = g_ref[b, rs, :] + bias_ref[...]
            hc, cn_new, m_new = _mlstm_chunk(q_ref[b, rs, :], k_ref[b, rs, :], v_ref[b, rs, :], g, d,
                                             *state[b, d], reads, block, tri, eye, bd, bd_f32)
            h_ref[b, rs, :] = hc
            state[b, d] = (cn_new, m_new)
    for bd_ in chains:
        cn_scr[bd_], m_scr[bd_] = state[bd_]

    if zero_init:
        cout_ref, nout_ref, mout_ref = outs[2:]

        @pl.when(j == nblk - 1)
        def _():
            r = lax.broadcasted_iota(jnp.int32, (D_STATE, D_STATE), 0)
            c = lax.broadcasted_iota(jnp.int32, (D_STATE, D_STATE), 1)
            for b, d in chains:
                for h in range(N_HEADS):
                    r0, r1 = h * D_STATE, (h + 1) * D_STATE
                    cout_ref[b, 0, d, h] = cn_scr[b, d, :, r0:r1]
                    n_spread = cn_scr[b, d, :, HEAD_LANES + r0:HEAD_LANES + r1]
                    nout_ref[b, 0, d, h:h + 1, :] = jnp.sum(jnp.where(r == c, n_spread, 0.0), 0, keepdims=True)
                mout_ref[b, 0, d] = m_scr[b, d]
            _zero_later_layers(cout_ref)
            _zero_later_layers(nout_ref)
            _zero_later_layers(mout_ref)


def _mlstm(u, gate_bias, bsz, t, init, l=0, caches=None):
    rows = min(t, SCAN_ROWS)
    nblk = t // rows
    zero_init = init is None

    nb = SCAN_BATCH
    fwd = lambda col: (lambda p, j: (p, j, col))
    bwd = lambda col: (lambda p, j: (p, nblk - 1 - j, col))
    cq, ck, cv, cg = COL_MQ // D_REC, COL_MK // D_REC, COL_MV // D_REC, COL_SMALL // LANES
    in_specs = []
    for mk in (fwd, bwd):
        in_specs += [pl.BlockSpec((nb, rows, D_REC), mk(cq)), pl.BlockSpec((nb, rows, D_REC), mk(ck)),
                     pl.BlockSpec((nb, rows, D_REC), mk(cv)), pl.BlockSpec((nb, rows, LANES), mk(cg))]
    in_specs.append(pl.BlockSpec((1, LANES), lambda p, j: (0, 0)))
    args = [u.reshape(bsz, t, U_COLS)] * 8 + [gate_bias]
    state_c = pl.BlockSpec((nb, 2, D_STATE, 2 * HEAD_LANES), lambda p, j: (p, 0, 0, 0))
    state_m = pl.BlockSpec((nb, 2, 1, HEAD_LANES), lambda p, j: (p, 0, 0, 0))
    out_specs = [pl.BlockSpec((nb, rows, D_REC), fwd(0)), pl.BlockSpec((nb, rows, D_REC), bwd(0))]
    out_shape = [jax.ShapeDtypeStruct((bsz, t, D_REC), F32), jax.ShapeDtypeStruct((bsz, t, D_REC), F32)]
    aliases = {}
    if zero_init:
        for tail in ((2, N_HEADS, D_STATE, D_STATE), (2, N_HEADS, D_STATE), (2, 1, HEAD_LANES)):
            shape, spec, _ = _layer_cache(tail, l, bsz, nb)
            out_shape.append(shape)
            out_specs.append(spec)
        if caches is not None:
            in_specs += [pl.BlockSpec(memory_space=pl.ANY)] * len(caches)
            aliases = {len(args) + i: 2 + i for i in range(len(caches))}
            args += list(caches)
    else:
        in_specs += [state_c, state_m]
        args += list(init)
    hf, hb, *state_out = pl.pallas_call(
        functools.partial(_mlstm_kernel, nblk=nblk, rows=rows, zero_init=zero_init, n_alias=len(aliases)),
        grid=(bsz // nb, nblk),
        in_specs=in_specs,
        out_specs=out_specs,
        out_shape=out_shape,
        input_output_aliases=aliases,
        scratch_shapes=[pltpu.VMEM((nb, 2, D_STATE, 2 * HEAD_LANES), F32),
                        pltpu.VMEM((nb, 2, 1, HEAD_LANES), F32)],
        compiler_params=_params(("parallel", "arbitrary")),
        name="mlstm_scan",
    )(*args)
    return (hf.reshape(bsz * t, D_REC), hb.reshape(bsz * t, D_REC), *state_out)


def _ssd_chunk(x4, bcm, dt128, da128, d, sg_prev, reads, block, tri, b_sel, s_sel, bd):
    chans = [GATE_DT + d * N_HEADS + h for h in range(N_HEADS)]
    dt = _spread(dt128, chans)
    da = _spread(da128, chans)
    ac = _cumsum_rows(tri[d], da)
    atot = jnp.sum(da, 0, keepdims=True)
    a_row = jnp.sum(reads[1 - d] * da, 0, keepdims=True)
    decay = jnp.exp(ac - a_row + block[d])
    bmat = bcm[:, :LANES].astype(BF16)
    cmat = bcm[:, LANES:].astype(BF16)
    bbd = b_sel * _stack_heads(bmat)
    g4 = lax.dot_general(cmat, bbd, NT_DIMS, preferred_element_type=F32)
    xbd = bd * _stack_heads((x4 * dt).astype(BF16))
    y = (jnp.dot((g4 * decay).astype(BF16), xbd, preferred_element_type=F32)
         + jnp.dot(cmat, sg_prev.astype(BF16), preferred_element_type=F32) * jnp.exp(ac))
    w = jnp.exp(atot - ac) * dt
    dsg = lax.dot_general(bmat, (x4 * w).astype(BF16), TN_DIMS, preferred_element_type=F32)
    sg_new = jnp.exp(atot) * sg_prev + s_sel * dsg
    return y, sg_new


def _ssd_kernel(*refs, nblk, rows, zero_init, n_alias):
    xf_ref, bcf_ref, gf_ref, xb_ref, bcb_ref, gb_ref, dtb_ref, alog_ref, dskip_ref = refs[:9]
    if not zero_init:
        s0_ref = refs[9]
    outs = refs[9 + (0 if zero_init else 1) + n_alias:-1]
    yf_ref, yb_ref = outs[:2]
    s_scr = refs[-1]
    j = pl.program_id(1)
    nchunk = rows // CHUNK

    @pl.when(j == 0)
    def _():
        if zero_init:
            s_scr[...] = jnp.zeros_like(s_scr)
        else:
            s_scr[...] = s0_ref[0]

    reads, block, tri, _ = _scan_consts()
    bd = _group_mask(HEAD_LANES, HEAD_LANES, 0, 0, BF16)
    b_sel = _group_mask(HEAD_LANES, LANES, 1, 0, BF16)
    s_sel = _group_mask(LANES, HEAD_LANES, 0, 1, F32)
    a_coef = -jnp.exp(alog_ref[...])

    state = [s_scr[d] for d in range(2)]
    for ci in range(nchunk):
        for d in range(2):
            x_ref, bc_ref, g_ref, y_ref = ((xf_ref, bcf_ref, gf_ref, yf_ref) if d == 0
                                           else (xb_ref, bcb_ref, gb_ref, yb_ref))
            cj = ci if d == 0 else nchunk - 1 - ci
            rs = slice(cj * CHUNK, (cj + 1) * CHUNK)
            dt128 = jax.nn.softplus(g_ref[rs, :] + dtb_ref[...])
            x4 = x_ref[rs, :]
            y, state[d] = _ssd_chunk(x4, bc_ref[rs, :], dt128, dt128 * a_coef, d, state[d],
                                     reads, block, tri, b_sel, s_sel, bd)
            if d == 0:
                y = y + dskip_ref[...] * x4
            y_ref[rs, :] = y
    for d in range(2):
        s_scr[d] = state[d]

    if zero_init:
        sout_ref = outs[2]

        @pl.when(j == nblk - 1)
        def _():
            for d in range(2):
                s_t = s_scr[d].T
                for h in range(N_HEADS):
                    g0 = (h // 2) * D_STATE
                    sout_ref[0, 0, d, h] = s_t[h * D_STATE:(h + 1) * D_STATE, g0:g0 + D_STATE]
            _zero_later_layers(sout_ref)


def _ssd(u, dt_bias_row, alog_row, dskip_row, bsz, t, init, l=0, cache=None):
    rows = min(t, SCAN_ROWS)
    nblk = t // rows
    zero_init = init is None

    def fwd(col):
        return lambda b, j: (b * nblk + j, col)

    def bwd(col):
        return lambda b, j: (b * nblk + nblk - 1 - j, col)

    in_specs = []
    for mk in (fwd, bwd):
        in_specs += [pl.BlockSpec((rows, D_REC), mk(COL_SX // D_REC)), pl.BlockSpec((rows, D_REC), mk(COL_SBC // D_REC)),
                     pl.BlockSpec((rows, LANES), mk(COL_SMALL // LANES))]
    in_specs += [pl.BlockSpec((1, LANES), lambda b, j: (0, 0)),
                 pl.BlockSpec((1, LANES), lambda b, j: (0, 0)),
                 pl.BlockSpec((1, D_REC), lambda b, j: (0, 0))]
    args = [u] * 6 + [dt_bias_row, alog_row, dskip_row]
    out_specs = [pl.BlockSpec((rows, D_REC), fwd(0)), pl.BlockSpec((rows, D_REC), bwd(0))]
    out_shape = [jax.ShapeDtypeStruct((bsz * t, D_REC), F32), jax.ShapeDtypeStruct((bsz * t, D_REC), F32)]
    aliases = {}
    if zero_init:
        shape, spec, _ = _layer_cache((2, N_HEADS, D_STATE, D_STATE), l, bsz)
        out_shape.append(shape)
        out_specs.append(spec)
        if cache is not None:
            in_specs.append(pl.BlockSpec(memory_space=pl.ANY))
            aliases = {len(args): 2}
            args.append(cache)
    else:
        in_specs.append(pl.BlockSpec((1, 2, LANES, HEAD_LANES), lambda b, j: (b, 0, 0, 0)))
        args.append(init)
    return pl.pallas_call(
        functools.partial(_ssd_kernel, nblk=nblk, rows=rows, zero_init=zero_init, n_alias=len(aliases)),
        grid=(bsz, nblk),
        in_specs=in_specs,
        out_specs=out_specs,
        out_shape=out_shape,
        input_output_aliases=aliases,
        scratch_shapes=[pltpu.VMEM((2, LANES, HEAD_LANES), F32)],
        compiler_params=_params(("parallel", "arbitrary")),
        name="ssd_scan",
    )(*args)


def _outproj_kernel(att_ref, hf_ref, hb_ref, mo_ref, yf_ref, yb_ref, z_ref, w_ref, x_ref, gate_ref,
                    mnw_ref, snw_ref, lg_ref, lb_ref, o_ref):
    hh = hf_ref[...] + hb_ref[...]
    parts = []
    for h in range(N_HEADS):
        xh = hh[:, h * D_STATE:(h + 1) * D_STATE]
        mu = jnp.mean(xh, -1, keepdims=True)
        dlt = xh - mu
        var = jnp.mean(dlt * dlt, -1, keepdims=True)
        parts.append(dlt * lax.rsqrt(var + EPS))
    ml = jax.nn.sigmoid(mo_ref[...]) * jnp.concatenate(parts, axis=1) * mnw_ref[...]
    yz = (yf_ref[...] + yb_ref[...]) * _silu(z_ref[...])
    parts = []
    for grp in range(N_GROUPS):
        yg = yz[:, grp * LANES:(grp + 1) * LANES]
        parts.append(yg * lax.rsqrt(jnp.mean(yg * yg, -1, keepdims=True) + EPS))
    ssm = jnp.concatenate(parts, axis=1) * snw_ref[...]
    mixed = (jnp.dot(att_ref[...], w_ref[0:D_ATT], preferred_element_type=F32)
             + _bdot(ml, w_ref[D_ATT:D_ATT + D_REC])
             + _bdot(ssm, w_ref[D_ATT + D_REC:D_MODEL]))
    y = ALPHA * x_ref[...] + gate_ref[0] * mixed
    o_ref[...] = _layernorm_rows(y, lg_ref[...], lb_ref[...])


def _out_proj(att, hf, hb, yf, yb, u, w, x, gate, mnw, snw, lg, lb, rows_per_mod):
    n = x.shape[0]
    tpb = rows_per_mod // ROW_TILE
    row = lambda width, col: pl.BlockSpec((ROW_TILE, width), lambda i: (i, col))
    vec = lambda width: pl.BlockSpec((1, width), lambda i: (0, 0))
    return pl.pallas_call(
        _outproj_kernel,
        grid=(n // ROW_TILE,),
        in_specs=[row(D_ATT, 0), row(D_REC, 0), row(D_REC, 0), row(D_REC, COL_MO // D_REC),
                  row(D_REC, 0), row(D_REC, 0), row(D_REC, COL_SZ // D_REC),
                  pl.BlockSpec((D_MODEL, D_MODEL), lambda i: (0, 0)),
                  row(D_MODEL, 0),
                  pl.BlockSpec((1, 1, D_MODEL), lambda i: (i // tpb, 0, 0)),
                  vec(D_REC), vec(D_REC), vec(D_MODEL), vec(D_MODEL)],
        out_specs=row(D_MODEL, 0),
        out_shape=jax.ShapeDtypeStruct((n, D_MODEL), F32),
        compiler_params=_params(("parallel",)),
        name="out_proj",
    )(att, hf, hb, u, yf, yb, u, w, x, gate, mnw.reshape(1, D_REC), snw.reshape(1, D_REC),
      lg.reshape(1, D_MODEL), lb.reshape(1, D_MODEL))


FF_TILE = D_FF // 2
FFN_ROWS = 1024
FFN_SUB = 512
FFN_VMEM_LIMIT = 56 * 1024 * 1024


def _swiglu_partial(h, w1, w3, w2):
    a = jnp.dot(h, w1, preferred_element_type=F32)
    b = jnp.dot(h, w3, preferred_element_type=F32)
    return jnp.dot((_silu(a) * b).astype(BF16), w2, preferred_element_type=F32)


def _ffn_kernel(x_ref, sc_ref, sh_ref, gate_ref, w1_ref, w3_ref, w2_ref, lg_ref, lb_ref, o_ref):
    for r0 in range(0, FFN_ROWS, FFN_SUB):
        rows = slice(r0, r0 + FFN_SUB)
        x = x_ref[rows, :]
        h = (x * (1.0 + sc_ref[0]) + sh_ref[0]).astype(BF16)
        y = ALPHA * x + gate_ref[0] * _swiglu_partial(h, w1_ref[...], w3_ref[...], w2_ref[...])
        o_ref[rows, :] = _layernorm_rows(y, lg_ref[...], lb_ref[...])


def _ffn(x, sc, sh, gate, w1, w3, w2, lg, lb, rows_per_mod):
    n = x.shape[0]
    tpb = rows_per_mod // FFN_ROWS
    modspec = pl.BlockSpec((1, 1, D_MODEL), lambda i: (i // tpb, 0, 0))
    vec = pl.BlockSpec((1, D_MODEL), lambda i: (0, 0))
    once = pl.Buffered(1)
    return pl.pallas_call(
        _ffn_kernel,
        grid=(n // FFN_ROWS,),
        in_specs=[pl.BlockSpec((FFN_ROWS, D_MODEL), lambda i: (i, 0)), modspec, modspec, modspec,
                  pl.BlockSpec((D_MODEL, D_FF), lambda i: (0, 0), pipeline_mode=once),
                  pl.BlockSpec((D_MODEL, D_FF), lambda i: (0, 0), pipeline_mode=once),
                  pl.BlockSpec((D_FF, D_MODEL), lambda i: (0, 0), pipeline_mode=once), vec, vec],
        out_specs=pl.BlockSpec((FFN_ROWS, D_MODEL), lambda i: (i, 0)),
        out_shape=jax.ShapeDtypeStruct((n, D_MODEL), F32),
        compiler_params=_params(("parallel",), FFN_VMEM_LIMIT),
        name="ffn_dense",
    )(x, sc, sh, gate, w1, w3, w2, lg.reshape(1, D_MODEL), lb.reshape(1, D_MODEL))


def _router_kernel(x_ref, sc_ref, sh_ref, rw_ref, gates_ref, h_ref):
    h = x_ref[...] * (1.0 + sc_ref[0]) + sh_ref[0]
    h_ref[...] = h.astype(BF16)
    logits = jnp.dot(h, rw_ref[...], precision=HIGHEST, preferred_element_type=F32)
    lane = lax.broadcasted_iota(jnp.int32, logits.shape, 1)
    valid = lane < N_EXPERTS
    p = jnp.where(valid, _softmax_rows(jnp.where(valid, logits, -jnp.inf)), -2.0)
    p1 = jnp.max(p, -1, keepdims=True)
    i1 = jnp.min(jnp.where(p == p1, lane, LANES), -1, keepdims=True)
    rest = jnp.where(lane == i1, -1.0, p)
    p2 = jnp.max(rest, -1, keepdims=True)
    i2 = jnp.min(jnp.where(rest == p2, lane, LANES), -1, keepdims=True)
    tot = p1 + p2
    gates_ref[...] = jnp.where(lane == i1, p1 / tot, jnp.where(lane == i2, p2 / tot, 0.0))


def _router(x, sc, sh, router_w, rows_per_mod):
    n = x.shape[0]
    tpb = rows_per_mod // ROW_TILE
    modspec = pl.BlockSpec((1, 1, D_MODEL), lambda i: (i // tpb, 0, 0))
    rw = jnp.pad(router_w, ((0, 0), (0, LANES - N_EXPERTS)))
    return pl.pallas_call(
        _router_kernel,
        grid=(n // ROW_TILE,),
        in_specs=[pl.BlockSpec((ROW_TILE, D_MODEL), lambda i: (i, 0)), modspec, modspec,
                  pl.BlockSpec((D_MODEL, LANES), lambda i: (0, 0))],
        out_specs=[pl.BlockSpec((ROW_TILE, LANES), lambda i: (i, 0)),
                   pl.BlockSpec((ROW_TILE, D_MODEL), lambda i: (i, 0))],
        out_shape=[jax.ShapeDtypeStruct((n, LANES), F32), jax.ShapeDtypeStruct((n, D_MODEL), BF16)],
        compiler_params=_params(("parallel",)),
        name="router",
    )(x, sc, sh, rw)


MOE_ROWS = 1024
MOE_TILE = 128
MOE_MAX_TILES = MOE_ROWS // MOE_TILE
MOE_VMEM_LIMIT = 56 * 1024 * 1024


def _moe_kernel(h_ref, gates_ref, w1_ref, w3_ref, w2_ref, x_ref, gate_ref, lg_ref, lb_ref, o_ref,
                slot_scr, slott_scr, hs_scr, ys_scr):
    e = pl.program_id(1)
    j = pl.program_id(2)
    last_j = pl.num_programs(2) - 1

    @pl.when(jnp.logical_and(e == 0, j == 0))
    def _():
        r = lax.broadcasted_iota(jnp.int32, (MOE_ROWS, MOE_ROWS), 0)
        c = lax.broadcasted_iota(jnp.int32, (MOE_ROWS, MOE_ROWS), 1)
        before = jnp.where(c < r, 1.0, 0.0).astype(BF16)
        mask = gates_ref[...] != 0.0
        rank = jnp.dot(before, jnp.where(mask, 1.0, 0.0).astype(BF16), preferred_element_type=F32)
        slot = jnp.where(mask, rank, -1.0).astype(jnp.int32)
        slot_scr[...] = slot
        slott_scr[...] = slot.T
        o_ref[...] = jnp.zeros_like(o_ref)

    lane = lax.broadcasted_iota(jnp.int32, (MOE_ROWS, LANES), 1)
    slot_col = jnp.max(jnp.where(lane == e, slot_scr[...], -1), -1, keepdims=True)
    n_tiles = (jnp.max(slot_col) + MOE_TILE) // MOE_TILE

    def tile(k, carry):
        @pl.when(j == 0)
        def _():
            slot_row = slott_scr[pl.ds(e, 1), :]
            rr = lax.broadcasted_iota(jnp.int32, (MOE_TILE, MOE_ROWS), 0) + k * MOE_TILE
            pick = jnp.where(rr == slot_row, 1.0, 0.0).astype(BF16)
            hs_scr[k] = jnp.dot(pick, h_ref[...], preferred_element_type=F32).astype(BF16)
            ys_scr[k] = jnp.zeros((MOE_TILE, D_MODEL), F32)

        ys_scr[k] += _swiglu_partial(hs_scr[k], w1_ref[0], w3_ref[0], w2_ref[0])

        @pl.when(j == last_j)
        def _():
            g_col = jnp.sum(jnp.where(lane == e, gates_ref[...], 0.0), -1, keepdims=True)
            cc = lax.broadcasted_iota(jnp.int32, (MOE_ROWS, 2 * MOE_TILE), 1)
            cc = jnp.where(cc >= MOE_TILE, cc - MOE_TILE, cc) + k * MOE_TILE
            put = jnp.where(slot_col == cc, 1.0, 0.0).astype(BF16)
            y = ys_scr[k]
            y_hi = y.astype(BF16)
            y_lo = (y - y_hi.astype(F32)).astype(BF16)
            back = jnp.dot(put, jnp.concatenate([y_hi, y_lo], axis=0), preferred_element_type=F32)
            o_ref[...] += g_col * back

        return carry

    lax.fori_loop(0, n_tiles, tile, 0)

    @pl.when(jnp.logical_and(e == pl.num_programs(1) - 1, j == last_j))
    def _():
        y = ALPHA * x_ref[...] + gate_ref[0] * o_ref[...]
        o_ref[...] = _layernorm_rows(y, lg_ref[...], lb_ref[...])


def _moe(h, gates, w1, w3, w2, x, gate, lg, lb, rows_per_mod):
    n = h.shape[0]
    tpb = rows_per_mod // MOE_ROWS
    vec = pl.BlockSpec((1, D_MODEL), lambda i, e, j: (0, 0))
    return pl.pallas_call(
        _moe_kernel,
        grid=(n // MOE_ROWS, N_EXPERTS, D_FF // FF_TILE),
        in_specs=[pl.BlockSpec((MOE_ROWS, D_MODEL), lambda i, e, j: (i, 0)),
                  pl.BlockSpec((MOE_ROWS, LANES), lambda i, e, j: (i, 0)),
                  pl.BlockSpec((1, D_MODEL, FF_TILE), lambda i, e, j: (e, 0, j)),
                  pl.BlockSpec((1, D_MODEL, FF_TILE), lambda i, e, j: (e, 0, j)),
                  pl.BlockSpec((1, FF_TILE, D_MODEL), lambda i, e, j: (e, j, 0)),
                  pl.BlockSpec((MOE_ROWS, D_MODEL), lambda i, e, j: (i, 0)),
                  pl.BlockSpec((1, 1, D_MODEL), lambda i, e, j: (i // tpb, 0, 0)), vec, vec],
        out_specs=pl.BlockSpec((MOE_ROWS, D_MODEL), lambda i, e, j: (i, 0)),
        out_shape=jax.ShapeDtypeStruct((n, D_MODEL), F32),
        scratch_shapes=[pltpu.VMEM((MOE_ROWS, LANES), jnp.int32), pltpu.VMEM((LANES, MOE_ROWS), jnp.int32),
                        pltpu.VMEM((MOE_MAX_TILES, MOE_TILE, D_MODEL), BF16),
                        pltpu.VMEM((MOE_MAX_TILES, MOE_TILE, D_MODEL), F32)],
        compiler_params=_params(("parallel", "arbitrary", "arbitrary"), MOE_VMEM_LIMIT),
        name="moe",
    )(h, gates, w1, w3, w2, x, gate, lg.reshape(1, D_MODEL), lb.reshape(1, D_MODEL))


def _permute_w_in(w):
    pad = jnp.zeros((D_MODEL, U_COLS - ORIG_END), w.dtype)
    return jnp.concatenate([w[:, :ORIG_GATES], w[:, ORIG_SX:ORIG_DT], w[:, ORIG_SZ:ORIG_SX],
                            w[:, ORIG_GATES:ORIG_SZ], w[:, ORIG_DT:ORIG_END], pad], axis=1).astype(BF16)


def _small_row(vals, offset):
    v = vals.reshape(-1).astype(F32)
    return jnp.zeros((1, LANES), F32).at[0, offset:offset + v.shape[0]].set(v)


def _pack_mlstm_state(c, n, m):
    shape = c.shape[:2] + (D_STATE, HEAD_LANES)
    c_rows = jnp.swapaxes(c, 2, 3).reshape(shape)
    n_rows = jnp.broadcast_to(jnp.swapaxes(n, 2, 3)[..., None], c.shape[:2] + (D_STATE, N_HEADS, D_STATE))
    return (jnp.concatenate([c_rows, n_rows.reshape(shape)], axis=-1),
            jnp.repeat(m, D_STATE, axis=-1)[:, :, None, :])


def _pack_ssd_state(s):
    sel = (jnp.arange(N_GROUPS)[:, None] == jnp.arange(N_HEADS)[None, :] // 2).astype(F32)
    return jnp.einsum('bdhpn,gh->bdgnhp', s, sel).reshape(s.shape[:2] + (LANES, HEAD_LANES))


def _layer(x, mods, P, l, bsz, t, ctx, caches=None):
    sh1, sc1, g1, sh2, sc2, g2 = mods
    rows_per_mod = x.shape[0] // sh1.shape[0]
    lam_init = 0.8 - 0.6 * math.exp(-0.3 * l)
    u = _in_proj(x, sc1, sh1, P['w_in'][l], P['conv_w'][l], P['conv_b'][l], rows_per_mod, t)

    gate_bias = (_small_row(P['mlstm_gate_b'][l, 0], GATE_I) + _small_row(P['mlstm_gate_b'][l, 1], GATE_F))
    dt_bias = _small_row(P['ssm_dt_bias'][l], GATE_DT)
    alog = _small_row(P['ssm_A_log'][l], GATE_DT)
    dskip = jnp.repeat(P['ssm_D'][l].astype(F32), D_STATE).reshape(1, D_REC)

    if ctx is None:
        att, k_new, v_new = _attention_ctx(u, P['attn_lambda'][l], P['attn_norm_w'][l], lam_init, bsz, t, l,
                                           None if caches is None else caches[0:2])
        m_init = s_init = None
    else:
        ck, cv, c_c, c_n, c_m, c_s = ctx
        q, k, v = _rope_prep(u, bsz, t)
        k_all = jnp.concatenate([k, ck.astype(BF16)], axis=3)
        v_all = jnp.concatenate([v, cv.astype(BF16)], axis=2)
        att = _attention_lat(q, k_all, v_all, P['attn_lambda'][l], P['attn_norm_w'][l], lam_init, bsz, t)
        m_init = _pack_mlstm_state(c_c, c_n, c_m)
        s_init = _pack_ssd_state(c_s)
    hf, hb, *mlstm_caches = _mlstm(u, gate_bias, bsz, t, m_init, l, None if caches is None else caches[2:5])
    yf, yb, *ssd_caches = _ssd(u, dt_bias, alog, dskip, bsz, t, s_init, l,
                               None if caches is None else caches[5])

    x = _out_proj(att, hf, hb, yf, yb, u, P['w_out'][l], x, g1, P['mlstm_norm_w'][l], P['ssm_norm_w'][l],
                  P['ln_g'][l, 0], P['ln_b'][l, 0], rows_per_mod)
    if l % 2 == 0:
        x = _ffn(x, sc2, sh2, g2, P['ffn_w1'][l // 2], P['ffn_w3'][l // 2], P['ffn_w2'][l // 2],
                 P['ln_g'][l, 1], P['ln_b'][l, 1], rows_per_mod)
    else:
        gates, h2 = _router(x, sc2, sh2, P['router_w'][l // 2], rows_per_mod)
        x = _moe(h2, gates, P['moe_w1'][l // 2], P['moe_w3'][l // 2], P['moe_w2'][l // 2],
                 x, g2, P['ln_g'][l, 1], P['ln_b'][l, 1], rows_per_mod)
    if ctx is None:
        return x, (k_new, v_new, *mlstm_caches, *ssd_caches)
    return x, None


def kernel(x_prompt, x_sample, c, cache_attn_k, cache_attn_v, state_mlstm_C, state_mlstm_n, state_mlstm_m, state_ssm, c_ctx, w_ada, b_ada, w_in, w_out, attn_lambda, attn_norm_w, mlstm_gate_b, mlstm_norm_w, conv_w, conv_b, ssm_A_log, ssm_dt_bias, ssm_D, ssm_norm_w, ln_g, ln_b, ffn_w1, ffn_w3, ffn_w2, router_w, moe_w1, moe_w3, moe_w2):
    bsz, seq, _ = x_prompt.shape
    dbsz, dseq, _ = x_sample.shape
    P = dict(w_in=[_permute_w_in(w_in[l]) for l in range(DEPTH)], w_out=w_out.astype(BF16),
             attn_lambda=attn_lambda, attn_norm_w=attn_norm_w, mlstm_gate_b=mlstm_gate_b,
             mlstm_norm_w=mlstm_norm_w, conv_w=conv_w, conv_b=conv_b, ssm_A_log=ssm_A_log,
             ssm_dt_bias=ssm_dt_bias, ssm_D=ssm_D, ssm_norm_w=ssm_norm_w, ln_g=ln_g, ln_b=ln_b,
             ffn_w1=ffn_w1.astype(BF16), ffn_w3=ffn_w3.astype(BF16), ffn_w2=ffn_w2.astype(BF16),
             router_w=router_w, moe_w1=moe_w1.astype(BF16), moe_w3=moe_w3.astype(BF16),
             moe_w2=moe_w2.astype(BF16))

    cvec = jnp.zeros((8, D_MODEL), F32).at[0].set(c_ctx).at[1:1 + dbsz].set(c)
    mod = _modulation(cvec, w_ada, b_ada)

    def mods_for(l, lo, hi):
        return [mod[l, lo:hi, i * D_MODEL:(i + 1) * D_MODEL][:, None, :] for i in range(6)]

    y_prompt = x_prompt.reshape(bsz * seq, D_MODEL)
    caches = None
    for l in range(DEPTH):
        y_prompt, caches = _layer(y_prompt, mods_for(l, 0, 1), P, l, bsz, seq, None, caches)
    new_k, new_v, new_c, new_n, m_spread, new_s = caches

    y_sample = x_sample.reshape(dbsz * dseq, D_MODEL)
    for l in range(DEPTH):
        ctx = (cache_attn_k[:, l], cache_attn_v[:, l], state_mlstm_C[:, l], state_mlstm_n[:, l],
               state_mlstm_m[:, l], state_ssm[:, l])
        y_sample, _ = _layer(y_sample, mods_for(l, 1, 1 + dbsz), P, l, dbsz, dseq, ctx)

    return (y_prompt.reshape(bsz, seq, D_MODEL), y_sample.reshape(dbsz, dseq, D_MODEL),
            new_k, new_v, new_c, new_n, m_spread[:, :, :, 0, ::D_STATE], new_s)
```

```python
import functools
import math

import jax
import jax.numpy as jnp
from jax import lax
from jax.experimental import pallas as pl
from jax.experimental.pallas import tpu as pltpu

F32 = jnp.float32
BF16 = jnp.bfloat16
HIGHEST = lax.Precision.HIGHEST

D_MODEL = 1024
DEPTH = 2
GRID_W = 64
N_HEADS = 4
D_ATT = 512
D_HEAD_V = 128
D_QK = 64
D_REC = 256
D_STATE = 64
N_GROUPS = 2
D_CONV = 3
D_FF = 2816
N_EXPERTS = 8
ALPHA = (2.0 * DEPTH) ** 0.25
CHUNK = 64
ROPE_BASE = 10000.0
LOG2E = 1.4426950408889634
EPS = 1e-5

COL_AQ, COL_AK, COL_AV = 0, 512, 1024
COL_MQ, COL_MK, COL_MV, COL_MO = 1536, 1792, 2048, 2304
COL_SX, COL_SBC, COL_SZ = 2560, 2816, 3072
COL_SMALL = 3328
U_COLS = 3584
ORIG_GATES, ORIG_SZ, ORIG_SX, ORIG_DT, ORIG_END = 2560, 2576, 2832, 3344, 3352
GATE_I, GATE_F, GATE_DT = 0, 8, 16

LANES = 128
ROW_TILE = 512
SCAN_ROWS = 512
SCAN_BATCH = 2
VMEM_LIMIT = 48 * 1024 * 1024

NT_DIMS = (((1,), (1,)), ((), ()))
TN_DIMS = (((0,), (0,)), ((), ()))


def _params(sem, vmem=VMEM_LIMIT):
    return pltpu.CompilerParams(dimension_semantics=sem, vmem_limit_bytes=vmem)


def _silu(x):
    return x * jax.nn.sigmoid(x)


def _bdot(a, b):
    return jnp.dot(a.astype(BF16), b.astype(BF16), preferred_element_type=F32)


def _bdot_nt(a, b):
    return lax.dot_general(a.astype(BF16), b.astype(BF16), NT_DIMS, preferred_element_type=F32)


def _layernorm_rows(y, g, b):
    mu = jnp.mean(y, -1, keepdims=True)
    d = y - mu
    var = jnp.mean(d * d, -1, keepdims=True)
    return d * lax.rsqrt(var + EPS) * g + b


def _mod_kernel(c_ref, w_ref, b_ref, o_ref):
    o_ref[0] = jnp.dot(_silu(c_ref[...]), w_ref[0], precision=HIGHEST,
                       preferred_element_type=F32) + b_ref[0]


def _modulation(cvec, w_ada, b_ada):
    tn = 1536
    return pl.pallas_call(
        _mod_kernel,
        grid=(DEPTH, 6 * D_MODEL // tn),
        in_specs=[pl.BlockSpec((8, D_MODEL), lambda l, j: (0, 0)),
                  pl.BlockSpec((1, D_MODEL, tn), lambda l, j: (l, 0, j)),
                  pl.BlockSpec((1, 1, tn), lambda l, j: (l, 0, j))],
        out_specs=pl.BlockSpec((1, 8, tn), lambda l, j: (l, 0, j)),
        out_shape=jax.ShapeDtypeStruct((DEPTH, 8, 6 * D_MODEL), F32),
        compiler_params=_params(("parallel", "parallel")),
        name="modulation",
    )(cvec, w_ada, b_ada.reshape(DEPTH, 1, 6 * D_MODEL))


CONV_COLS = 2 * D_REC
HALO = 16


def _inproj_kernel(x_ref, prev_ref, next_ref, sc_ref, sh_ref, w_ref, cw_ref, cb_ref, o_ref, *, seq):
    def modulate(x):
        return (x * (1.0 + sc_ref[0]) + sh_ref[0]).astype(BF16)

    h = modulate(x_ref[...])
    for n0 in range(0, U_COLS, CONV_COLS):
        if n0 != COL_SX:
            o_ref[:, n0:n0 + CONV_COLS] = jnp.dot(h, w_ref[:, n0:n0 + CONV_COLS], preferred_element_type=F32)
            continue
        h_ext = jnp.concatenate([h, modulate(prev_ref[...]), modulate(next_ref[...])], axis=0)
        pre_ext = jnp.dot(h_ext, w_ref[:, n0:n0 + CONV_COLS], preferred_element_type=F32)
        pre = pre_ext[:ROW_TILE]
        before = pre_ext[ROW_TILE + HALO - 1:ROW_TILE + HALO]
        after = pre_ext[ROW_TILE + HALO:ROW_TILE + HALO + 1]
        r = lax.broadcasted_iota(jnp.int32, (ROW_TILE, 1), 0)
        pos = (r + pl.program_id(0) * ROW_TILE) % seq
        up = jnp.where(r == 0, before, pltpu.roll(pre, 1, 0))
        down = jnp.where(r == ROW_TILE - 1, after, pltpu.roll(pre, ROW_TILE - 1, 0))
        up = jnp.where(pos == 0, 0.0, up)
        down = jnp.where(pos == seq - 1, 0.0, down)
        cw = cw_ref[...]
        o_ref[:, n0:n0 + CONV_COLS] = _silu(up * cw[0:1] + pre * cw[1:2] + down * cw[2:3] + cb_ref[...])


def _in_proj(x, sc, sh, w, conv_w, conv_b, rows_per_mod, seq):
    n = x.shape[0]
    tpb = rows_per_mod // ROW_TILE
    r8 = ROW_TILE // HALO
    last8 = n // HALO - 1
    assert COL_SX % CONV_COLS == 0 and U_COLS % CONV_COLS == 0
    return pl.pallas_call(
        functools.partial(_inproj_kernel, seq=seq),
        grid=(n // ROW_TILE,),
        in_specs=[pl.BlockSpec((ROW_TILE, D_MODEL), lambda i: (i, 0)),
                  pl.BlockSpec((HALO, D_MODEL), lambda i: (jnp.maximum(i * r8 - 1, 0), 0)),
                  pl.BlockSpec((HALO, D_MODEL), lambda i: (jnp.minimum((i + 1) * r8, last8), 0)),
                  pl.BlockSpec((1, 1, D_MODEL), lambda i: (i // tpb, 0, 0)),
                  pl.BlockSpec((1, 1, D_MODEL), lambda i: (i // tpb, 0, 0)),
                  pl.BlockSpec((D_MODEL, U_COLS), lambda i: (0, 0)),
                  pl.BlockSpec((D_CONV, CONV_COLS), lambda i: (0, 0)),
                  pl.BlockSpec((1, CONV_COLS), lambda i: (0, 0))],
        out_specs=pl.BlockSpec((ROW_TILE, U_COLS), lambda i: (i, 0)),
        out_shape=jax.ShapeDtypeStruct((n, U_COLS), F32),
        compiler_params=_params(("parallel",)),
        name="in_proj",
    )(x, x, x, sc, sh, w, conv_w, conv_b.reshape(1, CONV_COLS))


def _lambda_scalar(lam_ref, lam_init):
    lp = lam_ref[...]
    s01 = jnp.sum(lp[0:1] * lp[1:2], axis=-1, keepdims=True)
    s23 = jnp.sum(lp[2:3] * lp[3:4], axis=-1, keepdims=True)
    return jnp.exp(s01) - jnp.exp(s23) + lam_init


def _softmax_rows(s):
    e = jnp.exp(s - jnp.max(s, -1, keepdims=True))
    return e / jnp.sum(e, -1, keepdims=True)


def _head_norm(o, nw, lam_init):
    return o * lax.rsqrt(jnp.mean(o * o, -1, keepdims=True) + EPS) * nw * (1.0 - lam_init)


def _layer_cache(tail, l, bsz, nb=1):
    zeros = (0,) * len(tail)
    owned = DEPTH if l == 0 else 1
    spec = pl.BlockSpec((nb, owned) + tuple(tail), lambda b, *_: (b, l) + zeros)
    return jax.ShapeDtypeStruct((bsz, DEPTH) + tuple(tail), F32), spec, owned


def _zero_later_layers(ref):
    for b in range(ref.shape[0]):
        for later in range(1, ref.shape[1]):
            ref[b, later] = jnp.zeros(ref.shape[2:], ref.dtype)


def _attn_ctx_kernel(u_ref, lam_ref, nw_ref, *rest, lam_init):
    att_ref, k_ref, v_ref = rest[-3:]
    lam = _lambda_scalar(lam_ref, lam_init)
    for h in range(N_HEADS):
        v = u_ref[:, COL_AV + h * D_HEAD_V:COL_AV + (h + 1) * D_HEAD_V]
        v_ref[0, 0, h] = v
        ps = []
        for m in range(2):
            c0 = h * D_HEAD_V + m * D_QK
            q = u_ref[:, COL_AQ + c0:COL_AQ + c0 + D_QK] * (D_QK ** -0.5)
            k = u_ref[:, COL_AK + c0:COL_AK + c0 + D_QK]
            k_ref[0, 0, h, m] = k
            ps.append(_softmax_rows(_bdot_nt(q, k)))
        o = _bdot(ps[0] - lam * ps[1], v)
        att_ref[:, h * D_HEAD_V:(h + 1) * D_HEAD_V] = _head_norm(o, nw_ref[...], lam_init).astype(BF16)
    _zero_later_layers(k_ref)
    _zero_later_layers(v_ref)


def _attention_ctx(u, lam_p, norm_w, lam_init, bsz, t, l, caches):
    n = bsz * t
    k_shape, k_spec, _ = _layer_cache((N_HEADS, 2, t, D_QK), l, bsz)
    v_shape, v_spec, _ = _layer_cache((N_HEADS, t, D_HEAD_V), l, bsz)
    in_specs = [pl.BlockSpec((t, 3 * D_ATT), lambda b: (b, 0)),
                pl.BlockSpec((4, D_QK), lambda b: (0, 0)),
                pl.BlockSpec((1, D_HEAD_V), lambda b: (0, 0))]
    args = [u, lam_p, norm_w.reshape(1, D_HEAD_V)]
    aliases = {}
    if caches is not None:
        in_specs += [pl.BlockSpec(memory_space=pl.ANY)] * 2
        aliases = {len(args): 1, len(args) + 1: 2}
        args += list(caches)
    return pl.pallas_call(
        functools.partial(_attn_ctx_kernel, lam_init=lam_init),
        grid=(bsz,),
        in_specs=in_specs,
        out_specs=[pl.BlockSpec((t, D_ATT), lambda b: (b, 0)), k_spec, v_spec],
        out_shape=[jax.ShapeDtypeStruct((n, D_ATT), BF16), k_shape, v_shape],
        input_output_aliases=aliases,
        compiler_params=_params(("parallel",)),
        name="attn_ctx",
    )(*args)


def _rope_kernel(u_ref, cos_ref, sa_ref, sb_ref, q_ref, k_ref, v_ref):
    cos, sa, sb = cos_ref[...], sa_ref[...], sb_ref[...]

    def rope(x):
        return x * cos + pltpu.roll(x, LANES - 16, 1) * sa + pltpu.roll(x, 16, 1) * sb

    for h in range(N_HEADS):
        q = rope(u_ref[:, COL_AQ + h * D_HEAD_V:COL_AQ + (h + 1) * D_HEAD_V]) * (LOG2E * D_QK ** -0.5)
        k = rope(u_ref[:, COL_AK + h * D_HEAD_V:COL_AK + (h + 1) * D_HEAD_V])
        for m in range(2):
            q_ref[0, h, m] = q[:, m * D_QK:(m + 1) * D_QK].astype(BF16)
            k_ref[0, h, m] = k[:, m * D_QK:(m + 1) * D_QK].astype(BF16)
        v_ref[0, h] = u_ref[:, COL_AV + h * D_HEAD_V:COL_AV + (h + 1) * D_HEAD_V].astype(BF16)


def _rope_tables(t):
    rows = jnp.repeat(jnp.arange(t // GRID_W, dtype=F32), GRID_W)
    cols = jnp.tile(jnp.arange(GRID_W, dtype=F32), t // GRID_W)
    half = D_QK // 2
    inv = ROPE_BASE ** (-jnp.arange(0, half, 2, dtype=F32) / half)
    ang_r = rows[:, None] * inv
    ang_c = cols[:, None] * inv
    ang = jnp.concatenate([ang_r, ang_r, ang_c, ang_c], -1)
    cos, sin = jnp.cos(ang), jnp.sin(ang)
    quarter = (jnp.arange(D_QK) // (D_QK // 4)) % 2
    sa = jnp.where(quarter == 0, -sin, 0.0)
    sb = jnp.where(quarter == 1, sin, 0.0)
    tile2 = lambda a: jnp.concatenate([a, a], -1)
    return tile2(cos), tile2(sa), tile2(sb)


def _rope_prep(u, bsz, t):
    tr = 512
    nb = t // tr
    cos, sa, sb = _rope_tables(t)
    tab = pl.BlockSpec((tr, LANES), lambda b, i: (i, 0))
    return pl.pallas_call(
        _rope_kernel,
        grid=(bsz, nb),
        in_specs=[pl.BlockSpec((tr, 3 * D_ATT), lambda b, i: (b * nb + i, 0)), tab, tab, tab],
        out_specs=[pl.BlockSpec((1, N_HEADS, 2, tr, D_QK), lambda b, i: (b, 0, 0, i, 0)),
                   pl.BlockSpec((1, N_HEADS, 2, tr, D_QK), lambda b, i: (b, 0, 0, i, 0)),
                   pl.BlockSpec((1, N_HEADS, tr, D_HEAD_V), lambda b, i: (b, 0, i, 0))],
        out_shape=[jax.ShapeDtypeStruct((bsz, N_HEADS, 2, t, D_QK), BF16),
                   jax.ShapeDtypeStruct((bsz, N_HEADS, 2, t, D_QK), BF16),
                   jax.ShapeDtypeStruct((bsz, N_HEADS, t, D_HEAD_V), BF16)],
        compiler_params=_params(("parallel", "parallel")),
        name="rope_prep",
    )(u, cos, sa, sb)


def _attn_lat_kernel(q_ref, k_ref, v_ref, lam_ref, nw_ref, o_ref, *, lam_init):
    lam = _lambda_scalar(lam_ref, lam_init)
    es, sums = [], []
    for m in range(2):
        s = lax.dot_general(q_ref[0, 0, m], k_ref[0, 0, m], NT_DIMS, preferred_element_type=F32)
        e = jnp.exp2(s - jnp.max(s, -1, keepdims=True))
        es.append(e)
        sums.append(jnp.sum(e, -1, keepdims=True))
    a = es[0] - (lam * sums[0] / sums[1]) * es[1]
    o = _bdot(a, v_ref[0, 0]) / sums[0]
    o_ref[...] = _head_norm(o, nw_ref[...], lam_init).astype(BF16)


def _attention_lat(q, k_all, v_all, lam_p, norm_w, lam_init, bsz, t):
    tq = 256
    nq = t // tq
    s = k_all.shape[3]
    return pl.pallas_call(
        functools.partial(_attn_lat_kernel, lam_init=lam_init),
        grid=(bsz, N_HEADS, nq),
        in_specs=[pl.BlockSpec((1, 1, 2, tq, D_QK), lambda b, h, i: (b, h, 0, i, 0)),
                  pl.BlockSpec((1, 1, 2, s, D_QK), lambda b, h, i: (b, h, 0, 0, 0)),
                  pl.BlockSpec((1, 1, s, D_HEAD_V), lambda b, h, i: (b, h, 0, 0)),
                  pl.BlockSpec((4, D_QK), lambda b, h, i: (0, 0)),
                  pl.BlockSpec((1, D_HEAD_V), lambda b, h, i: (0, 0))],
        out_specs=pl.BlockSpec((tq, D_HEAD_V), lambda b, h, i: (b * nq + i, h)),
        out_shape=jax.ShapeDtypeStruct((bsz * t, D_ATT), BF16),
        compiler_params=_params(("parallel", "parallel", "parallel")),
        name="attn_lat",
    )(q, k_all, v_all, lam_p, norm_w.reshape(1, D_HEAD_V))


HEAD_LANES = N_HEADS * D_STATE


def _scan_consts():
    t = lax.broadcasted_iota(jnp.int32, (CHUNK, HEAD_LANES), 0)
    s = lax.broadcasted_iota(jnp.int32, (CHUNK, HEAD_LANES), 1) & (CHUNK - 1)
    r = lax.broadcasted_iota(jnp.int32, (CHUNK, CHUNK), 0)
    c = lax.broadcasted_iota(jnp.int32, (CHUNK, CHUNK), 1)
    reads = (jnp.where(s <= t, 1.0, 0.0), jnp.where(s >= t, 1.0, 0.0))
    block = (jnp.where(s <= t, 0.0, -jnp.inf), jnp.where(s >= t, 0.0, -jnp.inf))
    tri = (jnp.where(c <= r, 1.0, 0.0).astype(BF16), jnp.where(c >= r, 1.0, 0.0).astype(BF16))
    return reads, block, tri, jnp.where(s == t, 1.0, 0.0)


def _cumsum_rows(tri, x):
    hi = x.astype(BF16)
    rest = x - hi.astype(F32)
    mid = rest.astype(BF16)
    lo = (rest - mid.astype(F32)).astype(BF16)
    parts = jnp.dot(tri, jnp.concatenate([hi, mid, lo], axis=1), preferred_element_type=F32)
    return parts[:, :HEAD_LANES] + parts[:, HEAD_LANES:2 * HEAD_LANES] + parts[:, 2 * HEAD_LANES:]


def _group_mask(rows, cols, row_shift, col_shift, dtype):
    r = lax.broadcasted_iota(jnp.int32, (rows, cols), 0) >> 6
    c = (lax.broadcasted_iota(jnp.int32, (rows, cols), 1) >> 6) & (N_HEADS - 1)
    return jnp.where((r >> row_shift) == (c >> col_shift), 1.0, 0.0).astype(dtype)


def _spread(x, chans):
    return jnp.concatenate([jnp.broadcast_to(x[:, c:c + 1], (CHUNK, D_STATE)) for c in chans], axis=1)


def _stack_heads(x):
    return jnp.concatenate([x] * N_HEADS, axis=0)


def _mlstm_chunk(q4, k4, v4, g, d, cn_prev, m_prev, reads, block, tri, eye, bd, bd_f32):
    li = _spread(g, [GATE_I + d * N_HEADS + h for h in range(N_HEADS)])
    lf = _spread(jax.nn.log_sigmoid(g), [GATE_F + d * N_HEADS + h for h in range(N_HEADS)])
    bc = _cumsum_rows(tri[d], lf)
    btot = jnp.sum(lf, 0, keepdims=True)
    b_row = jnp.sum(reads[1 - d] * lf, 0, keepdims=True)
    li_row = jnp.sum(eye * li, 0, keepdims=True)
    dm = bc - b_row + li_row + block[d]
    rmax = jnp.concatenate(
        [jnp.broadcast_to(jnp.max(dm[:, h * D_STATE:(h + 1) * D_STATE], -1, keepdims=True), (CHUNK, D_STATE))
         for h in range(N_HEADS)], axis=1)
    inter = bc + m_prev
    m_t = jnp.maximum(inter, rmax)
    w_inter = jnp.exp(inter - m_t)
    qs = (q4 * (D_STATE ** -0.5)).astype(BF16)
    kbd = bd[:, :HEAD_LANES] * _stack_heads(k4.astype(BF16))
    s4 = lax.dot_general(qs, kbd, NT_DIMS, preferred_element_type=F32) * jnp.exp(dm - m_t)
    vo = jnp.concatenate([v4.astype(BF16), jnp.ones((CHUNK, HEAD_LANES), BF16)], axis=1)
    vbd = bd * _stack_heads(vo)
    nd = (jnp.concatenate([w_inter, w_inter], axis=1)
          * jnp.dot(qs, bd * _stack_heads(cn_prev.astype(BF16)), preferred_element_type=F32)
          + jnp.dot(s4.astype(BF16), vbd, preferred_element_type=F32))
    hc = nd[:, :HEAD_LANES] / jnp.maximum(jnp.abs(nd[:, HEAD_LANES:]), jnp.exp(-m_t))
    gcol = btot - bc + li
    m_new = jnp.maximum(btot + m_prev, jnp.max(gcol, 0, keepdims=True))
    w_c = jnp.exp(btot + m_prev - m_new)
    kw = (k4 * jnp.exp(gcol - m_new)).astype(BF16)
    dcn = lax.dot_general(kw, vo, TN_DIMS, preferred_element_type=F32)
    own = sum(bd_f32[h * D_STATE:(h + 1) * D_STATE] * dcn[h * D_STATE:(h + 1) * D_STATE]
              for h in range(N_HEADS))
    cn_new = jnp.concatenate([w_c, w_c], axis=1) * cn_prev + own
    return hc, cn_new, m_new


def _mlstm_kernel(*refs, nblk, rows, zero_init, n_alias):
    qf_ref, kf_ref, vf_ref, gf_ref, qb_ref, kb_ref, vb_ref, gb_ref, bias_ref = refs[:9]
    if not zero_init:
        c0_ref, m0_ref = refs[9:11]
    outs = refs[9 + (0 if zero_init else 2) + n_alias:-2]
    hf_ref, hb_ref = outs[:2]
    cn_scr, m_scr = refs[-2:]
    j = pl.program_id(1)
    nchunk = rows // CHUNK

    @pl.when(j == 0)
    def _():
        if zero_init:
            cn_scr[...] = jnp.zeros_like(cn_scr)
            m_scr[...] = jnp.zeros_like(m_scr)
        else:
            cn_scr[...] = c0_ref[...]
            m_scr[...] = m0_ref[...]

    reads, block, tri, eye = _scan_consts()
    bd = _group_mask(HEAD_LANES, 2 * HEAD_LANES, 0, 0, BF16)
    bd_f32 = _group_mask(HEAD_LANES, 2 * HEAD_LANES, 0, 0, F32)

    chains = [(b, d) for b in range(SCAN_BATCH) for d in range(2)]
    state = {bd_: (cn_scr[bd_], m_scr[bd_]) for bd_ in chains}
    for ci in range(nchunk):
        for b, d in chains:
            q_ref, k_ref, v_ref, g_ref, h_ref = ((qf_ref, kf_ref, vf_ref, gf_ref, hf_ref) if d == 0
                                                 else (qb_ref, kb_ref, vb_ref, gb_ref, hb_ref))
            cj = ci if d == 0 else nchunk - 1 - ci
            rs = slice(cj * CHUNK, (cj + 1) * CHUNK)
            g = g_ref[b, rs, :] + bias_ref[...]
            hc, cn_new, m_new = _mlstm_chunk(q_ref[b, rs, :], k_ref[b, rs, :], v_ref[b, rs, :], g, d,
                                             *state[b, d], reads, block, tri, eye, bd, bd_f32)
            h_ref[b, rs, :] = hc
            state[b, d] = (cn_new, m_new)
    for bd_ in chains:
        cn_scr[bd_], m_scr[bd_] = state[bd_]

    if zero_init:
        cout_ref, nout_ref, mout_ref = outs[2:]

        @pl.when(j == nblk - 1)
        def _():
            r = lax.broadcasted_iota(jnp.int32, (D_STATE, D_STATE), 0)
            c = lax.broadcasted_iota(jnp.int32, (D_STATE, D_STATE), 1)
            for b, d in chains:
                for h in range(N_HEADS):
                    r0, r1 = h * D_STATE, (h + 1) * D_STATE
                    cout_ref[b, 0, d, h] = cn_scr[b, d, :, r0:r1]
                    n_spread = cn_scr[b, d, :, HEAD_LANES + r0:HEAD_LANES + r1]
                    nout_ref[b, 0, d, h:h + 1, :] = jnp.sum(jnp.where(r == c, n_spread, 0.0), 0, keepdims=True)
                mout_ref[b, 0, d] = m_scr[b, d]
            _zero_later_layers(cout_ref)
            _zero_later_layers(nout_ref)
            _zero_later_layers(mout_ref)


def _mlstm(u, gate_bias, bsz, t, init, l=0, caches=None):
    rows = min(t, SCAN_ROWS)
    nblk = t // rows
    zero_init = init is None

    nb = SCAN_BATCH
    fwd = lambda col: (lambda p, j: (p, j, col))
    bwd = lambda col: (lambda p, j: (p, nblk - 1 - j, col))
    cq, ck, cv, cg = COL_MQ // D_REC, COL_MK // D_REC, COL_MV // D_REC, COL_SMALL // LANES
    in_specs = []
    for mk in (fwd, bwd):
        in_specs += [pl.BlockSpec((nb, rows, D_REC), mk(cq)), pl.BlockSpec((nb, rows, D_REC), mk(ck)),
                     pl.BlockSpec((nb, rows, D_REC), mk(cv)), pl.BlockSpec((nb, rows, LANES), mk(cg))]
    in_specs.append(pl.BlockSpec((1, LANES), lambda p, j: (0, 0)))
    args = [u.reshape(bsz, t, U_COLS)] * 8 + [gate_bias]
    state_c = pl.BlockSpec((nb, 2, D_STATE, 2 * HEAD_LANES), lambda p, j: (p, 0, 0, 0))
    state_m = pl.BlockSpec((nb, 2, 1, HEAD_LANES), lambda p, j: (p, 0, 0, 0))
    out_specs = [pl.BlockSpec((nb, rows, D_REC), fwd(0)), pl.BlockSpec((nb, rows, D_REC), bwd(0))]
    out_shape = [jax.ShapeDtypeStruct((bsz, t, D_REC), F32), jax.ShapeDtypeStruct((bsz, t, D_REC), F32)]
    aliases = {}
    if zero_init:
        for tail in ((2, N_HEADS, D_STATE, D_STATE), (2, N_HEADS, D_STATE), (2, 1, HEAD_LANES)):
            shape, spec, _ = _layer_cache(tail, l, bsz, nb)
            out_shape.append(shape)
            out_specs.append(spec)
        if caches is not None:
            in_specs += [pl.BlockSpec(memory_space=pl.ANY)] * len(caches)
            aliases = {len(args) + i: 2 + i for i in range(len(caches))}
            args += list(caches)
    else:
        in_specs += [state_c, state_m]
        args += list(init)
    hf, hb, *state_out = pl.pallas_call(
        functools.partial(_mlstm_kernel, nblk=nblk, rows=rows, zero_init=zero_init, n_alias=len(aliases)),
        grid=(bsz // nb, nblk),
        in_specs=in_specs,
        out_specs=out_specs,
        out_shape=out_shape,
        input_output_aliases=aliases,
        scratch_shapes=[pltpu.VMEM((nb, 2, D_STATE, 2 * HEAD_LANES), F32),
                        pltpu.VMEM((nb, 2, 1, HEAD_LANES), F32)],
        compiler_params=_params(("parallel", "arbitrary")),
        name="mlstm_scan",
    )(*args)
    return (hf.reshape(bsz * t, D_REC), hb.reshape(bsz * t, D_REC), *state_out)


def _ssd_chunk(x4, bcm, dt128, da128, d, sg_prev, reads, block, tri, b_sel, s_sel, bd):
    chans = [GATE_DT + d * N_HEADS + h for h in range(N_HEADS)]
    dt = _spread(dt128, chans)
    da = _spread(da128, chans)
    ac = _cumsum_rows(tri[d], da)
    atot = jnp.sum(da, 0, keepdims=True)
    a_row = jnp.sum(reads[1 - d] * da, 0, keepdims=True)
    decay = jnp.exp(ac - a_row + block[d])
    bmat = bcm[:, :LANES].astype(BF16)
    cmat = bcm[:, LANES:].astype(BF16)
    bbd = b_sel * _stack_heads(bmat)
    g4 = lax.dot_general(cmat, bbd, NT_DIMS, preferred_element_type=F32)
    xbd = bd * _stack_heads((x4 * dt).astype(BF16))
    y = (jnp.dot((g4 * decay).astype(BF16), xbd, preferred_element_type=F32)
         + jnp.dot(cmat, sg_prev.astype(BF16), preferred_element_type=F32) * jnp.exp(ac))
    w = jnp.exp(atot - ac) * dt
    dsg = lax.dot_general(bmat, (x4 * w).astype(BF16), TN_DIMS, preferred_element_type=F32)
    sg_new = jnp.exp(atot) * sg_prev + s_sel * dsg
    return y, sg_new


def _ssd_kernel(*refs, nblk, rows, zero_init, n_alias):
    xf_ref, bcf_ref, gf_ref, xb_ref, bcb_ref, gb_ref, dtb_ref, alog_ref, dskip_ref = refs[:9]
    if not zero_init:
        s0_ref = refs[9]
    outs = refs[9 + (0 if zero_init else 1) + n_alias:-1]
    yf_ref, yb_ref = outs[:2]
    s_scr = refs[-1]
    j = pl.program_id(1)
    nchunk = rows // CHUNK

    @pl.when(j == 0)
    def _():
        if zero_init:
            s_scr[...] = jnp.zeros_like(s_scr)
        else:
            s_scr[...] = s0_ref[0]

    reads, block, tri, _ = _scan_consts()
    bd = _group_mask(HEAD_LANES, HEAD_LANES, 0, 0, BF16)
    b_sel = _group_mask(HEAD_LANES, LANES, 1, 0, BF16)
    s_sel = _group_mask(LANES, HEAD_LANES, 0, 1, F32)
    a_coef = -jnp.exp(alog_ref[...])

    state = [s_scr[d] for d in range(2)]
    for ci in range(nchunk):
        for d in range(2):
            x_ref, bc_ref, g_ref, y_ref = ((xf_ref, bcf_ref, gf_ref, yf_ref) if d == 0
                                           else (xb_ref, bcb_ref, gb_ref, yb_ref))
            cj = ci if d == 0 else nchunk - 1 - ci
            rs = slice(cj * CHUNK, (cj + 1) * CHUNK)
            dt128 = jax.nn.softplus(g_ref[rs, :] + dtb_ref[...])
            x4 = x_ref[rs, :]
            y, state[d] = _ssd_chunk(x4, bc_ref[rs, :], dt128, dt128 * a_coef, d, state[d],
                                     reads, block, tri, b_sel, s_sel, bd)
            if d == 0:
                y = y + dskip_ref[...] * x4
            y_ref[rs, :] = y
    for d in range(2):
        s_scr[d] = state[d]

    if zero_init:
        sout_ref = outs[2]

        @pl.when(j == nblk - 1)
        def _():
            for d in range(2):
                s_t = s_scr[d].T
                for h in range(N_HEADS):
                    g0 = (h // 2) * D_STATE
                    sout_ref[0, 0, d, h] = s_t[h * D_STATE:(h + 1) * D_STATE, g0:g0 + D_STATE]
            _zero_later_layers(sout_ref)


def _ssd(u, dt_bias_row, alog_row, dskip_row, bsz, t, init, l=0, cache=None):
    rows = min(t, SCAN_ROWS)
    nblk = t // rows
    zero_init = init is None

    def fwd(col):
        return lambda b, j: (b * nblk + j, col)

    def bwd(col):
        return lambda b, j: (b * nblk + nblk - 1 - j, col)

    in_specs = []
    for mk in (fwd, bwd):
        in_specs += [pl.BlockSpec((rows, D_REC), mk(COL_SX // D_REC)),
                     pl.BlockSpec((rows, D_REC), mk(COL_SBC // D_REC)),
                     pl.BlockSpec((rows, LANES), mk(COL_SMALL // LANES))]
    in_specs += [pl.BlockSpec((1, LANES), lambda b, j: (0, 0)),
                 pl.BlockSpec((1, LANES), lambda b, j: (0, 0)),
                 pl.BlockSpec((1, D_REC), lambda b, j: (0, 0))]
    args = [u] * 6 + [dt_bias_row, alog_row, dskip_row]
    out_specs = [pl.BlockSpec((rows, D_REC), fwd(0)), pl.BlockSpec((rows, D_REC), bwd(0))]
    out_shape = [jax.ShapeDtypeStruct((bsz * t, D_REC), F32), jax.ShapeDtypeStruct((bsz * t, D_REC), F32)]
    aliases = {}
    if zero_init:
        shape, spec, _ = _layer_cache((2, N_HEADS, D_STATE, D_STATE), l, bsz)
        out_shape.append(shape)
        out_specs.append(spec)
        if cache is not None:
            in_specs.append(pl.BlockSpec(memory_space=pl.ANY))
            aliases = {len(args): 2}
            args.append(cache)
    else:
        in_specs.append(pl.BlockSpec((1, 2, LANES, HEAD_LANES), lambda b, j: (b, 0, 0, 0)))
        args.append(init)
    return pl.pallas_call(
        functools.partial(_ssd_kernel, nblk=nblk, rows=rows, zero_init=zero_init, n_alias=len(aliases)),
        grid=(bsz, nblk),
        in_specs=in_specs,
        out_specs=out_specs,
        out_shape=out_shape,
        input_output_aliases=aliases,
        scratch_shapes=[pltpu.VMEM((2, LANES, HEAD_LANES), F32)],
        compiler_params=_params(("parallel", "arbitrary")),
        name="ssd_scan",
    )(*args)


def _outproj_kernel(att_ref, hf_ref, hb_ref, mo_ref, yf_ref, yb_ref, z_ref, w_ref, x_ref, gate_ref,
                    mnw_ref, snw_ref, lg_ref, lb_ref, o_ref):
    hh = hf_ref[...] + hb_ref[...]
    parts = []
    for h in range(N_HEADS):
        xh = hh[:, h * D_STATE:(h + 1) * D_STATE]
        mu = jnp.mean(xh, -1, keepdims=True)
        dlt = xh - mu
        var = jnp.mean(dlt * dlt, -1, keepdims=True)
        parts.append(dlt * lax.rsqrt(var + EPS))
    ml = jax.nn.sigmoid(mo_ref[...]) * jnp.concatenate(parts, axis=1) * mnw_ref[...]
    yz = (yf_ref[...] + yb_ref[...]) * _silu(z_ref[...])
    parts = []
    for grp in range(N_GROUPS):
        yg = yz[:, grp * LANES:(grp + 1) * LANES]
        parts.append(yg * lax.rsqrt(jnp.mean(yg * yg, -1, keepdims=True) + EPS))
    ssm = jnp.concatenate(parts, axis=1) * snw_ref[...]
    mixed = (jnp.dot(att_ref[...], w_ref[0:D_ATT], preferred_element_type=F32)
             + _bdot(ml, w_ref[D_ATT:D_ATT + D_REC])
             + _bdot(ssm, w_ref[D_ATT + D_REC:D_MODEL]))
    y = ALPHA * x_ref[...] + gate_ref[0] * mixed
    o_ref[...] = _layernorm_rows(y, lg_ref[...], lb_ref[...])


def _out_proj(att, hf, hb, yf, yb, u, w, x, gate, mnw, snw, lg, lb, rows_per_mod):
    n = x.shape[0]
    tpb = rows_per_mod // ROW_TILE
    row = lambda width, col: pl.BlockSpec((ROW_TILE, width), lambda i: (i, col))
    vec = lambda width: pl.BlockSpec((1, width), lambda i: (0, 0))
    return pl.pallas_call(
        _outproj_kernel,
        grid=(n // ROW_TILE,),
        in_specs=[row(D_ATT, 0), row(D_REC, 0), row(D_REC, 0), row(D_REC, COL_MO // D_REC),
                  row(D_REC, 0), row(D_REC, 0), row(D_REC, COL_SZ // D_REC),
                  pl.BlockSpec((D_MODEL, D_MODEL), lambda i: (0, 0)),
                  row(D_MODEL, 0),
                  pl.BlockSpec((1, 1, D_MODEL), lambda i: (i // tpb, 0, 0)),
                  vec(D_REC), vec(D_REC), vec(D_MODEL), vec(D_MODEL)],
        out_specs=row(D_MODEL, 0),
        out_shape=jax.ShapeDtypeStruct((n, D_MODEL), F32),
        compiler_params=_params(("parallel",)),
        name="out_proj",
    )(att, hf, hb, u, yf, yb, u, w, x, gate, mnw.reshape(1, D_REC), snw.reshape(1, D_REC),
      lg.reshape(1, D_MODEL), lb.reshape(1, D_MODEL))


FF_TILE = D_FF // 2
FFN_ROWS = 1024
FFN_SUB = 512
FFN_VMEM_LIMIT = 56 * 1024 * 1024


def _swiglu_partial(h, w1, w3, w2):
    a = jnp.dot(h, w1, preferred_element_type=F32)
    b = jnp.dot(h, w3, preferred_element_type=F32)
    return jnp.dot((_silu(a) * b).astype(BF16), w2, preferred_element_type=F32)


def _ffn_kernel(x_ref, sc_ref, sh_ref, gate_ref, w1_ref, w3_ref, w2_ref, lg_ref, lb_ref, o_ref):
    for r0 in range(0, FFN_ROWS, FFN_SUB):
        rows = slice(r0, r0 + FFN_SUB)
        x = x_ref[rows, :]
        h = (x * (1.0 + sc_ref[0]) + sh_ref[0]).astype(BF16)
        y = ALPHA * x + gate_ref[0] * _swiglu_partial(h, w1_ref[...], w3_ref[...], w2_ref[...])
        o_ref[rows, :] = _layernorm_rows(y, lg_ref[...], lb_ref[...])


def _ffn(x, sc, sh, gate, w1, w3, w2, lg, lb, rows_per_mod):
    n = x.shape[0]
    tpb = rows_per_mod // FFN_ROWS
    modspec = pl.BlockSpec((1, 1, D_MODEL), lambda i: (i // tpb, 0, 0))
    vec = pl.BlockSpec((1, D_MODEL), lambda i: (0, 0))
    once = pl.Buffered(1)
    return pl.pallas_call(
        _ffn_kernel,
        grid=(n // FFN_ROWS,),
        in_specs=[pl.BlockSpec((FFN_ROWS, D_MODEL), lambda i: (i, 0)), modspec, modspec, modspec,
                  pl.BlockSpec((D_MODEL, D_FF), lambda i: (0, 0), pipeline_mode=once),
                  pl.BlockSpec((D_MODEL, D_FF), lambda i: (0, 0), pipeline_mode=once),
                  pl.BlockSpec((D_FF, D_MODEL), lambda i: (0, 0), pipeline_mode=once), vec, vec],
        out_specs=pl.BlockSpec((FFN_ROWS, D_MODEL), lambda i: (i, 0)),
        out_shape=jax.ShapeDtypeStruct((n, D_MODEL), F32),
        compiler_params=_params(("parallel",), FFN_VMEM_LIMIT),
        name="ffn_dense",
    )(x, sc, sh, gate, w1, w3, w2, lg.reshape(1, D_MODEL), lb.reshape(1, D_MODEL))


def _router_kernel(x_ref, sc_ref, sh_ref, rw_ref, gates_ref, h_ref):
    h = x_ref[...] * (1.0 + sc_ref[0]) + sh_ref[0]
    h_ref[...] = h.astype(BF16)
    logits = jnp.dot(h, rw_ref[...], precision=HIGHEST, preferred_element_type=F32)
    lane = lax.broadcasted_iota(jnp.int32, logits.shape, 1)
    valid = lane < N_EXPERTS
    p = jnp.where(valid, _softmax_rows(jnp.where(valid, logits, -jnp.inf)), -2.0)
    p1 = jnp.max(p, -1, keepdims=True)
    i1 = jnp.min(jnp.where(p == p1, lane, LANES), -1, keepdims=True)
    rest = jnp.where(lane == i1, -1.0, p)
    p2 = jnp.max(rest, -1, keepdims=True)
    i2 = jnp.min(jnp.where(rest == p2, lane, LANES), -1, keepdims=True)
    tot = p1 + p2
    gates_ref[...] = jnp.where(lane == i1, p1 / tot, jnp.where(lane == i2, p2 / tot, 0.0))


def _router(x, sc, sh, router_w, rows_per_mod):
    n = x.shape[0]
    tpb = rows_per_mod // ROW_TILE
    modspec = pl.BlockSpec((1, 1, D_MODEL), lambda i: (i // tpb, 0, 0))
    rw = jnp.pad(router_w, ((0, 0), (0, LANES - N_EXPERTS)))
    return pl.pallas_call(
        _router_kernel,
        grid=(n // ROW_TILE,),
        in_specs=[pl.BlockSpec((ROW_TILE, D_MODEL), lambda i: (i, 0)), modspec, modspec,
                  pl.BlockSpec((D_MODEL, LANES), lambda i: (0, 0))],
        out_specs=[pl.BlockSpec((ROW_TILE, LANES), lambda i: (i, 0)),
                   pl.BlockSpec((ROW_TILE, D_MODEL), lambda i: (i, 0))],
        out_shape=[jax.ShapeDtypeStruct((n, LANES), F32), jax.ShapeDtypeStruct((n, D_MODEL), BF16)],
        compiler_params=_params(("parallel",)),
        name="router",
    )(x, sc, sh, rw)


MOE_ROWS = 1024
MOE_TILE = 128
MOE_MAX_TILES = MOE_ROWS // MOE_TILE
MOE_VMEM_LIMIT = 56 * 1024 * 1024


def _moe_kernel(h_ref, gates_ref, w1_ref, w3_ref, w2_ref, x_ref, gate_ref, lg_ref, lb_ref, o_ref,
                slot_scr, slott_scr, hs_scr, ys_scr):
    e = pl.program_id(1)
    j = pl.program_id(2)
    last_j = pl.num_programs(2) - 1

    @pl.when(jnp.logical_and(e == 0, j == 0))
    def _():
        r = lax.broadcasted_iota(jnp.int32, (MOE_ROWS, MOE_ROWS), 0)
        c = lax.broadcasted_iota(jnp.int32, (MOE_ROWS, MOE_ROWS), 1)
        before = jnp.where(c < r, 1.0, 0.0).astype(BF16)
        mask = gates_ref[...] != 0.0
        rank = jnp.dot(before, jnp.where(mask, 1.0, 0.0).astype(BF16), preferred_element_type=F32)
        slot = jnp.where(mask, rank, -1.0).astype(jnp.int32)
        slot_scr[...] = slot
        slott_scr[...] = slot.T
        o_ref[...] = jnp.zeros_like(o_ref)

    lane = lax.broadcasted_iota(jnp.int32, (MOE_ROWS, LANES), 1)
    slot_col = jnp.max(jnp.where(lane == e, slot_scr[...], -1), -1, keepdims=True)
    n_tiles = (jnp.max(slot_col) + MOE_TILE) // MOE_TILE

    def tile(k, carry):
        @pl.when(j == 0)
        def _():
            slot_row = slott_scr[pl.ds(e, 1), :]
            rr = lax.broadcasted_iota(jnp.int32, (MOE_TILE, MOE_ROWS), 0) + k * MOE_TILE
            pick = jnp.where(rr == slot_row, 1.0, 0.0).astype(BF16)
            hs_scr[k] = jnp.dot(pick, h_ref[...], preferred_element_type=F32).astype(BF16)
            ys_scr[k] = jnp.zeros((MOE_TILE, D_MODEL), F32)

        ys_scr[k] += _swiglu_partial(hs_scr[k], w1_ref[0], w3_ref[0], w2_ref[0])

        @pl.when(j == last_j)
        def _():
            g_col = jnp.sum(jnp.where(lane == e, gates_ref[...], 0.0), -1, keepdims=True)
            cc = lax.broadcasted_iota(jnp.int32, (MOE_ROWS, 2 * MOE_TILE), 1)
            cc = jnp.where(cc >= MOE_TILE, cc - MOE_TILE, cc) + k * MOE_TILE
            put = jnp.where(slot_col == cc, 1.0, 0.0).astype(BF16)
            y = ys_scr[k]
            y_hi = y.astype(BF16)
            y_lo = (y - y_hi.astype(F32)).astype(BF16)
            back = jnp.dot(put, jnp.concatenate([y_hi, y_lo], axis=0), preferred_element_type=F32)
            o_ref[...] += g_col * back

        return carry

    lax.fori_loop(0, n_tiles, tile, 0)

    @pl.when(jnp.logical_and(e == pl.num_programs(1) - 1, j == last_j))
    def _():
        y = ALPHA * x_ref[...] + gate_ref[0] * o_ref[...]
        o_ref[...] = _layernorm_rows(y, lg_ref[...], lb_ref[...])


def _moe(h, gates, w1, w3, w2, x, gate, lg, lb, rows_per_mod):
    n = h.shape[0]
    tpb = rows_per_mod // MOE_ROWS
    vec = pl.BlockSpec((1, D_MODEL), lambda i, e, j: (0, 0))
    return pl.pallas_call(
        _moe_kernel,
        grid=(n // MOE_ROWS, N_EXPERTS, D_FF // FF_TILE),
        in_specs=[pl.BlockSpec((MOE_ROWS, D_MODEL), lambda i, e, j: (i, 0)),
                  pl.BlockSpec((MOE_ROWS, LANES), lambda i, e, j: (i, 0)),
                  pl.BlockSpec((1, D_MODEL, FF_TILE), lambda i, e, j: (e, 0, j)),
                  pl.BlockSpec((1, D_MODEL, FF_TILE), lambda i, e, j: (e, 0, j)),
                  pl.BlockSpec((1, FF_TILE, D_MODEL), lambda i, e, j: (e, j, 0)),
                  pl.BlockSpec((MOE_ROWS, D_MODEL), lambda i, e, j: (i, 0)),
                  pl.BlockSpec((1, 1, D_MODEL), lambda i, e, j: (i // tpb, 0, 0)), vec, vec],
        out_specs=pl.BlockSpec((MOE_ROWS, D_MODEL), lambda i, e, j: (i, 0)),
        out_shape=jax.ShapeDtypeStruct((n, D_MODEL), F32),
        scratch_shapes=[pltpu.VMEM((MOE_ROWS, LANES), jnp.int32), pltpu.VMEM((LANES, MOE_ROWS), jnp.int32),
                        pltpu.VMEM((MOE_MAX_TILES, MOE_TILE, D_MODEL), BF16),
                        pltpu.VMEM((MOE_MAX_TILES, MOE_TILE, D_MODEL), F32)],
        compiler_params=_params(("parallel", "arbitrary", "arbitrary"), MOE_VMEM_LIMIT),
        name="moe",
    )(h, gates, w1, w3, w2, x, gate, lg.reshape(1, D_MODEL), lb.reshape(1, D_MODEL))


def _permute_w_in(w):
    pad = jnp.zeros((D_MODEL, U_COLS - ORIG_END), w.dtype)
    return jnp.concatenate([w[:, :ORIG_GATES], w[:, ORIG_SX:ORIG_DT], w[:, ORIG_SZ:ORIG_SX],
                            w[:, ORIG_GATES:ORIG_SZ], w[:, ORIG_DT:ORIG_END], pad], axis=1).astype(BF16)


def _small_row(vals, offset):
    v = vals.reshape(-1).astype(F32)
    return jnp.zeros((1, LANES), F32).at[0, offset:offset + v.shape[0]].set(v)


def _pack_mlstm_state(c, n, m):
    shape = c.shape[:2] + (D_STATE, HEAD_LANES)
    c_rows = jnp.swapaxes(c, 2, 3).reshape(shape)
    n_rows = jnp.broadcast_to(jnp.swapaxes(n, 2, 3)[..., None], c.shape[:2] + (D_STATE, N_HEADS, D_STATE))
    return (jnp.concatenate([c_rows, n_rows.reshape(shape)], axis=-1),
            jnp.repeat(m, D_STATE, axis=-1)[:, :, None, :])


def _pack_ssd_state(s):
    sel = (jnp.arange(N_GROUPS)[:, None] == jnp.arange(N_HEADS)[None, :] // 2).astype(F32)
    return jnp.einsum('bdhpn,gh->bdgnhp', s, sel).reshape(s.shape[:2] + (LANES, HEAD_LANES))


def _layer(x, mods, P, l, bsz, t, ctx, caches=None):
    sh1, sc1, g1, sh2, sc2, g2 = mods
    rows_per_mod = x.shape[0] // sh1.shape[0]
    lam_init = 0.8 - 0.6 * math.exp(-0.3 * l)
    u = _in_proj(x, sc1, sh1, P['w_in'][l], P['conv_w'][l], P['conv_b'][l], rows_per_mod, t)

    gate_bias = (_small_row(P['mlstm_gate_b'][l, 0], GATE_I) + _small_row(P['mlstm_gate_b'][l, 1], GATE_F))
    dt_bias = _small_row(P['ssm_dt_bias'][l], GATE_DT)
    alog = _small_row(P['ssm_A_log'][l], GATE_DT)
    dskip = jnp.repeat(P['ssm_D'][l].astype(F32), D_STATE).reshape(1, D_REC)

    if ctx is None:
        att, k_new, v_new = _attention_ctx(u, P['attn_lambda'][l], P['attn_norm_w'][l], lam_init, bsz, t, l,
                                           None if caches is None else caches[0:2])
        m_init = s_init = None
    else:
        ck, cv, c_c, c_n, c_m, c_s = ctx
        q, k, v = _rope_prep(u, bsz, t)
        k_all = jnp.concatenate([k, ck.astype(BF16)], axis=3)
        v_all = jnp.concatenate([v, cv.astype(BF16)], axis=2)
        att = _attention_lat(q, k_all, v_all, P['attn_lambda'][l], P['attn_norm_w'][l], lam_init, bsz, t)
        m_init = _pack_mlstm_state(c_c, c_n, c_m)
        s_init = _pack_ssd_state(c_s)
    hf, hb, *mlstm_caches = _mlstm(u, gate_bias, bsz, t, m_init, l, None if caches is None else caches[2:5])
    yf, yb, *ssd_caches = _ssd(u, dt_bias, alog, dskip, bsz, t, s_init, l,
                               None if caches is None else caches[5])

    x = _out_proj(att, hf, hb, yf, yb, u, P['w_out'][l], x, g1, P['mlstm_norm_w'][l], P['ssm_norm_w'][l],
                  P['ln_g'][l, 0], P['ln_b'][l, 0], rows_per_mod)
    if l % 2 == 0:
        x = _ffn(x, sc2, sh2, g2, P['ffn_w1'][l // 2], P['ffn_w3'][l // 2], P['ffn_w2'][l // 2],
                 P['ln_g'][l, 1], P['ln_b'][l, 1], rows_per_mod)
    else:
        gates, h2 = _router(x, sc2, sh2, P['router_w'][l // 2], rows_per_mod)
        x = _moe(h2, gates, P['moe_w1'][l // 2], P['moe_w3'][l // 2], P['moe_w2'][l // 2],
                 x, g2, P['ln_g'][l, 1], P['ln_b'][l, 1], rows_per_mod)
    if ctx is None:
        return x, (k_new, v_new, *mlstm_caches, *ssd_caches)
    return x, None


def kernel(x_prompt, x_sample, c, cache_attn_k, cache_attn_v, state_mlstm_C, state_mlstm_n, state_mlstm_m, state_ssm, c_ctx, w_ada, b_ada, w_in, w_out, attn_lambda, attn_norm_w, mlstm_gate_b, mlstm_norm_w, conv_w, conv_b, ssm_A_log, ssm_dt_bias, ssm_D, ssm_norm_w, ln_g, ln_b, ffn_w1, ffn_w3, ffn_w2, router_w, moe_w1, moe_w3, moe_w2):
    bsz, seq, _ = x_prompt.shape
    dbsz, dseq, _ = x_sample.shape
    P = dict(w_in=[_permute_w_in(w_in[l]) for l in range(DEPTH)], w_out=w_out.astype(BF16),
             attn_lambda=attn_lambda, attn_norm_w=attn_norm_w, mlstm_gate_b=mlstm_gate_b,
             mlstm_norm_w=mlstm_norm_w, conv_w=conv_w, conv_b=conv_b, ssm_A_log=ssm_A_log,
             ssm_dt_bias=ssm_dt_bias, ssm_D=ssm_D, ssm_norm_w=ssm_norm_w, ln_g=ln_g, ln_b=ln_b,
             ffn_w1=ffn_w1.astype(BF16), ffn_w3=ffn_w3.astype(BF16), ffn_w2=ffn_w2.astype(BF16),
             router_w=router_w, moe_w1=moe_w1.astype(BF16), moe_w3=moe_w3.astype(BF16),
             moe_w2=moe_w2.astype(BF16))

    cvec = jnp.zeros((8, D_MODEL), F32).at[0].set(c_ctx).at[1:1 + dbsz].set(c)
    mod = _modulation(cvec, w_ada, b_ada)

    def mods_for(l, lo, hi):
        return [mod[l, lo:hi, i * D_MODEL:(i + 1) * D_MODEL][:, None, :] for i in range(6)]

    y_prompt = x_prompt.reshape(bsz * seq, D_MODEL)
    caches = None
    for l in range(DEPTH):
        y_prompt, caches = _layer(y_prompt, mods_for(l, 0, 1), P, l, bsz, seq, None, caches)
    new_k, new_v, new_c, new_n, m_spread, new_s = caches

    y_sample = x_sample.reshape(dbsz * dseq, D_MODEL)
    for l in range(DEPTH):
        ctx = (cache_attn_k[:, l], cache_attn_v[:, l], state_mlstm_C[:, l], state_mlstm_n[:, l],
               state_mlstm_m[:, l], state_ssm[:, l])
        y_sample, _ = _layer(y_sample, mods_for(l, 1, 1 + dbsz), P, l, dbsz, dseq, ctx)

    return (y_prompt.reshape(bsz, seq, D_MODEL), y_sample.reshape(dbsz, dseq, D_MODEL),
            new_k, new_v, new_c, new_n, m_spread[:, :, :, 0, ::D_STATE], new_s)
```

```python
import functools
import math

import jax
import jax.numpy as jnp
from jax import lax
from jax.experimental import pallas as pl
from jax.experimental.pallas import tpu as pltpu

F32 = jnp.float32
BF16 = jnp.bfloat16
HIGHEST = lax.Precision.HIGHEST

D_MODEL = 1024
DEPTH = 2
GRID_W = 64
N_HEADS = 4
D_ATT = 512
D_HEAD_V = 128
D_QK = 64
D_REC = 256
D_STATE = 64
N_GROUPS = 2
D_CONV = 3
D_FF = 2816
N_EXPERTS = 8
ALPHA = (2.0 * DEPTH) ** 0.25
CHUNK = 64
ROPE_BASE = 10000.0
LOG2E = 1.4426950408889634
EPS = 1e-5

COL_AQ, COL_AK, COL_AV = 0, 512, 1024
COL_MQ, COL_MK, COL_MV, COL_MO = 1536, 1792, 2048, 2304
COL_SX, COL_SBC, COL_SZ = 2560, 2816, 3072
COL_SMALL = 3328
U_COLS = 3584
ORIG_GATES, ORIG_SZ, ORIG_SX, ORIG_DT, ORIG_END = 2560, 2576, 2832, 3344, 3352
GATE_I, GATE_F, GATE_DT = 0, 8, 16

LANES = 128
ROW_TILE = 512
SCAN_ROWS = 512
SCAN_BATCH = 2
VMEM_LIMIT = 48 * 1024 * 1024

NT_DIMS = (((1,), (1,)), ((), ()))
TN_DIMS = (((0,), (0,)), ((), ()))


def _params(sem, vmem=VMEM_LIMIT):
    return pltpu.CompilerParams(dimension_semantics=sem, vmem_limit_bytes=vmem)


def _silu(x):
    return x * jax.nn.sigmoid(x)


def _bdot(a, b):
    return jnp.dot(a.astype(BF16), b.astype(BF16), preferred_element_type=F32)


def _bdot_nt(a, b):
    return lax.dot_general(a.astype(BF16), b.astype(BF16), NT_DIMS, preferred_element_type=F32)


def _layernorm_rows(y, g, b):
    mu = jnp.mean(y, -1, keepdims=True)
    d = y - mu
    var = jnp.mean(d * d, -1, keepdims=True)
    return d * lax.rsqrt(var + EPS) * g + b


def _mod_kernel(c_ref, w_ref, b_ref, o_ref):
    o_ref[0] = jnp.dot(_silu(c_ref[...]), w_ref[0], precision=HIGHEST,
                       preferred_element_type=F32) + b_ref[0]


def _modulation(cvec, w_ada, b_ada):
    tn = 1536
    return pl.pallas_call(
        _mod_kernel,
        grid=(DEPTH, 6 * D_MODEL // tn),
        in_specs=[pl.BlockSpec((8, D_MODEL), lambda l, j: (0, 0)),
                  pl.BlockSpec((1, D_MODEL, tn), lambda l, j: (l, 0, j)),
                  pl.BlockSpec((1, 1, tn), lambda l, j: (l, 0, j))],
        out_specs=pl.BlockSpec((1, 8, tn), lambda l, j: (l, 0, j)),
        out_shape=jax.ShapeDtypeStruct((DEPTH, 8, 6 * D_MODEL), F32),
        compiler_params=_params(("parallel", "parallel")),
        name="modulation",
    )(cvec, w_ada, b_ada.reshape(DEPTH, 1, 6 * D_MODEL))


CONV_COLS = 2 * D_REC
HALO = 16


def _inproj_kernel(x_ref, prev_ref, next_ref, sc_ref, sh_ref, w_ref, cw_ref, cb_ref, o_ref, *, seq):
    def modulate(x):
        return (x * (1.0 + sc_ref[0]) + sh_ref[0]).astype(BF16)

    h = modulate(x_ref[...])
    for n0 in range(0, U_COLS, CONV_COLS):
        if n0 != COL_SX:
            o_ref[:, n0:n0 + CONV_COLS] = jnp.dot(h, w_ref[:, n0:n0 + CONV_COLS], preferred_element_type=F32)
            continue
        h_ext = jnp.concatenate([h, modulate(prev_ref[...]), modulate(next_ref[...])], axis=0)
        pre_ext = jnp.dot(h_ext, w_ref[:, n0:n0 + CONV_COLS], preferred_element_type=F32)
        pre = pre_ext[:ROW_TILE]
        before = pre_ext[ROW_TILE + HALO - 1:ROW_TILE + HALO]
        after = pre_ext[ROW_TILE + HALO:ROW_TILE + HALO + 1]
        r = lax.broadcasted_iota(jnp.int32, (ROW_TILE, 1), 0)
        pos = (r + pl.program_id(0) * ROW_TILE) % seq
        up = jnp.where(r == 0, before, pltpu.roll(pre, 1, 0))
        down = jnp.where(r == ROW_TILE - 1, after, pltpu.roll(pre, ROW_TILE - 1, 0))
        up = jnp.where(pos == 0, 0.0, up)
        down = jnp.where(pos == seq - 1, 0.0, down)
        cw = cw_ref[...]
        o_ref[:, n0:n0 + CONV_COLS] = _silu(up * cw[0:1] + pre * cw[1:2] + down * cw[2:3] + cb_ref[...])


def _in_proj(x, sc, sh, w, conv_w, conv_b, rows_per_mod, seq):
    n = x.shape[0]
    tpb = rows_per_mod // ROW_TILE
    r8 = ROW_TILE // HALO
    last8 = n // HALO - 1
    assert COL_SX % CONV_COLS == 0 and U_COLS % CONV_COLS == 0
    return pl.pallas_call(
        functools.partial(_inproj_kernel, seq=seq),
        grid=(n // ROW_TILE,),
        in_specs=[pl.BlockSpec((ROW_TILE, D_MODEL), lambda i: (i, 0)),
                  pl.BlockSpec((HALO, D_MODEL), lambda i: (jnp.maximum(i * r8 - 1, 0), 0)),
                  pl.BlockSpec((HALO, D_MODEL), lambda i: (jnp.minimum((i + 1) * r8, last8), 0)),
                  pl.BlockSpec((1, 1, D_MODEL), lambda i: (i // tpb, 0, 0)),
                  pl.BlockSpec((1, 1, D_MODEL), lambda i: (i // tpb, 0, 0)),
                  pl.BlockSpec((D_MODEL, U_COLS), lambda i: (0, 0)),
                  pl.BlockSpec((D_CONV, CONV_COLS), lambda i: (0, 0)),
                  pl.BlockSpec((1, CONV_COLS), lambda i: (0, 0))],
        out_specs=pl.BlockSpec((ROW_TILE, U_COLS), lambda i: (i, 0)),
        out_shape=jax.ShapeDtypeStruct((n, U_COLS), F32),
        compiler_params=_params(("parallel",)),
        name="in_proj",
    )(x, x, x, sc, sh, w, conv_w, conv_b.reshape(1, CONV_COLS))


def _lambda_scalar(lam_ref, lam_init):
    lp = lam_ref[...]
    s01 = jnp.sum(lp[0:1] * lp[1:2], axis=-1, keepdims=True)
    s23 = jnp.sum(lp[2:3] * lp[3:4], axis=-1, keepdims=True)
    return jnp.exp(s01) - jnp.exp(s23) + lam_init


def _softmax_rows(s):
    e = jnp.exp(s - jnp.max(s, -1, keepdims=True))
    return e / jnp.sum(e, -1, keepdims=True)


def _head_norm(o, nw, lam_init):
    return o * lax.rsqrt(jnp.mean(o * o, -1, keepdims=True) + EPS) * nw * (1.0 - lam_init)


def _layer_cache(tail, l, bsz, nb=1):
    zeros = (0,) * len(tail)
    owned = DEPTH if l == 0 else 1
    spec = pl.BlockSpec((nb, owned) + tuple(tail), lambda b, *_: (b, l) + zeros)
    return jax.ShapeDtypeStruct((bsz, DEPTH) + tuple(tail), F32), spec, owned


def _zero_later_layers(ref):
    for b in range(ref.shape[0]):
        for later in range(1, ref.shape[1]):
            ref[b, later] = jnp.zeros(ref.shape[2:], ref.dtype)


def _attn_ctx_kernel(u_ref, lam_ref, nw_ref, *rest, lam_init):
    att_ref, k_ref, v_ref = rest[-3:]
    lam = _lambda_scalar(lam_ref, lam_init)
    for h in range(N_HEADS):
        v = u_ref[:, COL_AV + h * D_HEAD_V:COL_AV + (h + 1) * D_HEAD_V]
        v_ref[0, 0, h] = v
        ps = []
        for m in range(2):
            c0 = h * D_HEAD_V + m * D_QK
            q = u_ref[:, COL_AQ + c0:COL_AQ + c0 + D_QK] * (D_QK ** -0.5)
            k = u_ref[:, COL_AK + c0:COL_AK + c0 + D_QK]
            k_ref[0, 0, h, m] = k
            ps.append(_softmax_rows(_bdot_nt(q, k)))
        o = _bdot(ps[0] - lam * ps[1], v)
        att_ref[:, h * D_HEAD_V:(h + 1) * D_HEAD_V] = _head_norm(o, nw_ref[...], lam_init).astype(BF16)
    _zero_later_layers(k_ref)
    _zero_later_layers(v_ref)


def _attention_ctx(u, lam_p, norm_w, lam_init, bsz, t, l, caches):
    n = bsz * t
    k_shape, k_spec, _ = _layer_cache((N_HEADS, 2, t, D_QK), l, bsz)
    v_shape, v_spec, _ = _layer_cache((N_HEADS, t, D_HEAD_V), l, bsz)
    in_specs = [pl.BlockSpec((t, 3 * D_ATT), lambda b: (b, 0)),
                pl.BlockSpec((4, D_QK), lambda b: (0, 0)),
                pl.BlockSpec((1, D_HEAD_V), lambda b: (0, 0))]
    args = [u, lam_p, norm_w.reshape(1, D_HEAD_V)]
    aliases = {}
    if caches is not None:
        in_specs += [pl.BlockSpec(memory_space=pl.ANY)] * 2
        aliases = {len(args): 1, len(args) + 1: 2}
        args += list(caches)
    return pl.pallas_call(
        functools.partial(_attn_ctx_kernel, lam_init=lam_init),
        grid=(bsz,),
        in_specs=in_specs,
        out_specs=[pl.BlockSpec((t, D_ATT), lambda b: (b, 0)), k_spec, v_spec],
        out_shape=[jax.ShapeDtypeStruct((n, D_ATT), BF16), k_shape, v_shape],
        input_output_aliases=aliases,
        compiler_params=_params(("parallel",)),
        name="attn_ctx",
    )(*args)


def _rope_kernel(u_ref, cos_ref, sa_ref, sb_ref, q_ref, k_ref, v_ref):
    cos, sa, sb = cos_ref[...], sa_ref[...], sb_ref[...]

    def rope(x):
        return x * cos + pltpu.roll(x, LANES - 16, 1) * sa + pltpu.roll(x, 16, 1) * sb

    for h in range(N_HEADS):
        q = rope(u_ref[:, COL_AQ + h * D_HEAD_V:COL_AQ + (h + 1) * D_HEAD_V]) * (LOG2E * D_QK ** -0.5)
        k = rope(u_ref[:, COL_AK + h * D_HEAD_V:COL_AK + (h + 1) * D_HEAD_V])
        for m in range(2):
            q_ref[0, h, m] = q[:, m * D_QK:(m + 1) * D_QK].astype(BF16)
            k_ref[0, h, m] = k[:, m * D_QK:(m + 1) * D_QK].astype(BF16)
        v_ref[0, h] = u_ref[:, COL_AV + h * D_HEAD_V:COL_AV + (h + 1) * D_HEAD_V].astype(BF16)


def _rope_tables(t):
    rows = jnp.repeat(jnp.arange(t // GRID_W, dtype=F32), GRID_W)
    cols = jnp.tile(jnp.arange(GRID_W, dtype=F32), t // GRID_W)
    half = D_QK // 2
    inv = ROPE_BASE ** (-jnp.arange(0, half, 2, dtype=F32) / half)
    ang_r = rows[:, None] * inv
    ang_c = cols[:, None] * inv
    ang = jnp.concatenate([ang_r, ang_r, ang_c, ang_c], -1)
    cos, sin = jnp.cos(ang), jnp.sin(ang)
    quarter = (jnp.arange(D_QK) // (D_QK // 4)) % 2
    sa = jnp.where(quarter == 0, -sin, 0.0)
    sb = jnp.where(quarter == 1, sin, 0.0)
    tile2 = lambda a: jnp.concatenate([a, a], -1)
    return tile2(cos), tile2(sa), tile2(sb)


def _rope_prep(u, bsz, t):
    tr = 512
    nb = t // tr
    cos, sa, sb = _rope_tables(t)
    tab = pl.BlockSpec((tr, LANES), lambda b, i: (i, 0))
    return pl.pallas_call(
        _rope_kernel,
        grid=(bsz, nb),
        in_specs=[pl.BlockSpec((tr, 3 * D_ATT), lambda b, i: (b * nb + i, 0)), tab, tab, tab],
        out_specs=[pl.BlockSpec((1, N_HEADS, 2, tr, D_QK), lambda b, i: (b, 0, 0, i, 0)),
                   pl.BlockSpec((1, N_HEADS, 2, tr, D_QK), lambda b, i: (b, 0, 0, i, 0)),
                   pl.BlockSpec((1, N_HEADS, tr, D_HEAD_V), lambda b, i: (b, 0, i, 0))],
        out_shape=[jax.ShapeDtypeStruct((bsz, N_HEADS, 2, t, D_QK), BF16),
                   jax.ShapeDtypeStruct((bsz, N_HEADS, 2, t, D_QK), BF16),
                   jax.ShapeDtypeStruct((bsz, N_HEADS, t, D_HEAD_V), BF16)],
        compiler_params=_params(("parallel", "parallel")),
        name="rope_prep",
    )(u, cos, sa, sb)


def _attn_lat_kernel(q_ref, k_ref, v_ref, lam_ref, nw_ref, o_ref, *, lam_init):
    lam = _lambda_scalar(lam_ref, lam_init)
    es, sums = [], []
    for m in range(2):
        s = lax.dot_general(q_ref[0, 0, m], k_ref[0, 0, m], NT_DIMS, preferred_element_type=F32)
        e = jnp.exp2(s - jnp.max(s, -1, keepdims=True))
        es.append(e)
        sums.append(jnp.sum(e, -1, keepdims=True))
    a = es[0] - (lam * sums[0] / sums[1]) * es[1]
    o = _bdot(a, v_ref[0, 0]) / sums[0]
    o_ref[...] = _head_norm(o, nw_ref[...], lam_init).astype(BF16)


def _attention_lat(q, k_all, v_all, lam_p, norm_w, lam_init, bsz, t):
    tq = 256
    nq = t // tq
    s = k_all.shape[3]
    return pl.pallas_call(
        functools.partial(_attn_lat_kernel, lam_init=lam_init),
        grid=(bsz, N_HEADS, nq),
        in_specs=[pl.BlockSpec((1, 1, 2, tq, D_QK), lambda b, h, i: (b, h, 0, i, 0)),
                  pl.BlockSpec((1, 1, 2, s, D_QK), lambda b, h, i: (b, h, 0, 0, 0)),
                  pl.BlockSpec((1, 1, s, D_HEAD_V), lambda b, h, i: (b, h, 0, 0)),
                  pl.BlockSpec((4, D_QK), lambda b, h, i: (0, 0)),
                  pl.BlockSpec((1, D_HEAD_V), lambda b, h, i: (0, 0))],
        out_specs=pl.BlockSpec((tq, D_HEAD_V), lambda b, h, i: (b * nq + i, h)),
        out_shape=jax.ShapeDtypeStruct((bsz * t, D_ATT), BF16),
        compiler_params=_params(("parallel", "parallel", "parallel")),
        name="attn_lat",
    )(q, k_all, v_all, lam_p, norm_w.reshape(1, D_HEAD_V))


HEAD_LANES = N_HEADS * D_STATE


def _scan_consts():
    t = lax.broadcasted_iota(jnp.int32, (CHUNK, HEAD_LANES), 0)
    s = lax.broadcasted_iota(jnp.int32, (CHUNK, HEAD_LANES), 1) & (CHUNK - 1)
    r = lax.broadcasted_iota(jnp.int32, (CHUNK, CHUNK), 0)
    c = lax.broadcasted_iota(jnp.int32, (CHUNK, CHUNK), 1)
    reads = (jnp.where(s <= t, 1.0, 0.0), jnp.where(s >= t, 1.0, 0.0))
    block = (jnp.where(s <= t, 0.0, -jnp.inf), jnp.where(s >= t, 0.0, -jnp.inf))
    tri = (jnp.where(c <= r, 1.0, 0.0).astype(BF16), jnp.where(c >= r, 1.0, 0.0).astype(BF16))
    return reads, block, tri, jnp.where(s == t, 1.0, 0.0)


def _cumsum_rows(tri, x):
    hi = x.astype(BF16)
    rest = x - hi.astype(F32)
    mid = rest.astype(BF16)
    lo = (rest - mid.astype(F32)).astype(BF16)
    parts = jnp.dot(tri, jnp.concatenate([hi, mid, lo], axis=1), preferred_element_type=F32)
    return parts[:, :HEAD_LANES] + parts[:, HEAD_LANES:2 * HEAD_LANES] + parts[:, 2 * HEAD_LANES:]


def _group_mask(rows, cols, row_shift, col_shift, dtype):
    r = lax.broadcasted_iota(jnp.int32, (rows, cols), 0) >> 6
    c = (lax.broadcasted_iota(jnp.int32, (rows, cols), 1) >> 6) & (N_HEADS - 1)
    return jnp.where((r >> row_shift) == (c >> col_shift), 1.0, 0.0).astype(dtype)


def _spread(x, chans):
    return jnp.concatenate([jnp.broadcast_to(x[:, c:c + 1], (CHUNK, D_STATE)) for c in chans], axis=1)


def _stack_heads(x):
    return jnp.concatenate([x] * N_HEADS, axis=0)


def _mlstm_chunk(q4, k4, v4, g, d, cn_prev, m_prev, reads, block, tri, eye, bd, bd_f32):
    li = _spread(g, [GATE_I + d * N_HEADS + h for h in range(N_HEADS)])
    lf = _spread(jax.nn.log_sigmoid(g), [GATE_F + d * N_HEADS + h for h in range(N_HEADS)])
    bc = _cumsum_rows(tri[d], lf)
    btot = jnp.sum(lf, 0, keepdims=True)
    b_row = jnp.sum(reads[1 - d] * lf, 0, keepdims=True)
    li_row = jnp.sum(eye * li, 0, keepdims=True)
    dm = bc - b_row + li_row + block[d]
    rmax = jnp.concatenate(
        [jnp.broadcast_to(jnp.max(dm[:, h * D_STATE:(h + 1) * D_STATE], -1, keepdims=True), (CHUNK, D_STATE))
         for h in range(N_HEADS)], axis=1)
    inter = bc + m_prev
    m_t = jnp.maximum(inter, rmax)
    w_inter = jnp.exp(inter - m_t)
    qs = (q4 * (D_STATE ** -0.5)).astype(BF16)
    kbd = bd[:, :HEAD_LANES] * _stack_heads(k4.astype(BF16))
    s4 = lax.dot_general(qs, kbd, NT_DIMS, preferred_element_type=F32) * jnp.exp(dm - m_t)
    vo = jnp.concatenate([v4.astype(BF16), jnp.ones((CHUNK, HEAD_LANES), BF16)], axis=1)
    vbd = bd * _stack_heads(vo)
    nd = (jnp.concatenate([w_inter, w_inter], axis=1)
          * jnp.dot(qs, bd * _stack_heads(cn_prev.astype(BF16)), preferred_element_type=F32)
          + jnp.dot(s4.astype(BF16), vbd, preferred_element_type=F32))
    hc = nd[:, :HEAD_LANES] / jnp.maximum(jnp.abs(nd[:, HEAD_LANES:]), jnp.exp(-m_t))
    gcol = btot - bc + li
    m_new = jnp.maximum(btot + m_prev, jnp.max(gcol, 0, keepdims=True))
    w_c = jnp.exp(btot + m_prev - m_new)
    kw = (k4 * jnp.exp(gcol - m_new)).astype(BF16)
    dcn = lax.dot_general(kw, vo, TN_DIMS, preferred_element_type=F32)
    own = sum(bd_f32[h * D_STATE:(h + 1) * D_STATE] * dcn[h * D_STATE:(h + 1) * D_STATE]
              for h in range(N_HEADS))
    cn_new = jnp.concatenate([w_c, w_c], axis=1) * cn_prev + own
    return hc, cn_new, m_new


def _mlstm_kernel(*refs, nblk, rows, zero_init, n_alias):
    qf_ref, kf_ref, vf_ref, gf_ref, qb_ref, kb_ref, vb_ref, gb_ref, bias_ref = refs[:9]
    if not zero_init:
        c0_ref, m0_ref = refs[9:11]
    outs = refs[9 + (0 if zero_init else 2) + n_alias:-2]
    hf_ref, hb_ref = outs[:2]
    cn_scr, m_scr = refs[-2:]
    j = pl.program_id(1)
    nchunk = rows // CHUNK

    @pl.when(j == 0)
    def _():
        if zero_init:
            cn_scr[...] = jnp.zeros_like(cn_scr)
            m_scr[...] = jnp.zeros_like(m_scr)
        else:
            cn_scr[...] = c0_ref[...]
            m_scr[...] = m0_ref[...]

    reads, block, tri, eye = _scan_consts()
    bd = _group_mask(HEAD_LANES, 2 * HEAD_LANES, 0, 0, BF16)
    bd_f32 = _group_mask(HEAD_LANES, 2 * HEAD_LANES, 0, 0, F32)

    chains = [(b, d) for b in range(SCAN_BATCH) for d in range(2)]
    state = {bd_: (cn_scr[bd_], m_scr[bd_]) for bd_ in chains}
    for ci in range(nchunk):
        for b, d in chains:
            q_ref, k_ref, v_ref, g_ref, h_ref = ((qf_ref, kf_ref, vf_ref, gf_ref, hf_ref) if d == 0
                                                 else (qb_ref, kb_ref, vb_ref, gb_ref, hb_ref))
            cj = ci if d == 0 else nchunk - 1 - ci
            rs = slice(cj * CHUNK, (cj + 1) * CHUNK)
            g = g_ref[b, rs, :] + bias_ref[...]
            hc, cn_new, m_new = _mlstm_chunk(q_ref[b, rs, :], k_ref[b, rs, :], v_ref[b, rs, :], g, d,
                                             *state[b, d], reads, block, tri, eye, bd, bd_f32)
            h_ref[b, rs, :] = hc
            state[b, d] = (cn_new, m_new)
    for bd_ in chains:
        cn_scr[bd_], m_scr[bd_] = state[bd_]

    if zero_init:
        cout_ref, nout_ref, mout_ref = outs[2:]

        @pl.when(j == nblk - 1)
        def _():
            r = lax.broadcasted_iota(jnp.int32, (D_STATE, D_STATE), 0)
            c = lax.broadcasted_iota(jnp.int32, (D_STATE, D_STATE), 1)
            for b, d in chains:
                for h in range(N_HEADS):
                    r0, r1 = h * D_STATE, (h + 1) * D_STATE
                    cout_ref[b, 0, d, h] = cn_scr[b, d, :, r0:r1]
                    n_spread = cn_scr[b, d, :, HEAD_LANES + r0:HEAD_LANES + r1]
                    nout_ref[b, 0, d, h:h + 1, :] = jnp.sum(jnp.where(r == c, n_spread, 0.0), 0, keepdims=True)
                mout_ref[b, 0, d] = m_scr[b, d]
            _zero_later_layers(cout_ref)
            _zero_later_layers(nout_ref)
            _zero_later_layers(mout_ref)


def _mlstm(u, gate_bias, bsz, t, init, l=0, caches=None):
    rows = min(t, SCAN_ROWS)
    nblk = t // rows
    zero_init = init is None

    nb = SCAN_BATCH
    fwd = lambda col: (lambda p, j: (p, j, col))
    bwd = lambda col: (lambda p, j: (p, nblk - 1 - j, col))
    cq, ck, cv, cg = COL_MQ // D_REC, COL_MK // D_REC, COL_MV // D_REC, COL_SMALL // LANES
    in_specs = []
    for mk in (fwd, bwd):
        in_specs += [pl.BlockSpec((nb, rows, D_REC), mk(cq)), pl.BlockSpec((nb, rows, D_REC), mk(ck)),
                     pl.BlockSpec((nb, rows, D_REC), mk(cv)), pl.BlockSpec((nb, rows, LANES), mk(cg))]
    in_specs.append(pl.BlockSpec((1, LANES), lambda p, j: (0, 0)))
    args = [u.reshape(bsz, t, U_COLS)] * 8 + [gate_bias]
    state_c = pl.BlockSpec((nb, 2, D_STATE, 2 * HEAD_LANES), lambda p, j: (p, 0, 0, 0))
    state_m = pl.BlockSpec((nb, 2, 1, HEAD_LANES), lambda p, j: (p, 0, 0, 0))
    out_specs = [pl.BlockSpec((nb, rows, D_REC), fwd(0)), pl.BlockSpec((nb, rows, D_REC), bwd(0))]
    out_shape = [jax.ShapeDtypeStruct((bsz, t, D_REC), F32), jax.ShapeDtypeStruct((bsz, t, D_REC), F32)]
    aliases = {}
    if zero_init:
        for tail in ((2, N_HEADS, D_STATE, D_STATE), (2, N_HEADS, D_STATE), (2, 1, HEAD_LANES)):
            shape, spec, _ = _layer_cache(tail, l, bsz, nb)
            out_shape.append(shape)
            out_specs.append(spec)
        if caches is not None:
            in_specs += [pl.BlockSpec(memory_space=pl.ANY)] * len(caches)
            aliases = {len(args) + i: 2 + i for i in range(len(caches))}
            args += list(caches)
    else:
        in_specs += [state_c, state_m]
        args += list(init)
    hf, hb, *state_out = pl.pallas_call(
        functools.partial(_mlstm_kernel, nblk=nblk, rows=rows, zero_init=zero_init, n_alias=len(aliases)),
        grid=(bsz // nb, nblk),
        in_specs=in_specs,
        out_specs=out_specs,
        out_shape=out_shape,
        input_output_aliases=aliases,
        scratch_shapes=[pltpu.VMEM((nb, 2, D_STATE, 2 * HEAD_LANES), F32),
                        pltpu.VMEM((nb, 2, 1, HEAD_LANES), F32)],
        compiler_params=_params(("parallel", "arbitrary")),
        name="mlstm_scan",
    )(*args)
    return (hf.reshape(bsz * t, D_REC), hb.reshape(bsz * t, D_REC), *state_out)


def _ssd_chunk(x4, bcm, dt128, da128, d, sg_prev, reads, block, tri, b_sel, s_sel, bd):
    chans = [GATE_DT + d * N_HEADS + h for h in range(N_HEADS)]
    dt = _spread(dt128, chans)
    da = _spread(da128, chans)
    ac = _cumsum_rows(tri[d], da)
    atot = jnp.sum(da, 0, keepdims=True)
    a_row = jnp.sum(reads[1 - d] * da, 0, keepdims=True)
    decay = jnp.exp(ac - a_row + block[d])
    bmat = bcm[:, :LANES].astype(BF16)
    cmat = bcm[:, LANES:].astype(BF16)
    bbd = b_sel * _stack_heads(bmat)
    g4 = lax.dot_general(cmat, bbd, NT_DIMS, preferred_element_type=F32)
    xbd = bd * _stack_heads((x4 * dt).astype(BF16))
    y = (jnp.dot((g4 * decay).astype(BF16), xbd, preferred_element_type=F32)
         + jnp.dot(cmat, sg_prev.astype(BF16), preferred_element_type=F32) * jnp.exp(ac))
    w = jnp.exp(atot - ac) * dt
    dsg = lax.dot_general(bmat, (x4 * w).astype(BF16), TN_DIMS, preferred_element_type=F32)
    sg_new = jnp.exp(atot) * sg_prev + s_sel * dsg
    return y, sg_new


def _ssd_kernel(*refs, nblk, rows, zero_init, n_alias):
    xf_ref, bcf_ref, gf_ref, xb_ref, bcb_ref, gb_ref, dtb_ref, alog_ref, dskip_ref = refs[:9]
    if not zero_init:
        s0_ref = refs[9]
    outs = refs[9 + (0 if zero_init else 1) + n_alias:-1]
    yf_ref, yb_ref = outs[:2]
    s_scr = refs[-1]
    j = pl.program_id(1)
    nchunk = rows // CHUNK

    @pl.when(j == 0)
    def _():
        if zero_init:
            s_scr[...] = jnp.zeros_like(s_scr)
        else:
            s_scr[...] = s0_ref[0]

    reads, block, tri, _ = _scan_consts()
    bd = _group_mask(HEAD_LANES, HEAD_LANES, 0, 0, BF16)
    b_sel = _group_mask(HEAD_LANES, LANES, 1, 0, BF16)
    s_sel = _group_mask(LANES, HEAD_LANES, 0, 1, F32)
    a_coef = -jnp.exp(alog_ref[...])

    state = [s_scr[d] for d in range(2)]
    for ci in range(nchunk):
        for d in range(2):
            x_ref, bc_ref, g_ref, y_ref = ((xf_ref, bcf_ref, gf_ref, yf_ref) if d == 0
                                           else (xb_ref, bcb_ref, gb_ref, yb_ref))
            cj = ci if d == 0 else nchunk - 1 - ci
            rs = slice(cj * CHUNK, (cj + 1) * CHUNK)
            dt128 = jax.nn.softplus(g_ref[rs, :] + dtb_ref[...])
            x4 = x_ref[rs, :]
            y, state[d] = _ssd_chunk(x4, bc_ref[rs, :], dt128, dt128 * a_coef, d, state[d],
                                     reads, block, tri, b_sel, s_sel, bd)
            if d == 0:
                y = y + dskip_ref[...] * x4
            y_ref[rs, :] = y
    for d in range(2):
        s_scr[d] = state[d]

    if zero_init:
        sout_ref = outs[2]

        @pl.when(j == nblk - 1)
        def _():
            for d in range(2):
                s_t = s_scr[d].T
                for h in range(N_HEADS):
                    g0 = (h // 2) * D_STATE
                    sout_ref[0, 0, d, h] = s_t[h * D_STATE:(h + 1) * D_STATE, g0:g0 + D_STATE]
            _zero_later_layers(sout_ref)


def _ssd(u, dt_bias_row, alog_row, dskip_row, bsz, t, init, l=0, cache=None):
    rows = min(t, SCAN_ROWS)
    nblk = t // rows
    zero_init = init is None

    def fwd(col):
        return lambda b, j: (b * nblk + j, col)

    def bwd(col):
        return lambda b, j: (b * nblk + nblk - 1 - j, col)

    in_specs = []
    for mk in (fwd, bwd):
        in_specs += [pl.BlockSpec((rows, D_REC), mk(COL_SX // D_REC)),
                     pl.BlockSpec((rows, D_REC), mk(COL_SBC // D_REC)),
                     pl.BlockSpec((rows, LANES), mk(COL_SMALL // LANES))]
    in_specs += [pl.BlockSpec((1, LANES), lambda b, j: (0, 0)),
                 pl.BlockSpec((1, LANES), lambda b, j: (0, 0)),
                 pl.BlockSpec((1, D_REC), lambda b, j: (0, 0))]
    args = [u] * 6 + [dt_bias_row, alog_row, dskip_row]
    out_specs = [pl.BlockSpec((rows, D_REC), fwd(0)), pl.BlockSpec((rows, D_REC), bwd(0))]
    out_shape = [jax.ShapeDtypeStruct((bsz * t, D_REC), F32), jax.ShapeDtypeStruct((bsz * t, D_REC), F32)]
    aliases = {}
    if zero_init:
        shape, spec, _ = _layer_cache((2, N_HEADS, D_STATE, D_STATE), l, bsz)
        out_shape.append(shape)
        out_specs.append(spec)
        if cache is not None:
            in_specs.append(pl.BlockSpec(memory_space=pl.ANY))
            aliases = {len(args): 2}
            args.append(cache)
    else:
        in_specs.append(pl.BlockSpec((1, 2, LANES, HEAD_LANES), lambda b, j: (b, 0, 0, 0)))
        args.append(init)
    return pl.pallas_call(
        functools.partial(_ssd_kernel, nblk=nblk, rows=rows, zero_init=zero_init, n_alias=len(aliases)),
        grid=(bsz, nblk),
        in_specs=in_specs,
        out_specs=out_specs,
        out_shape=out_shape,
        input_output_aliases=aliases,
        scratch_shapes=[pltpu.VMEM((2, LANES, HEAD_LANES), F32)],
        compiler_params=_params(("parallel", "arbitrary")),
        name="ssd_scan",
    )(*args)


def _outproj_kernel(att_ref, hf_ref, hb_ref, mo_ref, yf_ref, yb_ref, z_ref, w_ref, x_ref, gate_ref,
                    mnw_ref, snw_ref, lg_ref, lb_ref, o_ref):
    hh = hf_ref[...] + hb_ref[...]
    parts = []
    for h in range(N_HEADS):
        xh = hh[:, h * D_STATE:(h + 1) * D_STATE]
        mu = jnp.mean(xh, -1, keepdims=True)
        dlt = xh - mu
        var = jnp.mean(dlt * dlt, -1, keepdims=True)
        parts.append(dlt * lax.rsqrt(var + EPS))
    ml = jax.nn.sigmoid(mo_ref[...]) * jnp.concatenate(parts, axis=1) * mnw_ref[...]
    yz = (yf_ref[...] + yb_ref[...]) * _silu(z_ref[...])
    parts = []
    for grp in range(N_GROUPS):
        yg = yz[:, grp * LANES:(grp + 1) * LANES]
        parts.append(yg * lax.rsqrt(jnp.mean(yg * yg, -1, keepdims=True) + EPS))
    ssm = jnp.concatenate(parts, axis=1) * snw_ref[...]
    mixed = (jnp.dot(att_ref[...], w_ref[0:D_ATT], preferred_element_type=F32)
             + _bdot(ml, w_ref[D_ATT:D_ATT + D_REC])
             + _bdot(ssm, w_ref[D_ATT + D_REC:D_MODEL]))
    y = ALPHA * x_ref[...] + gate_ref[0] * mixed
    o_ref[...] = _layernorm_rows(y, lg_ref[...], lb_ref[...])


def _out_proj(att, hf, hb, yf, yb, u, w, x, gate, mnw, snw, lg, lb, rows_per_mod):
    n = x.shape[0]
    tpb = rows_per_mod // ROW_TILE
    row = lambda width, col: pl.BlockSpec((ROW_TILE, width), lambda i: (i, col))
    vec = lambda width: pl.BlockSpec((1, width), lambda i: (0, 0))
    return pl.pallas_call(
        _outproj_kernel,
        grid=(n // ROW_TILE,),
        in_specs=[row(D_ATT, 0), row(D_REC, 0), row(D_REC, 0), row(D_REC, COL_MO // D_REC),
                  row(D_REC, 0), row(D_REC, 0), row(D_REC, COL_SZ // D_REC),
                  pl.BlockSpec((D_MODEL, D_MODEL), lambda i: (0, 0)),
                  row(D_MODEL, 0),
                  pl.BlockSpec((1, 1, D_MODEL), lambda i: (i // tpb, 0, 0)),
                  vec(D_REC), vec(D_REC), vec(D_MODEL), vec(D_MODEL)],
        out_specs=row(D_MODEL, 0),
        out_shape=jax.ShapeDtypeStruct((n, D_MODEL), F32),
        compiler_params=_params(("parallel",)),
        name="out_proj",
    )(att, hf, hb, u, yf, yb, u, w, x, gate, mnw.reshape(1, D_REC), snw.reshape(1, D_REC),
      lg.reshape(1, D_MODEL), lb.reshape(1, D_MODEL))


FF_TILE = D_FF // 2
FFN_ROWS = 1024
FFN_SUB = 512
FFN_VMEM_LIMIT = 56 * 1024 * 1024


def _swiglu_partial(h, w1, w3, w2):
    a = jnp.dot(h, w1, preferred_element_type=F32)
    b = jnp.dot(h, w3, preferred_element_type=F32)
    return jnp.dot((_silu(a) * b).astype(BF16), w2, preferred_element_type=F32)


def _ffn_kernel(x_ref, sc_ref, sh_ref, gate_ref, w1_ref, w3_ref, w2_ref, lg_ref, lb_ref, o_ref):
    for r0 in range(0, FFN_ROWS, FFN_SUB):
        rows = slice(r0, r0 + FFN_SUB)
        x = x_ref[rows, :]
        h = (x * (1.0 + sc_ref[0]) + sh_ref[0]).astype(BF16)
        y = ALPHA * x + gate_ref[0] * _swiglu_partial(h, w1_ref[...], w3_ref[...], w2_ref[...])
        o_ref[rows, :] = _layernorm_rows(y, lg_ref[...], lb_ref[...])


def _ffn(x, sc, sh, gate, w1, w3, w2, lg, lb, rows_per_mod):
    n = x.shape[0]
    tpb = rows_per_mod // FFN_ROWS
    modspec = pl.BlockSpec((1, 1, D_MODEL), lambda i: (i // tpb, 0, 0))
    vec = pl.BlockSpec((1, D_MODEL), lambda i: (0, 0))
    once = pl.Buffered(1)
    return pl.pallas_call(
        _ffn_kernel,
        grid=(n // FFN_ROWS,),
        in_specs=[pl.BlockSpec((FFN_ROWS, D_MODEL), lambda i: (i, 0)), modspec, modspec, modspec,
                  pl.BlockSpec((D_MODEL, D_FF), lambda i: (0, 0), pipeline_mode=once),
                  pl.BlockSpec((D_MODEL, D_FF), lambda i: (0, 0), pipeline_mode=once),
                  pl.BlockSpec((D_FF, D_MODEL), lambda i: (0, 0), pipeline_mode=once), vec, vec],
        out_specs=pl.BlockSpec((FFN_ROWS, D_MODEL), lambda i: (i, 0)),
        out_shape=jax.ShapeDtypeStruct((n, D_MODEL), F32),
        compiler_params=_params(("parallel",), FFN_VMEM_LIMIT),
        name="ffn_dense",
    )(x, sc, sh, gate, w1, w3, w2, lg.reshape(1, D_MODEL), lb.reshape(1, D_MODEL))


def _router_kernel(x_ref, sc_ref, sh_ref, rw_ref, gates_ref, h_ref):
    h = x_ref[...] * (1.0 + sc_ref[0]) + sh_ref[0]
    h_ref[...] = h.astype(BF16)
    logits = jnp.dot(h, rw_ref[...], precision=HIGHEST, preferred_element_type=F32)
    lane = lax.broadcasted_iota(jnp.int32, logits.shape, 1)
    valid = lane < N_EXPERTS
    p = jnp.where(valid, _softmax_rows(jnp.where(valid, logits, -jnp.inf)), -2.0)
    p1 = jnp.max(p, -1, keepdims=True)
    i1 = jnp.min(jnp.where(p == p1, lane, LANES), -1, keepdims=True)
    rest = jnp.where(lane == i1, -1.0, p)
    p2 = jnp.max(rest, -1, keepdims=True)
    i2 = jnp.min(jnp.where(rest == p2, lane, LANES), -1, keepdims=True)
    tot = p1 + p2
    gates_ref[...] = jnp.where(lane == i1, p1 / tot, jnp.where(lane == i2, p2 / tot, 0.0))


def _router(x, sc, sh, router_w, rows_per_mod):
    n = x.shape[0]
    tpb = rows_per_mod // ROW_TILE
    modspec = pl.BlockSpec((1, 1, D_MODEL), lambda i: (i // tpb, 0, 0))
    rw = jnp.pad(router_w, ((0, 0), (0, LANES - N_EXPERTS)))
    return pl.pallas_call(
        _router_kernel,
        grid=(n // ROW_TILE,),
        in_specs=[pl.BlockSpec((ROW_TILE, D_MODEL), lambda i: (i, 0)), modspec, modspec,
                  pl.BlockSpec((D_MODEL, LANES), lambda i: (0, 0))],
        out_specs=[pl.BlockSpec((ROW_TILE, LANES), lambda i: (i, 0)),
                   pl.BlockSpec((ROW_TILE, D_MODEL), lambda i: (i, 0))],
        out_shape=[jax.ShapeDtypeStruct((n, LANES), F32), jax.ShapeDtypeStruct((n, D_MODEL), BF16)],
        compiler_params=_params(("parallel",)),
        name="router",
    )(x, sc, sh, rw)


MOE_ROWS = 1024
MOE_TILE = 128
MOE_MAX_TILES = MOE_ROWS // MOE_TILE
MOE_VMEM_LIMIT = 56 * 1024 * 1024


def _moe_kernel(h_ref, gates_ref, w13_ref, w2_ref, x_ref, gate_ref, lg_ref, lb_ref, o_ref,
                slot_scr, slott_scr, hs_scr, ys_scr):
    e = pl.program_id(1)
    j = pl.program_id(2)
    last_j = pl.num_programs(2) - 1

    @pl.when(jnp.logical_and(e == 0, j == 0))
    def _():
        r = lax.broadcasted_iota(jnp.int32, (MOE_ROWS, MOE_ROWS), 0)
        c = lax.broadcasted_iota(jnp.int32, (MOE_ROWS, MOE_ROWS), 1)
        before = jnp.where(c < r, 1.0, 0.0).astype(BF16)
        mask = gates_ref[...] != 0.0
        rank = jnp.dot(before, jnp.where(mask, 1.0, 0.0).astype(BF16), preferred_element_type=F32)
        slot = jnp.where(mask, rank, -1.0).astype(jnp.int32)
        slot_scr[...] = slot
        slott_scr[...] = slot.T
        o_ref[...] = jnp.zeros_like(o_ref)

    lane = lax.broadcasted_iota(jnp.int32, (MOE_ROWS, LANES), 1)
    slot_col = jnp.max(jnp.where(lane == e, slot_scr[...], -1), -1, keepdims=True)
    n_tiles = (jnp.max(slot_col) + MOE_TILE) // MOE_TILE

    def tile(k, carry):
        @pl.when(j == 0)
        def _():
            slot_row = slott_scr[pl.ds(e, 1), :]
            rr = lax.broadcasted_iota(jnp.int32, (MOE_TILE, MOE_ROWS), 0) + k * MOE_TILE
            pick = jnp.where(rr == slot_row, 1.0, 0.0).astype(BF16)
            hs_scr[k] = jnp.dot(pick, h_ref[...], preferred_element_type=F32).astype(BF16)
            ys_scr[k] = jnp.zeros((MOE_TILE, D_MODEL), F32)

        ab = jnp.dot(hs_scr[k], w13_ref[0], preferred_element_type=F32)
        act = (_silu(ab[:, :FF_TILE]) * ab[:, FF_TILE:]).astype(BF16)
        ys_scr[k] += jnp.dot(act, w2_ref[0], preferred_element_type=F32)

        @pl.when(j == last_j)
        def _():
            g_col = jnp.sum(jnp.where(lane == e, gates_ref[...], 0.0), -1, keepdims=True)
            cc = lax.broadcasted_iota(jnp.int32, (MOE_ROWS, 2 * MOE_TILE), 1)
            cc = jnp.where(cc >= MOE_TILE, cc - MOE_TILE, cc) + k * MOE_TILE
            put = jnp.where(slot_col == cc, 1.0, 0.0).astype(BF16)
            y = ys_scr[k]
            y_hi = y.astype(BF16)
            y_lo = (y - y_hi.astype(F32)).astype(BF16)
            back = jnp.dot(put, jnp.concatenate([y_hi, y_lo], axis=0), preferred_element_type=F32)
            o_ref[...] += g_col * back

        return carry

    lax.fori_loop(0, n_tiles, tile, 0)

    @pl.when(jnp.logical_and(e == pl.num_programs(1) - 1, j == last_j))
    def _():
        y = ALPHA * x_ref[...] + gate_ref[0] * o_ref[...]
        o_ref[...] = _layernorm_rows(y, lg_ref[...], lb_ref[...])


def _moe(h, gates, w13, w2, x, gate, lg, lb, rows_per_mod):
    n = h.shape[0]
    tpb = rows_per_mod // MOE_ROWS
    vec = pl.BlockSpec((1, D_MODEL), lambda i, e, j: (0, 0))
    return pl.pallas_call(
        _moe_kernel,
        grid=(n // MOE_ROWS, N_EXPERTS, D_FF // FF_TILE),
        in_specs=[pl.BlockSpec((MOE_ROWS, D_MODEL), lambda i, e, j: (i, 0)),
                  pl.BlockSpec((MOE_ROWS, LANES), lambda i, e, j: (i, 0)),
                  pl.BlockSpec((1, D_MODEL, 2 * FF_TILE), lambda i, e, j: (e, 0, j)),
                  pl.BlockSpec((1, FF_TILE, D_MODEL), lambda i, e, j: (e, j, 0)),
                  pl.BlockSpec((MOE_ROWS, D_MODEL), lambda i, e, j: (i, 0)),
                  pl.BlockSpec((1, 1, D_MODEL), lambda i, e, j: (i // tpb, 0, 0)), vec, vec],
        out_specs=pl.BlockSpec((MOE_ROWS, D_MODEL), lambda i, e, j: (i, 0)),
        out_shape=jax.ShapeDtypeStruct((n, D_MODEL), F32),
        scratch_shapes=[pltpu.VMEM((MOE_ROWS, LANES), jnp.int32), pltpu.VMEM((LANES, MOE_ROWS), jnp.int32),
                        pltpu.VMEM((MOE_MAX_TILES, MOE_TILE, D_MODEL), BF16),
                        pltpu.VMEM((MOE_MAX_TILES, MOE_TILE, D_MODEL), F32)],
        compiler_params=_params(("parallel", "arbitrary", "arbitrary"), MOE_VMEM_LIMIT),
        name="moe",
    )(h, gates, w13, w2, x, gate, lg.reshape(1, D_MODEL), lb.reshape(1, D_MODEL))


def _permute_w_in(w):
    pad = jnp.zeros((D_MODEL, U_COLS - ORIG_END), w.dtype)
    return jnp.concatenate([w[:, :ORIG_GATES], w[:, ORIG_SX:ORIG_DT], w[:, ORIG_SZ:ORIG_SX],
                            w[:, ORIG_GATES:ORIG_SZ], w[:, ORIG_DT:ORIG_END], pad], axis=1).astype(BF16)


def _pair_ff_tiles(w1, w3):
    tiles = lambda w: w.astype(BF16).reshape(w.shape[:-1] + (D_FF // FF_TILE, FF_TILE))
    return jnp.concatenate([tiles(w1), tiles(w3)], axis=-1).reshape(w1.shape[:-1] + (2 * D_FF,))


def _small_row(vals, offset):
    v = vals.reshape(-1).astype(F32)
    return jnp.zeros((1, LANES), F32).at[0, offset:offset + v.shape[0]].set(v)


def _pack_mlstm_state(c, n, m):
    shape = c.shape[:2] + (D_STATE, HEAD_LANES)
    c_rows = jnp.swapaxes(c, 2, 3).reshape(shape)
    n_rows = jnp.broadcast_to(jnp.swapaxes(n, 2, 3)[..., None], c.shape[:2] + (D_STATE, N_HEADS, D_STATE))
    return (jnp.concatenate([c_rows, n_rows.reshape(shape)], axis=-1),
            jnp.repeat(m, D_STATE, axis=-1)[:, :, None, :])


def _pack_ssd_state(s):
    sel = (jnp.arange(N_GROUPS)[:, None] == jnp.arange(N_HEADS)[None, :] // 2).astype(F32)
    return jnp.einsum('bdhpn,gh->bdgnhp', s, sel).reshape(s.shape[:2] + (LANES, HEAD_LANES))


def _layer(x, mods, P, l, bsz, t, ctx, caches=None):
    sh1, sc1, g1, sh2, sc2, g2 = mods
    rows_per_mod = x.shape[0] // sh1.shape[0]
    lam_init = 0.8 - 0.6 * math.exp(-0.3 * l)
    u = _in_proj(x, sc1, sh1, P['w_in'][l], P['conv_w'][l], P['conv_b'][l], rows_per_mod, t)

    gate_bias = (_small_row(P['mlstm_gate_b'][l, 0], GATE_I) + _small_row(P['mlstm_gate_b'][l, 1], GATE_F))
    dt_bias = _small_row(P['ssm_dt_bias'][l], GATE_DT)
    alog = _small_row(P['ssm_A_log'][l], GATE_DT)
    dskip = jnp.repeat(P['ssm_D'][l].astype(F32), D_STATE).reshape(1, D_REC)

    if ctx is None:
        att, k_new, v_new = _attention_ctx(u, P['attn_lambda'][l], P['attn_norm_w'][l], lam_init, bsz, t, l,
                                           None if caches is None else caches[0:2])
        m_init = s_init = None
    else:
        ck, cv, c_c, c_n, c_m, c_s = ctx
        q, k, v = _rope_prep(u, bsz, t)
        k_all = jnp.concatenate([k, ck.astype(BF16)], axis=3)
        v_all = jnp.concatenate([v, cv.astype(BF16)], axis=2)
        att = _attention_lat(q, k_all, v_all, P['attn_lambda'][l], P['attn_norm_w'][l], lam_init, bsz, t)
        m_init = _pack_mlstm_state(c_c, c_n, c_m)
        s_init = _pack_ssd_state(c_s)
    hf, hb, *mlstm_caches = _mlstm(u, gate_bias, bsz, t, m_init, l, None if caches is None else caches[2:5])
    yf, yb, *ssd_caches = _ssd(u, dt_bias, alog, dskip, bsz, t, s_init, l,
                               None if caches is None else caches[5])

    x = _out_proj(att, hf, hb, yf, yb, u, P['w_out'][l], x, g1, P['mlstm_norm_w'][l], P['ssm_norm_w'][l],
                  P['ln_g'][l, 0], P['ln_b'][l, 0], rows_per_mod)
    if l % 2 == 0:
        x = _ffn(x, sc2, sh2, g2, P['ffn_w1'][l // 2], P['ffn_w3'][l // 2], P['ffn_w2'][l // 2],
                 P['ln_g'][l, 1], P['ln_b'][l, 1], rows_per_mod)
    else:
        gates, h2 = _router(x, sc2, sh2, P['router_w'][l // 2], rows_per_mod)
        x = _moe(h2, gates, P['moe_w13'][l // 2], P['moe_w2'][l // 2],
                 x, g2, P['ln_g'][l, 1], P['ln_b'][l, 1], rows_per_mod)
    if ctx is None:
        return x, (k_new, v_new, *mlstm_caches, *ssd_caches)
    return x, None


def kernel(x_prompt, x_sample, c, cache_attn_k, cache_attn_v, state_mlstm_C, state_mlstm_n, state_mlstm_m, state_ssm, c_ctx, w_ada, b_ada, w_in, w_out, attn_lambda, attn_norm_w, mlstm_gate_b, mlstm_norm_w, conv_w, conv_b, ssm_A_log, ssm_dt_bias, ssm_D, ssm_norm_w, ln_g, ln_b, ffn_w1, ffn_w3, ffn_w2, router_w, moe_w1, moe_w3, moe_w2):
    bsz, seq, _ = x_prompt.shape
    dbsz, dseq, _ = x_sample.shape
    P = dict(w_in=[_permute_w_in(w_in[l]) for l in range(DEPTH)], w_out=w_out.astype(BF16),
             attn_lambda=attn_lambda, attn_norm_w=attn_norm_w, mlstm_gate_b=mlstm_gate_b,
             mlstm_norm_w=mlstm_norm_w, conv_w=conv_w, conv_b=conv_b, ssm_A_log=ssm_A_log,
             ssm_dt_bias=ssm_dt_bias, ssm_D=ssm_D, ssm_norm_w=ssm_norm_w, ln_g=ln_g, ln_b=ln_b,
             ffn_w1=ffn_w1.astype(BF16), ffn_w3=ffn_w3.astype(BF16), ffn_w2=ffn_w2.astype(BF16),
             router_w=router_w, moe_w13=_pair_ff_tiles(moe_w1, moe_w3),
             moe_w2=moe_w2.astype(BF16))

    cvec = jnp.zeros((8, D_MODEL), F32).at[0].set(c_ctx).at[1:1 + dbsz].set(c)
    mod = _modulation(cvec, w_ada, b_ada)

    def mods_for(l, lo, hi):
        return [mod[l, lo:hi, i * D_MODEL:(i + 1) * D_MODEL][:, None, :] for i in range(6)]

    y_prompt = x_prompt.reshape(bsz * seq, D_MODEL)
    caches = None
    for l in range(DEPTH):
        y_prompt, caches = _layer(y_prompt, mods_for(l, 0, 1), P, l, bsz, seq, None, caches)
    new_k, new_v, new_c, new_n, m_spread, new_s = caches

    y_sample = x_sample.reshape(dbsz * dseq, D_MODEL)
    for l in range(DEPTH):
        ctx = (cache_attn_k[:, l], cache_attn_v[:, l], state_mlstm_C[:, l], state_mlstm_n[:, l],
               state_mlstm_m[:, l], state_ssm[:, l])
        y_sample, _ = _layer(y_sample, mods_for(l, 1, 1 + dbsz), P, l, dbsz, dseq, ctx)

    return (y_prompt.reshape(bsz, seq, D_MODEL), y_sample.reshape(dbsz, dseq, D_MODEL),
            new_k, new_v, new_c, new_n, m_spread[:, :, :, 0, ::D_STATE], new_s)
```

```python
import functools
import math

import jax
import jax.numpy as jnp
from jax import lax
from jax.experimental import pallas as pl
from jax.experimental.pallas import tpu as pltpu

F32 = jnp.float32
BF16 = jnp.bfloat16
HIGHEST = lax.Precision.HIGHEST

D_MODEL = 1024
DEPTH = 2
GRID_W = 64
N_HEADS = 4
D_ATT = 512
D_HEAD_V = 128
D_QK = 64
D_REC = 256
D_STATE = 64
N_GROUPS = 2
D_CONV = 3
D_FF = 2816
N_EXPERTS = 8
ALPHA = (2.0 * DEPTH) ** 0.25
CHUNK = 64
ROPE_BASE = 10000.0
LOG2E = 1.4426950408889634
EPS = 1e-5

COL_AQ, COL_AK, COL_AV = 0, 512, 1024
COL_MQ, COL_MK, COL_MV, COL_MO = 1536, 1792, 2048, 2304
COL_SX, COL_SBC, COL_SZ = 2560, 2816, 3072
COL_SMALL = 3328
U_COLS = 3584
ORIG_GATES, ORIG_SZ, ORIG_SX, ORIG_DT, ORIG_END = 2560, 2576, 2832, 3344, 3352
GATE_I, GATE_F, GATE_DT = 0, 8, 16

LANES = 128
ROW_TILE = 512
SCAN_ROWS = 512
SCAN_BATCH = 2
VMEM_LIMIT = 48 * 1024 * 1024

NT_DIMS = (((1,), (1,)), ((), ()))
TN_DIMS = (((0,), (0,)), ((), ()))


def _params(sem, vmem=VMEM_LIMIT):
    return pltpu.CompilerParams(dimension_semantics=sem, vmem_limit_bytes=vmem)


def _silu(x):
    return x * jax.nn.sigmoid(x)


def _bdot(a, b):
    return jnp.dot(a.astype(BF16), b.astype(BF16), preferred_element_type=F32)


def _bdot_nt(a, b):
    return lax.dot_general(a.astype(BF16), b.astype(BF16), NT_DIMS, preferred_element_type=F32)


def _layernorm_rows(y, g, b):
    mu = jnp.mean(y, -1, keepdims=True)
    d = y - mu
    var = jnp.mean(d * d, -1, keepdims=True)
    return d * lax.rsqrt(var + EPS) * g + b


def _mod_kernel(c_ref, w_ref, b_ref, o_ref):
    o_ref[0] = jnp.dot(_silu(c_ref[...]), w_ref[0], precision=HIGHEST,
                       preferred_element_type=F32) + b_ref[0]


def _modulation(cvec, w_ada, b_ada):
    tn = 1536
    return pl.pallas_call(
        _mod_kernel,
        grid=(DEPTH, 6 * D_MODEL // tn),
        in_specs=[pl.BlockSpec((8, D_MODEL), lambda l, j: (0, 0)),
                  pl.BlockSpec((1, D_MODEL, tn), lambda l, j: (l, 0, j)),
                  pl.BlockSpec((1, 1, tn), lambda l, j: (l, 0, j))],
        out_specs=pl.BlockSpec((1, 8, tn), lambda l, j: (l, 0, j)),
        out_shape=jax.ShapeDtypeStruct((DEPTH, 8, 6 * D_MODEL), F32),
        compiler_params=_params(("parallel", "parallel")),
        name="modulation",
    )(cvec, w_ada, b_ada.reshape(DEPTH, 1, 6 * D_MODEL))


CONV_COLS = 2 * D_REC
HALO = 16


def _inproj_kernel(x_ref, prev_ref, next_ref, sc_ref, sh_ref, w_ref, cw_ref, cb_ref, o_ref, *, seq):
    def modulate(x):
        return (x * (1.0 + sc_ref[0]) + sh_ref[0]).astype(BF16)

    h = modulate(x_ref[...])
    for n0 in range(0, U_COLS, CONV_COLS):
        if n0 != COL_SX:
            o_ref[:, n0:n0 + CONV_COLS] = jnp.dot(h, w_ref[:, n0:n0 + CONV_COLS], preferred_element_type=F32)
            continue
        h_ext = jnp.concatenate([h, modulate(prev_ref[...]), modulate(next_ref[...])], axis=0)
        pre_ext = jnp.dot(h_ext, w_ref[:, n0:n0 + CONV_COLS], preferred_element_type=F32)
        pre = pre_ext[:ROW_TILE]
        before = pre_ext[ROW_TILE + HALO - 1:ROW_TILE + HALO]
        after = pre_ext[ROW_TILE + HALO:ROW_TILE + HALO + 1]
        r = lax.broadcasted_iota(jnp.int32, (ROW_TILE, 1), 0)
        pos = (r + pl.program_id(0) * ROW_TILE) % seq
        up = jnp.where(r == 0, before, pltpu.roll(pre, 1, 0))
        down = jnp.where(r == ROW_TILE - 1, after, pltpu.roll(pre, ROW_TILE - 1, 0))
        up = jnp.where(pos == 0, 0.0, up)
        down = jnp.where(pos == seq - 1, 0.0, down)
        cw = cw_ref[...]
        o_ref[:, n0:n0 + CONV_COLS] = _silu(up * cw[0:1] + pre * cw[1:2] + down * cw[2:3] + cb_ref[...])


def _in_proj(x, sc, sh, w, conv_w, conv_b, rows_per_mod, seq):
    n = x.shape[0]
    tpb = rows_per_mod // ROW_TILE
    r8 = ROW_TILE // HALO
    last8 = n // HALO - 1
    assert COL_SX % CONV_COLS == 0 and U_COLS % CONV_COLS == 0
    return pl.pallas_call(
        functools.partial(_inproj_kernel, seq=seq),
        grid=(n // ROW_TILE,),
        in_specs=[pl.BlockSpec((ROW_TILE, D_MODEL), lambda i: (i, 0)),
                  pl.BlockSpec((HALO, D_MODEL), lambda i: (jnp.maximum(i * r8 - 1, 0), 0)),
                  pl.BlockSpec((HALO, D_MODEL), lambda i: (jnp.minimum((i + 1) * r8, last8), 0)),
                  pl.BlockSpec((1, 1, D_MODEL), lambda i: (i // tpb, 0, 0)),
                  pl.BlockSpec((1, 1, D_MODEL), lambda i: (i // tpb, 0, 0)),
                  pl.BlockSpec((D_MODEL, U_COLS), lambda i: (0, 0)),
                  pl.BlockSpec((D_CONV, CONV_COLS), lambda i: (0, 0)),
                  pl.BlockSpec((1, CONV_COLS), lambda i: (0, 0))],
        out_specs=pl.BlockSpec((ROW_TILE, U_COLS), lambda i: (i, 0)),
        out_shape=jax.ShapeDtypeStruct((n, U_COLS), F32),
        compiler_params=_params(("parallel",)),
        name="in_proj",
    )(x, x, x, sc, sh, w, conv_w, conv_b.reshape(1, CONV_COLS))


def _lambda_scalar(lam_ref, lam_init):
    lp = lam_ref[...]
    s01 = jnp.sum(lp[0:1] * lp[1:2], axis=-1, keepdims=True)
    s23 = jnp.sum(lp[2:3] * lp[3:4], axis=-1, keepdims=True)
    return jnp.exp(s01) - jnp.exp(s23) + lam_init


def _softmax_rows(s):
    e = jnp.exp(s - jnp.max(s, -1, keepdims=True))
    return e / jnp.sum(e, -1, keepdims=True)


def _head_norm(o, nw, lam_init):
    return o * lax.rsqrt(jnp.mean(o * o, -1, keepdims=True) + EPS) * nw * (1.0 - lam_init)


def _layer_cache(tail, l, bsz, nb=1):
    zeros = (0,) * len(tail)
    owned = DEPTH if l == 0 else 1
    spec = pl.BlockSpec((nb, owned) + tuple(tail), lambda b, *_: (b, l) + zeros)
    return jax.ShapeDtypeStruct((bsz, DEPTH) + tuple(tail), F32), spec, owned


def _zero_later_layers(ref):
    for b in range(ref.shape[0]):
        for later in range(1, ref.shape[1]):
            ref[b, later] = jnp.zeros(ref.shape[2:], ref.dtype)


def _attn_ctx_kernel(u_ref, lam_ref, nw_ref, *rest, lam_init):
    att_ref, k_ref, v_ref = rest[-3:]
    lam = _lambda_scalar(lam_ref, lam_init)
    for h in range(N_HEADS):
        v = u_ref[:, COL_AV + h * D_HEAD_V:COL_AV + (h + 1) * D_HEAD_V]
        v_ref[0, 0, h] = v
        ps = []
        for m in range(2):
            c0 = h * D_HEAD_V + m * D_QK
            q = u_ref[:, COL_AQ + c0:COL_AQ + c0 + D_QK] * (D_QK ** -0.5)
            k = u_ref[:, COL_AK + c0:COL_AK + c0 + D_QK]
            k_ref[0, 0, h, m] = k
            ps.append(_softmax_rows(_bdot_nt(q, k)))
        o = _bdot(ps[0] - lam * ps[1], v)
        att_ref[:, h * D_HEAD_V:(h + 1) * D_HEAD_V] = _head_norm(o, nw_ref[...], lam_init).astype(BF16)
    _zero_later_layers(k_ref)
    _zero_later_layers(v_ref)


def _attention_ctx(u, lam_p, norm_w, lam_init, bsz, t, l, caches):
    n = bsz * t
    k_shape, k_spec, _ = _layer_cache((N_HEADS, 2, t, D_QK), l, bsz)
    v_shape, v_spec, _ = _layer_cache((N_HEADS, t, D_HEAD_V), l, bsz)
    in_specs = [pl.BlockSpec((t, 3 * D_ATT), lambda b: (b, 0)),
                pl.BlockSpec((4, D_QK), lambda b: (0, 0)),
                pl.BlockSpec((1, D_HEAD_V), lambda b: (0, 0))]
    args = [u, lam_p, norm_w.reshape(1, D_HEAD_V)]
    aliases = {}
    if caches is not None:
        in_specs += [pl.BlockSpec(memory_space=pl.ANY)] * 2
        aliases = {len(args): 1, len(args) + 1: 2}
        args += list(caches)
    return pl.pallas_call(
        functools.partial(_attn_ctx_kernel, lam_init=lam_init),
        grid=(bsz,),
        in_specs=in_specs,
        out_specs=[pl.BlockSpec((t, D_ATT), lambda b: (b, 0)), k_spec, v_spec],
        out_shape=[jax.ShapeDtypeStruct((n, D_ATT), BF16), k_shape, v_shape],
        input_output_aliases=aliases,
        compiler_params=_params(("parallel",)),
        name="attn_ctx",
    )(*args)


def _rope_kernel(u_ref, cos_ref, sa_ref, sb_ref, q_ref, k_ref, v_ref):
    cos, sa, sb = cos_ref[...], sa_ref[...], sb_ref[...]

    def rope(x):
        return x * cos + pltpu.roll(x, LANES - 16, 1) * sa + pltpu.roll(x, 16, 1) * sb

    for h in range(N_HEADS):
        q = rope(u_ref[:, COL_AQ + h * D_HEAD_V:COL_AQ + (h + 1) * D_HEAD_V]) * (LOG2E * D_QK ** -0.5)
        k = rope(u_ref[:, COL_AK + h * D_HEAD_V:COL_AK + (h + 1) * D_HEAD_V])
        for m in range(2):
            q_ref[0, h, m] = q[:, m * D_QK:(m + 1) * D_QK].astype(BF16)
            k_ref[0, h, m] = k[:, m * D_QK:(m + 1) * D_QK].astype(BF16)
        v_ref[0, h] = u_ref[:, COL_AV + h * D_HEAD_V:COL_AV + (h + 1) * D_HEAD_V].astype(BF16)


def _rope_tables(t):
    rows = jnp.repeat(jnp.arange(t // GRID_W, dtype=F32), GRID_W)
    cols = jnp.tile(jnp.arange(GRID_W, dtype=F32), t // GRID_W)
    half = D_QK // 2
    inv = ROPE_BASE ** (-jnp.arange(0, half, 2, dtype=F32) / half)
    ang_r = rows[:, None] * inv
    ang_c = cols[:, None] * inv
    ang = jnp.concatenate([ang_r, ang_r, ang_c, ang_c], -1)
    cos, sin = jnp.cos(ang), jnp.sin(ang)
    quarter = (jnp.arange(D_QK) // (D_QK // 4)) % 2
    sa = jnp.where(quarter == 0, -sin, 0.0)
    sb = jnp.where(quarter == 1, sin, 0.0)
    tile2 = lambda a: jnp.concatenate([a, a], -1)
    return tile2(cos), tile2(sa), tile2(sb)


def _rope_prep(u, bsz, t):
    tr = 512
    nb = t // tr
    cos, sa, sb = _rope_tables(t)
    tab = pl.BlockSpec((tr, LANES), lambda b, i: (i, 0))
    return pl.pallas_call(
        _rope_kernel,
        grid=(bsz, nb),
        in_specs=[pl.BlockSpec((tr, 3 * D_ATT), lambda b, i: (b * nb + i, 0)), tab, tab, tab],
        out_specs=[pl.BlockSpec((1, N_HEADS, 2, tr, D_QK), lambda b, i: (b, 0, 0, i, 0)),
                   pl.BlockSpec((1, N_HEADS, 2, tr, D_QK), lambda b, i: (b, 0, 0, i, 0)),
                   pl.BlockSpec((1, N_HEADS, tr, D_HEAD_V), lambda b, i: (b, 0, i, 0))],
        out_shape=[jax.ShapeDtypeStruct((bsz, N_HEADS, 2, t, D_QK), BF16),
                   jax.ShapeDtypeStruct((bsz, N_HEADS, 2, t, D_QK), BF16),
                   jax.ShapeDtypeStruct((bsz, N_HEADS, t, D_HEAD_V), BF16)],
        compiler_params=_params(("parallel", "parallel")),
        name="rope_prep",
    )(u, cos, sa, sb)


def _attn_lat_kernel(q_ref, k_ref, v_ref, lam_ref, nw_ref, o_ref, *, lam_init):
    lam = _lambda_scalar(lam_ref, lam_init)
    es, sums = [], []
    for m in range(2):
        s = lax.dot_general(q_ref[0, 0, m], k_ref[0, 0, m], NT_DIMS, preferred_element_type=F32)
        e = jnp.exp2(s - jnp.max(s, -1, keepdims=True))
        es.append(e)
        sums.append(jnp.sum(e, -1, keepdims=True))
    a = es[0] - (lam * sums[0] / sums[1]) * es[1]
    o = _bdot(a, v_ref[0, 0]) / sums[0]
    o_ref[...] = _head_norm(o, nw_ref[...], lam_init).astype(BF16)


def _attention_lat(q, k_all, v_all, lam_p, norm_w, lam_init, bsz, t):
    tq = 256
    nq = t // tq
    s = k_all.shape[3]
    return pl.pallas_call(
        functools.partial(_attn_lat_kernel, lam_init=lam_init),
        grid=(bsz, N_HEADS, nq),
        in_specs=[pl.BlockSpec((1, 1, 2, tq, D_QK), lambda b, h, i: (b, h, 0, i, 0)),
                  pl.BlockSpec((1, 1, 2, s, D_QK), lambda b, h, i: (b, h, 0, 0, 0)),
                  pl.BlockSpec((1, 1, s, D_HEAD_V), lambda b, h, i: (b, h, 0, 0)),
                  pl.BlockSpec((4, D_QK), lambda b, h, i: (0, 0)),
                  pl.BlockSpec((1, D_HEAD_V), lambda b, h, i: (0, 0))],
        out_specs=pl.BlockSpec((tq, D_HEAD_V), lambda b, h, i: (b * nq + i, h)),
        out_shape=jax.ShapeDtypeStruct((bsz * t, D_ATT), BF16),
        compiler_params=_params(("parallel", "parallel", "parallel")),
        name="attn_lat",
    )(q, k_all, v_all, lam_p, norm_w.reshape(1, D_HEAD_V))


HEAD_LANES = N_HEADS * D_STATE


def _scan_consts():
    t = lax.broadcasted_iota(jnp.int32, (CHUNK, HEAD_LANES), 0)
    s = lax.broadcasted_iota(jnp.int32, (CHUNK, HEAD_LANES), 1) & (CHUNK - 1)
    r = lax.broadcasted_iota(jnp.int32, (CHUNK, CHUNK), 0)
    c = lax.broadcasted_iota(jnp.int32, (CHUNK, CHUNK), 1)
    reads = (jnp.where(s <= t, 1.0, 0.0), jnp.where(s >= t, 1.0, 0.0))
    block = (jnp.where(s <= t, 0.0, -jnp.inf), jnp.where(s >= t, 0.0, -jnp.inf))
    tri = (jnp.where(c <= r, 1.0, 0.0).astype(BF16), jnp.where(c >= r, 1.0, 0.0).astype(BF16))
    return reads, block, tri, jnp.where(s == t, 1.0, 0.0)


def _cumsum_rows(tri, x):
    hi = x.astype(BF16)
    rest = x - hi.astype(F32)
    mid = rest.astype(BF16)
    lo = (rest - mid.astype(F32)).astype(BF16)
    parts = jnp.dot(tri, jnp.concatenate([hi, mid, lo], axis=1), preferred_element_type=F32)
    return parts[:, :HEAD_LANES] + parts[:, HEAD_LANES:2 * HEAD_LANES] + parts[:, 2 * HEAD_LANES:]


def _group_mask(rows, cols, row_shift, col_shift, dtype):
    r = lax.broadcasted_iota(jnp.int32, (rows, cols), 0) >> 6
    c = (lax.broadcasted_iota(jnp.int32, (rows, cols), 1) >> 6) & (N_HEADS - 1)
    return jnp.where((r >> row_shift) == (c >> col_shift), 1.0, 0.0).astype(dtype)


def _spread(x, chans):
    return jnp.concatenate([jnp.broadcast_to(x[:, c:c + 1], (CHUNK, D_STATE)) for c in chans], axis=1)


def _stack_heads(x):
    return jnp.concatenate([x] * N_HEADS, axis=0)


def _mlstm_chunk(q4, k4, v4, g, d, cn_prev, m_prev, reads, block, tri, eye, bd, bd_f32):
    li = _spread(g, [GATE_I + d * N_HEADS + h for h in range(N_HEADS)])
    lf = _spread(jax.nn.log_sigmoid(g), [GATE_F + d * N_HEADS + h for h in range(N_HEADS)])
    bc = _cumsum_rows(tri[d], lf)
    btot = jnp.sum(lf, 0, keepdims=True)
    b_row = jnp.sum(reads[1 - d] * lf, 0, keepdims=True)
    li_row = jnp.sum(eye * li, 0, keepdims=True)
    dm = bc - b_row + li_row + block[d]
    rmax = jnp.concatenate(
        [jnp.broadcast_to(jnp.max(dm[:, h * D_STATE:(h + 1) * D_STATE], -1, keepdims=True), (CHUNK, D_STATE))
         for h in range(N_HEADS)], axis=1)
    inter = bc + m_prev
    m_t = jnp.maximum(inter, rmax)
    w_inter = jnp.exp(inter - m_t)
    qs = (q4 * (D_STATE ** -0.5)).astype(BF16)
    kbd = bd[:, :HEAD_LANES] * _stack_heads(k4.astype(BF16))
    s4 = lax.dot_general(qs, kbd, NT_DIMS, preferred_element_type=F32) * jnp.exp(dm - m_t)
    vo = jnp.concatenate([v4.astype(BF16), jnp.ones((CHUNK, HEAD_LANES), BF16)], axis=1)
    vbd = bd * _stack_heads(vo)
    nd = (jnp.concatenate([w_inter, w_inter], axis=1)
          * jnp.dot(qs, bd * _stack_heads(cn_prev.astype(BF16)), preferred_element_type=F32)
          + jnp.dot(s4.astype(BF16), vbd, preferred_element_type=F32))
    hc = nd[:, :HEAD_LANES] / jnp.maximum(jnp.abs(nd[:, HEAD_LANES:]), jnp.exp(-m_t))
    gcol = btot - bc + li
    m_new = jnp.maximum(btot + m_prev, jnp.max(gcol, 0, keepdims=True))
    w_c = jnp.exp(btot + m_prev - m_new)
    kw = (k4 * jnp.exp(gcol - m_new)).astype(BF16)
    dcn = lax.dot_general(kw, vo, TN_DIMS, preferred_element_type=F32)
    own = sum(bd_f32[h * D_STATE:(h + 1) * D_STATE] * dcn[h * D_STATE:(h + 1) * D_STATE]
              for h in range(N_HEADS))
    cn_new = jnp.concatenate([w_c, w_c], axis=1) * cn_prev + own
    return hc, cn_new, m_new


def _mlstm_kernel(*refs, nblk, rows, zero_init, n_alias):
    qf_ref, kf_ref, vf_ref, gf_ref, qb_ref, kb_ref, vb_ref, gb_ref, bias_ref = refs[:9]
    if not zero_init:
        c0_ref, m0_ref = refs[9:11]
    outs = refs[9 + (0 if zero_init else 2) + n_alias:-2]
    hf_ref, hb_ref = outs[:2]
    cn_scr, m_scr = refs[-2:]
    j = pl.program_id(1)
    nchunk = rows // CHUNK

    @pl.when(j == 0)
    def _():
        if zero_init:
            cn_scr[...] = jnp.zeros_like(cn_scr)
            m_scr[...] = jnp.zeros_like(m_scr)
        else:
            cn_scr[...] = c0_ref[...]
            m_scr[...] = m0_ref[...]

    reads, block, tri, eye = _scan_consts()
    bd = _group_mask(HEAD_LANES, 2 * HEAD_LANES, 0, 0, BF16)
    bd_f32 = _group_mask(HEAD_LANES, 2 * HEAD_LANES, 0, 0, F32)

    chains = [(b, d) for b in range(SCAN_BATCH) for d in range(2)]
    state = {bd_: (cn_scr[bd_], m_scr[bd_]) for bd_ in chains}
    for ci in range(nchunk):
        for b, d in chains:
            q_ref, k_ref, v_ref, g_ref, h_ref = ((qf_ref, kf_ref, vf_ref, gf_ref, hf_ref) if d == 0
                                                 else (qb_ref, kb_ref, vb_ref, gb_ref, hb_ref))
            cj = ci if d == 0 else nchunk - 1 - ci
            rs = slice(cj * CHUNK, (cj + 1) * CHUNK)
            g = g_ref[b, rs, :] + bias_ref[...]
            hc, cn_new, m_new = _mlstm_chunk(q_ref[b, rs, :], k_ref[b, rs, :], v_ref[b, rs, :], g, d,
                                             *state[b, d], reads, block, tri, eye, bd, bd_f32)
            h_ref[b, rs, :] = hc
            state[b, d] = (cn_new, m_new)
    for bd_ in chains:
        cn_scr[bd_], m_scr[bd_] = state[bd_]

    if zero_init:
        cout_ref, nout_ref, mout_ref = outs[2:]

        @pl.when(j == nblk - 1)
        def _():
            r = lax.broadcasted_iota(jnp.int32, (D_STATE, D_STATE), 0)
            c = lax.broadcasted_iota(jnp.int32, (D_STATE, D_STATE), 1)
            for b, d in chains:
                for h in range(N_HEADS):
                    r0, r1 = h * D_STATE, (h + 1) * D_STATE
                    cout_ref[b, 0, d, h] = cn_scr[b, d, :, r0:r1]
                    n_spread = cn_scr[b, d, :, HEAD_LANES + r0:HEAD_LANES + r1]
                    nout_ref[b, 0, d, h:h + 1, :] = jnp.sum(jnp.where(r == c, n_spread, 0.0), 0, keepdims=True)
                mout_ref[b, 0, d] = m_scr[b, d]
            _zero_later_layers(cout_ref)
            _zero_later_layers(nout_ref)
            _zero_later_layers(mout_ref)


def _mlstm(u, gate_bias, bsz, t, init, l=0, caches=None):
    rows = min(t, SCAN_ROWS)
    nblk = t // rows
    zero_init = init is None

    nb = SCAN_BATCH
    fwd = lambda col: (lambda p, j: (p, j, col))
    bwd = lambda col: (lambda p, j: (p, nblk - 1 - j, col))
    cq, ck, cv, cg = COL_MQ // D_REC, COL_MK // D_REC, COL_MV // D_REC, COL_SMALL // LANES
    in_specs = []
    for mk in (fwd, bwd):
        in_specs += [pl.BlockSpec((nb, rows, D_REC), mk(cq)), pl.BlockSpec((nb, rows, D_REC), mk(ck)),
                     pl.BlockSpec((nb, rows, D_REC), mk(cv)), pl.BlockSpec((nb, rows, LANES), mk(cg))]
    in_specs.append(pl.BlockSpec((1, LANES), lambda p, j: (0, 0)))
    args = [u.reshape(bsz, t, U_COLS)] * 8 + [gate_bias]
    state_c = pl.BlockSpec((nb, 2, D_STATE, 2 * HEAD_LANES), lambda p, j: (p, 0, 0, 0))
    state_m = pl.BlockSpec((nb, 2, 1, HEAD_LANES), lambda p, j: (p, 0, 0, 0))
    out_specs = [pl.BlockSpec((nb, rows, D_REC), fwd(0)), pl.BlockSpec((nb, rows, D_REC), bwd(0))]
    out_shape = [jax.ShapeDtypeStruct((bsz, t, D_REC), F32), jax.ShapeDtypeStruct((bsz, t, D_REC), F32)]
    aliases = {}
    if zero_init:
        for tail in ((2, N_HEADS, D_STATE, D_STATE), (2, N_HEADS, D_STATE), (2, 1, HEAD_LANES)):
            shape, spec, _ = _layer_cache(tail, l, bsz, nb)
            out_shape.append(shape)
            out_specs.append(spec)
        if caches is not None:
            in_specs += [pl.BlockSpec(memory_space=pl.ANY)] * len(caches)
            aliases = {len(args) + i: 2 + i for i in range(len(caches))}
            args += list(caches)
    else:
        in_specs += [state_c, state_m]
        args += list(init)
    hf, hb, *state_out = pl.pallas_call(
        functools.partial(_mlstm_kernel, nblk=nblk, rows=rows, zero_init=zero_init, n_alias=len(aliases)),
        grid=(bsz // nb, nblk),
        in_specs=in_specs,
        out_specs=out_specs,
        out_shape=out_shape,
        input_output_aliases=aliases,
        scratch_shapes=[pltpu.VMEM((nb, 2, D_STATE, 2 * HEAD_LANES), F32),
                        pltpu.VMEM((nb, 2, 1, HEAD_LANES), F32)],
        compiler_params=_params(("parallel", "arbitrary")),
        name="mlstm_scan",
    )(*args)
    return (hf.reshape(bsz * t, D_REC), hb.reshape(bsz * t, D_REC), *state_out)


def _ssd_chunk(x4, bcm, dt128, da128, d, sg_prev, reads, block, tri, b_sel, s_sel, bd):
    chans = [GATE_DT + d * N_HEADS + h for h in range(N_HEADS)]
    dt = _spread(dt128, chans)
    da = _spread(da128, chans)
    ac = _cumsum_rows(tri[d], da)
    atot = jnp.sum(da, 0, keepdims=True)
    a_row = jnp.sum(reads[1 - d] * da, 0, keepdims=True)
    decay = jnp.exp(ac - a_row + block[d])
    bmat = bcm[:, :LANES].astype(BF16)
    cmat = bcm[:, LANES:].astype(BF16)
    bbd = b_sel * _stack_heads(bmat)
    g4 = lax.dot_general(cmat, bbd, NT_DIMS, preferred_element_type=F32)
    xbd = bd * _stack_heads((x4 * dt).astype(BF16))
    y = (jnp.dot((g4 * decay).astype(BF16), xbd, preferred_element_type=F32)
         + jnp.dot(cmat, sg_prev.astype(BF16), preferred_element_type=F32) * jnp.exp(ac))
    w = jnp.exp(atot - ac) * dt
    dsg = lax.dot_general(bmat, (x4 * w).astype(BF16), TN_DIMS, preferred_element_type=F32)
    sg_new = jnp.exp(atot) * sg_prev + s_sel * dsg
    return y, sg_new


def _ssd_kernel(*refs, nblk, rows, zero_init, n_alias):
    xf_ref, bcf_ref, gf_ref, xb_ref, bcb_ref, gb_ref, dtb_ref, alog_ref, dskip_ref = refs[:9]
    if not zero_init:
        s0_ref = refs[9]
    outs = refs[9 + (0 if zero_init else 1) + n_alias:-1]
    yf_ref, yb_ref = outs[:2]
    s_scr = refs[-1]
    j = pl.program_id(1)
    nchunk = rows // CHUNK

    @pl.when(j == 0)
    def _():
        if zero_init:
            s_scr[...] = jnp.zeros_like(s_scr)
        else:
            s_scr[...] = s0_ref[0]

    reads, block, tri, _ = _scan_consts()
    bd = _group_mask(HEAD_LANES, HEAD_LANES, 0, 0, BF16)
    b_sel = _group_mask(HEAD_LANES, LANES, 1, 0, BF16)
    s_sel = _group_mask(LANES, HEAD_LANES, 0, 1, F32)
    a_coef = -jnp.exp(alog_ref[...])

    state = [s_scr[d] for d in range(2)]
    for ci in range(nchunk):
        for d in range(2):
            x_ref, bc_ref, g_ref, y_ref = ((xf_ref, bcf_ref, gf_ref, yf_ref) if d == 0
                                           else (xb_ref, bcb_ref, gb_ref, yb_ref))
            cj = ci if d == 0 else nchunk - 1 - ci
            rs = slice(cj * CHUNK, (cj + 1) * CHUNK)
            dt128 = jax.nn.softplus(g_ref[rs, :] + dtb_ref[...])
            x4 = x_ref[rs, :]
            y, state[d] = _ssd_chunk(x4, bc_ref[rs, :], dt128, dt128 * a_coef, d, state[d],
                                     reads, block, tri, b_sel, s_sel, bd)
            if d == 0:
                y = y + dskip_ref[...] * x4
            y_ref[rs, :] = y
    for d in range(2):
        s_scr[d] = state[d]

    if zero_init:
        sout_ref = outs[2]

        @pl.when(j == nblk - 1)
        def _():
            for d in range(2):
                s_t = s_scr[d].T
                for h in range(N_HEADS):
                    g0 = (h // 2) * D_STATE
                    sout_ref[0, 0, d, h] = s_t[h * D_STATE:(h + 1) * D_STATE, g0:g0 + D_STATE]
            _zero_later_layers(sout_ref)


def _ssd(u, dt_bias_row, alog_row, dskip_row, bsz, t, init, l=0, cache=None):
    rows = min(t, SCAN_ROWS)
    nblk = t // rows
    zero_init = init is None

    def fwd(col):
        return lambda b, j: (b * nblk + j, col)

    def bwd(col):
        return lambda b, j: (b * nblk + nblk - 1 - j, col)

    in_specs = []
    for mk in (fwd, bwd):
        in_specs += [pl.BlockSpec((rows, D_REC), mk(COL_SX // D_REC)),
                     pl.BlockSpec((rows, D_REC), mk(COL_SBC // D_REC)),
                     pl.BlockSpec((rows, LANES), mk(COL_SMALL // LANES))]
    in_specs += [pl.BlockSpec((1, LANES), lambda b, j: (0, 0)),
                 pl.BlockSpec((1, LANES), lambda b, j: (0, 0)),
                 pl.BlockSpec((1, D_REC), lambda b, j: (0, 0))]
    args = [u] * 6 + [dt_bias_row, alog_row, dskip_row]
    out_specs = [pl.BlockSpec((rows, D_REC), fwd(0)), pl.BlockSpec((rows, D_REC), bwd(0))]
    out_shape = [jax.ShapeDtypeStruct((bsz * t, D_REC), F32), jax.ShapeDtypeStruct((bsz * t, D_REC), F32)]
    aliases = {}
    if zero_init:
        shape, spec, _ = _layer_cache((2, N_HEADS, D_STATE, D_STATE), l, bsz)
        out_shape.append(shape)
        out_specs.append(spec)
        if cache is not None:
            in_specs.append(pl.BlockSpec(memory_space=pl.ANY))
            aliases = {len(args): 2}
            args.append(cache)
    else:
        in_specs.append(pl.BlockSpec((1, 2, LANES, HEAD_LANES), lambda b, j: (b, 0, 0, 0)))
        args.append(init)
    return pl.pallas_call(
        functools.partial(_ssd_kernel, nblk=nblk, rows=rows, zero_init=zero_init, n_alias=len(aliases)),
        grid=(bsz, nblk),
        in_specs=in_specs,
        out_specs=out_specs,
        out_shape=out_shape,
        input_output_aliases=aliases,
        scratch_shapes=[pltpu.VMEM((2, LANES, HEAD_LANES), F32)],
        compiler_params=_params(("parallel", "arbitrary")),
        name="ssd_scan",
    )(*args)


def _outproj_kernel(att_ref, hf_ref, hb_ref, mo_ref, yf_ref, yb_ref, z_ref, w_ref, x_ref, gate_ref,
                    mnw_ref, snw_ref, lg_ref, lb_ref, o_ref):
    hh = hf_ref[...] + hb_ref[...]
    parts = []
    for h in range(N_HEADS):
        xh = hh[:, h * D_STATE:(h + 1) * D_STATE]
        mu = jnp.mean(xh, -1, keepdims=True)
        dlt = xh - mu
        var = jnp.mean(dlt * dlt, -1, keepdims=True)
        parts.append(dlt * lax.rsqrt(var + EPS))
    ml = jax.nn.sigmoid(mo_ref[...]) * jnp.concatenate(parts, axis=1) * mnw_ref[...]
    yz = (yf_ref[...] + yb_ref[...]) * _silu(z_ref[...])
    parts = []
    for grp in range(N_GROUPS):
        yg = yz[:, grp * LANES:(grp + 1) * LANES]
        parts.append(yg * lax.rsqrt(jnp.mean(yg * yg, -1, keepdims=True) + EPS))
    ssm = jnp.concatenate(parts, axis=1) * snw_ref[...]
    mixed = (jnp.dot(att_ref[...], w_ref[0:D_ATT], preferred_element_type=F32)
             + _bdot(ml, w_ref[D_ATT:D_ATT + D_REC])
             + _bdot(ssm, w_ref[D_ATT + D_REC:D_MODEL]))
    y = ALPHA * x_ref[...] + gate_ref[0] * mixed
    o_ref[...] = _layernorm_rows(y, lg_ref[...], lb_ref[...])


def _out_proj(att, hf, hb, yf, yb, u, w, x, gate, mnw, snw, lg, lb, rows_per_mod):
    n = x.shape[0]
    tpb = rows_per_mod // ROW_TILE
    row = lambda width, col: pl.BlockSpec((ROW_TILE, width), lambda i: (i, col))
    vec = lambda width: pl.BlockSpec((1, width), lambda i: (0, 0))
    return pl.pallas_call(
        _outproj_kernel,
        grid=(n // ROW_TILE,),
        in_specs=[row(D_ATT, 0), row(D_REC, 0), row(D_REC, 0), row(D_REC, COL_MO // D_REC),
                  row(D_REC, 0), row(D_REC, 0), row(D_REC, COL_SZ // D_REC),
                  pl.BlockSpec((D_MODEL, D_MODEL), lambda i: (0, 0)),
                  row(D_MODEL, 0),
                  pl.BlockSpec((1, 1, D_MODEL), lambda i: (i // tpb, 0, 0)),
                  vec(D_REC), vec(D_REC), vec(D_MODEL), vec(D_MODEL)],
        out_specs=row(D_MODEL, 0),
        out_shape=jax.ShapeDtypeStruct((n, D_MODEL), F32),
        compiler_params=_params(("parallel",)),
        name="out_proj",
    )(att, hf, hb, u, yf, yb, u, w, x, gate, mnw.reshape(1, D_REC), snw.reshape(1, D_REC),
      lg.reshape(1, D_MODEL), lb.reshape(1, D_MODEL))


FF_TILE = D_FF // 2
FFN_ROWS = 1024
FFN_SUB = 512
FFN_VMEM_LIMIT = 56 * 1024 * 1024


def _swiglu_partial(h, w1, w3, w2):
    a = jnp.dot(h, w1, preferred_element_type=F32)
    b = jnp.dot(h, w3, preferred_element_type=F32)
    return jnp.dot((_silu(a) * b).astype(BF16), w2, preferred_element_type=F32)


def _ffn_kernel(x_ref, sc_ref, sh_ref, gate_ref, w1_ref, w3_ref, w2_ref, lg_ref, lb_ref, o_ref):
    for r0 in range(0, FFN_ROWS, FFN_SUB):
        rows = slice(r0, r0 + FFN_SUB)
        x = x_ref[rows, :]
        h = (x * (1.0 + sc_ref[0]) + sh_ref[0]).astype(BF16)
        y = ALPHA * x + gate_ref[0] * _swiglu_partial(h, w1_ref[...], w3_ref[...], w2_ref[...])
        o_ref[rows, :] = _layernorm_rows(y, lg_ref[...], lb_ref[...])


def _ffn(x, sc, sh, gate, w1, w3, w2, lg, lb, rows_per_mod):
    n = x.shape[0]
    tpb = rows_per_mod // FFN_ROWS
    modspec = pl.BlockSpec((1, 1, D_MODEL), lambda i: (i // tpb, 0, 0))
    vec = pl.BlockSpec((1, D_MODEL), lambda i: (0, 0))
    once = pl.Buffered(1)
    return pl.pallas_call(
        _ffn_kernel,
        grid=(n // FFN_ROWS,),
        in_specs=[pl.BlockSpec((FFN_ROWS, D_MODEL), lambda i: (i, 0)), modspec, modspec, modspec,
                  pl.BlockSpec((D_MODEL, D_FF), lambda i: (0, 0), pipeline_mode=once),
                  pl.BlockSpec((D_MODEL, D_FF), lambda i: (0, 0), pipeline_mode=once),
                  pl.BlockSpec((D_FF, D_MODEL), lambda i: (0, 0), pipeline_mode=once), vec, vec],
        out_specs=pl.BlockSpec((FFN_ROWS, D_MODEL), lambda i: (i, 0)),
        out_shape=jax.ShapeDtypeStruct((n, D_MODEL), F32),
        compiler_params=_params(("parallel",), FFN_VMEM_LIMIT),
        name="ffn_dense",
    )(x, sc, sh, gate, w1, w3, w2, lg.reshape(1, D_MODEL), lb.reshape(1, D_MODEL))


def _router_kernel(x_ref, sc_ref, sh_ref, rw_ref, gates_ref, h_ref):
    h = x_ref[...] * (1.0 + sc_ref[0]) + sh_ref[0]
    h_ref[...] = h.astype(BF16)
    logits = jnp.dot(h, rw_ref[...], precision=HIGHEST, preferred_element_type=F32)
    lane = lax.broadcasted_iota(jnp.int32, logits.shape, 1)
    valid = lane < N_EXPERTS
    p = jnp.where(valid, _softmax_rows(jnp.where(valid, logits, -jnp.inf)), -2.0)
    p1 = jnp.max(p, -1, keepdims=True)
    i1 = jnp.min(jnp.where(p == p1, lane, LANES), -1, keepdims=True)
    rest = jnp.where(lane == i1, -1.0, p)
    p2 = jnp.max(rest, -1, keepdims=True)
    i2 = jnp.min(jnp.where(rest == p2, lane, LANES), -1, keepdims=True)
    tot = p1 + p2
    gates_ref[...] = jnp.where(lane == i1, p1 / tot, jnp.where(lane == i2, p2 / tot, 0.0))


def _router(x, sc, sh, router_w, rows_per_mod):
    n = x.shape[0]
    tpb = rows_per_mod // ROW_TILE
    modspec = pl.BlockSpec((1, 1, D_MODEL), lambda i: (i // tpb, 0, 0))
    rw = jnp.pad(router_w, ((0, 0), (0, LANES - N_EXPERTS)))
    return pl.pallas_call(
        _router_kernel,
        grid=(n // ROW_TILE,),
        in_specs=[pl.BlockSpec((ROW_TILE, D_MODEL), lambda i: (i, 0)), modspec, modspec,
                  pl.BlockSpec((D_MODEL, LANES), lambda i: (0, 0))],
        out_specs=[pl.BlockSpec((ROW_TILE, LANES), lambda i: (i, 0)),
                   pl.BlockSpec((ROW_TILE, D_MODEL), lambda i: (i, 0))],
        out_shape=[jax.ShapeDtypeStruct((n, LANES), F32), jax.ShapeDtypeStruct((n, D_MODEL), BF16)],
        compiler_params=_params(("parallel",)),
        name="router",
    )(x, sc, sh, rw)


MOE_ROWS = 1024
MOE_TILE = 128
MOE_MAX_TILES = MOE_ROWS // MOE_TILE
MOE_VMEM_LIMIT = 56 * 1024 * 1024


def _moe_kernel(h_ref, gates_ref, w13_ref, w2_ref, x_ref, gate_ref, lg_ref, lb_ref, o_ref,
                slot_scr, slott_scr, hs_scr, ys_scr):
    e = pl.program_id(1)
    j = pl.program_id(2)
    last_j = pl.num_programs(2) - 1

    @pl.when(jnp.logical_and(e == 0, j == 0))
    def _():
        r = lax.broadcasted_iota(jnp.int32, (MOE_ROWS, MOE_ROWS), 0)
        c = lax.broadcasted_iota(jnp.int32, (MOE_ROWS, MOE_ROWS), 1)
        before = jnp.where(c < r, 1.0, 0.0).astype(BF16)
        mask = gates_ref[...] != 0.0
        rank = jnp.dot(before, jnp.where(mask, 1.0, 0.0).astype(BF16), preferred_element_type=F32)
        slot = jnp.where(mask, rank, -1.0).astype(jnp.int32)
        slot_scr[...] = slot
        slott_scr[...] = slot.T
        o_ref[...] = jnp.zeros_like(o_ref)

    lane = lax.broadcasted_iota(jnp.int32, (MOE_ROWS, LANES), 1)
    slot_col = jnp.max(jnp.where(lane == e, slot_scr[...], -1), -1, keepdims=True)
    n_tiles = (jnp.max(slot_col) + MOE_TILE) // MOE_TILE

    def tile(k, carry):
        @pl.when(j == 0)
        def _():
            slot_row = slott_scr[pl.ds(e, 1), :]
            rr = lax.broadcasted_iota(jnp.int32, (MOE_TILE, MOE_ROWS), 0) + k * MOE_TILE
            pick = jnp.where(rr == slot_row, 1.0, 0.0).astype(BF16)
            hs_scr[k] = jnp.dot(pick, h_ref[...], preferred_element_type=F32).astype(BF16)
            ys_scr[k] = jnp.zeros((MOE_TILE, D_MODEL), F32)

        ab = jnp.dot(hs_scr[k], w13_ref[0], preferred_element_type=F32)
        act = (_silu(ab[:, :FF_TILE]) * ab[:, FF_TILE:]).astype(BF16)
        ys_scr[k] += jnp.dot(act, w2_ref[0], preferred_element_type=F32)

        @pl.when(j == last_j)
        def _():
            g_col = jnp.sum(jnp.where(lane == e, gates_ref[...], 0.0), -1, keepdims=True)
            cc = lax.broadcasted_iota(jnp.int32, (MOE_ROWS, 2 * MOE_TILE), 1)
            cc = jnp.where(cc >= MOE_TILE, cc - MOE_TILE, cc) + k * MOE_TILE
            put = jnp.where(slot_col == cc, 1.0, 0.0).astype(BF16)
            y = ys_scr[k]
            y_hi = y.astype(BF16)
            y_lo = (y - y_hi.astype(F32)).astype(BF16)
            back = jnp.dot(put, jnp.concatenate([y_hi, y_lo], axis=0), preferred_element_type=F32)
            o_ref[...] += g_col * back

        return carry

    lax.fori_loop(0, n_tiles, tile, 0)

    @pl.when(jnp.logical_and(e == pl.num_programs(1) - 1, j == last_j))
    def _():
        y = ALPHA * x_ref[...] + gate_ref[0] * o_ref[...]
        o_ref[...] = _layernorm_rows(y, lg_ref[...], lb_ref[...])


def _moe(h, gates, w13, w2, x, gate, lg, lb, rows_per_mod):
    n = h.shape[0]
    tpb = rows_per_mod // MOE_ROWS
    vec = pl.BlockSpec((1, D_MODEL), lambda i, e, j: (0, 0))
    return pl.pallas_call(
        _moe_kernel,
        grid=(n // MOE_ROWS, N_EXPERTS, D_FF // FF_TILE),
        in_specs=[pl.BlockSpec((MOE_ROWS, D_MODEL), lambda i, e, j: (i, 0)),
                  pl.BlockSpec((MOE_ROWS, LANES), lambda i, e, j: (i, 0)),
                  pl.BlockSpec((1, D_MODEL, 2 * FF_TILE), lambda i, e, j: (e, 0, j)),
                  pl.BlockSpec((1, FF_TILE, D_MODEL), lambda i, e, j: (e, j, 0)),
                  pl.BlockSpec((MOE_ROWS, D_MODEL), lambda i, e, j: (i, 0)),
                  pl.BlockSpec((1, 1, D_MODEL), lambda i, e, j: (i // tpb, 0, 0)), vec, vec],
        out_specs=pl.BlockSpec((MOE_ROWS, D_MODEL), lambda i, e, j: (i, 0)),
        out_shape=jax.ShapeDtypeStruct((n, D_MODEL), F32),
        scratch_shapes=[pltpu.VMEM((MOE_ROWS, LANES), jnp.int32), pltpu.VMEM((LANES, MOE_ROWS), jnp.int32),
                        pltpu.VMEM((MOE_MAX_TILES, MOE_TILE, D_MODEL), BF16),
                        pltpu.VMEM((MOE_MAX_TILES, MOE_TILE, D_MODEL), F32)],
        compiler_params=_params(("parallel", "arbitrary", "arbitrary"), MOE_VMEM_LIMIT),
        name="moe",
    )(h, gates, w13, w2, x, gate, lg.reshape(1, D_MODEL), lb.reshape(1, D_MODEL))


def _permute_w_in(w):
    pad = jnp.zeros((D_MODEL, U_COLS - ORIG_END), w.dtype)
    return jnp.concatenate([w[:, :ORIG_GATES], w[:, ORIG_SX:ORIG_DT], w[:, ORIG_SZ:ORIG_SX],
                            w[:, ORIG_GATES:ORIG_SZ], w[:, ORIG_DT:ORIG_END], pad], axis=1).astype(BF16)


def _pair_ff_tiles(w1, w3):
    pieces = []
    for c0 in range(0, D_FF, FF_TILE):
        pieces += [w1[..., c0:c0 + FF_TILE], w3[..., c0:c0 + FF_TILE]]
    return jnp.concatenate(pieces, axis=-1).astype(BF16)


def _small_row(vals, offset):
    v = vals.reshape(-1).astype(F32)
    return jnp.zeros((1, LANES), F32).at[0, offset:offset + v.shape[0]].set(v)


def _pack_mlstm_state(c, n, m):
    shape = c.shape[:2] + (D_STATE, HEAD_LANES)
    c_rows = jnp.swapaxes(c, 2, 3).reshape(shape)
    n_rows = jnp.broadcast_to(jnp.swapaxes(n, 2, 3)[..., None], c.shape[:2] + (D_STATE, N_HEADS, D_STATE))
    return (jnp.concatenate([c_rows, n_rows.reshape(shape)], axis=-1),
            jnp.repeat(m, D_STATE, axis=-1)[:, :, None, :])


def _pack_ssd_state(s):
    sel = (jnp.arange(N_GROUPS)[:, None] == jnp.arange(N_HEADS)[None, :] // 2).astype(F32)
    return jnp.einsum('bdhpn,gh->bdgnhp', s, sel).reshape(s.shape[:2] + (LANES, HEAD_LANES))


def _layer(x, mods, P, l, bsz, t, ctx, caches=None):
    sh1, sc1, g1, sh2, sc2, g2 = mods
    rows_per_mod = x.shape[0] // sh1.shape[0]
    lam_init = 0.8 - 0.6 * math.exp(-0.3 * l)
    u = _in_proj(x, sc1, sh1, P['w_in'][l], P['conv_w'][l], P['conv_b'][l], rows_per_mod, t)

    gate_bias = (_small_row(P['mlstm_gate_b'][l, 0], GATE_I) + _small_row(P['mlstm_gate_b'][l, 1], GATE_F))
    dt_bias = _small_row(P['ssm_dt_bias'][l], GATE_DT)
    alog = _small_row(P['ssm_A_log'][l], GATE_DT)
    dskip = jnp.repeat(P['ssm_D'][l].astype(F32), D_STATE).reshape(1, D_REC)

    if ctx is None:
        att, k_new, v_new = _attention_ctx(u, P['attn_lambda'][l], P['attn_norm_w'][l], lam_init, bsz, t, l,
                                           None if caches is None else caches[0:2])
        m_init = s_init = None
    else:
        ck, cv, c_c, c_n, c_m, c_s = ctx
        q, k, v = _rope_prep(u, bsz, t)
        k_all = jnp.concatenate([k, ck.astype(BF16)], axis=3)
        v_all = jnp.concatenate([v, cv.astype(BF16)], axis=2)
        att = _attention_lat(q, k_all, v_all, P['attn_lambda'][l], P['attn_norm_w'][l], lam_init, bsz, t)
        m_init = _pack_mlstm_state(c_c, c_n, c_m)
        s_init = _pack_ssd_state(c_s)
    hf, hb, *mlstm_caches = _mlstm(u, gate_bias, bsz, t, m_init, l, None if caches is None else caches[2:5])
    yf, yb, *ssd_caches = _ssd(u, dt_bias, alog, dskip, bsz, t, s_init, l,
                               None if caches is None else caches[5])

    x = _out_proj(att, hf, hb, yf, yb, u, P['w_out'][l], x, g1, P['mlstm_norm_w'][l], P['ssm_norm_w'][l],
                  P['ln_g'][l, 0], P['ln_b'][l, 0], rows_per_mod)
    if l % 2 == 0:
        x = _ffn(x, sc2, sh2, g2, P['ffn_w1'][l // 2], P['ffn_w3'][l // 2], P['ffn_w2'][l // 2],
                 P['ln_g'][l, 1], P['ln_b'][l, 1], rows_per_mod)
    else:
        gates, h2 = _router(x, sc2, sh2, P['router_w'][l // 2], rows_per_mod)
        x = _moe(h2, gates, P['moe_w13'][l // 2], P['moe_w2'][l // 2],
                 x, g2, P['ln_g'][l, 1], P['ln_b'][l, 1], rows_per_mod)
    if ctx is None:
        return x, (k_new, v_new, *mlstm_caches, *ssd_caches)
    return x, None


def kernel(x_prompt, x_sample, c, cache_attn_k, cache_attn_v, state_mlstm_C, state_mlstm_n, state_mlstm_m, state_ssm, c_ctx, w_ada, b_ada, w_in, w_out, attn_lambda, attn_norm_w, mlstm_gate_b, mlstm_norm_w, conv_w, conv_b, ssm_A_log, ssm_dt_bias, ssm_D, ssm_norm_w, ln_g, ln_b, ffn_w1, ffn_w3, ffn_w2, router_w, moe_w1, moe_w3, moe_w2):
    bsz, seq, _ = x_prompt.shape
    dbsz, dseq, _ = x_sample.shape
    P = dict(w_in=[_permute_w_in(w_in[l]) for l in range(DEPTH)], w_out=w_out.astype(BF16),
             attn_lambda=attn_lambda, attn_norm_w=attn_norm_w, mlstm_gate_b=mlstm_gate_b,
             mlstm_norm_w=mlstm_norm_w, conv_w=conv_w, conv_b=conv_b, ssm_A_log=ssm_A_log,
             ssm_dt_bias=ssm_dt_bias, ssm_D=ssm_D, ssm_norm_w=ssm_norm_w, ln_g=ln_g, ln_b=ln_b,
             ffn_w1=ffn_w1.astype(BF16), ffn_w3=ffn_w3.astype(BF16), ffn_w2=ffn_w2.astype(BF16),
             router_w=router_w, moe_w13=_pair_ff_tiles(moe_w1, moe_w3),
             moe_w2=moe_w2.astype(BF16))

    cvec = jnp.zeros((8, D_MODEL), F32).at[0].set(c_ctx).at[1:1 + dbsz].set(c)
    mod = _modulation(cvec, w_ada, b_ada)

    def mods_for(l, lo, hi):
        return [mod[l, lo:hi, i * D_MODEL:(i + 1) * D_MODEL][:, None, :] for i in range(6)]

    y_prompt = x_prompt.reshape(bsz * seq, D_MODEL)
    caches = None
    for l in range(DEPTH):
        y_prompt, caches = _layer(y_prompt, mods_for(l, 0, 1), P, l, bsz, seq, None, caches)
    new_k, new_v, new_c, new_n, m_spread, new_s = caches

    y_sample = x_sample.reshape(dbsz * dseq, D_MODEL)
    for l in range(DEPTH):
        ctx = (cache_attn_k[:, l], cache_attn_v[:, l], state_mlstm_C[:, l], state_mlstm_n[:, l],
               state_mlstm_m[:, l], state_ssm[:, l])
        y_sample, _ = _layer(y_sample, mods_for(l, 1, 1 + dbsz), P, l, dbsz, dseq, ctx)

    return (y_prompt.reshape(bsz, seq, D_MODEL), y_sample.reshape(dbsz, dseq, D_MODEL),
            new_k, new_v, new_c, new_n, m_spread[:, :, :, 0, ::D_STATE], new_s)
```

```python
import functools
import math

import jax
import jax.numpy as jnp
from jax import lax
from jax.experimental import pallas as pl
from jax.experimental.pallas import tpu as pltpu

F32 = jnp.float32
BF16 = jnp.bfloat16
HIGHEST = lax.Precision.HIGHEST

D_MODEL = 1024
DEPTH = 2
GRID_W = 64
N_HEADS = 4
D_ATT = 512
D_HEAD_V = 128
D_QK = 64
D_REC = 256
D_STATE = 64
N_GROUPS = 2
D_CONV = 3
D_FF = 2816
N_EXPERTS = 8
ALPHA = (2.0 * DEPTH) ** 0.25
CHUNK = 64
ROPE_BASE = 10000.0
LOG2E = 1.4426950408889634
EPS = 1e-5

COL_AQ, COL_AK, COL_AV = 0, 512, 1024
COL_MQ, COL_MK, COL_MV, COL_MO = 1536, 1792, 2048, 2304
COL_SX, COL_SBC, COL_SZ = 2560, 2816, 3072
COL_SMALL = 3328
U_COLS = 3584
ORIG_GATES, ORIG_SZ, ORIG_SX, ORIG_DT, ORIG_END = 2560, 2576, 2832, 3344, 3352
GATE_I, GATE_F, GATE_DT = 0, 8, 16

LANES = 128
ROW_TILE = 512
SCAN_ROWS = 512
SCAN_BATCH = 2
VMEM_LIMIT = 48 * 1024 * 1024

NT_DIMS = (((1,), (1,)), ((), ()))
TN_DIMS = (((0,), (0,)), ((), ()))


def _params(sem, vmem=VMEM_LIMIT):
    return pltpu.CompilerParams(dimension_semantics=sem, vmem_limit_bytes=vmem)


def _silu(x):
    return x * jax.nn.sigmoid(x)


def _bdot(a, b):
    return jnp.dot(a.astype(BF16), b.astype(BF16), preferred_element_type=F32)


def _bdot_nt(a, b):
    return lax.dot_general(a.astype(BF16), b.astype(BF16), NT_DIMS, preferred_element_type=F32)


def _layernorm_rows(y, g, b):
    mu = jnp.mean(y, -1, keepdims=True)
    d = y - mu
    var = jnp.mean(d * d, -1, keepdims=True)
    return d * lax.rsqrt(var + EPS) * g + b


def _mod_kernel(c_ref, w_ref, b_ref, o_ref):
    o_ref[0] = jnp.dot(_silu(c_ref[...]), w_ref[0], precision=HIGHEST,
                       preferred_element_type=F32) + b_ref[0]


def _modulation(cvec, w_ada, b_ada):
    tn = 1536
    return pl.pallas_call(
        _mod_kernel,
        grid=(DEPTH, 6 * D_MODEL // tn),
        in_specs=[pl.BlockSpec((8, D_MODEL), lambda l, j: (0, 0)),
                  pl.BlockSpec((1, D_MODEL, tn), lambda l, j: (l, 0, j)),
                  pl.BlockSpec((1, 1, tn), lambda l, j: (l, 0, j))],
        out_specs=pl.BlockSpec((1, 8, tn), lambda l, j: (l, 0, j)),
        out_shape=jax.ShapeDtypeStruct((DEPTH, 8, 6 * D_MODEL), F32),
        compiler_params=_params(("parallel", "parallel")),
        name="modulation",
    )(cvec, w_ada, b_ada.reshape(DEPTH, 1, 6 * D_MODEL))


CONV_COLS = 2 * D_REC
HALO = 16


def _inproj_kernel(x_ref, prev_ref, next_ref, sc_ref, sh_ref, w_ref, cw_ref, cb_ref, o_ref, *, seq):
    def modulate(x):
        return (x * (1.0 + sc_ref[0]) + sh_ref[0]).astype(BF16)

    h = modulate(x_ref[...])
    for n0 in range(0, U_COLS, CONV_COLS):
        if n0 != COL_SX:
            o_ref[:, n0:n0 + CONV_COLS] = jnp.dot(h, w_ref[:, n0:n0 + CONV_COLS], preferred_element_type=F32)
            continue
        h_ext = jnp.concatenate([h, modulate(prev_ref[...]), modulate(next_ref[...])], axis=0)
        pre_ext = jnp.dot(h_ext, w_ref[:, n0:n0 + CONV_COLS], preferred_element_type=F32)
        pre = pre_ext[:ROW_TILE]
        before = pre_ext[ROW_TILE + HALO - 1:ROW_TILE + HALO]
        after = pre_ext[ROW_TILE + HALO:ROW_TILE + HALO + 1]
        r = lax.broadcasted_iota(jnp.int32, (ROW_TILE, 1), 0)
        pos = (r + pl.program_id(0) * ROW_TILE) % seq
        up = jnp.where(r == 0, before, pltpu.roll(pre, 1, 0))
        down = jnp.where(r == ROW_TILE - 1, after, pltpu.roll(pre, ROW_TILE - 1, 0))
        up = jnp.where(pos == 0, 0.0, up)
        down = jnp.where(pos == seq - 1, 0.0, down)
        cw = cw_ref[...]
        o_ref[:, n0:n0 + CONV_COLS] = _silu(up * cw[0:1] + pre * cw[1:2] + down * cw[2:3] + cb_ref[...])


def _in_proj(x, sc, sh, w, conv_w, conv_b, rows_per_mod, seq):
    n = x.shape[0]
    tpb = rows_per_mod // ROW_TILE
    r8 = ROW_TILE // HALO
    last8 = n // HALO - 1
    assert COL_SX % CONV_COLS == 0 and U_COLS % CONV_COLS == 0
    return pl.pallas_call(
        functools.partial(_inproj_kernel, seq=seq),
        grid=(n // ROW_TILE,),
        in_specs=[pl.BlockSpec((ROW_TILE, D_MODEL), lambda i: (i, 0)),
                  pl.BlockSpec((HALO, D_MODEL), lambda i: (jnp.maximum(i * r8 - 1, 0), 0)),
                  pl.BlockSpec((HALO, D_MODEL), lambda i: (jnp.minimum((i + 1) * r8, last8), 0)),
                  pl.BlockSpec((1, 1, D_MODEL), lambda i: (i // tpb, 0, 0)),
                  pl.BlockSpec((1, 1, D_MODEL), lambda i: (i // tpb, 0, 0)),
                  pl.BlockSpec((D_MODEL, U_COLS), lambda i: (0, 0)),
                  pl.BlockSpec((D_CONV, CONV_COLS), lambda i: (0, 0)),
                  pl.BlockSpec((1, CONV_COLS), lambda i: (0, 0))],
        out_specs=pl.BlockSpec((ROW_TILE, U_COLS), lambda i: (i, 0)),
        out_shape=jax.ShapeDtypeStruct((n, U_COLS), F32),
        compiler_params=_params(("parallel",)),
        name="in_proj",
    )(x, x, x, sc, sh, w, conv_w, conv_b.reshape(1, CONV_COLS))


def _lambda_scalar(lam_ref, lam_init):
    lp = lam_ref[...]
    s01 = jnp.sum(lp[0:1] * lp[1:2], axis=-1, keepdims=True)
    s23 = jnp.sum(lp[2:3] * lp[3:4], axis=-1, keepdims=True)
    return jnp.exp(s01) - jnp.exp(s23) + lam_init


def _softmax_rows(s):
    e = jnp.exp(s - jnp.max(s, -1, keepdims=True))
    return e / jnp.sum(e, -1, keepdims=True)


def _head_norm(o, nw, lam_init):
    return o * lax.rsqrt(jnp.mean(o * o, -1, keepdims=True) + EPS) * nw * (1.0 - lam_init)


def _layer_cache(tail, l, bsz, nb=1):
    zeros = (0,) * len(tail)
    owned = DEPTH if l == 0 else 1
    spec = pl.BlockSpec((nb, owned) + tuple(tail), lambda b, *_: (b, l) + zeros)
    return jax.ShapeDtypeStruct((bsz, DEPTH) + tuple(tail), F32), spec, owned


def _zero_later_layers(ref):
    for b in range(ref.shape[0]):
        for later in range(1, ref.shape[1]):
            ref[b, later] = jnp.zeros(ref.shape[2:], ref.dtype)


def _attn_ctx_kernel(u_ref, lam_ref, nw_ref, *rest, lam_init):
    att_ref, k_ref, v_ref = rest[-3:]
    lam = _lambda_scalar(lam_ref, lam_init)
    for h in range(N_HEADS):
        v = u_ref[:, COL_AV + h * D_HEAD_V:COL_AV + (h + 1) * D_HEAD_V]
        v_ref[0, 0, h] = v
        ps = []
        for m in range(2):
            c0 = h * D_HEAD_V + m * D_QK
            q = u_ref[:, COL_AQ + c0:COL_AQ + c0 + D_QK] * (D_QK ** -0.5)
            k = u_ref[:, COL_AK + c0:COL_AK + c0 + D_QK]
            k_ref[0, 0, h, m] = k
            ps.append(_softmax_rows(_bdot_nt(q, k)))
        o = _bdot(ps[0] - lam * ps[1], v)
        att_ref[:, h * D_HEAD_V:(h + 1) * D_HEAD_V] = _head_norm(o, nw_ref[...], lam_init).astype(BF16)
    _zero_later_layers(k_ref)
    _zero_later_layers(v_ref)


def _attention_ctx(u, lam_p, norm_w, lam_init, bsz, t, l, caches):
    n = bsz * t
    k_shape, k_spec, _ = _layer_cache((N_HEADS, 2, t, D_QK), l, bsz)
    v_shape, v_spec, _ = _layer_cache((N_HEADS, t, D_HEAD_V), l, bsz)
    in_specs = [pl.BlockSpec((t, 3 * D_ATT), lambda b: (b, 0)),
                pl.BlockSpec((4, D_QK), lambda b: (0, 0)),
                pl.BlockSpec((1, D_HEAD_V), lambda b: (0, 0))]
    args = [u, lam_p, norm_w.reshape(1, D_HEAD_V)]
    aliases = {}
    if caches is not None:
        in_specs += [pl.BlockSpec(memory_space=pl.ANY)] * 2
        aliases = {len(args): 1, len(args) + 1: 2}
        args += list(caches)
    return pl.pallas_call(
        functools.partial(_attn_ctx_kernel, lam_init=lam_init),
        grid=(bsz,),
        in_specs=in_specs,
        out_specs=[pl.BlockSpec((t, D_ATT), lambda b: (b, 0)), k_spec, v_spec],
        out_shape=[jax.ShapeDtypeStruct((n, D_ATT), BF16), k_shape, v_shape],
        input_output_aliases=aliases,
        compiler_params=_params(("parallel",)),
        name="attn_ctx",
    )(*args)


def _rope_kernel(u_ref, cos_ref, sa_ref, sb_ref, q_ref, k_ref, v_ref):
    cos, sa, sb = cos_ref[...], sa_ref[...], sb_ref[...]

    def rope(x):
        return x * cos + pltpu.roll(x, LANES - 16, 1) * sa + pltpu.roll(x, 16, 1) * sb

    for h in range(N_HEADS):
        q = rope(u_ref[:, COL_AQ + h * D_HEAD_V:COL_AQ + (h + 1) * D_HEAD_V]) * (LOG2E * D_QK ** -0.5)
        k = rope(u_ref[:, COL_AK + h * D_HEAD_V:COL_AK + (h + 1) * D_HEAD_V])
        for m in range(2):
            q_ref[0, h, m] = q[:, m * D_QK:(m + 1) * D_QK].astype(BF16)
            k_ref[0, h, m] = k[:, m * D_QK:(m + 1) * D_QK].astype(BF16)
        v_ref[0, h] = u_ref[:, COL_AV + h * D_HEAD_V:COL_AV + (h + 1) * D_HEAD_V].astype(BF16)


def _rope_tables(t):
    rows = jnp.repeat(jnp.arange(t // GRID_W, dtype=F32), GRID_W)
    cols = jnp.tile(jnp.arange(GRID_W, dtype=F32), t // GRID_W)
    half = D_QK // 2
    inv = ROPE_BASE ** (-jnp.arange(0, half, 2, dtype=F32) / half)
    ang_r = rows[:, None] * inv
    ang_c = cols[:, None] * inv
    ang = jnp.concatenate([ang_r, ang_r, ang_c, ang_c], -1)
    cos, sin = jnp.cos(ang), jnp.sin(ang)
    quarter = (jnp.arange(D_QK) // (D_QK // 4)) % 2
    sa = jnp.where(quarter == 0, -sin, 0.0)
    sb = jnp.where(quarter == 1, sin, 0.0)
    tile2 = lambda a: jnp.concatenate([a, a], -1)
    return tile2(cos), tile2(sa), tile2(sb)


def _rope_prep(u, bsz, t):
    tr = 512
    nb = t // tr
    cos, sa, sb = _rope_tables(t)
    tab = pl.BlockSpec((tr, LANES), lambda b, i: (i, 0))
    return pl.pallas_call(
        _rope_kernel,
        grid=(bsz, nb),
        in_specs=[pl.BlockSpec((tr, 3 * D_ATT), lambda b, i: (b * nb + i, 0)), tab, tab, tab],
        out_specs=[pl.BlockSpec((1, N_HEADS, 2, tr, D_QK), lambda b, i: (b, 0, 0, i, 0)),
                   pl.BlockSpec((1, N_HEADS, 2, tr, D_QK), lambda b, i: (b, 0, 0, i, 0)),
                   pl.BlockSpec((1, N_HEADS, tr, D_HEAD_V), lambda b, i: (b, 0, i, 0))],
        out_shape=[jax.ShapeDtypeStruct((bsz, N_HEADS, 2, t, D_QK), BF16),
                   jax.ShapeDtypeStruct((bsz, N_HEADS, 2, t, D_QK), BF16),
                   jax.ShapeDtypeStruct((bsz, N_HEADS, t, D_HEAD_V), BF16)],
        compiler_params=_params(("parallel", "parallel")),
        name="rope_prep",
    )(u, cos, sa, sb)


def _attn_lat_kernel(q_ref, k_ref, v_ref, lam_ref, nw_ref, o_ref, *, lam_init):
    lam = _lambda_scalar(lam_ref, lam_init)
    es, sums = [], []
    for m in range(2):
        s = lax.dot_general(q_ref[0, 0, m], k_ref[0, 0, m], NT_DIMS, preferred_element_type=F32)
        e = jnp.exp2(s - jnp.max(s, -1, keepdims=True))
        es.append(e)
        sums.append(jnp.sum(e, -1, keepdims=True))
    a = es[0] - (lam * sums[0] / sums[1]) * es[1]
    o = _bdot(a, v_ref[0, 0]) / sums[0]
    o_ref[...] = _head_norm(o, nw_ref[...], lam_init).astype(BF16)


def _attention_lat(q, k_all, v_all, lam_p, norm_w, lam_init, bsz, t):
    tq = 256
    nq = t // tq
    s = k_all.shape[3]
    return pl.pallas_call(
        functools.partial(_attn_lat_kernel, lam_init=lam_init),
        grid=(bsz, N_HEADS, nq),
        in_specs=[pl.BlockSpec((1, 1, 2, tq, D_QK), lambda b, h, i: (b, h, 0, i, 0)),
                  pl.BlockSpec((1, 1, 2, s, D_QK), lambda b, h, i: (b, h, 0, 0, 0)),
                  pl.BlockSpec((1, 1, s, D_HEAD_V), lambda b, h, i: (b, h, 0, 0)),
                  pl.BlockSpec((4, D_QK), lambda b, h, i: (0, 0)),
                  pl.BlockSpec((1, D_HEAD_V), lambda b, h, i: (0, 0))],
        out_specs=pl.BlockSpec((tq, D_HEAD_V), lambda b, h, i: (b * nq + i, h)),
        out_shape=jax.ShapeDtypeStruct((bsz * t, D_ATT), BF16),
        compiler_params=_params(("parallel", "parallel", "parallel")),
        name="attn_lat",
    )(q, k_all, v_all, lam_p, norm_w.reshape(1, D_HEAD_V))


HEAD_LANES = N_HEADS * D_STATE


def _scan_consts():
    t = lax.broadcasted_iota(jnp.int32, (CHUNK, HEAD_LANES), 0)
    s = lax.broadcasted_iota(jnp.int32, (CHUNK, HEAD_LANES), 1) & (CHUNK - 1)
    r = lax.broadcasted_iota(jnp.int32, (CHUNK, CHUNK), 0)
    c = lax.broadcasted_iota(jnp.int32, (CHUNK, CHUNK), 1)
    reads = (jnp.where(s <= t, 1.0, 0.0), jnp.where(s >= t, 1.0, 0.0))
    block = (jnp.where(s <= t, 0.0, -jnp.inf), jnp.where(s >= t, 0.0, -jnp.inf))
    tri = (jnp.where(c <= r, 1.0, 0.0).astype(BF16), jnp.where(c >= r, 1.0, 0.0).astype(BF16))
    return reads, block, tri, jnp.where(s == t, 1.0, 0.0)


def _cumsum_rows(tri, x):
    hi = x.astype(BF16)
    rest = x - hi.astype(F32)
    mid = rest.astype(BF16)
    lo = (rest - mid.astype(F32)).astype(BF16)
    parts = jnp.dot(tri, jnp.concatenate([hi, mid, lo], axis=1), preferred_element_type=F32)
    return parts[:, :HEAD_LANES] + parts[:, HEAD_LANES:2 * HEAD_LANES] + parts[:, 2 * HEAD_LANES:]


def _group_mask(rows, cols, row_shift, col_shift, dtype):
    r = lax.broadcasted_iota(jnp.int32, (rows, cols), 0) >> 6
    c = (lax.broadcasted_iota(jnp.int32, (rows, cols), 1) >> 6) & (N_HEADS - 1)
    return jnp.where((r >> row_shift) == (c >> col_shift), 1.0, 0.0).astype(dtype)


def _spread(x, chans):
    return jnp.concatenate([jnp.broadcast_to(x[:, c:c + 1], (CHUNK, D_STATE)) for c in chans], axis=1)


def _stack_heads(x):
    return jnp.concatenate([x] * N_HEADS, axis=0)


def _mlstm_chunk(q4, k4, v4, g, d, cn_prev, m_prev, reads, block, tri, eye, bd, bd_f32):
    li = _spread(g, [GATE_I + d * N_HEADS + h for h in range(N_HEADS)])
    lf = _spread(jax.nn.log_sigmoid(g), [GATE_F + d * N_HEADS + h for h in range(N_HEADS)])
    bc = _cumsum_rows(tri[d], lf)
    btot = jnp.sum(lf, 0, keepdims=True)
    b_row = jnp.sum(reads[1 - d] * lf, 0, keepdims=True)
    li_row = jnp.sum(eye * li, 0, keepdims=True)
    dm = bc - b_row + li_row + block[d]
    rmax = jnp.concatenate(
        [jnp.broadcast_to(jnp.max(dm[:, h * D_STATE:(h + 1) * D_STATE], -1, keepdims=True), (CHUNK, D_STATE))
         for h in range(N_HEADS)], axis=1)
    inter = bc + m_prev
    m_t = jnp.maximum(inter, rmax)
    w_inter = jnp.exp(inter - m_t)
    qs = (q4 * (D_STATE ** -0.5)).astype(BF16)
    kbd = bd[:, :HEAD_LANES] * _stack_heads(k4.astype(BF16))
    s4 = lax.dot_general(qs, kbd, NT_DIMS, preferred_element_type=F32) * jnp.exp(dm - m_t)
    vo = jnp.concatenate([v4.astype(BF16), jnp.ones((CHUNK, HEAD_LANES), BF16)], axis=1)
    vbd = bd * _stack_heads(vo)
    nd = (jnp.concatenate([w_inter, w_inter], axis=1)
          * jnp.dot(qs, bd * _stack_heads(cn_prev.astype(BF16)), preferred_element_type=F32)
          + jnp.dot(s4.astype(BF16), vbd, preferred_element_type=F32))
    hc = nd[:, :HEAD_LANES] / jnp.maximum(jnp.abs(nd[:, HEAD_LANES:]), jnp.exp(-m_t))
    gcol = btot - bc + li
    m_new = jnp.maximum(btot + m_prev, jnp.max(gcol, 0, keepdims=True))
    w_c = jnp.exp(btot + m_prev - m_new)
    kw = (k4 * jnp.exp(gcol - m_new)).astype(BF16)
    dcn = lax.dot_general(kw, vo, TN_DIMS, preferred_element_type=F32)
    own = sum(bd_f32[h * D_STATE:(h + 1) * D_STATE] * dcn[h * D_STATE:(h + 1) * D_STATE]
              for h in range(N_HEADS))
    cn_new = jnp.concatenate([w_c, w_c], axis=1) * cn_prev + own
    return hc, cn_new, m_new


def _mlstm_kernel(*refs, nblk, rows, zero_init, n_alias):
    qf_ref, kf_ref, vf_ref, gf_ref, qb_ref, kb_ref, vb_ref, gb_ref, bias_ref = refs[:9]
    if not zero_init:
        c0_ref, m0_ref = refs[9:11]
    outs = refs[9 + (0 if zero_init else 2) + n_alias:-2]
    hf_ref, hb_ref = outs[:2]
    cn_scr, m_scr = refs[-2:]
    j = pl.program_id(1)
    nchunk = rows // CHUNK

    @pl.when(j == 0)
    def _():
        if zero_init:
            cn_scr[...] = jnp.zeros_like(cn_scr)
            m_scr[...] = jnp.zeros_like(m_scr)
        else:
            cn_scr[...] = c0_ref[...]
            m_scr[...] = m0_ref[...]

    reads, block, tri, eye = _scan_consts()
    bd = _group_mask(HEAD_LANES, 2 * HEAD_LANES, 0, 0, BF16)
    bd_f32 = _group_mask(HEAD_LANES, 2 * HEAD_LANES, 0, 0, F32)

    chains = [(b, d) for b in range(SCAN_BATCH) for d in range(2)]
    state = {bd_: (cn_scr[bd_], m_scr[bd_]) for bd_ in chains}
    for ci in range(nchunk):
        for b, d in chains:
            q_ref, k_ref, v_ref, g_ref, h_ref = ((qf_ref, kf_ref, vf_ref, gf_ref, hf_ref) if d == 0
                                                 else (qb_ref, kb_ref, vb_ref, gb_ref, hb_ref))
            cj = ci if d == 0 else nchunk - 1 - ci
            rs = slice(cj * CHUNK, (cj + 1) * CHUNK)
            g = g_ref[b, rs, :] + bias_ref[...]
            hc, cn_new, m_new = _mlstm_chunk(q_ref[b, rs, :], k_ref[b, rs, :], v_ref[b, rs, :], g, d,
                                             *state[b, d], reads, block, tri, eye, bd, bd_f32)
            h_ref[b, rs, :] = hc
            state[b, d] = (cn_new, m_new)
    for bd_ in chains:
        cn_scr[bd_], m_scr[bd_] = state[bd_]

    if zero_init:
        cout_ref, nout_ref, mout_ref = outs[2:]

        @pl.when(j == nblk - 1)
        def _():
            r = lax.broadcasted_iota(jnp.int32, (D_STATE, D_STATE), 0)
            c = lax.broadcasted_iota(jnp.int32, (D_STATE, D_STATE), 1)
            for b, d in chains:
                for h in range(N_HEADS):
                    r0, r1 = h * D_STATE, (h + 1) * D_STATE
                    cout_ref[b, 0, d, h] = cn_scr[b, d, :, r0:r1]
                    n_spread = cn_scr[b, d, :, HEAD_LANES + r0:HEAD_LANES + r1]
                    nout_ref[b, 0, d, h:h + 1, :] = jnp.sum(jnp.where(r == c, n_spread, 0.0), 0, keepdims=True)
                mout_ref[b, 0, d] = m_scr[b, d]
            _zero_later_layers(cout_ref)
            _zero_later_layers(nout_ref)
            _zero_later_layers(mout_ref)


def _mlstm(u, gate_bias, bsz, t, init, l=0, caches=None):
    rows = min(t, SCAN_ROWS)
    nblk = t // rows
    zero_init = init is None

    nb = SCAN_BATCH
    fwd = lambda col: (lambda p, j: (p, j, col))
    bwd = lambda col: (lambda p, j: (p, nblk - 1 - j, col))
    cq, ck, cv, cg = COL_MQ // D_REC, COL_MK // D_REC, COL_MV // D_REC, COL_SMALL // LANES
    in_specs = []
    for mk in (fwd, bwd):
        in_specs += [pl.BlockSpec((nb, rows, D_REC), mk(cq)), pl.BlockSpec((nb, rows, D_REC), mk(ck)),
                     pl.BlockSpec((nb, rows, D_REC), mk(cv)), pl.BlockSpec((nb, rows, LANES), mk(cg))]
    in_specs.append(pl.BlockSpec((1, LANES), lambda p, j: (0, 0)))
    args = [u.reshape(bsz, t, U_COLS)] * 8 + [gate_bias]
    state_c = pl.BlockSpec((nb, 2, D_STATE, 2 * HEAD_LANES), lambda p, j: (p, 0, 0, 0))
    state_m = pl.BlockSpec((nb, 2, 1, HEAD_LANES), lambda p, j: (p, 0, 0, 0))
    out_specs = [pl.BlockSpec((nb, rows, D_REC), fwd(0)), pl.BlockSpec((nb, rows, D_REC), bwd(0))]
    out_shape = [jax.ShapeDtypeStruct((bsz, t, D_REC), F32), jax.ShapeDtypeStruct((bsz, t, D_REC), F32)]
    aliases = {}
    if zero_init:
        for tail in ((2, N_HEADS, D_STATE, D_STATE), (2, N_HEADS, D_STATE), (2, 1, HEAD_LANES)):
            shape, spec, _ = _layer_cache(tail, l, bsz, nb)
            out_shape.append(shape)
            out_specs.append(spec)
        if caches is not None:
            in_specs += [pl.BlockSpec(memory_space=pl.ANY)] * len(caches)
            aliases = {len(args) + i: 2 + i for i in range(len(caches))}
            args += list(caches)
    else:
        in_specs += [state_c, state_m]
        args += list(init)
    hf, hb, *state_out = pl.pallas_call(
        functools.partial(_mlstm_kernel, nblk=nblk, rows=rows, zero_init=zero_init, n_alias=len(aliases)),
        grid=(bsz // nb, nblk),
        in_specs=in_specs,
        out_specs=out_specs,
        out_shape=out_shape,
        input_output_aliases=aliases,
        scratch_shapes=[pltpu.VMEM((nb, 2, D_STATE, 2 * HEAD_LANES), F32),
                        pltpu.VMEM((nb, 2, 1, HEAD_LANES), F32)],
        compiler_params=_params(("parallel", "arbitrary")),
        name="mlstm_scan",
    )(*args)
    return (hf.reshape(bsz * t, D_REC), hb.reshape(bsz * t, D_REC), *state_out)


def _ssd_chunk(x4, bcm, dt128, da128, d, sg_prev, reads, block, tri, b_sel, s_sel, bd):
    chans = [GATE_DT + d * N_HEADS + h for h in range(N_HEADS)]
    dt = _spread(dt128, chans)
    da = _spread(da128, chans)
    ac = _cumsum_rows(tri[d], da)
    atot = jnp.sum(da, 0, keepdims=True)
    a_row = jnp.sum(reads[1 - d] * da, 0, keepdims=True)
    decay = jnp.exp(ac - a_row + block[d])
    bmat = bcm[:, :LANES].astype(BF16)
    cmat = bcm[:, LANES:].astype(BF16)
    bbd = b_sel * _stack_heads(bmat)
    g4 = lax.dot_general(cmat, bbd, NT_DIMS, preferred_element_type=F32)
    xbd = bd * _stack_heads((x4 * dt).astype(BF16))
    y = (jnp.dot((g4 * decay).astype(BF16), xbd, preferred_element_type=F32)
         + jnp.dot(cmat, sg_prev.astype(BF16), preferred_element_type=F32) * jnp.exp(ac))
    w = jnp.exp(atot - ac) * dt
    dsg = lax.dot_general(bmat, (x4 * w).astype(BF16), TN_DIMS, preferred_element_type=F32)
    sg_new = jnp.exp(atot) * sg_prev + s_sel * dsg
    return y, sg_new


def _ssd_kernel(*refs, nblk, rows, zero_init, n_alias):
    xf_ref, bcf_ref, gf_ref, xb_ref, bcb_ref, gb_ref, dtb_ref, alog_ref, dskip_ref = refs[:9]
    if not zero_init:
        s0_ref = refs[9]
    outs = refs[9 + (0 if zero_init else 1) + n_alias:-1]
    yf_ref, yb_ref = outs[:2]
    s_scr = refs[-1]
    j = pl.program_id(1)
    nchunk = rows // CHUNK

    @pl.when(j == 0)
    def _():
        if zero_init:
            s_scr[...] = jnp.zeros_like(s_scr)
        else:
            s_scr[...] = s0_ref[0]

    reads, block, tri, _ = _scan_consts()
    bd = _group_mask(HEAD_LANES, HEAD_LANES, 0, 0, BF16)
    b_sel = _group_mask(HEAD_LANES, LANES, 1, 0, BF16)
    s_sel = _group_mask(LANES, HEAD_LANES, 0, 1, F32)
    a_coef = -jnp.exp(alog_ref[...])

    state = [s_scr[d] for d in range(2)]
    for ci in range(nchunk):
        for d in range(2):
            x_ref, bc_ref, g_ref, y_ref = ((xf_ref, bcf_ref, gf_ref, yf_ref) if d == 0
                                           else (xb_ref, bcb_ref, gb_ref, yb_ref))
            cj = ci if d == 0 else nchunk - 1 - ci
            rs = slice(cj * CHUNK, (cj + 1) * CHUNK)
            dt128 = jax.nn.softplus(g_ref[rs, :] + dtb_ref[...])
            x4 = x_ref[rs, :]
            y, state[d] = _ssd_chunk(x4, bc_ref[rs, :], dt128, dt128 * a_coef, d, state[d],
                                     reads, block, tri, b_sel, s_sel, bd)
            if d == 0:
                y = y + dskip_ref[...] * x4
            y_ref[rs, :] = y
    for d in range(2):
        s_scr[d] = state[d]

    if zero_init:
        sout_ref = outs[2]

        @pl.when(j == nblk - 1)
        def _():
            for d in range(2):
                s_t = s_scr[d].T
                for h in range(N_HEADS):
                    g0 = (h // 2) * D_STATE
                    sout_ref[0, 0, d, h] = s_t[h * D_STATE:(h + 1) * D_STATE, g0:g0 + D_STATE]
            _zero_later_layers(sout_ref)


def _ssd(u, dt_bias_row, alog_row, dskip_row, bsz, t, init, l=0, cache=None):
    rows = min(t, SCAN_ROWS)
    nblk = t // rows
    zero_init = init is None

    def fwd(col):
        return lambda b, j: (b * nblk + j, col)

    def bwd(col):
        return lambda b, j: (b * nblk + nblk - 1 - j, col)

    in_specs = []
    for mk in (fwd, bwd):
        in_specs += [pl.BlockSpec((rows, D_REC), mk(COL_SX // D_REC)),
                     pl.BlockSpec((rows, D_REC), mk(COL_SBC // D_REC)),
                     pl.BlockSpec((rows, LANES), mk(COL_SMALL // LANES))]
    in_specs += [pl.BlockSpec((1, LANES), lambda b, j: (0, 0)),
                 pl.BlockSpec((1, LANES), lambda b, j: (0, 0)),
                 pl.BlockSpec((1, D_REC), lambda b, j: (0, 0))]
    args = [u] * 6 + [dt_bias_row, alog_row, dskip_row]
    out_specs = [pl.BlockSpec((rows, D_REC), fwd(0)), pl.BlockSpec((rows, D_REC), bwd(0))]
    out_shape = [jax.ShapeDtypeStruct((bsz * t, D_REC), F32), jax.ShapeDtypeStruct((bsz * t, D_REC), F32)]
    aliases = {}
    if zero_init:
        shape, spec, _ = _layer_cache((2, N_HEADS, D_STATE, D_STATE), l, bsz)
        out_shape.append(shape)
        out_specs.append(spec)
        if cache is not None:
            in_specs.append(pl.BlockSpec(memory_space=pl.ANY))
            aliases = {len(args): 2}
            args.append(cache)
    else:
        in_specs.append(pl.BlockSpec((1, 2, LANES, HEAD_LANES), lambda b, j: (b, 0, 0, 0)))
        args.append(init)
    return pl.pallas_call(
        functools.partial(_ssd_kernel, nblk=nblk, rows=rows, zero_init=zero_init, n_alias=len(aliases)),
        grid=(bsz, nblk),
        in_specs=in_specs,
        out_specs=out_specs,
        out_shape=out_shape,
        input_output_aliases=aliases,
        scratch_shapes=[pltpu.VMEM((2, LANES, HEAD_LANES), F32)],
        compiler_params=_params(("parallel", "arbitrary")),
        name="ssd_scan",
    )(*args)


def _outproj_kernel(att_ref, hf_ref, hb_ref, mo_ref, yf_ref, yb_ref, z_ref, w_ref, x_ref, gate_ref,
                    mnw_ref, snw_ref, lg_ref, lb_ref, o_ref):
    hh = hf_ref[...] + hb_ref[...]
    parts = []
    for h in range(N_HEADS):
        xh = hh[:, h * D_STATE:(h + 1) * D_STATE]
        mu = jnp.mean(xh, -1, keepdims=True)
        dlt = xh - mu
        var = jnp.mean(dlt * dlt, -1, keepdims=True)
        parts.append(dlt * lax.rsqrt(var + EPS))
    ml = jax.nn.sigmoid(mo_ref[...]) * jnp.concatenate(parts, axis=1) * mnw_ref[...]
    yz = (yf_ref[...] + yb_ref[...]) * _silu(z_ref[...])
    parts = []
    for grp in range(N_GROUPS):
        yg = yz[:, grp * LANES:(grp + 1) * LANES]
        parts.append(yg * lax.rsqrt(jnp.mean(yg * yg, -1, keepdims=True) + EPS))
    ssm = jnp.concatenate(parts, axis=1) * snw_ref[...]
    mixed = (jnp.dot(att_ref[...], w_ref[0:D_ATT], preferred_element_type=F32)
             + _bdot(ml, w_ref[D_ATT:D_ATT + D_REC])
             + _bdot(ssm, w_ref[D_ATT + D_REC:D_MODEL]))
    y = ALPHA * x_ref[...] + gate_ref[0] * mixed
    o_ref[...] = _layernorm_rows(y, lg_ref[...], lb_ref[...])


def _out_proj(att, hf, hb, yf, yb, u, w, x, gate, mnw, snw, lg, lb, rows_per_mod):
    n = x.shape[0]
    tpb = rows_per_mod // ROW_TILE
    row = lambda width, col: pl.BlockSpec((ROW_TILE, width), lambda i: (i, col))
    vec = lambda width: pl.BlockSpec((1, width), lambda i: (0, 0))
    return pl.pallas_call(
        _outproj_kernel,
        grid=(n // ROW_TILE,),
        in_specs=[row(D_ATT, 0), row(D_REC, 0), row(D_REC, 0), row(D_REC, COL_MO // D_REC),
                  row(D_REC, 0), row(D_REC, 0), row(D_REC, COL_SZ // D_REC),
                  pl.BlockSpec((D_MODEL, D_MODEL), lambda i: (0, 0)),
                  row(D_MODEL, 0),
                  pl.BlockSpec((1, 1, D_MODEL), lambda i: (i // tpb, 0, 0)),
                  vec(D_REC), vec(D_REC), vec(D_MODEL), vec(D_MODEL)],
        out_specs=row(D_MODEL, 0),
        out_shape=jax.ShapeDtypeStruct((n, D_MODEL), F32),
        compiler_params=_params(("parallel",)),
        name="out_proj",
    )(att, hf, hb, u, yf, yb, u, w, x, gate, mnw.reshape(1, D_REC), snw.reshape(1, D_REC),
      lg.reshape(1, D_MODEL), lb.reshape(1, D_MODEL))


FF_TILE = D_FF // 2
FFN_ROWS = 1024
FFN_SUB = 512
FFN_VMEM_LIMIT = 56 * 1024 * 1024


def _swiglu_partial(h, w1, w3, w2):
    a = jnp.dot(h, w1, preferred_element_type=F32)
    b = jnp.dot(h, w3, preferred_element_type=F32)
    return jnp.dot((_silu(a) * b).astype(BF16), w2, preferred_element_type=F32)


def _ffn_kernel(x_ref, sc_ref, sh_ref, gate_ref, w1_ref, w3_ref, w2_ref, lg_ref, lb_ref, o_ref):
    for r0 in range(0, FFN_ROWS, FFN_SUB):
        rows = slice(r0, r0 + FFN_SUB)
        x = x_ref[rows, :]
        h = (x * (1.0 + sc_ref[0]) + sh_ref[0]).astype(BF16)
        y = ALPHA * x + gate_ref[0] * _swiglu_partial(h, w1_ref[...], w3_ref[...], w2_ref[...])
        o_ref[rows, :] = _layernorm_rows(y, lg_ref[...], lb_ref[...])


def _ffn(x, sc, sh, gate, w1, w3, w2, lg, lb, rows_per_mod):
    n = x.shape[0]
    tpb = rows_per_mod // FFN_ROWS
    modspec = pl.BlockSpec((1, 1, D_MODEL), lambda i: (i // tpb, 0, 0))
    vec = pl.BlockSpec((1, D_MODEL), lambda i: (0, 0))
    once = pl.Buffered(1)
    return pl.pallas_call(
        _ffn_kernel,
        grid=(n // FFN_ROWS,),
        in_specs=[pl.BlockSpec((FFN_ROWS, D_MODEL), lambda i: (i, 0)), modspec, modspec, modspec,
                  pl.BlockSpec((D_MODEL, D_FF), lambda i: (0, 0), pipeline_mode=once),
                  pl.BlockSpec((D_MODEL, D_FF), lambda i: (0, 0), pipeline_mode=once),
                  pl.BlockSpec((D_FF, D_MODEL), lambda i: (0, 0), pipeline_mode=once), vec, vec],
        out_specs=pl.BlockSpec((FFN_ROWS, D_MODEL), lambda i: (i, 0)),
        out_shape=jax.ShapeDtypeStruct((n, D_MODEL), F32),
        compiler_params=_params(("parallel",), FFN_VMEM_LIMIT),
        name="ffn_dense",
    )(x, sc, sh, gate, w1, w3, w2, lg.reshape(1, D_MODEL), lb.reshape(1, D_MODEL))


def _router_kernel(x_ref, sc_ref, sh_ref, rw_ref, gates_ref, h_ref):
    h = x_ref[...] * (1.0 + sc_ref[0]) + sh_ref[0]
    h_ref[...] = h.astype(BF16)
    logits = jnp.dot(h, rw_ref[...], precision=HIGHEST, preferred_element_type=F32)
    lane = lax.broadcasted_iota(jnp.int32, logits.shape, 1)
    valid = lane < N_EXPERTS
    p = jnp.where(valid, _softmax_rows(jnp.where(valid, logits, -jnp.inf)), -2.0)
    p1 = jnp.max(p, -1, keepdims=True)
    i1 = jnp.min(jnp.where(p == p1, lane, LANES), -1, keepdims=True)
    rest = jnp.where(lane == i1, -1.0, p)
    p2 = jnp.max(rest, -1, keepdims=True)
    i2 = jnp.min(jnp.where(rest == p2, lane, LANES), -1, keepdims=True)
    tot = p1 + p2
    gates_ref[...] = jnp.where(lane == i1, p1 / tot, jnp.where(lane == i2, p2 / tot, 0.0))


def _router(x, sc, sh, router_w, rows_per_mod):
    n = x.shape[0]
    tpb = rows_per_mod // ROW_TILE
    modspec = pl.BlockSpec((1, 1, D_MODEL), lambda i: (i // tpb, 0, 0))
    rw = jnp.pad(router_w, ((0, 0), (0, LANES - N_EXPERTS)))
    return pl.pallas_call(
        _router_kernel,
        grid=(n // ROW_TILE,),
        in_specs=[pl.BlockSpec((ROW_TILE, D_MODEL), lambda i: (i, 0)), modspec, modspec,
                  pl.BlockSpec((D_MODEL, LANES), lambda i: (0, 0))],
        out_specs=[pl.BlockSpec((ROW_TILE, LANES), lambda i: (i, 0)),
                   pl.BlockSpec((ROW_TILE, D_MODEL), lambda i: (i, 0))],
        out_shape=[jax.ShapeDtypeStruct((n, LANES), F32), jax.ShapeDtypeStruct((n, D_MODEL), BF16)],
        compiler_params=_params(("parallel",)),
        name="router",
    )(x, sc, sh, rw)


MOE_ROWS = 1024
MOE_TILE = 128
MOE_MAX_TILES = MOE_ROWS // MOE_TILE
MOE_VMEM_LIMIT = 56 * 1024 * 1024


def _moe_kernel(h_ref, gates_ref, w1_ref, w3_ref, w2_ref, x_ref, gate_ref, lg_ref, lb_ref, o_ref,
                slot_scr, slott_scr, hs_scr, ys_scr):
    e = pl.program_id(1)
    j = pl.program_id(2)
    last_j = pl.num_programs(2) - 1

    @pl.when(jnp.logical_and(e == 0, j == 0))
    def _():
        r = lax.broadcasted_iota(jnp.int32, (MOE_ROWS, MOE_ROWS), 0)
        c = lax.broadcasted_iota(jnp.int32, (MOE_ROWS, MOE_ROWS), 1)
        before = jnp.where(c < r, 1.0, 0.0).astype(BF16)
        mask = gates_ref[...] != 0.0
        rank = jnp.dot(before, jnp.where(mask, 1.0, 0.0).astype(BF16), preferred_element_type=F32)
        slot = jnp.where(mask, rank, -1.0).astype(jnp.int32)
        slot_scr[...] = slot
        slott_scr[...] = slot.T
        o_ref[...] = jnp.zeros_like(o_ref)

    lane = lax.broadcasted_iota(jnp.int32, (MOE_ROWS, LANES), 1)
    slot_col = jnp.max(jnp.where(lane == e, slot_scr[...], -1), -1, keepdims=True)
    n_tiles = (jnp.max(slot_col) + MOE_TILE) // MOE_TILE

    def tile(k, carry):
        @pl.when(j == 0)
        def _():
            slot_row = slott_scr[pl.ds(e, 1), :]
            rr = lax.broadcasted_iota(jnp.int32, (MOE_TILE, MOE_ROWS), 0) + k * MOE_TILE
            pick = jnp.where(rr == slot_row, 1.0, 0.0).astype(BF16)
            hs_scr[k] = jnp.dot(pick, h_ref[...], preferred_element_type=F32).astype(BF16)
            ys_scr[k] = jnp.zeros((MOE_TILE, D_MODEL), F32)

        ys_scr[k] += _swiglu_partial(hs_scr[k], w1_ref[0], w3_ref[0], w2_ref[0])

        @pl.when(j == last_j)
        def _():
            g_col = jnp.sum(jnp.where(lane == e, gates_ref[...], 0.0), -1, keepdims=True)
            cc = lax.broadcasted_iota(jnp.int32, (MOE_ROWS, 2 * MOE_TILE), 1)
            cc = jnp.where(cc >= MOE_TILE, cc - MOE_TILE, cc) + k * MOE_TILE
            put = jnp.where(slot_col == cc, 1.0, 0.0).astype(BF16)
            y = ys_scr[k]
            y_hi = y.astype(BF16)
            y_lo = (y - y_hi.astype(F32)).astype(BF16)
            back = jnp.dot(put, jnp.concatenate([y_hi, y_lo], axis=0), preferred_element_type=F32)
            o_ref[...] += g_col * back

        return carry

    lax.fori_loop(0, n_tiles, tile, 0)

    @pl.when(jnp.logical_and(e == pl.num_programs(1) - 1, j == last_j))
    def _():
        y = ALPHA * x_ref[...] + gate_ref[0] * o_ref[...]
        o_ref[...] = _layernorm_rows(y, lg_ref[...], lb_ref[...])


def _moe(h, gates, w1, w3, w2, x, gate, lg, lb, rows_per_mod):
    n = h.shape[0]
    tpb = rows_per_mod // MOE_ROWS
    vec = pl.BlockSpec((1, D_MODEL), lambda i, e, j: (0, 0))
    return pl.pallas_call(
        _moe_kernel,
        grid=(n // MOE_ROWS, N_EXPERTS, D_FF // FF_TILE),
        in_specs=[pl.BlockSpec((MOE_ROWS, D_MODEL), lambda i, e, j: (i, 0)),
                  pl.BlockSpec((MOE_ROWS, LANES), lambda i, e, j: (i, 0)),
                  pl.BlockSpec((1, D_MODEL, FF_TILE), lambda i, e, j: (e, 0, j)),
                  pl.BlockSpec((1, D_MODEL, FF_TILE), lambda i, e, j: (e, 0, j)),
                  pl.BlockSpec((1, FF_TILE, D_MODEL), lambda i, e, j: (e, j, 0)),
                  pl.BlockSpec((MOE_ROWS, D_MODEL), lambda i, e, j: (i, 0)),
                  pl.BlockSpec((1, 1, D_MODEL), lambda i, e, j: (i // tpb, 0, 0)), vec, vec],
        out_specs=pl.BlockSpec((MOE_ROWS, D_MODEL), lambda i, e, j: (i, 0)),
        out_shape=jax.ShapeDtypeStruct((n, D_MODEL), F32),
        scratch_shapes=[pltpu.VMEM((MOE_ROWS, LANES), jnp.int32), pltpu.VMEM((LANES, MOE_ROWS), jnp.int32),
                        pltpu.VMEM((MOE_MAX_TILES, MOE_TILE, D_MODEL), BF16),
                        pltpu.VMEM((MOE_MAX_TILES, MOE_TILE, D_MODEL), F32)],
        compiler_params=_params(("parallel", "arbitrary", "arbitrary"), MOE_VMEM_LIMIT),
        name="moe",
    )(h, gates, w1, w3, w2, x, gate, lg.reshape(1, D_MODEL), lb.reshape(1, D_MODEL))


def _permute_w_in(w):
    pad = jnp.zeros((D_MODEL, U_COLS - ORIG_END), w.dtype)
    return jnp.concatenate([w[:, :ORIG_GATES], w[:, ORIG_SX:ORIG_DT], w[:, ORIG_SZ:ORIG_SX],
                            w[:, ORIG_GATES:ORIG_SZ], w[:, ORIG_DT:ORIG_END], pad], axis=1).astype(BF16)


def _small_row(vals, offset):
    v = vals.reshape(-1).astype(F32)
    return jnp.zeros((1, LANES), F32).at[0, offset:offset + v.shape[0]].set(v)


def _pack_mlstm_state(c, n, m):
    shape = c.shape[:2] + (D_STATE, HEAD_LANES)
    c_rows = jnp.swapaxes(c, 2, 3).reshape(shape)
    n_rows = jnp.broadcast_to(jnp.swapaxes(n, 2, 3)[..., None], c.shape[:2] + (D_STATE, N_HEADS, D_STATE))
    return (jnp.concatenate([c_rows, n_rows.reshape(shape)], axis=-1),
            jnp.repeat(m, D_STATE, axis=-1)[:, :, None, :])


def _pack_ssd_state(s):
    sel = (jnp.arange(N_GROUPS)[:, None] == jnp.arange(N_HEADS)[None, :] // 2).astype(F32)
    return jnp.einsum('bdhpn,gh->bdgnhp', s, sel).reshape(s.shape[:2] + (LANES, HEAD_LANES))


def _layer(x, mods, P, l, bsz, t, ctx, caches=None):
    sh1, sc1, g1, sh2, sc2, g2 = mods
    rows_per_mod = x.shape[0] // sh1.shape[0]
    lam_init = 0.8 - 0.6 * math.exp(-0.3 * l)
    u = _in_proj(x, sc1, sh1, P['w_in'][l], P['conv_w'][l], P['conv_b'][l], rows_per_mod, t)

    gate_bias = (_small_row(P['mlstm_gate_b'][l, 0], GATE_I) + _small_row(P['mlstm_gate_b'][l, 1], GATE_F))
    dt_bias = _small_row(P['ssm_dt_bias'][l], GATE_DT)
    alog = _small_row(P['ssm_A_log'][l], GATE_DT)
    dskip = jnp.repeat(P['ssm_D'][l].astype(F32), D_STATE).reshape(1, D_REC)

    if ctx is None:
        att, k_new, v_new = _attention_ctx(u, P['attn_lambda'][l], P['attn_norm_w'][l], lam_init, bsz, t, l,
                                           None if caches is None else caches[0:2])
        m_init = s_init = None
    else:
        ck, cv, c_c, c_n, c_m, c_s = ctx
        q, k, v = _rope_prep(u, bsz, t)
        k_all = jnp.concatenate([k, ck.astype(BF16)], axis=3)
        v_all = jnp.concatenate([v, cv.astype(BF16)], axis=2)
        att = _attention_lat(q, k_all, v_all, P['attn_lambda'][l], P['attn_norm_w'][l], lam_init, bsz, t)
        m_init = _pack_mlstm_state(c_c, c_n, c_m)
        s_init = _pack_ssd_state(c_s)
    hf, hb, *mlstm_caches = _mlstm(u, gate_bias, bsz, t, m_init, l, None if caches is None else caches[2:5])
    yf, yb, *ssd_caches = _ssd(u, dt_bias, alog, dskip, bsz, t, s_init, l,
                               None if caches is None else caches[5])

    x = _out_proj(att, hf, hb, yf, yb, u, P['w_out'][l], x, g1, P['mlstm_norm_w'][l], P['ssm_norm_w'][l],
                  P['ln_g'][l, 0], P['ln_b'][l, 0], rows_per_mod)
    if l % 2 == 0:
        x = _ffn(x, sc2, sh2, g2, P['ffn_w1'][l // 2], P['ffn_w3'][l // 2], P['ffn_w2'][l // 2],
                 P['ln_g'][l, 1], P['ln_b'][l, 1], rows_per_mod)
    else:
        gates, h2 = _router(x, sc2, sh2, P['router_w'][l // 2], rows_per_mod)
        x = _moe(h2, gates, P['moe_w1'][l // 2], P['moe_w3'][l // 2], P['moe_w2'][l // 2],
                 x, g2, P['ln_g'][l, 1], P['ln_b'][l, 1], rows_per_mod)
    if ctx is None:
        return x, (k_new, v_new, *mlstm_caches, *ssd_caches)
    return x, None


def kernel(x_prompt, x_sample, c, cache_attn_k, cache_attn_v, state_mlstm_C, state_mlstm_n, state_mlstm_m, state_ssm, c_ctx, w_ada, b_ada, w_in, w_out, attn_lambda, attn_norm_w, mlstm_gate_b, mlstm_norm_w, conv_w, conv_b, ssm_A_log, ssm_dt_bias, ssm_D, ssm_norm_w, ln_g, ln_b, ffn_w1, ffn_w3, ffn_w2, router_w, moe_w1, moe_w3, moe_w2):
    bsz, seq, _ = x_prompt.shape
    dbsz, dseq, _ = x_sample.shape
    P = dict(w_in=[_permute_w_in(w_in[l]) for l in range(DEPTH)], w_out=w_out.astype(BF16),
             attn_lambda=attn_lambda, attn_norm_w=attn_norm_w, mlstm_gate_b=mlstm_gate_b,
             mlstm_norm_w=mlstm_norm_w, conv_w=conv_w, conv_b=conv_b, ssm_A_log=ssm_A_log,
             ssm_dt_bias=ssm_dt_bias, ssm_D=ssm_D, ssm_norm_w=ssm_norm_w, ln_g=ln_g, ln_b=ln_b,
             ffn_w1=ffn_w1.astype(BF16), ffn_w3=ffn_w3.astype(BF16), ffn_w2=ffn_w2.astype(BF16),
             router_w=router_w, moe_w1=moe_w1.astype(BF16), moe_w3=moe_w3.astype(BF16),
             moe_w2=moe_w2.astype(BF16))

    cvec = jnp.zeros((8, D_MODEL), F32).at[0].set(c_ctx).at[1:1 + dbsz].set(c)
    mod = _modulation(cvec, w_ada, b_ada)

    def mods_for(l, lo, hi):
        return [mod[l, lo:hi, i * D_MODEL:(i + 1) * D_MODEL][:, None, :] for i in range(6)]

    y_prompt = x_prompt.reshape(bsz * seq, D_MODEL)
    caches = None
    for l in range(DEPTH):
        y_prompt, caches = _layer(y_prompt, mods_for(l, 0, 1), P, l, bsz, seq, None, caches)
    new_k, new_v, new_c, new_n, m_spread, new_s = caches

    y_sample = x_sample.reshape(dbsz * dseq, D_MODEL)
    for l in range(DEPTH):
        ctx = (cache_attn_k[:, l], cache_attn_v[:, l], state_mlstm_C[:, l], state_mlstm_n[:, l],
               state_mlstm_m[:, l], state_ssm[:, l])
        y_sample, _ = _layer(y_sample, mods_for(l, 1, 1 + dbsz), P, l, dbsz, dseq, ctx)

    return (y_prompt.reshape(bsz, seq, D_MODEL), y_sample.reshape(dbsz, dseq, D_MODEL),
            new_k, new_v, new_c, new_n, m_spread[:, :, :, 0, ::D_STATE], new_s)
```

```python
import functools
import math

import jax
import jax.numpy as jnp
from jax import lax
from jax.experimental import pallas as pl
from jax.experimental.pallas import tpu as pltpu

F32 = jnp.float32
BF16 = jnp.bfloat16
HIGHEST = lax.Precision.HIGHEST

D_MODEL = 1024
DEPTH = 2
GRID_W = 64
N_HEADS = 4
D_ATT = 512
D_HEAD_V = 128
D_QK = 64
D_REC = 256
D_STATE = 64
N_GROUPS = 2
D_CONV = 3
D_FF = 2816
N_EXPERTS = 8
ALPHA = (2.0 * DEPTH) ** 0.25
CHUNK = 64
ROPE_BASE = 10000.0
LOG2E = 1.4426950408889634
EPS = 1e-5

COL_AQ, COL_AK, COL_AV = 0, 512, 1024
COL_MQ, COL_MK, COL_MV, COL_MO = 1536, 1792, 2048, 2304
COL_SX, COL_SBC, COL_SZ = 2560, 2816, 3072
COL_SMALL = 3328
U_COLS = 3584
ORIG_GATES, ORIG_SZ, ORIG_SX, ORIG_DT, ORIG_END = 2560, 2576, 2832, 3344, 3352
GATE_I, GATE_F, GATE_DT = 0, 8, 16

LANES = 128
ROW_TILE = 512
SCAN_ROWS = 512
SCAN_BATCH = 2
VMEM_LIMIT = 48 * 1024 * 1024

NT_DIMS = (((1,), (1,)), ((), ()))
TN_DIMS = (((0,), (0,)), ((), ()))


def _params(sem, vmem=VMEM_LIMIT):
    return pltpu.CompilerParams(dimension_semantics=sem, vmem_limit_bytes=vmem)


def _silu(x):
    return x * jax.nn.sigmoid(x)


def _bdot(a, b):
    return jnp.dot(a.astype(BF16), b.astype(BF16), preferred_element_type=F32)


def _bdot_nt(a, b):
    return lax.dot_general(a.astype(BF16), b.astype(BF16), NT_DIMS, preferred_element_type=F32)


def _layernorm_rows(y, g, b):
    mu = jnp.mean(y, -1, keepdims=True)
    d = y - mu
    var = jnp.mean(d * d, -1, keepdims=True)
    return d * lax.rsqrt(var + EPS) * g + b


def _mod_kernel(c_ref, w_ref, b_ref, o_ref):
    o_ref[0] = jnp.dot(_silu(c_ref[...]), w_ref[0], precision=HIGHEST,
                       preferred_element_type=F32) + b_ref[0]


def _modulation(cvec, w_ada, b_ada):
    tn = 1536
    return pl.pallas_call(
        _mod_kernel,
        grid=(DEPTH, 6 * D_MODEL // tn),
        in_specs=[pl.BlockSpec((8, D_MODEL), lambda l, j: (0, 0)),
                  pl.BlockSpec((1, D_MODEL, tn), lambda l, j: (l, 0, j)),
                  pl.BlockSpec((1, 1, tn), lambda l, j: (l, 0, j))],
        out_specs=pl.BlockSpec((1, 8, tn), lambda l, j: (l, 0, j)),
        out_shape=jax.ShapeDtypeStruct((DEPTH, 8, 6 * D_MODEL), F32),
        compiler_params=_params(("parallel", "parallel")),
        name="modulation",
    )(cvec, w_ada, b_ada.reshape(DEPTH, 1, 6 * D_MODEL))


CONV_COLS = 2 * D_REC
HALO = 16


def _inproj_kernel(x_ref, prev_ref, next_ref, sc_ref, sh_ref, w_ref, cw_ref, cb_ref, o_ref, *, seq):
    def modulate(x):
        return (x * (1.0 + sc_ref[0]) + sh_ref[0]).astype(BF16)

    h = modulate(x_ref[...])
    for n0 in range(0, U_COLS, CONV_COLS):
        if n0 != COL_SX:
            o_ref[:, n0:n0 + CONV_COLS] = jnp.dot(h, w_ref[:, n0:n0 + CONV_COLS], preferred_element_type=F32)
            continue
        h_ext = jnp.concatenate([h, modulate(prev_ref[...]), modulate(next_ref[...])], axis=0)
        pre_ext = jnp.dot(h_ext, w_ref[:, n0:n0 + CONV_COLS], preferred_element_type=F32)
        pre = pre_ext[:ROW_TILE]
        before = pre_ext[ROW_TILE + HALO - 1:ROW_TILE + HALO]
        after = pre_ext[ROW_TILE + HALO:ROW_TILE + HALO + 1]
        r = lax.broadcasted_iota(jnp.int32, (ROW_TILE, 1), 0)
        pos = (r + pl.program_id(0) * ROW_TILE) % seq
        up = jnp.where(r == 0, before, pltpu.roll(pre, 1, 0))
        down = jnp.where(r == ROW_TILE - 1, after, pltpu.roll(pre, ROW_TILE - 1, 0))
        up = jnp.where(pos == 0, 0.0, up)
        down = jnp.where(pos == seq - 1, 0.0, down)
        cw = cw_ref[...]
        o_ref[:, n0:n0 + CONV_COLS] = _silu(up * cw[0:1] + pre * cw[1:2] + down * cw[2:3] + cb_ref[...])


def _in_proj(x, sc, sh, w, conv_w, conv_b, rows_per_mod, seq):
    n = x.shape[0]
    tpb = rows_per_mod // ROW_TILE
    r8 = ROW_TILE // HALO
    last8 = n // HALO - 1
    assert COL_SX % CONV_COLS == 0 and U_COLS % CONV_COLS == 0
    return pl.pallas_call(
        functools.partial(_inproj_kernel, seq=seq),
        grid=(n // ROW_TILE,),
        in_specs=[pl.BlockSpec((ROW_TILE, D_MODEL), lambda i: (i, 0)),
                  pl.BlockSpec((HALO, D_MODEL), lambda i: (jnp.maximum(i * r8 - 1, 0), 0)),
                  pl.BlockSpec((HALO, D_MODEL), lambda i: (jnp.minimum((i + 1) * r8, last8), 0)),
                  pl.BlockSpec((1, 1, D_MODEL), lambda i: (i // tpb, 0, 0)),
                  pl.BlockSpec((1, 1, D_MODEL), lambda i: (i // tpb, 0, 0)),
                  pl.BlockSpec((D_MODEL, U_COLS), lambda i: (0, 0)),
                  pl.BlockSpec((D_CONV, CONV_COLS), lambda i: (0, 0)),
                  pl.BlockSpec((1, CONV_COLS), lambda i: (0, 0))],
        out_specs=pl.BlockSpec((ROW_TILE, U_COLS), lambda i: (i, 0)),
        out_shape=jax.ShapeDtypeStruct((n, U_COLS), F32),
        compiler_params=_params(("parallel",)),
        name="in_proj",
    )(x, x, x, sc, sh, w, conv_w, conv_b.reshape(1, CONV_COLS))


def _lambda_scalar(lam_ref, lam_init):
    lp = lam_ref[...]
    s01 = jnp.sum(lp[0:1] * lp[1:2], axis=-1, keepdims=True)
    s23 = jnp.sum(lp[2:3] * lp[3:4], axis=-1, keepdims=True)
    return jnp.exp(s01) - jnp.exp(s23) + lam_init


def _softmax_rows(s):
    e = jnp.exp(s - jnp.max(s, -1, keepdims=True))
    return e / jnp.sum(e, -1, keepdims=True)


def _head_norm(o, nw, lam_init):
    return o * lax.rsqrt(jnp.mean(o * o, -1, keepdims=True) + EPS) * nw * (1.0 - lam_init)


def _layer_cache(tail, l, bsz, nb=1):
    zeros = (0,) * len(tail)
    owned = DEPTH if l == 0 else 1
    spec = pl.BlockSpec((nb, owned) + tuple(tail), lambda b, *_: (b, l) + zeros)
    return jax.ShapeDtypeStruct((bsz, DEPTH) + tuple(tail), F32), spec, owned


def _zero_later_layers(ref):
    for b in range(ref.shape[0]):
        for later in range(1, ref.shape[1]):
            ref[b, later] = jnp.zeros(ref.shape[2:], ref.dtype)


def _attn_ctx_kernel(u_ref, lam_ref, nw_ref, *rest, lam_init):
    att_ref, k_ref, v_ref = rest[-3:]
    lam = _lambda_scalar(lam_ref, lam_init)
    for h in range(N_HEADS):
        v = u_ref[:, COL_AV + h * D_HEAD_V:COL_AV + (h + 1) * D_HEAD_V]
        v_ref[0, 0, h] = v
        ps = []
        for m in range(2):
            c0 = h * D_HEAD_V + m * D_QK
            q = u_ref[:, COL_AQ + c0:COL_AQ + c0 + D_QK] * (D_QK ** -0.5)
            k = u_ref[:, COL_AK + c0:COL_AK + c0 + D_QK]
            k_ref[0, 0, h, m] = k
            ps.append(_softmax_rows(_bdot_nt(q, k)))
        o = _bdot(ps[0] - lam * ps[1], v)
        att_ref[:, h * D_HEAD_V:(h + 1) * D_HEAD_V] = _head_norm(o, nw_ref[...], lam_init).astype(BF16)
    _zero_later_layers(k_ref)
    _zero_later_layers(v_ref)


def _attention_ctx(u, lam_p, norm_w, lam_init, bsz, t, l, caches):
    n = bsz * t
    k_shape, k_spec, _ = _layer_cache((N_HEADS, 2, t, D_QK), l, bsz)
    v_shape, v_spec, _ = _layer_cache((N_HEADS, t, D_HEAD_V), l, bsz)
    in_specs = [pl.BlockSpec((t, 3 * D_ATT), lambda b: (b, 0)),
                pl.BlockSpec((4, D_QK), lambda b: (0, 0)),
                pl.BlockSpec((1, D_HEAD_V), lambda b: (0, 0))]
    args = [u, lam_p, norm_w.reshape(1, D_HEAD_V)]
    aliases = {}
    if caches is not None:
        in_specs += [pl.BlockSpec(memory_space=pl.ANY)] * 2
        aliases = {len(args): 1, len(args) + 1: 2}
        args += list(caches)
    return pl.pallas_call(
        functools.partial(_attn_ctx_kernel, lam_init=lam_init),
        grid=(bsz,),
        in_specs=in_specs,
        out_specs=[pl.BlockSpec((t, D_ATT), lambda b: (b, 0)), k_spec, v_spec],
        out_shape=[jax.ShapeDtypeStruct((n, D_ATT), BF16), k_shape, v_shape],
        input_output_aliases=aliases,
        compiler_params=_params(("parallel",)),
        name="attn_ctx",
    )(*args)


def _rope_kernel(u_ref, cos_ref, sa_ref, sb_ref, q_ref, k_ref, v_ref):
    cos, sa, sb = cos_ref[...], sa_ref[...], sb_ref[...]

    def rope(x):
        return x * cos + pltpu.roll(x, LANES - 16, 1) * sa + pltpu.roll(x, 16, 1) * sb

    for h in range(N_HEADS):
        q = rope(u_ref[:, COL_AQ + h * D_HEAD_V:COL_AQ + (h + 1) * D_HEAD_V]) * (LOG2E * D_QK ** -0.5)
        k = rope(u_ref[:, COL_AK + h * D_HEAD_V:COL_AK + (h + 1) * D_HEAD_V])
        for m in range(2):
            q_ref[0, h, m] = q[:, m * D_QK:(m + 1) * D_QK].astype(BF16)
            k_ref[0, h, m] = k[:, m * D_QK:(m + 1) * D_QK].astype(BF16)
        v_ref[0, h] = u_ref[:, COL_AV + h * D_HEAD_V:COL_AV + (h + 1) * D_HEAD_V].astype(BF16)


def _rope_tables(t):
    rows = jnp.repeat(jnp.arange(t // GRID_W, dtype=F32), GRID_W)
    cols = jnp.tile(jnp.arange(GRID_W, dtype=F32), t // GRID_W)
    half = D_QK // 2
    inv = ROPE_BASE ** (-jnp.arange(0, half, 2, dtype=F32) / half)
    ang_r = rows[:, None] * inv
    ang_c = cols[:, None] * inv
    ang = jnp.concatenate([ang_r, ang_r, ang_c, ang_c], -1)
    cos, sin = jnp.cos(ang), jnp.sin(ang)
    quarter = (jnp.arange(D_QK) // (D_QK // 4)) % 2
    sa = jnp.where(quarter == 0, -sin, 0.0)
    sb = jnp.where(quarter == 1, sin, 0.0)
    tile2 = lambda a: jnp.concatenate([a, a], -1)
    return tile2(cos), tile2(sa), tile2(sb)


def _rope_prep(u, bsz, t):
    tr = 512
    nb = t // tr
    cos, sa, sb = _rope_tables(t)
    tab = pl.BlockSpec((tr, LANES), lambda b, i: (i, 0))
    return pl.pallas_call(
        _rope_kernel,
        grid=(bsz, nb),
        in_specs=[pl.BlockSpec((tr, 3 * D_ATT), lambda b, i: (b * nb + i, 0)), tab, tab, tab],
        out_specs=[pl.BlockSpec((1, N_HEADS, 2, tr, D_QK), lambda b, i: (b, 0, 0, i, 0)),
                   pl.BlockSpec((1, N_HEADS, 2, tr, D_QK), lambda b, i: (b, 0, 0, i, 0)),
                   pl.BlockSpec((1, N_HEADS, tr, D_HEAD_V), lambda b, i: (b, 0, i, 0))],
        out_shape=[jax.ShapeDtypeStruct((bsz, N_HEADS, 2, t, D_QK), BF16),
                   jax.ShapeDtypeStruct((bsz, N_HEADS, 2, t, D_QK), BF16),
                   jax.ShapeDtypeStruct((bsz, N_HEADS, t, D_HEAD_V), BF16)],
        compiler_params=_params(("parallel", "parallel")),
        name="rope_prep",
    )(u, cos, sa, sb)


def _attn_lat_kernel(q_ref, k_ref, v_ref, lam_ref, nw_ref, o_ref, *, lam_init):
    lam = _lambda_scalar(lam_ref, lam_init)
    for hh in range(ATTN_HEADS):
        es, sums = [], []
        for m in range(2):
            s = lax.dot_general(q_ref[0, hh, m], k_ref[0, hh, m], NT_DIMS, preferred_element_type=F32)
            e = jnp.exp2(s - jnp.max(s, -1, keepdims=True))
            es.append(e)
            sums.append(jnp.sum(e, -1, keepdims=True))
        a = es[0] - (lam * sums[0] / sums[1]) * es[1]
        o = _bdot(a, v_ref[0, hh]) / sums[0]
        o_ref[:, hh * D_HEAD_V:(hh + 1) * D_HEAD_V] = _head_norm(o, nw_ref[...], lam_init).astype(BF16)


ATTN_HEADS = 2


def _attention_lat(q, k_all, v_all, lam_p, norm_w, lam_init, bsz, t):
    tq = 256
    nq = t // tq
    s = k_all.shape[3]
    nh = ATTN_HEADS
    return pl.pallas_call(
        functools.partial(_attn_lat_kernel, lam_init=lam_init),
        grid=(bsz, N_HEADS // nh, nq),
        in_specs=[pl.BlockSpec((1, nh, 2, tq, D_QK), lambda b, h, i: (b, h, 0, i, 0)),
                  pl.BlockSpec((1, nh, 2, s, D_QK), lambda b, h, i: (b, h, 0, 0, 0)),
                  pl.BlockSpec((1, nh, s, D_HEAD_V), lambda b, h, i: (b, h, 0, 0)),
                  pl.BlockSpec((4, D_QK), lambda b, h, i: (0, 0)),
                  pl.BlockSpec((1, D_HEAD_V), lambda b, h, i: (0, 0))],
        out_specs=pl.BlockSpec((tq, nh * D_HEAD_V), lambda b, h, i: (b * nq + i, h)),
        out_shape=jax.ShapeDtypeStruct((bsz * t, D_ATT), BF16),
        compiler_params=_params(("parallel", "parallel", "parallel")),
        name="attn_lat",
    )(q, k_all, v_all, lam_p, norm_w.reshape(1, D_HEAD_V))


HEAD_LANES = N_HEADS * D_STATE


def _scan_consts():
    t = lax.broadcasted_iota(jnp.int32, (CHUNK, HEAD_LANES), 0)
    s = lax.broadcasted_iota(jnp.int32, (CHUNK, HEAD_LANES), 1) & (CHUNK - 1)
    r = lax.broadcasted_iota(jnp.int32, (CHUNK, CHUNK), 0)
    c = lax.broadcasted_iota(jnp.int32, (CHUNK, CHUNK), 1)
    reads = (jnp.where(s <= t, 1.0, 0.0), jnp.where(s >= t, 1.0, 0.0))
    block = (jnp.where(s <= t, 0.0, -jnp.inf), jnp.where(s >= t, 0.0, -jnp.inf))
    tri = (jnp.where(c <= r, 1.0, 0.0).astype(BF16), jnp.where(c >= r, 1.0, 0.0).astype(BF16))
    return reads, block, tri, jnp.where(s == t, 1.0, 0.0)


def _cumsum_rows(tri, x):
    hi = x.astype(BF16)
    rest = x - hi.astype(F32)
    mid = rest.astype(BF16)
    lo = (rest - mid.astype(F32)).astype(BF16)
    parts = jnp.dot(tri, jnp.concatenate([hi, mid, lo], axis=1), preferred_element_type=F32)
    return parts[:, :HEAD_LANES] + parts[:, HEAD_LANES:2 * HEAD_LANES] + parts[:, 2 * HEAD_LANES:]


def _group_mask(rows, cols, row_shift, col_shift, dtype):
    r = lax.broadcasted_iota(jnp.int32, (rows, cols), 0) >> 6
    c = (lax.broadcasted_iota(jnp.int32, (rows, cols), 1) >> 6) & (N_HEADS - 1)
    return jnp.where((r >> row_shift) == (c >> col_shift), 1.0, 0.0).astype(dtype)


def _spread(x, chans):
    return jnp.concatenate([jnp.broadcast_to(x[:, c:c + 1], (CHUNK, D_STATE)) for c in chans], axis=1)


def _stack_heads(x):
    return jnp.concatenate([x] * N_HEADS, axis=0)


def _mlstm_chunk(q4, k4, v4, g, d, cn_prev, m_prev, reads, block, tri, eye, bd, bd_f32):
    li = _spread(g, [GATE_I + d * N_HEADS + h for h in range(N_HEADS)])
    lf = _spread(jax.nn.log_sigmoid(g), [GATE_F + d * N_HEADS + h for h in range(N_HEADS)])
    bc = _cumsum_rows(tri[d], lf)
    btot = jnp.sum(lf, 0, keepdims=True)
    b_row = jnp.sum(reads[1 - d] * lf, 0, keepdims=True)
    li_row = jnp.sum(eye * li, 0, keepdims=True)
    dm = bc - b_row + li_row + block[d]
    rmax = jnp.concatenate(
        [jnp.broadcast_to(jnp.max(dm[:, h * D_STATE:(h + 1) * D_STATE], -1, keepdims=True), (CHUNK, D_STATE))
         for h in range(N_HEADS)], axis=1)
    inter = bc + m_prev
    m_t = jnp.maximum(inter, rmax)
    w_inter = jnp.exp(inter - m_t)
    qs = (q4 * (D_STATE ** -0.5)).astype(BF16)
    kbd = bd[:, :HEAD_LANES] * _stack_heads(k4.astype(BF16))
    s4 = lax.dot_general(qs, kbd, NT_DIMS, preferred_element_type=F32) * jnp.exp(dm - m_t)
    vo = jnp.concatenate([v4.astype(BF16), jnp.ones((CHUNK, HEAD_LANES), BF16)], axis=1)
    vbd = bd * _stack_heads(vo)
    nd = (jnp.concatenate([w_inter, w_inter], axis=1)
          * jnp.dot(qs, bd * _stack_heads(cn_prev.astype(BF16)), preferred_element_type=F32)
          + jnp.dot(s4.astype(BF16), vbd, preferred_element_type=F32))
    hc = nd[:, :HEAD_LANES] / jnp.maximum(jnp.abs(nd[:, HEAD_LANES:]), jnp.exp(-m_t))
    gcol = btot - bc + li
    m_new = jnp.maximum(btot + m_prev, jnp.max(gcol, 0, keepdims=True))
    w_c = jnp.exp(btot + m_prev - m_new)
    kw = (k4 * jnp.exp(gcol - m_new)).astype(BF16)
    dcn = lax.dot_general(kw, vo, TN_DIMS, preferred_element_type=F32)
    own = sum(bd_f32[h * D_STATE:(h + 1) * D_STATE] * dcn[h * D_STATE:(h + 1) * D_STATE]
              for h in range(N_HEADS))
    cn_new = jnp.concatenate([w_c, w_c], axis=1) * cn_prev + own
    return hc, cn_new, m_new


def _mlstm_kernel(*refs, nblk, rows, zero_init, n_alias):
    qf_ref, kf_ref, vf_ref, gf_ref, qb_ref, kb_ref, vb_ref, gb_ref, bias_ref = refs[:9]
    if not zero_init:
        c0_ref, m0_ref = refs[9:11]
    outs = refs[9 + (0 if zero_init else 2) + n_alias:-2]
    hf_ref, hb_ref = outs[:2]
    cn_scr, m_scr = refs[-2:]
    j = pl.program_id(1)
    nchunk = rows // CHUNK

    @pl.when(j == 0)
    def _():
        if zero_init:
            cn_scr[...] = jnp.zeros_like(cn_scr)
            m_scr[...] = jnp.zeros_like(m_scr)
        else:
            cn_scr[...] = c0_ref[...]
            m_scr[...] = m0_ref[...]

    reads, block, tri, eye = _scan_consts()
    bd = _group_mask(HEAD_LANES, 2 * HEAD_LANES, 0, 0, BF16)
    bd_f32 = _group_mask(HEAD_LANES, 2 * HEAD_LANES, 0, 0, F32)

    chains = [(b, d) for b in range(SCAN_BATCH) for d in range(2)]
    state = {bd_: (cn_scr[bd_], m_scr[bd_]) for bd_ in chains}
    for ci in range(nchunk):
        for b, d in chains:
            q_ref, k_ref, v_ref, g_ref, h_ref = ((qf_ref, kf_ref, vf_ref, gf_ref, hf_ref) if d == 0
                                                 else (qb_ref, kb_ref, vb_ref, gb_ref, hb_ref))
            cj = ci if d == 0 else nchunk - 1 - ci
            rs = slice(cj * CHUNK, (cj + 1) * CHUNK)
            g = g_ref[b, rs, :] + bias_ref[...]
            hc, cn_new, m_new = _mlstm_chunk(q_ref[b, rs, :], k_ref[b, rs, :], v_ref[b, rs, :], g, d,
                                             *state[b, d], reads, block, tri, eye, bd, bd_f32)
            h_ref[b, rs, :] = hc
            state[b, d] = (cn_new, m_new)
    for bd_ in chains:
        cn_scr[bd_], m_scr[bd_] = state[bd_]

    if zero_init:
        cout_ref, nout_ref, mout_ref = outs[2:]

        @pl.when(j == nblk - 1)
        def _():
            r = lax.broadcasted_iota(jnp.int32, (D_STATE, D_STATE), 0)
            c = lax.broadcasted_iota(jnp.int32, (D_STATE, D_STATE), 1)
            for b, d in chains:
                for h in range(N_HEADS):
                    r0, r1 = h * D_STATE, (h + 1) * D_STATE
                    cout_ref[b, 0, d, h] = cn_scr[b, d, :, r0:r1]
                    n_spread = cn_scr[b, d, :, HEAD_LANES + r0:HEAD_LANES + r1]
                    nout_ref[b, 0, d, h:h + 1, :] = jnp.sum(jnp.where(r == c, n_spread, 0.0), 0, keepdims=True)
                mout_ref[b, 0, d] = m_scr[b, d]
            _zero_later_layers(cout_ref)
            _zero_later_layers(nout_ref)
            _zero_later_layers(mout_ref)


def _mlstm(u, gate_bias, bsz, t, init, l=0, caches=None):
    rows = min(t, SCAN_ROWS)
    nblk = t // rows
    zero_init = init is None

    nb = SCAN_BATCH
    fwd = lambda col: (lambda p, j: (p, j, col))
    bwd = lambda col: (lambda p, j: (p, nblk - 1 - j, col))
    cq, ck, cv, cg = COL_MQ // D_REC, COL_MK // D_REC, COL_MV // D_REC, COL_SMALL // LANES
    in_specs = []
    for mk in (fwd, bwd):
        in_specs += [pl.BlockSpec((nb, rows, D_REC), mk(cq)), pl.BlockSpec((nb, rows, D_REC), mk(ck)),
                     pl.BlockSpec((nb, rows, D_REC), mk(cv)), pl.BlockSpec((nb, rows, LANES), mk(cg))]
    in_specs.append(pl.BlockSpec((1, LANES), lambda p, j: (0, 0)))
    args = [u.reshape(bsz, t, U_COLS)] * 8 + [gate_bias]
    state_c = pl.BlockSpec((nb, 2, D_STATE, 2 * HEAD_LANES), lambda p, j: (p, 0, 0, 0))
    state_m = pl.BlockSpec((nb, 2, 1, HEAD_LANES), lambda p, j: (p, 0, 0, 0))
    out_specs = [pl.BlockSpec((nb, rows, D_REC), fwd(0)), pl.BlockSpec((nb, rows, D_REC), bwd(0))]
    out_shape = [jax.ShapeDtypeStruct((bsz, t, D_REC), F32), jax.ShapeDtypeStruct((bsz, t, D_REC), F32)]
    aliases = {}
    if zero_init:
        for tail in ((2, N_HEADS, D_STATE, D_STATE), (2, N_HEADS, D_STATE), (2, 1, HEAD_LANES)):
            shape, spec, _ = _layer_cache(tail, l, bsz, nb)
            out_shape.append(shape)
            out_specs.append(spec)
        if caches is not None:
            in_specs += [pl.BlockSpec(memory_space=pl.ANY)] * len(caches)
            aliases = {len(args) + i: 2 + i for i in range(len(caches))}
            args += list(caches)
    else:
        in_specs += [state_c, state_m]
        args += list(init)
    hf, hb, *state_out = pl.pallas_call(
        functools.partial(_mlstm_kernel, nblk=nblk, rows=rows, zero_init=zero_init, n_alias=len(aliases)),
        grid=(bsz // nb, nblk),
        in_specs=in_specs,
        out_specs=out_specs,
        out_shape=out_shape,
        input_output_aliases=aliases,
        scratch_shapes=[pltpu.VMEM((nb, 2, D_STATE, 2 * HEAD_LANES), F32),
                        pltpu.VMEM((nb, 2, 1, HEAD_LANES), F32)],
        compiler_params=_params(("parallel", "arbitrary")),
        name="mlstm_scan",
    )(*args)
    return (hf.reshape(bsz * t, D_REC), hb.reshape(bsz * t, D_REC), *state_out)


def _ssd_chunk(x4, bcm, dt128, da128, d, sg_prev, reads, block, tri, b_sel, s_sel, bd):
    chans = [GATE_DT + d * N_HEADS + h for h in range(N_HEADS)]
    dt = _spread(dt128, chans)
    da = _spread(da128, chans)
    ac = _cumsum_rows(tri[d], da)
    atot = jnp.sum(da, 0, keepdims=True)
    a_row = jnp.sum(reads[1 - d] * da, 0, keepdims=True)
    decay = jnp.exp(ac - a_row + block[d])
    bmat = bcm[:, :LANES].astype(BF16)
    cmat = bcm[:, LANES:].astype(BF16)
    bbd = b_sel * _stack_heads(bmat)
    g4 = lax.dot_general(cmat, bbd, NT_DIMS, preferred_element_type=F32)
    xbd = bd * _stack_heads((x4 * dt).astype(BF16))
    y = (jnp.dot((g4 * decay).astype(BF16), xbd, preferred_element_type=F32)
         + jnp.dot(cmat, sg_prev.astype(BF16), preferred_element_type=F32) * jnp.exp(ac))
    w = jnp.exp(atot - ac) * dt
    dsg = lax.dot_general(bmat, (x4 * w).astype(BF16), TN_DIMS, preferred_element_type=F32)
    sg_new = jnp.exp(atot) * sg_prev + s_sel * dsg
    return y, sg_new


def _ssd_kernel(*refs, nblk, rows, zero_init, n_alias):
    xf_ref, bcf_ref, gf_ref, xb_ref, bcb_ref, gb_ref, dtb_ref, alog_ref, dskip_ref = refs[:9]
    if not zero_init:
        s0_ref = refs[9]
    outs = refs[9 + (0 if zero_init else 1) + n_alias:-1]
    yf_ref, yb_ref = outs[:2]
    s_scr = refs[-1]
    j = pl.program_id(1)
    nchunk = rows // CHUNK

    @pl.when(j == 0)
    def _():
        if zero_init:
            s_scr[...] = jnp.zeros_like(s_scr)
        else:
            s_scr[...] = s0_ref[0]

    reads, block, tri, _ = _scan_consts()
    bd = _group_mask(HEAD_LANES, HEAD_LANES, 0, 0, BF16)
    b_sel = _group_mask(HEAD_LANES, LANES, 1, 0, BF16)
    s_sel = _group_mask(LANES, HEAD_LANES, 0, 1, F32)
    a_coef = -jnp.exp(alog_ref[...])

    state = [s_scr[d] for d in range(2)]
    for ci in range(nchunk):
        for d in range(2):
            x_ref, bc_ref, g_ref, y_ref = ((xf_ref, bcf_ref, gf_ref, yf_ref) if d == 0
                                           else (xb_ref, bcb_ref, gb_ref, yb_ref))
            cj = ci if d == 0 else nchunk - 1 - ci
            rs = slice(cj * CHUNK, (cj + 1) * CHUNK)
            dt128 = jax.nn.softplus(g_ref[rs, :] + dtb_ref[...])
            x4 = x_ref[rs, :]
            y, state[d] = _ssd_chunk(x4, bc_ref[rs, :], dt128, dt128 * a_coef, d, state[d],
                                     reads, block, tri, b_sel, s_sel, bd)
            if d == 0:
                y = y + dskip_ref[...] * x4
            y_ref[rs, :] = y
    for d in range(2):
        s_scr[d] = state[d]

    if zero_init:
        sout_ref = outs[2]

        @pl.when(j == nblk - 1)
        def _():
            for d in range(2):
                s_t = s_scr[d].T
                for h in range(N_HEADS):
                    g0 = (h // 2) * D_STATE
                    sout_ref[0, 0, d, h] = s_t[h * D_STATE:(h + 1) * D_STATE, g0:g0 + D_STATE]
            _zero_later_layers(sout_ref)


def _ssd(u, dt_bias_row, alog_row, dskip_row, bsz, t, init, l=0, cache=None):
    rows = min(t, SCAN_ROWS)
    nblk = t // rows
    zero_init = init is None

    def fwd(col):
        return lambda b, j: (b * nblk + j, col)

    def bwd(col):
        return lambda b, j: (b * nblk + nblk - 1 - j, col)

    in_specs = []
    for mk in (fwd, bwd):
        in_specs += [pl.BlockSpec((rows, D_REC), mk(COL_SX // D_REC)),
                     pl.BlockSpec((rows, D_REC), mk(COL_SBC // D_REC)),
                     pl.BlockSpec((rows, LANES), mk(COL_SMALL // LANES))]
    in_specs += [pl.BlockSpec((1, LANES), lambda b, j: (0, 0)),
                 pl.BlockSpec((1, LANES), lambda b, j: (0, 0)),
                 pl.BlockSpec((1, D_REC), lambda b, j: (0, 0))]
    args = [u] * 6 + [dt_bias_row, alog_row, dskip_row]
    out_specs = [pl.BlockSpec((rows, D_REC), fwd(0)), pl.BlockSpec((rows, D_REC), bwd(0))]
    out_shape = [jax.ShapeDtypeStruct((bsz * t, D_REC), F32), jax.ShapeDtypeStruct((bsz * t, D_REC), F32)]
    aliases = {}
    if zero_init:
        shape, spec, _ = _layer_cache((2, N_HEADS, D_STATE, D_STATE), l, bsz)
        out_shape.append(shape)
        out_specs.append(spec)
        if cache is not None:
            in_specs.append(pl.BlockSpec(memory_space=pl.ANY))
            aliases = {len(args): 2}
            args.append(cache)
    else:
        in_specs.append(pl.BlockSpec((1, 2, LANES, HEAD_LANES), lambda b, j: (b, 0, 0, 0)))
        args.append(init)
    return pl.pallas_call(
        functools.partial(_ssd_kernel, nblk=nblk, rows=rows, zero_init=zero_init, n_alias=len(aliases)),
        grid=(bsz, nblk),
        in_specs=in_specs,
        out_specs=out_specs,
        out_shape=out_shape,
        input_output_aliases=aliases,
        scratch_shapes=[pltpu.VMEM((2, LANES, HEAD_LANES), F32)],
        compiler_params=_params(("parallel", "arbitrary")),
        name="ssd_scan",
    )(*args)


def _outproj_kernel(att_ref, hf_ref, hb_ref, mo_ref, yf_ref, yb_ref, z_ref, w_ref, x_ref, gate_ref,
                    mnw_ref, snw_ref, lg_ref, lb_ref, o_ref):
    hh = hf_ref[...] + hb_ref[...]
    parts = []
    for h in range(N_HEADS):
        xh = hh[:, h * D_STATE:(h + 1) * D_STATE]
        mu = jnp.mean(xh, -1, keepdims=True)
        dlt = xh - mu
        var = jnp.mean(dlt * dlt, -1, keepdims=True)
        parts.append(dlt * lax.rsqrt(var + EPS))
    ml = jax.nn.sigmoid(mo_ref[...]) * jnp.concatenate(parts, axis=1) * mnw_ref[...]
    yz = (yf_ref[...] + yb_ref[...]) * _silu(z_ref[...])
    parts = []
    for grp in range(N_GROUPS):
        yg = yz[:, grp * LANES:(grp + 1) * LANES]
        parts.append(yg * lax.rsqrt(jnp.mean(yg * yg, -1, keepdims=True) + EPS))
    ssm = jnp.concatenate(parts, axis=1) * snw_ref[...]
    mixed = (jnp.dot(att_ref[...], w_ref[0:D_ATT], preferred_element_type=F32)
             + _bdot(ml, w_ref[D_ATT:D_ATT + D_REC])
             + _bdot(ssm, w_ref[D_ATT + D_REC:D_MODEL]))
    y = ALPHA * x_ref[...] + gate_ref[0] * mixed
    o_ref[...] = _layernorm_rows(y, lg_ref[...], lb_ref[...])


def _out_proj(att, hf, hb, yf, yb, u, w, x, gate, mnw, snw, lg, lb, rows_per_mod):
    n = x.shape[0]
    tpb = rows_per_mod // ROW_TILE
    row = lambda width, col: pl.BlockSpec((ROW_TILE, width), lambda i: (i, col))
    vec = lambda width: pl.BlockSpec((1, width), lambda i: (0, 0))
    return pl.pallas_call(
        _outproj_kernel,
        grid=(n // ROW_TILE,),
        in_specs=[row(D_ATT, 0), row(D_REC, 0), row(D_REC, 0), row(D_REC, COL_MO // D_REC),
                  row(D_REC, 0), row(D_REC, 0), row(D_REC, COL_SZ // D_REC),
                  pl.BlockSpec((D_MODEL, D_MODEL), lambda i: (0, 0)),
                  row(D_MODEL, 0),
                  pl.BlockSpec((1, 1, D_MODEL), lambda i: (i // tpb, 0, 0)),
                  vec(D_REC), vec(D_REC), vec(D_MODEL), vec(D_MODEL)],
        out_specs=row(D_MODEL, 0),
        out_shape=jax.ShapeDtypeStruct((n, D_MODEL), F32),
        compiler_params=_params(("parallel",)),
        name="out_proj",
    )(att, hf, hb, u, yf, yb, u, w, x, gate, mnw.reshape(1, D_REC), snw.reshape(1, D_REC),
      lg.reshape(1, D_MODEL), lb.reshape(1, D_MODEL))


FF_TILE = D_FF // 2
FFN_ROWS = 1024
FFN_SUB = 512
FFN_VMEM_LIMIT = 56 * 1024 * 1024


def _swiglu_partial(h, w1, w3, w2):
    a = jnp.dot(h, w1, preferred_element_type=F32)
    b = jnp.dot(h, w3, preferred_element_type=F32)
    return jnp.dot((_silu(a) * b).astype(BF16), w2, preferred_element_type=F32)


def _ffn_kernel(x_ref, sc_ref, sh_ref, gate_ref, w1_ref, w3_ref, w2_ref, lg_ref, lb_ref, o_ref):
    for r0 in range(0, FFN_ROWS, FFN_SUB):
        rows = slice(r0, r0 + FFN_SUB)
        x = x_ref[rows, :]
        h = (x * (1.0 + sc_ref[0]) + sh_ref[0]).astype(BF16)
        y = ALPHA * x + gate_ref[0] * _swiglu_partial(h, w1_ref[...], w3_ref[...], w2_ref[...])
        o_ref[rows, :] = _layernorm_rows(y, lg_ref[...], lb_ref[...])


def _ffn(x, sc, sh, gate, w1, w3, w2, lg, lb, rows_per_mod):
    n = x.shape[0]
    tpb = rows_per_mod // FFN_ROWS
    modspec = pl.BlockSpec((1, 1, D_MODEL), lambda i: (i // tpb, 0, 0))
    vec = pl.BlockSpec((1, D_MODEL), lambda i: (0, 0))
    once = pl.Buffered(1)
    return pl.pallas_call(
        _ffn_kernel,
        grid=(n // FFN_ROWS,),
        in_specs=[pl.BlockSpec((FFN_ROWS, D_MODEL), lambda i: (i, 0)), modspec, modspec, modspec,
                  pl.BlockSpec((D_MODEL, D_FF), lambda i: (0, 0), pipeline_mode=once),
                  pl.BlockSpec((D_MODEL, D_FF), lambda i: (0, 0), pipeline_mode=once),
                  pl.BlockSpec((D_FF, D_MODEL), lambda i: (0, 0), pipeline_mode=once), vec, vec],
        out_specs=pl.BlockSpec((FFN_ROWS, D_MODEL), lambda i: (i, 0)),
        out_shape=jax.ShapeDtypeStruct((n, D_MODEL), F32),
        compiler_params=_params(("parallel",), FFN_VMEM_LIMIT),
        name="ffn_dense",
    )(x, sc, sh, gate, w1, w3, w2, lg.reshape(1, D_MODEL), lb.reshape(1, D_MODEL))


def _router_kernel(x_ref, sc_ref, sh_ref, rw_ref, gates_ref, h_ref):
    h = x_ref[...] * (1.0 + sc_ref[0]) + sh_ref[0]
    h_ref[...] = h.astype(BF16)
    logits = jnp.dot(h, rw_ref[...], precision=HIGHEST, preferred_element_type=F32)
    lane = lax.broadcasted_iota(jnp.int32, logits.shape, 1)
    valid = lane < N_EXPERTS
    p = jnp.where(valid, _softmax_rows(jnp.where(valid, logits, -jnp.inf)), -2.0)
    p1 = jnp.max(p, -1, keepdims=True)
    i1 = jnp.min(jnp.where(p == p1, lane, LANES), -1, keepdims=True)
    rest = jnp.where(lane == i1, -1.0, p)
    p2 = jnp.max(rest, -1, keepdims=True)
    i2 = jnp.min(jnp.where(rest == p2, lane, LANES), -1, keepdims=True)
    tot = p1 + p2
    gates_ref[...] = jnp.where(lane == i1, p1 / tot, jnp.where(lane == i2, p2 / tot, 0.0))


def _router(x, sc, sh, router_w, rows_per_mod):
    n = x.shape[0]
    tpb = rows_per_mod // ROW_TILE
    modspec = pl.BlockSpec((1, 1, D_MODEL), lambda i: (i // tpb, 0, 0))
    rw = jnp.pad(router_w, ((0, 0), (0, LANES - N_EXPERTS)))
    return pl.pallas_call(
        _router_kernel,
        grid=(n // ROW_TILE,),
        in_specs=[pl.BlockSpec((ROW_TILE, D_MODEL), lambda i: (i, 0)), modspec, modspec,
                  pl.BlockSpec((D_MODEL, LANES), lambda i: (0, 0))],
        out_specs=[pl.BlockSpec((ROW_TILE, LANES), lambda i: (i, 0)),
                   pl.BlockSpec((ROW_TILE, D_MODEL), lambda i: (i, 0))],
        out_shape=[jax.ShapeDtypeStruct((n, LANES), F32), jax.ShapeDtypeStruct((n, D_MODEL), BF16)],
        compiler_params=_params(("parallel",)),
        name="router",
    )(x, sc, sh, rw)


MOE_ROWS = 1024
MOE_TILE = 128
MOE_MAX_TILES = MOE_ROWS // MOE_TILE
MOE_VMEM_LIMIT = 56 * 1024 * 1024


def _moe_kernel(h_ref, gates_ref, w1_ref, w3_ref, w2_ref, x_ref, gate_ref, lg_ref, lb_ref, o_ref,
                slot_scr, slott_scr, hs_scr, ys_scr):
    e = pl.program_id(1)
    j = pl.program_id(2)
    last_j = pl.num_programs(2) - 1

    @pl.when(jnp.logical_and(e == 0, j == 0))
    def _():
        r = lax.broadcasted_iota(jnp.int32, (MOE_ROWS, MOE_ROWS), 0)
        c = lax.broadcasted_iota(jnp.int32, (MOE_ROWS, MOE_ROWS), 1)
        before = jnp.where(c < r, 1.0, 0.0).astype(BF16)
        mask = gates_ref[...] != 0.0
        rank = jnp.dot(before, jnp.where(mask, 1.0, 0.0).astype(BF16), preferred_element_type=F32)
        slot = jnp.where(mask, rank, -1.0).astype(jnp.int32)
        slot_scr[...] = slot
        slott_scr[...] = slot.T
        o_ref[...] = jnp.zeros_like(o_ref)

    lane = lax.broadcasted_iota(jnp.int32, (MOE_ROWS, LANES), 1)
    slot_col = jnp.max(jnp.where(lane == e, slot_scr[...], -1), -1, keepdims=True)
    n_tiles = (jnp.max(slot_col) + MOE_TILE) // MOE_TILE

    def tile(k, carry):
        @pl.when(j == 0)
        def _():
            slot_row = slott_scr[pl.ds(e, 1), :]
            rr = lax.broadcasted_iota(jnp.int32, (MOE_TILE, MOE_ROWS), 0) + k * MOE_TILE
            pick = jnp.where(rr == slot_row, 1.0, 0.0).astype(BF16)
            hs_scr[k] = jnp.dot(pick, h_ref[...], preferred_element_type=F32).astype(BF16)
            ys_scr[k] = jnp.zeros((MOE_TILE, D_MODEL), F32)

        ys_scr[k] += _swiglu_partial(hs_scr[k], w1_ref[0], w3_ref[0], w2_ref[0])

        @pl.when(j == last_j)
        def _():
            g_col = jnp.sum(jnp.where(lane == e, gates_ref[...], 0.0), -1, keepdims=True)
            cc = lax.broadcasted_iota(jnp.int32, (MOE_ROWS, 2 * MOE_TILE), 1)
            cc = jnp.where(cc >= MOE_TILE, cc - MOE_TILE, cc) + k * MOE_TILE
            put = jnp.where(slot_col == cc, 1.0, 0.0).astype(BF16)
            y = ys_scr[k]
            y_hi = y.astype(BF16)
            y_lo = (y - y_hi.astype(F32)).astype(BF16)
            back = jnp.dot(put, jnp.concatenate([y_hi, y_lo], axis=0), preferred_element_type=F32)
            o_ref[...] += g_col * back

        return carry

    lax.fori_loop(0, n_tiles, tile, 0)

    @pl.when(jnp.logical_and(e == pl.num_programs(1) - 1, j == last_j))
    def _():
        y = ALPHA * x_ref[...] + gate_ref[0] * o_ref[...]
        o_ref[...] = _layernorm_rows(y, lg_ref[...], lb_ref[...])


def _moe(h, gates, w1, w3, w2, x, gate, lg, lb, rows_per_mod):
    n = h.shape[0]
    tpb = rows_per_mod // MOE_ROWS
    vec = pl.BlockSpec((1, D_MODEL), lambda i, e, j: (0, 0))
    return pl.pallas_call(
        _moe_kernel,
        grid=(n // MOE_ROWS, N_EXPERTS, D_FF // FF_TILE),
        in_specs=[pl.BlockSpec((MOE_ROWS, D_MODEL), lambda i, e, j: (i, 0)),
                  pl.BlockSpec((MOE_ROWS, LANES), lambda i, e, j: (i, 0)),
                  pl.BlockSpec((1, D_MODEL, FF_TILE), lambda i, e, j: (e, 0, j)),
                  pl.BlockSpec((1, D_MODEL, FF_TILE), lambda i, e, j: (e, 0, j)),
                  pl.BlockSpec((1, FF_TILE, D_MODEL), lambda i, e, j: (e, j, 0)),
                  pl.BlockSpec((MOE_ROWS, D_MODEL), lambda i, e, j: (i, 0)),
                  pl.BlockSpec((1, 1, D_MODEL), lambda i, e, j: (i // tpb, 0, 0)), vec, vec],
        out_specs=pl.BlockSpec((MOE_ROWS, D_MODEL), lambda i, e, j: (i, 0)),
        out_shape=jax.ShapeDtypeStruct((n, D_MODEL), F32),
        scratch_shapes=[pltpu.VMEM((MOE_ROWS, LANES), jnp.int32), pltpu.VMEM((LANES, MOE_ROWS), jnp.int32),
                        pltpu.VMEM((MOE_MAX_TILES, MOE_TILE, D_MODEL), BF16),
                        pltpu.VMEM((MOE_MAX_TILES, MOE_TILE, D_MODEL), F32)],
        compiler_params=_params(("parallel", "arbitrary", "arbitrary"), MOE_VMEM_LIMIT),
        name="moe",
    )(h, gates, w1, w3, w2, x, gate, lg.reshape(1, D_MODEL), lb.reshape(1, D_MODEL))


def _permute_w_in(w):
    pad = jnp.zeros((D_MODEL, U_COLS - ORIG_END), w.dtype)
    return jnp.concatenate([w[:, :ORIG_GATES], w[:, ORIG_SX:ORIG_DT], w[:, ORIG_SZ:ORIG_SX],
                            w[:, ORIG_GATES:ORIG_SZ], w[:, ORIG_DT:ORIG_END], pad], axis=1).astype(BF16)


def _small_row(vals, offset):
    v = vals.reshape(-1).astype(F32)
    return jnp.zeros((1, LANES), F32).at[0, offset:offset + v.shape[0]].set(v)


def _pack_mlstm_state(c, n, m):
    shape = c.shape[:2] + (D_STATE, HEAD_LANES)
    c_rows = jnp.swapaxes(c, 2, 3).reshape(shape)
    n_rows = jnp.broadcast_to(jnp.swapaxes(n, 2, 3)[..., None], c.shape[:2] + (D_STATE, N_HEADS, D_STATE))
    return (jnp.concatenate([c_rows, n_rows.reshape(shape)], axis=-1),
            jnp.repeat(m, D_STATE, axis=-1)[:, :, None, :])


def _pack_ssd_state(s):
    sel = (jnp.arange(N_GROUPS)[:, None] == jnp.arange(N_HEADS)[None, :] // 2).astype(F32)
    return jnp.einsum('bdhpn,gh->bdgnhp', s, sel).reshape(s.shape[:2] + (LANES, HEAD_LANES))


def _layer(x, mods, P, l, bsz, t, ctx, caches=None):
    sh1, sc1, g1, sh2, sc2, g2 = mods
    rows_per_mod = x.shape[0] // sh1.shape[0]
    lam_init = 0.8 - 0.6 * math.exp(-0.3 * l)
    u = _in_proj(x, sc1, sh1, P['w_in'][l], P['conv_w'][l], P['conv_b'][l], rows_per_mod, t)

    gate_bias = (_small_row(P['mlstm_gate_b'][l, 0], GATE_I) + _small_row(P['mlstm_gate_b'][l, 1], GATE_F))
    dt_bias = _small_row(P['ssm_dt_bias'][l], GATE_DT)
    alog = _small_row(P['ssm_A_log'][l], GATE_DT)
    dskip = jnp.repeat(P['ssm_D'][l].astype(F32), D_STATE).reshape(1, D_REC)

    if ctx is None:
        att, k_new, v_new = _attention_ctx(u, P['attn_lambda'][l], P['attn_norm_w'][l], lam_init, bsz, t, l,
                                           None if caches is None else caches[0:2])
        m_init = s_init = None
    else:
        ck, cv, c_c, c_n, c_m, c_s = ctx
        q, k, v = _rope_prep(u, bsz, t)
        k_all = jnp.concatenate([k, ck.astype(BF16)], axis=3)
        v_all = jnp.concatenate([v, cv.astype(BF16)], axis=2)
        att = _attention_lat(q, k_all, v_all, P['attn_lambda'][l], P['attn_norm_w'][l], lam_init, bsz, t)
        m_init = _pack_mlstm_state(c_c, c_n, c_m)
        s_init = _pack_ssd_state(c_s)
    hf, hb, *mlstm_caches = _mlstm(u, gate_bias, bsz, t, m_init, l, None if caches is None else caches[2:5])
    yf, yb, *ssd_caches = _ssd(u, dt_bias, alog, dskip, bsz, t, s_init, l,
                               None if caches is None else caches[5])

    x = _out_proj(att, hf, hb, yf, yb, u, P['w_out'][l], x, g1, P['mlstm_norm_w'][l], P['ssm_norm_w'][l],
                  P['ln_g'][l, 0], P['ln_b'][l, 0], rows_per_mod)
    if l % 2 == 0:
        x = _ffn(x, sc2, sh2, g2, P['ffn_w1'][l // 2], P['ffn_w3'][l // 2], P['ffn_w2'][l // 2],
                 P['ln_g'][l, 1], P['ln_b'][l, 1], rows_per_mod)
    else:
        gates, h2 = _router(x, sc2, sh2, P['router_w'][l // 2], rows_per_mod)
        x = _moe(h2, gates, P['moe_w1'][l // 2], P['moe_w3'][l // 2], P['moe_w2'][l // 2],
                 x, g2, P['ln_g'][l, 1], P['ln_b'][l, 1], rows_per_mod)
    if ctx is None:
        return x, (k_new, v_new, *mlstm_caches, *ssd_caches)
    return x, None


def kernel(x_prompt, x_sample, c, cache_attn_k, cache_attn_v, state_mlstm_C, state_mlstm_n, state_mlstm_m, state_ssm, c_ctx, w_ada, b_ada, w_in, w_out, attn_lambda, attn_norm_w, mlstm_gate_b, mlstm_norm_w, conv_w, conv_b, ssm_A_log, ssm_dt_bias, ssm_D, ssm_norm_w, ln_g, ln_b, ffn_w1, ffn_w3, ffn_w2, router_w, moe_w1, moe_w3, moe_w2):
    bsz, seq, _ = x_prompt.shape
    dbsz, dseq, _ = x_sample.shape
    P = dict(w_in=[_permute_w_in(w_in[l]) for l in range(DEPTH)], w_out=w_out.astype(BF16),
             attn_lambda=attn_lambda, attn_norm_w=attn_norm_w, mlstm_gate_b=mlstm_gate_b,
             mlstm_norm_w=mlstm_norm_w, conv_w=conv_w, conv_b=conv_b, ssm_A_log=ssm_A_log,
             ssm_dt_bias=ssm_dt_bias, ssm_D=ssm_D, ssm_norm_w=ssm_norm_w, ln_g=ln_g, ln_b=ln_b,
             ffn_w1=ffn_w1.astype(BF16), ffn_w3=ffn_w3.astype(BF16), ffn_w2=ffn_w2.astype(BF16),
             router_w=router_w, moe_w1=moe_w1.astype(BF16), moe_w3=moe_w3.astype(BF16),
             moe_w2=moe_w2.astype(BF16))

    cvec = jnp.zeros((8, D_MODEL), F32).at[0].set(c_ctx).at[1:1 + dbsz].set(c)
    mod = _modulation(cvec, w_ada, b_ada)

    def mods_for(l, lo, hi):
        return [mod[l, lo:hi, i * D_MODEL:(i + 1) * D_MODEL][:, None, :] for i in range(6)]

    y_prompt = x_prompt.reshape(bsz * seq, D_MODEL)
    caches = None
    for l in range(DEPTH):
        y_prompt, caches = _layer(y_prompt, mods_for(l, 0, 1), P, l, bsz, seq, None, caches)
    new_k, new_v, new_c, new_n, m_spread, new_s = caches

    y_sample = x_sample.reshape(dbsz * dseq, D_MODEL)
    for l in range(DEPTH):
        ctx = (cache_attn_k[:, l], cache_attn_v[:, l], state_mlstm_C[:, l], state_mlstm_n[:, l],
               state_mlstm_m[:, l], state_ssm[:, l])
        y_sample, _ = _layer(y_sample, mods_for(l, 1, 1 + dbsz), P, l, dbsz, dseq, ctx)

    return (y_prompt.reshape(bsz, seq, D_MODEL), y_sample.reshape(dbsz, dseq, D_MODEL),
            new_k, new_v, new_c, new_n, m_spread[:, :, :, 0, ::D_STATE], new_s)
```

```python
import functools
import math

import jax
import jax.numpy as jnp
from jax import lax
from jax.experimental import pallas as pl
from jax.experimental.pallas import tpu as pltpu

F32 = jnp.float32
BF16 = jnp.bfloat16
HIGHEST = lax.Precision.HIGHEST

D_MODEL = 1024
DEPTH = 2
GRID_W = 64
N_HEADS = 4
D_ATT = 512
D_HEAD_V = 128
D_QK = 64
D_REC = 256
D_STATE = 64
N_GROUPS = 2
D_CONV = 3
D_FF = 2816
N_EXPERTS = 8
ALPHA = (2.0 * DEPTH) ** 0.25
CHUNK = 64
ROPE_BASE = 10000.0
LOG2E = 1.4426950408889634
EPS = 1e-5

COL_AQ, COL_AK, COL_AV = 0, 512, 1024
COL_MQ, COL_MK, COL_MV, COL_MO = 1536, 1792, 2048, 2304
COL_SX, COL_SBC, COL_SZ = 2560, 2816, 3072
COL_SMALL = 3328
U_COLS = 3584
ORIG_GATES, ORIG_SZ, ORIG_SX, ORIG_DT, ORIG_END = 2560, 2576, 2832, 3344, 3352
GATE_I, GATE_F, GATE_DT = 0, 8, 16

LANES = 128
ROW_TILE = 512
SCAN_ROWS = 512
SCAN_BATCH = 2
VMEM_LIMIT = 48 * 1024 * 1024

NT_DIMS = (((1,), (1,)), ((), ()))
TN_DIMS = (((0,), (0,)), ((), ()))


def _params(sem, vmem=VMEM_LIMIT):
    return pltpu.CompilerParams(dimension_semantics=sem, vmem_limit_bytes=vmem)


def _silu(x):
    return x * jax.nn.sigmoid(x)


def _bdot(a, b):
    return jnp.dot(a.astype(BF16), b.astype(BF16), preferred_element_type=F32)


def _bdot_nt(a, b):
    return lax.dot_general(a.astype(BF16), b.astype(BF16), NT_DIMS, preferred_element_type=F32)


def _layernorm_rows(y, g, b):
    mu = jnp.mean(y, -1, keepdims=True)
    d = y - mu
    var = jnp.mean(d * d, -1, keepdims=True)
    return d * lax.rsqrt(var + EPS) * g + b


def _mod_kernel(c_ref, w_ref, b_ref, o_ref):
    o_ref[0] = jnp.dot(_silu(c_ref[...]), w_ref[0], precision=HIGHEST,
                       preferred_element_type=F32) + b_ref[0]


def _modulation(cvec, w_ada, b_ada):
    tn = 1536
    return pl.pallas_call(
        _mod_kernel,
        grid=(DEPTH, 6 * D_MODEL // tn),
        in_specs=[pl.BlockSpec((8, D_MODEL), lambda l, j: (0, 0)),
                  pl.BlockSpec((1, D_MODEL, tn), lambda l, j: (l, 0, j)),
                  pl.BlockSpec((1, 1, tn), lambda l, j: (l, 0, j))],
        out_specs=pl.BlockSpec((1, 8, tn), lambda l, j: (l, 0, j)),
        out_shape=jax.ShapeDtypeStruct((DEPTH, 8, 6 * D_MODEL), F32),
        compiler_params=_params(("parallel", "parallel")),
        name="modulation",
    )(cvec, w_ada, b_ada.reshape(DEPTH, 1, 6 * D_MODEL))


CONV_COLS = 2 * D_REC
HALO = 16


def _inproj_kernel(x_ref, prev_ref, next_ref, sc_ref, sh_ref, w_ref, cw_ref, cb_ref, o_ref, *, seq):
    def modulate(x):
        return (x * (1.0 + sc_ref[0]) + sh_ref[0]).astype(BF16)

    h = modulate(x_ref[...])
    for n0 in range(0, U_COLS, CONV_COLS):
        if n0 != COL_SX:
            o_ref[:, n0:n0 + CONV_COLS] = jnp.dot(h, w_ref[:, n0:n0 + CONV_COLS], preferred_element_type=F32)
            continue
        h_ext = jnp.concatenate([h, modulate(prev_ref[...]), modulate(next_ref[...])], axis=0)
        pre_ext = jnp.dot(h_ext, w_ref[:, n0:n0 + CONV_COLS], preferred_element_type=F32)
        pre = pre_ext[:ROW_TILE]
        before = pre_ext[ROW_TILE + HALO - 1:ROW_TILE + HALO]
        after = pre_ext[ROW_TILE + HALO:ROW_TILE + HALO + 1]
        r = lax.broadcasted_iota(jnp.int32, (ROW_TILE, 1), 0)
        pos = (r + pl.program_id(0) * ROW_TILE) % seq
        up = jnp.where(r == 0, before, pltpu.roll(pre, 1, 0))
        down = jnp.where(r == ROW_TILE - 1, after, pltpu.roll(pre, ROW_TILE - 1, 0))
        up = jnp.where(pos == 0, 0.0, up)
        down = jnp.where(pos == seq - 1, 0.0, down)
        cw = cw_ref[...]
        o_ref[:, n0:n0 + CONV_COLS] = _silu(up * cw[0:1] + pre * cw[1:2] + down * cw[2:3] + cb_ref[...])


def _in_proj(x, sc, sh, w, conv_w, conv_b, rows_per_mod, seq):
    n = x.shape[0]
    tpb = rows_per_mod // ROW_TILE
    r8 = ROW_TILE // HALO
    last8 = n // HALO - 1
    assert COL_SX % CONV_COLS == 0 and U_COLS % CONV_COLS == 0
    return pl.pallas_call(
        functools.partial(_inproj_kernel, seq=seq),
        grid=(n // ROW_TILE,),
        in_specs=[pl.BlockSpec((ROW_TILE, D_MODEL), lambda i: (i, 0)),
                  pl.BlockSpec((HALO, D_MODEL), lambda i: (jnp.maximum(i * r8 - 1, 0), 0)),
                  pl.BlockSpec((HALO, D_MODEL), lambda i: (jnp.minimum((i + 1) * r8, last8), 0)),
                  pl.BlockSpec((1, 1, D_MODEL), lambda i: (i // tpb, 0, 0)),
                  pl.BlockSpec((1, 1, D_MODEL), lambda i: (i // tpb, 0, 0)),
                  pl.BlockSpec((D_MODEL, U_COLS), lambda i: (0, 0)),
                  pl.BlockSpec((D_CONV, CONV_COLS), lambda i: (0, 0)),
                  pl.BlockSpec((1, CONV_COLS), lambda i: (0, 0))],
        out_specs=pl.BlockSpec((ROW_TILE, U_COLS), lambda i: (i, 0)),
        out_shape=jax.ShapeDtypeStruct((n, U_COLS), F32),
        compiler_params=_params(("parallel",)),
        name="in_proj",
    )(x, x, x, sc, sh, w, conv_w, conv_b.reshape(1, CONV_COLS))


def _lambda_scalar(lam_ref, lam_init):
    lp = lam_ref[...]
    s01 = jnp.sum(lp[0:1] * lp[1:2], axis=-1, keepdims=True)
    s23 = jnp.sum(lp[2:3] * lp[3:4], axis=-1, keepdims=True)
    return jnp.exp(s01) - jnp.exp(s23) + lam_init


def _softmax_rows(s):
    e = jnp.exp(s - jnp.max(s, -1, keepdims=True))
    return e / jnp.sum(e, -1, keepdims=True)


def _head_norm(o, nw, lam_init):
    return o * lax.rsqrt(jnp.mean(o * o, -1, keepdims=True) + EPS) * nw * (1.0 - lam_init)


def _layer_cache(tail, l, bsz, nb=1):
    zeros = (0,) * len(tail)
    owned = DEPTH if l == 0 else 1
    spec = pl.BlockSpec((nb, owned) + tuple(tail), lambda b, *_: (b, l) + zeros)
    return jax.ShapeDtypeStruct((bsz, DEPTH) + tuple(tail), F32), spec, owned


def _zero_later_layers(ref):
    for b in range(ref.shape[0]):
        for later in range(1, ref.shape[1]):
            ref[b, later] = jnp.zeros(ref.shape[2:], ref.dtype)


def _attn_ctx_kernel(u_ref, lam_ref, nw_ref, *rest, lam_init):
    att_ref, k_ref, v_ref = rest[-3:]
    lam = _lambda_scalar(lam_ref, lam_init)
    for h in range(N_HEADS):
        v = u_ref[:, COL_AV + h * D_HEAD_V:COL_AV + (h + 1) * D_HEAD_V]
        v_ref[0, 0, h] = v
        ps = []
        for m in range(2):
            c0 = h * D_HEAD_V + m * D_QK
            q = u_ref[:, COL_AQ + c0:COL_AQ + c0 + D_QK] * (D_QK ** -0.5)
            k = u_ref[:, COL_AK + c0:COL_AK + c0 + D_QK]
            k_ref[0, 0, h, m] = k
            ps.append(_softmax_rows(_bdot_nt(q, k)))
        o = _bdot(ps[0] - lam * ps[1], v)
        att_ref[:, h * D_HEAD_V:(h + 1) * D_HEAD_V] = _head_norm(o, nw_ref[...], lam_init).astype(BF16)
    _zero_later_layers(k_ref)
    _zero_later_layers(v_ref)


def _attention_ctx(u, lam_p, norm_w, lam_init, bsz, t, l, caches):
    n = bsz * t
    k_shape, k_spec, _ = _layer_cache((N_HEADS, 2, t, D_QK), l, bsz)
    v_shape, v_spec, _ = _layer_cache((N_HEADS, t, D_HEAD_V), l, bsz)
    in_specs = [pl.BlockSpec((t, 3 * D_ATT), lambda b: (b, 0)),
                pl.BlockSpec((4, D_QK), lambda b: (0, 0)),
                pl.BlockSpec((1, D_HEAD_V), lambda b: (0, 0))]
    args = [u, lam_p, norm_w.reshape(1, D_HEAD_V)]
    aliases = {}
    if caches is not None:
        in_specs += [pl.BlockSpec(memory_space=pl.ANY)] * 2
        aliases = {len(args): 1, len(args) + 1: 2}
        args += list(caches)
    return pl.pallas_call(
        functools.partial(_attn_ctx_kernel, lam_init=lam_init),
        grid=(bsz,),
        in_specs=in_specs,
        out_specs=[pl.BlockSpec((t, D_ATT), lambda b: (b, 0)), k_spec, v_spec],
        out_shape=[jax.ShapeDtypeStruct((n, D_ATT), BF16), k_shape, v_shape],
        input_output_aliases=aliases,
        compiler_params=_params(("parallel",)),
        name="attn_ctx",
    )(*args)


def _rope_kernel(u_ref, cos_ref, sa_ref, sb_ref, q_ref, k_ref, v_ref):
    cos, sa, sb = cos_ref[...], sa_ref[...], sb_ref[...]

    def rope(x):
        return x * cos + pltpu.roll(x, LANES - 16, 1) * sa + pltpu.roll(x, 16, 1) * sb

    for h in range(N_HEADS):
        q = rope(u_ref[:, COL_AQ + h * D_HEAD_V:COL_AQ + (h + 1) * D_HEAD_V]) * (LOG2E * D_QK ** -0.5)
        k = rope(u_ref[:, COL_AK + h * D_HEAD_V:COL_AK + (h + 1) * D_HEAD_V])
        for m in range(2):
            q_ref[0, h, m] = q[:, m * D_QK:(m + 1) * D_QK].astype(BF16)
            k_ref[0, h, m] = k[:, m * D_QK:(m + 1) * D_QK].astype(BF16)
        v_ref[0, h] = u_ref[:, COL_AV + h * D_HEAD_V:COL_AV + (h + 1) * D_HEAD_V].astype(BF16)


def _rope_tables(t):
    rows = jnp.repeat(jnp.arange(t // GRID_W, dtype=F32), GRID_W)
    cols = jnp.tile(jnp.arange(GRID_W, dtype=F32), t // GRID_W)
    half = D_QK // 2
    inv = ROPE_BASE ** (-jnp.arange(0, half, 2, dtype=F32) / half)
    ang_r = rows[:, None] * inv
    ang_c = cols[:, None] * inv
    ang = jnp.concatenate([ang_r, ang_r, ang_c, ang_c], -1)
    cos, sin = jnp.cos(ang), jnp.sin(ang)
    quarter = (jnp.arange(D_QK) // (D_QK // 4)) % 2
    sa = jnp.where(quarter == 0, -sin, 0.0)
    sb = jnp.where(quarter == 1, sin, 0.0)
    tile2 = lambda a: jnp.concatenate([a, a], -1)
    return tile2(cos), tile2(sa), tile2(sb)


def _rope_prep(u, bsz, t):
    tr = 512
    nb = t // tr
    cos, sa, sb = _rope_tables(t)
    tab = pl.BlockSpec((tr, LANES), lambda b, i: (i, 0))
    return pl.pallas_call(
        _rope_kernel,
        grid=(bsz, nb),
        in_specs=[pl.BlockSpec((tr, 3 * D_ATT), lambda b, i: (b * nb + i, 0)), tab, tab, tab],
        out_specs=[pl.BlockSpec((1, N_HEADS, 2, tr, D_QK), lambda b, i: (b, 0, 0, i, 0)),
                   pl.BlockSpec((1, N_HEADS, 2, tr, D_QK), lambda b, i: (b, 0, 0, i, 0)),
                   pl.BlockSpec((1, N_HEADS, tr, D_HEAD_V), lambda b, i: (b, 0, i, 0))],
        out_shape=[jax.ShapeDtypeStruct((bsz, N_HEADS, 2, t, D_QK), BF16),
                   jax.ShapeDtypeStruct((bsz, N_HEADS, 2, t, D_QK), BF16),
                   jax.ShapeDtypeStruct((bsz, N_HEADS, t, D_HEAD_V), BF16)],
        compiler_params=_params(("parallel", "parallel")),
        name="rope_prep",
    )(u, cos, sa, sb)


def _attn_lat_kernel(q_ref, k_ref, v_ref, lam_ref, nw_ref, o_ref, *, lam_init):
    lam = _lambda_scalar(lam_ref, lam_init)
    for hh in range(ATTN_HEADS):
        es, sums = [], []
        for m in range(2):
            s = lax.dot_general(q_ref[0, hh, m], k_ref[0, hh, m], NT_DIMS, preferred_element_type=F32)
            e = jnp.exp2(s - jnp.max(s, -1, keepdims=True))
            es.append(e)
            sums.append(jnp.sum(e, -1, keepdims=True))
        a = es[0] - (lam * sums[0] / sums[1]) * es[1]
        o = _bdot(a, v_ref[0, hh]) / sums[0]
        o_ref[:, hh * D_HEAD_V:(hh + 1) * D_HEAD_V] = _head_norm(o, nw_ref[...], lam_init).astype(BF16)


ATTN_HEADS = 4


def _attention_lat(q, k_all, v_all, lam_p, norm_w, lam_init, bsz, t):
    tq = 256
    nq = t // tq
    s = k_all.shape[3]
    nh = ATTN_HEADS
    return pl.pallas_call(
        functools.partial(_attn_lat_kernel, lam_init=lam_init),
        grid=(bsz, N_HEADS // nh, nq),
        in_specs=[pl.BlockSpec((1, nh, 2, tq, D_QK), lambda b, h, i: (b, h, 0, i, 0)),
                  pl.BlockSpec((1, nh, 2, s, D_QK), lambda b, h, i: (b, h, 0, 0, 0)),
                  pl.BlockSpec((1, nh, s, D_HEAD_V), lambda b, h, i: (b, h, 0, 0)),
                  pl.BlockSpec((4, D_QK), lambda b, h, i: (0, 0)),
                  pl.BlockSpec((1, D_HEAD_V), lambda b, h, i: (0, 0))],
        out_specs=pl.BlockSpec((tq, nh * D_HEAD_V), lambda b, h, i: (b * nq + i, h)),
        out_shape=jax.ShapeDtypeStruct((bsz * t, D_ATT), BF16),
        compiler_params=_params(("parallel", "parallel", "parallel")),
        name="attn_lat",
    )(q, k_all, v_all, lam_p, norm_w.reshape(1, D_HEAD_V))


HEAD_LANES = N_HEADS * D_STATE


def _scan_consts():
    t = lax.broadcasted_iota(jnp.int32, (CHUNK, HEAD_LANES), 0)
    s = lax.broadcasted_iota(jnp.int32, (CHUNK, HEAD_LANES), 1) & (CHUNK - 1)
    r = lax.broadcasted_iota(jnp.int32, (CHUNK, CHUNK), 0)
    c = lax.broadcasted_iota(jnp.int32, (CHUNK, CHUNK), 1)
    reads = (jnp.where(s <= t, 1.0, 0.0), jnp.where(s >= t, 1.0, 0.0))
    block = (jnp.where(s <= t, 0.0, -jnp.inf), jnp.where(s >= t, 0.0, -jnp.inf))
    tri = (jnp.where(c <= r, 1.0, 0.0).astype(BF16), jnp.where(c >= r, 1.0, 0.0).astype(BF16))
    return reads, block, tri, jnp.where(s == t, 1.0, 0.0)


def _cumsum_rows(tri, x):
    hi = x.astype(BF16)
    rest = x - hi.astype(F32)
    mid = rest.astype(BF16)
    lo = (rest - mid.astype(F32)).astype(BF16)
    parts = jnp.dot(tri, jnp.concatenate([hi, mid, lo], axis=1), preferred_element_type=F32)
    return parts[:, :HEAD_LANES] + parts[:, HEAD_LANES:2 * HEAD_LANES] + parts[:, 2 * HEAD_LANES:]


def _group_mask(rows, cols, row_shift, col_shift, dtype):
    r = lax.broadcasted_iota(jnp.int32, (rows, cols), 0) >> 6
    c = (lax.broadcasted_iota(jnp.int32, (rows, cols), 1) >> 6) & (N_HEADS - 1)
    return jnp.where((r >> row_shift) == (c >> col_shift), 1.0, 0.0).astype(dtype)


def _spread(x, chans):
    return jnp.concatenate([jnp.broadcast_to(x[:, c:c + 1], (CHUNK, D_STATE)) for c in chans], axis=1)


def _stack_heads(x):
    return jnp.concatenate([x] * N_HEADS, axis=0)


def _mlstm_chunk(q4, k4, v4, g, d, cn_prev, m_prev, reads, block, tri, eye, bd, bd_f32):
    li = _spread(g, [GATE_I + d * N_HEADS + h for h in range(N_HEADS)])
    lf = _spread(jax.nn.log_sigmoid(g), [GATE_F + d * N_HEADS + h for h in range(N_HEADS)])
    bc = _cumsum_rows(tri[d], lf)
    btot = jnp.sum(lf, 0, keepdims=True)
    b_row = jnp.sum(reads[1 - d] * lf, 0, keepdims=True)
    li_row = jnp.sum(eye * li, 0, keepdims=True)
    dm = bc - b_row + li_row + block[d]
    rmax = jnp.concatenate(
        [jnp.broadcast_to(jnp.max(dm[:, h * D_STATE:(h + 1) * D_STATE], -1, keepdims=True), (CHUNK, D_STATE))
         for h in range(N_HEADS)], axis=1)
    inter = bc + m_prev
    m_t = jnp.maximum(inter, rmax)
    w_inter = jnp.exp(inter - m_t)
    qs = (q4 * (D_STATE ** -0.5)).astype(BF16)
    kbd = bd[:, :HEAD_LANES] * _stack_heads(k4.astype(BF16))
    s4 = lax.dot_general(qs, kbd, NT_DIMS, preferred_element_type=F32) * jnp.exp(dm - m_t)
    vo = jnp.concatenate([v4.astype(BF16), jnp.ones((CHUNK, HEAD_LANES), BF16)], axis=1)
    vbd = bd * _stack_heads(vo)
    nd = (jnp.concatenate([w_inter, w_inter], axis=1)
          * jnp.dot(qs, bd * _stack_heads(cn_prev.astype(BF16)), preferred_element_type=F32)
          + jnp.dot(s4.astype(BF16), vbd, preferred_element_type=F32))
    hc = nd[:, :HEAD_LANES] / jnp.maximum(jnp.abs(nd[:, HEAD_LANES:]), jnp.exp(-m_t))
    gcol = btot - bc + li
    m_new = jnp.maximum(btot + m_prev, jnp.max(gcol, 0, keepdims=True))
    w_c = jnp.exp(btot + m_prev - m_new)
    kw = (k4 * jnp.exp(gcol - m_new)).astype(BF16)
    dcn = lax.dot_general(kw, vo, TN_DIMS, preferred_element_type=F32)
    own = sum(bd_f32[h * D_STATE:(h + 1) * D_STATE] * dcn[h * D_STATE:(h + 1) * D_STATE]
              for h in range(N_HEADS))
    cn_new = jnp.concatenate([w_c, w_c], axis=1) * cn_prev + own
    return hc, cn_new, m_new


def _mlstm_kernel(*refs, nblk, rows, zero_init, n_alias):
    qf_ref, kf_ref, vf_ref, gf_ref, qb_ref, kb_ref, vb_ref, gb_ref, bias_ref = refs[:9]
    if not zero_init:
        c0_ref, m0_ref = refs[9:11]
    outs = refs[9 + (0 if zero_init else 2) + n_alias:-2]
    hf_ref, hb_ref = outs[:2]
    cn_scr, m_scr = refs[-2:]
    j = pl.program_id(1)
    nchunk = rows // CHUNK

    @pl.when(j == 0)
    def _():
        if zero_init:
            cn_scr[...] = jnp.zeros_like(cn_scr)
            m_scr[...] = jnp.zeros_like(m_scr)
        else:
            cn_scr[...] = c0_ref[...]
            m_scr[...] = m0_ref[...]

    reads, block, tri, eye = _scan_consts()
    bd = _group_mask(HEAD_LANES, 2 * HEAD_LANES, 0, 0, BF16)
    bd_f32 = _group_mask(HEAD_LANES, 2 * HEAD_LANES, 0, 0, F32)

    chains = [(b, d) for b in range(SCAN_BATCH) for d in range(2)]
    state = {bd_: (cn_scr[bd_], m_scr[bd_]) for bd_ in chains}
    for ci in range(nchunk):
        for b, d in chains:
            q_ref, k_ref, v_ref, g_ref, h_ref = ((qf_ref, kf_ref, vf_ref, gf_ref, hf_ref) if d == 0
                                                 else (qb_ref, kb_ref, vb_ref, gb_ref, hb_ref))
            cj = ci if d == 0 else nchunk - 1 - ci
            rs = slice(cj * CHUNK, (cj + 1) * CHUNK)
            g = g_ref[b, rs, :] + bias_ref[...]
            hc, cn_new, m_new = _mlstm_chunk(q_ref[b, rs, :], k_ref[b, rs, :], v_ref[b, rs, :], g, d,
                                             *state[b, d], reads, block, tri, eye, bd, bd_f32)
            h_ref[b, rs, :] = hc
            state[b, d] = (cn_new, m_new)
    for bd_ in chains:
        cn_scr[bd_], m_scr[bd_] = state[bd_]

    if zero_init:
        cout_ref, nout_ref, mout_ref = outs[2:]

        @pl.when(j == nblk - 1)
        def _():
            r = lax.broadcasted_iota(jnp.int32, (D_STATE, D_STATE), 0)
            c = lax.broadcasted_iota(jnp.int32, (D_STATE, D_STATE), 1)
            for b, d in chains:
                for h in range(N_HEADS):
                    r0, r1 = h * D_STATE, (h + 1) * D_STATE
                    cout_ref[b, 0, d, h] = cn_scr[b, d, :, r0:r1]
                    n_spread = cn_scr[b, d, :, HEAD_LANES + r0:HEAD_LANES + r1]
                    nout_ref[b, 0, d, h:h + 1, :] = jnp.sum(jnp.where(r == c, n_spread, 0.0), 0, keepdims=True)
                mout_ref[b, 0, d] = m_scr[b, d]
            _zero_later_layers(cout_ref)
            _zero_later_layers(nout_ref)
            _zero_later_layers(mout_ref)


def _mlstm(u, gate_bias, bsz, t, init, l=0, caches=None):
    rows = min(t, SCAN_ROWS)
    nblk = t // rows
    zero_init = init is None

    nb = SCAN_BATCH
    fwd = lambda col: (lambda p, j: (p, j, col))
    bwd = lambda col: (lambda p, j: (p, nblk - 1 - j, col))
    cq, ck, cv, cg = COL_MQ // D_REC, COL_MK // D_REC, COL_MV // D_REC, COL_SMALL // LANES
    in_specs = []
    for mk in (fwd, bwd):
        in_specs += [pl.BlockSpec((nb, rows, D_REC), mk(cq)), pl.BlockSpec((nb, rows, D_REC), mk(ck)),
                     pl.BlockSpec((nb, rows, D_REC), mk(cv)), pl.BlockSpec((nb, rows, LANES), mk(cg))]
    in_specs.append(pl.BlockSpec((1, LANES), lambda p, j: (0, 0)))
    args = [u.reshape(bsz, t, U_COLS)] * 8 + [gate_bias]
    state_c = pl.BlockSpec((nb, 2, D_STATE, 2 * HEAD_LANES), lambda p, j: (p, 0, 0, 0))
    state_m = pl.BlockSpec((nb, 2, 1, HEAD_LANES), lambda p, j: (p, 0, 0, 0))
    out_specs = [pl.BlockSpec((nb, rows, D_REC), fwd(0)), pl.BlockSpec((nb, rows, D_REC), bwd(0))]
    out_shape = [jax.ShapeDtypeStruct((bsz, t, D_REC), F32), jax.ShapeDtypeStruct((bsz, t, D_REC), F32)]
    aliases = {}
    if zero_init:
        for tail in ((2, N_HEADS, D_STATE, D_STATE), (2, N_HEADS, D_STATE), (2, 1, HEAD_LANES)):
            shape, spec, _ = _layer_cache(tail, l, bsz, nb)
            out_shape.append(shape)
            out_specs.append(spec)
        if caches is not None:
            in_specs += [pl.BlockSpec(memory_space=pl.ANY)] * len(caches)
            aliases = {len(args) + i: 2 + i for i in range(len(caches))}
            args += list(caches)
    else:
        in_specs += [state_c, state_m]
        args += list(init)
    hf, hb, *state_out = pl.pallas_call(
        functools.partial(_mlstm_kernel, nblk=nblk, rows=rows, zero_init=zero_init, n_alias=len(aliases)),
        grid=(bsz // nb, nblk),
        in_specs=in_specs,
        out_specs=out_specs,
        out_shape=out_shape,
        input_output_aliases=aliases,
        scratch_shapes=[pltpu.VMEM((nb, 2, D_STATE, 2 * HEAD_LANES), F32),
                        pltpu.VMEM((nb, 2, 1, HEAD_LANES), F32)],
        compiler_params=_params(("parallel", "arbitrary")),
        name="mlstm_scan",
    )(*args)
    return (hf.reshape(bsz * t, D_REC), hb.reshape(bsz * t, D_REC), *state_out)


def _ssd_chunk(x4, bcm, dt128, da128, d, sg_prev, reads, block, tri, b_sel, s_sel, bd):
    chans = [GATE_DT + d * N_HEADS + h for h in range(N_HEADS)]
    dt = _spread(dt128, chans)
    da = _spread(da128, chans)
    ac = _cumsum_rows(tri[d], da)
    atot = jnp.sum(da, 0, keepdims=True)
    a_row = jnp.sum(reads[1 - d] * da, 0, keepdims=True)
    decay = jnp.exp(ac - a_row + block[d])
    bmat = bcm[:, :LANES].astype(BF16)
    cmat = bcm[:, LANES:].astype(BF16)
    bbd = b_sel * _stack_heads(bmat)
    g4 = lax.dot_general(cmat, bbd, NT_DIMS, preferred_element_type=F32)
    xbd = bd * _stack_heads((x4 * dt).astype(BF16))
    y = (jnp.dot((g4 * decay).astype(BF16), xbd, preferred_element_type=F32)
         + jnp.dot(cmat, sg_prev.astype(BF16), preferred_element_type=F32) * jnp.exp(ac))
    w = jnp.exp(atot - ac) * dt
    dsg = lax.dot_general(bmat, (x4 * w).astype(BF16), TN_DIMS, preferred_element_type=F32)
    sg_new = jnp.exp(atot) * sg_prev + s_sel * dsg
    return y, sg_new


def _ssd_kernel(*refs, nblk, rows, zero_init, n_alias):
    xf_ref, bcf_ref, gf_ref, xb_ref, bcb_ref, gb_ref, dtb_ref, alog_ref, dskip_ref = refs[:9]
    if not zero_init:
        s0_ref = refs[9]
    outs = refs[9 + (0 if zero_init else 1) + n_alias:-1]
    yf_ref, yb_ref = outs[:2]
    s_scr = refs[-1]
    j = pl.program_id(1)
    nchunk = rows // CHUNK

    @pl.when(j == 0)
    def _():
        if zero_init:
            s_scr[...] = jnp.zeros_like(s_scr)
        else:
            s_scr[...] = s0_ref[0]

    reads, block, tri, _ = _scan_consts()
    bd = _group_mask(HEAD_LANES, HEAD_LANES, 0, 0, BF16)
    b_sel = _group_mask(HEAD_LANES, LANES, 1, 0, BF16)
    s_sel = _group_mask(LANES, HEAD_LANES, 0, 1, F32)
    a_coef = -jnp.exp(alog_ref[...])

    state = [s_scr[d] for d in range(2)]
    for ci in range(nchunk):
        for d in range(2):
            x_ref, bc_ref, g_ref, y_ref = ((xf_ref, bcf_ref, gf_ref, yf_ref) if d == 0
                                           else (xb_ref, bcb_ref, gb_ref, yb_ref))
            cj = ci if d == 0 else nchunk - 1 - ci
            rs = slice(cj * CHUNK, (cj + 1) * CHUNK)
            dt128 = jax.nn.softplus(g_ref[rs, :] + dtb_ref[...])
            x4 = x_ref[rs, :]
            y, state[d] = _ssd_chunk(x4, bc_ref[rs, :], dt128, dt128 * a_coef, d, state[d],
                                     reads, block, tri, b_sel, s_sel, bd)
            if d == 0:
                y = y + dskip_ref[...] * x4
            y_ref[rs, :] = y
    for d in range(2):
        s_scr[d] = state[d]

    if zero_init:
        sout_ref = outs[2]

        @pl.when(j == nblk - 1)
        def _():
            for d in range(2):
                s_t = s_scr[d].T
                for h in range(N_HEADS):
                    g0 = (h // 2) * D_STATE
                    sout_ref[0, 0, d, h] = s_t[h * D_STATE:(h + 1) * D_STATE, g0:g0 + D_STATE]
            _zero_later_layers(sout_ref)


def _ssd(u, dt_bias_row, alog_row, dskip_row, bsz, t, init, l=0, cache=None):
    rows = min(t, SCAN_ROWS)
    nblk = t // rows
    zero_init = init is None

    def fwd(col):
        return lambda b, j: (b * nblk + j, col)

    def bwd(col):
        return lambda b, j: (b * nblk + nblk - 1 - j, col)

    in_specs = []
    for mk in (fwd, bwd):
        in_specs += [pl.BlockSpec((rows, D_REC), mk(COL_SX // D_REC)),
                     pl.BlockSpec((rows, D_REC), mk(COL_SBC // D_REC)),
                     pl.BlockSpec((rows, LANES), mk(COL_SMALL // LANES))]
    in_specs += [pl.BlockSpec((1, LANES), lambda b, j: (0, 0)),
                 pl.BlockSpec((1, LANES), lambda b, j: (0, 0)),
                 pl.BlockSpec((1, D_REC), lambda b, j: (0, 0))]
    args = [u] * 6 + [dt_bias_row, alog_row, dskip_row]
    out_specs = [pl.BlockSpec((rows, D_REC), fwd(0)), pl.BlockSpec((rows, D_REC), bwd(0))]
    out_shape = [jax.ShapeDtypeStruct((bsz * t, D_REC), F32), jax.ShapeDtypeStruct((bsz * t, D_REC), F32)]
    aliases = {}
    if zero_init:
        shape, spec, _ = _layer_cache((2, N_HEADS, D_STATE, D_STATE), l, bsz)
        out_shape.append(shape)
        out_specs.append(spec)
        if cache is not None:
            in_specs.append(pl.BlockSpec(memory_space=pl.ANY))
            aliases = {len(args): 2}
            args.append(cache)
    else:
        in_specs.append(pl.BlockSpec((1, 2, LANES, HEAD_LANES), lambda b, j: (b, 0, 0, 0)))
        args.append(init)
    return pl.pallas_call(
        functools.partial(_ssd_kernel, nblk=nblk, rows=rows, zero_init=zero_init, n_alias=len(aliases)),
        grid=(bsz, nblk),
        in_specs=in_specs,
        out_specs=out_specs,
        out_shape=out_shape,
        input_output_aliases=aliases,
        scratch_shapes=[pltpu.VMEM((2, LANES, HEAD_LANES), F32)],
        compiler_params=_params(("parallel", "arbitrary")),
        name="ssd_scan",
    )(*args)


def _outproj_kernel(att_ref, hf_ref, hb_ref, mo_ref, yf_ref, yb_ref, z_ref, w_ref, x_ref, gate_ref,
                    mnw_ref, snw_ref, lg_ref, lb_ref, o_ref):
    hh = hf_ref[...] + hb_ref[...]
    parts = []
    for h in range(N_HEADS):
        xh = hh[:, h * D_STATE:(h + 1) * D_STATE]
        mu = jnp.mean(xh, -1, keepdims=True)
        dlt = xh - mu
        var = jnp.mean(dlt * dlt, -1, keepdims=True)
        parts.append(dlt * lax.rsqrt(var + EPS))
    ml = jax.nn.sigmoid(mo_ref[...]) * jnp.concatenate(parts, axis=1) * mnw_ref[...]
    yz = (yf_ref[...] + yb_ref[...]) * _silu(z_ref[...])
    parts = []
    for grp in range(N_GROUPS):
        yg = yz[:, grp * LANES:(grp + 1) * LANES]
        parts.append(yg * lax.rsqrt(jnp.mean(yg * yg, -1, keepdims=True) + EPS))
    ssm = jnp.concatenate(parts, axis=1) * snw_ref[...]
    mixed = (jnp.dot(att_ref[...], w_ref[0:D_ATT], preferred_element_type=F32)
             + _bdot(ml, w_ref[D_ATT:D_ATT + D_REC])
             + _bdot(ssm, w_ref[D_ATT + D_REC:D_MODEL]))
    y = ALPHA * x_ref[...] + gate_ref[0] * mixed
    o_ref[...] = _layernorm_rows(y, lg_ref[...], lb_ref[...])


def _out_proj(att, hf, hb, yf, yb, u, w, x, gate, mnw, snw, lg, lb, rows_per_mod):
    n = x.shape[0]
    tpb = rows_per_mod // ROW_TILE
    row = lambda width, col: pl.BlockSpec((ROW_TILE, width), lambda i: (i, col))
    vec = lambda width: pl.BlockSpec((1, width), lambda i: (0, 0))
    return pl.pallas_call(
        _outproj_kernel,
        grid=(n // ROW_TILE,),
        in_specs=[row(D_ATT, 0), row(D_REC, 0), row(D_REC, 0), row(D_REC, COL_MO // D_REC),
                  row(D_REC, 0), row(D_REC, 0), row(D_REC, COL_SZ // D_REC),
                  pl.BlockSpec((D_MODEL, D_MODEL), lambda i: (0, 0)),
                  row(D_MODEL, 0),
                  pl.BlockSpec((1, 1, D_MODEL), lambda i: (i // tpb, 0, 0)),
                  vec(D_REC), vec(D_REC), vec(D_MODEL), vec(D_MODEL)],
        out_specs=row(D_MODEL, 0),
        out_shape=jax.ShapeDtypeStruct((n, D_MODEL), F32),
        compiler_params=_params(("parallel",)),
        name="out_proj",
    )(att, hf, hb, u, yf, yb, u, w, x, gate, mnw.reshape(1, D_REC), snw.reshape(1, D_REC),
      lg.reshape(1, D_MODEL), lb.reshape(1, D_MODEL))


FF_TILE = D_FF // 2
FFN_ROWS = 1024
FFN_SUB = 512
FFN_VMEM_LIMIT = 56 * 1024 * 1024


def _swiglu_partial(h, w1, w3, w2):
    a = jnp.dot(h, w1, preferred_element_type=F32)
    b = jnp.dot(h, w3, preferred_element_type=F32)
    return jnp.dot((_silu(a) * b).astype(BF16), w2, preferred_element_type=F32)


def _ffn_kernel(x_ref, sc_ref, sh_ref, gate_ref, w1_ref, w3_ref, w2_ref, lg_ref, lb_ref, o_ref):
    for r0 in range(0, FFN_ROWS, FFN_SUB):
        rows = slice(r0, r0 + FFN_SUB)
        x = x_ref[rows, :]
        h = (x * (1.0 + sc_ref[0]) + sh_ref[0]).astype(BF16)
        y = ALPHA * x + gate_ref[0] * _swiglu_partial(h, w1_ref[...], w3_ref[...], w2_ref[...])
        o_ref[rows, :] = _layernorm_rows(y, lg_ref[...], lb_ref[...])


def _ffn(x, sc, sh, gate, w1, w3, w2, lg, lb, rows_per_mod):
    n = x.shape[0]
    tpb = rows_per_mod // FFN_ROWS
    modspec = pl.BlockSpec((1, 1, D_MODEL), lambda i: (i // tpb, 0, 0))
    vec = pl.BlockSpec((1, D_MODEL), lambda i: (0, 0))
    once = pl.Buffered(1)
    return pl.pallas_call(
        _ffn_kernel,
        grid=(n // FFN_ROWS,),
        in_specs=[pl.BlockSpec((FFN_ROWS, D_MODEL), lambda i: (i, 0)), modspec, modspec, modspec,
                  pl.BlockSpec((D_MODEL, D_FF), lambda i: (0, 0), pipeline_mode=once),
                  pl.BlockSpec((D_MODEL, D_FF), lambda i: (0, 0), pipeline_mode=once),
                  pl.BlockSpec((D_FF, D_MODEL), lambda i: (0, 0), pipeline_mode=once), vec, vec],
        out_specs=pl.BlockSpec((FFN_ROWS, D_MODEL), lambda i: (i, 0)),
        out_shape=jax.ShapeDtypeStruct((n, D_MODEL), F32),
        compiler_params=_params(("parallel",), FFN_VMEM_LIMIT),
        name="ffn_dense",
    )(x, sc, sh, gate, w1, w3, w2, lg.reshape(1, D_MODEL), lb.reshape(1, D_MODEL))


def _router_kernel(x_ref, sc_ref, sh_ref, rw_ref, gates_ref, h_ref):
    h = x_ref[...] * (1.0 + sc_ref[0]) + sh_ref[0]
    h_ref[...] = h.astype(BF16)
    logits = jnp.dot(h, rw_ref[...], precision=HIGHEST, preferred_element_type=F32)
    lane = lax.broadcasted_iota(jnp.int32, logits.shape, 1)
    valid = lane < N_EXPERTS
    p = jnp.where(valid, _softmax_rows(jnp.where(valid, logits, -jnp.inf)), -2.0)
    p1 = jnp.max(p, -1, keepdims=True)
    i1 = jnp.min(jnp.where(p == p1, lane, LANES), -1, keepdims=True)
    rest = jnp.where(lane == i1, -1.0, p)
    p2 = jnp.max(rest, -1, keepdims=True)
    i2 = jnp.min(jnp.where(rest == p2, lane, LANES), -1, keepdims=True)
    tot = p1 + p2
    gates_ref[...] = jnp.where(lane == i1, p1 / tot, jnp.where(lane == i2, p2 / tot, 0.0))


def _router(x, sc, sh, router_w, rows_per_mod):
    n = x.shape[0]
    tpb = rows_per_mod // ROW_TILE
    modspec = pl.BlockSpec((1, 1, D_MODEL), lambda i: (i // tpb, 0, 0))
    rw = jnp.pad(router_w, ((0, 0), (0, LANES - N_EXPERTS)))
    return pl.pallas_call(
        _router_kernel,
        grid=(n // ROW_TILE,),
        in_specs=[pl.BlockSpec((ROW_TILE, D_MODEL), lambda i: (i, 0)), modspec, modspec,
                  pl.BlockSpec((D_MODEL, LANES), lambda i: (0, 0))],
        out_specs=[pl.BlockSpec((ROW_TILE, LANES), lambda i: (i, 0)),
                   pl.BlockSpec((ROW_TILE, D_MODEL), lambda i: (i, 0))],
        out_shape=[jax.ShapeDtypeStruct((n, LANES), F32), jax.ShapeDtypeStruct((n, D_MODEL), BF16)],
        compiler_params=_params(("parallel",)),
        name="router",
    )(x, sc, sh, rw)


MOE_ROWS = 1024
MOE_TILE = 128
MOE_MAX_TILES = MOE_ROWS // MOE_TILE
MOE_VMEM_LIMIT = 56 * 1024 * 1024


def _moe_kernel(h_ref, gates_ref, w1_ref, w3_ref, w2_ref, x_ref, gate_ref, lg_ref, lb_ref, o_ref,
                slot_scr, slott_scr, hs_scr, ys_scr):
    e = pl.program_id(1)
    j = pl.program_id(2)
    last_j = pl.num_programs(2) - 1

    @pl.when(jnp.logical_and(e == 0, j == 0))
    def _():
        r = lax.broadcasted_iota(jnp.int32, (MOE_ROWS, MOE_ROWS), 0)
        c = lax.broadcasted_iota(jnp.int32, (MOE_ROWS, MOE_ROWS), 1)
        before = jnp.where(c < r, 1.0, 0.0).astype(BF16)
        mask = gates_ref[...] != 0.0
        rank = jnp.dot(before, jnp.where(mask, 1.0, 0.0).astype(BF16), preferred_element_type=F32)
        slot = jnp.where(mask, rank, -1.0).astype(jnp.int32)
        slot_scr[...] = slot
        slott_scr[...] = slot.T
        o_ref[...] = jnp.zeros_like(o_ref)

    lane = lax.broadcasted_iota(jnp.int32, (MOE_ROWS, LANES), 1)
    slot_col = jnp.max(jnp.where(lane == e, slot_scr[...], -1), -1, keepdims=True)
    n_tiles = (jnp.max(slot_col) + MOE_TILE) // MOE_TILE

    def tile(k, carry):
        @pl.when(j == 0)
        def _():
            slot_row = slott_scr[pl.ds(e, 1), :]
            rr = lax.broadcasted_iota(jnp.int32, (MOE_TILE, MOE_ROWS), 0) + k * MOE_TILE
            pick = jnp.where(rr == slot_row, 1.0, 0.0).astype(BF16)
            hs_scr[k] = jnp.dot(pick, h_ref[...], preferred_element_type=F32).astype(BF16)
            ys_scr[k] = jnp.zeros((MOE_TILE, D_MODEL), F32)

        ys_scr[k] += _swiglu_partial(hs_scr[k], w1_ref[0], w3_ref[0], w2_ref[0])

        @pl.when(j == last_j)
        def _():
            g_col = jnp.sum(jnp.where(lane == e, gates_ref[...], 0.0), -1, keepdims=True)
            cc = lax.broadcasted_iota(jnp.int32, (MOE_ROWS, 2 * MOE_TILE), 1)
            cc = jnp.where(cc >= MOE_TILE, cc - MOE_TILE, cc) + k * MOE_TILE
            put = jnp.where(slot_col == cc, 1.0, 0.0).astype(BF16)
            y = ys_scr[k]
            y_hi = y.astype(BF16)
            y_lo = (y - y_hi.astype(F32)).astype(BF16)
            back = jnp.dot(put, jnp.concatenate([y_hi, y_lo], axis=0), preferred_element_type=F32)
            o_ref[...] += g_col * back

        return carry

    lax.fori_loop(0, n_tiles, tile, 0)

    @pl.when(jnp.logical_and(e == pl.num_programs(1) - 1, j == last_j))
    def _():
        y = ALPHA * x_ref[...] + gate_ref[0] * o_ref[...]
        o_ref[...] = _layernorm_rows(y, lg_ref[...], lb_ref[...])


def _moe(h, gates, w1, w3, w2, x, gate, lg, lb, rows_per_mod):
    n = h.shape[0]
    tpb = rows_per_mod // MOE_ROWS
    vec = pl.BlockSpec((1, D_MODEL), lambda i, e, j: (0, 0))
    return pl.pallas_call(
        _moe_kernel,
        grid=(n // MOE_ROWS, N_EXPERTS, D_FF // FF_TILE),
        in_specs=[pl.BlockSpec((MOE_ROWS, D_MODEL), lambda i, e, j: (i, 0)),
                  pl.BlockSpec((MOE_ROWS, LANES), lambda i, e, j: (i, 0)),
                  pl.BlockSpec((1, D_MODEL, FF_TILE), lambda i, e, j: (e, 0, j)),
                  pl.BlockSpec((1, D_MODEL, FF_TILE), lambda i, e, j: (e, 0, j)),
                  pl.BlockSpec((1, FF_TILE, D_MODEL), lambda i, e, j: (e, j, 0)),
                  pl.BlockSpec((MOE_ROWS, D_MODEL), lambda i, e, j: (i, 0)),
                  pl.BlockSpec((1, 1, D_MODEL), lambda i, e, j: (i // tpb, 0, 0)), vec, vec],
        out_specs=pl.BlockSpec((MOE_ROWS, D_MODEL), lambda i, e, j: (i, 0)),
        out_shape=jax.ShapeDtypeStruct((n, D_MODEL), F32),
        scratch_shapes=[pltpu.VMEM((MOE_ROWS, LANES), jnp.int32), pltpu.VMEM((LANES, MOE_ROWS), jnp.int32),
                        pltpu.VMEM((MOE_MAX_TILES, MOE_TILE, D_MODEL), BF16),
                        pltpu.VMEM((MOE_MAX_TILES, MOE_TILE, D_MODEL), F32)],
        compiler_params=_params(("parallel", "arbitrary", "arbitrary"), MOE_VMEM_LIMIT),
        name="moe",
    )(h, gates, w1, w3, w2, x, gate, lg.reshape(1, D_MODEL), lb.reshape(1, D_MODEL))


def _permute_w_in(w):
    pad = jnp.zeros((D_MODEL, U_COLS - ORIG_END), w.dtype)
    return jnp.concatenate([w[:, :ORIG_GATES], w[:, ORIG_SX:ORIG_DT], w[:, ORIG_SZ:ORIG_SX],
                            w[:, ORIG_GATES:ORIG_SZ], w[:, ORIG_DT:ORIG_END], pad], axis=1).astype(BF16)


def _small_row(vals, offset):
    v = vals.reshape(-1).astype(F32)
    return jnp.zeros((1, LANES), F32).at[0, offset:offset + v.shape[0]].set(v)


def _pack_mlstm_state(c, n, m):
    shape = c.shape[:2] + (D_STATE, HEAD_LANES)
    c_rows = jnp.swapaxes(c, 2, 3).reshape(shape)
    n_rows = jnp.broadcast_to(jnp.swapaxes(n, 2, 3)[..., None], c.shape[:2] + (D_STATE, N_HEADS, D_STATE))
    return (jnp.concatenate([c_rows, n_rows.reshape(shape)], axis=-1),
            jnp.repeat(m, D_STATE, axis=-1)[:, :, None, :])


def _pack_ssd_state(s):
    sel = (jnp.arange(N_GROUPS)[:, None] == jnp.arange(N_HEADS)[None, :] // 2).astype(F32)
    return jnp.einsum('bdhpn,gh->bdgnhp', s, sel).reshape(s.shape[:2] + (LANES, HEAD_LANES))


def _layer(x, mods, P, l, bsz, t, ctx, caches=None):
    sh1, sc1, g1, sh2, sc2, g2 = mods
    rows_per_mod = x.shape[0] // sh1.shape[0]
    lam_init = 0.8 - 0.6 * math.exp(-0.3 * l)
    u = _in_proj(x, sc1, sh1, P['w_in'][l], P['conv_w'][l], P['conv_b'][l], rows_per_mod, t)

    gate_bias = (_small_row(P['mlstm_gate_b'][l, 0], GATE_I) + _small_row(P['mlstm_gate_b'][l, 1], GATE_F))
    dt_bias = _small_row(P['ssm_dt_bias'][l], GATE_DT)
    alog = _small_row(P['ssm_A_log'][l], GATE_DT)
    dskip = jnp.repeat(P['ssm_D'][l].astype(F32), D_STATE).reshape(1, D_REC)

    if ctx is None:
        att, k_new, v_new = _attention_ctx(u, P['attn_lambda'][l], P['attn_norm_w'][l], lam_init, bsz, t, l,
                                           None if caches is None else caches[0:2])
        m_init = s_init = None
    else:
        ck, cv, c_c, c_n, c_m, c_s = ctx
        q, k, v = _rope_prep(u, bsz, t)
        k_all = jnp.concatenate([k, ck.astype(BF16)], axis=3)
        v_all = jnp.concatenate([v, cv.astype(BF16)], axis=2)
        att = _attention_lat(q, k_all, v_all, P['attn_lambda'][l], P['attn_norm_w'][l], lam_init, bsz, t)
        m_init = _pack_mlstm_state(c_c, c_n, c_m)
        s_init = _pack_ssd_state(c_s)
    hf, hb, *mlstm_caches = _mlstm(u, gate_bias, bsz, t, m_init, l, None if caches is None else caches[2:5])
    yf, yb, *ssd_caches = _ssd(u, dt_bias, alog, dskip, bsz, t, s_init, l,
                               None if caches is None else caches[5])

    x = _out_proj(att, hf, hb, yf, yb, u, P['w_out'][l], x, g1, P['mlstm_norm_w'][l], P['ssm_norm_w'][l],
                  P['ln_g'][l, 0], P['ln_b'][l, 0], rows_per_mod)
    if l % 2 == 0:
        x = _ffn(x, sc2, sh2, g2, P['ffn_w1'][l // 2], P['ffn_w3'][l // 2], P['ffn_w2'][l // 2],
                 P['ln_g'][l, 1], P['ln_b'][l, 1], rows_per_mod)
    else:
        gates, h2 = _router(x, sc2, sh2, P['router_w'][l // 2], rows_per_mod)
        x = _moe(h2, gates, P['moe_w1'][l // 2], P['moe_w3'][l // 2], P['moe_w2'][l // 2],
                 x, g2, P['ln_g'][l, 1], P['ln_b'][l, 1], rows_per_mod)
    if ctx is None:
        return x, (k_new, v_new, *mlstm_caches, *ssd_caches)
    return x, None


def kernel(x_prompt, x_sample, c, cache_attn_k, cache_attn_v, state_mlstm_C, state_mlstm_n, state_mlstm_m, state_ssm, c_ctx, w_ada, b_ada, w_in, w_out, attn_lambda, attn_norm_w, mlstm_gate_b, mlstm_norm_w, conv_w, conv_b, ssm_A_log, ssm_dt_bias, ssm_D, ssm_norm_w, ln_g, ln_b, ffn_w1, ffn_w3, ffn_w2, router_w, moe_w1, moe_w3, moe_w2):
    bsz, seq, _ = x_prompt.shape
    dbsz, dseq, _ = x_sample.shape
    P = dict(w_in=[_permute_w_in(w_in[l]) for l in range(DEPTH)], w_out=w_out.astype(BF16),
             attn_lambda=attn_lambda, attn_norm_w=attn_norm_w, mlstm_gate_b=mlstm_gate_b,
             mlstm_norm_w=mlstm_norm_w, conv_w=conv_w, conv_b=conv_b, ssm_A_log=ssm_A_log,
             ssm_dt_bias=ssm_dt_bias, ssm_D=ssm_D, ssm_norm_w=ssm_norm_w, ln_g=ln_g, ln_b=ln_b,
             ffn_w1=ffn_w1.astype(BF16), ffn_w3=ffn_w3.astype(BF16), ffn_w2=ffn_w2.astype(BF16),
             router_w=router_w, moe_w1=moe_w1.astype(BF16), moe_w3=moe_w3.astype(BF16),
             moe_w2=moe_w2.astype(BF16))

    cvec = jnp.zeros((8, D_MODEL), F32).at[0].set(c_ctx).at[1:1 + dbsz].set(c)
    mod = _modulation(cvec, w_ada, b_ada)

    def mods_for(l, lo, hi):
        return [mod[l, lo:hi, i * D_MODEL:(i + 1) * D_MODEL][:, None, :] for i in range(6)]

    y_prompt = x_prompt.reshape(bsz * seq, D_MODEL)
    caches = None
    for l in range(DEPTH):
        y_prompt, caches = _layer(y_prompt, mods_for(l, 0, 1), P, l, bsz, seq, None, caches)
    new_k, new_v, new_c, new_n, m_spread, new_s = caches

    y_sample = x_sample.reshape(dbsz * dseq, D_MODEL)
    for l in range(DEPTH):
        ctx = (cache_attn_k[:, l], cache_attn_v[:, l], state_mlstm_C[:, l], state_mlstm_n[:, l],
               state_mlstm_m[:, l], state_ssm[:, l])
        y_sample, _ = _layer(y_sample, mods_for(l, 1, 1 + dbsz), P, l, dbsz, dseq, ctx)

    return (y_prompt.reshape(bsz, seq, D_MODEL), y_sample.reshape(dbsz, dseq, D_MODEL),
            new_k, new_v, new_c, new_n, m_spread[:, :, :, 0, ::D_STATE], new_s)
```
